```python
import math
import jax
import jax.numpy as jnp
from jax import lax
import numpy as np

D_MODEL = 1024
BATCH = 16
SEQ = 256
DEPTH = 2
DEC_BATCH = 8
DEC_SEQ = 4096
PAST_LEN = 512

GRID_W = 64
EPS = 1e-6
N_EVEN = (DEPTH + 1) // 2
N_ODD = DEPTH // 2
N_MOD = 6
D_LRU = D_MODEL // 2
LRU_BLOCKS = 8
LRU_BW = D_LRU // LRU_BLOCKS
LRU_C = 8.0
CONV_W = 4
CONV_LEFT = 2
D_SSM = D_MODEL // 2
SSM_GROUP = 16
SSM_GROUPS = D_SSM // SSM_GROUP
SSM_STATE = 64
HEAD_DIM = 64
N_HEADS = D_MODEL // HEAD_DIM
N_KV = N_HEADS // 4
GQA = N_HEADS // N_KV
WINDOW = 128
Q_BLOCK = 128
ROPE_BASE = 10000.0
ATTN_SCALE = HEAD_DIM ** -0.5
NEG_INF = -1e30
N_EXPERTS = 256
TOP_K = 8
N_GROUPS = 8
TOPK_GROUPS = 4
D_EXPERT = D_MODEL // 4
D_SHARED = D_MODEL // 4
ROUTE_SCALE = 2.5
MOE_BLOCK = 128

kernel_name = 'hybrid_diffusion_prefix_step'

F32 = jnp.float32


def rmsnorm(x, g):
    xf = x.astype(F32)
    y = xf * lax.rsqrt(jnp.mean(xf * xf, axis=-1, keepdims=True) + EPS)
    return (y * g.astype(F32)).astype(x.dtype)


def modulate(x, shift, scale):
    return x * (1 + scale) + shift


def adaln(c_vec, w_mod, b_mod):
    mod = jax.nn.silu(c_vec) @ w_mod + b_mod
    return jnp.split(mod[:, None, :], N_MOD, axis=-1)


def linear_scan(a, b, h0, reverse):
    if reverse:
        a = jnp.flip(a, 1)
        b = jnp.flip(b, 1)
    a_cum, h = lax.associative_scan(lambda e1, e2: (e1[0] * e2[0], e2[0] * e1[1] + e2[1]), (a, b), axis=1)
    if h0 is not None:
        h = h + a_cum * h0[:, None]
    if reverse:
        h = jnp.flip(h, 1)
    return h


def depthwise_conv_centred(x, w, b):
    s_len = x.shape[1]
    xp = jnp.pad(x, ((0, 0), (CONV_LEFT, CONV_W - 1 - CONV_LEFT), (0, 0)))
    y = b
    for k in range(CONV_W):
        y = y + xp[:, k:k + s_len] * w[k]
    return y


def rg_lru(xc, wa, ba, wx, bx, lam, h0, reverse):
    bsz, s_len = xc.shape[:2]
    xb = xc.reshape(bsz, s_len, LRU_BLOCKS, LRU_BW)
    r = jax.nn.sigmoid(jnp.einsum('bsni,nij->bsnj', xb, wa.astype(F32)).reshape(bsz, s_len, D_LRU) + ba.astype(F32))
    i = jax.nn.sigmoid(jnp.einsum('bsni,nij->bsnj', xb, wx.astype(F32)).reshape(bsz, s_len, D_LRU) + bx.astype(F32))
    log_a = -LRU_C * r * jax.nn.softplus(-lam.astype(F32))
    b = jnp.sqrt(-jnp.expm1(2.0 * log_a)) * (i * xc)
    return linear_scan(jnp.exp(log_a), b, h0, reverse)


def s5_scan(u, a_re, a_im, log_dt, b_re, b_im, h0, reverse):
    lam = lax.complex(a_re.astype(F32), a_im.astype(F32))
    dt = jnp.exp(log_dt.astype(F32))[:, None]
    a_bar = jnp.exp(lam * dt)
    b_bar = ((a_bar - 1.0) / lam)[..., None] * lax.complex(b_re.astype(F32), b_im.astype(F32))
    bu = jnp.einsum('bsgh,gph->bsgp', u.astype(jnp.complex64), b_bar)
    return linear_scan(jnp.broadcast_to(a_bar, bu.shape), bu, h0, reverse)


def s5_read(h, c_re, c_im):
    return jnp.real(jnp.einsum('bsgp,ghp->bsgh', h, lax.complex(c_re.astype(F32), c_im.astype(F32))))


def even_mixer(hn, p, h0s):
    bsz, s_len, _ = hn.shape
    gate_br, rec_br, u = jnp.split(hn @ p['w_in'], [D_LRU, 2 * D_LRU], axis=-1)
    xc = depthwise_conv_centred(rec_br, p['conv_w'], p['conv_b']).astype(F32)
    h_f = rg_lru(xc, p['lru_wa'][0], p['lru_ba'][0], p['lru_wx'][0], p['lru_bx'][0], p['lru_lam'][0], h0s[0], False)
    h_b = rg_lru(xc, p['lru_wa'][1], p['lru_ba'][1], p['lru_wx'][1], p['lru_bx'][1], p['lru_lam'][1], h0s[1], True)
    y_a = (h_f + h_b) * jax.nn.gelu(gate_br.astype(F32))
    u = u.astype(F32).reshape(bsz, s_len, SSM_GROUPS, SSM_GROUP)
    s_f = s5_scan(u, p['a_re'][0], p['a_im'][0], p['log_dt'][0], p['b_re'][0], p['b_im'][0], h0s[2], False)
    s_b = s5_scan(u, p['a_re'][1], p['a_im'][1], p['log_dt'][1], p['b_re'][1], p['b_im'][1], h0s[3], True)
    y_s = (s5_read(s_f, p['c_re'][0], p['c_im'][0]) + s5_read(s_b, p['c_re'][1], p['c_im'][1])
           + p['d'].astype(F32).reshape(SSM_GROUPS, SSM_GROUP) * u)
    g = jax.nn.gelu(y_s.reshape(bsz, s_len, D_SSM))
    y_b = g * jax.nn.sigmoid(g @ p['glu_w'].astype(F32) + p['glu_b'].astype(F32))
    out = jnp.concatenate([y_a, y_b], axis=-1).astype(hn.dtype) @ p['w_out']
    return out, (h_f, h_b, s_f, s_b)


def axial_rope(n_tok):
    rows = n_tok // GRID_W
    row = jnp.repeat(jnp.arange(rows), GRID_W).astype(F32)
    col = jnp.tile(jnp.arange(GRID_W), rows).astype(F32)
    nf = HEAD_DIM // 4
    inv = ROPE_BASE ** (-jnp.arange(nf, dtype=F32) / nf)
    ang = jnp.concatenate([row[:, None] * inv, col[:, None] * inv], axis=-1)
    return jnp.cos(ang), jnp.sin(ang)


def apply_rope(x, cos, sin):
    x = x.astype(F32)
    x1, x2 = jnp.split(x, 2, axis=-1)
    c = cos[None, :, None, :]
    s = sin[None, :, None, :]
    return jnp.concatenate([x1 * c - x2 * s, x1 * s + x2 * c], axis=-1)


def split_qkv(hn, w_qkv):
    return jnp.split(hn @ w_qkv, [N_HEADS * HEAD_DIM, (N_HEADS + N_KV) * HEAD_DIM], axis=-1)


def attn_context(hn, w_qkv, sink, w_out):
    bsz, ln, _ = hn.shape
    q, k, v = split_qkv(hn, w_qkv)
    q = q.reshape(bsz, ln, N_KV, GQA, HEAD_DIM).astype(F32)
    k = k.reshape(bsz, ln, N_KV, HEAD_DIM)
    v = v.reshape(bsz, ln, N_KV, HEAD_DIM)
    s = jnp.einsum('blkgd,bmkd->bkglm', q, k.astype(F32)) * ATTN_SCALE
    sink_b = jnp.broadcast_to(sink.astype(F32).reshape(1, N_KV, GQA, 1, 1), (bsz, N_KV, GQA, ln, 1))
    pr = jax.nn.softmax(jnp.concatenate([s, sink_b], axis=-1), axis=-1)[..., :ln]
    o = jnp.einsum('bkglm,bmkd->blkgd', pr, v.astype(F32)).reshape(bsz, ln, N_HEADS * HEAD_DIM)
    return o.astype(hn.dtype) @ w_out, jnp.swapaxes(k, 1, 2), jnp.swapaxes(v, 1, 2)


def attn_latent(hn, ck, cv, w_qkv, sink, w_out, cos, sin):
    bsz, s_len, _ = hn.shape
    nb = s_len // Q_BLOCK
    q, k, v = split_qkv(hn, w_qkv)
    q = apply_rope(q.reshape(bsz, s_len, N_HEADS, HEAD_DIM), cos, sin).reshape(bsz, s_len, N_KV, GQA, HEAD_DIM)
    k = apply_rope(k.reshape(bsz, s_len, N_KV, HEAD_DIM), cos, sin)
    v = v.reshape(bsz, s_len, N_KV, HEAD_DIM).astype(F32)

    def band(t):
        tp = jnp.pad(t, ((0, 0), (Q_BLOCK, Q_BLOCK), (0, 0), (0, 0))).reshape(bsz, nb + 2, Q_BLOCK, N_KV, HEAD_DIM)
        return jnp.moveaxis(jnp.concatenate([tp[:, :-2], tp[:, 1:-1], tp[:, 2:]], axis=2), 1, 0)

    qb = jnp.moveaxis(q.reshape(bsz, nb, Q_BLOCK, N_KV, GQA, HEAD_DIM), 1, 0)
    kb = band(k)
    vb = band(v)
    qi = jnp.arange(Q_BLOCK)[:, None]
    km = jnp.arange(3 * Q_BLOCK)[None, :]
    kpos = jnp.arange(nb)[:, None, None] * Q_BLOCK - Q_BLOCK + km[None]
    mask = (jnp.abs(km - Q_BLOCK - qi) <= WINDOW)[None] & (kpos >= 0) & (kpos < s_len)
    ck32 = ck.astype(F32)
    cv32 = cv.astype(F32)
    n_ctx = ck.shape[2]
    sink_b = jnp.broadcast_to(sink.astype(F32).reshape(1, N_KV, GQA, 1, 1), (bsz, N_KV, GQA, Q_BLOCK, 1))

    def block(args):
        qj, kj, vj, mj = args
        s_loc = jnp.einsum('bqkgd,bmkd->bkgqm', qj, kj) * ATTN_SCALE
        s_loc = jnp.where(mj[None, None, None], s_loc, NEG_INF)
        s_ctx = jnp.einsum('bqkgd,bkld->bkgql', qj, ck32) * ATTN_SCALE
        pr = jax.nn.softmax(jnp.concatenate([s_loc, s_ctx, sink_b], axis=-1), axis=-1)
        return (jnp.einsum('bkgqm,bmkd->bqkgd', pr[..., :3 * Q_BLOCK], vj)
                + jnp.einsum('bkgql,bkld->bqkgd', pr[..., 3 * Q_BLOCK:3 * Q_BLOCK + n_ctx], cv32))

    o = lax.map(block, (qb, kb, vb, mask))
    o = jnp.moveaxis(o, 0, 1).reshape(bsz, s_len, N_HEADS * HEAD_DIM)
    return o.astype(hn.dtype) @ w_out


def moe(h, router_w, router_b, w_gate, w_up, w_down, s_gate, s_up, s_down):
    x2 = h.reshape(-1, h.shape[-1])
    n_tok = x2.shape[0]
    scores = jax.nn.sigmoid(x2.astype(F32) @ router_w.astype(F32))
    choice = scores + router_b.astype(F32)
    grp = choice.reshape(n_tok, N_GROUPS, N_EXPERTS // N_GROUPS)
    grp_score = jnp.sum(lax.top_k(grp, 2)[0], axis=-1)
    _, top_g = lax.top_k(grp_score, TOPK_GROUPS)
    gmask = jnp.any(top_g[..., None] == jnp.arange(N_GROUPS), axis=1)
    emask = jnp.repeat(gmask, N_EXPERTS // N_GROUPS, axis=1)
    _, top_e = lax.top_k(jnp.where(emask, choice, -jnp.inf), TOP_K)
    wts = jnp.take_along_axis(scores, top_e, axis=1)
    wts = wts / jnp.sum(wts, axis=-1, keepdims=True) * ROUTE_SCALE
    n_as = n_tok * TOP_K
    flat_e = top_e.reshape(n_as)
    flat_t = jnp.repeat(jnp.arange(n_tok, dtype=jnp.int32), TOP_K)
    flat_w = wts.reshape(n_as)
    order = jnp.argsort(flat_e)
    se, st, sw = flat_e[order], flat_t[order], flat_w[order]
    counts = jnp.bincount(flat_e, length=N_EXPERTS)
    padded = (counts + MOE_BLOCK - 1) // MOE_BLOCK * MOE_BLOCK
    pad_end = jnp.cumsum(padded)
    pad_start = pad_end - padded
    cnt_start = jnp.cumsum(counts) - counts
    dest = pad_start[se] + jnp.arange(n_as) - cnt_start[se]
    n_blk = -(-(n_as + N_EXPERTS * (MOE_BLOCK - 1)) // MOE_BLOCK)
    n_rows = n_blk * MOE_BLOCK
    row_tok = jnp.full((n_rows,), n_tok, jnp.int32).at[dest].set(st)
    row_w = jnp.zeros((n_rows,), F32).at[dest].set(sw)
    blk_e = jnp.minimum(jnp.searchsorted(pad_end, jnp.arange(n_blk) * MOE_BLOCK, side='right'), N_EXPERTS - 1)
    x_pad = jnp.concatenate([x2, jnp.zeros((1, x2.shape[1]), x2.dtype)], axis=0)

    def expert_block(args):
        tok, e = args
        xs = x_pad[tok]
        return (jax.nn.silu(xs @ w_gate[e]) * (xs @ w_up[e])) @ w_down[e]

    yb = lax.map(expert_block, (row_tok.reshape(n_blk, MOE_BLOCK), blk_e)).reshape(n_rows, -1)
    routed = jax.ops.segment_sum(yb.astype(F32) * row_w[:, None], row_tok, num_segments=n_tok + 1)[:n_tok]
    shared = (jax.nn.silu(x2 @ s_gate) * (x2 @ s_up)) @ s_down
    return (routed.astype(x2.dtype) + shared).reshape(h.shape)


def setup_inputs(seed: int = 0) -> dict:
    key = jax.random.key(seed)
    ks = iter(jax.random.split(key, 64))

    def nrm(shape, scale):
        return scale * jax.random.normal(next(ks), shape, F32)

    def unif(shape, lo, hi):
        return jax.random.uniform(next(ks), shape, F32, lo, hi)

    a0 = unif((N_EVEN, 2, D_LRU), 0.9, 0.999) ** (1.0 / LRU_C)
    return {
        'x_prompt': nrm((BATCH, SEQ, D_MODEL), 1.0),
        'x_sample': nrm((DEC_BATCH, DEC_SEQ, D_MODEL), 1.0),
        'state_lru': nrm((DEC_BATCH, N_EVEN, 2, D_LRU), 0.5),
        'state_ssm_re': nrm((DEC_BATCH, N_EVEN, 2, SSM_GROUPS, SSM_STATE), 0.5),
        'state_ssm_im': nrm((DEC_BATCH, N_EVEN, 2, SSM_GROUPS, SSM_STATE), 0.5),
        'cache_k': nrm((DEC_BATCH, N_ODD, N_KV, PAST_LEN, HEAD_DIM), 1.0),
        'cache_v': nrm((DEC_BATCH, N_ODD, N_KV, PAST_LEN, HEAD_DIM), 1.0),
        'c': nrm((DEC_BATCH, D_MODEL), 1.0),
        'c_ctx': nrm((D_MODEL,), 1.0),
        'g_mix': 1.0 + nrm((DEPTH, D_MODEL), 0.02),
        'g_ffn': 1.0 + nrm((DEPTH, D_MODEL), 0.02),
        'w_mod': nrm((DEPTH, D_MODEL, N_MOD * D_MODEL), 0.5 * D_MODEL ** -0.5),
        'b_mod': nrm((DEPTH, N_MOD * D_MODEL), 0.02),
        'ev_w_in': nrm((N_EVEN, D_MODEL, 2 * D_LRU + D_SSM), D_MODEL ** -0.5),
        'lru_conv_w': nrm((N_EVEN, CONV_W, D_LRU), CONV_W ** -0.5),
        'lru_conv_b': nrm((N_EVEN, D_LRU), 0.02),
        'lru_wa': nrm((N_EVEN, 2, LRU_BLOCKS, LRU_BW, LRU_BW), LRU_BW ** -0.5),
        'lru_ba': nrm((N_EVEN, 2, D_LRU), 0.02),
        'lru_wx': nrm((N_EVEN, 2, LRU_BLOCKS, LRU_BW, LRU_BW), LRU_BW ** -0.5),
        'lru_bx': nrm((N_EVEN, 2, D_LRU), 0.02),
        'lru_lam': jnp.log(a0) - jnp.log1p(-a0),
        'ssm_a_re': -0.5 + nrm((N_EVEN, 2, SSM_GROUPS, SSM_STATE), 0.01),
        'ssm_a_im': math.pi * jnp.arange(SSM_STATE, dtype=F32) + nrm((N_EVEN, 2, SSM_GROUPS, SSM_STATE), 0.01),
        'ssm_log_dt': unif((N_EVEN, 2, SSM_GROUPS), math.log(1e-3), math.log(1e-1)),
        'ssm_b_re': nrm((N_EVEN, 2, SSM_GROUPS, SSM_STATE, SSM_GROUP), (2 * SSM_GROUP) ** -0.5),
        'ssm_b_im': nrm((N_EVEN, 2, SSM_GROUPS, SSM_STATE, SSM_GROUP), (2 * SSM_GROUP) ** -0.5),
        'ssm_c_re': nrm((N_EVEN, 2, SSM_GROUPS, SSM_GROUP, SSM_STATE), (2 * SSM_STATE) ** -0.5),
        'ssm_c_im': nrm((N_EVEN, 2, SSM_GROUPS, SSM_GROUP, SSM_STATE), (2 * SSM_STATE) ** -0.5),
        'ssm_d': nrm((N_EVEN, D_SSM), 1.0),
        'ssm_glu_w': nrm((N_EVEN, D_SSM, D_SSM), D_SSM ** -0.5),
        'ssm_glu_b': nrm((N_EVEN, D_SSM), 0.02),
        'ev_w_out': nrm((N_EVEN, D_LRU + D_SSM, D_MODEL), (D_LRU + D_SSM) ** -0.5),
        'at_w_qkv': nrm((N_ODD, D_MODEL, (N_HEADS + 2 * N_KV) * HEAD_DIM), D_MODEL ** -0.5),
        'at_sink': nrm((N_ODD, N_HEADS), 0.5),
        'at_w_out': nrm((N_ODD, N_HEADS * HEAD_DIM, D_MODEL), (N_HEADS * HEAD_DIM) ** -0.5),
        'router_w': nrm((DEPTH, D_MODEL, N_EXPERTS), D_MODEL ** -0.5),
        'router_b': nrm((DEPTH, N_EXPERTS), 0.01),
        'exp_w_gate': nrm((DEPTH, N_EXPERTS, D_MODEL, D_EXPERT), D_MODEL ** -0.5),
        'exp_w_up': nrm((DEPTH, N_EXPERTS, D_MODEL, D_EXPERT), D_MODEL ** -0.5),
        'exp_w_down': nrm((DEPTH, N_EXPERTS, D_EXPERT, D_MODEL), D_EXPERT ** -0.5),
        'sh_w_gate': nrm((DEPTH, D_MODEL, D_SHARED), D_MODEL ** -0.5),
        'sh_w_up': nrm((DEPTH, D_MODEL, D_SHARED), D_MODEL ** -0.5),
        'sh_w_down': nrm((DEPTH, D_SHARED, D_MODEL), D_SHARED ** -0.5),
        'g_final': 1.0 + nrm((D_MODEL,), 0.02),
    }


def reference(x_prompt, x_sample, state_lru, state_ssm_re, state_ssm_im, cache_k, cache_v, c, c_ctx,
              g_mix, g_ffn, w_mod, b_mod,
              ev_w_in, lru_conv_w, lru_conv_b, lru_wa, lru_ba, lru_wx, lru_bx, lru_lam,
              ssm_a_re, ssm_a_im, ssm_log_dt, ssm_b_re, ssm_b_im, ssm_c_re, ssm_c_im, ssm_d,
              ssm_glu_w, ssm_glu_b, ev_w_out,
              at_w_qkv, at_sink, at_w_out,
              router_w, router_b, exp_w_gate, exp_w_up, exp_w_down, sh_w_gate, sh_w_up, sh_w_down,
              g_final):
    rope_cos, rope_sin = axial_rope(x_sample.shape[1])
    xc = x_prompt
    xl = x_sample
    new_lru, new_ssm_re, new_ssm_im, new_k, new_v = [], [], [], [], []
    for l in range(DEPTH):
        i = l // 2
        mc = adaln(c_ctx[None, :], w_mod[l], b_mod[l])
        ml = adaln(c, w_mod[l], b_mod[l])
        hc = modulate(rmsnorm(xc, g_mix[l]), mc[0], mc[1])
        hl = modulate(rmsnorm(xl, g_mix[l]), ml[0], ml[1])
        if l % 2 == 0:
            ep = dict(w_in=ev_w_in[i], conv_w=lru_conv_w[i], conv_b=lru_conv_b[i],
                      lru_wa=lru_wa[i], lru_ba=lru_ba[i], lru_wx=lru_wx[i], lru_bx=lru_bx[i], lru_lam=lru_lam[i],
                      a_re=ssm_a_re[i], a_im=ssm_a_im[i], log_dt=ssm_log_dt[i],
                      b_re=ssm_b_re[i], b_im=ssm_b_im[i], c_re=ssm_c_re[i], c_im=ssm_c_im[i], d=ssm_d[i],
                      glu_w=ssm_glu_w[i], glu_b=ssm_glu_b[i], w_out=ev_w_out[i])
            oc, (hf, hb, sf, sb) = even_mixer(hc, ep, (None, None, None, None))
            new_lru.append(jnp.stack([hf[:, -1], hb[:, 0]], axis=1))
            ssm_end = jnp.stack([sf[:, -1], sb[:, 0]], axis=1)
            new_ssm_re.append(jnp.real(ssm_end))
            new_ssm_im.append(jnp.imag(ssm_end))
            ssm0 = lax.complex(state_ssm_re[:, i].astype(F32), state_ssm_im[:, i].astype(F32))
            h0s = (state_lru[:, i, 0].astype(F32), state_lru[:, i, 1].astype(F32), ssm0[:, 0], ssm0[:, 1])
            ol, _ = even_mixer(hl, ep, h0s)
        else:
            oc, k_ctx, v_ctx = attn_context(hc, at_w_qkv[i], at_sink[i], at_w_out[i])
            new_k.append(k_ctx)
            new_v.append(v_ctx)
            ol = attn_latent(hl, cache_k[:, i], cache_v[:, i], at_w_qkv[i], at_sink[i], at_w_out[i],
                             rope_cos, rope_sin)
        xc = xc + mc[2] * oc
        xl = xl + ml[2] * ol
        mp = (router_w[l], router_b[l], exp_w_gate[l], exp_w_up[l], exp_w_down[l],
              sh_w_gate[l], sh_w_up[l], sh_w_down[l])
        xc = xc + mc[5] * moe(modulate(rmsnorm(xc, g_ffn[l]), mc[3], mc[4]), *mp)
        xl = xl + ml[5] * moe(modulate(rmsnorm(xl, g_ffn[l]), ml[3], ml[4]), *mp)
    y_prompt = rmsnorm(xc, g_final)
    y_sample = rmsnorm(xl, g_final)
    state_lru_new = jnp.stack(new_lru, axis=1)
    state_ssm_re_new = jnp.stack(new_ssm_re, axis=1)
    state_ssm_im_new = jnp.stack(new_ssm_im, axis=1)
    cache_k_new = jnp.stack(new_k, axis=1)
    cache_v_new = jnp.stack(new_v, axis=1)
    return (y_prompt, y_sample, state_lru_new, state_ssm_re_new, state_ssm_im_new, cache_k_new, cache_v_new)
```

```python
import functools
import math
from typing import NamedTuple

import jax
import jax.numpy as jnp
from jax import lax
from jax.experimental import pallas as pl
from jax.experimental.pallas import tpu as pltpu

F32 = jnp.float32
BF16 = jnp.bfloat16
I32 = jnp.int32
HIGHEST = lax.Precision.HIGHEST

D_MODEL = 1024
EPS = 1e-6
N_MOD = 6
GRID_W = 64
D_LRU = 512
LRU_BLOCKS = 8
LRU_C = 8.0
CONV_W = 4
CONV_LEFT = 2
D_SSM = 512
SSM_GROUP = 16
SSM_GROUPS = 32
SSM_STATE = 64
S5_CHUNK = 16
S5_LANES = S5_CHUNK * SSM_GROUP
HEAD_DIM = 64
N_HEADS = 16
N_KV = 4
GQA = 4
WINDOW = 128
Q_BLOCK = 128
ROPE_BASE = 10000.0
ATTN_SCALE = HEAD_DIM ** -0.5
NEG_INF = -1e30
N_EXPERTS = 256
TOP_K = 8
N_GROUPS = 8
TOPK_GROUPS = 4
GROUP_SIZE = N_EXPERTS // N_GROUPS
D_EXPERT = 256
ROUTE_SCALE = 2.5
MOE_BLOCK = 128

SEQ_TILE = 256
ROW_TILE = 512
ROUTER_TILE = 512
COMBINE_TILE = 128
SUBLANES = 8
VMEM_LIMIT = 56 * 1024 * 1024


class Layout(NamedTuple):
    n_ctx: int
    s_ctx: int
    n_lat: int
    s_lat: int

    @property
    def t_ctx(self):
        return self.n_ctx * self.s_ctx

    @property
    def t_lat(self):
        return self.n_lat * self.s_lat

    @property
    def t(self):
        return self.t_ctx + self.t_lat

    @property
    def n_seq(self):
        return self.n_ctx + self.n_lat


def _cparams(sem):
    return pltpu.CompilerParams(dimension_semantics=sem, vmem_limit_bytes=VMEM_LIMIT)


def _mod_index(lay, tile_rows):
    n_ctx_tiles = lay.t_ctx // tile_rows
    per_lat = lay.s_lat // tile_rows

    def f(i):
        return jnp.where(i < n_ctx_tiles, 0, 1 + (i - n_ctx_tiles) // per_lat)
    return f


def _adaln_kernel(c_ref, w_ref, b_ref, o_ref):
    c = c_ref[...]
    s = c * jax.nn.sigmoid(c)
    o_ref[...] = jnp.dot(s, w_ref[...], precision=HIGHEST, preferred_element_type=F32) + b_ref[...]


def adaln_table(c_rows, w_mod, b_mod):
    n = c_rows.shape[0]
    tn = 1536
    out = pl.pallas_call(
        _adaln_kernel,
        grid=(N_MOD * D_MODEL // tn,),
        in_specs=[pl.BlockSpec((n, D_MODEL), lambda j: (0, 0)),
                  pl.BlockSpec((D_MODEL, tn), lambda j: (0, j)),
                  pl.BlockSpec((1, tn), lambda j: (0, j))],
        out_specs=pl.BlockSpec((n, tn), lambda j: (0, j)),
        out_shape=jax.ShapeDtypeStruct((n, N_MOD * D_MODEL), F32),
        compiler_params=_cparams(("arbitrary",)),
        name="adaln",
    )(c_rows, w_mod, b_mod.reshape(1, -1))
    return out.reshape(n, N_MOD, D_MODEL)


def _modnorm(x, g, mod_ref, slot):
    ms = jnp.mean(x * x, axis=-1, keepdims=True)
    y = x * lax.rsqrt(ms + EPS) * g
    shift = mod_ref[0, slot:slot + 1, :]
    scale = mod_ref[0, slot + 1:slot + 2, :]
    return y * (1.0 + scale) + shift


def _modnorm_mm_kernel(x_ref, g_ref, mod_ref, w_ref, o_ref, *, slot):
    h = _modnorm(x_ref[...], g_ref[...], mod_ref, slot)
    o_ref[...] = jnp.dot(h.astype(BF16), w_ref[...], preferred_element_type=F32)


def modnorm_matmul(lay, x, g, mods, slot, w_bf16):
    t = lay.t
    n = w_bf16.shape[1]
    mi = _mod_index(lay, ROW_TILE)
    return pl.pallas_call(
        functools.partial(_modnorm_mm_kernel, slot=slot),
        grid=(t // ROW_TILE,),
        in_specs=[pl.BlockSpec((ROW_TILE, D_MODEL), lambda i: (i, 0)),
                  pl.BlockSpec((1, D_MODEL), lambda i: (0, 0)),
                  pl.BlockSpec((1, N_MOD, D_MODEL), lambda i: (mi(i), 0, 0)),
                  pl.BlockSpec((D_MODEL, n), lambda i: (0, 0))],
        out_specs=pl.BlockSpec((ROW_TILE, n), lambda i: (i, 0)),
        out_shape=jax.ShapeDtypeStruct((t, n), F32),
        compiler_params=_cparams(("arbitrary",)),
        name="modnorm_matmul",
    )(x, g.reshape(1, -1), mods, w_bf16)


def _seq_tile_maps(lay, reverse):
    assert lay.s_ctx == SEQ_TILE and lay.s_lat % SEQ_TILE == 0
    n_tiles = lay.t // SEQ_TILE
    per_lat = lay.s_lat // SEQ_TILE

    def tile(i):
        return (n_tiles - 1 - i) if reverse else i

    def seq(i):
        ti = tile(i)
        return jnp.where(ti < lay.n_ctx, ti, lay.n_ctx + (ti - lay.n_ctx) // per_lat)

    def first(ti):
        return jnp.logical_or(ti < lay.n_ctx, (ti - lay.n_ctx) % per_lat == 0)

    def last(ti):
        return jnp.logical_or(ti < lay.n_ctx, (ti - lay.n_ctx) % per_lat == per_lat - 1)

    return n_tiles, tile, seq, first, last


def _softplus(x):
    return jnp.maximum(x, 0.0) + jnp.log(1.0 + jnp.exp(-jnp.abs(x)))


def _lru_kernel(rec_ref, prev_ref, next_ref, cw_ref, cb_ref, wg_ref, bg_ref, lam_ref, h0_ref,
                *rest, reverse, n_ctx, per_lat, n_tiles):
    if reverse:
        gate_ref, hf_ref, y_ref, st_ref, a_s, b_s, h_s, carry = rest
    else:
        y_ref, st_ref, a_s, b_s, h_s, carry = rest
    i = pl.program_id(0)
    ti = (n_tiles - 1 - i) if reverse else i
    is_first = jnp.logical_or(ti < n_ctx, (ti - n_ctx) % per_lat == 0)
    is_last = jnp.logical_or(ti < n_ctx, (ti - n_ctx) % per_lat == per_lat - 1)
    ts = SEQ_TILE

    rec = rec_ref[...]
    prev = jnp.where(is_first, 0.0, prev_ref[...])
    nxt = jnp.where(is_last, 0.0, next_ref[...])
    ext = jnp.concatenate([prev, rec, nxt], axis=0)
    n_ext = ts + 2 * SUBLANES
    cw = cw_ref[...]
    xc = cb_ref[...] + cw[2:3, :] * rec
    xc = xc + cw[0:1, :] * pltpu.roll(ext, 2, 0)[SUBLANES:SUBLANES + ts]
    xc = xc + cw[1:2, :] * pltpu.roll(ext, 1, 0)[SUBLANES:SUBLANES + ts]
    xc = xc + cw[3:4, :] * pltpu.roll(ext, n_ext - 1, 0)[SUBLANES:SUBLANES + ts]

    gates = jax.nn.sigmoid(jnp.dot(xc.astype(BF16), wg_ref[...], preferred_element_type=F32)
                           + bg_ref[...])
    r = gates[:, :D_LRU]
    ig = gates[:, D_LRU:]
    log_a = (-LRU_C) * r * _softplus(-lam_ref[...])
    a = jnp.exp(log_a)
    b = jnp.sqrt(1.0 - jnp.exp(2.0 * log_a)) * (ig * xc)

    row8 = lax.broadcasted_iota(I32, (ts, D_LRU), 0) % SUBLANES
    for sh in (1, 2, 4):
        if reverse:
            keep = row8 < SUBLANES - sh
            a_sh = pltpu.roll(a, ts - sh, 0)
            b_sh = pltpu.roll(b, ts - sh, 0)
        else:
            keep = row8 >= sh
            a_sh = pltpu.roll(a, sh, 0)
            b_sh = pltpu.roll(b, sh, 0)
        b = b + a * jnp.where(keep, b_sh, 0.0)
        a = a * jnp.where(keep, a_sh, 1.0)
    a_s[...] = a
    b_s[...] = b

    @pl.when(is_first if not reverse else is_last)
    def _():
        carry[...] = h0_ref[0]

    n_grp = ts // SUBLANES

    def body(k, c):
        gi = (n_grp - 1 - k) if reverse else k
        sl = pl.ds(pl.multiple_of(gi * SUBLANES, SUBLANES), SUBLANES)
        h = b_s[sl, :] + a_s[sl, :] * c
        h_s[sl, :] = h
        return h[0:1, :] if reverse else h[SUBLANES - 1:SUBLANES, :]

    c_fin = lax.fori_loop(0, n_grp, body, carry[...], unroll=4)
    carry[...] = c_fin
    st_ref[0] = c_fin
    if reverse:
        y_ref[...] = (hf_ref[...] + h_s[...]) * jax.nn.gelu(gate_ref[...])
    else:
        y_ref[...] = h_s[...]


def lru_pass(lay, proj, conv_w, conv_b, wg_bf16, bg, lam, h0, reverse, hf=None):
    n_tiles, tile, seq, _, _ = _seq_tile_maps(lay, reverse)
    per_lat = lay.s_lat // SEQ_TILE
    blk8 = SEQ_TILE // SUBLANES
    last8 = lay.t // SUBLANES - 1
    c = D_LRU
    in_specs = [
        pl.BlockSpec((SEQ_TILE, c), lambda i: (tile(i), 1)),
        pl.BlockSpec((SUBLANES, c), lambda i: (jnp.maximum(tile(i) * blk8 - 1, 0), 1)),
        pl.BlockSpec((SUBLANES, c), lambda i: (jnp.minimum(tile(i) * blk8 + blk8, last8), 1)),
        pl.BlockSpec((CONV_W, c), lambda i: (0, 0)),
        pl.BlockSpec((1, c), lambda i: (0, 0)),
        pl.BlockSpec((c, 2 * c), lambda i: (0, 0)),
        pl.BlockSpec((1, 2 * c), lambda i: (0, 0)),
        pl.BlockSpec((1, c), lambda i: (0, 0)),
        pl.BlockSpec((1, 1, c), lambda i: (seq(i), 0, 0)),
    ]
    args = [proj, proj, proj, conv_w, conv_b.reshape(1, -1), wg_bf16, bg.reshape(1, -1),
            lam.reshape(1, -1), h0.reshape(lay.n_seq, 1, c)]
    if reverse:
        in_specs += [pl.BlockSpec((SEQ_TILE, c), lambda i: (tile(i), 0)),
                     pl.BlockSpec((SEQ_TILE, c), lambda i: (tile(i), 0))]
        args += [proj, hf]
    y, st = pl.pallas_call(
        functools.partial(_lru_kernel, reverse=reverse, n_ctx=lay.n_ctx, per_lat=per_lat,
                          n_tiles=n_tiles),
        grid=(n_tiles,),
        in_specs=in_specs,
        out_specs=[pl.BlockSpec((SEQ_TILE, c), lambda i: (tile(i), 0)),
                   pl.BlockSpec((1, 1, c), lambda i: (seq(i), 0, 0))],
        out_shape=[jax.ShapeDtypeStruct((lay.t, c), F32),
                   jax.ShapeDtypeStruct((lay.n_seq, 1, c), F32)],
        scratch_shapes=[pltpu.VMEM((SEQ_TILE, c), F32), pltpu.VMEM((SEQ_TILE, c), F32),
                        pltpu.VMEM((SEQ_TILE, c), F32), pltpu.VMEM((1, c), F32)],
        compiler_params=_cparams(("arbitrary",)),
        name="lru_bwd" if reverse else "lru_fwd",
    )(*args)
    return y, st.reshape(lay.n_seq, c)


def _s5_matrices(a_re, a_im, log_dt, b_re, b_im, c_re, c_im):
    lam = lax.complex(a_re.astype(F32), a_im.astype(F32))
    dt = jnp.exp(log_dt.astype(F32))[..., None]
    a_bar = jnp.exp(lam * dt)
    b_bar = ((a_bar - 1.0) / lam)[..., None] * lax.complex(b_re.astype(F32), b_im.astype(F32))
    cc = lax.complex(c_re.astype(F32), c_im.astype(F32))
    el = S5_CHUNK
    ks = jnp.arange(el + 1, dtype=F32)
    pw = jnp.exp(ks[:, None, None, None] * (lam * dt)[None])
    idx = jnp.arange(el)
    m_in, m_toep, m_out = [], [], []
    for d in range(2):
        p_d, b_d, c_d = pw[:, d], b_bar[d], cc[d]
        k_in = (el - 1 - idx) if d == 0 else idx
        w_in = p_d[k_in][..., None] * b_d[None]
        w_in = jnp.transpose(w_in, (1, 0, 3, 2))
        w_in = w_in.reshape(SSM_GROUPS, S5_LANES, SSM_STATE)
        m_in.append(jnp.concatenate([jnp.real(w_in), jnp.imag(w_in)], axis=-1))
        kern = jnp.real(jnp.einsum('ghp,kgp,gpc->kgch', c_d, p_d[:el], b_d))
        diff = (idx[None, :] - idx[:, None]) if d == 0 else (idx[:, None] - idx[None, :])
        blocks = jnp.where((diff >= 0)[:, :, None, None, None],
                           kern[jnp.clip(diff, 0, el - 1)], 0.0)
        m_toep.append(jnp.transpose(blocks, (2, 0, 3, 1, 4)).reshape(SSM_GROUPS, S5_LANES, S5_LANES))
        k_out = (idx + 1) if d == 0 else (el - idx)
        w_out = c_d[None] * p_d[k_out][:, :, None, :]
        w_out = jnp.transpose(w_out, (1, 3, 0, 2)).reshape(SSM_GROUPS, SSM_STATE, S5_LANES)
        m_out.append(jnp.concatenate([jnp.real(w_out), -jnp.imag(w_out)], axis=1))
    n_steps = 8
    steps = (el * 2.0 ** jnp.arange(n_steps, dtype=F32))
    mul = jnp.exp(steps[:, None, None, None] * (lam * dt)[None])
    mul = jnp.transpose(mul, (2, 1, 0, 3))
    coef_a = jnp.concatenate([jnp.real(mul), jnp.real(mul)], axis=-1)
    coef_b = jnp.concatenate([-jnp.imag(mul), jnp.imag(mul)], axis=-1)
    stack = lambda xs: jnp.stack(xs, axis=1)
    return (stack(m_in).astype(BF16), stack(m_toep).astype(BF16), stack(m_out).astype(BF16),
            coef_a, coef_b)


def _s5_scan(v, ca, cb, seg, reverse):
    n = v.shape[0]
    row = lax.broadcasted_iota(I32, (n, 2 * SSM_STATE), 0) % seg
    k, sh = 0, 1
    while sh < seg:
        if reverse:
            s = jnp.where(row < seg - sh, pltpu.roll(v, n - sh, 0), 0.0)
        else:
            s = jnp.where(row >= sh, pltpu.roll(v, sh, 0), 0.0)
        v = v + ca[k:k + 1, :] * s + cb[k:k + 1, :] * pltpu.roll(s, SSM_STATE, 1)
        k += 1
        sh *= 2
    return v


def _s5_shift(h, seg, reverse):
    n = h.shape[0]
    row = lax.broadcasted_iota(I32, (n, 2 * SSM_STATE), 0) % seg
    if reverse:
        return jnp.where(row < seg - 1, pltpu.roll(h, n - 1, 0), 0.0)
    return jnp.where(row >= 1, pltpu.roll(h, 1, 0), 0.0)


def _s5_kernel(u_ref, min_ref, mtoep_ref, mout_ref, ca_ref, cb_ref, h0_ref, y_ref, hc_ref,
               v_s, hp_s, *, rc, seg_c, n_lat, seg_l):
    u = u_ref[0].astype(BF16)
    u_c, u_l = u[:rc], u[rc:]
    y_c = jnp.zeros((rc, S5_LANES), F32)
    y_l = jnp.zeros((n_lat * seg_l, S5_LANES), F32)
    for d in range(2):
        reverse = d == 1
        ca = ca_ref[0, d]
        cb = cb_ref[0, d]
        m_in = min_ref[0, d]
        m_toep = mtoep_ref[0, d]
        m_out = mout_ref[0, d]
        h_c = _s5_scan(jnp.dot(u_c, m_in, preferred_element_type=F32), ca, cb, seg_c, reverse)
        hc_ref[0, d] = h_c
        hp_c = _s5_shift(h_c, seg_c, reverse)
        y_c = y_c + jnp.dot(u_c, m_toep, preferred_element_type=F32)
        y_c = y_c + jnp.dot(hp_c.astype(BF16), m_out, preferred_element_type=F32)
        v_s[...] = jnp.dot(u_l, m_in, preferred_element_type=F32)
        for s in range(n_lat):
            h0 = h0_ref[0, d, s:s + 1, :]
            r0 = s * seg_l + (seg_l - 1 if reverse else 0)
            v_s[r0:r0 + 1, :] = (v_s[r0:r0 + 1, :] + ca[0:1, :] * h0
                                 + cb[0:1, :] * pltpu.roll(h0, SSM_STATE, 1))
        h_l = _s5_scan(v_s[...], ca, cb, seg_l, reverse)
        hp_s[...] = _s5_shift(h_l, seg_l, reverse)
        for s in range(n_lat):
            r0 = s * seg_l + (seg_l - 1 if reverse else 0)
            hp_s[r0:r0 + 1, :] = h0_ref[0, d, s:s + 1, :]
        y_l = y_l + jnp.dot(u_l, m_toep, preferred_element_type=F32)
        y_l = y_l + jnp.dot(hp_s[...].astype(BF16), m_out, preferred_element_type=F32)
    y_ref[0, :rc, :] = y_c
    y_ref[0, rc:, :] = y_l


def s5_mixer(lay, u_g, mats, h0):
    m_in, m_toep, m_out, coef_a, coef_b = mats
    rows = lay.t // S5_CHUNK
    rc = lay.t_ctx // S5_CHUNK
    rl = rows - rc
    st2 = 2 * SSM_STATE
    g4 = lambda g: (g, 0, 0, 0)
    return pl.pallas_call(
        functools.partial(_s5_kernel, rc=rc, seg_c=lay.s_ctx // S5_CHUNK, n_lat=lay.n_lat,
                          seg_l=lay.s_lat // S5_CHUNK),
        grid=(SSM_GROUPS,),
        in_specs=[pl.BlockSpec((1, rows, S5_LANES), lambda g: (g, 0, 0)),
                  pl.BlockSpec((1, 2, S5_LANES, st2), g4),
                  pl.BlockSpec((1, 2, S5_LANES, S5_LANES), g4),
                  pl.BlockSpec((1, 2, st2, S5_LANES), g4),
                  pl.BlockSpec((1, 2, 8, st2), g4),
                  pl.BlockSpec((1, 2, 8, st2), g4),
                  pl.BlockSpec((1, 2, lay.n_lat, st2), g4)],
        out_specs=[pl.BlockSpec((1, rows, S5_LANES), lambda g: (g, 0, 0)),
                   pl.BlockSpec((1, 2, rc, st2), g4)],
        out_shape=[jax.ShapeDtypeStruct((SSM_GROUPS, rows, S5_LANES), F32),
                   jax.ShapeDtypeStruct((SSM_GROUPS, 2, rc, st2), F32)],
        scratch_shapes=[pltpu.VMEM((rl, st2), F32), pltpu.VMEM((rl, st2), F32)],
        compiler_params=_cparams(("arbitrary",)),
        name="s5_mixer",
    )(u_g, m_in, m_toep, m_out, coef_a, coef_b, h0)


def _even_out_kernel(x_ref, ya_ref, yt_ref, u_ref, d_ref, gw_ref, gb_ref, w_ref, mod_ref, o_ref):
    ys = yt_ref[...] + d_ref[...] * u_ref[...]
    g = jax.nn.gelu(ys)
    yb = g * jax.nn.sigmoid(jnp.dot(g.astype(BF16), gw_ref[...], preferred_element_type=F32)
                            + gb_ref[...])
    out = jnp.dot(ya_ref[...].astype(BF16), w_ref[:D_LRU, :], preferred_element_type=F32)
    out = out + jnp.dot(yb.astype(BF16), w_ref[D_LRU:, :], preferred_element_type=F32)
    o_ref[...] = x_ref[...] + mod_ref[0, 2:3, :] * out


def even_out(lay, x, y_a, y_t, proj, ssm_d, glu_w_bf16, glu_b, w_out_bf16, mods):
    mi = _mod_index(lay, ROW_TILE)
    c = D_SSM
    row = lambda i: (i, 0)
    const = lambda i: (0, 0)
    return pl.pallas_call(
        _even_out_kernel,
        grid=(lay.t // ROW_TILE,),
        in_specs=[pl.BlockSpec((ROW_TILE, D_MODEL), row),
                  pl.BlockSpec((ROW_TILE, c), row),
                  pl.BlockSpec((ROW_TILE, c), row),
                  pl.BlockSpec((ROW_TILE, c), lambda i: (i, 2)),
                  pl.BlockSpec((1, c), const),
                  pl.BlockSpec((c, c), const),
                  pl.BlockSpec((1, c), const),
                  pl.BlockSpec((D_MODEL, D_MODEL), const),
                  pl.BlockSpec((1, N_MOD, D_MODEL), lambda i: (mi(i), 0, 0))],
        out_specs=pl.BlockSpec((ROW_TILE, D_MODEL), row),
        out_shape=jax.ShapeDtypeStruct((lay.t, D_MODEL), F32),
        compiler_params=_cparams(("arbitrary",)),
        name="even_out",
    )(x, y_a, y_t, proj, ssm_d.reshape(1, -1), glu_w_bf16, glu_b.reshape(1, -1), w_out_bf16, mods)


def _softmax_pv(parts, sink_col):
    m = sink_col
    for s, _ in parts:
        m = jnp.maximum(m, jnp.max(s, axis=-1, keepdims=True))
    den = jnp.exp(sink_col - m)
    acc = None
    for s, v in parts:
        p = jnp.exp(s - m)
        den = den + jnp.sum(p, axis=-1, keepdims=True)
        pv = jnp.dot(p.astype(BF16), v.astype(BF16), preferred_element_type=F32)
        acc = pv if acc is None else acc + pv
    return acc / den


def _nt_dot(a, b):
    return lax.dot_general(a.astype(BF16), b.astype(BF16), (((1,), (1,)), ((), ())),
                           preferred_element_type=F32)


def _attn_ctx_kernel(q_ref, k_ref, v_ref, sink_ref, o_ref):
    n = q_ref.shape[0]
    for kh in range(N_KV):
        k = k_ref[:, kh * HEAD_DIM:(kh + 1) * HEAD_DIM]
        v = v_ref[:, kh * HEAD_DIM:(kh + 1) * HEAD_DIM]
        for g in range(GQA):
            h = kh * GQA + g
            q = q_ref[:, h * HEAD_DIM:(h + 1) * HEAD_DIM]
            s = _nt_dot(q, k) * ATTN_SCALE
            sink = jnp.broadcast_to(sink_ref[0:1, h:h + 1], (n, 1))
            o_ref[:, h * HEAD_DIM:(h + 1) * HEAD_DIM] = _softmax_pv([(s, v)], sink)


def attn_context(lay, qkv, sink):
    nq = N_HEADS * HEAD_DIM
    nkv = N_KV * HEAD_DIM
    return pl.pallas_call(
        _attn_ctx_kernel,
        grid=(lay.n_ctx,),
        in_specs=[pl.BlockSpec((lay.s_ctx, nq), lambda b: (b, 0)),
                  pl.BlockSpec((lay.s_ctx, nkv), lambda b: (b, nq // nkv)),
                  pl.BlockSpec((lay.s_ctx, nkv), lambda b: (b, nq // nkv + 1)),
                  pl.BlockSpec((1, N_HEADS), lambda b: (0, 0))],
        out_specs=pl.BlockSpec((lay.s_ctx, nq), lambda b: (b, 0)),
        out_shape=jax.ShapeDtypeStruct((lay.t_ctx, nq), F32),
        compiler_params=_cparams(("arbitrary",)),
        name="attn_context",
    )(qkv, qkv, qkv, sink.reshape(1, -1))


def _rope(x, cos, sin):
    lane = lax.broadcasted_iota(I32, (x.shape[0], 2 * HEAD_DIM), 1) % HEAD_DIM
    outs = []
    for j in range(x.shape[1] // (2 * HEAD_DIM)):
        xs = x[:, j * 2 * HEAD_DIM:(j + 1) * 2 * HEAD_DIM]
        sw = jnp.where(lane < HEAD_DIM // 2,
                       pltpu.roll(xs, 2 * HEAD_DIM - HEAD_DIM // 2, 1),
                       pltpu.roll(xs, HEAD_DIM // 2, 1))
        outs.append(xs * cos + sw * sin)
    return outs


def _attn_lat_kernel(q_ref, k0_ref, k1_ref, k2_ref, v0_ref, v1_ref, v2_ref, ck_ref, cv_ref,
                     cq_ref, sq_ref, c0_ref, c1_ref, c2_ref, s0_ref, s1_ref, s2_ref, sink_ref,
                     o_ref, *, n_blk):
    j = pl.program_id(1)
    qb = Q_BLOCK
    q_parts = _rope(q_ref[...], cq_ref[...], sq_ref[...])
    k_parts = [_rope(kr[...], cr[...], sr[...])
               for kr, cr, sr in ((k0_ref, c0_ref, s0_ref), (k1_ref, c1_ref, s1_ref),
                                  (k2_ref, c2_ref, s2_ref))]
    qi = lax.broadcasted_iota(I32, (qb, 3 * qb), 0)
    km = lax.broadcasted_iota(I32, (qb, 3 * qb), 1)
    kpos = j * qb - qb + km
    mask1 = (jnp.abs(km - qb - qi) <= WINDOW) & (kpos >= 0) & (kpos < n_blk * qb)
    mask = jnp.concatenate([mask1] * GQA, axis=0)
    for kh in range(N_KV):
        half = (kh % 2) * HEAD_DIM
        k_loc = jnp.concatenate([kp[kh // 2][:, half:half + HEAD_DIM] for kp in k_parts], axis=0)
        v_loc = jnp.concatenate([vr[:, kh * HEAD_DIM:(kh + 1) * HEAD_DIM]
                                 for vr in (v0_ref, v1_ref, v2_ref)], axis=0)
        qs, sinks = [], []
        for g in range(GQA):
            h = kh * GQA + g
            qs.append(q_parts[h // 2][:, (h % 2) * HEAD_DIM:(h % 2 + 1) * HEAD_DIM])
            sinks.append(jnp.broadcast_to(sink_ref[0:1, h:h + 1], (qb, 1)))
        q = jnp.concatenate(qs, axis=0)
        sink = jnp.concatenate(sinks, axis=0)
        s_loc = jnp.where(mask, _nt_dot(q, k_loc) * ATTN_SCALE, NEG_INF)
        s_ctx = _nt_dot(q, ck_ref[0, kh]) * ATTN_SCALE
        o = _softmax_pv([(s_loc, v_loc), (s_ctx, cv_ref[0, kh])], sink)
        for g in range(GQA):
            h = kh * GQA + g
            o_ref[:, h * HEAD_DIM:(h + 1) * HEAD_DIM] = o[g * qb:(g + 1) * qb]


def _rope_tables(s_len):
    rows = s_len // GRID_W
    row = jnp.repeat(jnp.arange(rows), GRID_W).astype(F32)
    col = jnp.tile(jnp.arange(GRID_W), rows).astype(F32)
    nf = HEAD_DIM // 4
    inv = ROPE_BASE ** (-jnp.arange(nf, dtype=F32) / nf)
    ang = jnp.concatenate([row[:, None] * inv, col[:, None] * inv], axis=-1)
    cos, sin = jnp.cos(ang), jnp.sin(ang)
    cos2 = jnp.tile(jnp.concatenate([cos, cos], axis=-1), (1, 2))
    sin2 = jnp.tile(jnp.concatenate([-sin, sin], axis=-1), (1, 2))
    return cos2, sin2


def attn_latent(lay, qkv, cache_k, cache_v, sink):
    nq = N_HEADS * HEAD_DIM
    nkv = N_KV * HEAD_DIM
    n_blk = lay.s_lat // Q_BLOCK
    base = lay.t_ctx // Q_BLOCK
    n_ctx_keys = cache_k.shape[2]
    cos2, sin2 = _rope_tables(lay.s_lat)
    kcol = nq // nkv

    def qrow(b, j):
        return base + b * n_blk + j

    def krow(off):
        return lambda b, j: base + b * n_blk + jnp.clip(j + off, 0, n_blk - 1)

    def trow(off):
        return lambda b, j: (jnp.clip(j + off, 0, n_blk - 1), 0)

    kv_spec = lambda off, col: pl.BlockSpec((Q_BLOCK, nkv), lambda b, j: (krow(off)(b, j), col))
    tab = lambda off: pl.BlockSpec((Q_BLOCK, 2 * HEAD_DIM), trow(off))
    cache_spec = pl.BlockSpec((1, N_KV, n_ctx_keys, HEAD_DIM), lambda b, j: (b, 0, 0, 0))
    return pl.pallas_call(
        functools.partial(_attn_lat_kernel, n_blk=n_blk),
        grid=(lay.n_lat, n_blk),
        in_specs=[pl.BlockSpec((Q_BLOCK, nq), lambda b, j: (qrow(b, j), 0)),
                  kv_spec(-1, kcol), kv_spec(0, kcol), kv_spec(1, kcol),
                  kv_spec(-1, kcol + 1), kv_spec(0, kcol + 1), kv_spec(1, kcol + 1),
                  cache_spec, cache_spec,
                  tab(0), tab(0), tab(-1), tab(0), tab(1), tab(-1), tab(0), tab(1),
                  pl.BlockSpec((1, N_HEADS), lambda b, j: (0, 0))],
        out_specs=pl.BlockSpec((Q_BLOCK, nq), lambda b, j: (b * n_blk + j, 0)),
        out_shape=jax.ShapeDtypeStruct((lay.t_lat, nq), F32),
        compiler_params=_cparams(("arbitrary", "arbitrary")),
        name="attn_latent",
    )(qkv, qkv, qkv, qkv, qkv, qkv, qkv, cache_k, cache_v,
      cos2, sin2, cos2, cos2, cos2, sin2, sin2, sin2, sink.reshape(1, -1))


def _mm_res_kernel(x_ref, a_ref, w_ref, mod_ref, o_ref):
    out = jnp.dot(a_ref[...].astype(BF16), w_ref[...], preferred_element_type=F32)
    o_ref[...] = x_ref[...] + mod_ref[0, 2:3, :] * out


def matmul_residual(lay, x, a, w_bf16, mods):
    mi = _mod_index(lay, ROW_TILE)
    k = a.shape[1]
    return pl.pallas_call(
        _mm_res_kernel,
        grid=(lay.t // ROW_TILE,),
        in_specs=[pl.BlockSpec((ROW_TILE, D_MODEL), lambda i: (i, 0)),
                  pl.BlockSpec((ROW_TILE, k), lambda i: (i, 0)),
                  pl.BlockSpec((k, D_MODEL), lambda i: (0, 0)),
                  pl.BlockSpec((1, N_MOD, D_MODEL), lambda i: (mi(i), 0, 0))],
        out_specs=pl.BlockSpec((ROW_TILE, D_MODEL), lambda i: (i, 0)),
        out_shape=jax.ShapeDtypeStruct((lay.t, D_MODEL), F32),
        compiler_params=_cparams(("arbitrary",)),
        name="matmul_residual",
    )(x, a, w_bf16, mods)


def _router_kernel(x_ref, g_ref, mod_ref, rwt_ref, rb_ref, tri_ref,
                   hn_ref, eidx_ref, wts_ref, rank_ref, cnt_ref, cnt_s):
    tm = ROUTER_TILE

    @pl.when(pl.program_id(0) == 0)
    def _():
        cnt_s[...] = jnp.zeros_like(cnt_s)

    h = _modnorm(x_ref[...], g_ref[...], mod_ref, 3)
    hn_ref[...] = h
    logits = lax.dot_general(rwt_ref[...], h, (((1,), (1,)), ((), ())),
                             precision=HIGHEST, preferred_element_type=F32)
    scores = jax.nn.sigmoid(logits)
    choice = scores + rb_ref[...]
    gs_rows = []
    for g in range(N_GROUPS):
        cg = choice[g * GROUP_SIZE:(g + 1) * GROUP_SIZE, :]
        m1 = jnp.max(cg, axis=0, keepdims=True)
        eq = cg == m1
        cnt = jnp.sum(eq.astype(F32), axis=0, keepdims=True)
        m2 = jnp.max(jnp.where(eq, -jnp.inf, cg), axis=0, keepdims=True)
        gs_rows.append(m1 + jnp.where(cnt >= 2.0, m1, m2))
    gs = jnp.concatenate(gs_rows, axis=0)
    gi = lax.broadcasted_iota(I32, (N_GROUPS, tm), 0)
    grank = jnp.zeros((N_GROUPS, tm), I32)
    for g in range(N_GROUPS):
        other = gs[g:g + 1, :]
        ahead = (other > gs) | ((other == gs) & (g < gi))
        grank = grank + ahead.astype(I32)
    gsel = grank < TOPK_GROUPS
    emask = jnp.concatenate(
        [jnp.broadcast_to(gsel[g:g + 1, :], (GROUP_SIZE, tm)) for g in range(N_GROUPS)], axis=0)
    masked = jnp.where(emask, choice, -jnp.inf)
    ei = lax.broadcasted_iota(I32, (N_EXPERTS, tm), 0)
    idxs, ws = [], []
    member = jnp.zeros((N_EXPERTS, tm), F32)
    for _ in range(TOP_K):
        m = jnp.max(masked, axis=0, keepdims=True)
        idx = jnp.min(jnp.where(masked == m, ei, N_EXPERTS), axis=0, keepdims=True)
        hit = ei == idx
        ws.append(jnp.sum(jnp.where(hit, scores, 0.0), axis=0, keepdims=True))
        idxs.append(idx)
        member = jnp.where(hit, 1.0, member)
        masked = jnp.where(hit, -jnp.inf, masked)
    w = jnp.concatenate(ws, axis=0)
    wts_ref[...] = w / jnp.sum(w, axis=0, keepdims=True) * ROUTE_SCALE
    eidx_ref[...] = jnp.concatenate(idxs, axis=0)
    before = jnp.dot(member.astype(BF16), tri_ref[...], preferred_element_type=F32) + cnt_s[...]
    ranks = [jnp.sum(jnp.where(ei == idx, before, 0.0), axis=0, keepdims=True) for idx in idxs]
    rank_ref[...] = jnp.concatenate(ranks, axis=0).astype(I32)
    cnt_s[...] = cnt_s[...] + jnp.sum(member, axis=1, keepdims=True)
    cnt_ref[...] = jnp.broadcast_to(cnt_s[...], cnt_ref.shape)


def moe_router(lay, x, g, mods, router_w, router_b):
    t = lay.t
    tm = ROUTER_TILE
    mi = _mod_index(lay, tm)
    tri = (jnp.arange(tm)[:, None] < jnp.arange(tm)[None, :]).astype(BF16)
    tok = lambda i: (0, i)
    const = lambda i: (0, 0)
    return pl.pallas_call(
        _router_kernel,
        grid=(t // tm,),
        in_specs=[pl.BlockSpec((tm, D_MODEL), lambda i: (i, 0)),
                  pl.BlockSpec((1, D_MODEL), const),
                  pl.BlockSpec((1, N_MOD, D_MODEL), lambda i: (mi(i), 0, 0)),
                  pl.BlockSpec((N_EXPERTS, D_MODEL), const),
                  pl.BlockSpec((N_EXPERTS, 1), const),
                  pl.BlockSpec((tm, tm), const)],
        out_specs=[pl.BlockSpec((tm, D_MODEL), lambda i: (i, 0)),
                   pl.BlockSpec((TOP_K, tm), tok),
                   pl.BlockSpec((TOP_K, tm), tok),
                   pl.BlockSpec((TOP_K, tm), tok),
                   pl.BlockSpec((N_EXPERTS, 128), const)],
        out_shape=[jax.ShapeDtypeStruct((t, D_MODEL), F32),
                   jax.ShapeDtypeStruct((TOP_K, t), I32),
                   jax.ShapeDtypeStruct((TOP_K, t), F32),
                   jax.ShapeDtypeStruct((TOP_K, t), I32),
                   jax.ShapeDtypeStruct((N_EXPERTS, 128), F32)],
        scratch_shapes=[pltpu.VMEM((N_EXPERTS, 1), F32)],
        compiler_params=_cparams(("arbitrary",)),
        name="moe_router",
    )(x, g.reshape(1, -1), mods, router_w.T, router_b.reshape(-1, 1), tri)


def _issue_rows(idx_smem, src_hbm, dst, sem, n):
    def body(r, c):
        pltpu.make_async_copy(src_hbm.at[pl.ds(idx_smem[0, r], 1), :],
                              dst.at[pl.ds(r, 1), :], sem).start()
        return c
    lax.fori_loop(0, n, body, 0, unroll=8)


def _expert_kernel(blk_e_ref, rows_hbm, hn_hbm, wg_ref, wu_ref, wd_ref, y_ref,
                   idx_s, xbuf, wg_s, wu_s, wd_s, isem, gsem, *, n_blk):
    b = pl.program_id(0)
    slot = b % 2

    def idx_copy(blk, s):
        return pltpu.make_async_copy(rows_hbm.at[pl.ds(blk, 1), :], idx_s.at[s], isem.at[s])

    def gather_wait(s):
        pltpu.make_async_copy(hn_hbm.at[pl.ds(0, MOE_BLOCK), :], xbuf.at[s], gsem.at[s]).wait()

    @pl.when(b == 0)
    def _():
        c = idx_copy(0, 0)
        c.start()
        c.wait()
        _issue_rows(idx_s.at[0], hn_hbm, xbuf.at[0], gsem.at[0], MOE_BLOCK)

    @pl.when(b + 1 < n_blk)
    def _():
        c = idx_copy(b + 1, 1 - slot)
        c.start()
        c.wait()
        _issue_rows(idx_s.at[1 - slot], hn_hbm, xbuf.at[1 - slot], gsem.at[1 - slot], MOE_BLOCK)

    prev_e = blk_e_ref[jnp.maximum(b - 1, 0)]

    @pl.when(jnp.logical_or(b == 0, blk_e_ref[b] != prev_e))
    def _():
        wg_s[...] = wg_ref[0].astype(BF16)
        wu_s[...] = wu_ref[0].astype(BF16)
        wd_s[...] = wd_ref[0].astype(BF16)

    gather_wait(slot)
    x = xbuf[slot].astype(BF16)
    gate = jnp.dot(x, wg_s[...], preferred_element_type=F32)
    up = jnp.dot(x, wu_s[...], preferred_element_type=F32)
    act = gate * jax.nn.sigmoid(gate) * up
    y_ref[...] = jnp.dot(act.astype(BF16), wd_s[...], preferred_element_type=F32)


def moe_experts(hn, row_tok, blk_e, w_gate, w_up, w_down):
    n_blk = blk_e.shape[0]
    wspec = lambda shape: pl.BlockSpec((1,) + shape, lambda b, be: (be[b], 0, 0))
    return pl.pallas_call(
        functools.partial(_expert_kernel, n_blk=n_blk),
        grid_spec=pltpu.PrefetchScalarGridSpec(
            num_scalar_prefetch=1,
            grid=(n_blk,),
            in_specs=[pl.BlockSpec(memory_space=pl.ANY),
                      pl.BlockSpec(memory_space=pl.ANY),
                      wspec((D_MODEL, D_EXPERT)), wspec((D_MODEL, D_EXPERT)),
                      wspec((D_EXPERT, D_MODEL))],
            out_specs=pl.BlockSpec((MOE_BLOCK, D_MODEL), lambda b, be: (b, 0)),
            scratch_shapes=[pltpu.SMEM((2, 1, MOE_BLOCK), I32),
                            pltpu.VMEM((2, MOE_BLOCK, D_MODEL), F32),
                            pltpu.VMEM((D_MODEL, D_EXPERT), BF16),
                            pltpu.VMEM((D_MODEL, D_EXPERT), BF16),
                            pltpu.VMEM((D_EXPERT, D_MODEL), BF16),
                            pltpu.SemaphoreType.DMA((2,)),
                            pltpu.SemaphoreType.DMA((2,))]),
        out_shape=jax.ShapeDtypeStruct((n_blk * MOE_BLOCK, D_MODEL), F32),
        compiler_params=_cparams(("arbitrary",)),
        name="moe_experts",
    )(blk_e, row_tok.reshape(n_blk, MOE_BLOCK), hn, w_gate, w_up, w_down)


def _combine_kernel(dest_hbm, y_hbm, x_ref, hn_ref, w_ref, sg_ref, su_ref, sd_ref, mod_ref, o_ref,
                    idx_s, ybuf, isem, gsem, *, n_tiles):
    i = pl.program_id(0)
    slot = i % 2
    tm = COMBINE_TILE

    def idx_copy(tile, s):
        return pltpu.make_async_copy(dest_hbm.at[pl.ds(tile, 1), :], idx_s.at[s], isem.at[s])

    def fetch(tile, s):
        c = idx_copy(tile, s)
        c.start()
        c.wait()
        _issue_rows(idx_s.at[s], y_hbm, ybuf.at[s], gsem.at[s], TOP_K * tm)

    @pl.when(i == 0)
    def _():
        fetch(0, 0)

    @pl.when(i + 1 < n_tiles)
    def _():
        fetch(i + 1, 1 - slot)

    h = hn_ref[...].astype(BF16)
    sgate = jnp.dot(h, sg_ref[...], preferred_element_type=F32)
    sup = jnp.dot(h, su_ref[...], preferred_element_type=F32)
    shared = jnp.dot((sgate * jax.nn.sigmoid(sgate) * sup).astype(BF16), sd_ref[...],
                     preferred_element_type=F32)
    pltpu.make_async_copy(y_hbm.at[pl.ds(0, TOP_K * tm), :], ybuf.at[slot], gsem.at[slot]).wait()
    w = w_ref[...]
    routed = jnp.zeros((tm, D_MODEL), F32)
    for k in range(TOP_K):
        routed = routed + w[:, k:k + 1] * ybuf[slot, k * tm:(k + 1) * tm, :]
    o_ref[...] = x_ref[...] + mod_ref[0, 5:6, :] * (routed + shared)


def moe_combine(lay, x, hn, y_rows, dest, wts_t, sg_bf16, su_bf16, sd_bf16, mods):
    tm = COMBINE_TILE
    n_tiles = lay.t // tm
    mi = _mod_index(lay, tm)
    row = lambda i: (i, 0)
    const = lambda i: (0, 0)
    return pl.pallas_call(
        functools.partial(_combine_kernel, n_tiles=n_tiles),
        grid=(n_tiles,),
        in_specs=[pl.BlockSpec(memory_space=pl.ANY),
                  pl.BlockSpec(memory_space=pl.ANY),
                  pl.BlockSpec((tm, D_MODEL), row),
                  pl.BlockSpec((tm, D_MODEL), row),
                  pl.BlockSpec((tm, TOP_K), row),
                  pl.BlockSpec((D_MODEL, D_EXPERT), const),
                  pl.BlockSpec((D_MODEL, D_EXPERT), const),
                  pl.BlockSpec((D_EXPERT, D_MODEL), const),
                  pl.BlockSpec((1, N_MOD, D_MODEL), lambda i: (mi(i), 0, 0))],
        out_specs=pl.BlockSpec((tm, D_MODEL), row),
        out_shape=jax.ShapeDtypeStruct((lay.t, D_MODEL), F32),
        scratch_shapes=[pltpu.SMEM((2, 1, TOP_K * tm), I32),
                        pltpu.VMEM((2, TOP_K * tm, D_MODEL), F32),
                        pltpu.SemaphoreType.DMA((2,)),
                        pltpu.SemaphoreType.DMA((2,))],
        compiler_params=_cparams(("arbitrary",)),
        name="moe_combine",
    )(dest, y_rows, x, hn, wts_t, sg_bf16, su_bf16, sd_bf16, mods)


def moe_layer(lay, x, g, mods, router_w, router_b, w_gate, w_up, w_down, s_gate, s_up, s_down):
    t = lay.t
    hn, eidx, wts, rank, cnt = moe_router(lay, x, g, mods, router_w, router_b)
    counts = cnt[:, 0].astype(I32)
    padded = (counts + MOE_BLOCK - 1) // MOE_BLOCK * MOE_BLOCK
    pad_end = jnp.cumsum(padded)
    pad_start = pad_end - padded
    dest = pad_start[eidx] + rank
    n_blk = -(-(t * TOP_K + N_EXPERTS * (MOE_BLOCK - 1)) // MOE_BLOCK)
    tok = jnp.broadcast_to(jnp.arange(t, dtype=I32)[None, :], (TOP_K, t))
    row_tok = jnp.zeros((n_blk * MOE_BLOCK,), I32).at[dest.reshape(-1)].set(tok.reshape(-1))
    blk_e = jnp.minimum(jnp.searchsorted(pad_end, jnp.arange(n_blk) * MOE_BLOCK, side='right'),
                        N_EXPERTS - 1).astype(I32)
    y_rows = moe_experts(hn, row_tok, blk_e, w_gate, w_up, w_down)
    tm = COMBINE_TILE
    dest_tiles = dest.reshape(TOP_K, t // tm, tm).transpose(1, 0, 2).reshape(t // tm, TOP_K * tm)
    return moe_combine(lay, x, hn, y_rows, dest_tiles, wts.T,
                       s_gate.astype(BF16), s_up.astype(BF16), s_down.astype(BF16), mods)


def _final_norm_kernel(x_ref, g_ref, o_ref):
    x = x_ref[...]
    ms = jnp.mean(x * x, axis=-1, keepdims=True)
    o_ref[...] = x * lax.rsqrt(ms + EPS) * g_ref[...]


def final_norm(x, g, row0, n_rows):
    base = row0 // ROW_TILE
    return pl.pallas_call(
        _final_norm_kernel,
        grid=(n_rows // ROW_TILE,),
        in_specs=[pl.BlockSpec((ROW_TILE, D_MODEL), lambda i: (base + i, 0)),
                  pl.BlockSpec((1, D_MODEL), lambda i: (0, 0))],
        out_specs=pl.BlockSpec((ROW_TILE, D_MODEL), lambda i: (i, 0)),
        out_shape=jax.ShapeDtypeStruct((n_rows, D_MODEL), F32),
        compiler_params=_cparams(("arbitrary",)),
        name="final_norm",
    )(x, g.reshape(1, -1))


def _block_diag(w):
    nb, bw, _ = w.shape
    eye = jnp.eye(nb, dtype=w.dtype)
    return (eye[:, None, :, None] * w[:, :, None, :]).reshape(nb * bw, nb * bw)


def even_layer(lay, x, mods, g_mix, p, state_lru, state_ssm_re, state_ssm_im):
    t = lay.t
    proj = modnorm_matmul(lay, x, g_mix, mods, 0, p['w_in'].astype(BF16))
    zeros_c = jnp.zeros((lay.n_ctx, D_LRU), F32)
    hf_y, st = None, []
    for d in range(2):
        wg = jnp.concatenate([_block_diag(p['lru_wa'][d]), _block_diag(p['lru_wx'][d])], axis=1)
        bg = jnp.concatenate([p['lru_ba'][d], p['lru_bx'][d]])
        h0 = jnp.concatenate([zeros_c, state_lru[:, d].astype(F32)], axis=0)
        hf_y, s = lru_pass(lay, proj, p['conv_w'], p['conv_b'], wg.astype(BF16), bg,
                           p['lru_lam'][d], h0, reverse=(d == 1), hf=hf_y)
        st.append(s[:lay.n_ctx])
    y_a = hf_y
    new_lru = jnp.stack(st, axis=1)

    mats = _s5_matrices(p['a_re'], p['a_im'], p['log_dt'], p['b_re'], p['b_im'], p['c_re'], p['c_im'])
    rows = t // S5_CHUNK
    u_g = proj[:, 2 * D_LRU:].reshape(rows, S5_CHUNK, SSM_GROUPS, SSM_GROUP)
    u_g = u_g.transpose(2, 0, 1, 3).reshape(SSM_GROUPS, rows, S5_LANES)
    h0 = jnp.concatenate([state_ssm_re, state_ssm_im], axis=-1).astype(F32)
    h0 = h0.transpose(2, 1, 0, 3)
    y_g, h_ctx = s5_mixer(lay, u_g, mats, h0)
    y_t = y_g.reshape(SSM_GROUPS, rows, S5_CHUNK, SSM_GROUP).transpose(1, 2, 0, 3).reshape(t, D_SSM)
    seg = lay.s_ctx // S5_CHUNK
    h_ctx = h_ctx.reshape(SSM_GROUPS, 2, lay.n_ctx, seg, 2 * SSM_STATE)
    ends = jnp.stack([h_ctx[:, 0, :, seg - 1], h_ctx[:, 1, :, 0]], axis=1)
    ends = ends.transpose(2, 1, 0, 3)
    x = even_out(lay, x, y_a, y_t, proj, p['d'], p['glu_w'].astype(BF16), p['glu_b'],
                 p['w_out'].astype(BF16), mods)
    return x, new_lru, ends[..., :SSM_STATE], ends[..., SSM_STATE:]


def odd_layer(lay, x, mods, g_mix, w_qkv, sink, w_out, cache_k, cache_v):
    qkv = modnorm_matmul(lay, x, g_mix, mods, 0, w_qkv.astype(BF16))
    o_ctx = attn_context(lay, qkv, sink)
    o_lat = attn_latent(lay, qkv, cache_k, cache_v, sink)
    o = jnp.concatenate([o_ctx, o_lat], axis=0)
    nq = N_HEADS * HEAD_DIM
    nkv = N_KV * HEAD_DIM
    kv = qkv[:lay.t_ctx, nq:].reshape(lay.n_ctx, lay.s_ctx, 2, N_KV, HEAD_DIM)
    k_new = kv[:, :, 0].swapaxes(1, 2)
    v_new = kv[:, :, 1].swapaxes(1, 2)
    x = matmul_residual(lay, x, o, w_out.astype(BF16), mods)
    return x, k_new, v_new


def _forward(lay, x_prompt, x_sample, state_lru, state_ssm_re, state_ssm_im, cache_k, cache_v,
             c, c_ctx, g_mix, g_ffn, w_mod, b_mod,
             ev_w_in, lru_conv_w, lru_conv_b, lru_wa, lru_ba, lru_wx, lru_bx, lru_lam,
             ssm_a_re, ssm_a_im, ssm_log_dt, ssm_b_re, ssm_b_im, ssm_c_re, ssm_c_im, ssm_d,
             ssm_glu_w, ssm_glu_b, ev_w_out, at_w_qkv, at_sink, at_w_out,
             router_w, router_b, exp_w_gate, exp_w_up, exp_w_down, sh_w_gate, sh_w_up, sh_w_down,
             g_final):
    depth = g_mix.shape[0]
    x = jnp.concatenate([x_prompt.reshape(lay.t_ctx, D_MODEL), x_sample.reshape(lay.t_lat, D_MODEL)],
                        axis=0)
    n_c = 1 + lay.n_lat
    c_rows = jnp.concatenate([c_ctx[None, :], c, jnp.zeros((16 - n_c, D_MODEL), F32)], axis=0)
    new_lru, new_re, new_im, new_k, new_v = [], [], [], [], []
    for l in range(depth):
        i = l // 2
        mods = adaln_table(c_rows, w_mod[l], b_mod[l])
        if l % 2 == 0:
            p = dict(w_in=ev_w_in[i], conv_w=lru_conv_w[i], conv_b=lru_conv_b[i],
                     lru_wa=lru_wa[i], lru_ba=lru_ba[i], lru_wx=lru_wx[i], lru_bx=lru_bx[i],
                     lru_lam=lru_lam[i], a_re=ssm_a_re[i], a_im=ssm_a_im[i], log_dt=ssm_log_dt[i],
                     b_re=ssm_b_re[i], b_im=ssm_b_im[i], c_re=ssm_c_re[i], c_im=ssm_c_im[i],
                     d=ssm_d[i], glu_w=ssm_glu_w[i], glu_b=ssm_glu_b[i], w_out=ev_w_out[i])
            x, lru_i, re_i, im_i = even_layer(lay, x, mods, g_mix[l], p, state_lru[:, i],
                                              state_ssm_re[:, i], state_ssm_im[:, i])
            new_lru.append(lru_i)
            new_re.append(re_i)
            new_im.append(im_i)
        else:
            x, k_i, v_i = odd_layer(lay, x, mods, g_mix[l], at_w_qkv[i], at_sink[i], at_w_out[i],
                                    cache_k[:, i], cache_v[:, i])
            new_k.append(k_i)
            new_v.append(v_i)
        x = moe_layer(lay, x, g_ffn[l], mods, router_w[l], router_b[l], exp_w_gate[l], exp_w_up[l],
                      exp_w_down[l], sh_w_gate[l], sh_w_up[l], sh_w_down[l])
    y_prompt = final_norm(x, g_final, 0, lay.t_ctx).reshape(x_prompt.shape)
    y_sample = final_norm(x, g_final, lay.t_ctx, lay.t_lat).reshape(x_sample.shape)
    return (y_prompt, y_sample, jnp.stack(new_lru, axis=1), jnp.stack(new_re, axis=1),
            jnp.stack(new_im, axis=1), jnp.stack(new_k, axis=1), jnp.stack(new_v, axis=1))


def kernel(x_prompt, x_sample, state_lru, state_ssm_re, state_ssm_im, cache_k, cache_v, c, c_ctx, g_mix, g_ffn, w_mod, b_mod, ev_w_in, lru_conv_w, lru_conv_b, lru_wa, lru_ba, lru_wx, lru_bx, lru_lam, ssm_a_re, ssm_a_im, ssm_log_dt, ssm_b_re, ssm_b_im, ssm_c_re, ssm_c_im, ssm_d, ssm_glu_w, ssm_glu_b, ev_w_out, at_w_qkv, at_sink, at_w_out, router_w, router_b, exp_w_gate, exp_w_up, exp_w_down, sh_w_gate, sh_w_up, sh_w_down, g_final):
    lay = Layout(n_ctx=x_prompt.shape[0], s_ctx=x_prompt.shape[1],
                 n_lat=x_sample.shape[0], s_lat=x_sample.shape[1])
    return _forward(lay, x_prompt, x_sample, state_lru, state_ssm_re, state_ssm_im, cache_k, cache_v,
                    c, c_ctx, g_mix, g_ffn, w_mod, b_mod,
                    ev_w_in, lru_conv_w, lru_conv_b, lru_wa, lru_ba, lru_wx, lru_bx, lru_lam,
                    ssm_a_re, ssm_a_im, ssm_log_dt, ssm_b_re, ssm_b_im, ssm_c_re, ssm_c_im, ssm_d,
                    ssm_glu_w, ssm_glu_b, ev_w_out, at_w_qkv, at_sink, at_w_out,
                    router_w, router_b, exp_w_gate, exp_w_up, exp_w_down, sh_w_gate, sh_w_up,
                    sh_w_down, g_final)
```

```python
import functools
from typing import NamedTuple

import jax
import jax.numpy as jnp
from jax import lax
from jax.experimental import pallas as pl
from jax.experimental.pallas import tpu as pltpu

F32 = jnp.float32
BF16 = jnp.bfloat16
I32 = jnp.int32
HIGHEST = lax.Precision.HIGHEST

D_MODEL = 1024
EPS = 1e-6
N_MOD = 6
GRID_W = 64
D_LRU = 512
LRU_BLOCKS = 8
LRU_C = 8.0
CONV_W = 4
CONV_LEFT = 2
D_SSM = 512
SSM_GROUP = 16
SSM_GROUPS = 32
SSM_STATE = 64
S5_CHUNK = 16
S5_LANES = S5_CHUNK * SSM_GROUP
S5_SCAN_STEPS = 8
HEAD_DIM = 64
N_HEADS = 16
N_KV = 4
GQA = 4
WINDOW = 128
Q_BLOCK = 128
ROPE_BASE = 10000.0
ATTN_SCALE = HEAD_DIM ** -0.5
NEG_INF = -1e30
N_EXPERTS = 256
TOP_K = 8
N_GROUPS = 8
TOPK_GROUPS = 4
GROUP_SIZE = N_EXPERTS // N_GROUPS
D_EXPERT = 256
ROUTE_SCALE = 2.5
MOE_BLOCK = 128

SUBLANES = 8
LANES = 128
ROW_CHUNKS = D_MODEL // LANES
SEQ_TILE = 256
ROW_TILE = 512
ROUTER_TILE = 512
DEST_TILE = 1024
DISPATCH_TILE = 256
COMBINE_TILE = 128
VMEM_LIMIT = 56 * 1024 * 1024


class Layout(NamedTuple):
    n_ctx: int
    s_ctx: int
    n_lat: int
    s_lat: int

    @property
    def t_ctx(self):
        return self.n_ctx * self.s_ctx

    @property
    def t_lat(self):
        return self.n_lat * self.s_lat

    @property
    def t(self):
        return self.t_ctx + self.t_lat

    @property
    def n_seq(self):
        return self.n_ctx + self.n_lat


def _cparams(sem):
    return pltpu.CompilerParams(dimension_semantics=sem, vmem_limit_bytes=VMEM_LIMIT)


def _mod_index(lay, tile_rows):
    n_ctx_tiles = lay.t_ctx // tile_rows
    per_lat = lay.s_lat // tile_rows

    def f(i):
        return jnp.where(i < n_ctx_tiles, 0, 1 + (i - n_ctx_tiles) // per_lat)
    return f


def _adaln_kernel(c_ref, w_ref, b_ref, o_ref):
    c = c_ref[...]
    s = c * jax.nn.sigmoid(c)
    o_ref[...] = jnp.dot(s, w_ref[...], precision=HIGHEST, preferred_element_type=F32) + b_ref[...]


def adaln_table(c_rows, w_mod, b_mod):
    n = c_rows.shape[0]
    tn = 1536
    out = pl.pallas_call(
        _adaln_kernel,
        grid=(N_MOD * D_MODEL // tn,),
        in_specs=[pl.BlockSpec((n, D_MODEL), lambda j: (0, 0)),
                  pl.BlockSpec((D_MODEL, tn), lambda j: (0, j)),
                  pl.BlockSpec((1, tn), lambda j: (0, j))],
        out_specs=pl.BlockSpec((n, tn), lambda j: (0, j)),
        out_shape=jax.ShapeDtypeStruct((n, N_MOD * D_MODEL), F32),
        compiler_params=_cparams(("arbitrary",)),
        name="adaln",
    )(c_rows, w_mod, b_mod.reshape(1, -1))
    return out.reshape(n, N_MOD, D_MODEL)


def _modnorm(x, g, mod_ref, slot):
    ms = jnp.mean(x * x, axis=-1, keepdims=True)
    y = x * lax.rsqrt(ms + EPS) * g
    shift = mod_ref[0, slot:slot + 1, :]
    scale = mod_ref[0, slot + 1:slot + 2, :]
    return y * (1.0 + scale) + shift


def _modnorm_mm_kernel(x_ref, g_ref, mod_ref, w_ref, o_ref, *, slot):
    h = _modnorm(x_ref[...], g_ref[...], mod_ref, slot)
    o_ref[...] = jnp.dot(h.astype(BF16), w_ref[...], preferred_element_type=F32)


def modnorm_matmul(lay, x, g, mods, slot, w_bf16):
    t = lay.t
    n = w_bf16.shape[1]
    mi = _mod_index(lay, ROW_TILE)
    return pl.pallas_call(
        functools.partial(_modnorm_mm_kernel, slot=slot),
        grid=(t // ROW_TILE,),
        in_specs=[pl.BlockSpec((ROW_TILE, D_MODEL), lambda i: (i, 0)),
                  pl.BlockSpec((1, D_MODEL), lambda i: (0, 0)),
                  pl.BlockSpec((1, N_MOD, D_MODEL), lambda i: (mi(i), 0, 0)),
                  pl.BlockSpec((D_MODEL, n), lambda i: (0, 0))],
        out_specs=pl.BlockSpec((ROW_TILE, n), lambda i: (i, 0)),
        out_shape=jax.ShapeDtypeStruct((t, n), F32),
        compiler_params=_cparams(("arbitrary",)),
        name="modnorm_matmul",
    )(x, g.reshape(1, -1), mods, w_bf16)


def _seq_tile_maps(lay, reverse):
    assert lay.s_ctx == SEQ_TILE and lay.s_lat % SEQ_TILE == 0
    n_tiles = lay.t // SEQ_TILE
    per_lat = lay.s_lat // SEQ_TILE

    def tile(i):
        return (n_tiles - 1 - i) if reverse else i

    def seq(i):
        ti = tile(i)
        return jnp.where(ti < lay.n_ctx, ti, lay.n_ctx + (ti - lay.n_ctx) // per_lat)

    return n_tiles, tile, seq


def _softplus(x):
    return jnp.maximum(x, 0.0) + jnp.log(1.0 + jnp.exp(-jnp.abs(x)))


def _lru_kernel(rec_ref, prev_ref, next_ref, cw_ref, cb_ref, wg_ref, bg_ref, lam_ref, h0_ref,
                *rest, reverse, n_ctx, per_lat, n_tiles):
    if reverse:
        gate_ref, hf_ref, y_ref, st_ref, a_s, b_s, h_s, carry = rest
    else:
        y_ref, st_ref, a_s, b_s, h_s, carry = rest
    i = pl.program_id(0)
    ti = (n_tiles - 1 - i) if reverse else i
    is_first = jnp.logical_or(ti < n_ctx, (ti - n_ctx) % per_lat == 0)
    is_last = jnp.logical_or(ti < n_ctx, (ti - n_ctx) % per_lat == per_lat - 1)
    ts = SEQ_TILE

    rec = rec_ref[...]
    prev = jnp.where(is_first, 0.0, prev_ref[...])
    nxt = jnp.where(is_last, 0.0, next_ref[...])
    ext = jnp.concatenate([prev, rec, nxt], axis=0)
    n_ext = ts + 2 * SUBLANES
    cw = cw_ref[...]
    xc = cb_ref[...] + cw[2:3, :] * rec
    xc = xc + cw[0:1, :] * pltpu.roll(ext, 2, 0)[SUBLANES:SUBLANES + ts]
    xc = xc + cw[1:2, :] * pltpu.roll(ext, 1, 0)[SUBLANES:SUBLANES + ts]
    xc = xc + cw[3:4, :] * pltpu.roll(ext, n_ext - 1, 0)[SUBLANES:SUBLANES + ts]

    gates = jax.nn.sigmoid(jnp.dot(xc.astype(BF16), wg_ref[...], preferred_element_type=F32)
                           + bg_ref[...])
    r = gates[:, :D_LRU]
    ig = gates[:, D_LRU:]
    log_a = (-LRU_C) * r * _softplus(-lam_ref[...])
    a = jnp.exp(log_a)
    b = jnp.sqrt(1.0 - jnp.exp(2.0 * log_a)) * (ig * xc)

    row8 = lax.broadcasted_iota(I32, (ts, D_LRU), 0) % SUBLANES
    for sh in (1, 2, 4):
        if reverse:
            keep = row8 < SUBLANES - sh
            a_sh = pltpu.roll(a, ts - sh, 0)
            b_sh = pltpu.roll(b, ts - sh, 0)
        else:
            keep = row8 >= sh
            a_sh = pltpu.roll(a, sh, 0)
            b_sh = pltpu.roll(b, sh, 0)
        b = b + a * jnp.where(keep, b_sh, 0.0)
        a = a * jnp.where(keep, a_sh, 1.0)
    a_s[...] = a
    b_s[...] = b

    @pl.when(is_last if reverse else is_first)
    def _():
        carry[...] = h0_ref[0]

    n_grp = ts // SUBLANES

    def body(k, c):
        gi = (n_grp - 1 - k) if reverse else k
        sl = pl.ds(pl.multiple_of(gi * SUBLANES, SUBLANES), SUBLANES)
        h = b_s[sl, :] + a_s[sl, :] * c
        h_s[sl, :] = h
        return h[0:1, :] if reverse else h[SUBLANES - 1:SUBLANES, :]

    c_fin = lax.fori_loop(0, n_grp, body, carry[...], unroll=4)
    carry[...] = c_fin
    st_ref[0] = c_fin
    if reverse:
        y_ref[...] = (hf_ref[...] + h_s[...]) * jax.nn.gelu(gate_ref[...])
    else:
        y_ref[...] = h_s[...]


def lru_pass(lay, proj, conv_w, conv_b, wg_bf16, bg, lam, h0, reverse, hf=None):
    n_tiles, tile, seq = _seq_tile_maps(lay, reverse)
    per_lat = lay.s_lat // SEQ_TILE
    blk8 = SEQ_TILE // SUBLANES
    last8 = lay.t // SUBLANES - 1
    c = D_LRU
    in_specs = [
        pl.BlockSpec((SEQ_TILE, c), lambda i: (tile(i), 1)),
        pl.BlockSpec((SUBLANES, c), lambda i: (jnp.maximum(tile(i) * blk8 - 1, 0), 1)),
        pl.BlockSpec((SUBLANES, c), lambda i: (jnp.minimum(tile(i) * blk8 + blk8, last8), 1)),
        pl.BlockSpec((CONV_W, c), lambda i: (0, 0)),
        pl.BlockSpec((1, c), lambda i: (0, 0)),
        pl.BlockSpec((c, 2 * c), lambda i: (0, 0)),
        pl.BlockSpec((1, 2 * c), lambda i: (0, 0)),
        pl.BlockSpec((1, c), lambda i: (0, 0)),
        pl.BlockSpec((1, 1, c), lambda i: (seq(i), 0, 0)),
    ]
    args = [proj, proj, proj, conv_w, conv_b.reshape(1, -1), wg_bf16, bg.reshape(1, -1),
            lam.reshape(1, -1), h0.reshape(lay.n_seq, 1, c)]
    if reverse:
        in_specs += [pl.BlockSpec((SEQ_TILE, c), lambda i: (tile(i), 0)),
                     pl.BlockSpec((SEQ_TILE, c), lambda i: (tile(i), 0))]
        args += [proj, hf]
    y, st = pl.pallas_call(
        functools.partial(_lru_kernel, reverse=reverse, n_ctx=lay.n_ctx, per_lat=per_lat,
                          n_tiles=n_tiles),
        grid=(n_tiles,),
        in_specs=in_specs,
        out_specs=[pl.BlockSpec((SEQ_TILE, c), lambda i: (tile(i), 0)),
                   pl.BlockSpec((1, 1, c), lambda i: (seq(i), 0, 0))],
        out_shape=[jax.ShapeDtypeStruct((lay.t, c), F32),
                   jax.ShapeDtypeStruct((lay.n_seq, 1, c), F32)],
        scratch_shapes=[pltpu.VMEM((SEQ_TILE, c), F32), pltpu.VMEM((SEQ_TILE, c), F32),
                        pltpu.VMEM((SEQ_TILE, c), F32), pltpu.VMEM((1, c), F32)],
        compiler_params=_cparams(("arbitrary",)),
        name="lru_bwd" if reverse else "lru_fwd",
    )(*args)
    return y, st.reshape(lay.n_seq, c)


def _cmul(a, b):
    return a[0] * b[0] - a[1] * b[1], a[0] * b[1] + a[1] * b[0]


def _s5_matrices(a_re, a_im, log_dt, b_re, b_im, c_re, c_im):
    a_re, a_im = a_re.astype(F32), a_im.astype(F32)
    dt = jnp.exp(log_dt.astype(F32))[..., None]
    z = (a_re * dt, a_im * dt)

    def zpow(k):
        k = k.reshape((-1,) + (1,) * z[0].ndim)
        mag = jnp.exp(k * z[0][None])
        return mag * jnp.cos(k * z[1][None]), mag * jnp.sin(k * z[1][None])

    a_bar = zpow(jnp.ones((1,), F32))
    a_bar = (a_bar[0][0], a_bar[1][0])
    den = a_re * a_re + a_im * a_im
    xr, xi = a_bar[0] - 1.0, a_bar[1]
    q = ((xr * a_re + xi * a_im) / den, (xi * a_re - xr * a_im) / den)
    b_bar = _cmul((q[0][..., None], q[1][..., None]), (b_re.astype(F32), b_im.astype(F32)))
    cc = (c_re.astype(F32), c_im.astype(F32))
    el = S5_CHUNK
    pw = zpow(jnp.arange(el + 1, dtype=F32))
    idx = jnp.arange(el)
    m_in, m_toep, m_out = [], [], []
    for d in range(2):
        p_d = (pw[0][:, d], pw[1][:, d])
        b_d = (b_bar[0][d], b_bar[1][d])
        c_d = (cc[0][d], cc[1][d])
        k_in = (el - 1 - idx) if d == 0 else idx
        w_in = _cmul((p_d[0][k_in][..., None], p_d[1][k_in][..., None]),
                     (b_d[0][None], b_d[1][None]))
        w_in = [jnp.transpose(w, (1, 0, 3, 2)).reshape(SSM_GROUPS, S5_LANES, SSM_STATE) for w in w_in]
        m_in.append(jnp.concatenate(w_in, axis=-1))
        cp = _cmul((c_d[0][None], c_d[1][None]),
                   (p_d[0][:, :, None, :], p_d[1][:, :, None, :]))
        kern = (jnp.einsum('kghp,gpc->kgch', cp[0][:el], b_d[0])
                - jnp.einsum('kghp,gpc->kgch', cp[1][:el], b_d[1]))
        diff = (idx[None, :] - idx[:, None]) if d == 0 else (idx[:, None] - idx[None, :])
        blocks = jnp.where((diff >= 0)[:, :, None, None, None],
                           kern[jnp.clip(diff, 0, el - 1)], 0.0)
        m_toep.append(jnp.transpose(blocks, (2, 0, 3, 1, 4)).reshape(SSM_GROUPS, S5_LANES, S5_LANES))
        k_out = (idx + 1) if d == 0 else (el - idx)
        w_out = [jnp.transpose(w[k_out], (1, 3, 0, 2)).reshape(SSM_GROUPS, SSM_STATE, S5_LANES)
                 for w in cp]
        m_out.append(jnp.concatenate([w_out[0], -w_out[1]], axis=1))
    mul = zpow(el * 2.0 ** jnp.arange(S5_SCAN_STEPS, dtype=F32))
    mul = [jnp.transpose(m, (2, 1, 0, 3)) for m in mul]
    coef_a = jnp.concatenate([mul[0], mul[0]], axis=-1)
    coef_b = jnp.concatenate([-mul[1], mul[1]], axis=-1)
    stack = lambda xs: jnp.stack(xs, axis=1)
    return (stack(m_in).astype(BF16), stack(m_toep).astype(BF16), stack(m_out).astype(BF16),
            coef_a, coef_b)


def _s5_scan(v, ca, cb, seg, reverse):
    n = v.shape[0]
    assert seg <= 2 ** S5_SCAN_STEPS
    row = lax.broadcasted_iota(I32, (n, 2 * SSM_STATE), 0) % seg
    k, sh = 0, 1
    while sh < seg:
        if reverse:
            s = jnp.where(row < seg - sh, pltpu.roll(v, n - sh, 0), 0.0)
        else:
            s = jnp.where(row >= sh, pltpu.roll(v, sh, 0), 0.0)
        v = v + ca[k:k + 1, :] * s + cb[k:k + 1, :] * pltpu.roll(s, SSM_STATE, 1)
        k += 1
        sh *= 2
    return v


def _s5_shift(h, seg, reverse):
    n = h.shape[0]
    row = lax.broadcasted_iota(I32, (n, 2 * SSM_STATE), 0) % seg
    if reverse:
        return jnp.where(row < seg - 1, pltpu.roll(h, n - 1, 0), 0.0)
    return jnp.where(row >= 1, pltpu.roll(h, 1, 0), 0.0)


def _s5_kernel(u_ref, min_ref, mtoep_ref, mout_ref, ca_ref, cb_ref, h0_ref, y_ref, hc_ref,
               v_s, hp_s, *, rc, seg_c, n_lat, seg_l):
    u = u_ref[0].astype(BF16)
    u_c, u_l = u[:rc], u[rc:]
    y_c = jnp.zeros((rc, S5_LANES), F32)
    y_l = jnp.zeros((n_lat * seg_l, S5_LANES), F32)
    for d in range(2):
        reverse = d == 1
        ca = ca_ref[0, d]
        cb = cb_ref[0, d]
        m_in = min_ref[0, d]
        m_toep = mtoep_ref[0, d]
        m_out = mout_ref[0, d]
        h_c = _s5_scan(jnp.dot(u_c, m_in, preferred_element_type=F32), ca, cb, seg_c, reverse)
        hc_ref[0, d] = h_c
        hp_c = _s5_shift(h_c, seg_c, reverse)
        y_c = y_c + jnp.dot(u_c, m_toep, preferred_element_type=F32)
        y_c = y_c + jnp.dot(hp_c.astype(BF16), m_out, preferred_element_type=F32)
        v_s[...] = jnp.dot(u_l, m_in, preferred_element_type=F32)
        for s in range(n_lat):
            h0 = h0_ref[0, d, s:s + 1, :]
            r0 = s * seg_l + (seg_l - 1 if reverse else 0)
            v_s[r0:r0 + 1, :] = (v_s[r0:r0 + 1, :] + ca[0:1, :] * h0
                                 + cb[0:1, :] * pltpu.roll(h0, SSM_STATE, 1))
        h_l = _s5_scan(v_s[...], ca, cb, seg_l, reverse)
        hp_s[...] = _s5_shift(h_l, seg_l, reverse)
        for s in range(n_lat):
            r0 = s * seg_l + (seg_l - 1 if reverse else 0)
            hp_s[r0:r0 + 1, :] = h0_ref[0, d, s:s + 1, :]
        y_l = y_l + jnp.dot(u_l, m_toep, preferred_element_type=F32)
        y_l = y_l + jnp.dot(hp_s[...].astype(BF16), m_out, preferred_element_type=F32)
    y_ref[0, :rc, :] = y_c
    y_ref[0, rc:, :] = y_l


def s5_mixer(lay, u_g, mats, h0):
    m_in, m_toep, m_out, coef_a, coef_b = mats
    rows = lay.t // S5_CHUNK
    rc = lay.t_ctx // S5_CHUNK
    rl = rows - rc
    st2 = 2 * SSM_STATE
    g4 = lambda g: (g, 0, 0, 0)
    return pl.pallas_call(
        functools.partial(_s5_kernel, rc=rc, seg_c=lay.s_ctx // S5_CHUNK, n_lat=lay.n_lat,
                          seg_l=lay.s_lat // S5_CHUNK),
        grid=(SSM_GROUPS,),
        in_specs=[pl.BlockSpec((1, rows, S5_LANES), lambda g: (g, 0, 0)),
                  pl.BlockSpec((1, 2, S5_LANES, st2), g4),
                  pl.BlockSpec((1, 2, S5_LANES, S5_LANES), g4),
                  pl.BlockSpec((1, 2, st2, S5_LANES), g4),
                  pl.BlockSpec((1, 2, S5_SCAN_STEPS, st2), g4),
                  pl.BlockSpec((1, 2, S5_SCAN_STEPS, st2), g4),
                  pl.BlockSpec((1, 2, lay.n_lat, st2), g4)],
        out_specs=[pl.BlockSpec((1, rows, S5_LANES), lambda g: (g, 0, 0)),
                   pl.BlockSpec((1, 2, rc, st2), g4)],
        out_shape=[jax.ShapeDtypeStruct((SSM_GROUPS, rows, S5_LANES), F32),
                   jax.ShapeDtypeStruct((SSM_GROUPS, 2, rc, st2), F32)],
        scratch_shapes=[pltpu.VMEM((rl, st2), F32), pltpu.VMEM((rl, st2), F32)],
        compiler_params=_cparams(("arbitrary",)),
        name="s5_mixer",
    )(u_g, m_in, m_toep, m_out, coef_a, coef_b, h0)


def _even_out_kernel(x_ref, ya_ref, yt_ref, u_ref, d_ref, gw_ref, gb_ref, w_ref, mod_ref, o_ref):
    ys = yt_ref[...] + d_ref[...] * u_ref[...]
    g = jax.nn.gelu(ys)
    yb = g * jax.nn.sigmoid(jnp.dot(g.astype(BF16), gw_ref[...], preferred_element_type=F32)
                            + gb_ref[...])
    out = jnp.dot(ya_ref[...].astype(BF16), w_ref[:D_LRU, :], preferred_element_type=F32)
    out = out + jnp.dot(yb.astype(BF16), w_ref[D_LRU:, :], preferred_element_type=F32)
    o_ref[...] = x_ref[...] + mod_ref[0, 2:3, :] * out


def even_out(lay, x, y_a, y_t, proj, ssm_d, glu_w_bf16, glu_b, w_out_bf16, mods):
    mi = _mod_index(lay, ROW_TILE)
    c = D_SSM
    row = lambda i: (i, 0)
    const = lambda i: (0, 0)
    return pl.pallas_call(
        _even_out_kernel,
        grid=(lay.t // ROW_TILE,),
        in_specs=[pl.BlockSpec((ROW_TILE, D_MODEL), row),
                  pl.BlockSpec((ROW_TILE, c), row),
                  pl.BlockSpec((ROW_TILE, c), row),
                  pl.BlockSpec((ROW_TILE, c), lambda i: (i, 2)),
                  pl.BlockSpec((1, c), const),
                  pl.BlockSpec((c, c), const),
                  pl.BlockSpec((1, c), const),
                  pl.BlockSpec((D_MODEL, D_MODEL), const),
                  pl.BlockSpec((1, N_MOD, D_MODEL), lambda i: (mi(i), 0, 0))],
        out_specs=pl.BlockSpec((ROW_TILE, D_MODEL), row),
        out_shape=jax.ShapeDtypeStruct((lay.t, D_MODEL), F32),
        compiler_params=_cparams(("arbitrary",)),
        name="even_out",
    )(x, y_a, y_t, proj, ssm_d.reshape(1, -1), glu_w_bf16, glu_b.reshape(1, -1), w_out_bf16, mods)


def _softmax_pv(parts, sink_col):
    m = sink_col
    for s, _ in parts:
        m = jnp.maximum(m, jnp.max(s, axis=-1, keepdims=True))
    den = jnp.exp(sink_col - m)
    acc = None
    for s, v in parts:
        p = jnp.exp(s - m)
        den = den + jnp.sum(p, axis=-1, keepdims=True)
        pv = jnp.dot(p.astype(BF16), v.astype(BF16), preferred_element_type=F32)
        acc = pv if acc is None else acc + pv
    return acc / den


def _nt_dot(a, b):
    return lax.dot_general(a.astype(BF16), b.astype(BF16), (((1,), (1,)), ((), ())),
                           preferred_element_type=F32)


def _attn_ctx_kernel(q_ref, k_ref, v_ref, sink_ref, o_ref):
    n = q_ref.shape[0]
    for kh in range(N_KV):
        k = k_ref[:, kh * HEAD_DIM:(kh + 1) * HEAD_DIM]
        v = v_ref[:, kh * HEAD_DIM:(kh + 1) * HEAD_DIM]
        for g in range(GQA):
            h = kh * GQA + g
            q = q_ref[:, h * HEAD_DIM:(h + 1) * HEAD_DIM]
            s = _nt_dot(q, k) * ATTN_SCALE
            sink = jnp.broadcast_to(sink_ref[0:1, h:h + 1], (n, 1))
            o_ref[:, h * HEAD_DIM:(h + 1) * HEAD_DIM] = _softmax_pv([(s, v)], sink)


def attn_context(lay, qkv, sink):
    nq = N_HEADS * HEAD_DIM
    nkv = N_KV * HEAD_DIM
    return pl.pallas_call(
        _attn_ctx_kernel,
        grid=(lay.n_ctx,),
        in_specs=[pl.BlockSpec((lay.s_ctx, nq), lambda b: (b, 0)),
                  pl.BlockSpec((lay.s_ctx, nkv), lambda b: (b, nq // nkv)),
                  pl.BlockSpec((lay.s_ctx, nkv), lambda b: (b, nq // nkv + 1)),
                  pl.BlockSpec((1, N_HEADS), lambda b: (0, 0))],
        out_specs=pl.BlockSpec((lay.s_ctx, nq), lambda b: (b, 0)),
        out_shape=jax.ShapeDtypeStruct((lay.t_ctx, nq), F32),
        compiler_params=_cparams(("arbitrary",)),
        name="attn_context",
    )(qkv, qkv, qkv, sink.reshape(1, -1))


def _rope(x, cos, sin):
    lane = lax.broadcasted_iota(I32, (x.shape[0], 2 * HEAD_DIM), 1) % HEAD_DIM
    outs = []
    for j in range(x.shape[1] // (2 * HEAD_DIM)):
        xs = x[:, j * 2 * HEAD_DIM:(j + 1) * 2 * HEAD_DIM]
        sw = jnp.where(lane < HEAD_DIM // 2,
                       pltpu.roll(xs, 2 * HEAD_DIM - HEAD_DIM // 2, 1),
                       pltpu.roll(xs, HEAD_DIM // 2, 1))
        outs.append(xs * cos + sw * sin)
    return outs


def _attn_lat_kernel(q_ref, k0_ref, k1_ref, k2_ref, v0_ref, v1_ref, v2_ref, ck_ref, cv_ref,
                     cq_ref, sq_ref, c0_ref, c1_ref, c2_ref, s0_ref, s1_ref, s2_ref, sink_ref,
                     o_ref, *, n_blk):
    j = pl.program_id(1)
    qb = Q_BLOCK
    q_parts = _rope(q_ref[...], cq_ref[...], sq_ref[...])
    k_parts = [_rope(kr[...], cr[...], sr[...])
               for kr, cr, sr in ((k0_ref, c0_ref, s0_ref), (k1_ref, c1_ref, s1_ref),
                                  (k2_ref, c2_ref, s2_ref))]
    qi = lax.broadcasted_iota(I32, (qb, 3 * qb), 0)
    km = lax.broadcasted_iota(I32, (qb, 3 * qb), 1)
    kpos = j * qb - qb + km
    mask1 = (jnp.abs(km - qb - qi) <= WINDOW) & (kpos >= 0) & (kpos < n_blk * qb)
    mask = jnp.concatenate([mask1] * GQA, axis=0)
    for kh in range(N_KV):
        half = (kh % 2) * HEAD_DIM
        k_loc = jnp.concatenate([kp[kh // 2][:, half:half + HEAD_DIM] for kp in k_parts], axis=0)
        v_loc = jnp.concatenate([vr[:, kh * HEAD_DIM:(kh + 1) * HEAD_DIM]
                                 for vr in (v0_ref, v1_ref, v2_ref)], axis=0)
        qs, sinks = [], []
        for g in range(GQA):
            h = kh * GQA + g
            qs.append(q_parts[h // 2][:, (h % 2) * HEAD_DIM:(h % 2 + 1) * HEAD_DIM])
            sinks.append(jnp.broadcast_to(sink_ref[0:1, h:h + 1], (qb, 1)))
        q = jnp.concatenate(qs, axis=0)
        sink = jnp.concatenate(sinks, axis=0)
        s_loc = jnp.where(mask, _nt_dot(q, k_loc) * ATTN_SCALE, NEG_INF)
        s_ctx = _nt_dot(q, ck_ref[0, kh]) * ATTN_SCALE
        o = _softmax_pv([(s_loc, v_loc), (s_ctx, cv_ref[0, kh])], sink)
        for g in range(GQA):
            h = kh * GQA + g
            o_ref[:, h * HEAD_DIM:(h + 1) * HEAD_DIM] = o[g * qb:(g + 1) * qb]


def _rope_tables(s_len):
    rows = s_len // GRID_W
    row = jnp.repeat(jnp.arange(rows), GRID_W).astype(F32)
    col = jnp.tile(jnp.arange(GRID_W), rows).astype(F32)
    nf = HEAD_DIM // 4
    inv = ROPE_BASE ** (-jnp.arange(nf, dtype=F32) / nf)
    ang = jnp.concatenate([row[:, None] * inv, col[:, None] * inv], axis=-1)
    cos, sin = jnp.cos(ang), jnp.sin(ang)
    cos2 = jnp.tile(jnp.concatenate([cos, cos], axis=-1), (1, 2))
    sin2 = jnp.tile(jnp.concatenate([-sin, sin], axis=-1), (1, 2))
    return cos2, sin2


def attn_latent(lay, qkv, cache_k, cache_v, sink):
    nq = N_HEADS * HEAD_DIM
    nkv = N_KV * HEAD_DIM
    n_blk = lay.s_lat // Q_BLOCK
    base = lay.t_ctx // Q_BLOCK
    n_ctx_keys = cache_k.shape[2]
    cos2, sin2 = _rope_tables(lay.s_lat)
    kcol = nq // nkv

    def qrow(b, j):
        return base + b * n_blk + j

    def krow(off):
        return lambda b, j: base + b * n_blk + jnp.clip(j + off, 0, n_blk - 1)

    def trow(off):
        return lambda b, j: (jnp.clip(j + off, 0, n_blk - 1), 0)

    kv_spec = lambda off, col: pl.BlockSpec((Q_BLOCK, nkv), lambda b, j: (krow(off)(b, j), col))
    tab = lambda off: pl.BlockSpec((Q_BLOCK, 2 * HEAD_DIM), trow(off))
    cache_spec = pl.BlockSpec((1, N_KV, n_ctx_keys, HEAD_DIM), lambda b, j: (b, 0, 0, 0))
    return pl.pallas_call(
        functools.partial(_attn_lat_kernel, n_blk=n_blk),
        grid=(lay.n_lat, n_blk),
        in_specs=[pl.BlockSpec((Q_BLOCK, nq), lambda b, j: (qrow(b, j), 0)),
                  kv_spec(-1, kcol), kv_spec(0, kcol), kv_spec(1, kcol),
                  kv_spec(-1, kcol + 1), kv_spec(0, kcol + 1), kv_spec(1, kcol + 1),
                  cache_spec, cache_spec,
                  tab(0), tab(0), tab(-1), tab(0), tab(1), tab(-1), tab(0), tab(1),
                  pl.BlockSpec((1, N_HEADS), lambda b, j: (0, 0))],
        out_specs=pl.BlockSpec((Q_BLOCK, nq), lambda b, j: (b * n_blk + j, 0)),
        out_shape=jax.ShapeDtypeStruct((lay.t_lat, nq), F32),
        compiler_params=_cparams(("arbitrary", "arbitrary")),
        name="attn_latent",
    )(qkv, qkv, qkv, qkv, qkv, qkv, qkv, cache_k, cache_v,
      cos2, sin2, cos2, cos2, cos2, sin2, sin2, sin2, sink.reshape(1, -1))


def _mm_res_kernel(x_ref, ac_ref, al_ref, w_ref, mod_ref, o_ref, *, n_ctx_tiles):
    a = jnp.where(pl.program_id(0) < n_ctx_tiles, ac_ref[...], al_ref[...])
    out = jnp.dot(a.astype(BF16), w_ref[...], preferred_element_type=F32)
    o_ref[...] = x_ref[...] + mod_ref[0, 2:3, :] * out


def matmul_residual(lay, x, a_ctx, a_lat, w_bf16, mods):
    mi = _mod_index(lay, ROW_TILE)
    k = a_ctx.shape[1]
    nct = lay.t_ctx // ROW_TILE
    return pl.pallas_call(
        functools.partial(_mm_res_kernel, n_ctx_tiles=nct),
        grid=(lay.t // ROW_TILE,),
        in_specs=[pl.BlockSpec((ROW_TILE, D_MODEL), lambda i: (i, 0)),
                  pl.BlockSpec((ROW_TILE, k), lambda i: (jnp.minimum(i, nct - 1), 0)),
                  pl.BlockSpec((ROW_TILE, k), lambda i: (jnp.maximum(i - nct, 0), 0)),
                  pl.BlockSpec((k, D_MODEL), lambda i: (0, 0)),
                  pl.BlockSpec((1, N_MOD, D_MODEL), lambda i: (mi(i), 0, 0))],
        out_specs=pl.BlockSpec((ROW_TILE, D_MODEL), lambda i: (i, 0)),
        out_shape=jax.ShapeDtypeStruct((lay.t, D_MODEL), F32),
        compiler_params=_cparams(("arbitrary",)),
        name="matmul_residual",
    )(x, a_ctx, a_lat, w_bf16, mods)


def _rowtile_load(ref, n, base=0):
    return jnp.concatenate([ref[pl.ds(base + c, n, stride=ROW_CHUNKS), :]
                            for c in range(ROW_CHUNKS)], axis=1)


def _rowtile_store(ref, val, n):
    for c in range(ROW_CHUNKS):
        ref[pl.ds(c, n, stride=ROW_CHUNKS), :] = val[:, c * LANES:(c + 1) * LANES]


def _row_copy(src, src_row, dst, dst_row, sem):
    return pltpu.make_async_copy(
        src.at[pl.ds(pl.multiple_of(src_row * ROW_CHUNKS, ROW_CHUNKS), ROW_CHUNKS), :],
        dst.at[pl.ds(pl.multiple_of(dst_row * ROW_CHUNKS, ROW_CHUNKS), ROW_CHUNKS), :], sem)


def _router_kernel(x_ref, g_ref, mod_ref, rwt_ref, rb_ref, tri_ref,
                   hn_ref, eidx_ref, wts_ref, rank_ref, cnt_ref, cnt_s):
    tm = ROUTER_TILE

    @pl.when(pl.program_id(0) == 0)
    def _():
        cnt_s[...] = jnp.zeros_like(cnt_s)

    h = _modnorm(x_ref[...], g_ref[...], mod_ref, 3)
    _rowtile_store(hn_ref, h, tm)
    logits = lax.dot_general(rwt_ref[...], h, (((1,), (1,)), ((), ())),
                             precision=HIGHEST, preferred_element_type=F32)
    scores = jax.nn.sigmoid(logits)
    choice = scores + rb_ref[...]
    gs_rows = []
    for g in range(N_GROUPS):
        cg = choice[g * GROUP_SIZE:(g + 1) * GROUP_SIZE, :]
        m1 = jnp.max(cg, axis=0, keepdims=True)
        eq = cg == m1
        cnt = jnp.sum(eq.astype(F32), axis=0, keepdims=True)
        m2 = jnp.max(jnp.where(eq, -jnp.inf, cg), axis=0, keepdims=True)
        gs_rows.append(m1 + jnp.where(cnt >= 2.0, m1, m2))
    gs = jnp.concatenate(gs_rows, axis=0)
    gi = lax.broadcasted_iota(I32, (N_GROUPS, tm), 0)
    grank = jnp.zeros((N_GROUPS, tm), I32)
    for g in range(N_GROUPS):
        other = gs[g:g + 1, :]
        ahead = (other > gs) | ((other == gs) & (g < gi))
        grank = grank + ahead.astype(I32)
    gsel = grank < TOPK_GROUPS
    emask = jnp.concatenate(
        [jnp.broadcast_to(gsel[g:g + 1, :], (GROUP_SIZE, tm)) for g in range(N_GROUPS)], axis=0)
    masked = jnp.where(emask, choice, -jnp.inf)
    ei = lax.broadcasted_iota(I32, (N_EXPERTS, tm), 0)
    idxs, ws = [], []
    member = jnp.zeros((N_EXPERTS, tm), F32)
    for _ in range(TOP_K):
        m = jnp.max(masked, axis=0, keepdims=True)
        idx = jnp.min(jnp.where(masked == m, ei, N_EXPERTS), axis=0, keepdims=True)
        hit = ei == idx
        ws.append(jnp.sum(jnp.where(hit, scores, 0.0), axis=0, keepdims=True))
        idxs.append(idx)
        member = jnp.where(hit, 1.0, member)
        masked = jnp.where(hit, -jnp.inf, masked)
    w = jnp.concatenate(ws, axis=0)
    wts_ref[...] = w / jnp.sum(w, axis=0, keepdims=True) * ROUTE_SCALE
    eidx_ref[...] = jnp.concatenate(idxs, axis=0)
    before = jnp.dot(member.astype(BF16), tri_ref[...], preferred_element_type=F32) + cnt_s[...]
    ranks = [jnp.sum(jnp.where(ei == idx, before, 0.0), axis=0, keepdims=True) for idx in idxs]
    rank_ref[...] = jnp.concatenate(ranks, axis=0).astype(I32)
    cnt_s[...] = cnt_s[...] + jnp.sum(member, axis=1, keepdims=True)
    cnt_ref[...] = jnp.broadcast_to(cnt_s[...], cnt_ref.shape)


def moe_router(lay, x, g, mods, router_w, router_b):
    t = lay.t
    tm = ROUTER_TILE
    mi = _mod_index(lay, tm)
    tri = (jnp.arange(tm)[:, None] < jnp.arange(tm)[None, :]).astype(BF16)
    tok = lambda i: (0, i)
    const = lambda i: (0, 0)
    return pl.pallas_call(
        _router_kernel,
        grid=(t // tm,),
        in_specs=[pl.BlockSpec((tm, D_MODEL), lambda i: (i, 0)),
                  pl.BlockSpec((1, D_MODEL), const),
                  pl.BlockSpec((1, N_MOD, D_MODEL), lambda i: (mi(i), 0, 0)),
                  pl.BlockSpec((N_EXPERTS, D_MODEL), const),
                  pl.BlockSpec((N_EXPERTS, 1), const),
                  pl.BlockSpec((tm, tm), const)],
        out_specs=[pl.BlockSpec((tm * ROW_CHUNKS, LANES), lambda i: (i, 0)),
                   pl.BlockSpec((TOP_K, tm), tok),
                   pl.BlockSpec((TOP_K, tm), tok),
                   pl.BlockSpec((TOP_K, tm), tok),
                   pl.BlockSpec((N_EXPERTS, LANES), const)],
        out_shape=[jax.ShapeDtypeStruct((t * ROW_CHUNKS, LANES), F32),
                   jax.ShapeDtypeStruct((TOP_K, t), I32),
                   jax.ShapeDtypeStruct((TOP_K, t), F32),
                   jax.ShapeDtypeStruct((TOP_K, t), I32),
                   jax.ShapeDtypeStruct((N_EXPERTS, LANES), F32)],
        scratch_shapes=[pltpu.VMEM((N_EXPERTS, 1), F32)],
        compiler_params=_cparams(("arbitrary",)),
        name="moe_router",
    )(x, g.reshape(1, -1), mods, router_w.T, router_b.reshape(-1, 1), tri)


def _dest_kernel(start_ref, eidx_ref, rank_ref, dest_ref):
    e = eidx_ref[...]

    def body(i, acc):
        return jnp.where(e == i, start_ref[i], acc)

    dest_ref[...] = lax.fori_loop(0, N_EXPERTS, body, jnp.zeros_like(e), unroll=8) + rank_ref[...]


def moe_dest(pad_start, eidx, rank):
    t = eidx.shape[1]
    tn = DEST_TILE
    spec = pl.BlockSpec((TOP_K, tn), lambda i, ps: (0, i))
    return pl.pallas_call(
        _dest_kernel,
        grid_spec=pltpu.PrefetchScalarGridSpec(
            num_scalar_prefetch=1, grid=(t // tn,), in_specs=[spec, spec], out_specs=spec),
        out_shape=jax.ShapeDtypeStruct((TOP_K, t), I32),
        compiler_params=_cparams(("arbitrary",)),
        name="moe_dest",
    )(pad_start, eidx, rank)


def _issue_row_copies(idx_at, n, copy_at):
    def body(r, c):
        copy_at(r, idx_at(r)).start()
        return c
    lax.fori_loop(0, n, body, 0, unroll=8)


def _dispatch_kernel(zrow_ref, dest_hbm, hn_ref, xs_hbm, idx_s, zbuf, isem, zsem, ssem, *, n_tiles):
    i = pl.program_id(0)
    slot = i % 2
    td = DISPATCH_TILE

    def idx_copy(tile, s):
        return pltpu.make_async_copy(dest_hbm.at[:, pl.ds(tile * td, td)], idx_s.at[s], isem.at[s])

    def zero_copy(e):
        r0 = pl.multiple_of(zrow_ref[e] * ROW_CHUNKS, ROW_CHUNKS)
        return pltpu.make_async_copy(zbuf, xs_hbm.at[pl.ds(r0, MOE_BLOCK * ROW_CHUNKS), :], zsem)

    @pl.when(i == 0)
    def _():
        zbuf[...] = jnp.zeros_like(zbuf)

        def zstart(e, c):
            @pl.when(zrow_ref[e] >= 0)
            def _():
                zero_copy(e).start()
            return c

        def zwait(e, c):
            @pl.when(zrow_ref[e] >= 0)
            def _():
                zero_copy(e).wait()
            return c

        lax.fori_loop(0, zrow_ref.shape[0], zstart, 0)
        idx_copy(0, 0).start()
        lax.fori_loop(0, zrow_ref.shape[0], zwait, 0)

    idx_copy(i, slot).wait()

    @pl.when(i + 1 < n_tiles)
    def _():
        idx_copy(i + 1, 1 - slot).start()

    for k in range(TOP_K):
        _issue_row_copies(lambda r: idx_s[slot, k, r], td,
                          lambda r, d: _row_copy(hn_ref, r, xs_hbm, d, ssem))
    for k in range(TOP_K):
        pltpu.make_async_copy(hn_ref, xs_hbm.at[pl.ds(0, td * ROW_CHUNKS), :], ssem).wait()


def moe_dispatch(lay, hn, dest, zero_row, n_rows):
    td = DISPATCH_TILE
    n_tiles = lay.t // td
    return pl.pallas_call(
        functools.partial(_dispatch_kernel, n_tiles=n_tiles),
        grid_spec=pltpu.PrefetchScalarGridSpec(
            num_scalar_prefetch=1,
            grid=(n_tiles,),
            in_specs=[pl.BlockSpec(memory_space=pl.ANY),
                      pl.BlockSpec((td * ROW_CHUNKS, LANES), lambda i, z: (i, 0))],
            out_specs=pl.BlockSpec(memory_space=pl.ANY),
            scratch_shapes=[pltpu.SMEM((2, TOP_K, td), I32),
                            pltpu.VMEM((MOE_BLOCK * ROW_CHUNKS, LANES), F32),
                            pltpu.SemaphoreType.DMA((2,)),
                            pltpu.SemaphoreType.DMA,
                            pltpu.SemaphoreType.DMA]),
        out_shape=jax.ShapeDtypeStruct((n_rows * ROW_CHUNKS, LANES), F32),
        compiler_params=_cparams(("arbitrary",)),
        name="moe_dispatch",
    )(zero_row, dest, hn)


def _expert_kernel(start_ref, nblk_ref, tail_ref, xs_hbm, wg_ref, wu_ref, wd_ref, y_hbm,
                   xbuf, ybuf, wg_s, wu_s, wd_s, isem, osem):
    e = pl.program_id(0)
    nb = nblk_ref[e]
    row0 = start_ref[e]
    blk_rows = MOE_BLOCK * ROW_CHUNKS

    def block_rows(j):
        return pl.ds(pl.multiple_of((row0 + j * MOE_BLOCK) * ROW_CHUNKS, blk_rows), blk_rows)

    def fetch(j, s):
        return pltpu.make_async_copy(xs_hbm.at[block_rows(j), :], xbuf.at[s], isem.at[s])

    def writeback(j, s):
        return pltpu.make_async_copy(ybuf.at[s], y_hbm.at[block_rows(j), :], osem.at[s])

    @pl.when(nb > 0)
    def _():
        fetch(0, 0).start()
        wg_s[...] = wg_ref[0].astype(BF16)
        wu_s[...] = wu_ref[0].astype(BF16)
        wd_s[...] = wd_ref[0].astype(BF16)

    def body(j, c):
        s = j % 2

        @pl.when(j + 1 < nb)
        def _():
            fetch(j + 1, 1 - s).start()

        fetch(j, s).wait()
        x = _rowtile_load(xbuf.at[s], MOE_BLOCK).astype(BF16)
        gate = jnp.dot(x, wg_s[...], preferred_element_type=F32)
        up = jnp.dot(x, wu_s[...], preferred_element_type=F32)
        act = gate * jax.nn.sigmoid(gate) * up
        y = jnp.dot(act.astype(BF16), wd_s[...], preferred_element_type=F32)

        @pl.when(j >= 2)
        def _():
            writeback(j - 2, s).wait()

        _rowtile_store(ybuf.at[s], y, MOE_BLOCK)
        writeback(j, s).start()
        return c

    lax.fori_loop(0, nb, body, 0)

    @pl.when(nb >= 2)
    def _():
        writeback(nb - 2, nb % 2).wait()

    @pl.when(nb >= 1)
    def _():
        writeback(nb - 1, (nb - 1) % 2).wait()

    @pl.when(e == N_EXPERTS - 1)
    def _():
        ybuf[0] = jnp.zeros(ybuf.shape[1:], F32)

        def tail_copy(i):
            r0 = pl.multiple_of(tail_ref[i] * ROW_CHUNKS, blk_rows)
            return pltpu.make_async_copy(ybuf.at[0], y_hbm.at[pl.ds(r0, blk_rows), :], osem.at[0])

        def tstart(i, c):
            @pl.when(tail_ref[i] >= 0)
            def _():
                tail_copy(i).start()
            return c

        def twait(i, c):
            @pl.when(tail_ref[i] >= 0)
            def _():
                tail_copy(i).wait()
            return c

        lax.fori_loop(0, tail_ref.shape[0], tstart, 0)
        lax.fori_loop(0, tail_ref.shape[0], twait, 0)


def moe_experts(xs, pad_start, n_blocks, tail_row, w_gate, w_up, w_down):
    wspec = lambda shape: pl.BlockSpec((1,) + shape, lambda e, a, b, c: (e, 0, 0))
    blk = (MOE_BLOCK * ROW_CHUNKS, LANES)
    return pl.pallas_call(
        _expert_kernel,
        grid_spec=pltpu.PrefetchScalarGridSpec(
            num_scalar_prefetch=3,
            grid=(N_EXPERTS,),
            in_specs=[pl.BlockSpec(memory_space=pl.ANY),
                      wspec((D_MODEL, D_EXPERT)), wspec((D_MODEL, D_EXPERT)),
                      wspec((D_EXPERT, D_MODEL))],
            out_specs=pl.BlockSpec(memory_space=pl.ANY),
            scratch_shapes=[pltpu.VMEM((2,) + blk, F32),
                            pltpu.VMEM((2,) + blk, F32),
                            pltpu.VMEM((D_MODEL, D_EXPERT), BF16),
                            pltpu.VMEM((D_MODEL, D_EXPERT), BF16),
                            pltpu.VMEM((D_EXPERT, D_MODEL), BF16),
                            pltpu.SemaphoreType.DMA((2,)),
                            pltpu.SemaphoreType.DMA((2,))]),
        out_shape=jax.ShapeDtypeStruct(xs.shape, F32),
        compiler_params=_cparams(("arbitrary",)),
        name="moe_experts",
    )(pad_start, n_blocks, tail_row, xs, w_gate, w_up, w_down)


def _combine_kernel(dest_hbm, y_hbm, x_ref, hn_ref, w_ref, sg_ref, su_ref, sd_ref, mod_ref, o_ref,
                    idx_s, ybuf, isem, gsem, *, n_tiles):
    i = pl.program_id(0)
    slot = i % 2
    tm = COMBINE_TILE

    def idx_copy(tile, s):
        return pltpu.make_async_copy(dest_hbm.at[:, pl.ds(tile * tm, tm)], idx_s.at[s], isem.at[s])

    def gather(s):
        for k in range(TOP_K):
            _issue_row_copies(lambda r: idx_s[s, k, r], tm,
                              lambda r, d: _row_copy(y_hbm, d, ybuf.at[s], k * tm + r, gsem.at[s]))

    @pl.when(i == 0)
    def _():
        c = idx_copy(0, 0)
        c.start()
        c.wait()
        gather(0)
        if n_tiles > 1:
            idx_copy(1, 1).start()

    @pl.when(i + 1 < n_tiles)
    def _():
        idx_copy(i + 1, 1 - slot).wait()
        gather(1 - slot)

    @pl.when(i + 2 < n_tiles)
    def _():
        idx_copy(i + 2, slot).start()

    h = _rowtile_load(hn_ref, tm).astype(BF16)
    sgate = jnp.dot(h, sg_ref[...], preferred_element_type=F32)
    sup = jnp.dot(h, su_ref[...], preferred_element_type=F32)
    shared = jnp.dot((sgate * jax.nn.sigmoid(sgate) * sup).astype(BF16), sd_ref[...],
                     preferred_element_type=F32)
    pltpu.make_async_copy(y_hbm.at[pl.ds(0, TOP_K * tm * ROW_CHUNKS), :], ybuf.at[slot],
                          gsem.at[slot]).wait()
    w = w_ref[...]
    routed = jnp.zeros((tm, D_MODEL), F32)
    for k in range(TOP_K):
        routed = routed + w[:, k:k + 1] * _rowtile_load(ybuf.at[slot], tm, base=k * tm * ROW_CHUNKS)
    o_ref[...] = x_ref[...] + mod_ref[0, 5:6, :] * (routed + shared)


def moe_combine(lay, x, hn, y_rows, dest, wts_t, sg_bf16, su_bf16, sd_bf16, mods):
    tm = COMBINE_TILE
    n_tiles = lay.t // tm
    mi = _mod_index(lay, tm)
    row = lambda i: (i, 0)
    const = lambda i: (0, 0)
    return pl.pallas_call(
        functools.partial(_combine_kernel, n_tiles=n_tiles),
        grid=(n_tiles,),
        in_specs=[pl.BlockSpec(memory_space=pl.ANY),
                  pl.BlockSpec(memory_space=pl.ANY),
                  pl.BlockSpec((tm, D_MODEL), row),
                  pl.BlockSpec((tm * ROW_CHUNKS, LANES), row),
                  pl.BlockSpec((tm, TOP_K), row),
                  pl.BlockSpec((D_MODEL, D_EXPERT), const),
                  pl.BlockSpec((D_MODEL, D_EXPERT), const),
                  pl.BlockSpec((D_EXPERT, D_MODEL), const),
                  pl.BlockSpec((1, N_MOD, D_MODEL), lambda i: (mi(i), 0, 0))],
        out_specs=pl.BlockSpec((tm, D_MODEL), row),
        out_shape=jax.ShapeDtypeStruct((lay.t, D_MODEL), F32),
        scratch_shapes=[pltpu.SMEM((2, TOP_K, tm), I32),
                        pltpu.VMEM((2, TOP_K * tm * ROW_CHUNKS, LANES), F32),
                        pltpu.SemaphoreType.DMA((2,)),
                        pltpu.SemaphoreType.DMA((2,))],
        compiler_params=_cparams(("arbitrary",)),
        name="moe_combine",
    )(dest, y_rows, x, hn, wts_t, sg_bf16, su_bf16, sd_bf16, mods)


def moe_layer(lay, x, g, mods, router_w, router_b, w_gate, w_up, w_down, s_gate, s_up, s_down):
    t = lay.t
    hn, eidx, wts, rank, cnt = moe_router(lay, x, g, mods, router_w, router_b)
    counts = cnt[:, 0].astype(I32)
    n_blocks = (counts + MOE_BLOCK - 1) // MOE_BLOCK
    padded = n_blocks * MOE_BLOCK
    pad_end = jnp.cumsum(padded)
    pad_start = pad_end - padded
    n_rows = -(-(t * TOP_K + N_EXPERTS * (MOE_BLOCK - 1)) // MOE_BLOCK) * MOE_BLOCK
    dest = moe_dest(pad_start, eidx, rank)
    last_row = jnp.where(n_blocks > 0, pad_end - MOE_BLOCK, -1)
    tail_blk = pad_end[-1] // MOE_BLOCK + jnp.arange(n_rows // MOE_BLOCK - t * TOP_K // MOE_BLOCK)
    tail_row = jnp.where(tail_blk < n_rows // MOE_BLOCK, tail_blk * MOE_BLOCK, -1).astype(I32)
    xs = moe_dispatch(lay, hn, dest, jnp.concatenate([last_row, tail_row]), n_rows)
    y_rows = moe_experts(xs, pad_start, n_blocks, tail_row, w_gate, w_up, w_down)
    return moe_combine(lay, x, hn, y_rows, dest, wts.T,
                       s_gate.astype(BF16), s_up.astype(BF16), s_down.astype(BF16), mods)


def _final_norm_kernel(x_ref, g_ref, o_ref):
    x = x_ref[...]
    ms = jnp.mean(x * x, axis=-1, keepdims=True)
    o_ref[...] = x * lax.rsqrt(ms + EPS) * g_ref[...]


def final_norm(x, g, row0, n_rows):
    base = row0 // ROW_TILE
    return pl.pallas_call(
        _final_norm_kernel,
        grid=(n_rows // ROW_TILE,),
        in_specs=[pl.BlockSpec((ROW_TILE, D_MODEL), lambda i: (base + i, 0)),
                  pl.BlockSpec((1, D_MODEL), lambda i: (0, 0))],
        out_specs=pl.BlockSpec((ROW_TILE, D_MODEL), lambda i: (i, 0)),
        out_shape=jax.ShapeDtypeStruct((n_rows, D_MODEL), F32),
        compiler_params=_cparams(("arbitrary",)),
        name="final_norm",
    )(x, g.reshape(1, -1))


def _block_diag(w):
    nb, bw, _ = w.shape
    eye = jnp.eye(nb, dtype=w.dtype)
    return (eye[:, None, :, None] * w[:, :, None, :]).reshape(nb * bw, nb * bw)


def even_layer(lay, x, mods, g_mix, p, state_lru, state_ssm_re, state_ssm_im):
    t = lay.t
    proj = modnorm_matmul(lay, x, g_mix, mods, 0, p['w_in'].astype(BF16))
    zeros_c = jnp.zeros((lay.n_ctx, D_LRU), F32)
    hf_y, st = None, []
    for d in range(2):
        wg = jnp.concatenate([_block_diag(p['lru_wa'][d]), _block_diag(p['lru_wx'][d])], axis=1)
        bg = jnp.concatenate([p['lru_ba'][d], p['lru_bx'][d]])
        h0 = jnp.concatenate([zeros_c, state_lru[:, d].astype(F32)], axis=0)
        hf_y, s = lru_pass(lay, proj, p['conv_w'], p['conv_b'], wg.astype(BF16), bg,
                           p['lru_lam'][d], h0, reverse=(d == 1), hf=hf_y)
        st.append(s[:lay.n_ctx])
    y_a = hf_y
    new_lru = jnp.stack(st, axis=1)

    mats = _s5_matrices(p['a_re'], p['a_im'], p['log_dt'], p['b_re'], p['b_im'], p['c_re'], p['c_im'])
    rows = t // S5_CHUNK
    u_g = proj[:, 2 * D_LRU:].reshape(rows, S5_CHUNK, SSM_GROUPS, SSM_GROUP)
    u_g = u_g.transpose(2, 0, 1, 3).reshape(SSM_GROUPS, rows, S5_LANES)
    h0 = jnp.concatenate([state_ssm_re, state_ssm_im], axis=-1).astype(F32)
    h0 = h0.transpose(2, 1, 0, 3)
    y_g, h_ctx = s5_mixer(lay, u_g, mats, h0)
    y_t = y_g.reshape(SSM_GROUPS, rows, S5_CHUNK, SSM_GROUP).transpose(1, 2, 0, 3).reshape(t, D_SSM)
    seg = lay.s_ctx // S5_CHUNK
    h_ctx = h_ctx.reshape(SSM_GROUPS, 2, lay.n_ctx, seg, 2 * SSM_STATE)
    ends = jnp.stack([h_ctx[:, 0, :, seg - 1], h_ctx[:, 1, :, 0]], axis=1)
    ends = ends.transpose(2, 1, 0, 3)
    x = even_out(lay, x, y_a, y_t, proj, p['d'], p['glu_w'].astype(BF16), p['glu_b'],
                 p['w_out'].astype(BF16), mods)
    return x, new_lru, ends[..., :SSM_STATE], ends[..., SSM_STATE:]


def odd_layer(lay, x, mods, g_mix, w_qkv, sink, w_out, cache_k, cache_v):
    qkv = modnorm_matmul(lay, x, g_mix, mods, 0, w_qkv.astype(BF16))
    o_ctx = attn_context(lay, qkv, sink)
    o_lat = attn_latent(lay, qkv, cache_k, cache_v, sink)
    nq = N_HEADS * HEAD_DIM
    kv = qkv[:lay.t_ctx, nq:].reshape(lay.n_ctx, lay.s_ctx, 2, N_KV, HEAD_DIM)
    k_new = kv[:, :, 0].swapaxes(1, 2)
    v_new = kv[:, :, 1].swapaxes(1, 2)
    x = matmul_residual(lay, x, o_ctx, o_lat, w_out.astype(BF16), mods)
    return x, k_new, v_new


def _forward(lay, x_prompt, x_sample, state_lru, state_ssm_re, state_ssm_im, cache_k, cache_v,
             c, c_ctx, g_mix, g_ffn, w_mod, b_mod,
             ev_w_in, lru_conv_w, lru_conv_b, lru_wa, lru_ba, lru_wx, lru_bx, lru_lam,
             ssm_a_re, ssm_a_im, ssm_log_dt, ssm_b_re, ssm_b_im, ssm_c_re, ssm_c_im, ssm_d,
             ssm_glu_w, ssm_glu_b, ev_w_out, at_w_qkv, at_sink, at_w_out,
             router_w, router_b, exp_w_gate, exp_w_up, exp_w_down, sh_w_gate, sh_w_up, sh_w_down,
             g_final):
    depth = g_mix.shape[0]
    x = jnp.concatenate([x_prompt.reshape(lay.t_ctx, D_MODEL), x_sample.reshape(lay.t_lat, D_MODEL)],
                        axis=0)
    n_c = 1 + lay.n_lat
    c_rows = jnp.concatenate([c_ctx[None, :], c, jnp.zeros((16 - n_c, D_MODEL), F32)], axis=0)
    new_lru, new_re, new_im, new_k, new_v = [], [], [], [], []
    for l in range(depth):
        i = l // 2
        mods = adaln_table(c_rows, w_mod[l], b_mod[l])
        if l % 2 == 0:
            p = dict(w_in=ev_w_in[i], conv_w=lru_conv_w[i], conv_b=lru_conv_b[i],
                     lru_wa=lru_wa[i], lru_ba=lru_ba[i], lru_wx=lru_wx[i], lru_bx=lru_bx[i],
                     lru_lam=lru_lam[i], a_re=ssm_a_re[i], a_im=ssm_a_im[i], log_dt=ssm_log_dt[i],
                     b_re=ssm_b_re[i], b_im=ssm_b_im[i], c_re=ssm_c_re[i], c_im=ssm_c_im[i],
                     d=ssm_d[i], glu_w=ssm_glu_w[i], glu_b=ssm_glu_b[i], w_out=ev_w_out[i])
            x, lru_i, re_i, im_i = even_layer(lay, x, mods, g_mix[l], p, state_lru[:, i],
                                              state_ssm_re[:, i], state_ssm_im[:, i])
            new_lru.append(lru_i)
            new_re.append(re_i)
            new_im.append(im_i)
        else:
            x, k_i, v_i = odd_layer(lay, x, mods, g_mix[l], at_w_qkv[i], at_sink[i], at_w_out[i],
                                    cache_k[:, i], cache_v[:, i])
            new_k.append(k_i)
            new_v.append(v_i)
        x = moe_layer(lay, x, g_ffn[l], mods, router_w[l], router_b[l], exp_w_gate[l], exp_w_up[l],
                      exp_w_down[l], sh_w_gate[l], sh_w_up[l], sh_w_down[l])
    y_prompt = final_norm(x, g_final, 0, lay.t_ctx).reshape(x_prompt.shape)
    y_sample = final_norm(x, g_final, lay.t_ctx, lay.t_lat).reshape(x_sample.shape)
    return (y_prompt, y_sample, jnp.stack(new_lru, axis=1), jnp.stack(new_re, axis=1),
            jnp.stack(new_im, axis=1), jnp.stack(new_k, axis=1), jnp.stack(new_v, axis=1))


def kernel(x_prompt, x_sample, state_lru, state_ssm_re, state_ssm_im, cache_k, cache_v, c, c_ctx, g_mix, g_ffn, w_mod, b_mod, ev_w_in, lru_conv_w, lru_conv_b, lru_wa, lru_ba, lru_wx, lru_bx, lru_lam, ssm_a_re, ssm_a_im, ssm_log_dt, ssm_b_re, ssm_b_im, ssm_c_re, ssm_c_im, ssm_d, ssm_glu_w, ssm_glu_b, ev_w_out, at_w_qkv, at_sink, at_w_out, router_w, router_b, exp_w_gate, exp_w_up, exp_w_down, sh_w_gate, sh_w_up, sh_w_down, g_final):
    lay = Layout(n_ctx=x_prompt.shape[0], s_ctx=x_prompt.shape[1],
                 n_lat=x_sample.shape[0], s_lat=x_sample.shape[1])
    return _forward(lay, x_prompt, x_sample, state_lru, state_ssm_re, state_ssm_im, cache_k, cache_v,
                    c, c_ctx, g_mix, g_ffn, w_mod, b_mod,
                    ev_w_in, lru_conv_w, lru_conv_b, lru_wa, lru_ba, lru_wx, lru_bx, lru_lam,
                    ssm_a_re, ssm_a_im, ssm_log_dt, ssm_b_re, ssm_b_im, ssm_c_re, ssm_c_im, ssm_d,
                    ssm_glu_w, ssm_glu_b, ev_w_out, at_w_qkv, at_sink, at_w_out,
                    router_w, router_b, exp_w_gate, exp_w_up, exp_w_down, sh_w_gate, sh_w_up,
                    sh_w_down, g_final)
```

```python
import functools
from typing import NamedTuple

import jax
import jax.numpy as jnp
from jax import lax
from jax.experimental import pallas as pl
from jax.experimental.pallas import tpu as pltpu

F32 = jnp.float32
BF16 = jnp.bfloat16
I32 = jnp.int32
HIGHEST = lax.Precision.HIGHEST

D_MODEL = 1024
EPS = 1e-6
N_MOD = 6
GRID_W = 64
D_LRU = 512
LRU_BLOCKS = 8
LRU_C = 8.0
CONV_W = 4
CONV_LEFT = 2
D_SSM = 512
SSM_GROUP = 16
SSM_GROUPS = 32
SSM_STATE = 64
S5_CHUNK = 16
S5_LANES = S5_CHUNK * SSM_GROUP
S5_SCAN_STEPS = 8
HEAD_DIM = 64
N_HEADS = 16
N_KV = 4
GQA = 4
WINDOW = 128
Q_BLOCK = 128
ROPE_BASE = 10000.0
ATTN_SCALE = HEAD_DIM ** -0.5
NEG_INF = -1e30
N_EXPERTS = 256
TOP_K = 8
N_GROUPS = 8
TOPK_GROUPS = 4
GROUP_SIZE = N_EXPERTS // N_GROUPS
D_EXPERT = 256
ROUTE_SCALE = 2.5
MOE_BLOCK = 128

SUBLANES = 8
LANES = 128
ROW_CHUNKS = D_MODEL // LANES
SEQ_TILE = 256
ROW_TILE = 512
ROUTER_TILE = 512
DEST_TILE = 1024
DISPATCH_TILE = 256
COMBINE_TILE = 128
EXPERT_X_BUFS = 4
EXPERT_Y_BUFS = 2
VMEM_LIMIT = 56 * 1024 * 1024


class Layout(NamedTuple):
    n_ctx: int
    s_ctx: int
    n_lat: int
    s_lat: int

    @property
    def t_ctx(self):
        return self.n_ctx * self.s_ctx

    @property
    def t_lat(self):
        return self.n_lat * self.s_lat

    @property
    def t(self):
        return self.t_ctx + self.t_lat

    @property
    def n_seq(self):
        return self.n_ctx + self.n_lat


def _cparams(sem):
    return pltpu.CompilerParams(dimension_semantics=sem, vmem_limit_bytes=VMEM_LIMIT)


def _mod_index(lay, tile_rows):
    n_ctx_tiles = lay.t_ctx // tile_rows
    per_lat = lay.s_lat // tile_rows

    def f(i):
        return jnp.where(i < n_ctx_tiles, 0, 1 + (i - n_ctx_tiles) // per_lat)
    return f


def _adaln_kernel(c_ref, w_ref, b_ref, o_ref):
    c = c_ref[...]
    s = c * jax.nn.sigmoid(c)
    o_ref[...] = jnp.dot(s, w_ref[0], precision=HIGHEST, preferred_element_type=F32) + b_ref[...]


def adaln_table(c_rows, layer, w_mod, b_mod):
    n = c_rows.shape[0]
    tn = 1536
    out = pl.pallas_call(
        _adaln_kernel,
        grid=(N_MOD * D_MODEL // tn,),
        in_specs=[pl.BlockSpec((n, D_MODEL), lambda j: (0, 0)),
                  pl.BlockSpec((1, D_MODEL, tn), lambda j: (layer, 0, j)),
                  pl.BlockSpec((1, tn), lambda j: (0, j))],
        out_specs=pl.BlockSpec((n, tn), lambda j: (0, j)),
        out_shape=jax.ShapeDtypeStruct((n, N_MOD * D_MODEL), F32),
        compiler_params=_cparams(("arbitrary",)),
        name="adaln",
    )(c_rows, w_mod, b_mod.reshape(1, -1))
    return out.reshape(n, N_MOD, D_MODEL)


def _modnorm(x, g, mod_ref, slot):
    ms = jnp.mean(x * x, axis=-1, keepdims=True)
    y = x * lax.rsqrt(ms + EPS) * g
    shift = mod_ref[0, slot:slot + 1, :]
    scale = mod_ref[0, slot + 1:slot + 2, :]
    return y * (1.0 + scale) + shift


GROUPS_PER_VREG = LANES // SSM_GROUP
SSM_COL_BLOCKS = D_SSM // LANES


def _group_major_store(val, tmp_ref, dst_ref):
    rows = dst_ref.shape[1]
    for j in range(SSM_COL_BLOCKS):
        tmp_ref[j] = val[:, j * LANES:(j + 1) * LANES]
    for j in range(SSM_COL_BLOCKS):
        steps = [tmp_ref[j, pl.ds(i, rows, stride=S5_CHUNK), :] for i in range(S5_CHUNK)]
        for q in range(GROUPS_PER_VREG):
            dst_ref[j * GROUPS_PER_VREG + q] = jnp.concatenate(
                [w[:, q * SSM_GROUP:(q + 1) * SSM_GROUP] for w in steps], axis=1)


def _group_major_load(src_ref, tmp_ref):
    rows = src_ref.shape[1]
    for j in range(SSM_COL_BLOCKS):
        blocks = [src_ref[j * GROUPS_PER_VREG + q] for q in range(GROUPS_PER_VREG)]
        for i in range(S5_CHUNK):
            tmp_ref[j, pl.ds(i, rows, stride=S5_CHUNK), :] = jnp.concatenate(
                [b[:, i * SSM_GROUP:(i + 1) * SSM_GROUP] for b in blocks], axis=1)
    return jnp.concatenate([tmp_ref[j] for j in range(SSM_COL_BLOCKS)], axis=1)


def _modnorm_mm_kernel(x_ref, g_ref, mod_ref, w_ref, o_ref, *ug_refs, slot, ug_col):
    h = _modnorm(x_ref[...], g_ref[...], mod_ref, slot)
    out = jnp.dot(h.astype(BF16), w_ref[...], preferred_element_type=F32)
    o_ref[...] = out
    if ug_col is not None:
        ug_ref, tmp_ref = ug_refs
        _group_major_store(out[:, ug_col:ug_col + D_SSM], tmp_ref, ug_ref)


def modnorm_matmul(lay, x, g, mods, slot, w_bf16, ug_col=None):
    t = lay.t
    n = w_bf16.shape[1]
    mi = _mod_index(lay, ROW_TILE)
    out_specs = [pl.BlockSpec((ROW_TILE, n), lambda i: (i, 0))]
    out_shape = [jax.ShapeDtypeStruct((t, n), F32)]
    if ug_col is not None:
        out_specs.append(pl.BlockSpec((SSM_GROUPS, ROW_TILE // S5_CHUNK, S5_LANES), lambda i: (0, i, 0)))
        out_shape.append(jax.ShapeDtypeStruct((SSM_GROUPS, t // S5_CHUNK, S5_LANES), F32))
    outs = pl.pallas_call(
        functools.partial(_modnorm_mm_kernel, slot=slot, ug_col=ug_col),
        grid=(t // ROW_TILE,),
        in_specs=[pl.BlockSpec((ROW_TILE, D_MODEL), lambda i: (i, 0)),
                  pl.BlockSpec((1, D_MODEL), lambda i: (0, 0)),
                  pl.BlockSpec((1, N_MOD, D_MODEL), lambda i: (mi(i), 0, 0)),
                  pl.BlockSpec((D_MODEL, n), lambda i: (0, 0))],
        out_specs=out_specs,
        out_shape=out_shape,
        scratch_shapes=([pltpu.VMEM((SSM_COL_BLOCKS, ROW_TILE, LANES), F32)]
                        if ug_col is not None else []),
        compiler_params=_cparams(("arbitrary",)),
        name="modnorm_matmul",
    )(x, g.reshape(1, -1), mods, w_bf16)
    return outs if ug_col is not None else outs[0]


def _seq_tile_maps(lay, reverse):
    assert lay.s_ctx == SEQ_TILE and lay.s_lat % SEQ_TILE == 0
    n_tiles = lay.t // SEQ_TILE
    per_lat = lay.s_lat // SEQ_TILE

    def tile(i):
        return (n_tiles - 1 - i) if reverse else i

    def seq(i):
        ti = tile(i)
        return jnp.where(ti < lay.n_ctx, ti, lay.n_ctx + (ti - lay.n_ctx) // per_lat)

    return n_tiles, tile, seq


def _softplus(x):
    return jnp.maximum(x, 0.0) + jnp.log(1.0 + jnp.exp(-jnp.abs(x)))


def _lru_kernel(rec_ref, prev_ref, next_ref, cw_ref, cb_ref, wg_ref, bg_ref, lam_ref, h0_ref,
                *rest, reverse, n_ctx, per_lat, n_tiles):
    if reverse:
        gate_ref, hf_ref, y_ref, st_ref, a_s, b_s, h_s, carry = rest
    else:
        y_ref, st_ref, a_s, b_s, h_s, carry = rest
    i = pl.program_id(0)
    ti = (n_tiles - 1 - i) if reverse else i
    is_first = jnp.logical_or(ti < n_ctx, (ti - n_ctx) % per_lat == 0)
    is_last = jnp.logical_or(ti < n_ctx, (ti - n_ctx) % per_lat == per_lat - 1)
    ts = SEQ_TILE

    rec = rec_ref[...]
    prev = jnp.where(is_first, 0.0, prev_ref[...])
    nxt = jnp.where(is_last, 0.0, next_ref[...])
    ext = jnp.concatenate([prev, rec, nxt], axis=0)
    n_ext = ts + 2 * SUBLANES
    cw = cw_ref[...]
    xc = cb_ref[...] + cw[2:3, :] * rec
    xc = xc + cw[0:1, :] * pltpu.roll(ext, 2, 0)[SUBLANES:SUBLANES + ts]
    xc = xc + cw[1:2, :] * pltpu.roll(ext, 1, 0)[SUBLANES:SUBLANES + ts]
    xc = xc + cw[3:4, :] * pltpu.roll(ext, n_ext - 1, 0)[SUBLANES:SUBLANES + ts]

    gates = jax.nn.sigmoid(jnp.dot(xc.astype(BF16), wg_ref[...], preferred_element_type=F32)
                           + bg_ref[...])
    r = gates[:, :D_LRU]
    ig = gates[:, D_LRU:]
    log_a = (-LRU_C) * r * _softplus(-lam_ref[...])
    a = jnp.exp(log_a)
    b = jnp.sqrt(1.0 - jnp.exp(2.0 * log_a)) * (ig * xc)

    row8 = lax.broadcasted_iota(I32, (ts, D_LRU), 0) % SUBLANES
    for sh in (1, 2, 4):
        if reverse:
            keep = row8 < SUBLANES - sh
            a_sh = pltpu.roll(a, ts - sh, 0)
            b_sh = pltpu.roll(b, ts - sh, 0)
        else:
            keep = row8 >= sh
            a_sh = pltpu.roll(a, sh, 0)
            b_sh = pltpu.roll(b, sh, 0)
        b = b + a * jnp.where(keep, b_sh, 0.0)
        a = a * jnp.where(keep, a_sh, 1.0)
    a_s[...] = a
    b_s[...] = b

    @pl.when(is_last if reverse else is_first)
    def _():
        carry[...] = h0_ref[0]

    n_grp = ts // SUBLANES

    def body(k, c):
        gi = (n_grp - 1 - k) if reverse else k
        sl = pl.ds(pl.multiple_of(gi * SUBLANES, SUBLANES), SUBLANES)
        h = b_s[sl, :] + a_s[sl, :] * c
        h_s[sl, :] = h
        return h[0:1, :] if reverse else h[SUBLANES - 1:SUBLANES, :]

    c_fin = lax.fori_loop(0, n_grp, body, carry[...], unroll=4)
    carry[...] = c_fin
    st_ref[0] = c_fin
    if reverse:
        y_ref[...] = (hf_ref[...] + h_s[...]) * jax.nn.gelu(gate_ref[...])
    else:
        y_ref[...] = h_s[...]


def lru_pass(lay, proj, conv_w, conv_b, wg_bf16, bg, lam, h0, reverse, hf=None):
    n_tiles, tile, seq = _seq_tile_maps(lay, reverse)
    per_lat = lay.s_lat // SEQ_TILE
    blk8 = SEQ_TILE // SUBLANES
    last8 = lay.t // SUBLANES - 1
    c = D_LRU
    in_specs = [
        pl.BlockSpec((SEQ_TILE, c), lambda i: (tile(i), 1)),
        pl.BlockSpec((SUBLANES, c), lambda i: (jnp.maximum(tile(i) * blk8 - 1, 0), 1)),
        pl.BlockSpec((SUBLANES, c), lambda i: (jnp.minimum(tile(i) * blk8 + blk8, last8), 1)),
        pl.BlockSpec((CONV_W, c), lambda i: (0, 0)),
        pl.BlockSpec((1, c), lambda i: (0, 0)),
        pl.BlockSpec((c, 2 * c), lambda i: (0, 0)),
        pl.BlockSpec((1, 2 * c), lambda i: (0, 0)),
        pl.BlockSpec((1, c), lambda i: (0, 0)),
        pl.BlockSpec((1, 1, c), lambda i: (seq(i), 0, 0)),
    ]
    args = [proj, proj, proj, conv_w, conv_b.reshape(1, -1), wg_bf16, bg.reshape(1, -1),
            lam.reshape(1, -1), h0.reshape(lay.n_seq, 1, c)]
    if reverse:
        in_specs += [pl.BlockSpec((SEQ_TILE, c), lambda i: (tile(i), 0)),
                     pl.BlockSpec((SEQ_TILE, c), lambda i: (tile(i), 0))]
        args += [proj, hf]
    y, st = pl.pallas_call(
        functools.partial(_lru_kernel, reverse=reverse, n_ctx=lay.n_ctx, per_lat=per_lat,
                          n_tiles=n_tiles),
        grid=(n_tiles,),
        in_specs=in_specs,
        out_specs=[pl.BlockSpec((SEQ_TILE, c), lambda i: (tile(i), 0)),
                   pl.BlockSpec((1, 1, c), lambda i: (seq(i), 0, 0))],
        out_shape=[jax.ShapeDtypeStruct((lay.t, c), F32),
                   jax.ShapeDtypeStruct((lay.n_seq, 1, c), F32)],
        scratch_shapes=[pltpu.VMEM((SEQ_TILE, c), F32), pltpu.VMEM((SEQ_TILE, c), F32),
                        pltpu.VMEM((SEQ_TILE, c), F32), pltpu.VMEM((1, c), F32)],
        compiler_params=_cparams(("arbitrary",)),
        name="lru_bwd" if reverse else "lru_fwd",
    )(*args)
    return y, st.reshape(lay.n_seq, c)


def _cmul(a, b):
    return a[0] * b[0] - a[1] * b[1], a[0] * b[1] + a[1] * b[0]


def _s5_matrices(a_re, a_im, log_dt, b_re, b_im, c_re, c_im):
    a_re, a_im = a_re.astype(F32), a_im.astype(F32)
    dt = jnp.exp(log_dt.astype(F32))[..., None]
    z = (a_re * dt, a_im * dt)

    def zpow(k):
        k = k.reshape((-1,) + (1,) * z[0].ndim)
        mag = jnp.exp(k * z[0][None])
        return mag * jnp.cos(k * z[1][None]), mag * jnp.sin(k * z[1][None])

    a_bar = zpow(jnp.ones((1,), F32))
    a_bar = (a_bar[0][0], a_bar[1][0])
    den = a_re * a_re + a_im * a_im
    xr, xi = a_bar[0] - 1.0, a_bar[1]
    q = ((xr * a_re + xi * a_im) / den, (xi * a_re - xr * a_im) / den)
    b_bar = _cmul((q[0][..., None], q[1][..., None]), (b_re.astype(F32), b_im.astype(F32)))
    cc = (c_re.astype(F32), c_im.astype(F32))
    el = S5_CHUNK
    pw = zpow(jnp.arange(el + 1, dtype=F32))
    idx = jnp.arange(el)
    m_in, m_toep, m_out = [], [], []
    for d in range(2):
        p_d = (pw[0][:, d], pw[1][:, d])
        b_d = (b_bar[0][d], b_bar[1][d])
        c_d = (cc[0][d], cc[1][d])
        k_in = (el - 1 - idx) if d == 0 else idx
        w_in = _cmul((p_d[0][k_in][..., None], p_d[1][k_in][..., None]),
                     (b_d[0][None], b_d[1][None]))
        w_in = [jnp.transpose(w, (1, 0, 3, 2)).reshape(SSM_GROUPS, S5_LANES, SSM_STATE) for w in w_in]
        m_in.append(jnp.concatenate(w_in, axis=-1))
        cp = _cmul((c_d[0][None], c_d[1][None]),
                   (p_d[0][:, :, None, :], p_d[1][:, :, None, :]))
        kern = (jnp.einsum('kghp,gpc->kgch', cp[0][:el], b_d[0])
                - jnp.einsum('kghp,gpc->kgch', cp[1][:el], b_d[1]))
        diff = (idx[None, :] - idx[:, None]) if d == 0 else (idx[:, None] - idx[None, :])
        blocks = jnp.where((diff >= 0)[:, :, None, None, None],
                           kern[jnp.clip(diff, 0, el - 1)], 0.0)
        m_toep.append(jnp.transpose(blocks, (2, 0, 3, 1, 4)).reshape(SSM_GROUPS, S5_LANES, S5_LANES))
        k_out = (idx + 1) if d == 0 else (el - idx)
        w_out = [jnp.transpose(w[k_out], (1, 3, 0, 2)).reshape(SSM_GROUPS, SSM_STATE, S5_LANES)
                 for w in cp]
        m_out.append(jnp.concatenate([w_out[0], -w_out[1]], axis=1))
    mul = zpow(el * 2.0 ** jnp.arange(S5_SCAN_STEPS, dtype=F32))
    mul = [jnp.transpose(m, (2, 1, 0, 3)) for m in mul]
    coef_a = jnp.concatenate([mul[0], mul[0]], axis=-1)
    coef_b = jnp.concatenate([-mul[1], mul[1]], axis=-1)
    stack = lambda xs: jnp.stack(xs, axis=1)
    return (stack(m_in).astype(BF16), stack(m_toep).astype(BF16), stack(m_out).astype(BF16),
            coef_a, coef_b)


def _s5_scan(v, ca, cb, seg, reverse):
    n = v.shape[0]
    assert seg <= 2 ** S5_SCAN_STEPS
    row = lax.broadcasted_iota(I32, (n, 2 * SSM_STATE), 0) % seg
    k, sh = 0, 1
    while sh < seg:
        if reverse:
            s = jnp.where(row < seg - sh, pltpu.roll(v, n - sh, 0), 0.0)
        else:
            s = jnp.where(row >= sh, pltpu.roll(v, sh, 0), 0.0)
        v = v + ca[k:k + 1, :] * s + cb[k:k + 1, :] * pltpu.roll(s, SSM_STATE, 1)
        k += 1
        sh *= 2
    return v


def _s5_shift(h, seg, reverse):
    n = h.shape[0]
    row = lax.broadcasted_iota(I32, (n, 2 * SSM_STATE), 0) % seg
    if reverse:
        return jnp.where(row < seg - 1, pltpu.roll(h, n - 1, 0), 0.0)
    return jnp.where(row >= 1, pltpu.roll(h, 1, 0), 0.0)


def _s5_kernel(u_ref, min_ref, mtoep_ref, mout_ref, ca_ref, cb_ref, h0_ref, y_ref, hc_ref,
               v_s, hp_s, *, rc, seg_c, n_lat, seg_l):
    u = u_ref[0].astype(BF16)
    u_c, u_l = u[:rc], u[rc:]
    y_c = jnp.zeros((rc, S5_LANES), F32)
    y_l = jnp.zeros((n_lat * seg_l, S5_LANES), F32)
    for d in range(2):
        reverse = d == 1
        ca = ca_ref[0, d]
        cb = cb_ref[0, d]
        m_in = min_ref[0, d]
        m_toep = mtoep_ref[0, d]
        m_out = mout_ref[0, d]
        h_c = _s5_scan(jnp.dot(u_c, m_in, preferred_element_type=F32), ca, cb, seg_c, reverse)
        hc_ref[0, d] = h_c
        hp_c = _s5_shift(h_c, seg_c, reverse)
        y_c = y_c + jnp.dot(u_c, m_toep, preferred_element_type=F32)
        y_c = y_c + jnp.dot(hp_c.astype(BF16), m_out, preferred_element_type=F32)
        v_s[...] = jnp.dot(u_l, m_in, preferred_element_type=F32)
        for s in range(n_lat):
            h0 = h0_ref[0, d, s:s + 1, :]
            r0 = s * seg_l + (seg_l - 1 if reverse else 0)
            v_s[r0:r0 + 1, :] = (v_s[r0:r0 + 1, :] + ca[0:1, :] * h0
                                 + cb[0:1, :] * pltpu.roll(h0, SSM_STATE, 1))
        h_l = _s5_scan(v_s[...], ca, cb, seg_l, reverse)
        hp_s[...] = _s5_shift(h_l, seg_l, reverse)
        for s in range(n_lat):
            r0 = s * seg_l + (seg_l - 1 if reverse else 0)
            hp_s[r0:r0 + 1, :] = h0_ref[0, d, s:s + 1, :]
        y_l = y_l + jnp.dot(u_l, m_toep, preferred_element_type=F32)
        y_l = y_l + jnp.dot(hp_s[...].astype(BF16), m_out, preferred_element_type=F32)
    y_ref[0, :rc, :] = y_c
    y_ref[0, rc:, :] = y_l


def s5_mixer(lay, u_g, mats, h0):
    m_in, m_toep, m_out, coef_a, coef_b = mats
    rows = lay.t // S5_CHUNK
    rc = lay.t_ctx // S5_CHUNK
    rl = rows - rc
    st2 = 2 * SSM_STATE
    g4 = lambda g: (g, 0, 0, 0)
    return pl.pallas_call(
        functools.partial(_s5_kernel, rc=rc, seg_c=lay.s_ctx // S5_CHUNK, n_lat=lay.n_lat,
                          seg_l=lay.s_lat // S5_CHUNK),
        grid=(SSM_GROUPS,),
        in_specs=[pl.BlockSpec((1, rows, S5_LANES), lambda g: (g, 0, 0)),
                  pl.BlockSpec((1, 2, S5_LANES, st2), g4),
                  pl.BlockSpec((1, 2, S5_LANES, S5_LANES), g4),
                  pl.BlockSpec((1, 2, st2, S5_LANES), g4),
                  pl.BlockSpec((1, 2, S5_SCAN_STEPS, st2), g4),
                  pl.BlockSpec((1, 2, S5_SCAN_STEPS, st2), g4),
                  pl.BlockSpec((1, 2, lay.n_lat, st2), g4)],
        out_specs=[pl.BlockSpec((1, rows, S5_LANES), lambda g: (g, 0, 0)),
                   pl.BlockSpec((1, 2, rc, st2), g4)],
        out_shape=[jax.ShapeDtypeStruct((SSM_GROUPS, rows, S5_LANES), F32),
                   jax.ShapeDtypeStruct((SSM_GROUPS, 2, rc, st2), F32)],
        scratch_shapes=[pltpu.VMEM((rl, st2), F32), pltpu.VMEM((rl, st2), F32)],
        compiler_params=_cparams(("arbitrary",)),
        name="s5_mixer",
    )(u_g, m_in, m_toep, m_out, coef_a, coef_b, h0)


def _even_out_kernel(x_ref, ya_ref, yg_ref, u_ref, d_ref, gw_ref, gb_ref, w_ref, mod_ref, o_ref, yt_s):
    ys = _group_major_load(yg_ref, yt_s) + d_ref[...] * u_ref[...]
    g = jax.nn.gelu(ys)
    yb = g * jax.nn.sigmoid(jnp.dot(g.astype(BF16), gw_ref[...], preferred_element_type=F32)
                            + gb_ref[...])
    out = jnp.dot(ya_ref[...].astype(BF16), w_ref[:D_LRU, :], preferred_element_type=F32)
    out = out + jnp.dot(yb.astype(BF16), w_ref[D_LRU:, :], preferred_element_type=F32)
    o_ref[...] = x_ref[...] + mod_ref[0, 2:3, :] * out


def even_out(lay, x, y_a, y_g, proj, ssm_d, glu_w_bf16, glu_b, w_out_bf16, mods):
    mi = _mod_index(lay, ROW_TILE)
    c = D_SSM
    row = lambda i: (i, 0)
    const = lambda i: (0, 0)
    return pl.pallas_call(
        _even_out_kernel,
        grid=(lay.t // ROW_TILE,),
        in_specs=[pl.BlockSpec((ROW_TILE, D_MODEL), row),
                  pl.BlockSpec((ROW_TILE, c), row),
                  pl.BlockSpec((SSM_GROUPS, ROW_TILE // S5_CHUNK, S5_LANES), lambda i: (0, i, 0)),
                  pl.BlockSpec((ROW_TILE, c), lambda i: (i, 2)),
                  pl.BlockSpec((1, c), const),
                  pl.BlockSpec((c, c), const),
                  pl.BlockSpec((1, c), const),
                  pl.BlockSpec((D_MODEL, D_MODEL), const),
                  pl.BlockSpec((1, N_MOD, D_MODEL), lambda i: (mi(i), 0, 0))],
        out_specs=pl.BlockSpec((ROW_TILE, D_MODEL), row),
        out_shape=jax.ShapeDtypeStruct((lay.t, D_MODEL), F32),
        scratch_shapes=[pltpu.VMEM((SSM_COL_BLOCKS, ROW_TILE, LANES), F32)],
        compiler_params=_cparams(("arbitrary",)),
        name="even_out",
    )(x, y_a, y_g, proj, ssm_d.reshape(1, -1), glu_w_bf16, glu_b.reshape(1, -1), w_out_bf16, mods)


def _softmax_pv(parts, sink_col):
    m = sink_col
    for s, _ in parts:
        m = jnp.maximum(m, jnp.max(s, axis=-1, keepdims=True))
    den = jnp.exp(sink_col - m)
    acc = None
    for s, v in parts:
        p = jnp.exp(s - m)
        den = den + jnp.sum(p, axis=-1, keepdims=True)
        pv = jnp.dot(p.astype(BF16), v.astype(BF16), preferred_element_type=F32)
        acc = pv if acc is None else acc + pv
    return acc / den


def _nt_dot(a, b):
    return lax.dot_general(a.astype(BF16), b.astype(BF16), (((1,), (1,)), ((), ())),
                           preferred_element_type=F32)


def _attn_ctx_kernel(q_ref, k_ref, v_ref, sink_ref, o_ref):
    n = q_ref.shape[0]
    for kh in range(N_KV):
        k = k_ref[:, kh * HEAD_DIM:(kh + 1) * HEAD_DIM]
        v = v_ref[:, kh * HEAD_DIM:(kh + 1) * HEAD_DIM]
        for g in range(GQA):
            h = kh * GQA + g
            q = q_ref[:, h * HEAD_DIM:(h + 1) * HEAD_DIM]
            s = _nt_dot(q, k) * ATTN_SCALE
            sink = jnp.broadcast_to(sink_ref[0:1, h:h + 1], (n, 1))
            o_ref[:, h * HEAD_DIM:(h + 1) * HEAD_DIM] = _softmax_pv([(s, v)], sink)


def attn_context(lay, qkv, sink):
    nq = N_HEADS * HEAD_DIM
    nkv = N_KV * HEAD_DIM
    return pl.pallas_call(
        _attn_ctx_kernel,
        grid=(lay.n_ctx,),
        in_specs=[pl.BlockSpec((lay.s_ctx, nq), lambda b: (b, 0)),
                  pl.BlockSpec((lay.s_ctx, nkv), lambda b: (b, nq // nkv)),
                  pl.BlockSpec((lay.s_ctx, nkv), lambda b: (b, nq // nkv + 1)),
                  pl.BlockSpec((1, N_HEADS), lambda b: (0, 0))],
        out_specs=pl.BlockSpec((lay.s_ctx, nq), lambda b: (b, 0)),
        out_shape=jax.ShapeDtypeStruct((lay.t_ctx, nq), F32),
        compiler_params=_cparams(("arbitrary",)),
        name="attn_context",
    )(qkv, qkv, qkv, sink.reshape(1, -1))


def _rope(x, cos, sin):
    lane = lax.broadcasted_iota(I32, (x.shape[0], 2 * HEAD_DIM), 1) % HEAD_DIM
    outs = []
    for j in range(x.shape[1] // (2 * HEAD_DIM)):
        xs = x[:, j * 2 * HEAD_DIM:(j + 1) * 2 * HEAD_DIM]
        sw = jnp.where(lane < HEAD_DIM // 2,
                       pltpu.roll(xs, 2 * HEAD_DIM - HEAD_DIM // 2, 1),
                       pltpu.roll(xs, HEAD_DIM // 2, 1))
        outs.append(xs * cos + sw * sin)
    return outs


def _attn_lat_kernel(q_ref, k0_ref, k1_ref, k2_ref, v0_ref, v1_ref, v2_ref, ck_ref, cv_ref,
                     cq_ref, sq_ref, c0_ref, c1_ref, c2_ref, s0_ref, s1_ref, s2_ref, sink_ref,
                     o_ref, *, n_blk):
    j = pl.program_id(1)
    qb = Q_BLOCK
    q_parts = _rope(q_ref[...], cq_ref[...], sq_ref[...])
    k_parts = [_rope(kr[...], cr[...], sr[...])
               for kr, cr, sr in ((k0_ref, c0_ref, s0_ref), (k1_ref, c1_ref, s1_ref),
                                  (k2_ref, c2_ref, s2_ref))]
    qi = lax.broadcasted_iota(I32, (qb, 3 * qb), 0)
    km = lax.broadcasted_iota(I32, (qb, 3 * qb), 1)
    kpos = j * qb - qb + km
    mask1 = (jnp.abs(km - qb - qi) <= WINDOW) & (kpos >= 0) & (kpos < n_blk * qb)
    mask = jnp.concatenate([mask1] * GQA, axis=0)
    for kh in range(N_KV):
        half = (kh % 2) * HEAD_DIM
        k_loc = jnp.concatenate([kp[kh // 2][:, half:half + HEAD_DIM] for kp in k_parts], axis=0)
        v_loc = jnp.concatenate([vr[:, kh * HEAD_DIM:(kh + 1) * HEAD_DIM]
                                 for vr in (v0_ref, v1_ref, v2_ref)], axis=0)
        qs, sinks = [], []
        for g in range(GQA):
            h = kh * GQA + g
            qs.append(q_parts[h // 2][:, (h % 2) * HEAD_DIM:(h % 2 + 1) * HEAD_DIM])
            sinks.append(jnp.broadcast_to(sink_ref[0:1, h:h + 1], (qb, 1)))
        q = jnp.concatenate(qs, axis=0)
        sink = jnp.concatenate(sinks, axis=0)
        s_loc = jnp.where(mask, _nt_dot(q, k_loc) * ATTN_SCALE, NEG_INF)
        s_ctx = _nt_dot(q, ck_ref[0, kh]) * ATTN_SCALE
        o = _softmax_pv([(s_loc, v_loc), (s_ctx, cv_ref[0, kh])], sink)
        for g in range(GQA):
            h = kh * GQA + g
            o_ref[:, h * HEAD_DIM:(h + 1) * HEAD_DIM] = o[g * qb:(g + 1) * qb]


def _rope_tables(s_len):
    rows = s_len // GRID_W
    row = jnp.repeat(jnp.arange(rows), GRID_W).astype(F32)
    col = jnp.tile(jnp.arange(GRID_W), rows).astype(F32)
    nf = HEAD_DIM // 4
    inv = ROPE_BASE ** (-jnp.arange(nf, dtype=F32) / nf)
    ang = jnp.concatenate([row[:, None] * inv, col[:, None] * inv], axis=-1)
    cos, sin = jnp.cos(ang), jnp.sin(ang)
    cos2 = jnp.tile(jnp.concatenate([cos, cos], axis=-1), (1, 2))
    sin2 = jnp.tile(jnp.concatenate([-sin, sin], axis=-1), (1, 2))
    return cos2, sin2


def attn_latent(lay, qkv, cache_k, cache_v, sink):
    nq = N_HEADS * HEAD_DIM
    nkv = N_KV * HEAD_DIM
    n_blk = lay.s_lat // Q_BLOCK
    base = lay.t_ctx // Q_BLOCK
    n_ctx_keys = cache_k.shape[2]
    cos2, sin2 = _rope_tables(lay.s_lat)
    kcol = nq // nkv

    def qrow(b, j):
        return base + b * n_blk + j

    def krow(off):
        return lambda b, j: base + b * n_blk + jnp.clip(j + off, 0, n_blk - 1)

    def trow(off):
        return lambda b, j: (jnp.clip(j + off, 0, n_blk - 1), 0)

    kv_spec = lambda off, col: pl.BlockSpec((Q_BLOCK, nkv), lambda b, j: (krow(off)(b, j), col))
    tab = lambda off: pl.BlockSpec((Q_BLOCK, 2 * HEAD_DIM), trow(off))
    cache_spec = pl.BlockSpec((1, N_KV, n_ctx_keys, HEAD_DIM), lambda b, j: (b, 0, 0, 0))
    return pl.pallas_call(
        functools.partial(_attn_lat_kernel, n_blk=n_blk),
        grid=(lay.n_lat, n_blk),
        in_specs=[pl.BlockSpec((Q_BLOCK, nq), lambda b, j: (qrow(b, j), 0)),
                  kv_spec(-1, kcol), kv_spec(0, kcol), kv_spec(1, kcol),
                  kv_spec(-1, kcol + 1), kv_spec(0, kcol + 1), kv_spec(1, kcol + 1),
                  cache_spec, cache_spec,
                  tab(0), tab(0), tab(-1), tab(0), tab(1), tab(-1), tab(0), tab(1),
                  pl.BlockSpec((1, N_HEADS), lambda b, j: (0, 0))],
        out_specs=pl.BlockSpec((Q_BLOCK, nq), lambda b, j: (b * n_blk + j, 0)),
        out_shape=jax.ShapeDtypeStruct((lay.t_lat, nq), F32),
        compiler_params=_cparams(("arbitrary", "arbitrary")),
        name="attn_latent",
    )(qkv, qkv, qkv, qkv, qkv, qkv, qkv, cache_k, cache_v,
      cos2, sin2, cos2, cos2, cos2, sin2, sin2, sin2, sink.reshape(1, -1))


def _mm_res_kernel(x_ref, ac_ref, al_ref, w_ref, mod_ref, o_ref, *, n_ctx_tiles):
    a = jnp.where(pl.program_id(0) < n_ctx_tiles, ac_ref[...], al_ref[...])
    out = jnp.dot(a.astype(BF16), w_ref[...], preferred_element_type=F32)
    o_ref[...] = x_ref[...] + mod_ref[0, 2:3, :] * out


def matmul_residual(lay, x, a_ctx, a_lat, w_bf16, mods):
    mi = _mod_index(lay, ROW_TILE)
    k = a_ctx.shape[1]
    nct = lay.t_ctx // ROW_TILE
    return pl.pallas_call(
        functools.partial(_mm_res_kernel, n_ctx_tiles=nct),
        grid=(lay.t // ROW_TILE,),
        in_specs=[pl.BlockSpec((ROW_TILE, D_MODEL), lambda i: (i, 0)),
                  pl.BlockSpec((ROW_TILE, k), lambda i: (jnp.minimum(i, nct - 1), 0)),
                  pl.BlockSpec((ROW_TILE, k), lambda i: (jnp.maximum(i - nct, 0), 0)),
                  pl.BlockSpec((k, D_MODEL), lambda i: (0, 0)),
                  pl.BlockSpec((1, N_MOD, D_MODEL), lambda i: (mi(i), 0, 0))],
        out_specs=pl.BlockSpec((ROW_TILE, D_MODEL), lambda i: (i, 0)),
        out_shape=jax.ShapeDtypeStruct((lay.t, D_MODEL), F32),
        compiler_params=_cparams(("arbitrary",)),
        name="matmul_residual",
    )(x, a_ctx, a_lat, w_bf16, mods)


def _rowtile_load(ref, n, base=0):
    return jnp.concatenate([ref[pl.ds(base + c, n, stride=ROW_CHUNKS), :]
                            for c in range(ROW_CHUNKS)], axis=1)


def _rowtile_store(ref, val, n):
    for c in range(ROW_CHUNKS):
        ref[pl.ds(c, n, stride=ROW_CHUNKS), :] = val[:, c * LANES:(c + 1) * LANES]


def _row_copy(src, src_row, dst, dst_row, sem):
    return pltpu.make_async_copy(
        src.at[pl.ds(pl.multiple_of(src_row * ROW_CHUNKS, ROW_CHUNKS), ROW_CHUNKS), :],
        dst.at[pl.ds(pl.multiple_of(dst_row * ROW_CHUNKS, ROW_CHUNKS), ROW_CHUNKS), :], sem)


def _router_kernel(x_ref, g_ref, mod_ref, rwt_ref, rb_ref, tri_ref,
                   hn_ref, eidx_ref, wts_ref, rank_ref, cnt_ref, cnt_s):
    tm = ROUTER_TILE

    @pl.when(pl.program_id(0) == 0)
    def _():
        cnt_s[...] = jnp.zeros_like(cnt_s)

    h = _modnorm(x_ref[...], g_ref[...], mod_ref, 3)
    _rowtile_store(hn_ref, h, tm)
    logits = lax.dot_general(rwt_ref[...], h, (((1,), (1,)), ((), ())),
                             precision=HIGHEST, preferred_element_type=F32)
    scores = jax.nn.sigmoid(logits)
    choice = scores + rb_ref[...]
    gs_rows = []
    for g in range(N_GROUPS):
        cg = choice[g * GROUP_SIZE:(g + 1) * GROUP_SIZE, :]
        m1 = jnp.max(cg, axis=0, keepdims=True)
        eq = cg == m1
        cnt = jnp.sum(eq.astype(F32), axis=0, keepdims=True)
        m2 = jnp.max(jnp.where(eq, -jnp.inf, cg), axis=0, keepdims=True)
        gs_rows.append(m1 + jnp.where(cnt >= 2.0, m1, m2))
    gs = jnp.concatenate(gs_rows, axis=0)
    gi = lax.broadcasted_iota(I32, (N_GROUPS, tm), 0)
    grank = jnp.zeros((N_GROUPS, tm), I32)
    for g in range(N_GROUPS):
        other = gs[g:g + 1, :]
        ahead = (other > gs) | ((other == gs) & (g < gi))
        grank = grank + ahead.astype(I32)
    gsel = grank < TOPK_GROUPS
    emask = jnp.concatenate(
        [jnp.broadcast_to(gsel[g:g + 1, :], (GROUP_SIZE, tm)) for g in range(N_GROUPS)], axis=0)
    masked = jnp.where(emask, choice, -jnp.inf)
    ei = lax.broadcasted_iota(I32, (N_EXPERTS, tm), 0)
    idxs, ws = [], []
    member = jnp.zeros((N_EXPERTS, tm), F32)
    for _ in range(TOP_K):
        m = jnp.max(masked, axis=0, keepdims=True)
        idx = jnp.min(jnp.where(masked == m, ei, N_EXPERTS), axis=0, keepdims=True)
        hit = ei == idx
        ws.append(jnp.sum(jnp.where(hit, scores, 0.0), axis=0, keepdims=True))
        idxs.append(idx)
        member = jnp.where(hit, 1.0, member)
        masked = jnp.where(hit, -jnp.inf, masked)
    w = jnp.concatenate(ws, axis=0)
    wts_ref[...] = w / jnp.sum(w, axis=0, keepdims=True) * ROUTE_SCALE
    eidx_ref[...] = jnp.concatenate(idxs, axis=0)
    before = jnp.dot(member.astype(BF16), tri_ref[...], preferred_element_type=F32) + cnt_s[...]
    ranks = [jnp.sum(jnp.where(ei == idx, before, 0.0), axis=0, keepdims=True) for idx in idxs]
    rank_ref[...] = jnp.concatenate(ranks, axis=0).astype(I32)
    cnt_s[...] = cnt_s[...] + jnp.sum(member, axis=1, keepdims=True)
    cnt_ref[...] = jnp.broadcast_to(cnt_s[...], cnt_ref.shape)


def moe_router(lay, x, g, mods, router_w, router_b):
    t = lay.t
    tm = ROUTER_TILE
    mi = _mod_index(lay, tm)
    tri = (jnp.arange(tm)[:, None] < jnp.arange(tm)[None, :]).astype(BF16)
    tok = lambda i: (0, i)
    const = lambda i: (0, 0)
    return pl.pallas_call(
        _router_kernel,
        grid=(t // tm,),
        in_specs=[pl.BlockSpec((tm, D_MODEL), lambda i: (i, 0)),
                  pl.BlockSpec((1, D_MODEL), const),
                  pl.BlockSpec((1, N_MOD, D_MODEL), lambda i: (mi(i), 0, 0)),
                  pl.BlockSpec((N_EXPERTS, D_MODEL), const),
                  pl.BlockSpec((N_EXPERTS, 1), const),
                  pl.BlockSpec((tm, tm), const)],
        out_specs=[pl.BlockSpec((tm * ROW_CHUNKS, LANES), lambda i: (i, 0)),
                   pl.BlockSpec((TOP_K, tm), tok),
                   pl.BlockSpec((TOP_K, tm), tok),
                   pl.BlockSpec((TOP_K, tm), tok),
                   pl.BlockSpec((N_EXPERTS, LANES), const)],
        out_shape=[jax.ShapeDtypeStruct((t * ROW_CHUNKS, LANES), F32),
                   jax.ShapeDtypeStruct((TOP_K, t), I32),
                   jax.ShapeDtypeStruct((TOP_K, t), F32),
                   jax.ShapeDtypeStruct((TOP_K, t), I32),
                   jax.ShapeDtypeStruct((N_EXPERTS, LANES), F32)],
        scratch_shapes=[pltpu.VMEM((N_EXPERTS, 1), F32)],
        compiler_params=_cparams(("arbitrary",)),
        name="moe_router",
    )(x, g.reshape(1, -1), mods, router_w.T, router_b.reshape(-1, 1), tri)


def _dest_kernel(start_ref, eidx_ref, rank_ref, dest_ref):
    e = eidx_ref[...]

    def body(i, acc):
        return jnp.where(e == i, start_ref[i], acc)

    dest_ref[...] = lax.fori_loop(0, N_EXPERTS, body, jnp.zeros_like(e), unroll=8) + rank_ref[...]


def moe_dest(pad_start, eidx, rank):
    t = eidx.shape[1]
    tn = DEST_TILE
    spec = pl.BlockSpec((TOP_K, tn), lambda i, ps: (0, i))
    return pl.pallas_call(
        _dest_kernel,
        grid_spec=pltpu.PrefetchScalarGridSpec(
            num_scalar_prefetch=1, grid=(t // tn,), in_specs=[spec, spec], out_specs=spec),
        out_shape=jax.ShapeDtypeStruct((TOP_K, t), I32),
        compiler_params=_cparams(("arbitrary",)),
        name="moe_dest",
    )(pad_start, eidx, rank)


def _issue_row_copies(idx_at, n, copy_at):
    def body(r, c):
        copy_at(r, idx_at(r)).start()
        return c
    lax.fori_loop(0, n, body, 0, unroll=8)


def _dispatch_kernel(zrow_ref, dest_hbm, hn_ref, xs_hbm, idx_s, zbuf, isem, zsem, ssem, *, n_tiles):
    i = pl.program_id(0)
    slot = i % 2
    td = DISPATCH_TILE

    def idx_copy(tile, s):
        return pltpu.make_async_copy(dest_hbm.at[:, pl.ds(tile * td, td)], idx_s.at[s], isem.at[s])

    def zero_copy(e):
        r0 = pl.multiple_of(zrow_ref[e] * ROW_CHUNKS, ROW_CHUNKS)
        return pltpu.make_async_copy(zbuf, xs_hbm.at[pl.ds(r0, MOE_BLOCK * ROW_CHUNKS), :], zsem)

    @pl.when(i == 0)
    def _():
        zbuf[...] = jnp.zeros_like(zbuf)

        def zstart(e, c):
            @pl.when(zrow_ref[e] >= 0)
            def _():
                zero_copy(e).start()
            return c

        def zwait(e, c):
            @pl.when(zrow_ref[e] >= 0)
            def _():
                zero_copy(e).wait()
            return c

        lax.fori_loop(0, zrow_ref.shape[0], zstart, 0)
        idx_copy(0, 0).start()
        lax.fori_loop(0, zrow_ref.shape[0], zwait, 0)

    idx_copy(i, slot).wait()

    @pl.when(i + 1 < n_tiles)
    def _():
        idx_copy(i + 1, 1 - slot).start()

    for k in range(TOP_K):
        _issue_row_copies(lambda r: idx_s[slot, k, r], td,
                          lambda r, d: _row_copy(hn_ref, r, xs_hbm, d, ssem))
    for k in range(TOP_K):
        pltpu.make_async_copy(hn_ref, xs_hbm.at[pl.ds(0, td * ROW_CHUNKS), :], ssem).wait()


def moe_dispatch(lay, hn, dest, zero_row, n_rows):
    td = DISPATCH_TILE
    n_tiles = lay.t // td
    return pl.pallas_call(
        functools.partial(_dispatch_kernel, n_tiles=n_tiles),
        grid_spec=pltpu.PrefetchScalarGridSpec(
            num_scalar_prefetch=1,
            grid=(n_tiles,),
            in_specs=[pl.BlockSpec(memory_space=pl.ANY),
                      pl.BlockSpec((td * ROW_CHUNKS, LANES), lambda i, z: (i, 0))],
            out_specs=pl.BlockSpec(memory_space=pl.ANY),
            scratch_shapes=[pltpu.SMEM((2, TOP_K, td), I32),
                            pltpu.VMEM((MOE_BLOCK * ROW_CHUNKS, LANES), F32),
                            pltpu.SemaphoreType.DMA((2,)),
                            pltpu.SemaphoreType.DMA,
                            pltpu.SemaphoreType.DMA]),
        out_shape=jax.ShapeDtypeStruct((n_rows * ROW_CHUNKS, LANES), F32),
        compiler_params=_cparams(("arbitrary",)),
        name="moe_dispatch",
    )(zero_row, dest, hn)


def _expert_kernel(blk0_ref, nblk_ref, tail_ref, xs_hbm, wg_ref, wu_ref, wd_ref, y_hbm,
                   xbuf, ybuf, wg_s, wu_s, wd_s, isem, osem):
    e = pl.program_id(0)
    nb = nblk_ref[e]
    g0 = blk0_ref[e]
    total = blk0_ref[N_EXPERTS - 1] + nblk_ref[N_EXPERTS - 1]
    blk_rows = MOE_BLOCK * ROW_CHUNKS
    n_x = xbuf.shape[0]
    n_y = ybuf.shape[0]

    def block_rows(g):
        return pl.ds(pl.multiple_of(g * blk_rows, blk_rows), blk_rows)

    def fetch(g):
        s = g % n_x
        return pltpu.make_async_copy(xs_hbm.at[block_rows(g), :], xbuf.at[s], isem.at[s])

    def writeback(g):
        s = g % n_y
        return pltpu.make_async_copy(ybuf.at[s], y_hbm.at[block_rows(g), :], osem.at[s])

    @pl.when(e == 0)
    def _():
        for p in range(n_x - 1):
            @pl.when(p < total)
            def _():
                fetch(p).start()

    @pl.when(nb > 0)
    def _():
        wg_s[...] = wg_ref[0, 0].astype(BF16)
        wu_s[...] = wu_ref[0, 0].astype(BF16)
        wd_s[...] = wd_ref[0, 0].astype(BF16)

    def body(j, c):
        g = g0 + j

        @pl.when(g + n_x - 1 < total)
        def _():
            fetch(g + n_x - 1).start()

        fetch(g).wait()
        x = _rowtile_load(xbuf.at[g % n_x], MOE_BLOCK).astype(BF16)
        gate = jnp.dot(x, wg_s[...], preferred_element_type=F32)
        up = jnp.dot(x, wu_s[...], preferred_element_type=F32)
        act = gate * jax.nn.sigmoid(gate) * up
        y = jnp.dot(act.astype(BF16), wd_s[...], preferred_element_type=F32)

        @pl.when(g >= n_y)
        def _():
            writeback(g - n_y).wait()

        _rowtile_store(ybuf.at[g % n_y], y, MOE_BLOCK)
        writeback(g).start()
        return c

    lax.fori_loop(0, nb, body, 0)

    @pl.when(e == N_EXPERTS - 1)
    def _():
        for p in range(n_y, 0, -1):
            @pl.when(total >= p)
            def _():
                writeback(total - p).wait()

        ybuf[0] = jnp.zeros(ybuf.shape[1:], F32)

        def tail_copy(i):
            r0 = pl.multiple_of(tail_ref[i] * ROW_CHUNKS, blk_rows)
            return pltpu.make_async_copy(ybuf.at[0], y_hbm.at[pl.ds(r0, blk_rows), :], osem.at[0])

        def tstart(i, c):
            @pl.when(tail_ref[i] >= 0)
            def _():
                tail_copy(i).start()
            return c

        def twait(i, c):
            @pl.when(tail_ref[i] >= 0)
            def _():
                tail_copy(i).wait()
            return c

        lax.fori_loop(0, tail_ref.shape[0], tstart, 0)
        lax.fori_loop(0, tail_ref.shape[0], twait, 0)


def moe_experts(xs, first_block, n_blocks, tail_row, layer, w_gate, w_up, w_down):
    wspec = lambda shape: pl.BlockSpec((1, 1) + shape, lambda e, a, b, c: (layer, e, 0, 0))
    blk = (MOE_BLOCK * ROW_CHUNKS, LANES)
    return pl.pallas_call(
        _expert_kernel,
        grid_spec=pltpu.PrefetchScalarGridSpec(
            num_scalar_prefetch=3,
            grid=(N_EXPERTS,),
            in_specs=[pl.BlockSpec(memory_space=pl.ANY),
                      wspec((D_MODEL, D_EXPERT)), wspec((D_MODEL, D_EXPERT)),
                      wspec((D_EXPERT, D_MODEL))],
            out_specs=pl.BlockSpec(memory_space=pl.ANY),
            scratch_shapes=[pltpu.VMEM((EXPERT_X_BUFS,) + blk, F32),
                            pltpu.VMEM((EXPERT_Y_BUFS,) + blk, F32),
                            pltpu.VMEM((D_MODEL, D_EXPERT), BF16),
                            pltpu.VMEM((D_MODEL, D_EXPERT), BF16),
                            pltpu.VMEM((D_EXPERT, D_MODEL), BF16),
                            pltpu.SemaphoreType.DMA((EXPERT_X_BUFS,)),
                            pltpu.SemaphoreType.DMA((EXPERT_Y_BUFS,))]),
        out_shape=jax.ShapeDtypeStruct(xs.shape, F32),
        compiler_params=_cparams(("arbitrary",)),
        name="moe_experts",
    )(first_block, n_blocks, tail_row, xs, w_gate, w_up, w_down)


def _combine_kernel(dest_hbm, y_hbm, x_ref, hn_ref, w_ref, sg_ref, su_ref, sd_ref, mod_ref, o_ref,
                    idx_s, ybuf, isem, gsem, *, n_tiles):
    i = pl.program_id(0)
    slot = i % 2
    tm = COMBINE_TILE

    def idx_copy(tile, s):
        return pltpu.make_async_copy(dest_hbm.at[:, pl.ds(tile * tm, tm)], idx_s.at[s], isem.at[s])

    def gather(s):
        for k in range(TOP_K):
            _issue_row_copies(lambda r: idx_s[s, k, r], tm,
                              lambda r, d: _row_copy(y_hbm, d, ybuf.at[s], k * tm + r, gsem.at[s]))

    @pl.when(i == 0)
    def _():
        c = idx_copy(0, 0)
        c.start()
        c.wait()
        gather(0)
        if n_tiles > 1:
            idx_copy(1, 1).start()

    @pl.when(i + 1 < n_tiles)
    def _():
        idx_copy(i + 1, 1 - slot).wait()
        gather(1 - slot)

    @pl.when(i + 2 < n_tiles)
    def _():
        idx_copy(i + 2, slot).start()

    h = _rowtile_load(hn_ref, tm).astype(BF16)
    sgate = jnp.dot(h, sg_ref[...], preferred_element_type=F32)
    sup = jnp.dot(h, su_ref[...], preferred_element_type=F32)
    shared = jnp.dot((sgate * jax.nn.sigmoid(sgate) * sup).astype(BF16), sd_ref[...],
                     preferred_element_type=F32)
    pltpu.make_async_copy(y_hbm.at[pl.ds(0, TOP_K * tm * ROW_CHUNKS), :], ybuf.at[slot],
                          gsem.at[slot]).wait()
    w = w_ref[...]
    routed = jnp.zeros((tm, D_MODEL), F32)
    for k in range(TOP_K):
        routed = routed + w[:, k:k + 1] * _rowtile_load(ybuf.at[slot], tm, base=k * tm * ROW_CHUNKS)
    o_ref[...] = x_ref[...] + mod_ref[0, 5:6, :] * (routed + shared)


def moe_combine(lay, x, hn, y_rows, dest, wts_t, sg_bf16, su_bf16, sd_bf16, mods):
    tm = COMBINE_TILE
    n_tiles = lay.t // tm
    mi = _mod_index(lay, tm)
    row = lambda i: (i, 0)
    const = lambda i: (0, 0)
    return pl.pallas_call(
        functools.partial(_combine_kernel, n_tiles=n_tiles),
        grid=(n_tiles,),
        in_specs=[pl.BlockSpec(memory_space=pl.ANY),
                  pl.BlockSpec(memory_space=pl.ANY),
                  pl.BlockSpec((tm, D_MODEL), row),
                  pl.BlockSpec((tm * ROW_CHUNKS, LANES), row),
                  pl.BlockSpec((tm, TOP_K), row),
                  pl.BlockSpec((D_MODEL, D_EXPERT), const),
                  pl.BlockSpec((D_MODEL, D_EXPERT), const),
                  pl.BlockSpec((D_EXPERT, D_MODEL), const),
                  pl.BlockSpec((1, N_MOD, D_MODEL), lambda i: (mi(i), 0, 0))],
        out_specs=pl.BlockSpec((tm, D_MODEL), row),
        out_shape=jax.ShapeDtypeStruct((lay.t, D_MODEL), F32),
        scratch_shapes=[pltpu.SMEM((2, TOP_K, tm), I32),
                        pltpu.VMEM((2, TOP_K * tm * ROW_CHUNKS, LANES), F32),
                        pltpu.SemaphoreType.DMA((2,)),
                        pltpu.SemaphoreType.DMA((2,))],
        compiler_params=_cparams(("arbitrary",)),
        name="moe_combine",
    )(dest, y_rows, x, hn, wts_t, sg_bf16, su_bf16, sd_bf16, mods)


def moe_layer(lay, x, g, mods, router_w, router_b, layer, w_gate, w_up, w_down, s_gate, s_up, s_down):
    t = lay.t
    hn, eidx, wts, rank, cnt = moe_router(lay, x, g, mods, router_w, router_b)
    counts = cnt[:, 0].astype(I32)
    n_blocks = (counts + MOE_BLOCK - 1) // MOE_BLOCK
    padded = n_blocks * MOE_BLOCK
    pad_end = jnp.cumsum(padded)
    pad_start = pad_end - padded
    n_rows = -(-(t * TOP_K + N_EXPERTS * (MOE_BLOCK - 1)) // MOE_BLOCK) * MOE_BLOCK
    dest = moe_dest(pad_start, eidx, rank)
    last_row = jnp.where(n_blocks > 0, pad_end - MOE_BLOCK, -1)
    tail_blk = pad_end[-1] // MOE_BLOCK + jnp.arange(n_rows // MOE_BLOCK - t * TOP_K // MOE_BLOCK)
    tail_row = jnp.where(tail_blk < n_rows // MOE_BLOCK, tail_blk * MOE_BLOCK, -1).astype(I32)
    xs = moe_dispatch(lay, hn, dest, jnp.concatenate([last_row, tail_row]), n_rows)
    y_rows = moe_experts(xs, pad_start // MOE_BLOCK, n_blocks, tail_row, layer, w_gate, w_up, w_down)
    return moe_combine(lay, x, hn, y_rows, dest, wts.T,
                       s_gate.astype(BF16), s_up.astype(BF16), s_down.astype(BF16), mods)


def _final_norm_kernel(x_ref, g_ref, o_ref):
    x = x_ref[...]
    ms = jnp.mean(x * x, axis=-1, keepdims=True)
    o_ref[...] = x * lax.rsqrt(ms + EPS) * g_ref[...]


def final_norm(x, g, row0, n_rows):
    base = row0 // ROW_TILE
    return pl.pallas_call(
        _final_norm_kernel,
        grid=(n_rows // ROW_TILE,),
        in_specs=[pl.BlockSpec((ROW_TILE, D_MODEL), lambda i: (base + i, 0)),
                  pl.BlockSpec((1, D_MODEL), lambda i: (0, 0))],
        out_specs=pl.BlockSpec((ROW_TILE, D_MODEL), lambda i: (i, 0)),
        out_shape=jax.ShapeDtypeStruct((n_rows, D_MODEL), F32),
        compiler_params=_cparams(("arbitrary",)),
        name="final_norm",
    )(x, g.reshape(1, -1))


def _block_diag(w):
    nb, bw, _ = w.shape
    eye = jnp.eye(nb, dtype=w.dtype)
    return (eye[:, None, :, None] * w[:, :, None, :]).reshape(nb * bw, nb * bw)


def even_layer(lay, x, mods, g_mix, p, state_lru, state_ssm_re, state_ssm_im):
    t = lay.t
    proj, u_g = modnorm_matmul(lay, x, g_mix, mods, 0, p['w_in'].astype(BF16),
                               ug_col=2 * D_LRU)
    zeros_c = jnp.zeros((lay.n_ctx, D_LRU), F32)
    hf_y, st = None, []
    for d in range(2):
        wg = jnp.concatenate([_block_diag(p['lru_wa'][d]), _block_diag(p['lru_wx'][d])], axis=1)
        bg = jnp.concatenate([p['lru_ba'][d], p['lru_bx'][d]])
        h0 = jnp.concatenate([zeros_c, state_lru[:, d].astype(F32)], axis=0)
        hf_y, s = lru_pass(lay, proj, p['conv_w'], p['conv_b'], wg.astype(BF16), bg,
                           p['lru_lam'][d], h0, reverse=(d == 1), hf=hf_y)
        st.append(s[:lay.n_ctx])
    y_a = hf_y
    new_lru = jnp.stack(st, axis=1)

    mats = _s5_matrices(p['a_re'], p['a_im'], p['log_dt'], p['b_re'], p['b_im'], p['c_re'], p['c_im'])
    h0 = jnp.concatenate([state_ssm_re, state_ssm_im], axis=-1).astype(F32)
    h0 = h0.transpose(2, 1, 0, 3)
    y_g, h_ctx = s5_mixer(lay, u_g, mats, h0)
    seg = lay.s_ctx // S5_CHUNK
    h_ctx = h_ctx.reshape(SSM_GROUPS, 2, lay.n_ctx, seg, 2 * SSM_STATE)
    ends = jnp.stack([h_ctx[:, 0, :, seg - 1], h_ctx[:, 1, :, 0]], axis=1)
    ends = ends.transpose(2, 1, 0, 3)
    x = even_out(lay, x, y_a, y_g, proj, p['d'], p['glu_w'].astype(BF16), p['glu_b'],
                 p['w_out'].astype(BF16), mods)
    return x, new_lru, ends[..., :SSM_STATE], ends[..., SSM_STATE:]


def odd_layer(lay, x, mods, g_mix, w_qkv, sink, w_out, cache_k, cache_v):
    qkv = modnorm_matmul(lay, x, g_mix, mods, 0, w_qkv.astype(BF16))
    o_ctx = attn_context(lay, qkv, sink)
    o_lat = attn_latent(lay, qkv, cache_k, cache_v, sink)
    nq = N_HEADS * HEAD_DIM
    kv = qkv[:lay.t_ctx, nq:].reshape(lay.n_ctx, lay.s_ctx, 2, N_KV, HEAD_DIM)
    k_new = kv[:, :, 0].swapaxes(1, 2)
    v_new = kv[:, :, 1].swapaxes(1, 2)
    x = matmul_residual(lay, x, o_ctx, o_lat, w_out.astype(BF16), mods)
    return x, k_new, v_new


def _forward(lay, x_prompt, x_sample, state_lru, state_ssm_re, state_ssm_im, cache_k, cache_v,
             c, c_ctx, g_mix, g_ffn, w_mod, b_mod,
             ev_w_in, lru_conv_w, lru_conv_b, lru_wa, lru_ba, lru_wx, lru_bx, lru_lam,
             ssm_a_re, ssm_a_im, ssm_log_dt, ssm_b_re, ssm_b_im, ssm_c_re, ssm_c_im, ssm_d,
             ssm_glu_w, ssm_glu_b, ev_w_out, at_w_qkv, at_sink, at_w_out,
             router_w, router_b, exp_w_gate, exp_w_up, exp_w_down, sh_w_gate, sh_w_up, sh_w_down,
             g_final):
    depth = g_mix.shape[0]
    x = jnp.concatenate([x_prompt.reshape(lay.t_ctx, D_MODEL), x_sample.reshape(lay.t_lat, D_MODEL)],
                        axis=0)
    n_c = 1 + lay.n_lat
    c_rows = jnp.concatenate([c_ctx[None, :], c, jnp.zeros((16 - n_c, D_MODEL), F32)], axis=0)
    new_lru, new_re, new_im, new_k, new_v = [], [], [], [], []
    for l in range(depth):
        i = l // 2
        mods = adaln_table(c_rows, l, w_mod, b_mod[l])
        if l % 2 == 0:
            p = dict(w_in=ev_w_in[i], conv_w=lru_conv_w[i], conv_b=lru_conv_b[i],
                     lru_wa=lru_wa[i], lru_ba=lru_ba[i], lru_wx=lru_wx[i], lru_bx=lru_bx[i],
                     lru_lam=lru_lam[i], a_re=ssm_a_re[i], a_im=ssm_a_im[i], log_dt=ssm_log_dt[i],
                     b_re=ssm_b_re[i], b_im=ssm_b_im[i], c_re=ssm_c_re[i], c_im=ssm_c_im[i],
                     d=ssm_d[i], glu_w=ssm_glu_w[i], glu_b=ssm_glu_b[i], w_out=ev_w_out[i])
            x, lru_i, re_i, im_i = even_layer(lay, x, mods, g_mix[l], p, state_lru[:, i],
                                              state_ssm_re[:, i], state_ssm_im[:, i])
            new_lru.append(lru_i)
            new_re.append(re_i)
            new_im.append(im_i)
        else:
            x, k_i, v_i = odd_layer(lay, x, mods, g_mix[l], at_w_qkv[i], at_sink[i], at_w_out[i],
                                    cache_k[:, i], cache_v[:, i])
            new_k.append(k_i)
            new_v.append(v_i)
        x = moe_layer(lay, x, g_ffn[l], mods, router_w[l], router_b[l], l, exp_w_gate, exp_w_up,
                      exp_w_down, sh_w_gate[l], sh_w_up[l], sh_w_down[l])
    y_prompt = final_norm(x, g_final, 0, lay.t_ctx).reshape(x_prompt.shape)
    y_sample = final_norm(x, g_final, lay.t_ctx, lay.t_lat).reshape(x_sample.shape)
    return (y_prompt, y_sample, jnp.stack(new_lru, axis=1), jnp.stack(new_re, axis=1),
            jnp.stack(new_im, axis=1), jnp.stack(new_k, axis=1), jnp.stack(new_v, axis=1))


def kernel(x_prompt, x_sample, state_lru, state_ssm_re, state_ssm_im, cache_k, cache_v, c, c_ctx, g_mix, g_ffn, w_mod, b_mod, ev_w_in, lru_conv_w, lru_conv_b, lru_wa, lru_ba, lru_wx, lru_bx, lru_lam, ssm_a_re, ssm_a_im, ssm_log_dt, ssm_b_re, ssm_b_im, ssm_c_re, ssm_c_im, ssm_d, ssm_glu_w, ssm_glu_b, ev_w_out, at_w_qkv, at_sink, at_w_out, router_w, router_b, exp_w_gate, exp_w_up, exp_w_down, sh_w_gate, sh_w_up, sh_w_down, g_final):
    lay = Layout(n_ctx=x_prompt.shape[0], s_ctx=x_prompt.shape[1],
                 n_lat=x_sample.shape[0], s_lat=x_sample.shape[1])
    return _forward(lay, x_prompt, x_sample, state_lru, state_ssm_re, state_ssm_im, cache_k, cache_v,
                    c, c_ctx, g_mix, g_ffn, w_mod, b_mod,
                    ev_w_in, lru_conv_w, lru_conv_b, lru_wa, lru_ba, lru_wx, lru_bx, lru_lam,
                    ssm_a_re, ssm_a_im, ssm_log_dt, ssm_b_re, ssm_b_im, ssm_c_re, ssm_c_im, ssm_d,
                    ssm_glu_w, ssm_glu_b, ev_w_out, at_w_qkv, at_sink, at_w_out,
                    router_w, router_b, exp_w_gate, exp_w_up, exp_w_down, sh_w_gate, sh_w_up,
                    sh_w_down, g_final)
```

```python
import functools
from typing import NamedTuple

import jax
import jax.numpy as jnp
from jax import lax
from jax.experimental import pallas as pl
from jax.experimental.pallas import tpu as pltpu

F32 = jnp.float32
BF16 = jnp.bfloat16
I32 = jnp.int32
HIGHEST = lax.Precision.HIGHEST

D_MODEL = 1024
EPS = 1e-6
N_MOD = 6
GRID_W = 64
D_LRU = 512
LRU_BLOCKS = 8
LRU_C = 8.0
CONV_W = 4
CONV_LEFT = 2
D_SSM = 512
SSM_GROUP = 16
SSM_GROUPS = 32
SSM_STATE = 64
S5_CHUNK = 16
S5_LANES = S5_CHUNK * SSM_GROUP
S5_SCAN_STEPS = 8
HEAD_DIM = 64
N_HEADS = 16
N_KV = 4
GQA = 4
WINDOW = 128
Q_BLOCK = 128
ROPE_BASE = 10000.0
ATTN_SCALE = HEAD_DIM ** -0.5
NEG_INF = -1e30
N_EXPERTS = 256
TOP_K = 8
N_GROUPS = 8
TOPK_GROUPS = 4
GROUP_SIZE = N_EXPERTS // N_GROUPS
D_EXPERT = 256
ROUTE_SCALE = 2.5
MOE_BLOCK = 128

SUBLANES = 8
LANES = 128
ROW_CHUNKS = D_MODEL // LANES
SEQ_TILE = 256
ROW_TILE = 512
ROUTER_TILE = 512
DEST_TILE = 1024
DISPATCH_TILE = 256
COMBINE_TILE = 128
EXPERT_X_BUFS = 6
EXPERT_Y_BUFS = 4
VMEM_LIMIT = 56 * 1024 * 1024


class Layout(NamedTuple):
    n_ctx: int
    s_ctx: int
    n_lat: int
    s_lat: int

    @property
    def t_ctx(self):
        return self.n_ctx * self.s_ctx

    @property
    def t_lat(self):
        return self.n_lat * self.s_lat

    @property
    def t(self):
        return self.t_ctx + self.t_lat

    @property
    def n_seq(self):
        return self.n_ctx + self.n_lat


def _cparams(sem):
    return pltpu.CompilerParams(dimension_semantics=sem, vmem_limit_bytes=VMEM_LIMIT)


def _mod_index(lay, tile_rows):
    n_ctx_tiles = lay.t_ctx // tile_rows
    per_lat = lay.s_lat // tile_rows

    def f(i):
        return jnp.where(i < n_ctx_tiles, 0, 1 + (i - n_ctx_tiles) // per_lat)
    return f


def _adaln_kernel(c_ref, w_ref, b_ref, o_ref):
    c = c_ref[...]
    s = c * jax.nn.sigmoid(c)
    o_ref[...] = jnp.dot(s, w_ref[0], precision=HIGHEST, preferred_element_type=F32) + b_ref[...]


def adaln_table(c_rows, layer, w_mod, b_mod):
    n = c_rows.shape[0]
    tn = 1536
    out = pl.pallas_call(
        _adaln_kernel,
        grid=(N_MOD * D_MODEL // tn,),
        in_specs=[pl.BlockSpec((n, D_MODEL), lambda j: (0, 0)),
                  pl.BlockSpec((1, D_MODEL, tn), lambda j: (layer, 0, j)),
                  pl.BlockSpec((1, tn), lambda j: (0, j))],
        out_specs=pl.BlockSpec((n, tn), lambda j: (0, j)),
        out_shape=jax.ShapeDtypeStruct((n, N_MOD * D_MODEL), F32),
        compiler_params=_cparams(("arbitrary",)),
        name="adaln",
    )(c_rows, w_mod, b_mod.reshape(1, -1))
    return out.reshape(n, N_MOD, D_MODEL)


def _modnorm(x, g, mod_ref, slot):
    ms = jnp.mean(x * x, axis=-1, keepdims=True)
    y = x * lax.rsqrt(ms + EPS) * g
    shift = mod_ref[0, slot:slot + 1, :]
    scale = mod_ref[0, slot + 1:slot + 2, :]
    return y * (1.0 + scale) + shift


GROUPS_PER_VREG = LANES // SSM_GROUP
SSM_COL_BLOCKS = D_SSM // LANES


def _group_major_store(val, tmp_ref, dst_ref):
    rows = dst_ref.shape[1]
    for j in range(SSM_COL_BLOCKS):
        tmp_ref[j] = val[:, j * LANES:(j + 1) * LANES]
    for j in range(SSM_COL_BLOCKS):
        steps = [tmp_ref[j, pl.ds(i, rows, stride=S5_CHUNK), :] for i in range(S5_CHUNK)]
        for q in range(GROUPS_PER_VREG):
            dst_ref[j * GROUPS_PER_VREG + q] = jnp.concatenate(
                [w[:, q * SSM_GROUP:(q + 1) * SSM_GROUP] for w in steps], axis=1)


def _group_major_load(src_ref, tmp_ref):
    rows = src_ref.shape[1]
    for j in range(SSM_COL_BLOCKS):
        blocks = [src_ref[j * GROUPS_PER_VREG + q] for q in range(GROUPS_PER_VREG)]
        for i in range(S5_CHUNK):
            tmp_ref[j, pl.ds(i, rows, stride=S5_CHUNK), :] = jnp.concatenate(
                [b[:, i * SSM_GROUP:(i + 1) * SSM_GROUP] for b in blocks], axis=1)
    return jnp.concatenate([tmp_ref[j] for j in range(SSM_COL_BLOCKS)], axis=1)


def _modnorm_mm_kernel(x_ref, g_ref, mod_ref, w_ref, o_ref, *ug_refs, slot, ug_col):
    h = _modnorm(x_ref[...], g_ref[...], mod_ref, slot)
    out = jnp.dot(h.astype(BF16), w_ref[...], preferred_element_type=F32)
    o_ref[...] = out
    if ug_col is not None:
        ug_ref, tmp_ref = ug_refs
        _group_major_store(out[:, ug_col:ug_col + D_SSM], tmp_ref, ug_ref)


def modnorm_matmul(lay, x, g, mods, slot, w_bf16, ug_col=None):
    t = lay.t
    n = w_bf16.shape[1]
    mi = _mod_index(lay, ROW_TILE)
    out_specs = [pl.BlockSpec((ROW_TILE, n), lambda i: (i, 0))]
    out_shape = [jax.ShapeDtypeStruct((t, n), F32)]
    if ug_col is not None:
        out_specs.append(pl.BlockSpec((SSM_GROUPS, ROW_TILE // S5_CHUNK, S5_LANES), lambda i: (0, i, 0)))
        out_shape.append(jax.ShapeDtypeStruct((SSM_GROUPS, t // S5_CHUNK, S5_LANES), F32))
    outs = pl.pallas_call(
        functools.partial(_modnorm_mm_kernel, slot=slot, ug_col=ug_col),
        grid=(t // ROW_TILE,),
        in_specs=[pl.BlockSpec((ROW_TILE, D_MODEL), lambda i: (i, 0)),
                  pl.BlockSpec((1, D_MODEL), lambda i: (0, 0)),
                  pl.BlockSpec((1, N_MOD, D_MODEL), lambda i: (mi(i), 0, 0)),
                  pl.BlockSpec((D_MODEL, n), lambda i: (0, 0))],
        out_specs=out_specs,
        out_shape=out_shape,
        scratch_shapes=([pltpu.VMEM((SSM_COL_BLOCKS, ROW_TILE, LANES), F32)]
                        if ug_col is not None else []),
        compiler_params=_cparams(("arbitrary",)),
        name="modnorm_matmul",
    )(x, g.reshape(1, -1), mods, w_bf16)
    return outs if ug_col is not None else outs[0]


def _seq_tile_maps(lay, reverse):
    assert lay.s_ctx == SEQ_TILE and lay.s_lat % SEQ_TILE == 0
    n_tiles = lay.t // SEQ_TILE
    per_lat = lay.s_lat // SEQ_TILE

    def tile(i):
        return (n_tiles - 1 - i) if reverse else i

    def seq(i):
        ti = tile(i)
        return jnp.where(ti < lay.n_ctx, ti, lay.n_ctx + (ti - lay.n_ctx) // per_lat)

    return n_tiles, tile, seq


def _softplus(x):
    return jnp.maximum(x, 0.0) + jnp.log(1.0 + jnp.exp(-jnp.abs(x)))


def _lru_kernel(rec_ref, prev_ref, next_ref, cw_ref, cb_ref, wg_ref, bg_ref, lam_ref, h0_ref,
                *rest, reverse, n_ctx, per_lat, n_tiles):
    if reverse:
        gate_ref, hf_ref, y_ref, st_ref, a_s, b_s, h_s, carry = rest
    else:
        y_ref, st_ref, a_s, b_s, h_s, carry = rest
    i = pl.program_id(0)
    ti = (n_tiles - 1 - i) if reverse else i
    is_first = jnp.logical_or(ti < n_ctx, (ti - n_ctx) % per_lat == 0)
    is_last = jnp.logical_or(ti < n_ctx, (ti - n_ctx) % per_lat == per_lat - 1)
    ts = SEQ_TILE

    rec = rec_ref[...]
    prev = jnp.where(is_first, 0.0, prev_ref[...])
    nxt = jnp.where(is_last, 0.0, next_ref[...])
    ext = jnp.concatenate([prev, rec, nxt], axis=0)
    n_ext = ts + 2 * SUBLANES
    cw = cw_ref[...]
    xc = cb_ref[...] + cw[2:3, :] * rec
    xc = xc + cw[0:1, :] * pltpu.roll(ext, 2, 0)[SUBLANES:SUBLANES + ts]
    xc = xc + cw[1:2, :] * pltpu.roll(ext, 1, 0)[SUBLANES:SUBLANES + ts]
    xc = xc + cw[3:4, :] * pltpu.roll(ext, n_ext - 1, 0)[SUBLANES:SUBLANES + ts]

    gates = jax.nn.sigmoid(jnp.dot(xc.astype(BF16), wg_ref[...], preferred_element_type=F32)
                           + bg_ref[...])
    r = gates[:, :D_LRU]
    ig = gates[:, D_LRU:]
    log_a = (-LRU_C) * r * _softplus(-lam_ref[...])
    a = jnp.exp(log_a)
    b = jnp.sqrt(1.0 - jnp.exp(2.0 * log_a)) * (ig * xc)

    row8 = lax.broadcasted_iota(I32, (ts, D_LRU), 0) % SUBLANES
    for sh in (1, 2, 4):
        if reverse:
            keep = row8 < SUBLANES - sh
            a_sh = pltpu.roll(a, ts - sh, 0)
            b_sh = pltpu.roll(b, ts - sh, 0)
        else:
            keep = row8 >= sh
            a_sh = pltpu.roll(a, sh, 0)
            b_sh = pltpu.roll(b, sh, 0)
        b = b + a * jnp.where(keep, b_sh, 0.0)
        a = a * jnp.where(keep, a_sh, 1.0)
    a_s[...] = a
    b_s[...] = b

    @pl.when(is_last if reverse else is_first)
    def _():
        carry[...] = h0_ref[0]

    n_grp = ts // SUBLANES

    def body(k, c):
        gi = (n_grp - 1 - k) if reverse else k
        sl = pl.ds(pl.multiple_of(gi * SUBLANES, SUBLANES), SUBLANES)
        h = b_s[sl, :] + a_s[sl, :] * c
        h_s[sl, :] = h
        return h[0:1, :] if reverse else h[SUBLANES - 1:SUBLANES, :]

    c_fin = lax.fori_loop(0, n_grp, body, carry[...], unroll=4)
    carry[...] = c_fin
    st_ref[0] = c_fin
    if reverse:
        y_ref[...] = (hf_ref[...] + h_s[...]) * jax.nn.gelu(gate_ref[...])
    else:
        y_ref[...] = h_s[...]


def lru_pass(lay, proj, conv_w, conv_b, wg_bf16, bg, lam, h0, reverse, hf=None):
    n_tiles, tile, seq = _seq_tile_maps(lay, reverse)
    per_lat = lay.s_lat // SEQ_TILE
    blk8 = SEQ_TILE // SUBLANES
    last8 = lay.t // SUBLANES - 1
    c = D_LRU
    in_specs = [
        pl.BlockSpec((SEQ_TILE, c), lambda i: (tile(i), 1)),
        pl.BlockSpec((SUBLANES, c), lambda i: (jnp.maximum(tile(i) * blk8 - 1, 0), 1)),
        pl.BlockSpec((SUBLANES, c), lambda i: (jnp.minimum(tile(i) * blk8 + blk8, last8), 1)),
        pl.BlockSpec((CONV_W, c), lambda i: (0, 0)),
        pl.BlockSpec((1, c), lambda i: (0, 0)),
        pl.BlockSpec((c, 2 * c), lambda i: (0, 0)),
        pl.BlockSpec((1, 2 * c), lambda i: (0, 0)),
        pl.BlockSpec((1, c), lambda i: (0, 0)),
        pl.BlockSpec((1, 1, c), lambda i: (seq(i), 0, 0)),
    ]
    args = [proj, proj, proj, conv_w, conv_b.reshape(1, -1), wg_bf16, bg.reshape(1, -1),
            lam.reshape(1, -1), h0.reshape(lay.n_seq, 1, c)]
    if reverse:
        in_specs += [pl.BlockSpec((SEQ_TILE, c), lambda i: (tile(i), 0)),
                     pl.BlockSpec((SEQ_TILE, c), lambda i: (tile(i), 0))]
        args += [proj, hf]
    y, st = pl.pallas_call(
        functools.partial(_lru_kernel, reverse=reverse, n_ctx=lay.n_ctx, per_lat=per_lat,
                          n_tiles=n_tiles),
        grid=(n_tiles,),
        in_specs=in_specs,
        out_specs=[pl.BlockSpec((SEQ_TILE, c), lambda i: (tile(i), 0)),
                   pl.BlockSpec((1, 1, c), lambda i: (seq(i), 0, 0))],
        out_shape=[jax.ShapeDtypeStruct((lay.t, c), F32),
                   jax.ShapeDtypeStruct((lay.n_seq, 1, c), F32)],
        scratch_shapes=[pltpu.VMEM((SEQ_TILE, c), F32), pltpu.VMEM((SEQ_TILE, c), F32),
                        pltpu.VMEM((SEQ_TILE, c), F32), pltpu.VMEM((1, c), F32)],
        compiler_params=_cparams(("arbitrary",)),
        name="lru_bwd" if reverse else "lru_fwd",
    )(*args)
    return y, st.reshape(lay.n_seq, c)


def _cmul(a, b):
    return a[0] * b[0] - a[1] * b[1], a[0] * b[1] + a[1] * b[0]


def _s5_matrices(a_re, a_im, log_dt, b_re, b_im, c_re, c_im):
    a_re, a_im = a_re.astype(F32), a_im.astype(F32)
    dt = jnp.exp(log_dt.astype(F32))[..., None]
    z = (a_re * dt, a_im * dt)

    def zpow(k):
        k = k.reshape((-1,) + (1,) * z[0].ndim)
        mag = jnp.exp(k * z[0][None])
        return mag * jnp.cos(k * z[1][None]), mag * jnp.sin(k * z[1][None])

    a_bar = zpow(jnp.ones((1,), F32))
    a_bar = (a_bar[0][0], a_bar[1][0])
    den = a_re * a_re + a_im * a_im
    xr, xi = a_bar[0] - 1.0, a_bar[1]
    q = ((xr * a_re + xi * a_im) / den, (xi * a_re - xr * a_im) / den)
    b_bar = _cmul((q[0][..., None], q[1][..., None]), (b_re.astype(F32), b_im.astype(F32)))
    cc = (c_re.astype(F32), c_im.astype(F32))
    el = S5_CHUNK
    pw = zpow(jnp.arange(el + 1, dtype=F32))
    idx = jnp.arange(el)
    m_in, m_toep, m_out = [], [], []
    for d in range(2):
        p_d = (pw[0][:, d], pw[1][:, d])
        b_d = (b_bar[0][d], b_bar[1][d])
        c_d = (cc[0][d], cc[1][d])
        k_in = (el - 1 - idx) if d == 0 else idx
        w_in = _cmul((p_d[0][k_in][..., None], p_d[1][k_in][..., None]),
                     (b_d[0][None], b_d[1][None]))
        w_in = [jnp.transpose(w, (1, 0, 3, 2)).reshape(SSM_GROUPS, S5_LANES, SSM_STATE) for w in w_in]
        m_in.append(jnp.concatenate(w_in, axis=-1))
        cp = _cmul((c_d[0][None], c_d[1][None]),
                   (p_d[0][:, :, None, :], p_d[1][:, :, None, :]))
        kern = (jnp.einsum('kghp,gpc->kgch', cp[0][:el], b_d[0])
                - jnp.einsum('kghp,gpc->kgch', cp[1][:el], b_d[1]))
        diff = (idx[None, :] - idx[:, None]) if d == 0 else (idx[:, None] - idx[None, :])
        blocks = jnp.where((diff >= 0)[:, :, None, None, None],
                           kern[jnp.clip(diff, 0, el - 1)], 0.0)
        m_toep.append(jnp.transpose(blocks, (2, 0, 3, 1, 4)).reshape(SSM_GROUPS, S5_LANES, S5_LANES))
        k_out = (idx + 1) if d == 0 else (el - idx)
        w_out = [jnp.transpose(w[k_out], (1, 3, 0, 2)).reshape(SSM_GROUPS, SSM_STATE, S5_LANES)
                 for w in cp]
        m_out.append(jnp.concatenate([w_out[0], -w_out[1]], axis=1))
    mul = zpow(el * 2.0 ** jnp.arange(S5_SCAN_STEPS, dtype=F32))
    mul = [jnp.transpose(m, (2, 1, 0, 3)) for m in mul]
    coef_a = jnp.concatenate([mul[0], mul[0]], axis=-1)
    coef_b = jnp.concatenate([-mul[1], mul[1]], axis=-1)
    stack = lambda xs: jnp.stack(xs, axis=1)
    return (stack(m_in).astype(BF16), stack(m_toep).astype(BF16), stack(m_out).astype(BF16),
            coef_a, coef_b)


def _s5_scan(v, ca, cb, seg, reverse):
    n = v.shape[0]
    assert seg <= 2 ** S5_SCAN_STEPS
    row = lax.broadcasted_iota(I32, (n, 2 * SSM_STATE), 0) % seg
    k, sh = 0, 1
    while sh < seg:
        if reverse:
            s = jnp.where(row < seg - sh, pltpu.roll(v, n - sh, 0), 0.0)
        else:
            s = jnp.where(row >= sh, pltpu.roll(v, sh, 0), 0.0)
        v = v + ca[k:k + 1, :] * s + cb[k:k + 1, :] * pltpu.roll(s, SSM_STATE, 1)
        k += 1
        sh *= 2
    return v


def _s5_shift(h, seg, reverse):
    n = h.shape[0]
    row = lax.broadcasted_iota(I32, (n, 2 * SSM_STATE), 0) % seg
    if reverse:
        return jnp.where(row < seg - 1, pltpu.roll(h, n - 1, 0), 0.0)
    return jnp.where(row >= 1, pltpu.roll(h, 1, 0), 0.0)


def _s5_kernel(u_ref, min_ref, mtoep_ref, mout_ref, ca_ref, cb_ref, h0_ref, y_ref, hc_ref,
               v_s, hp_s, *, rc, seg_c, n_lat, seg_l):
    u = u_ref[0].astype(BF16)
    u_c, u_l = u[:rc], u[rc:]
    y_c = jnp.zeros((rc, S5_LANES), F32)
    y_l = jnp.zeros((n_lat * seg_l, S5_LANES), F32)
    for d in range(2):
        reverse = d == 1
        ca = ca_ref[0, d]
        cb = cb_ref[0, d]
        m_in = min_ref[0, d]
        m_toep = mtoep_ref[0, d]
        m_out = mout_ref[0, d]
        h_c = _s5_scan(jnp.dot(u_c, m_in, preferred_element_type=F32), ca, cb, seg_c, reverse)
        hc_ref[0, d] = h_c
        hp_c = _s5_shift(h_c, seg_c, reverse)
        y_c = y_c + jnp.dot(u_c, m_toep, preferred_element_type=F32)
        y_c = y_c + jnp.dot(hp_c.astype(BF16), m_out, preferred_element_type=F32)
        v_s[...] = jnp.dot(u_l, m_in, preferred_element_type=F32)
        for s in range(n_lat):
            h0 = h0_ref[0, d, s:s + 1, :]
            r0 = s * seg_l + (seg_l - 1 if reverse else 0)
            v_s[r0:r0 + 1, :] = (v_s[r0:r0 + 1, :] + ca[0:1, :] * h0
                                 + cb[0:1, :] * pltpu.roll(h0, SSM_STATE, 1))
        h_l = _s5_scan(v_s[...], ca, cb, seg_l, reverse)
        hp_s[...] = _s5_shift(h_l, seg_l, reverse)
        for s in range(n_lat):
            r0 = s * seg_l + (seg_l - 1 if reverse else 0)
            hp_s[r0:r0 + 1, :] = h0_ref[0, d, s:s + 1, :]
        y_l = y_l + jnp.dot(u_l, m_toep, preferred_element_type=F32)
        y_l = y_l + jnp.dot(hp_s[...].astype(BF16), m_out, preferred_element_type=F32)
    y_ref[0, :rc, :] = y_c
    y_ref[0, rc:, :] = y_l


def s5_mixer(lay, u_g, mats, h0):
    m_in, m_toep, m_out, coef_a, coef_b = mats
    rows = lay.t // S5_CHUNK
    rc = lay.t_ctx // S5_CHUNK
    rl = rows - rc
    st2 = 2 * SSM_STATE
    g4 = lambda g: (g, 0, 0, 0)
    return pl.pallas_call(
        functools.partial(_s5_kernel, rc=rc, seg_c=lay.s_ctx // S5_CHUNK, n_lat=lay.n_lat,
                          seg_l=lay.s_lat // S5_CHUNK),
        grid=(SSM_GROUPS,),
        in_specs=[pl.BlockSpec((1, rows, S5_LANES), lambda g: (g, 0, 0)),
                  pl.BlockSpec((1, 2, S5_LANES, st2), g4),
                  pl.BlockSpec((1, 2, S5_LANES, S5_LANES), g4),
                  pl.BlockSpec((1, 2, st2, S5_LANES), g4),
                  pl.BlockSpec((1, 2, S5_SCAN_STEPS, st2), g4),
                  pl.BlockSpec((1, 2, S5_SCAN_STEPS, st2), g4),
                  pl.BlockSpec((1, 2, lay.n_lat, st2), g4)],
        out_specs=[pl.BlockSpec((1, rows, S5_LANES), lambda g: (g, 0, 0)),
                   pl.BlockSpec((1, 2, rc, st2), g4)],
        out_shape=[jax.ShapeDtypeStruct((SSM_GROUPS, rows, S5_LANES), F32),
                   jax.ShapeDtypeStruct((SSM_GROUPS, 2, rc, st2), F32)],
        scratch_shapes=[pltpu.VMEM((rl, st2), F32), pltpu.VMEM((rl, st2), F32)],
        compiler_params=_cparams(("arbitrary",)),
        name="s5_mixer",
    )(u_g, m_in, m_toep, m_out, coef_a, coef_b, h0)


def _even_out_kernel(x_ref, ya_ref, yg_ref, u_ref, d_ref, gw_ref, gb_ref, w_ref, mod_ref, o_ref, yt_s):
    ys = _group_major_load(yg_ref, yt_s) + d_ref[...] * u_ref[...]
    g = jax.nn.gelu(ys)
    yb = g * jax.nn.sigmoid(jnp.dot(g.astype(BF16), gw_ref[...], preferred_element_type=F32)
                            + gb_ref[...])
    out = jnp.dot(ya_ref[...].astype(BF16), w_ref[:D_LRU, :], preferred_element_type=F32)
    out = out + jnp.dot(yb.astype(BF16), w_ref[D_LRU:, :], preferred_element_type=F32)
    o_ref[...] = x_ref[...] + mod_ref[0, 2:3, :] * out


def even_out(lay, x, y_a, y_g, proj, ssm_d, glu_w_bf16, glu_b, w_out_bf16, mods):
    mi = _mod_index(lay, ROW_TILE)
    c = D_SSM
    row = lambda i: (i, 0)
    const = lambda i: (0, 0)
    return pl.pallas_call(
        _even_out_kernel,
        grid=(lay.t // ROW_TILE,),
        in_specs=[pl.BlockSpec((ROW_TILE, D_MODEL), row),
                  pl.BlockSpec((ROW_TILE, c), row),
                  pl.BlockSpec((SSM_GROUPS, ROW_TILE // S5_CHUNK, S5_LANES), lambda i: (0, i, 0)),
                  pl.BlockSpec((ROW_TILE, c), lambda i: (i, 2)),
                  pl.BlockSpec((1, c), const),
                  pl.BlockSpec((c, c), const),
                  pl.BlockSpec((1, c), const),
                  pl.BlockSpec((D_MODEL, D_MODEL), const),
                  pl.BlockSpec((1, N_MOD, D_MODEL), lambda i: (mi(i), 0, 0))],
        out_specs=pl.BlockSpec((ROW_TILE, D_MODEL), row),
        out_shape=jax.ShapeDtypeStruct((lay.t, D_MODEL), F32),
        scratch_shapes=[pltpu.VMEM((SSM_COL_BLOCKS, ROW_TILE, LANES), F32)],
        compiler_params=_cparams(("arbitrary",)),
        name="even_out",
    )(x, y_a, y_g, proj, ssm_d.reshape(1, -1), glu_w_bf16, glu_b.reshape(1, -1), w_out_bf16, mods)


def _softmax_pv(parts, sink_col):
    m = sink_col
    for s, _ in parts:
        m = jnp.maximum(m, jnp.max(s, axis=-1, keepdims=True))
    den = jnp.exp(sink_col - m)
    acc = None
    for s, v in parts:
        p = jnp.exp(s - m)
        den = den + jnp.sum(p, axis=-1, keepdims=True)
        pv = jnp.dot(p.astype(BF16), v.astype(BF16), preferred_element_type=F32)
        acc = pv if acc is None else acc + pv
    return acc / den


def _nt_dot(a, b):
    return lax.dot_general(a.astype(BF16), b.astype(BF16), (((1,), (1,)), ((), ())),
                           preferred_element_type=F32)


def _attn_ctx_kernel(q_ref, k_ref, v_ref, sink_ref, o_ref):
    n = q_ref.shape[0]
    for kh in range(N_KV):
        k = k_ref[:, kh * HEAD_DIM:(kh + 1) * HEAD_DIM]
        v = v_ref[:, kh * HEAD_DIM:(kh + 1) * HEAD_DIM]
        for g in range(GQA):
            h = kh * GQA + g
            q = q_ref[:, h * HEAD_DIM:(h + 1) * HEAD_DIM]
            s = _nt_dot(q, k) * ATTN_SCALE
            sink = jnp.broadcast_to(sink_ref[0:1, h:h + 1], (n, 1))
            o_ref[:, h * HEAD_DIM:(h + 1) * HEAD_DIM] = _softmax_pv([(s, v)], sink)


def attn_context(lay, qkv, sink):
    nq = N_HEADS * HEAD_DIM
    nkv = N_KV * HEAD_DIM
    return pl.pallas_call(
        _attn_ctx_kernel,
        grid=(lay.n_ctx,),
        in_specs=[pl.BlockSpec((lay.s_ctx, nq), lambda b: (b, 0)),
                  pl.BlockSpec((lay.s_ctx, nkv), lambda b: (b, nq // nkv)),
                  pl.BlockSpec((lay.s_ctx, nkv), lambda b: (b, nq // nkv + 1)),
                  pl.BlockSpec((1, N_HEADS), lambda b: (0, 0))],
        out_specs=pl.BlockSpec((lay.s_ctx, nq), lambda b: (b, 0)),
        out_shape=jax.ShapeDtypeStruct((lay.t_ctx, nq), F32),
        compiler_params=_cparams(("arbitrary",)),
        name="attn_context",
    )(qkv, qkv, qkv, sink.reshape(1, -1))


def _rope(x, cos, sin):
    lane = lax.broadcasted_iota(I32, (x.shape[0], 2 * HEAD_DIM), 1) % HEAD_DIM
    outs = []
    for j in range(x.shape[1] // (2 * HEAD_DIM)):
        xs = x[:, j * 2 * HEAD_DIM:(j + 1) * 2 * HEAD_DIM]
        sw = jnp.where(lane < HEAD_DIM // 2,
                       pltpu.roll(xs, 2 * HEAD_DIM - HEAD_DIM // 2, 1),
                       pltpu.roll(xs, HEAD_DIM // 2, 1))
        outs.append(xs * cos + sw * sin)
    return outs


def _attn_lat_kernel(q_ref, k0_ref, k1_ref, k2_ref, v0_ref, v1_ref, v2_ref, ck_ref, cv_ref,
                     cq_ref, sq_ref, c0_ref, c1_ref, c2_ref, s0_ref, s1_ref, s2_ref, sink_ref,
                     o_ref, *, n_blk):
    j = pl.program_id(1)
    qb = Q_BLOCK
    q_parts = [qp * ATTN_SCALE for qp in _rope(q_ref[...], cq_ref[...], sq_ref[...])]
    k_parts = [_rope(kr[...], cr[...], sr[...])
               for kr, cr, sr in ((k0_ref, c0_ref, s0_ref), (k1_ref, c1_ref, s1_ref),
                                  (k2_ref, c2_ref, s2_ref))]
    qi = lax.broadcasted_iota(I32, (qb, 3 * qb), 0)
    km = lax.broadcasted_iota(I32, (qb, 3 * qb), 1)
    kpos = j * qb - qb + km
    mask1 = (jnp.abs(km - qb - qi) <= WINDOW) & (kpos >= 0) & (kpos < n_blk * qb)
    mask = jnp.concatenate([mask1] * GQA, axis=0)
    for kh in range(N_KV):
        half = (kh % 2) * HEAD_DIM
        k_loc = jnp.concatenate([kp[kh // 2][:, half:half + HEAD_DIM] for kp in k_parts], axis=0)
        v_loc = jnp.concatenate([vr[:, kh * HEAD_DIM:(kh + 1) * HEAD_DIM]
                                 for vr in (v0_ref, v1_ref, v2_ref)], axis=0)
        qs, sinks = [], []
        for g in range(GQA):
            h = kh * GQA + g
            qs.append(q_parts[h // 2][:, (h % 2) * HEAD_DIM:(h % 2 + 1) * HEAD_DIM])
            sinks.append(jnp.broadcast_to(sink_ref[0:1, h:h + 1], (qb, 1)))
        q = jnp.concatenate(qs, axis=0)
        sink = jnp.concatenate(sinks, axis=0)
        s_loc = jnp.where(mask, _nt_dot(q, k_loc), NEG_INF)
        s_ctx = _nt_dot(q, ck_ref[0, kh])
        o = _softmax_pv([(s_loc, v_loc), (s_ctx, cv_ref[0, kh])], sink)
        for g in range(GQA):
            h = kh * GQA + g
            o_ref[:, h * HEAD_DIM:(h + 1) * HEAD_DIM] = o[g * qb:(g + 1) * qb]


def _rope_tables(s_len):
    rows = s_len // GRID_W
    row = jnp.repeat(jnp.arange(rows), GRID_W).astype(F32)
    col = jnp.tile(jnp.arange(GRID_W), rows).astype(F32)
    nf = HEAD_DIM // 4
    inv = ROPE_BASE ** (-jnp.arange(nf, dtype=F32) / nf)
    ang = jnp.concatenate([row[:, None] * inv, col[:, None] * inv], axis=-1)
    cos, sin = jnp.cos(ang), jnp.sin(ang)
    cos2 = jnp.tile(jnp.concatenate([cos, cos], axis=-1), (1, 2))
    sin2 = jnp.tile(jnp.concatenate([-sin, sin], axis=-1), (1, 2))
    return cos2, sin2


def attn_latent(lay, qkv, cache_k, cache_v, sink):
    nq = N_HEADS * HEAD_DIM
    nkv = N_KV * HEAD_DIM
    n_blk = lay.s_lat // Q_BLOCK
    base = lay.t_ctx // Q_BLOCK
    n_ctx_keys = cache_k.shape[2]
    cos2, sin2 = _rope_tables(lay.s_lat)
    kcol = nq // nkv

    def qrow(b, j):
        return base + b * n_blk + j

    def krow(off):
        return lambda b, j: base + b * n_blk + jnp.clip(j + off, 0, n_blk - 1)

    def trow(off):
        return lambda b, j: (jnp.clip(j + off, 0, n_blk - 1), 0)

    kv_spec = lambda off, col: pl.BlockSpec((Q_BLOCK, nkv), lambda b, j: (krow(off)(b, j), col))
    tab = lambda off: pl.BlockSpec((Q_BLOCK, 2 * HEAD_DIM), trow(off))
    cache_spec = pl.BlockSpec((1, N_KV, n_ctx_keys, HEAD_DIM), lambda b, j: (b, 0, 0, 0))
    return pl.pallas_call(
        functools.partial(_attn_lat_kernel, n_blk=n_blk),
        grid=(lay.n_lat, n_blk),
        in_specs=[pl.BlockSpec((Q_BLOCK, nq), lambda b, j: (qrow(b, j), 0)),
                  kv_spec(-1, kcol), kv_spec(0, kcol), kv_spec(1, kcol),
                  kv_spec(-1, kcol + 1), kv_spec(0, kcol + 1), kv_spec(1, kcol + 1),
                  cache_spec, cache_spec,
                  tab(0), tab(0), tab(-1), tab(0), tab(1), tab(-1), tab(0), tab(1),
                  pl.BlockSpec((1, N_HEADS), lambda b, j: (0, 0))],
        out_specs=pl.BlockSpec((Q_BLOCK, nq), lambda b, j: (b * n_blk + j, 0)),
        out_shape=jax.ShapeDtypeStruct((lay.t_lat, nq), F32),
        compiler_params=_cparams(("arbitrary", "arbitrary")),
        name="attn_latent",
    )(qkv, qkv, qkv, qkv, qkv, qkv, qkv, cache_k, cache_v,
      cos2, sin2, cos2, cos2, cos2, sin2, sin2, sin2, sink.reshape(1, -1))


def _mm_res_kernel(x_ref, ac_ref, al_ref, w_ref, mod_ref, o_ref, *, n_ctx_tiles):
    a = jnp.where(pl.program_id(0) < n_ctx_tiles, ac_ref[...], al_ref[...])
    out = jnp.dot(a.astype(BF16), w_ref[...], preferred_element_type=F32)
    o_ref[...] = x_ref[...] + mod_ref[0, 2:3, :] * out


def matmul_residual(lay, x, a_ctx, a_lat, w_bf16, mods):
    mi = _mod_index(lay, ROW_TILE)
    k = a_ctx.shape[1]
    nct = lay.t_ctx // ROW_TILE
    return pl.pallas_call(
        functools.partial(_mm_res_kernel, n_ctx_tiles=nct),
        grid=(lay.t // ROW_TILE,),
        in_specs=[pl.BlockSpec((ROW_TILE, D_MODEL), lambda i: (i, 0)),
                  pl.BlockSpec((ROW_TILE, k), lambda i: (jnp.minimum(i, nct - 1), 0)),
                  pl.BlockSpec((ROW_TILE, k), lambda i: (jnp.maximum(i - nct, 0), 0)),
                  pl.BlockSpec((k, D_MODEL), lambda i: (0, 0)),
                  pl.BlockSpec((1, N_MOD, D_MODEL), lambda i: (mi(i), 0, 0))],
        out_specs=pl.BlockSpec((ROW_TILE, D_MODEL), lambda i: (i, 0)),
        out_shape=jax.ShapeDtypeStruct((lay.t, D_MODEL), F32),
        compiler_params=_cparams(("arbitrary",)),
        name="matmul_residual",
    )(x, a_ctx, a_lat, w_bf16, mods)


def _rowtile_load(ref, n, base=0):
    return jnp.concatenate([ref[pl.ds(base + c, n, stride=ROW_CHUNKS), :]
                            for c in range(ROW_CHUNKS)], axis=1)


def _rowtile_store(ref, val, n):
    for c in range(ROW_CHUNKS):
        ref[pl.ds(c, n, stride=ROW_CHUNKS), :] = val[:, c * LANES:(c + 1) * LANES]


def _row_copy(src, src_row, dst, dst_row, sem):
    return pltpu.make_async_copy(
        src.at[pl.ds(pl.multiple_of(src_row * ROW_CHUNKS, ROW_CHUNKS), ROW_CHUNKS), :],
        dst.at[pl.ds(pl.multiple_of(dst_row * ROW_CHUNKS, ROW_CHUNKS), ROW_CHUNKS), :], sem)


def _router_kernel(x_ref, g_ref, mod_ref, rwt_ref, rb_ref, tri_ref,
                   hn_ref, eidx_ref, wts_ref, rank_ref, cnt_ref, cnt_s):
    tm = ROUTER_TILE

    @pl.when(pl.program_id(0) == 0)
    def _():
        cnt_s[...] = jnp.zeros_like(cnt_s)

    h = _modnorm(x_ref[...], g_ref[...], mod_ref, 3)
    _rowtile_store(hn_ref, h, tm)
    logits = lax.dot_general(rwt_ref[...], h, (((1,), (1,)), ((), ())),
                             precision=HIGHEST, preferred_element_type=F32)
    scores = jax.nn.sigmoid(logits)
    choice = scores + rb_ref[...]
    gs_rows = []
    for g in range(N_GROUPS):
        cg = choice[g * GROUP_SIZE:(g + 1) * GROUP_SIZE, :]
        m1 = jnp.max(cg, axis=0, keepdims=True)
        eq = cg == m1
        cnt = jnp.sum(eq.astype(F32), axis=0, keepdims=True)
        m2 = jnp.max(jnp.where(eq, -jnp.inf, cg), axis=0, keepdims=True)
        gs_rows.append(m1 + jnp.where(cnt >= 2.0, m1, m2))
    gs = jnp.concatenate(gs_rows, axis=0)
    gi = lax.broadcasted_iota(I32, (N_GROUPS, tm), 0)
    grank = jnp.zeros((N_GROUPS, tm), I32)
    for g in range(N_GROUPS):
        other = gs[g:g + 1, :]
        ahead = (other > gs) | ((other == gs) & (g < gi))
        grank = grank + ahead.astype(I32)
    gsel = grank < TOPK_GROUPS
    emask = jnp.concatenate(
        [jnp.broadcast_to(gsel[g:g + 1, :], (GROUP_SIZE, tm)) for g in range(N_GROUPS)], axis=0)
    masked = jnp.where(emask, choice, -jnp.inf)
    ei = lax.broadcasted_iota(I32, (N_EXPERTS, tm), 0)
    idxs, ws = [], []
    member = jnp.zeros((N_EXPERTS, tm), F32)
    for _ in range(TOP_K):
        m = jnp.max(masked, axis=0, keepdims=True)
        idx = jnp.min(jnp.where(masked == m, ei, N_EXPERTS), axis=0, keepdims=True)
        hit = ei == idx
        ws.append(jnp.sum(jnp.where(hit, scores, 0.0), axis=0, keepdims=True))
        idxs.append(idx)
        member = jnp.where(hit, 1.0, member)
        masked = jnp.where(hit, -jnp.inf, masked)
    w = jnp.concatenate(ws, axis=0)
    wts_ref[...] = w / jnp.sum(w, axis=0, keepdims=True) * ROUTE_SCALE
    eidx_ref[...] = jnp.concatenate(idxs, axis=0)
    before = jnp.dot(member.astype(BF16), tri_ref[...], preferred_element_type=F32) + cnt_s[...]
    ranks = [jnp.sum(jnp.where(ei == idx, before, 0.0), axis=0, keepdims=True) for idx in idxs]
    rank_ref[...] = jnp.concatenate(ranks, axis=0).astype(I32)
    cnt_s[...] = cnt_s[...] + jnp.sum(member, axis=1, keepdims=True)
    cnt_ref[...] = jnp.broadcast_to(cnt_s[...], cnt_ref.shape)


def moe_router(lay, x, g, mods, router_w, router_b):
    t = lay.t
    tm = ROUTER_TILE
    mi = _mod_index(lay, tm)
    tri = (jnp.arange(tm)[:, None] < jnp.arange(tm)[None, :]).astype(BF16)
    tok = lambda i: (0, i)
    const = lambda i: (0, 0)
    return pl.pallas_call(
        _router_kernel,
        grid=(t // tm,),
        in_specs=[pl.BlockSpec((tm, D_MODEL), lambda i: (i, 0)),
                  pl.BlockSpec((1, D_MODEL), const),
                  pl.BlockSpec((1, N_MOD, D_MODEL), lambda i: (mi(i), 0, 0)),
                  pl.BlockSpec((N_EXPERTS, D_MODEL), const),
                  pl.BlockSpec((N_EXPERTS, 1), const),
                  pl.BlockSpec((tm, tm), const)],
        out_specs=[pl.BlockSpec((tm * ROW_CHUNKS, LANES), lambda i: (i, 0)),
                   pl.BlockSpec((TOP_K, tm), tok),
                   pl.BlockSpec((TOP_K, tm), tok),
                   pl.BlockSpec((TOP_K, tm), tok),
                   pl.BlockSpec((N_EXPERTS, LANES), const)],
        out_shape=[jax.ShapeDtypeStruct((t * ROW_CHUNKS, LANES), F32),
                   jax.ShapeDtypeStruct((TOP_K, t), I32),
                   jax.ShapeDtypeStruct((TOP_K, t), F32),
                   jax.ShapeDtypeStruct((TOP_K, t), I32),
                   jax.ShapeDtypeStruct((N_EXPERTS, LANES), F32)],
        scratch_shapes=[pltpu.VMEM((N_EXPERTS, 1), F32)],
        compiler_params=_cparams(("arbitrary",)),
        name="moe_router",
    )(x, g.reshape(1, -1), mods, router_w.T, router_b.reshape(-1, 1), tri)


def _dest_kernel(start_ref, eidx_ref, rank_ref, dest_ref):
    e = eidx_ref[...]

    def body(i, acc):
        return jnp.where(e == i, start_ref[i], acc)

    dest_ref[...] = lax.fori_loop(0, N_EXPERTS, body, jnp.zeros_like(e), unroll=8) + rank_ref[...]


def moe_dest(pad_start, eidx, rank):
    t = eidx.shape[1]
    tn = DEST_TILE
    spec = pl.BlockSpec((TOP_K, tn), lambda i, ps: (0, i))
    return pl.pallas_call(
        _dest_kernel,
        grid_spec=pltpu.PrefetchScalarGridSpec(
            num_scalar_prefetch=1, grid=(t // tn,), in_specs=[spec, spec], out_specs=spec),
        out_shape=jax.ShapeDtypeStruct((TOP_K, t), I32),
        compiler_params=_cparams(("arbitrary",)),
        name="moe_dest",
    )(pad_start, eidx, rank)


def _issue_row_copies(idx_at, n, copy_at):
    def body(i, c):
        for p in range(2):
            r = 2 * i + p
            copy_at(r, idx_at(r)).start(priority=p)
        return c
    lax.fori_loop(0, n // 2, body, 0, unroll=4)


def _dispatch_kernel(zrow_ref, dest_hbm, hn_ref, xs_hbm, idx_s, zbuf, isem, zsem, ssem, *, n_tiles):
    i = pl.program_id(0)
    slot = i % 2
    td = DISPATCH_TILE

    n_idx = TOP_K * td

    def idx_copy(tile, s):
        return pltpu.make_async_copy(dest_hbm.at[tile], idx_s.at[pl.ds(s * n_idx, n_idx)], isem.at[s])

    def zero_copy(e):
        r0 = pl.multiple_of(zrow_ref[e] * ROW_CHUNKS, ROW_CHUNKS)
        return pltpu.make_async_copy(zbuf, xs_hbm.at[pl.ds(r0, MOE_BLOCK * ROW_CHUNKS), :], zsem)

    @pl.when(i == 0)
    def _():
        zbuf[...] = jnp.zeros_like(zbuf)

        def zstart(e, c):
            @pl.when(zrow_ref[e] >= 0)
            def _():
                zero_copy(e).start()
            return c

        def zwait(e, c):
            @pl.when(zrow_ref[e] >= 0)
            def _():
                zero_copy(e).wait()
            return c

        lax.fori_loop(0, zrow_ref.shape[0], zstart, 0)
        idx_copy(0, 0).start()
        lax.fori_loop(0, zrow_ref.shape[0], zwait, 0)

    idx_copy(i, slot).wait()

    @pl.when(i + 1 < n_tiles)
    def _():
        idx_copy(i + 1, 1 - slot).start()

    for k in range(TOP_K):
        _issue_row_copies(lambda r: idx_s[slot * n_idx + k * td + r], td,
                          lambda r, d: _row_copy(hn_ref, r, xs_hbm, d, ssem))
    for k in range(TOP_K):
        pltpu.make_async_copy(hn_ref, xs_hbm.at[pl.ds(0, td * ROW_CHUNKS), :], ssem).wait()


def _tile_major(dest, tile):
    t = dest.shape[1]
    return dest.reshape(TOP_K, t // tile, tile).transpose(1, 0, 2).reshape(t // tile, TOP_K * tile)


def moe_dispatch(lay, hn, dest, zero_row, n_rows):
    td = DISPATCH_TILE
    n_tiles = lay.t // td
    return pl.pallas_call(
        functools.partial(_dispatch_kernel, n_tiles=n_tiles),
        grid_spec=pltpu.PrefetchScalarGridSpec(
            num_scalar_prefetch=1,
            grid=(n_tiles,),
            in_specs=[pl.BlockSpec(memory_space=pl.ANY),
                      pl.BlockSpec((td * ROW_CHUNKS, LANES), lambda i, z: (i, 0))],
            out_specs=pl.BlockSpec(memory_space=pl.ANY),
            scratch_shapes=[pltpu.SMEM((2 * TOP_K * td,), I32),
                            pltpu.VMEM((MOE_BLOCK * ROW_CHUNKS, LANES), F32),
                            pltpu.SemaphoreType.DMA((2,)),
                            pltpu.SemaphoreType.DMA,
                            pltpu.SemaphoreType.DMA]),
        out_shape=jax.ShapeDtypeStruct((n_rows * ROW_CHUNKS, LANES), F32),
        compiler_params=_cparams(("arbitrary",)),
        name="moe_dispatch",
    )(zero_row, dest, hn)


def _expert_kernel(blk0_ref, nblk_ref, tail_ref, xs_hbm, wg_ref, wu_ref, wd_ref, y_hbm,
                   xbuf, ybuf, wg_s, wu_s, wd_s, isem, osem):
    e = pl.program_id(0)
    nb = nblk_ref[e]
    g0 = blk0_ref[e]
    total = blk0_ref[N_EXPERTS - 1] + nblk_ref[N_EXPERTS - 1]
    blk_rows = MOE_BLOCK * ROW_CHUNKS
    n_x = xbuf.shape[0]
    n_y = ybuf.shape[0]

    def block_rows(g):
        return pl.ds(pl.multiple_of(g * blk_rows, blk_rows), blk_rows)

    def fetch(g):
        s = g % n_x
        return pltpu.make_async_copy(xs_hbm.at[block_rows(g), :], xbuf.at[s], isem.at[s])

    def writeback(g):
        s = g % n_y
        return pltpu.make_async_copy(ybuf.at[s], y_hbm.at[block_rows(g), :], osem.at[s])

    ahead = n_x - 2

    @pl.when(e == 0)
    def _():
        for p in range(ahead):
            @pl.when(p < total)
            def _():
                fetch(p).start()

    @pl.when(nb > 0)
    def _():
        wg_s[...] = wg_ref[0, 0].astype(BF16)
        wu_s[...] = wu_ref[0, 0].astype(BF16)
        wd_s[...] = wd_ref[0, 0].astype(BF16)

    def run_blocks(g, n):
        for p in range(n):
            @pl.when(g + ahead + p < total)
            def _():
                fetch(g + ahead + p).start()
        for p in range(n):
            fetch(g + p).wait()
        ys = []
        for p in range(n):
            x = _rowtile_load(xbuf.at[(g + p) % n_x], MOE_BLOCK).astype(BF16)
            gate = jnp.dot(x, wg_s[...], preferred_element_type=F32)
            up = jnp.dot(x, wu_s[...], preferred_element_type=F32)
            act = gate * jax.nn.sigmoid(gate) * up
            ys.append(jnp.dot(act.astype(BF16), wd_s[...], preferred_element_type=F32))
        for p in range(n):
            @pl.when(g + p >= n_y)
            def _():
                writeback(g + p - n_y).wait()
        for p in range(n):
            _rowtile_store(ybuf.at[(g + p) % n_y], ys[p], MOE_BLOCK)
            writeback(g + p).start()

    def pair(jj, c):
        run_blocks(g0 + 2 * jj, 2)
        return c

    lax.fori_loop(0, nb // 2, pair, 0)

    @pl.when(nb % 2 == 1)
    def _():
        run_blocks(g0 + nb - 1, 1)

    @pl.when(e == N_EXPERTS - 1)
    def _():
        for p in range(n_y, 0, -1):
            @pl.when(total >= p)
            def _():
                writeback(total - p).wait()

        ybuf[0] = jnp.zeros(ybuf.shape[1:], F32)

        def tail_copy(i):
            r0 = pl.multiple_of(tail_ref[i] * ROW_CHUNKS, blk_rows)
            return pltpu.make_async_copy(ybuf.at[0], y_hbm.at[pl.ds(r0, blk_rows), :], osem.at[0])

        def tstart(i, c):
            @pl.when(tail_ref[i] >= 0)
            def _():
                tail_copy(i).start()
            return c

        def twait(i, c):
            @pl.when(tail_ref[i] >= 0)
            def _():
                tail_copy(i).wait()
            return c

        lax.fori_loop(0, tail_ref.shape[0], tstart, 0)
        lax.fori_loop(0, tail_ref.shape[0], twait, 0)


def moe_experts(xs, first_block, n_blocks, tail_row, layer, w_gate, w_up, w_down):
    wspec = lambda shape: pl.BlockSpec((1, 1) + shape, lambda e, a, b, c: (layer, e, 0, 0))
    blk = (MOE_BLOCK * ROW_CHUNKS, LANES)
    return pl.pallas_call(
        _expert_kernel,
        grid_spec=pltpu.PrefetchScalarGridSpec(
            num_scalar_prefetch=3,
            grid=(N_EXPERTS,),
            in_specs=[pl.BlockSpec(memory_space=pl.ANY),
                      wspec((D_MODEL, D_EXPERT)), wspec((D_MODEL, D_EXPERT)),
                      wspec((D_EXPERT, D_MODEL))],
            out_specs=pl.BlockSpec(memory_space=pl.ANY),
            scratch_shapes=[pltpu.VMEM((EXPERT_X_BUFS,) + blk, F32),
                            pltpu.VMEM((EXPERT_Y_BUFS,) + blk, F32),
                            pltpu.VMEM((D_MODEL, D_EXPERT), BF16),
                            pltpu.VMEM((D_MODEL, D_EXPERT), BF16),
                            pltpu.VMEM((D_EXPERT, D_MODEL), BF16),
                            pltpu.SemaphoreType.DMA((EXPERT_X_BUFS,)),
                            pltpu.SemaphoreType.DMA((EXPERT_Y_BUFS,))]),
        out_shape=jax.ShapeDtypeStruct(xs.shape, F32),
        compiler_params=_cparams(("arbitrary",)),
        name="moe_experts",
    )(first_block, n_blocks, tail_row, xs, w_gate, w_up, w_down)


def _combine_kernel(dest_hbm, y_hbm, x_ref, hn_ref, w_ref, sg_ref, su_ref, sd_ref, mod_ref, o_ref,
                    idx_s, ybuf, isem, gsem, *, n_tiles):
    i = pl.program_id(0)
    slot = i % 2
    tm = COMBINE_TILE

    n_idx = TOP_K * tm

    def idx_copy(tile, s):
        return pltpu.make_async_copy(dest_hbm.at[tile], idx_s.at[pl.ds(s * n_idx, n_idx)], isem.at[s])

    def gather(s):
        _issue_row_copies(lambda r: idx_s[s * n_idx + r], n_idx,
                          lambda r, d: _row_copy(y_hbm, d, ybuf.at[s], r, gsem.at[s]))

    @pl.when(i == 0)
    def _():
        c = idx_copy(0, 0)
        c.start()
        c.wait()
        gather(0)
        if n_tiles > 1:
            idx_copy(1, 1).start()

    @pl.when(i + 1 < n_tiles)
    def _():
        idx_copy(i + 1, 1 - slot).wait()
        gather(1 - slot)

    @pl.when(i + 2 < n_tiles)
    def _():
        idx_copy(i + 2, slot).start()

    h = _rowtile_load(hn_ref, tm).astype(BF16)
    sgate = jnp.dot(h, sg_ref[...], preferred_element_type=F32)
    sup = jnp.dot(h, su_ref[...], preferred_element_type=F32)
    shared = jnp.dot((sgate * jax.nn.sigmoid(sgate) * sup).astype(BF16), sd_ref[...],
                     preferred_element_type=F32)
    pltpu.make_async_copy(y_hbm.at[pl.ds(0, TOP_K * tm * ROW_CHUNKS), :], ybuf.at[slot],
                          gsem.at[slot]).wait()
    w = w_ref[...]
    routed = jnp.zeros((tm, D_MODEL), F32)
    for k in range(TOP_K):
        routed = routed + w[:, k:k + 1] * _rowtile_load(ybuf.at[slot], tm, base=k * tm * ROW_CHUNKS)
    o_ref[...] = x_ref[...] + mod_ref[0, 5:6, :] * (routed + shared)


def moe_combine(lay, x, hn, y_rows, dest, wts_t, sg_bf16, su_bf16, sd_bf16, mods):
    tm = COMBINE_TILE
    n_tiles = lay.t // tm
    mi = _mod_index(lay, tm)
    row = lambda i: (i, 0)
    const = lambda i: (0, 0)
    return pl.pallas_call(
        functools.partial(_combine_kernel, n_tiles=n_tiles),
        grid=(n_tiles,),
        in_specs=[pl.BlockSpec(memory_space=pl.ANY),
                  pl.BlockSpec(memory_space=pl.ANY),
                  pl.BlockSpec((tm, D_MODEL), row),
                  pl.BlockSpec((tm * ROW_CHUNKS, LANES), row),
                  pl.BlockSpec((tm, TOP_K), row),
                  pl.BlockSpec((D_MODEL, D_EXPERT), const),
                  pl.BlockSpec((D_MODEL, D_EXPERT), const),
                  pl.BlockSpec((D_EXPERT, D_MODEL), const),
                  pl.BlockSpec((1, N_MOD, D_MODEL), lambda i: (mi(i), 0, 0))],
        out_specs=pl.BlockSpec((tm, D_MODEL), row),
        out_shape=jax.ShapeDtypeStruct((lay.t, D_MODEL), F32),
        scratch_shapes=[pltpu.SMEM((2 * TOP_K * tm,), I32),
                        pltpu.VMEM((2, TOP_K * tm * ROW_CHUNKS, LANES), F32),
                        pltpu.SemaphoreType.DMA((2,)),
                        pltpu.SemaphoreType.DMA((2,))],
        compiler_params=_cparams(("arbitrary",)),
        name="moe_combine",
    )(dest, y_rows, x, hn, wts_t, sg_bf16, su_bf16, sd_bf16, mods)


def moe_layer(lay, x, g, mods, router_w, router_b, layer, w_gate, w_up, w_down, s_gate, s_up, s_down):
    t = lay.t
    hn, eidx, wts, rank, cnt = moe_router(lay, x, g, mods, router_w, router_b)
    counts = cnt[:, 0].astype(I32)
    n_blocks = (counts + MOE_BLOCK - 1) // MOE_BLOCK
    padded = n_blocks * MOE_BLOCK
    pad_end = jnp.cumsum(padded)
    pad_start = pad_end - padded
    n_rows = -(-(t * TOP_K + N_EXPERTS * (MOE_BLOCK - 1)) // MOE_BLOCK) * MOE_BLOCK
    dest = moe_dest(pad_start, eidx, rank)
    last_row = jnp.where(n_blocks > 0, pad_end - MOE_BLOCK, -1)
    tail_blk = pad_end[-1] // MOE_BLOCK + jnp.arange(n_rows // MOE_BLOCK - t * TOP_K // MOE_BLOCK)
    tail_row = jnp.where(tail_blk < n_rows // MOE_BLOCK, tail_blk * MOE_BLOCK, -1).astype(I32)
    xs = moe_dispatch(lay, hn, _tile_major(dest, DISPATCH_TILE),
                      jnp.concatenate([last_row, tail_row]), n_rows)
    y_rows = moe_experts(xs, pad_start // MOE_BLOCK, n_blocks, tail_row, layer, w_gate, w_up, w_down)
    return moe_combine(lay, x, hn, y_rows, _tile_major(dest, COMBINE_TILE), wts.T,
                       s_gate.astype(BF16), s_up.astype(BF16), s_down.astype(BF16), mods)


def _final_norm_kernel(x_ref, g_ref, o_ref):
    x = x_ref[...]
    ms = jnp.mean(x * x, axis=-1, keepdims=True)
    o_ref[...] = x * lax.rsqrt(ms + EPS) * g_ref[...]


def final_norm(x, g, row0, n_rows):
    base = row0 // ROW_TILE
    return pl.pallas_call(
        _final_norm_kernel,
        grid=(n_rows // ROW_TILE,),
        in_specs=[pl.BlockSpec((ROW_TILE, D_MODEL), lambda i: (base + i, 0)),
                  pl.BlockSpec((1, D_MODEL), lambda i: (0, 0))],
        out_specs=pl.BlockSpec((ROW_TILE, D_MODEL), lambda i: (i, 0)),
        out_shape=jax.ShapeDtypeStruct((n_rows, D_MODEL), F32),
        compiler_params=_cparams(("arbitrary",)),
        name="final_norm",
    )(x, g.reshape(1, -1))


def _block_diag(w):
    nb, bw, _ = w.shape
    eye = jnp.eye(nb, dtype=w.dtype)
    return (eye[:, None, :, None] * w[:, :, None, :]).reshape(nb * bw, nb * bw)


def even_layer(lay, x, mods, g_mix, p, state_lru, state_ssm_re, state_ssm_im):
    t = lay.t
    proj, u_g = modnorm_matmul(lay, x, g_mix, mods, 0, p['w_in'].astype(BF16),
                               ug_col=2 * D_LRU)
    zeros_c = jnp.zeros((lay.n_ctx, D_LRU), F32)
    hf_y, st = None, []
    for d in range(2):
        wg = jnp.concatenate([_block_diag(p['lru_wa'][d]), _block_diag(p['lru_wx'][d])], axis=1)
        bg = jnp.concatenate([p['lru_ba'][d], p['lru_bx'][d]])
        h0 = jnp.concatenate([zeros_c, state_lru[:, d].astype(F32)], axis=0)
        hf_y, s = lru_pass(lay, proj, p['conv_w'], p['conv_b'], wg.astype(BF16), bg,
                           p['lru_lam'][d], h0, reverse=(d == 1), hf=hf_y)
        st.append(s[:lay.n_ctx])
    y_a = hf_y
    new_lru = jnp.stack(st, axis=1)

    mats = _s5_matrices(p['a_re'], p['a_im'], p['log_dt'], p['b_re'], p['b_im'], p['c_re'], p['c_im'])
    h0 = jnp.concatenate([state_ssm_re, state_ssm_im], axis=-1).astype(F32)
    h0 = h0.transpose(2, 1, 0, 3)
    y_g, h_ctx = s5_mixer(lay, u_g, mats, h0)
    seg = lay.s_ctx // S5_CHUNK
    h_ctx = h_ctx.reshape(SSM_GROUPS, 2, lay.n_ctx, seg, 2 * SSM_STATE)
    ends = jnp.stack([h_ctx[:, 0, :, seg - 1], h_ctx[:, 1, :, 0]], axis=1)
    ends = ends.transpose(2, 1, 0, 3)
    x = even_out(lay, x, y_a, y_g, proj, p['d'], p['glu_w'].astype(BF16), p['glu_b'],
                 p['w_out'].astype(BF16), mods)
    return x, new_lru, ends[..., :SSM_STATE], ends[..., SSM_STATE:]


def odd_layer(lay, x, mods, g_mix, w_qkv, sink, w_out, cache_k, cache_v):
    qkv = modnorm_matmul(lay, x, g_mix, mods, 0, w_qkv.astype(BF16))
    o_ctx = attn_context(lay, qkv, sink)
    o_lat = attn_latent(lay, qkv, cache_k, cache_v, sink)
    nq = N_HEADS * HEAD_DIM
    kv = qkv[:lay.t_ctx, nq:].reshape(lay.n_ctx, lay.s_ctx, 2, N_KV, HEAD_DIM)
    k_new = kv[:, :, 0].swapaxes(1, 2)
    v_new = kv[:, :, 1].swapaxes(1, 2)
    x = matmul_residual(lay, x, o_ctx, o_lat, w_out.astype(BF16), mods)
    return x, k_new, v_new


def _forward(lay, x_prompt, x_sample, state_lru, state_ssm_re, state_ssm_im, cache_k, cache_v,
             c, c_ctx, g_mix, g_ffn, w_mod, b_mod,
             ev_w_in, lru_conv_w, lru_conv_b, lru_wa, lru_ba, lru_wx, lru_bx, lru_lam,
             ssm_a_re, ssm_a_im, ssm_log_dt, ssm_b_re, ssm_b_im, ssm_c_re, ssm_c_im, ssm_d,
             ssm_glu_w, ssm_glu_b, ev_w_out, at_w_qkv, at_sink, at_w_out,
             router_w, router_b, exp_w_gate, exp_w_up, exp_w_down, sh_w_gate, sh_w_up, sh_w_down,
             g_final):
    depth = g_mix.shape[0]
    x = jnp.concatenate([x_prompt.reshape(lay.t_ctx, D_MODEL), x_sample.reshape(lay.t_lat, D_MODEL)],
                        axis=0)
    n_c = 1 + lay.n_lat
    c_rows = jnp.concatenate([c_ctx[None, :], c, jnp.zeros((16 - n_c, D_MODEL), F32)], axis=0)
    new_lru, new_re, new_im, new_k, new_v = [], [], [], [], []
    for l in range(depth):
        i = l // 2
        mods = adaln_table(c_rows, l, w_mod, b_mod[l])
        if l % 2 == 0:
            p = dict(w_in=ev_w_in[i], conv_w=lru_conv_w[i], conv_b=lru_conv_b[i],
                     lru_wa=lru_wa[i], lru_ba=lru_ba[i], lru_wx=lru_wx[i], lru_bx=lru_bx[i],
                     lru_lam=lru_lam[i], a_re=ssm_a_re[i], a_im=ssm_a_im[i], log_dt=ssm_log_dt[i],
                     b_re=ssm_b_re[i], b_im=ssm_b_im[i], c_re=ssm_c_re[i], c_im=ssm_c_im[i],
                     d=ssm_d[i], glu_w=ssm_glu_w[i], glu_b=ssm_glu_b[i], w_out=ev_w_out[i])
            x, lru_i, re_i, im_i = even_layer(lay, x, mods, g_mix[l], p, state_lru[:, i],
                                              state_ssm_re[:, i], state_ssm_im[:, i])
            new_lru.append(lru_i)
            new_re.append(re_i)
            new_im.append(im_i)
        else:
            x, k_i, v_i = odd_layer(lay, x, mods, g_mix[l], at_w_qkv[i], at_sink[i], at_w_out[i],
                                    cache_k[:, i], cache_v[:, i])
            new_k.append(k_i)
            new_v.append(v_i)
        x = moe_layer(lay, x, g_ffn[l], mods, router_w[l], router_b[l], l, exp_w_gate, exp_w_up,
                      exp_w_down, sh_w_gate[l], sh_w_up[l], sh_w_down[l])
    y_prompt = final_norm(x, g_final, 0, lay.t_ctx).reshape(x_prompt.shape)
    y_sample = final_norm(x, g_final, lay.t_ctx, lay.t_lat).reshape(x_sample.shape)
    return (y_prompt, y_sample, jnp.stack(new_lru, axis=1), jnp.stack(new_re, axis=1),
            jnp.stack(new_im, axis=1), jnp.stack(new_k, axis=1), jnp.stack(new_v, axis=1))


def kernel(x_prompt, x_sample, state_lru, state_ssm_re, state_ssm_im, cache_k, cache_v, c, c_ctx, g_mix, g_ffn, w_mod, b_mod, ev_w_in, lru_conv_w, lru_conv_b, lru_wa, lru_ba, lru_wx, lru_bx, lru_lam, ssm_a_re, ssm_a_im, ssm_log_dt, ssm_b_re, ssm_b_im, ssm_c_re, ssm_c_im, ssm_d, ssm_glu_w, ssm_glu_b, ev_w_out, at_w_qkv, at_sink, at_w_out, router_w, router_b, exp_w_gate, exp_w_up, exp_w_down, sh_w_gate, sh_w_up, sh_w_down, g_final):
    lay = Layout(n_ctx=x_prompt.shape[0], s_ctx=x_prompt.shape[1],
                 n_lat=x_sample.shape[0], s_lat=x_sample.shape[1])
    return _forward(lay, x_prompt, x_sample, state_lru, state_ssm_re, state_ssm_im, cache_k, cache_v,
                    c, c_ctx, g_mix, g_ffn, w_mod, b_mod,
                    ev_w_in, lru_conv_w, lru_conv_b, lru_wa, lru_ba, lru_wx, lru_bx, lru_lam,
                    ssm_a_re, ssm_a_im, ssm_log_dt, ssm_b_re, ssm_b_im, ssm_c_re, ssm_c_im, ssm_d,
                    ssm_glu_w, ssm_glu_b, ev_w_out, at_w_qkv, at_sink, at_w_out,
                    router_w, router_b, exp_w_gate, exp_w_up, exp_w_down, sh_w_gate, sh_w_up,
                    sh_w_down, g_final)
```

```python
import functools
from typing import NamedTuple

import jax
import jax.numpy as jnp
from jax import lax
from jax.experimental import pallas as pl
from jax.experimental.pallas import tpu as pltpu

F32 = jnp.float32
BF16 = jnp.bfloat16
I32 = jnp.int32
HIGHEST = lax.Precision.HIGHEST

D_MODEL = 1024
EPS = 1e-6
N_MOD = 6
GRID_W = 64
D_LRU = 512
LRU_BLOCKS = 8
LRU_C = 8.0
CONV_W = 4
CONV_LEFT = 2
D_SSM = 512
SSM_GROUP = 16
SSM_GROUPS = 32
SSM_STATE = 64
S5_CHUNK = 16
S5_LANES = S5_CHUNK * SSM_GROUP
S5_SCAN_STEPS = 8
HEAD_DIM = 64
N_HEADS = 16
N_KV = 4
GQA = 4
WINDOW = 128
Q_BLOCK = 128
ROPE_BASE = 10000.0
ATTN_SCALE = HEAD_DIM ** -0.5
NEG_INF = -1e30
N_EXPERTS = 256
TOP_K = 8
N_GROUPS = 8
TOPK_GROUPS = 4
GROUP_SIZE = N_EXPERTS // N_GROUPS
D_EXPERT = 256
ROUTE_SCALE = 2.5
MOE_BLOCK = 128

SUBLANES = 8
LANES = 128
ROW_CHUNKS = D_MODEL // LANES
SEQ_TILE = 256
ROW_TILE = 512
ROUTER_TILE = 512
DEST_TILE = 1024
DISPATCH_TILE = 256
COMBINE_TILE = 128
EXPERT_X_BUFS = 8
EXPERT_Y_BUFS = 8
VMEM_LIMIT = 56 * 1024 * 1024


class Layout(NamedTuple):
    n_ctx: int
    s_ctx: int
    n_lat: int
    s_lat: int

    @property
    def t_ctx(self):
        return self.n_ctx * self.s_ctx

    @property
    def t_lat(self):
        return self.n_lat * self.s_lat

    @property
    def t(self):
        return self.t_ctx + self.t_lat

    @property
    def n_seq(self):
        return self.n_ctx + self.n_lat


def _cparams(sem):
    return pltpu.CompilerParams(dimension_semantics=sem, vmem_limit_bytes=VMEM_LIMIT)


def _mod_index(lay, tile_rows):
    n_ctx_tiles = lay.t_ctx // tile_rows
    per_lat = lay.s_lat // tile_rows

    def f(i):
        return jnp.where(i < n_ctx_tiles, 0, 1 + (i - n_ctx_tiles) // per_lat)
    return f


def _adaln_kernel(c_ref, w_ref, b_ref, o_ref):
    c = c_ref[...]
    s = c * jax.nn.sigmoid(c)
    o_ref[...] = jnp.dot(s, w_ref[0], precision=HIGHEST, preferred_element_type=F32) + b_ref[...]


def adaln_table(c_rows, layer, w_mod, b_mod):
    n = c_rows.shape[0]
    tn = 1536
    out = pl.pallas_call(
        _adaln_kernel,
        grid=(N_MOD * D_MODEL // tn,),
        in_specs=[pl.BlockSpec((n, D_MODEL), lambda j: (0, 0)),
                  pl.BlockSpec((1, D_MODEL, tn), lambda j: (layer, 0, j)),
                  pl.BlockSpec((1, tn), lambda j: (0, j))],
        out_specs=pl.BlockSpec((n, tn), lambda j: (0, j)),
        out_shape=jax.ShapeDtypeStruct((n, N_MOD * D_MODEL), F32),
        compiler_params=_cparams(("arbitrary",)),
        name="adaln",
    )(c_rows, w_mod, b_mod.reshape(1, -1))
    return out.reshape(n, N_MOD, D_MODEL)


def _modnorm(x, g, mod_ref, slot):
    ms = jnp.mean(x * x, axis=-1, keepdims=True)
    y = x * lax.rsqrt(ms + EPS) * g
    shift = mod_ref[0, slot:slot + 1, :]
    scale = mod_ref[0, slot + 1:slot + 2, :]
    return y * (1.0 + scale) + shift


GROUPS_PER_VREG = LANES // SSM_GROUP
SSM_COL_BLOCKS = D_SSM // LANES


def _group_major_store(val, tmp_ref, dst_ref):
    rows = dst_ref.shape[1]
    for j in range(SSM_COL_BLOCKS):
        tmp_ref[j] = val[:, j * LANES:(j + 1) * LANES]
    for j in range(SSM_COL_BLOCKS):
        steps = [tmp_ref[j, pl.ds(i, rows, stride=S5_CHUNK), :] for i in range(S5_CHUNK)]
        for q in range(GROUPS_PER_VREG):
            dst_ref[j * GROUPS_PER_VREG + q] = jnp.concatenate(
                [w[:, q * SSM_GROUP:(q + 1) * SSM_GROUP] for w in steps], axis=1)


def _group_major_load(src_ref, tmp_ref):
    rows = src_ref.shape[1]
    for j in range(SSM_COL_BLOCKS):
        blocks = [src_ref[j * GROUPS_PER_VREG + q] for q in range(GROUPS_PER_VREG)]
        for i in range(S5_CHUNK):
            tmp_ref[j, pl.ds(i, rows, stride=S5_CHUNK), :] = jnp.concatenate(
                [b[:, i * SSM_GROUP:(i + 1) * SSM_GROUP] for b in blocks], axis=1)
    return jnp.concatenate([tmp_ref[j] for j in range(SSM_COL_BLOCKS)], axis=1)


def _modnorm_mm_kernel(x_ref, g_ref, mod_ref, w_ref, o_ref, *ug_refs, slot, ug_col):
    h = _modnorm(x_ref[...], g_ref[...], mod_ref, slot)
    out = jnp.dot(h.astype(BF16), w_ref[...], preferred_element_type=F32)
    o_ref[...] = out
    if ug_col is not None:
        ug_ref, tmp_ref = ug_refs
        _group_major_store(out[:, ug_col:ug_col + D_SSM], tmp_ref, ug_ref)


def modnorm_matmul(lay, x, g, mods, slot, w_bf16, ug_col=None):
    t = lay.t
    n = w_bf16.shape[1]
    mi = _mod_index(lay, ROW_TILE)
    out_specs = [pl.BlockSpec((ROW_TILE, n), lambda i: (i, 0))]
    out_shape = [jax.ShapeDtypeStruct((t, n), F32)]
    if ug_col is not None:
        out_specs.append(pl.BlockSpec((SSM_GROUPS, ROW_TILE // S5_CHUNK, S5_LANES), lambda i: (0, i, 0)))
        out_shape.append(jax.ShapeDtypeStruct((SSM_GROUPS, t // S5_CHUNK, S5_LANES), F32))
    outs = pl.pallas_call(
        functools.partial(_modnorm_mm_kernel, slot=slot, ug_col=ug_col),
        grid=(t // ROW_TILE,),
        in_specs=[pl.BlockSpec((ROW_TILE, D_MODEL), lambda i: (i, 0)),
                  pl.BlockSpec((1, D_MODEL), lambda i: (0, 0)),
                  pl.BlockSpec((1, N_MOD, D_MODEL), lambda i: (mi(i), 0, 0)),
                  pl.BlockSpec((D_MODEL, n), lambda i: (0, 0))],
        out_specs=out_specs,
        out_shape=out_shape,
        scratch_shapes=([pltpu.VMEM((SSM_COL_BLOCKS, ROW_TILE, LANES), F32)]
                        if ug_col is not None else []),
        compiler_params=_cparams(("arbitrary",)),
        name="modnorm_matmul",
    )(x, g.reshape(1, -1), mods, w_bf16)
    return outs if ug_col is not None else outs[0]


def _seq_tile_maps(lay, reverse):
    assert lay.s_ctx == SEQ_TILE and lay.s_lat % SEQ_TILE == 0
    n_tiles = lay.t // SEQ_TILE
    per_lat = lay.s_lat // SEQ_TILE

    def tile(i):
        return (n_tiles - 1 - i) if reverse else i

    def seq(i):
        ti = tile(i)
        return jnp.where(ti < lay.n_ctx, ti, lay.n_ctx + (ti - lay.n_ctx) // per_lat)

    return n_tiles, tile, seq


def _softplus(x):
    return jnp.maximum(x, 0.0) + jnp.log(1.0 + jnp.exp(-jnp.abs(x)))


def _lru_kernel(rec_ref, prev_ref, next_ref, cw_ref, cb_ref, wg_ref, bg_ref, lam_ref, h0_ref,
                *rest, reverse, n_ctx, per_lat, n_tiles):
    if reverse:
        gate_ref, hf_ref, y_ref, st_ref, a_s, b_s, h_s, carry = rest
    else:
        y_ref, st_ref, a_s, b_s, h_s, carry = rest
    i = pl.program_id(0)
    ti = (n_tiles - 1 - i) if reverse else i
    is_first = jnp.logical_or(ti < n_ctx, (ti - n_ctx) % per_lat == 0)
    is_last = jnp.logical_or(ti < n_ctx, (ti - n_ctx) % per_lat == per_lat - 1)
    ts = SEQ_TILE

    rec = rec_ref[...]
    prev = jnp.where(is_first, 0.0, prev_ref[...])
    nxt = jnp.where(is_last, 0.0, next_ref[...])
    ext = jnp.concatenate([prev, rec, nxt], axis=0)
    n_ext = ts + 2 * SUBLANES
    cw = cw_ref[...]
    xc = cb_ref[...] + cw[2:3, :] * rec
    xc = xc + cw[0:1, :] * pltpu.roll(ext, 2, 0)[SUBLANES:SUBLANES + ts]
    xc = xc + cw[1:2, :] * pltpu.roll(ext, 1, 0)[SUBLANES:SUBLANES + ts]
    xc = xc + cw[3:4, :] * pltpu.roll(ext, n_ext - 1, 0)[SUBLANES:SUBLANES + ts]

    gates = jax.nn.sigmoid(jnp.dot(xc.astype(BF16), wg_ref[...], preferred_element_type=F32)
                           + bg_ref[...])
    r = gates[:, :D_LRU]
    ig = gates[:, D_LRU:]
    log_a = (-LRU_C) * r * _softplus(-lam_ref[...])
    a = jnp.exp(log_a)
    b = jnp.sqrt(1.0 - jnp.exp(2.0 * log_a)) * (ig * xc)

    row8 = lax.broadcasted_iota(I32, (ts, D_LRU), 0) % SUBLANES
    for sh in (1, 2, 4):
        if reverse:
            keep = row8 < SUBLANES - sh
            a_sh = pltpu.roll(a, ts - sh, 0)
            b_sh = pltpu.roll(b, ts - sh, 0)
        else:
            keep = row8 >= sh
            a_sh = pltpu.roll(a, sh, 0)
            b_sh = pltpu.roll(b, sh, 0)
        b = b + a * jnp.where(keep, b_sh, 0.0)
        a = a * jnp.where(keep, a_sh, 1.0)
    a_s[...] = a
    b_s[...] = b

    @pl.when(is_last if reverse else is_first)
    def _():
        carry[...] = h0_ref[0]

    n_grp = ts // SUBLANES

    def body(k, c):
        gi = (n_grp - 1 - k) if reverse else k
        sl = pl.ds(pl.multiple_of(gi * SUBLANES, SUBLANES), SUBLANES)
        h = b_s[sl, :] + a_s[sl, :] * c
        h_s[sl, :] = h
        return h[0:1, :] if reverse else h[SUBLANES - 1:SUBLANES, :]

    c_fin = lax.fori_loop(0, n_grp, body, carry[...], unroll=4)
    carry[...] = c_fin
    st_ref[0] = c_fin
    if reverse:
        y_ref[...] = (hf_ref[...] + h_s[...]) * jax.nn.gelu(gate_ref[...])
    else:
        y_ref[...] = h_s[...]


def lru_pass(lay, proj, conv_w, conv_b, wg_bf16, bg, lam, h0, reverse, hf=None):
    n_tiles, tile, seq = _seq_tile_maps(lay, reverse)
    per_lat = lay.s_lat // SEQ_TILE
    blk8 = SEQ_TILE // SUBLANES
    last8 = lay.t // SUBLANES - 1
    c = D_LRU
    in_specs = [
        pl.BlockSpec((SEQ_TILE, c), lambda i: (tile(i), 1)),
        pl.BlockSpec((SUBLANES, c), lambda i: (jnp.maximum(tile(i) * blk8 - 1, 0), 1)),
        pl.BlockSpec((SUBLANES, c), lambda i: (jnp.minimum(tile(i) * blk8 + blk8, last8), 1)),
        pl.BlockSpec((CONV_W, c), lambda i: (0, 0)),
        pl.BlockSpec((1, c), lambda i: (0, 0)),
        pl.BlockSpec((c, 2 * c), lambda i: (0, 0)),
        pl.BlockSpec((1, 2 * c), lambda i: (0, 0)),
        pl.BlockSpec((1, c), lambda i: (0, 0)),
        pl.BlockSpec((1, 1, c), lambda i: (seq(i), 0, 0)),
    ]
    args = [proj, proj, proj, conv_w, conv_b.reshape(1, -1), wg_bf16, bg.reshape(1, -1),
            lam.reshape(1, -1), h0.reshape(lay.n_seq, 1, c)]
    if reverse:
        in_specs += [pl.BlockSpec((SEQ_TILE, c), lambda i: (tile(i), 0)),
                     pl.BlockSpec((SEQ_TILE, c), lambda i: (tile(i), 0))]
        args += [proj, hf]
    y, st = pl.pallas_call(
        functools.partial(_lru_kernel, reverse=reverse, n_ctx=lay.n_ctx, per_lat=per_lat,
                          n_tiles=n_tiles),
        grid=(n_tiles,),
        in_specs=in_specs,
        out_specs=[pl.BlockSpec((SEQ_TILE, c), lambda i: (tile(i), 0)),
                   pl.BlockSpec((1, 1, c), lambda i: (seq(i), 0, 0))],
        out_shape=[jax.ShapeDtypeStruct((lay.t, c), F32),
                   jax.ShapeDtypeStruct((lay.n_seq, 1, c), F32)],
        scratch_shapes=[pltpu.VMEM((SEQ_TILE, c), F32), pltpu.VMEM((SEQ_TILE, c), F32),
                        pltpu.VMEM((SEQ_TILE, c), F32), pltpu.VMEM((1, c), F32)],
        compiler_params=_cparams(("arbitrary",)),
        name="lru_bwd" if reverse else "lru_fwd",
    )(*args)
    return y, st.reshape(lay.n_seq, c)


def _cmul(a, b):
    return a[0] * b[0] - a[1] * b[1], a[0] * b[1] + a[1] * b[0]


def _s5_matrices(a_re, a_im, log_dt, b_re, b_im, c_re, c_im):
    a_re, a_im = a_re.astype(F32), a_im.astype(F32)
    dt = jnp.exp(log_dt.astype(F32))[..., None]
    z = (a_re * dt, a_im * dt)

    def zpow(k):
        k = k.reshape((-1,) + (1,) * z[0].ndim)
        mag = jnp.exp(k * z[0][None])
        return mag * jnp.cos(k * z[1][None]), mag * jnp.sin(k * z[1][None])

    a_bar = zpow(jnp.ones((1,), F32))
    a_bar = (a_bar[0][0], a_bar[1][0])
    den = a_re * a_re + a_im * a_im
    xr, xi = a_bar[0] - 1.0, a_bar[1]
    q = ((xr * a_re + xi * a_im) / den, (xi * a_re - xr * a_im) / den)
    b_bar = _cmul((q[0][..., None], q[1][..., None]), (b_re.astype(F32), b_im.astype(F32)))
    cc = (c_re.astype(F32), c_im.astype(F32))
    el = S5_CHUNK
    pw = zpow(jnp.arange(el + 1, dtype=F32))
    idx = jnp.arange(el)
    m_in, m_toep, m_out = [], [], []
    for d in range(2):
        p_d = (pw[0][:, d], pw[1][:, d])
        b_d = (b_bar[0][d], b_bar[1][d])
        c_d = (cc[0][d], cc[1][d])
        k_in = (el - 1 - idx) if d == 0 else idx
        w_in = _cmul((p_d[0][k_in][..., None], p_d[1][k_in][..., None]),
                     (b_d[0][None], b_d[1][None]))
        w_in = [jnp.transpose(w, (1, 0, 3, 2)).reshape(SSM_GROUPS, S5_LANES, SSM_STATE) for w in w_in]
        m_in.append(jnp.concatenate(w_in, axis=-1))
        cp = _cmul((c_d[0][None], c_d[1][None]),
                   (p_d[0][:, :, None, :], p_d[1][:, :, None, :]))
        kern = (jnp.einsum('kghp,gpc->kgch', cp[0][:el], b_d[0])
                - jnp.einsum('kghp,gpc->kgch', cp[1][:el], b_d[1]))
        diff = (idx[None, :] - idx[:, None]) if d == 0 else (idx[:, None] - idx[None, :])
        blocks = jnp.where((diff >= 0)[:, :, None, None, None],
                           kern[jnp.clip(diff, 0, el - 1)], 0.0)
        m_toep.append(jnp.transpose(blocks, (2, 0, 3, 1, 4)).reshape(SSM_GROUPS, S5_LANES, S5_LANES))
        k_out = (idx + 1) if d == 0 else (el - idx)
        w_out = [jnp.transpose(w[k_out], (1, 3, 0, 2)).reshape(SSM_GROUPS, SSM_STATE, S5_LANES)
                 for w in cp]
        m_out.append(jnp.concatenate([w_out[0], -w_out[1]], axis=1))
    mul = zpow(el * 2.0 ** jnp.arange(S5_SCAN_STEPS, dtype=F32))
    mul = [jnp.transpose(m, (2, 1, 0, 3)) for m in mul]
    coef_a = jnp.concatenate([mul[0], mul[0]], axis=-1)
    coef_b = jnp.concatenate([-mul[1], mul[1]], axis=-1)
    stack = lambda xs: jnp.stack(xs, axis=1)
    return (stack(m_in).astype(BF16), stack(m_toep).astype(BF16), stack(m_out).astype(BF16),
            coef_a, coef_b)


def _s5_scan(v, ca, cb, seg, reverse):
    n = v.shape[0]
    assert seg <= 2 ** S5_SCAN_STEPS
    row = lax.broadcasted_iota(I32, (n, 2 * SSM_STATE), 0) % seg
    k, sh = 0, 1
    while sh < seg:
        if reverse:
            s = jnp.where(row < seg - sh, pltpu.roll(v, n - sh, 0), 0.0)
        else:
            s = jnp.where(row >= sh, pltpu.roll(v, sh, 0), 0.0)
        v = v + ca[k:k + 1, :] * s + cb[k:k + 1, :] * pltpu.roll(s, SSM_STATE, 1)
        k += 1
        sh *= 2
    return v


def _s5_shift(h, seg, reverse):
    n = h.shape[0]
    row = lax.broadcasted_iota(I32, (n, 2 * SSM_STATE), 0) % seg
    if reverse:
        return jnp.where(row < seg - 1, pltpu.roll(h, n - 1, 0), 0.0)
    return jnp.where(row >= 1, pltpu.roll(h, 1, 0), 0.0)


def _s5_kernel(u_ref, min_ref, mtoep_ref, mout_ref, ca_ref, cb_ref, h0_ref, y_ref, hc_ref,
               v_s, hp_s, *, rc, seg_c, n_lat, seg_l):
    u = u_ref[0].astype(BF16)
    u_c, u_l = u[:rc], u[rc:]
    y_c = jnp.zeros((rc, S5_LANES), F32)
    y_l = jnp.zeros((n_lat * seg_l, S5_LANES), F32)
    for d in range(2):
        reverse = d == 1
        ca = ca_ref[0, d]
        cb = cb_ref[0, d]
        m_in = min_ref[0, d]
        m_toep = mtoep_ref[0, d]
        m_out = mout_ref[0, d]
        h_c = _s5_scan(jnp.dot(u_c, m_in, preferred_element_type=F32), ca, cb, seg_c, reverse)
        hc_ref[0, d] = h_c
        hp_c = _s5_shift(h_c, seg_c, reverse)
        y_c = y_c + jnp.dot(u_c, m_toep, preferred_element_type=F32)
        y_c = y_c + jnp.dot(hp_c.astype(BF16), m_out, preferred_element_type=F32)
        v_s[...] = jnp.dot(u_l, m_in, preferred_element_type=F32)
        for s in range(n_lat):
            h0 = h0_ref[0, d, s:s + 1, :]
            r0 = s * seg_l + (seg_l - 1 if reverse else 0)
            v_s[r0:r0 + 1, :] = (v_s[r0:r0 + 1, :] + ca[0:1, :] * h0
                                 + cb[0:1, :] * pltpu.roll(h0, SSM_STATE, 1))
        h_l = _s5_scan(v_s[...], ca, cb, seg_l, reverse)
        hp_s[...] = _s5_shift(h_l, seg_l, reverse)
        for s in range(n_lat):
            r0 = s * seg_l + (seg_l - 1 if reverse else 0)
            hp_s[r0:r0 + 1, :] = h0_ref[0, d, s:s + 1, :]
        y_l = y_l + jnp.dot(u_l, m_toep, preferred_element_type=F32)
        y_l = y_l + jnp.dot(hp_s[...].astype(BF16), m_out, preferred_element_type=F32)
    y_ref[0, :rc, :] = y_c
    y_ref[0, rc:, :] = y_l


def s5_mixer(lay, u_g, mats, h0):
    m_in, m_toep, m_out, coef_a, coef_b = mats
    rows = lay.t // S5_CHUNK
    rc = lay.t_ctx // S5_CHUNK
    rl = rows - rc
    st2 = 2 * SSM_STATE
    g4 = lambda g: (g, 0, 0, 0)
    return pl.pallas_call(
        functools.partial(_s5_kernel, rc=rc, seg_c=lay.s_ctx // S5_CHUNK, n_lat=lay.n_lat,
                          seg_l=lay.s_lat // S5_CHUNK),
        grid=(SSM_GROUPS,),
        in_specs=[pl.BlockSpec((1, rows, S5_LANES), lambda g: (g, 0, 0)),
                  pl.BlockSpec((1, 2, S5_LANES, st2), g4),
                  pl.BlockSpec((1, 2, S5_LANES, S5_LANES), g4),
                  pl.BlockSpec((1, 2, st2, S5_LANES), g4),
                  pl.BlockSpec((1, 2, S5_SCAN_STEPS, st2), g4),
                  pl.BlockSpec((1, 2, S5_SCAN_STEPS, st2), g4),
                  pl.BlockSpec((1, 2, lay.n_lat, st2), g4)],
        out_specs=[pl.BlockSpec((1, rows, S5_LANES), lambda g: (g, 0, 0)),
                   pl.BlockSpec((1, 2, rc, st2), g4)],
        out_shape=[jax.ShapeDtypeStruct((SSM_GROUPS, rows, S5_LANES), F32),
                   jax.ShapeDtypeStruct((SSM_GROUPS, 2, rc, st2), F32)],
        scratch_shapes=[pltpu.VMEM((rl, st2), F32), pltpu.VMEM((rl, st2), F32)],
        compiler_params=_cparams(("arbitrary",)),
        name="s5_mixer",
    )(u_g, m_in, m_toep, m_out, coef_a, coef_b, h0)


def _even_out_kernel(x_ref, ya_ref, yg_ref, u_ref, d_ref, gw_ref, gb_ref, w_ref, mod_ref, o_ref, yt_s):
    ys = _group_major_load(yg_ref, yt_s) + d_ref[...] * u_ref[...]
    g = jax.nn.gelu(ys)
    yb = g * jax.nn.sigmoid(jnp.dot(g.astype(BF16), gw_ref[...], preferred_element_type=F32)
                            + gb_ref[...])
    out = jnp.dot(ya_ref[...].astype(BF16), w_ref[:D_LRU, :], preferred_element_type=F32)
    out = out + jnp.dot(yb.astype(BF16), w_ref[D_LRU:, :], preferred_element_type=F32)
    o_ref[...] = x_ref[...] + mod_ref[0, 2:3, :] * out


def even_out(lay, x, y_a, y_g, proj, ssm_d, glu_w_bf16, glu_b, w_out_bf16, mods):
    mi = _mod_index(lay, ROW_TILE)
    c = D_SSM
    row = lambda i: (i, 0)
    const = lambda i: (0, 0)
    return pl.pallas_call(
        _even_out_kernel,
        grid=(lay.t // ROW_TILE,),
        in_specs=[pl.BlockSpec((ROW_TILE, D_MODEL), row),
                  pl.BlockSpec((ROW_TILE, c), row),
                  pl.BlockSpec((SSM_GROUPS, ROW_TILE // S5_CHUNK, S5_LANES), lambda i: (0, i, 0)),
                  pl.BlockSpec((ROW_TILE, c), lambda i: (i, 2)),
                  pl.BlockSpec((1, c), const),
                  pl.BlockSpec((c, c), const),
                  pl.BlockSpec((1, c), const),
                  pl.BlockSpec((D_MODEL, D_MODEL), const),
                  pl.BlockSpec((1, N_MOD, D_MODEL), lambda i: (mi(i), 0, 0))],
        out_specs=pl.BlockSpec((ROW_TILE, D_MODEL), row),
        out_shape=jax.ShapeDtypeStruct((lay.t, D_MODEL), F32),
        scratch_shapes=[pltpu.VMEM((SSM_COL_BLOCKS, ROW_TILE, LANES), F32)],
        compiler_params=_cparams(("arbitrary",)),
        name="even_out",
    )(x, y_a, y_g, proj, ssm_d.reshape(1, -1), glu_w_bf16, glu_b.reshape(1, -1), w_out_bf16, mods)


def _softmax_pv(parts, sink_col):
    m = sink_col
    for s, _ in parts:
        m = jnp.maximum(m, jnp.max(s, axis=-1, keepdims=True))
    den = jnp.exp(sink_col - m)
    acc = None
    for s, v in parts:
        p = jnp.exp(s - m)
        den = den + jnp.sum(p, axis=-1, keepdims=True)
        pv = jnp.dot(p.astype(BF16), v.astype(BF16), preferred_element_type=F32)
        acc = pv if acc is None else acc + pv
    return acc / den


def _nt_dot(a, b):
    return lax.dot_general(a.astype(BF16), b.astype(BF16), (((1,), (1,)), ((), ())),
                           preferred_element_type=F32)


def _attn_ctx_kernel(q_ref, k_ref, v_ref, sink_ref, o_ref):
    n = q_ref.shape[0]
    for kh in range(N_KV):
        k = k_ref[:, kh * HEAD_DIM:(kh + 1) * HEAD_DIM]
        v = v_ref[:, kh * HEAD_DIM:(kh + 1) * HEAD_DIM]
        for g in range(GQA):
            h = kh * GQA + g
            q = q_ref[:, h * HEAD_DIM:(h + 1) * HEAD_DIM]
            s = _nt_dot(q, k) * ATTN_SCALE
            sink = jnp.broadcast_to(sink_ref[0:1, h:h + 1], (n, 1))
            o_ref[:, h * HEAD_DIM:(h + 1) * HEAD_DIM] = _softmax_pv([(s, v)], sink)


def attn_context(lay, qkv, sink):
    nq = N_HEADS * HEAD_DIM
    nkv = N_KV * HEAD_DIM
    return pl.pallas_call(
        _attn_ctx_kernel,
        grid=(lay.n_ctx,),
        in_specs=[pl.BlockSpec((lay.s_ctx, nq), lambda b: (b, 0)),
                  pl.BlockSpec((lay.s_ctx, nkv), lambda b: (b, nq // nkv)),
                  pl.BlockSpec((lay.s_ctx, nkv), lambda b: (b, nq // nkv + 1)),
                  pl.BlockSpec((1, N_HEADS), lambda b: (0, 0))],
        out_specs=pl.BlockSpec((lay.s_ctx, nq), lambda b: (b, 0)),
        out_shape=jax.ShapeDtypeStruct((lay.t_ctx, nq), F32),
        compiler_params=_cparams(("arbitrary",)),
        name="attn_context",
    )(qkv, qkv, qkv, sink.reshape(1, -1))


def _rope(x, cos, sin):
    lane = lax.broadcasted_iota(I32, (x.shape[0], 2 * HEAD_DIM), 1) % HEAD_DIM
    outs = []
    for j in range(x.shape[1] // (2 * HEAD_DIM)):
        xs = x[:, j * 2 * HEAD_DIM:(j + 1) * 2 * HEAD_DIM]
        sw = jnp.where(lane < HEAD_DIM // 2,
                       pltpu.roll(xs, 2 * HEAD_DIM - HEAD_DIM // 2, 1),
                       pltpu.roll(xs, HEAD_DIM // 2, 1))
        outs.append(xs * cos + sw * sin)
    return outs


def _attn_lat_kernel(q_ref, k0_ref, k1_ref, k2_ref, v0_ref, v1_ref, v2_ref, ck_ref, cv_ref,
                     cq_ref, sq_ref, c0_ref, c1_ref, c2_ref, s0_ref, s1_ref, s2_ref, sink_ref,
                     o_ref, *, n_blk):
    j = pl.program_id(1)
    qb = Q_BLOCK
    q_parts = [qp * ATTN_SCALE for qp in _rope(q_ref[...], cq_ref[...], sq_ref[...])]
    k_parts = [_rope(kr[...], cr[...], sr[...])
               for kr, cr, sr in ((k0_ref, c0_ref, s0_ref), (k1_ref, c1_ref, s1_ref),
                                  (k2_ref, c2_ref, s2_ref))]
    qi = lax.broadcasted_iota(I32, (qb, 3 * qb), 0)
    km = lax.broadcasted_iota(I32, (qb, 3 * qb), 1)
    kpos = j * qb - qb + km
    mask1 = (jnp.abs(km - qb - qi) <= WINDOW) & (kpos >= 0) & (kpos < n_blk * qb)
    mask = jnp.concatenate([mask1] * GQA, axis=0)
    for kh in range(N_KV):
        half = (kh % 2) * HEAD_DIM
        k_loc = jnp.concatenate([kp[kh // 2][:, half:half + HEAD_DIM] for kp in k_parts], axis=0)
        v_loc = jnp.concatenate([vr[:, kh * HEAD_DIM:(kh + 1) * HEAD_DIM]
                                 for vr in (v0_ref, v1_ref, v2_ref)], axis=0)
        qs, sinks = [], []
        for g in range(GQA):
            h = kh * GQA + g
            qs.append(q_parts[h // 2][:, (h % 2) * HEAD_DIM:(h % 2 + 1) * HEAD_DIM])
            sinks.append(jnp.broadcast_to(sink_ref[0:1, h:h + 1], (qb, 1)))
        q = jnp.concatenate(qs, axis=0)
        sink = jnp.concatenate(sinks, axis=0)
        s_loc = jnp.where(mask, _nt_dot(q, k_loc), NEG_INF)
        s_ctx = _nt_dot(q, ck_ref[0, kh])
        o = _softmax_pv([(s_loc, v_loc), (s_ctx, cv_ref[0, kh])], sink)
        for g in range(GQA):
            h = kh * GQA + g
            o_ref[:, h * HEAD_DIM:(h + 1) * HEAD_DIM] = o[g * qb:(g + 1) * qb]


def _rope_tables(s_len):
    rows = s_len // GRID_W
    row = jnp.repeat(jnp.arange(rows), GRID_W).astype(F32)
    col = jnp.tile(jnp.arange(GRID_W), rows).astype(F32)
    nf = HEAD_DIM // 4
    inv = ROPE_BASE ** (-jnp.arange(nf, dtype=F32) / nf)
    ang = jnp.concatenate([row[:, None] * inv, col[:, None] * inv], axis=-1)
    cos, sin = jnp.cos(ang), jnp.sin(ang)
    cos2 = jnp.tile(jnp.concatenate([cos, cos], axis=-1), (1, 2))
    sin2 = jnp.tile(jnp.concatenate([-sin, sin], axis=-1), (1, 2))
    return cos2, sin2


def attn_latent(lay, qkv, cache_k, cache_v, sink):
    nq = N_HEADS * HEAD_DIM
    nkv = N_KV * HEAD_DIM
    n_blk = lay.s_lat // Q_BLOCK
    base = lay.t_ctx // Q_BLOCK
    n_ctx_keys = cache_k.shape[2]
    cos2, sin2 = _rope_tables(lay.s_lat)
    kcol = nq // nkv

    def qrow(b, j):
        return base + b * n_blk + j

    def krow(off):
        return lambda b, j: base + b * n_blk + jnp.clip(j + off, 0, n_blk - 1)

    def trow(off):
        return lambda b, j: (jnp.clip(j + off, 0, n_blk - 1), 0)

    kv_spec = lambda off, col: pl.BlockSpec((Q_BLOCK, nkv), lambda b, j: (krow(off)(b, j), col))
    tab = lambda off: pl.BlockSpec((Q_BLOCK, 2 * HEAD_DIM), trow(off))
    cache_spec = pl.BlockSpec((1, N_KV, n_ctx_keys, HEAD_DIM), lambda b, j: (b, 0, 0, 0))
    return pl.pallas_call(
        functools.partial(_attn_lat_kernel, n_blk=n_blk),
        grid=(lay.n_lat, n_blk),
        in_specs=[pl.BlockSpec((Q_BLOCK, nq), lambda b, j: (qrow(b, j), 0)),
                  kv_spec(-1, kcol), kv_spec(0, kcol), kv_spec(1, kcol),
                  kv_spec(-1, kcol + 1), kv_spec(0, kcol + 1), kv_spec(1, kcol + 1),
                  cache_spec, cache_spec,
                  tab(0), tab(0), tab(-1), tab(0), tab(1), tab(-1), tab(0), tab(1),
                  pl.BlockSpec((1, N_HEADS), lambda b, j: (0, 0))],
        out_specs=pl.BlockSpec((Q_BLOCK, nq), lambda b, j: (b * n_blk + j, 0)),
        out_shape=jax.ShapeDtypeStruct((lay.t_lat, nq), F32),
        compiler_params=_cparams(("arbitrary", "arbitrary")),
        name="attn_latent",
    )(qkv, qkv, qkv, qkv, qkv, qkv, qkv, cache_k, cache_v,
      cos2, sin2, cos2, cos2, cos2, sin2, sin2, sin2, sink.reshape(1, -1))


def _mm_res_kernel(x_ref, ac_ref, al_ref, w_ref, mod_ref, o_ref, *, n_ctx_tiles):
    a = jnp.where(pl.program_id(0) < n_ctx_tiles, ac_ref[...], al_ref[...])
    out = jnp.dot(a.astype(BF16), w_ref[...], preferred_element_type=F32)
    o_ref[...] = x_ref[...] + mod_ref[0, 2:3, :] * out


def matmul_residual(lay, x, a_ctx, a_lat, w_bf16, mods):
    mi = _mod_index(lay, ROW_TILE)
    k = a_ctx.shape[1]
    nct = lay.t_ctx // ROW_TILE
    return pl.pallas_call(
        functools.partial(_mm_res_kernel, n_ctx_tiles=nct),
        grid=(lay.t // ROW_TILE,),
        in_specs=[pl.BlockSpec((ROW_TILE, D_MODEL), lambda i: (i, 0)),
                  pl.BlockSpec((ROW_TILE, k), lambda i: (jnp.minimum(i, nct - 1), 0)),
                  pl.BlockSpec((ROW_TILE, k), lambda i: (jnp.maximum(i - nct, 0), 0)),
                  pl.BlockSpec((k, D_MODEL), lambda i: (0, 0)),
                  pl.BlockSpec((1, N_MOD, D_MODEL), lambda i: (mi(i), 0, 0))],
        out_specs=pl.BlockSpec((ROW_TILE, D_MODEL), lambda i: (i, 0)),
        out_shape=jax.ShapeDtypeStruct((lay.t, D_MODEL), F32),
        compiler_params=_cparams(("arbitrary",)),
        name="matmul_residual",
    )(x, a_ctx, a_lat, w_bf16, mods)


def _rowtile_load(ref, n, base=0):
    return jnp.concatenate([ref[pl.ds(base + c, n, stride=ROW_CHUNKS), :]
                            for c in range(ROW_CHUNKS)], axis=1)


def _rowtile_store(ref, val, n):
    for c in range(ROW_CHUNKS):
        ref[pl.ds(c, n, stride=ROW_CHUNKS), :] = val[:, c * LANES:(c + 1) * LANES]


def _row_copy(src, src_row, dst, dst_row, sem):
    return pltpu.make_async_copy(
        src.at[pl.ds(pl.multiple_of(src_row * ROW_CHUNKS, ROW_CHUNKS), ROW_CHUNKS), :],
        dst.at[pl.ds(pl.multiple_of(dst_row * ROW_CHUNKS, ROW_CHUNKS), ROW_CHUNKS), :], sem)


def _router_kernel(x_ref, g_ref, mod_ref, rwt_ref, rb_ref, tri_ref, sg_ref, su_ref, sd_ref,
                   hn_ref, sh_ref, eidx_ref, wts_ref, rank_ref, cnt_ref, cnt_s):
    tm = ROUTER_TILE

    @pl.when(pl.program_id(0) == 0)
    def _():
        cnt_s[...] = jnp.zeros_like(cnt_s)

    h = _modnorm(x_ref[...], g_ref[...], mod_ref, 3)
    _rowtile_store(hn_ref, h, tm)
    hb = h.astype(BF16)
    sgate = jnp.dot(hb, sg_ref[...], preferred_element_type=F32)
    sup = jnp.dot(hb, su_ref[...], preferred_element_type=F32)
    sh_ref[...] = jnp.dot((sgate * jax.nn.sigmoid(sgate) * sup).astype(BF16), sd_ref[...],
                          preferred_element_type=F32)
    logits = lax.dot_general(rwt_ref[...], h, (((1,), (1,)), ((), ())),
                             precision=HIGHEST, preferred_element_type=F32)
    scores = jax.nn.sigmoid(logits)
    choice = scores + rb_ref[...]
    gs_rows = []
    for g in range(N_GROUPS):
        cg = choice[g * GROUP_SIZE:(g + 1) * GROUP_SIZE, :]
        m1 = jnp.max(cg, axis=0, keepdims=True)
        eq = cg == m1
        cnt = jnp.sum(eq.astype(F32), axis=0, keepdims=True)
        m2 = jnp.max(jnp.where(eq, -jnp.inf, cg), axis=0, keepdims=True)
        gs_rows.append(m1 + jnp.where(cnt >= 2.0, m1, m2))
    gs = jnp.concatenate(gs_rows, axis=0)
    gi = lax.broadcasted_iota(I32, (N_GROUPS, tm), 0)
    grank = jnp.zeros((N_GROUPS, tm), I32)
    for g in range(N_GROUPS):
        other = gs[g:g + 1, :]
        ahead = (other > gs) | ((other == gs) & (g < gi))
        grank = grank + ahead.astype(I32)
    gsel = grank < TOPK_GROUPS
    emask = jnp.concatenate(
        [jnp.broadcast_to(gsel[g:g + 1, :], (GROUP_SIZE, tm)) for g in range(N_GROUPS)], axis=0)
    masked = jnp.where(emask, choice, -jnp.inf)
    ei = lax.broadcasted_iota(I32, (N_EXPERTS, tm), 0)
    idxs, ws = [], []
    member = jnp.zeros((N_EXPERTS, tm), F32)
    for _ in range(TOP_K):
        m = jnp.max(masked, axis=0, keepdims=True)
        idx = jnp.min(jnp.where(masked == m, ei, N_EXPERTS), axis=0, keepdims=True)
        hit = ei == idx
        ws.append(jnp.sum(jnp.where(hit, scores, 0.0), axis=0, keepdims=True))
        idxs.append(idx)
        member = jnp.where(hit, 1.0, member)
        masked = jnp.where(hit, -jnp.inf, masked)
    w = jnp.concatenate(ws, axis=0)
    wts_ref[...] = w / jnp.sum(w, axis=0, keepdims=True) * ROUTE_SCALE
    eidx_ref[...] = jnp.concatenate(idxs, axis=0)
    before = jnp.dot(member.astype(BF16), tri_ref[...], preferred_element_type=F32) + cnt_s[...]
    ranks = [jnp.sum(jnp.where(ei == idx, before, 0.0), axis=0, keepdims=True) for idx in idxs]
    rank_ref[...] = jnp.concatenate(ranks, axis=0).astype(I32)
    cnt_s[...] = cnt_s[...] + jnp.sum(member, axis=1, keepdims=True)
    cnt_ref[...] = jnp.broadcast_to(cnt_s[...], cnt_ref.shape)


def moe_router(lay, x, g, mods, router_w, router_b, sg_bf16, su_bf16, sd_bf16):
    t = lay.t
    tm = ROUTER_TILE
    mi = _mod_index(lay, tm)
    tri = (jnp.arange(tm)[:, None] < jnp.arange(tm)[None, :]).astype(BF16)
    tok = lambda i: (0, i)
    const = lambda i: (0, 0)
    return pl.pallas_call(
        _router_kernel,
        grid=(t // tm,),
        in_specs=[pl.BlockSpec((tm, D_MODEL), lambda i: (i, 0)),
                  pl.BlockSpec((1, D_MODEL), const),
                  pl.BlockSpec((1, N_MOD, D_MODEL), lambda i: (mi(i), 0, 0)),
                  pl.BlockSpec((N_EXPERTS, D_MODEL), const),
                  pl.BlockSpec((N_EXPERTS, 1), const),
                  pl.BlockSpec((tm, tm), const),
                  pl.BlockSpec((D_MODEL, D_EXPERT), const),
                  pl.BlockSpec((D_MODEL, D_EXPERT), const),
                  pl.BlockSpec((D_EXPERT, D_MODEL), const)],
        out_specs=[pl.BlockSpec((tm * ROW_CHUNKS, LANES), lambda i: (i, 0)),
                   pl.BlockSpec((tm, D_MODEL), lambda i: (i, 0)),
                   pl.BlockSpec((TOP_K, tm), tok),
                   pl.BlockSpec((TOP_K, tm), tok),
                   pl.BlockSpec((TOP_K, tm), tok),
                   pl.BlockSpec((N_EXPERTS, LANES), const)],
        out_shape=[jax.ShapeDtypeStruct((t * ROW_CHUNKS, LANES), F32),
                   jax.ShapeDtypeStruct((t, D_MODEL), F32),
                   jax.ShapeDtypeStruct((TOP_K, t), I32),
                   jax.ShapeDtypeStruct((TOP_K, t), F32),
                   jax.ShapeDtypeStruct((TOP_K, t), I32),
                   jax.ShapeDtypeStruct((N_EXPERTS, LANES), F32)],
        scratch_shapes=[pltpu.VMEM((N_EXPERTS, 1), F32)],
        compiler_params=_cparams(("arbitrary",)),
        name="moe_router",
    )(x, g.reshape(1, -1), mods, router_w.T, router_b.reshape(-1, 1), tri, sg_bf16, su_bf16, sd_bf16)


def _dest_kernel(start_ref, eidx_ref, rank_ref, dest_ref):
    e = eidx_ref[...]

    def body(i, acc):
        return jnp.where(e == i, start_ref[i], acc)

    dest_ref[...] = lax.fori_loop(0, N_EXPERTS, body, jnp.zeros_like(e), unroll=8) + rank_ref[...]


def moe_dest(pad_start, eidx, rank):
    t = eidx.shape[1]
    tn = DEST_TILE
    spec = pl.BlockSpec((TOP_K, tn), lambda i, ps: (0, i))
    return pl.pallas_call(
        _dest_kernel,
        grid_spec=pltpu.PrefetchScalarGridSpec(
            num_scalar_prefetch=1, grid=(t // tn,), in_specs=[spec, spec], out_specs=spec),
        out_shape=jax.ShapeDtypeStruct((TOP_K, t), I32),
        compiler_params=_cparams(("arbitrary",)),
        name="moe_dest",
    )(pad_start, eidx, rank)


def _issue_row_copies(idx_at, n, copy_at, unroll=4):
    def body(i, c):
        for p in range(2):
            r = 2 * i + p
            copy_at(r, idx_at(r)).start(priority=p)
        return c
    lax.fori_loop(0, n // 2, body, 0, unroll=unroll)


def _dispatch_kernel(zrow_ref, dest_hbm, hn_ref, xs_hbm, idx_s, zbuf, isem, zsem, ssem, *, n_tiles):
    i = pl.program_id(0)
    slot = i % 2
    td = DISPATCH_TILE

    n_idx = TOP_K * td

    def idx_copy(tile, s):
        return pltpu.make_async_copy(dest_hbm.at[tile], idx_s.at[pl.ds(s * n_idx, n_idx)], isem.at[s])

    def zero_copy(e):
        r0 = pl.multiple_of(zrow_ref[e] * ROW_CHUNKS, ROW_CHUNKS)
        return pltpu.make_async_copy(zbuf, xs_hbm.at[pl.ds(r0, MOE_BLOCK * ROW_CHUNKS), :], zsem)

    @pl.when(i == 0)
    def _():
        zbuf[...] = jnp.zeros_like(zbuf)

        def zstart(e, c):
            @pl.when(zrow_ref[e] >= 0)
            def _():
                zero_copy(e).start()
            return c

        def zwait(e, c):
            @pl.when(zrow_ref[e] >= 0)
            def _():
                zero_copy(e).wait()
            return c

        lax.fori_loop(0, zrow_ref.shape[0], zstart, 0)
        idx_copy(0, 0).start()
        lax.fori_loop(0, zrow_ref.shape[0], zwait, 0)

    idx_copy(i, slot).wait()

    @pl.when(i + 1 < n_tiles)
    def _():
        idx_copy(i + 1, 1 - slot).start()

    for k in range(TOP_K):
        _issue_row_copies(lambda r: idx_s[slot * n_idx + k * td + r], td,
                          lambda r, d: _row_copy(hn_ref, r, xs_hbm, d, ssem))
    for k in range(TOP_K):
        pltpu.make_async_copy(hn_ref, xs_hbm.at[pl.ds(0, td * ROW_CHUNKS), :], ssem).wait()


def _tile_major(dest, tile):
    t = dest.shape[1]
    return dest.reshape(TOP_K, t // tile, tile).transpose(1, 0, 2).reshape(t // tile, TOP_K * tile)


def moe_dispatch(lay, hn, dest, zero_row, n_rows):
    td = DISPATCH_TILE
    n_tiles = lay.t // td
    return pl.pallas_call(
        functools.partial(_dispatch_kernel, n_tiles=n_tiles),
        grid_spec=pltpu.PrefetchScalarGridSpec(
            num_scalar_prefetch=1,
            grid=(n_tiles,),
            in_specs=[pl.BlockSpec(memory_space=pl.ANY),
                      pl.BlockSpec((td * ROW_CHUNKS, LANES), lambda i, z: (i, 0))],
            out_specs=pl.BlockSpec(memory_space=pl.ANY),
            scratch_shapes=[pltpu.SMEM((2 * TOP_K * td,), I32),
                            pltpu.VMEM((MOE_BLOCK * ROW_CHUNKS, LANES), F32),
                            pltpu.SemaphoreType.DMA((2,)),
                            pltpu.SemaphoreType.DMA,
                            pltpu.SemaphoreType.DMA]),
        out_shape=jax.ShapeDtypeStruct((n_rows * ROW_CHUNKS, LANES), F32),
        compiler_params=_cparams(("arbitrary",)),
        name="moe_dispatch",
    )(zero_row, dest, hn)


def _expert_kernel(blk0_ref, nblk_ref, tail_ref, xs_hbm, wg_ref, wu_ref, wd_ref, y_hbm,
                   xbuf, ybuf, wg_s, wu_s, wd_s, isem, osem):
    e = pl.program_id(0)
    nb = nblk_ref[e]
    g0 = blk0_ref[e]
    total = blk0_ref[N_EXPERTS - 1] + nblk_ref[N_EXPERTS - 1]
    blk_rows = MOE_BLOCK * ROW_CHUNKS
    n_x = xbuf.shape[0]
    n_y = ybuf.shape[0]

    def block_rows(g):
        return pl.ds(pl.multiple_of(g * blk_rows, blk_rows), blk_rows)

    def fetch(g):
        s = g % n_x
        return pltpu.make_async_copy(xs_hbm.at[block_rows(g), :], xbuf.at[s], isem.at[s])

    def writeback(g):
        s = g % n_y
        return pltpu.make_async_copy(ybuf.at[s], y_hbm.at[block_rows(g), :], osem.at[s])

    ahead = n_x // 2

    @pl.when(e == 0)
    def _():
        for p in range(ahead):
            @pl.when(p < total)
            def _():
                fetch(p).start()

    @pl.when(nb > 0)
    def _():
        wg_s[...] = wg_ref[0, 0].astype(BF16)
        wu_s[...] = wu_ref[0, 0].astype(BF16)
        wd_s[...] = wd_ref[0, 0].astype(BF16)

    def run_blocks(g, n):
        for p in range(n):
            @pl.when(g + ahead + p < total)
            def _():
                fetch(g + ahead + p).start()
        for p in range(n):
            fetch(g + p).wait()
        x = jnp.concatenate([_rowtile_load(xbuf.at[(g + p) % n_x], MOE_BLOCK) for p in range(n)],
                            axis=0).astype(BF16)
        gate = jnp.dot(x, wg_s[...], preferred_element_type=F32)
        up = jnp.dot(x, wu_s[...], preferred_element_type=F32)
        act = gate * jax.nn.sigmoid(gate) * up
        y = jnp.dot(act.astype(BF16), wd_s[...], preferred_element_type=F32)
        for p in range(n):
            @pl.when(g + p >= n_y)
            def _():
                writeback(g + p - n_y).wait()
        for p in range(n):
            _rowtile_store(ybuf.at[(g + p) % n_y], y[p * MOE_BLOCK:(p + 1) * MOE_BLOCK], MOE_BLOCK)
            writeback(g + p).start()

    def quad(jj, c):
        run_blocks(g0 + 4 * jj, 4)
        return c

    lax.fori_loop(0, nb // 4, quad, 0)
    rem = nb % 4

    @pl.when(rem >= 2)
    def _():
        run_blocks(g0 + nb - rem, 2)

    @pl.when(rem % 2 == 1)
    def _():
        run_blocks(g0 + nb - 1, 1)

    @pl.when(e == N_EXPERTS - 1)
    def _():
        for p in range(n_y, 0, -1):
            @pl.when(total >= p)
            def _():
                writeback(total - p).wait()

        ybuf[0] = jnp.zeros(ybuf.shape[1:], F32)

        def tail_copy(i):
            r0 = pl.multiple_of(tail_ref[i] * ROW_CHUNKS, blk_rows)
            return pltpu.make_async_copy(ybuf.at[0], y_hbm.at[pl.ds(r0, blk_rows), :], osem.at[0])

        def tstart(i, c):
            @pl.when(tail_ref[i] >= 0)
            def _():
                tail_copy(i).start()
            return c

        def twait(i, c):
            @pl.when(tail_ref[i] >= 0)
            def _():
                tail_copy(i).wait()
            return c

        lax.fori_loop(0, tail_ref.shape[0], tstart, 0)
        lax.fori_loop(0, tail_ref.shape[0], twait, 0)


def moe_experts(xs, first_block, n_blocks, tail_row, layer, w_gate, w_up, w_down):
    wspec = lambda shape: pl.BlockSpec((1, 1) + shape, lambda e, a, b, c: (layer, e, 0, 0))
    blk = (MOE_BLOCK * ROW_CHUNKS, LANES)
    return pl.pallas_call(
        _expert_kernel,
        grid_spec=pltpu.PrefetchScalarGridSpec(
            num_scalar_prefetch=3,
            grid=(N_EXPERTS,),
            in_specs=[pl.BlockSpec(memory_space=pl.ANY),
                      wspec((D_MODEL, D_EXPERT)), wspec((D_MODEL, D_EXPERT)),
                      wspec((D_EXPERT, D_MODEL))],
            out_specs=pl.BlockSpec(memory_space=pl.ANY),
            scratch_shapes=[pltpu.VMEM((EXPERT_X_BUFS,) + blk, F32),
                            pltpu.VMEM((EXPERT_Y_BUFS,) + blk, F32),
                            pltpu.VMEM((D_MODEL, D_EXPERT), BF16),
                            pltpu.VMEM((D_MODEL, D_EXPERT), BF16),
                            pltpu.VMEM((D_EXPERT, D_MODEL), BF16),
                            pltpu.SemaphoreType.DMA((EXPERT_X_BUFS,)),
                            pltpu.SemaphoreType.DMA((EXPERT_Y_BUFS,))]),
        out_shape=jax.ShapeDtypeStruct(xs.shape, F32),
        compiler_params=_cparams(("arbitrary",)),
        name="moe_experts",
    )(first_block, n_blocks, tail_row, xs, w_gate, w_up, w_down)


def _combine_kernel(dest_hbm, y_hbm, x_ref, sh_ref, w_ref, mod_ref, o_ref,
                    idx_s, ybuf, isem, gsem, *, n_tiles):
    i = pl.program_id(0)
    slot = i % 2
    tm = COMBINE_TILE
    n_idx = TOP_K * tm
    nxt = jnp.minimum(i + 1, n_tiles - 1)
    nxt2 = jnp.minimum(i + 2, n_tiles - 1)

    def idx_copy(tile, s):
        return pltpu.make_async_copy(dest_hbm.at[tile], idx_s.at[pl.ds(s * n_idx, n_idx)], isem.at[s])

    def gather(s, unroll=4):
        _issue_row_copies(lambda r: idx_s[s * n_idx + r], n_idx,
                          lambda r, d: _row_copy(y_hbm, d, ybuf.at[s], r, gsem.at[s]), unroll)

    def gather_wait(s):
        pltpu.make_async_copy(y_hbm.at[pl.ds(0, n_idx * ROW_CHUNKS), :], ybuf.at[s], gsem.at[s]).wait()

    @pl.when(i == 0)
    def _():
        c = idx_copy(0, 0)
        c.start()
        c.wait()
        gather(0)
        idx_copy(nxt, 1).start()

    idx_copy(nxt, 1 - slot).wait()
    gather_wait(slot)
    gather(1 - slot, unroll=True)
    idx_copy(nxt2, slot).start()

    w = w_ref[...]
    routed = jnp.zeros((tm, D_MODEL), F32)
    for k in range(TOP_K):
        routed = routed + w[:, k:k + 1] * _rowtile_load(ybuf.at[slot], tm, base=k * tm * ROW_CHUNKS)
    o_ref[...] = x_ref[...] + mod_ref[0, 5:6, :] * (routed + sh_ref[...])

    @pl.when(i == n_tiles - 1)
    def _():
        gather_wait(1 - slot)
        idx_copy(nxt2, slot).wait()


def moe_combine(lay, x, shared, y_rows, dest, wts_t, mods):
    tm = COMBINE_TILE
    n_tiles = lay.t // tm
    mi = _mod_index(lay, tm)
    row = lambda i: (i, 0)
    return pl.pallas_call(
        functools.partial(_combine_kernel, n_tiles=n_tiles),
        grid=(n_tiles,),
        in_specs=[pl.BlockSpec(memory_space=pl.ANY),
                  pl.BlockSpec(memory_space=pl.ANY),
                  pl.BlockSpec((tm, D_MODEL), row),
                  pl.BlockSpec((tm, D_MODEL), row),
                  pl.BlockSpec((tm, TOP_K), row),
                  pl.BlockSpec((1, N_MOD, D_MODEL), lambda i: (mi(i), 0, 0))],
        out_specs=pl.BlockSpec((tm, D_MODEL), row),
        out_shape=jax.ShapeDtypeStruct((lay.t, D_MODEL), F32),
        scratch_shapes=[pltpu.SMEM((2 * TOP_K * tm,), I32),
                        pltpu.VMEM((2, TOP_K * tm * ROW_CHUNKS, LANES), F32),
                        pltpu.SemaphoreType.DMA((2,)),
                        pltpu.SemaphoreType.DMA((2,))],
        compiler_params=_cparams(("arbitrary",)),
        name="moe_combine",
    )(dest, y_rows, x, shared, wts_t, mods)


def moe_layer(lay, x, g, mods, router_w, router_b, layer, w_gate, w_up, w_down, s_gate, s_up, s_down):
    t = lay.t
    hn, shared, eidx, wts, rank, cnt = moe_router(
        lay, x, g, mods, router_w, router_b, s_gate.astype(BF16), s_up.astype(BF16), s_down.astype(BF16))
    counts = cnt[:, 0].astype(I32)
    n_blocks = (counts + MOE_BLOCK - 1) // MOE_BLOCK
    padded = n_blocks * MOE_BLOCK
    pad_end = jnp.cumsum(padded)
    pad_start = pad_end - padded
    n_rows = -(-(t * TOP_K + N_EXPERTS * (MOE_BLOCK - 1)) // MOE_BLOCK) * MOE_BLOCK
    dest = moe_dest(pad_start, eidx, rank)
    last_row = jnp.where(n_blocks > 0, pad_end - MOE_BLOCK, -1)
    tail_blk = pad_end[-1] // MOE_BLOCK + jnp.arange(n_rows // MOE_BLOCK - t * TOP_K // MOE_BLOCK)
    tail_row = jnp.where(tail_blk < n_rows // MOE_BLOCK, tail_blk * MOE_BLOCK, -1).astype(I32)
    xs = moe_dispatch(lay, hn, _tile_major(dest, DISPATCH_TILE),
                      jnp.concatenate([last_row, tail_row]), n_rows)
    y_rows = moe_experts(xs, pad_start // MOE_BLOCK, n_blocks, tail_row, layer, w_gate, w_up, w_down)
    return moe_combine(lay, x, shared, y_rows, _tile_major(dest, COMBINE_TILE), wts.T, mods)


def _final_norm_kernel(x_ref, g_ref, o_ref):
    x = x_ref[...]
    ms = jnp.mean(x * x, axis=-1, keepdims=True)
    o_ref[...] = x * lax.rsqrt(ms + EPS) * g_ref[...]


def final_norm(x, g, row0, n_rows):
    base = row0 // ROW_TILE
    return pl.pallas_call(
        _final_norm_kernel,
        grid=(n_rows // ROW_TILE,),
        in_specs=[pl.BlockSpec((ROW_TILE, D_MODEL), lambda i: (base + i, 0)),
                  pl.BlockSpec((1, D_MODEL), lambda i: (0, 0))],
        out_specs=pl.BlockSpec((ROW_TILE, D_MODEL), lambda i: (i, 0)),
        out_shape=jax.ShapeDtypeStruct((n_rows, D_MODEL), F32),
        compiler_params=_cparams(("arbitrary",)),
        name="final_norm",
    )(x, g.reshape(1, -1))


def _block_diag(w):
    nb, bw, _ = w.shape
    eye = jnp.eye(nb, dtype=w.dtype)
    return (eye[:, None, :, None] * w[:, :, None, :]).reshape(nb * bw, nb * bw)


def even_layer(lay, x, mods, g_mix, p, state_lru, state_ssm_re, state_ssm_im):
    t = lay.t
    proj, u_g = modnorm_matmul(lay, x, g_mix, mods, 0, p['w_in'].astype(BF16),
                               ug_col=2 * D_LRU)
    zeros_c = jnp.zeros((lay.n_ctx, D_LRU), F32)
    hf_y, st = None, []
    for d in range(2):
        wg = jnp.concatenate([_block_diag(p['lru_wa'][d]), _block_diag(p['lru_wx'][d])], axis=1)
        bg = jnp.concatenate([p['lru_ba'][d], p['lru_bx'][d]])
        h0 = jnp.concatenate([zeros_c, state_lru[:, d].astype(F32)], axis=0)
        hf_y, s = lru_pass(lay, proj, p['conv_w'], p['conv_b'], wg.astype(BF16), bg,
                           p['lru_lam'][d], h0, reverse=(d == 1), hf=hf_y)
        st.append(s[:lay.n_ctx])
    y_a = hf_y
    new_lru = jnp.stack(st, axis=1)

    mats = _s5_matrices(p['a_re'], p['a_im'], p['log_dt'], p['b_re'], p['b_im'], p['c_re'], p['c_im'])
    h0 = jnp.concatenate([state_ssm_re, state_ssm_im], axis=-1).astype(F32)
    h0 = h0.transpose(2, 1, 0, 3)
    y_g, h_ctx = s5_mixer(lay, u_g, mats, h0)
    seg = lay.s_ctx // S5_CHUNK
    h_ctx = h_ctx.reshape(SSM_GROUPS, 2, lay.n_ctx, seg, 2 * SSM_STATE)
    ends = jnp.stack([h_ctx[:, 0, :, seg - 1], h_ctx[:, 1, :, 0]], axis=1)
    ends = ends.transpose(2, 1, 0, 3)
    x = even_out(lay, x, y_a, y_g, proj, p['d'], p['glu_w'].astype(BF16), p['glu_b'],
                 p['w_out'].astype(BF16), mods)
    return x, new_lru, ends[..., :SSM_STATE], ends[..., SSM_STATE:]


def odd_layer(lay, x, mods, g_mix, w_qkv, sink, w_out, cache_k, cache_v):
    qkv = modnorm_matmul(lay, x, g_mix, mods, 0, w_qkv.astype(BF16))
    o_ctx = attn_context(lay, qkv, sink)
    o_lat = attn_latent(lay, qkv, cache_k, cache_v, sink)
    nq = N_HEADS * HEAD_DIM
    kv = qkv[:lay.t_ctx, nq:].reshape(lay.n_ctx, lay.s_ctx, 2, N_KV, HEAD_DIM)
    k_new = kv[:, :, 0].swapaxes(1, 2)
    v_new = kv[:, :, 1].swapaxes(1, 2)
    x = matmul_residual(lay, x, o_ctx, o_lat, w_out.astype(BF16), mods)
    return x, k_new, v_new


def _forward(lay, x_prompt, x_sample, state_lru, state_ssm_re, state_ssm_im, cache_k, cache_v,
             c, c_ctx, g_mix, g_ffn, w_mod, b_mod,
             ev_w_in, lru_conv_w, lru_conv_b, lru_wa, lru_ba, lru_wx, lru_bx, lru_lam,
             ssm_a_re, ssm_a_im, ssm_log_dt, ssm_b_re, ssm_b_im, ssm_c_re, ssm_c_im, ssm_d,
             ssm_glu_w, ssm_glu_b, ev_w_out, at_w_qkv, at_sink, at_w_out,
             router_w, router_b, exp_w_gate, exp_w_up, exp_w_down, sh_w_gate, sh_w_up, sh_w_down,
             g_final):
    depth = g_mix.shape[0]
    x = jnp.concatenate([x_prompt.reshape(lay.t_ctx, D_MODEL), x_sample.reshape(lay.t_lat, D_MODEL)],
                        axis=0)
    n_c = 1 + lay.n_lat
    c_rows = jnp.concatenate([c_ctx[None, :], c, jnp.zeros((16 - n_c, D_MODEL), F32)], axis=0)
    new_lru, new_re, new_im, new_k, new_v = [], [], [], [], []
    for l in range(depth):
        i = l // 2
        mods = adaln_table(c_rows, l, w_mod, b_mod[l])
        if l % 2 == 0:
            p = dict(w_in=ev_w_in[i], conv_w=lru_conv_w[i], conv_b=lru_conv_b[i],
                     lru_wa=lru_wa[i], lru_ba=lru_ba[i], lru_wx=lru_wx[i], lru_bx=lru_bx[i],
                     lru_lam=lru_lam[i], a_re=ssm_a_re[i], a_im=ssm_a_im[i], log_dt=ssm_log_dt[i],
                     b_re=ssm_b_re[i], b_im=ssm_b_im[i], c_re=ssm_c_re[i], c_im=ssm_c_im[i],
                     d=ssm_d[i], glu_w=ssm_glu_w[i], glu_b=ssm_glu_b[i], w_out=ev_w_out[i])
            x, lru_i, re_i, im_i = even_layer(lay, x, mods, g_mix[l], p, state_lru[:, i],
                                              state_ssm_re[:, i], state_ssm_im[:, i])
            new_lru.append(lru_i)
            new_re.append(re_i)
            new_im.append(im_i)
        else:
            x, k_i, v_i = odd_layer(lay, x, mods, g_mix[l], at_w_qkv[i], at_sink[i], at_w_out[i],
                                    cache_k[:, i], cache_v[:, i])
            new_k.append(k_i)
            new_v.append(v_i)
        x = moe_layer(lay, x, g_ffn[l], mods, router_w[l], router_b[l], l, exp_w_gate, exp_w_up,
                      exp_w_down, sh_w_gate[l], sh_w_up[l], sh_w_down[l])
    y_prompt = final_norm(x, g_final, 0, lay.t_ctx).reshape(x_prompt.shape)
    y_sample = final_norm(x, g_final, lay.t_ctx, lay.t_lat).reshape(x_sample.shape)
    return (y_prompt, y_sample, jnp.stack(new_lru, axis=1), jnp.stack(new_re, axis=1),
            jnp.stack(new_im, axis=1), jnp.stack(new_k, axis=1), jnp.stack(new_v, axis=1))


def kernel(x_prompt, x_sample, state_lru, state_ssm_re, state_ssm_im, cache_k, cache_v, c, c_ctx, g_mix, g_ffn, w_mod, b_mod, ev_w_in, lru_conv_w, lru_conv_b, lru_wa, lru_ba, lru_wx, lru_bx, lru_lam, ssm_a_re, ssm_a_im, ssm_log_dt, ssm_b_re, ssm_b_im, ssm_c_re, ssm_c_im, ssm_d, ssm_glu_w, ssm_glu_b, ev_w_out, at_w_qkv, at_sink, at_w_out, router_w, router_b, exp_w_gate, exp_w_up, exp_w_down, sh_w_gate, sh_w_up, sh_w_down, g_final):
    lay = Layout(n_ctx=x_prompt.shape[0], s_ctx=x_prompt.shape[1],
                 n_lat=x_sample.shape[0], s_lat=x_sample.shape[1])
    return _forward(lay, x_prompt, x_sample, state_lru, state_ssm_re, state_ssm_im, cache_k, cache_v,
                    c, c_ctx, g_mix, g_ffn, w_mod, b_mod,
                    ev_w_in, lru_conv_w, lru_conv_b, lru_wa, lru_ba, lru_wx, lru_bx, lru_lam,
                    ssm_a_re, ssm_a_im, ssm_log_dt, ssm_b_re, ssm_b_im, ssm_c_re, ssm_c_im, ssm_d,
                    ssm_glu_w, ssm_glu_b, ev_w_out, at_w_qkv, at_sink, at_w_out,
                    router_w, router_b, exp_w_gate, exp_w_up, exp_w_down, sh_w_gate, sh_w_up,
                    sh_w_down, g_final)
```

```python
import functools
from typing import NamedTuple

import jax
import jax.numpy as jnp
from jax import lax
from jax.experimental import pallas as pl
from jax.experimental.pallas import tpu as pltpu

F32 = jnp.float32
BF16 = jnp.bfloat16
I32 = jnp.int32
HIGHEST = lax.Precision.HIGHEST

D_MODEL = 1024
EPS = 1e-6
N_MOD = 6
GRID_W = 64
D_LRU = 512
LRU_BLOCKS = 8
LRU_C = 8.0
CONV_W = 4
CONV_LEFT = 2
D_SSM = 512
SSM_GROUP = 16
SSM_GROUPS = 32
SSM_STATE = 64
S5_CHUNK = 16
S5_LANES = S5_CHUNK * SSM_GROUP
S5_SCAN_STEPS = 8
HEAD_DIM = 64
N_HEADS = 16
N_KV = 4
GQA = 4
WINDOW = 128
Q_BLOCK = 128
ROPE_BASE = 10000.0
ATTN_SCALE = HEAD_DIM ** -0.5
NEG_INF = -1e30
N_EXPERTS = 256
TOP_K = 8
N_GROUPS = 8
TOPK_GROUPS = 4
GROUP_SIZE = N_EXPERTS // N_GROUPS
D_EXPERT = 256
ROUTE_SCALE = 2.5
MOE_BLOCK = 128

SUBLANES = 8
LANES = 128
ROW_CHUNKS = D_MODEL // (2 * LANES)
PACKED = jnp.int32
SEQ_TILE = 256
ROW_TILE = 512
ROUTER_TILE = 512
DEST_TILE = 1024
DISPATCH_TILE = 512
COMBINE_TILE = 128
EXPERT_X_BUFS = 8
EXPERT_Y_BUFS = 8
VMEM_LIMIT = 56 * 1024 * 1024


class Layout(NamedTuple):
    n_ctx: int
    s_ctx: int
    n_lat: int
    s_lat: int

    @property
    def t_ctx(self):
        return self.n_ctx * self.s_ctx

    @property
    def t_lat(self):
        return self.n_lat * self.s_lat

    @property
    def t(self):
        return self.t_ctx + self.t_lat

    @property
    def n_seq(self):
        return self.n_ctx + self.n_lat


def _cparams(sem):
    return pltpu.CompilerParams(dimension_semantics=sem, vmem_limit_bytes=VMEM_LIMIT)


def _mod_index(lay, tile_rows):
    n_ctx_tiles = lay.t_ctx // tile_rows
    per_lat = lay.s_lat // tile_rows

    def f(i):
        return jnp.where(i < n_ctx_tiles, 0, 1 + (i - n_ctx_tiles) // per_lat)
    return f


def _adaln_kernel(c_ref, w_ref, b_ref, o_ref):
    c = c_ref[...]
    s = c * jax.nn.sigmoid(c)
    o_ref[...] = jnp.dot(s, w_ref[0], precision=HIGHEST, preferred_element_type=F32) + b_ref[...]


def adaln_table(c_rows, layer, w_mod, b_mod):
    n = c_rows.shape[0]
    tn = 1536
    out = pl.pallas_call(
        _adaln_kernel,
        grid=(N_MOD * D_MODEL // tn,),
        in_specs=[pl.BlockSpec((n, D_MODEL), lambda j: (0, 0)),
                  pl.BlockSpec((1, D_MODEL, tn), lambda j: (layer, 0, j)),
                  pl.BlockSpec((1, tn), lambda j: (0, j))],
        out_specs=pl.BlockSpec((n, tn), lambda j: (0, j)),
        out_shape=jax.ShapeDtypeStruct((n, N_MOD * D_MODEL), F32),
        compiler_params=_cparams(("arbitrary",)),
        name="adaln",
    )(c_rows, w_mod, b_mod.reshape(1, -1))
    return out.reshape(n, N_MOD, D_MODEL)


def _modnorm(x, g, mod_ref, slot):
    ms = jnp.mean(x * x, axis=-1, keepdims=True)
    y = x * lax.rsqrt(ms + EPS) * g
    shift = mod_ref[0, slot:slot + 1, :]
    scale = mod_ref[0, slot + 1:slot + 2, :]
    return y * (1.0 + scale) + shift


GROUPS_PER_VREG = LANES // SSM_GROUP
SSM_COL_BLOCKS = D_SSM // LANES


def _group_major_store(val, tmp_ref, dst_ref):
    rows = dst_ref.shape[1]
    for j in range(SSM_COL_BLOCKS):
        tmp_ref[j] = val[:, j * LANES:(j + 1) * LANES]
    for j in range(SSM_COL_BLOCKS):
        steps = [tmp_ref[j, pl.ds(i, rows, stride=S5_CHUNK), :] for i in range(S5_CHUNK)]
        for q in range(GROUPS_PER_VREG):
            dst_ref[j * GROUPS_PER_VREG + q] = jnp.concatenate(
                [w[:, q * SSM_GROUP:(q + 1) * SSM_GROUP] for w in steps], axis=1)


def _group_major_load(src_ref, tmp_ref):
    rows = src_ref.shape[1]
    for j in range(SSM_COL_BLOCKS):
        blocks = [src_ref[j * GROUPS_PER_VREG + q] for q in range(GROUPS_PER_VREG)]
        for i in range(S5_CHUNK):
            tmp_ref[j, pl.ds(i, rows, stride=S5_CHUNK), :] = jnp.concatenate(
                [b[:, i * SSM_GROUP:(i + 1) * SSM_GROUP] for b in blocks], axis=1)
    return jnp.concatenate([tmp_ref[j] for j in range(SSM_COL_BLOCKS)], axis=1)


def _modnorm_mm_kernel(x_ref, g_ref, mod_ref, w_ref, o_ref, *ug_refs, slot, ug_col):
    h = _modnorm(x_ref[...], g_ref[...], mod_ref, slot)
    out = jnp.dot(h.astype(BF16), w_ref[...], preferred_element_type=F32)
    o_ref[...] = out
    if ug_col is not None:
        ug_ref, tmp_ref = ug_refs
        _group_major_store(out[:, ug_col:ug_col + D_SSM], tmp_ref, ug_ref)


def modnorm_matmul(lay, x, g, mods, slot, w_bf16, ug_col=None):
    t = lay.t
    n = w_bf16.shape[1]
    mi = _mod_index(lay, ROW_TILE)
    out_specs = [pl.BlockSpec((ROW_TILE, n), lambda i: (i, 0))]
    out_shape = [jax.ShapeDtypeStruct((t, n), F32)]
    if ug_col is not None:
        out_specs.append(pl.BlockSpec((SSM_GROUPS, ROW_TILE // S5_CHUNK, S5_LANES), lambda i: (0, i, 0)))
        out_shape.append(jax.ShapeDtypeStruct((SSM_GROUPS, t // S5_CHUNK, S5_LANES), F32))
    outs = pl.pallas_call(
        functools.partial(_modnorm_mm_kernel, slot=slot, ug_col=ug_col),
        grid=(t // ROW_TILE,),
        in_specs=[pl.BlockSpec((ROW_TILE, D_MODEL), lambda i: (i, 0)),
                  pl.BlockSpec((1, D_MODEL), lambda i: (0, 0)),
                  pl.BlockSpec((1, N_MOD, D_MODEL), lambda i: (mi(i), 0, 0)),
                  pl.BlockSpec((D_MODEL, n), lambda i: (0, 0))],
        out_specs=out_specs,
        out_shape=out_shape,
        scratch_shapes=([pltpu.VMEM((SSM_COL_BLOCKS, ROW_TILE, LANES), F32)]
                        if ug_col is not None else []),
        compiler_params=_cparams(("arbitrary",)),
        name="modnorm_matmul",
    )(x, g.reshape(1, -1), mods, w_bf16)
    return outs if ug_col is not None else outs[0]


def _seq_tile_maps(lay, reverse):
    assert lay.s_ctx == SEQ_TILE and lay.s_lat % SEQ_TILE == 0
    n_tiles = lay.t // SEQ_TILE
    per_lat = lay.s_lat // SEQ_TILE

    def tile(i):
        return (n_tiles - 1 - i) if reverse else i

    def seq(i):
        ti = tile(i)
        return jnp.where(ti < lay.n_ctx, ti, lay.n_ctx + (ti - lay.n_ctx) // per_lat)

    return n_tiles, tile, seq


def _softplus(x):
    return jnp.maximum(x, 0.0) + jnp.log(1.0 + jnp.exp(-jnp.abs(x)))


def _lru_kernel(rec_ref, prev_ref, next_ref, cw_ref, cb_ref, wg_ref, bg_ref, lam_ref, h0_ref,
                *rest, reverse, n_ctx, per_lat, n_tiles):
    if reverse:
        gate_ref, hf_ref, y_ref, st_ref, a_s, b_s, h_s, carry = rest
    else:
        y_ref, st_ref, a_s, b_s, h_s, carry = rest
    i = pl.program_id(0)
    ti = (n_tiles - 1 - i) if reverse else i
    is_first = jnp.logical_or(ti < n_ctx, (ti - n_ctx) % per_lat == 0)
    is_last = jnp.logical_or(ti < n_ctx, (ti - n_ctx) % per_lat == per_lat - 1)
    ts = SEQ_TILE

    rec = rec_ref[...]
    prev = jnp.where(is_first, 0.0, prev_ref[...])
    nxt = jnp.where(is_last, 0.0, next_ref[...])
    ext = jnp.concatenate([prev, rec, nxt], axis=0)
    n_ext = ts + 2 * SUBLANES
    cw = cw_ref[...]
    xc = cb_ref[...] + cw[2:3, :] * rec
    xc = xc + cw[0:1, :] * pltpu.roll(ext, 2, 0)[SUBLANES:SUBLANES + ts]
    xc = xc + cw[1:2, :] * pltpu.roll(ext, 1, 0)[SUBLANES:SUBLANES + ts]
    xc = xc + cw[3:4, :] * pltpu.roll(ext, n_ext - 1, 0)[SUBLANES:SUBLANES + ts]

    gates = jax.nn.sigmoid(jnp.dot(xc.astype(BF16), wg_ref[...], preferred_element_type=F32)
                           + bg_ref[...])
    r = gates[:, :D_LRU]
    ig = gates[:, D_LRU:]
    log_a = (-LRU_C) * r * _softplus(-lam_ref[...])
    a = jnp.exp(log_a)
    b = jnp.sqrt(1.0 - jnp.exp(2.0 * log_a)) * (ig * xc)

    row8 = lax.broadcasted_iota(I32, (ts, D_LRU), 0) % SUBLANES
    for sh in (1, 2, 4):
        if reverse:
            keep = row8 < SUBLANES - sh
            a_sh = pltpu.roll(a, ts - sh, 0)
            b_sh = pltpu.roll(b, ts - sh, 0)
        else:
            keep = row8 >= sh
            a_sh = pltpu.roll(a, sh, 0)
            b_sh = pltpu.roll(b, sh, 0)
        b = b + a * jnp.where(keep, b_sh, 0.0)
        a = a * jnp.where(keep, a_sh, 1.0)
    a_s[...] = a
    b_s[...] = b

    @pl.when(is_last if reverse else is_first)
    def _():
        carry[...] = h0_ref[0]

    n_grp = ts // SUBLANES

    def body(k, c):
        gi = (n_grp - 1 - k) if reverse else k
        sl = pl.ds(pl.multiple_of(gi * SUBLANES, SUBLANES), SUBLANES)
        h = b_s[sl, :] + a_s[sl, :] * c
        h_s[sl, :] = h
        return h[0:1, :] if reverse else h[SUBLANES - 1:SUBLANES, :]

    c_fin = lax.fori_loop(0, n_grp, body, carry[...], unroll=4)
    carry[...] = c_fin
    st_ref[0] = c_fin
    if reverse:
        y_ref[...] = (hf_ref[...] + h_s[...]) * jax.nn.gelu(gate_ref[...])
    else:
        y_ref[...] = h_s[...]


def lru_pass(lay, proj, conv_w, conv_b, wg_bf16, bg, lam, h0, reverse, hf=None):
    n_tiles, tile, seq = _seq_tile_maps(lay, reverse)
    per_lat = lay.s_lat // SEQ_TILE
    blk8 = SEQ_TILE // SUBLANES
    last8 = lay.t // SUBLANES - 1
    c = D_LRU
    in_specs = [
        pl.BlockSpec((SEQ_TILE, c), lambda i: (tile(i), 1)),
        pl.BlockSpec((SUBLANES, c), lambda i: (jnp.maximum(tile(i) * blk8 - 1, 0), 1)),
        pl.BlockSpec((SUBLANES, c), lambda i: (jnp.minimum(tile(i) * blk8 + blk8, last8), 1)),
        pl.BlockSpec((CONV_W, c), lambda i: (0, 0)),
        pl.BlockSpec((1, c), lambda i: (0, 0)),
        pl.BlockSpec((c, 2 * c), lambda i: (0, 0)),
        pl.BlockSpec((1, 2 * c), lambda i: (0, 0)),
        pl.BlockSpec((1, c), lambda i: (0, 0)),
        pl.BlockSpec((1, 1, c), lambda i: (seq(i), 0, 0)),
    ]
    args = [proj, proj, proj, conv_w, conv_b.reshape(1, -1), wg_bf16, bg.reshape(1, -1),
            lam.reshape(1, -1), h0.reshape(lay.n_seq, 1, c)]
    if reverse:
        in_specs += [pl.BlockSpec((SEQ_TILE, c), lambda i: (tile(i), 0)),
                     pl.BlockSpec((SEQ_TILE, c), lambda i: (tile(i), 0))]
        args += [proj, hf]
    y, st = pl.pallas_call(
        functools.partial(_lru_kernel, reverse=reverse, n_ctx=lay.n_ctx, per_lat=per_lat,
                          n_tiles=n_tiles),
        grid=(n_tiles,),
        in_specs=in_specs,
        out_specs=[pl.BlockSpec((SEQ_TILE, c), lambda i: (tile(i), 0)),
                   pl.BlockSpec((1, 1, c), lambda i: (seq(i), 0, 0))],
        out_shape=[jax.ShapeDtypeStruct((lay.t, c), F32),
                   jax.ShapeDtypeStruct((lay.n_seq, 1, c), F32)],
        scratch_shapes=[pltpu.VMEM((SEQ_TILE, c), F32), pltpu.VMEM((SEQ_TILE, c), F32),
                        pltpu.VMEM((SEQ_TILE, c), F32), pltpu.VMEM((1, c), F32)],
        compiler_params=_cparams(("arbitrary",)),
        name="lru_bwd" if reverse else "lru_fwd",
    )(*args)
    return y, st.reshape(lay.n_seq, c)


def _cmul(a, b):
    return a[0] * b[0] - a[1] * b[1], a[0] * b[1] + a[1] * b[0]


def _s5_matrices(a_re, a_im, log_dt, b_re, b_im, c_re, c_im):
    a_re, a_im = a_re.astype(F32), a_im.astype(F32)
    dt = jnp.exp(log_dt.astype(F32))[..., None]
    z = (a_re * dt, a_im * dt)

    def zpow(k):
        k = k.reshape((-1,) + (1,) * z[0].ndim)
        mag = jnp.exp(k * z[0][None])
        return mag * jnp.cos(k * z[1][None]), mag * jnp.sin(k * z[1][None])

    a_bar = zpow(jnp.ones((1,), F32))
    a_bar = (a_bar[0][0], a_bar[1][0])
    den = a_re * a_re + a_im * a_im
    xr, xi = a_bar[0] - 1.0, a_bar[1]
    q = ((xr * a_re + xi * a_im) / den, (xi * a_re - xr * a_im) / den)
    b_bar = _cmul((q[0][..., None], q[1][..., None]), (b_re.astype(F32), b_im.astype(F32)))
    cc = (c_re.astype(F32), c_im.astype(F32))
    el = S5_CHUNK
    pw = zpow(jnp.arange(el + 1, dtype=F32))
    idx = jnp.arange(el)
    m_in, m_toep, m_out = [], [], []
    for d in range(2):
        p_d = (pw[0][:, d], pw[1][:, d])
        b_d = (b_bar[0][d], b_bar[1][d])
        c_d = (cc[0][d], cc[1][d])
        k_in = (el - 1 - idx) if d == 0 else idx
        w_in = _cmul((p_d[0][k_in][..., None], p_d[1][k_in][..., None]),
                     (b_d[0][None], b_d[1][None]))
        w_in = [jnp.transpose(w, (1, 0, 3, 2)).reshape(SSM_GROUPS, S5_LANES, SSM_STATE) for w in w_in]
        m_in.append(jnp.concatenate(w_in, axis=-1))
        cp = _cmul((c_d[0][None], c_d[1][None]),
                   (p_d[0][:, :, None, :], p_d[1][:, :, None, :]))
        kern = (jnp.einsum('kghp,gpc->kgch', cp[0][:el], b_d[0])
                - jnp.einsum('kghp,gpc->kgch', cp[1][:el], b_d[1]))
        diff = (idx[None, :] - idx[:, None]) if d == 0 else (idx[:, None] - idx[None, :])
        blocks = jnp.where((diff >= 0)[:, :, None, None, None],
                           kern[jnp.clip(diff, 0, el - 1)], 0.0)
        m_toep.append(jnp.transpose(blocks, (2, 0, 3, 1, 4)).reshape(SSM_GROUPS, S5_LANES, S5_LANES))
        k_out = (idx + 1) if d == 0 else (el - idx)
        w_out = [jnp.transpose(w[k_out], (1, 3, 0, 2)).reshape(SSM_GROUPS, SSM_STATE, S5_LANES)
                 for w in cp]
        m_out.append(jnp.concatenate([w_out[0], -w_out[1]], axis=1))
    mul = zpow(el * 2.0 ** jnp.arange(S5_SCAN_STEPS, dtype=F32))
    mul = [jnp.transpose(m, (2, 1, 0, 3)) for m in mul]
    coef_a = jnp.concatenate([mul[0], mul[0]], axis=-1)
    coef_b = jnp.concatenate([-mul[1], mul[1]], axis=-1)
    stack = lambda xs: jnp.stack(xs, axis=1)
    return (stack(m_in).astype(BF16), stack(m_toep).astype(BF16), stack(m_out).astype(BF16),
            coef_a, coef_b)


def _s5_scan(v, ca, cb, seg, reverse):
    n = v.shape[0]
    assert seg <= 2 ** S5_SCAN_STEPS
    row = lax.broadcasted_iota(I32, (n, 2 * SSM_STATE), 0) % seg
    k, sh = 0, 1
    while sh < seg:
        if reverse:
            s = jnp.where(row < seg - sh, pltpu.roll(v, n - sh, 0), 0.0)
        else:
            s = jnp.where(row >= sh, pltpu.roll(v, sh, 0), 0.0)
        v = v + ca[k:k + 1, :] * s + cb[k:k + 1, :] * pltpu.roll(s, SSM_STATE, 1)
        k += 1
        sh *= 2
    return v


def _s5_shift(h, seg, reverse):
    n = h.shape[0]
    row = lax.broadcasted_iota(I32, (n, 2 * SSM_STATE), 0) % seg
    if reverse:
        return jnp.where(row < seg - 1, pltpu.roll(h, n - 1, 0), 0.0)
    return jnp.where(row >= 1, pltpu.roll(h, 1, 0), 0.0)


def _s5_kernel(u_ref, min_ref, mtoep_ref, mout_ref, ca_ref, cb_ref, h0_ref, y_ref, hc_ref,
               v_s, hp_s, *, rc, seg_c, n_lat, seg_l):
    u = u_ref[0].astype(BF16)
    u_c, u_l = u[:rc], u[rc:]
    y_c = jnp.zeros((rc, S5_LANES), F32)
    y_l = jnp.zeros((n_lat * seg_l, S5_LANES), F32)
    for d in range(2):
        reverse = d == 1
        ca = ca_ref[0, d]
        cb = cb_ref[0, d]
        m_in = min_ref[0, d]
        m_toep = mtoep_ref[0, d]
        m_out = mout_ref[0, d]
        h_c = _s5_scan(jnp.dot(u_c, m_in, preferred_element_type=F32), ca, cb, seg_c, reverse)
        hc_ref[0, d] = h_c
        hp_c = _s5_shift(h_c, seg_c, reverse)
        y_c = y_c + jnp.dot(u_c, m_toep, preferred_element_type=F32)
        y_c = y_c + jnp.dot(hp_c.astype(BF16), m_out, preferred_element_type=F32)
        v_s[...] = jnp.dot(u_l, m_in, preferred_element_type=F32)
        for s in range(n_lat):
            h0 = h0_ref[0, d, s:s + 1, :]
            r0 = s * seg_l + (seg_l - 1 if reverse else 0)
            v_s[r0:r0 + 1, :] = (v_s[r0:r0 + 1, :] + ca[0:1, :] * h0
                                 + cb[0:1, :] * pltpu.roll(h0, SSM_STATE, 1))
        h_l = _s5_scan(v_s[...], ca, cb, seg_l, reverse)
        hp_s[...] = _s5_shift(h_l, seg_l, reverse)
        for s in range(n_lat):
            r0 = s * seg_l + (seg_l - 1 if reverse else 0)
            hp_s[r0:r0 + 1, :] = h0_ref[0, d, s:s + 1, :]
        y_l = y_l + jnp.dot(u_l, m_toep, preferred_element_type=F32)
        y_l = y_l + jnp.dot(hp_s[...].astype(BF16), m_out, preferred_element_type=F32)
    y_ref[0, :rc, :] = y_c
    y_ref[0, rc:, :] = y_l


def s5_mixer(lay, u_g, mats, h0):
    m_in, m_toep, m_out, coef_a, coef_b = mats
    rows = lay.t // S5_CHUNK
    rc = lay.t_ctx // S5_CHUNK
    rl = rows - rc
    st2 = 2 * SSM_STATE
    g4 = lambda g: (g, 0, 0, 0)
    return pl.pallas_call(
        functools.partial(_s5_kernel, rc=rc, seg_c=lay.s_ctx // S5_CHUNK, n_lat=lay.n_lat,
                          seg_l=lay.s_lat // S5_CHUNK),
        grid=(SSM_GROUPS,),
        in_specs=[pl.BlockSpec((1, rows, S5_LANES), lambda g: (g, 0, 0)),
                  pl.BlockSpec((1, 2, S5_LANES, st2), g4),
                  pl.BlockSpec((1, 2, S5_LANES, S5_LANES), g4),
                  pl.BlockSpec((1, 2, st2, S5_LANES), g4),
                  pl.BlockSpec((1, 2, S5_SCAN_STEPS, st2), g4),
                  pl.BlockSpec((1, 2, S5_SCAN_STEPS, st2), g4),
                  pl.BlockSpec((1, 2, lay.n_lat, st2), g4)],
        out_specs=[pl.BlockSpec((1, rows, S5_LANES), lambda g: (g, 0, 0)),
                   pl.BlockSpec((1, 2, rc, st2), g4)],
        out_shape=[jax.ShapeDtypeStruct((SSM_GROUPS, rows, S5_LANES), F32),
                   jax.ShapeDtypeStruct((SSM_GROUPS, 2, rc, st2), F32)],
        scratch_shapes=[pltpu.VMEM((rl, st2), F32), pltpu.VMEM((rl, st2), F32)],
        compiler_params=_cparams(("arbitrary",)),
        name="s5_mixer",
    )(u_g, m_in, m_toep, m_out, coef_a, coef_b, h0)


def _even_out_kernel(x_ref, ya_ref, yg_ref, u_ref, d_ref, gw_ref, gb_ref, w_ref, mod_ref, o_ref, yt_s):
    ys = _group_major_load(yg_ref, yt_s) + d_ref[...] * u_ref[...]
    g = jax.nn.gelu(ys)
    yb = g * jax.nn.sigmoid(jnp.dot(g.astype(BF16), gw_ref[...], preferred_element_type=F32)
                            + gb_ref[...])
    out = jnp.dot(ya_ref[...].astype(BF16), w_ref[:D_LRU, :], preferred_element_type=F32)
    out = out + jnp.dot(yb.astype(BF16), w_ref[D_LRU:, :], preferred_element_type=F32)
    o_ref[...] = x_ref[...] + mod_ref[0, 2:3, :] * out


def even_out(lay, x, y_a, y_g, proj, ssm_d, glu_w_bf16, glu_b, w_out_bf16, mods):
    mi = _mod_index(lay, ROW_TILE)
    c = D_SSM
    row = lambda i: (i, 0)
    const = lambda i: (0, 0)
    return pl.pallas_call(
        _even_out_kernel,
        grid=(lay.t // ROW_TILE,),
        in_specs=[pl.BlockSpec((ROW_TILE, D_MODEL), row),
                  pl.BlockSpec((ROW_TILE, c), row),
                  pl.BlockSpec((SSM_GROUPS, ROW_TILE // S5_CHUNK, S5_LANES), lambda i: (0, i, 0)),
                  pl.BlockSpec((ROW_TILE, c), lambda i: (i, 2)),
                  pl.BlockSpec((1, c), const),
                  pl.BlockSpec((c, c), const),
                  pl.BlockSpec((1, c), const),
                  pl.BlockSpec((D_MODEL, D_MODEL), const),
                  pl.BlockSpec((1, N_MOD, D_MODEL), lambda i: (mi(i), 0, 0))],
        out_specs=pl.BlockSpec((ROW_TILE, D_MODEL), row),
        out_shape=jax.ShapeDtypeStruct((lay.t, D_MODEL), F32),
        scratch_shapes=[pltpu.VMEM((SSM_COL_BLOCKS, ROW_TILE, LANES), F32)],
        compiler_params=_cparams(("arbitrary",)),
        name="even_out",
    )(x, y_a, y_g, proj, ssm_d.reshape(1, -1), glu_w_bf16, glu_b.reshape(1, -1), w_out_bf16, mods)


def _softmax_pv(parts, sink_col):
    m = sink_col
    for s, _ in parts:
        m = jnp.maximum(m, jnp.max(s, axis=-1, keepdims=True))
    den = jnp.exp(sink_col - m)
    acc = None
    for s, v in parts:
        p = jnp.exp(s - m)
        den = den + jnp.sum(p, axis=-1, keepdims=True)
        pv = jnp.dot(p.astype(BF16), v.astype(BF16), preferred_element_type=F32)
        acc = pv if acc is None else acc + pv
    return acc / den


def _nt_dot(a, b):
    return lax.dot_general(a.astype(BF16), b.astype(BF16), (((1,), (1,)), ((), ())),
                           preferred_element_type=F32)


def _attn_ctx_kernel(q_ref, k_ref, v_ref, sink_ref, o_ref):
    n = q_ref.shape[0]
    for kh in range(N_KV):
        k = k_ref[:, kh * HEAD_DIM:(kh + 1) * HEAD_DIM]
        v = v_ref[:, kh * HEAD_DIM:(kh + 1) * HEAD_DIM]
        for g in range(GQA):
            h = kh * GQA + g
            q = q_ref[:, h * HEAD_DIM:(h + 1) * HEAD_DIM]
            s = _nt_dot(q, k) * ATTN_SCALE
            sink = jnp.broadcast_to(sink_ref[0:1, h:h + 1], (n, 1))
            o_ref[:, h * HEAD_DIM:(h + 1) * HEAD_DIM] = _softmax_pv([(s, v)], sink)


def attn_context(lay, qkv, sink):
    nq = N_HEADS * HEAD_DIM
    nkv = N_KV * HEAD_DIM
    return pl.pallas_call(
        _attn_ctx_kernel,
        grid=(lay.n_ctx,),
        in_specs=[pl.BlockSpec((lay.s_ctx, nq), lambda b: (b, 0)),
                  pl.BlockSpec((lay.s_ctx, nkv), lambda b: (b, nq // nkv)),
                  pl.BlockSpec((lay.s_ctx, nkv), lambda b: (b, nq // nkv + 1)),
                  pl.BlockSpec((1, N_HEADS), lambda b: (0, 0))],
        out_specs=pl.BlockSpec((lay.s_ctx, nq), lambda b: (b, 0)),
        out_shape=jax.ShapeDtypeStruct((lay.t_ctx, nq), F32),
        compiler_params=_cparams(("arbitrary",)),
        name="attn_context",
    )(qkv, qkv, qkv, sink.reshape(1, -1))


def _rope(x, cos, sin):
    lane = lax.broadcasted_iota(I32, (x.shape[0], 2 * HEAD_DIM), 1) % HEAD_DIM
    outs = []
    for j in range(x.shape[1] // (2 * HEAD_DIM)):
        xs = x[:, j * 2 * HEAD_DIM:(j + 1) * 2 * HEAD_DIM]
        sw = jnp.where(lane < HEAD_DIM // 2,
                       pltpu.roll(xs, 2 * HEAD_DIM - HEAD_DIM // 2, 1),
                       pltpu.roll(xs, HEAD_DIM // 2, 1))
        outs.append(xs * cos + sw * sin)
    return outs


def _attn_lat_kernel(q_ref, k0_ref, k1_ref, k2_ref, v0_ref, v1_ref, v2_ref, ck_ref, cv_ref,
                     cq_ref, sq_ref, c0_ref, c1_ref, c2_ref, s0_ref, s1_ref, s2_ref, sink_ref,
                     o_ref, *, n_blk):
    j = pl.program_id(1)
    qb = Q_BLOCK
    q_parts = [qp * ATTN_SCALE for qp in _rope(q_ref[...], cq_ref[...], sq_ref[...])]
    k_parts = [_rope(kr[...], cr[...], sr[...])
               for kr, cr, sr in ((k0_ref, c0_ref, s0_ref), (k1_ref, c1_ref, s1_ref),
                                  (k2_ref, c2_ref, s2_ref))]
    qi = lax.broadcasted_iota(I32, (qb, 3 * qb), 0)
    km = lax.broadcasted_iota(I32, (qb, 3 * qb), 1)
    kpos = j * qb - qb + km
    mask1 = (jnp.abs(km - qb - qi) <= WINDOW) & (kpos >= 0) & (kpos < n_blk * qb)
    mask = jnp.concatenate([mask1] * GQA, axis=0)
    for kh in range(N_KV):
        half = (kh % 2) * HEAD_DIM
        k_loc = jnp.concatenate([kp[kh // 2][:, half:half + HEAD_DIM] for kp in k_parts], axis=0)
        v_loc = jnp.concatenate([vr[:, kh * HEAD_DIM:(kh + 1) * HEAD_DIM]
                                 for vr in (v0_ref, v1_ref, v2_ref)], axis=0)
        qs, sinks = [], []
        for g in range(GQA):
            h = kh * GQA + g
            qs.append(q_parts[h // 2][:, (h % 2) * HEAD_DIM:(h % 2 + 1) * HEAD_DIM])
            sinks.append(jnp.broadcast_to(sink_ref[0:1, h:h + 1], (qb, 1)))
        q = jnp.concatenate(qs, axis=0)
        sink = jnp.concatenate(sinks, axis=0)
        s_loc = jnp.where(mask, _nt_dot(q, k_loc), NEG_INF)
        s_ctx = _nt_dot(q, ck_ref[0, kh])
        o = _softmax_pv([(s_loc, v_loc), (s_ctx, cv_ref[0, kh])], sink)
        for g in range(GQA):
            h = kh * GQA + g
            o_ref[:, h * HEAD_DIM:(h + 1) * HEAD_DIM] = o[g * qb:(g + 1) * qb]


def _rope_tables(s_len):
    rows = s_len // GRID_W
    row = jnp.repeat(jnp.arange(rows), GRID_W).astype(F32)
    col = jnp.tile(jnp.arange(GRID_W), rows).astype(F32)
    nf = HEAD_DIM // 4
    inv = ROPE_BASE ** (-jnp.arange(nf, dtype=F32) / nf)
    ang = jnp.concatenate([row[:, None] * inv, col[:, None] * inv], axis=-1)
    cos, sin = jnp.cos(ang), jnp.sin(ang)
    cos2 = jnp.tile(jnp.concatenate([cos, cos], axis=-1), (1, 2))
    sin2 = jnp.tile(jnp.concatenate([-sin, sin], axis=-1), (1, 2))
    return cos2, sin2


def attn_latent(lay, qkv, cache_k, cache_v, sink):
    nq = N_HEADS * HEAD_DIM
    nkv = N_KV * HEAD_DIM
    n_blk = lay.s_lat // Q_BLOCK
    base = lay.t_ctx // Q_BLOCK
    n_ctx_keys = cache_k.shape[2]
    cos2, sin2 = _rope_tables(lay.s_lat)
    kcol = nq // nkv

    def qrow(b, j):
        return base + b * n_blk + j

    def krow(off):
        return lambda b, j: base + b * n_blk + jnp.clip(j + off, 0, n_blk - 1)

    def trow(off):
        return lambda b, j: (jnp.clip(j + off, 0, n_blk - 1), 0)

    kv_spec = lambda off, col: pl.BlockSpec((Q_BLOCK, nkv), lambda b, j: (krow(off)(b, j), col))
    tab = lambda off: pl.BlockSpec((Q_BLOCK, 2 * HEAD_DIM), trow(off))
    cache_spec = pl.BlockSpec((1, N_KV, n_ctx_keys, HEAD_DIM), lambda b, j: (b, 0, 0, 0))
    return pl.pallas_call(
        functools.partial(_attn_lat_kernel, n_blk=n_blk),
        grid=(lay.n_lat, n_blk),
        in_specs=[pl.BlockSpec((Q_BLOCK, nq), lambda b, j: (qrow(b, j), 0)),
                  kv_spec(-1, kcol), kv_spec(0, kcol), kv_spec(1, kcol),
                  kv_spec(-1, kcol + 1), kv_spec(0, kcol + 1), kv_spec(1, kcol + 1),
                  cache_spec, cache_spec,
                  tab(0), tab(0), tab(-1), tab(0), tab(1), tab(-1), tab(0), tab(1),
                  pl.BlockSpec((1, N_HEADS), lambda b, j: (0, 0))],
        out_specs=pl.BlockSpec((Q_BLOCK, nq), lambda b, j: (b * n_blk + j, 0)),
        out_shape=jax.ShapeDtypeStruct((lay.t_lat, nq), F32),
        compiler_params=_cparams(("arbitrary", "arbitrary")),
        name="attn_latent",
    )(qkv, qkv, qkv, qkv, qkv, qkv, qkv, cache_k, cache_v,
      cos2, sin2, cos2, cos2, cos2, sin2, sin2, sin2, sink.reshape(1, -1))


def _mm_res_kernel(x_ref, ac_ref, al_ref, w_ref, mod_ref, o_ref, *, n_ctx_tiles):
    a = jnp.where(pl.program_id(0) < n_ctx_tiles, ac_ref[...], al_ref[...])
    out = jnp.dot(a.astype(BF16), w_ref[...], preferred_element_type=F32)
    o_ref[...] = x_ref[...] + mod_ref[0, 2:3, :] * out


def matmul_residual(lay, x, a_ctx, a_lat, w_bf16, mods):
    mi = _mod_index(lay, ROW_TILE)
    k = a_ctx.shape[1]
    nct = lay.t_ctx // ROW_TILE
    return pl.pallas_call(
        functools.partial(_mm_res_kernel, n_ctx_tiles=nct),
        grid=(lay.t // ROW_TILE,),
        in_specs=[pl.BlockSpec((ROW_TILE, D_MODEL), lambda i: (i, 0)),
                  pl.BlockSpec((ROW_TILE, k), lambda i: (jnp.minimum(i, nct - 1), 0)),
                  pl.BlockSpec((ROW_TILE, k), lambda i: (jnp.maximum(i - nct, 0), 0)),
                  pl.BlockSpec((k, D_MODEL), lambda i: (0, 0)),
                  pl.BlockSpec((1, N_MOD, D_MODEL), lambda i: (mi(i), 0, 0))],
        out_specs=pl.BlockSpec((ROW_TILE, D_MODEL), lambda i: (i, 0)),
        out_shape=jax.ShapeDtypeStruct((lay.t, D_MODEL), F32),
        compiler_params=_cparams(("arbitrary",)),
        name="matmul_residual",
    )(x, a_ctx, a_lat, w_bf16, mods)


def _rowtile_load(ref, n, base=0):
    parts = []
    for c in range(ROW_CHUNKS):
        words = ref[pl.ds(base + c, n, stride=ROW_CHUNKS), :]
        for half in range(2):
            parts.append(pltpu.unpack_elementwise(words, index=half, packed_dtype=BF16,
                                                  unpacked_dtype=F32))
    return jnp.concatenate(parts, axis=1)


def _rowtile_store(ref, val, n):
    for c in range(ROW_CHUNKS):
        lo = val[:, 2 * c * LANES:(2 * c + 1) * LANES]
        hi = val[:, (2 * c + 1) * LANES:(2 * c + 2) * LANES]
        ref[pl.ds(c, n, stride=ROW_CHUNKS), :] = pltpu.pack_elementwise([lo, hi], packed_dtype=BF16)


def _row_copy(src, src_row, dst, dst_row, sem):
    return pltpu.make_async_copy(
        src.at[pl.ds(pl.multiple_of(src_row * ROW_CHUNKS, ROW_CHUNKS), ROW_CHUNKS), :],
        dst.at[pl.ds(pl.multiple_of(dst_row * ROW_CHUNKS, ROW_CHUNKS), ROW_CHUNKS), :], sem)


def _router_kernel(x_ref, g_ref, mod_ref, rwt_ref, rb_ref, tri_ref, sg_ref, su_ref, sd_ref,
                   hn_ref, sh_ref, eidx_ref, wts_ref, rank_ref, cnt_ref, cnt_s):
    tm = ROUTER_TILE

    @pl.when(pl.program_id(0) == 0)
    def _():
        cnt_s[...] = jnp.zeros_like(cnt_s)

    h = _modnorm(x_ref[...], g_ref[...], mod_ref, 3)
    _rowtile_store(hn_ref, h, tm)
    hb = h.astype(BF16)
    sgate = jnp.dot(hb, sg_ref[...], preferred_element_type=F32)
    sup = jnp.dot(hb, su_ref[...], preferred_element_type=F32)
    sh_ref[...] = jnp.dot((sgate * jax.nn.sigmoid(sgate) * sup).astype(BF16), sd_ref[...],
                          preferred_element_type=F32)
    logits = lax.dot_general(rwt_ref[...], h, (((1,), (1,)), ((), ())),
                             precision=HIGHEST, preferred_element_type=F32)
    scores = jax.nn.sigmoid(logits)
    choice = scores + rb_ref[...]
    gs_rows = []
    for g in range(N_GROUPS):
        cg = choice[g * GROUP_SIZE:(g + 1) * GROUP_SIZE, :]
        m1 = jnp.max(cg, axis=0, keepdims=True)
        eq = cg == m1
        cnt = jnp.sum(eq.astype(F32), axis=0, keepdims=True)
        m2 = jnp.max(jnp.where(eq, -jnp.inf, cg), axis=0, keepdims=True)
        gs_rows.append(m1 + jnp.where(cnt >= 2.0, m1, m2))
    gs = jnp.concatenate(gs_rows, axis=0)
    gi = lax.broadcasted_iota(I32, (N_GROUPS, tm), 0)
    grank = jnp.zeros((N_GROUPS, tm), I32)
    for g in range(N_GROUPS):
        other = gs[g:g + 1, :]
        ahead = (other > gs) | ((other == gs) & (g < gi))
        grank = grank + ahead.astype(I32)
    gsel = grank < TOPK_GROUPS
    emask = jnp.concatenate(
        [jnp.broadcast_to(gsel[g:g + 1, :], (GROUP_SIZE, tm)) for g in range(N_GROUPS)], axis=0)
    masked = jnp.where(emask, choice, -jnp.inf)
    ei = lax.broadcasted_iota(I32, (N_EXPERTS, tm), 0)
    idxs, ws = [], []
    member = jnp.zeros((N_EXPERTS, tm), F32)
    for _ in range(TOP_K):
        m = jnp.max(masked, axis=0, keepdims=True)
        idx = jnp.min(jnp.where(masked == m, ei, N_EXPERTS), axis=0, keepdims=True)
        hit = ei == idx
        ws.append(jnp.sum(jnp.where(hit, scores, 0.0), axis=0, keepdims=True))
        idxs.append(idx)
        member = jnp.where(hit, 1.0, member)
        masked = jnp.where(hit, -jnp.inf, masked)
    w = jnp.concatenate(ws, axis=0)
    wts_ref[...] = w / jnp.sum(w, axis=0, keepdims=True) * ROUTE_SCALE
    eidx_ref[...] = jnp.concatenate(idxs, axis=0)
    before = jnp.dot(member.astype(BF16), tri_ref[...], preferred_element_type=F32) + cnt_s[...]
    ranks = [jnp.sum(jnp.where(ei == idx, before, 0.0), axis=0, keepdims=True) for idx in idxs]
    rank_ref[...] = jnp.concatenate(ranks, axis=0).astype(I32)
    cnt_s[...] = cnt_s[...] + jnp.sum(member, axis=1, keepdims=True)
    cnt_ref[...] = jnp.broadcast_to(cnt_s[...], cnt_ref.shape)


def moe_router(lay, x, g, mods, router_w, router_b, sg_bf16, su_bf16, sd_bf16):
    t = lay.t
    tm = ROUTER_TILE
    mi = _mod_index(lay, tm)
    tri = (jnp.arange(tm)[:, None] < jnp.arange(tm)[None, :]).astype(BF16)
    tok = lambda i: (0, i)
    const = lambda i: (0, 0)
    return pl.pallas_call(
        _router_kernel,
        grid=(t // tm,),
        in_specs=[pl.BlockSpec((tm, D_MODEL), lambda i: (i, 0)),
                  pl.BlockSpec((1, D_MODEL), const),
                  pl.BlockSpec((1, N_MOD, D_MODEL), lambda i: (mi(i), 0, 0)),
                  pl.BlockSpec((N_EXPERTS, D_MODEL), const),
                  pl.BlockSpec((N_EXPERTS, 1), const),
                  pl.BlockSpec((tm, tm), const),
                  pl.BlockSpec((D_MODEL, D_EXPERT), const),
                  pl.BlockSpec((D_MODEL, D_EXPERT), const),
                  pl.BlockSpec((D_EXPERT, D_MODEL), const)],
        out_specs=[pl.BlockSpec((tm * ROW_CHUNKS, LANES), lambda i: (i, 0)),
                   pl.BlockSpec((tm, D_MODEL), lambda i: (i, 0)),
                   pl.BlockSpec((TOP_K, tm), tok),
                   pl.BlockSpec((TOP_K, tm), tok),
                   pl.BlockSpec((TOP_K, tm), tok),
                   pl.BlockSpec((N_EXPERTS, LANES), const)],
        out_shape=[jax.ShapeDtypeStruct((t * ROW_CHUNKS, LANES), PACKED),
                   jax.ShapeDtypeStruct((t, D_MODEL), F32),
                   jax.ShapeDtypeStruct((TOP_K, t), I32),
                   jax.ShapeDtypeStruct((TOP_K, t), F32),
                   jax.ShapeDtypeStruct((TOP_K, t), I32),
                   jax.ShapeDtypeStruct((N_EXPERTS, LANES), F32)],
        scratch_shapes=[pltpu.VMEM((N_EXPERTS, 1), F32)],
        compiler_params=_cparams(("arbitrary",)),
        name="moe_router",
    )(x, g.reshape(1, -1), mods, router_w.T, router_b.reshape(-1, 1), tri, sg_bf16, su_bf16, sd_bf16)


def _dest_kernel(start_ref, eidx_ref, rank_ref, dest_ref):
    e = eidx_ref[...]

    def body(i, acc):
        return jnp.where(e == i, start_ref[i], acc)

    dest_ref[...] = lax.fori_loop(0, N_EXPERTS, body, jnp.zeros_like(e), unroll=8) + rank_ref[...]


def moe_dest(pad_start, eidx, rank):
    t = eidx.shape[1]
    tn = DEST_TILE
    spec = pl.BlockSpec((TOP_K, tn), lambda i, ps: (0, i))
    return pl.pallas_call(
        _dest_kernel,
        grid_spec=pltpu.PrefetchScalarGridSpec(
            num_scalar_prefetch=1, grid=(t // tn,), in_specs=[spec, spec], out_specs=spec),
        out_shape=jax.ShapeDtypeStruct((TOP_K, t), I32),
        compiler_params=_cparams(("arbitrary",)),
        name="moe_dest",
    )(pad_start, eidx, rank)


def _issue_row_copies(idx_at, n, copy_at, unroll=4):
    def body(i, c):
        for p in range(2):
            r = 2 * i + p
            copy_at(r, idx_at(r)).start(priority=p)
        return c
    lax.fori_loop(0, n // 2, body, 0, unroll=unroll)


def _dispatch_kernel(zrow_ref, dest_hbm, hn_ref, xs_hbm, idx_s, zbuf, isem, zsem, ssem, *, n_tiles):
    i = pl.program_id(0)
    slot = i % 2
    td = DISPATCH_TILE

    n_idx = TOP_K * td

    def idx_copy(tile, s):
        return pltpu.make_async_copy(dest_hbm.at[tile], idx_s.at[pl.ds(s * n_idx, n_idx)], isem.at[s])

    def zero_copy(e):
        r0 = pl.multiple_of(zrow_ref[e] * ROW_CHUNKS, ROW_CHUNKS)
        return pltpu.make_async_copy(zbuf, xs_hbm.at[pl.ds(r0, MOE_BLOCK * ROW_CHUNKS), :], zsem)

    @pl.when(i == 0)
    def _():
        zbuf[...] = jnp.zeros_like(zbuf)

        def zstart(e, c):
            @pl.when(zrow_ref[e] >= 0)
            def _():
                zero_copy(e).start()
            return c

        def zwait(e, c):
            @pl.when(zrow_ref[e] >= 0)
            def _():
                zero_copy(e).wait()
            return c

        lax.fori_loop(0, zrow_ref.shape[0], zstart, 0)
        idx_copy(0, 0).start()
        lax.fori_loop(0, zrow_ref.shape[0], zwait, 0)

    idx_copy(i, slot).wait()

    @pl.when(i + 1 < n_tiles)
    def _():
        idx_copy(i + 1, 1 - slot).start()

    for k in range(TOP_K):
        _issue_row_copies(lambda r: idx_s[slot * n_idx + k * td + r], td,
                          lambda r, d: _row_copy(hn_ref, r, xs_hbm, d, ssem))
    for k in range(TOP_K):
        pltpu.make_async_copy(hn_ref, xs_hbm.at[pl.ds(0, td * ROW_CHUNKS), :], ssem).wait()


def _tile_major(dest, tile):
    t = dest.shape[1]
    return dest.reshape(TOP_K, t // tile, tile).transpose(1, 0, 2).reshape(t // tile, TOP_K * tile)


def moe_dispatch(lay, hn, dest, zero_row, n_rows):
    td = DISPATCH_TILE
    n_tiles = lay.t // td
    return pl.pallas_call(
        functools.partial(_dispatch_kernel, n_tiles=n_tiles),
        grid_spec=pltpu.PrefetchScalarGridSpec(
            num_scalar_prefetch=1,
            grid=(n_tiles,),
            in_specs=[pl.BlockSpec(memory_space=pl.ANY),
                      pl.BlockSpec((td * ROW_CHUNKS, LANES), lambda i, z: (i, 0))],
            out_specs=pl.BlockSpec(memory_space=pl.ANY),
            scratch_shapes=[pltpu.SMEM((2 * TOP_K * td,), I32),
                            pltpu.VMEM((MOE_BLOCK * ROW_CHUNKS, LANES), PACKED),
                            pltpu.SemaphoreType.DMA((2,)),
                            pltpu.SemaphoreType.DMA,
                            pltpu.SemaphoreType.DMA]),
        out_shape=jax.ShapeDtypeStruct((n_rows * ROW_CHUNKS, LANES), PACKED),
        compiler_params=_cparams(("arbitrary",)),
        name="moe_dispatch",
    )(zero_row, dest, hn)


def _expert_kernel(blk0_ref, nblk_ref, tail_ref, xs_hbm, wg_ref, wu_ref, wd_ref, y_hbm,
                   xbuf, ybuf, wg_s, wu_s, wd_s, isem, osem):
    e = pl.program_id(0)
    nb = nblk_ref[e]
    g0 = blk0_ref[e]
    total = blk0_ref[N_EXPERTS - 1] + nblk_ref[N_EXPERTS - 1]
    blk_rows = MOE_BLOCK * ROW_CHUNKS
    n_x = xbuf.shape[0]
    n_y = ybuf.shape[0]

    def block_rows(g):
        return pl.ds(pl.multiple_of(g * blk_rows, blk_rows), blk_rows)

    def fetch(g):
        s = g % n_x
        return pltpu.make_async_copy(xs_hbm.at[block_rows(g), :], xbuf.at[s], isem.at[s])

    def writeback(g):
        s = g % n_y
        return pltpu.make_async_copy(ybuf.at[s], y_hbm.at[block_rows(g), :], osem.at[s])

    ahead = n_x // 2

    @pl.when(e == 0)
    def _():
        for p in range(ahead):
            @pl.when(p < total)
            def _():
                fetch(p).start()

    @pl.when(nb > 0)
    def _():
        wg_s[...] = wg_ref[0, 0].astype(BF16)
        wu_s[...] = wu_ref[0, 0].astype(BF16)
        wd_s[...] = wd_ref[0, 0].astype(BF16)

    def run_blocks(g, n):
        for p in range(n):
            @pl.when(g + ahead + p < total)
            def _():
                fetch(g + ahead + p).start()
        for p in range(n):
            fetch(g + p).wait()
        x = jnp.concatenate([_rowtile_load(xbuf.at[(g + p) % n_x], MOE_BLOCK) for p in range(n)],
                            axis=0).astype(BF16)
        gate = jnp.dot(x, wg_s[...], preferred_element_type=F32)
        up = jnp.dot(x, wu_s[...], preferred_element_type=F32)
        act = gate * jax.nn.sigmoid(gate) * up
        y = jnp.dot(act.astype(BF16), wd_s[...], preferred_element_type=F32)
        for p in range(n):
            @pl.when(g + p >= n_y)
            def _():
                writeback(g + p - n_y).wait()
        for p in range(n):
            _rowtile_store(ybuf.at[(g + p) % n_y], y[p * MOE_BLOCK:(p + 1) * MOE_BLOCK], MOE_BLOCK)
            writeback(g + p).start()

    def quad(jj, c):
        run_blocks(g0 + 4 * jj, 4)
        return c

    lax.fori_loop(0, nb // 4, quad, 0)
    rem = nb % 4

    @pl.when(rem >= 2)
    def _():
        run_blocks(g0 + nb - rem, 2)

    @pl.when(rem % 2 == 1)
    def _():
        run_blocks(g0 + nb - 1, 1)

    @pl.when(e == N_EXPERTS - 1)
    def _():
        for p in range(n_y, 0, -1):
            @pl.when(total >= p)
            def _():
                writeback(total - p).wait()

        ybuf[0] = jnp.zeros(ybuf.shape[1:], PACKED)

        def tail_copy(i):
            r0 = pl.multiple_of(tail_ref[i] * ROW_CHUNKS, blk_rows)
            return pltpu.make_async_copy(ybuf.at[0], y_hbm.at[pl.ds(r0, blk_rows), :], osem.at[0])

        def tstart(i, c):
            @pl.when(tail_ref[i] >= 0)
            def _():
                tail_copy(i).start()
            return c

        def twait(i, c):
            @pl.when(tail_ref[i] >= 0)
            def _():
                tail_copy(i).wait()
            return c

        lax.fori_loop(0, tail_ref.shape[0], tstart, 0)
        lax.fori_loop(0, tail_ref.shape[0], twait, 0)


def moe_experts(xs, first_block, n_blocks, tail_row, layer, w_gate, w_up, w_down):
    wspec = lambda shape: pl.BlockSpec((1, 1) + shape, lambda e, a, b, c: (layer, e, 0, 0))
    blk = (MOE_BLOCK * ROW_CHUNKS, LANES)
    return pl.pallas_call(
        _expert_kernel,
        grid_spec=pltpu.PrefetchScalarGridSpec(
            num_scalar_prefetch=3,
            grid=(N_EXPERTS,),
            in_specs=[pl.BlockSpec(memory_space=pl.ANY),
                      wspec((D_MODEL, D_EXPERT)), wspec((D_MODEL, D_EXPERT)),
                      wspec((D_EXPERT, D_MODEL))],
            out_specs=pl.BlockSpec(memory_space=pl.ANY),
            scratch_shapes=[pltpu.VMEM((EXPERT_X_BUFS,) + blk, PACKED),
                            pltpu.VMEM((EXPERT_Y_BUFS,) + blk, PACKED),
                            pltpu.VMEM((D_MODEL, D_EXPERT), BF16),
                            pltpu.VMEM((D_MODEL, D_EXPERT), BF16),
                            pltpu.VMEM((D_EXPERT, D_MODEL), BF16),
                            pltpu.SemaphoreType.DMA((EXPERT_X_BUFS,)),
                            pltpu.SemaphoreType.DMA((EXPERT_Y_BUFS,))]),
        out_shape=jax.ShapeDtypeStruct(xs.shape, PACKED),
        compiler_params=_cparams(("arbitrary",)),
        name="moe_experts",
    )(first_block, n_blocks, tail_row, xs, w_gate, w_up, w_down)


def _combine_kernel(dest_hbm, y_hbm, x_ref, sh_ref, w_ref, mod_ref, o_ref,
                    idx_s, ybuf, isem, gsem, *, n_tiles):
    i = pl.program_id(0)
    slot = i % 2
    tm = COMBINE_TILE
    n_idx = TOP_K * tm
    nxt = jnp.minimum(i + 1, n_tiles - 1)
    nxt2 = jnp.minimum(i + 2, n_tiles - 1)

    def idx_copy(tile, s):
        return pltpu.make_async_copy(dest_hbm.at[tile], idx_s.at[pl.ds(s * n_idx, n_idx)], isem.at[s])

    def gather(s, unroll=4):
        _issue_row_copies(lambda r: idx_s[s * n_idx + r], n_idx,
                          lambda r, d: _row_copy(y_hbm, d, ybuf.at[s], r, gsem.at[s]), unroll)

    def gather_wait(s):
        pltpu.make_async_copy(y_hbm.at[pl.ds(0, n_idx * ROW_CHUNKS), :], ybuf.at[s], gsem.at[s]).wait()

    @pl.when(i == 0)
    def _():
        c = idx_copy(0, 0)
        c.start()
        c.wait()
        gather(0)
        idx_copy(nxt, 1).start()

    idx_copy(nxt, 1 - slot).wait()
    gather_wait(slot)
    gather(1 - slot, unroll=True)
    idx_copy(nxt2, slot).start()

    w = w_ref[...]
    routed = jnp.zeros((tm, D_MODEL), F32)
    for k in range(TOP_K):
        routed = routed + w[:, k:k + 1] * _rowtile_load(ybuf.at[slot], tm, base=k * tm * ROW_CHUNKS)
    o_ref[...] = x_ref[...] + mod_ref[0, 5:6, :] * (routed + sh_ref[...])

    @pl.when(i == n_tiles - 1)
    def _():
        gather_wait(1 - slot)
        idx_copy(nxt2, slot).wait()


def moe_combine(lay, x, shared, y_rows, dest, wts_t, mods):
    tm = COMBINE_TILE
    n_tiles = lay.t // tm
    mi = _mod_index(lay, tm)
    row = lambda i: (i, 0)
    return pl.pallas_call(
        functools.partial(_combine_kernel, n_tiles=n_tiles),
        grid=(n_tiles,),
        in_specs=[pl.BlockSpec(memory_space=pl.ANY),
                  pl.BlockSpec(memory_space=pl.ANY),
                  pl.BlockSpec((tm, D_MODEL), row),
                  pl.BlockSpec((tm, D_MODEL), row),
                  pl.BlockSpec((tm, TOP_K), row),
                  pl.BlockSpec((1, N_MOD, D_MODEL), lambda i: (mi(i), 0, 0))],
        out_specs=pl.BlockSpec((tm, D_MODEL), row),
        out_shape=jax.ShapeDtypeStruct((lay.t, D_MODEL), F32),
        scratch_shapes=[pltpu.SMEM((2 * TOP_K * tm,), I32),
                        pltpu.VMEM((2, TOP_K * tm * ROW_CHUNKS, LANES), PACKED),
                        pltpu.SemaphoreType.DMA((2,)),
                        pltpu.SemaphoreType.DMA((2,))],
        compiler_params=_cparams(("arbitrary",)),
        name="moe_combine",
    )(dest, y_rows, x, shared, wts_t, mods)


def moe_layer(lay, x, g, mods, router_w, router_b, layer, w_gate, w_up, w_down, s_gate, s_up, s_down):
    t = lay.t
    hn, shared, eidx, wts, rank, cnt = moe_router(
        lay, x, g, mods, router_w, router_b, s_gate.astype(BF16), s_up.astype(BF16), s_down.astype(BF16))
    counts = cnt[:, 0].astype(I32)
    n_blocks = (counts + MOE_BLOCK - 1) // MOE_BLOCK
    padded = n_blocks * MOE_BLOCK
    pad_end = jnp.cumsum(padded)
    pad_start = pad_end - padded
    n_rows = -(-(t * TOP_K + N_EXPERTS * (MOE_BLOCK - 1)) // MOE_BLOCK) * MOE_BLOCK
    dest = moe_dest(pad_start, eidx, rank)
    last_row = jnp.where(n_blocks > 0, pad_end - MOE_BLOCK, -1)
    tail_blk = pad_end[-1] // MOE_BLOCK + jnp.arange(n_rows // MOE_BLOCK - t * TOP_K // MOE_BLOCK)
    tail_row = jnp.where(tail_blk < n_rows // MOE_BLOCK, tail_blk * MOE_BLOCK, -1).astype(I32)
    xs = moe_dispatch(lay, hn, _tile_major(dest, DISPATCH_TILE),
                      jnp.concatenate([last_row, tail_row]), n_rows)
    y_rows = moe_experts(xs, pad_start // MOE_BLOCK, n_blocks, tail_row, layer, w_gate, w_up, w_down)
    return moe_combine(lay, x, shared, y_rows, _tile_major(dest, COMBINE_TILE), wts.T, mods)


def _final_norm_kernel(x_ref, g_ref, o_ref):
    x = x_ref[...]
    ms = jnp.mean(x * x, axis=-1, keepdims=True)
    o_ref[...] = x * lax.rsqrt(ms + EPS) * g_ref[...]


def final_norm(x, g, row0, n_rows):
    base = row0 // ROW_TILE
    return pl.pallas_call(
        _final_norm_kernel,
        grid=(n_rows // ROW_TILE,),
        in_specs=[pl.BlockSpec((ROW_TILE, D_MODEL), lambda i: (base + i, 0)),
                  pl.BlockSpec((1, D_MODEL), lambda i: (0, 0))],
        out_specs=pl.BlockSpec((ROW_TILE, D_MODEL), lambda i: (i, 0)),
        out_shape=jax.ShapeDtypeStruct((n_rows, D_MODEL), F32),
        compiler_params=_cparams(("arbitrary",)),
        name="final_norm",
    )(x, g.reshape(1, -1))


def _block_diag(w):
    nb, bw, _ = w.shape
    eye = jnp.eye(nb, dtype=w.dtype)
    return (eye[:, None, :, None] * w[:, :, None, :]).reshape(nb * bw, nb * bw)


def even_layer(lay, x, mods, g_mix, p, state_lru, state_ssm_re, state_ssm_im):
    t = lay.t
    proj, u_g = modnorm_matmul(lay, x, g_mix, mods, 0, p['w_in'].astype(BF16),
                               ug_col=2 * D_LRU)
    zeros_c = jnp.zeros((lay.n_ctx, D_LRU), F32)
    hf_y, st = None, []
    for d in range(2):
        wg = jnp.concatenate([_block_diag(p['lru_wa'][d]), _block_diag(p['lru_wx'][d])], axis=1)
        bg = jnp.concatenate([p['lru_ba'][d], p['lru_bx'][d]])
        h0 = jnp.concatenate([zeros_c, state_lru[:, d].astype(F32)], axis=0)
        hf_y, s = lru_pass(lay, proj, p['conv_w'], p['conv_b'], wg.astype(BF16), bg,
                           p['lru_lam'][d], h0, reverse=(d == 1), hf=hf_y)
        st.append(s[:lay.n_ctx])
    y_a = hf_y
    new_lru = jnp.stack(st, axis=1)

    mats = _s5_matrices(p['a_re'], p['a_im'], p['log_dt'], p['b_re'], p['b_im'], p['c_re'], p['c_im'])
    h0 = jnp.concatenate([state_ssm_re, state_ssm_im], axis=-1).astype(F32)
    h0 = h0.transpose(2, 1, 0, 3)
    y_g, h_ctx = s5_mixer(lay, u_g, mats, h0)
    seg = lay.s_ctx // S5_CHUNK
    h_ctx = h_ctx.reshape(SSM_GROUPS, 2, lay.n_ctx, seg, 2 * SSM_STATE)
    ends = jnp.stack([h_ctx[:, 0, :, seg - 1], h_ctx[:, 1, :, 0]], axis=1)
    ends = ends.transpose(2, 1, 0, 3)
    x = even_out(lay, x, y_a, y_g, proj, p['d'], p['glu_w'].astype(BF16), p['glu_b'],
                 p['w_out'].astype(BF16), mods)
    return x, new_lru, ends[..., :SSM_STATE], ends[..., SSM_STATE:]


def odd_layer(lay, x, mods, g_mix, w_qkv, sink, w_out, cache_k, cache_v):
    qkv = modnorm_matmul(lay, x, g_mix, mods, 0, w_qkv.astype(BF16))
    o_ctx = attn_context(lay, qkv, sink)
    o_lat = attn_latent(lay, qkv, cache_k, cache_v, sink)
    nq = N_HEADS * HEAD_DIM
    kv = qkv[:lay.t_ctx, nq:].reshape(lay.n_ctx, lay.s_ctx, 2, N_KV, HEAD_DIM)
    k_new = kv[:, :, 0].swapaxes(1, 2)
    v_new = kv[:, :, 1].swapaxes(1, 2)
    x = matmul_residual(lay, x, o_ctx, o_lat, w_out.astype(BF16), mods)
    return x, k_new, v_new


def _forward(lay, x_prompt, x_sample, state_lru, state_ssm_re, state_ssm_im, cache_k, cache_v,
             c, c_ctx, g_mix, g_ffn, w_mod, b_mod,
             ev_w_in, lru_conv_w, lru_conv_b, lru_wa, lru_ba, lru_wx, lru_bx, lru_lam,
             ssm_a_re, ssm_a_im, ssm_log_dt, ssm_b_re, ssm_b_im, ssm_c_re, ssm_c_im, ssm_d,
             ssm_glu_w, ssm_glu_b, ev_w_out, at_w_qkv, at_sink, at_w_out,
             router_w, router_b, exp_w_gate, exp_w_up, exp_w_down, sh_w_gate, sh_w_up, sh_w_down,
             g_final):
    depth = g_mix.shape[0]
    x = jnp.concatenate([x_prompt.reshape(lay.t_ctx, D_MODEL), x_sample.reshape(lay.t_lat, D_MODEL)],
                        axis=0)
    n_c = 1 + lay.n_lat
    c_rows = jnp.concatenate([c_ctx[None, :], c, jnp.zeros((16 - n_c, D_MODEL), F32)], axis=0)
    new_lru, new_re, new_im, new_k, new_v = [], [], [], [], []
    for l in range(depth):
        i = l // 2
        mods = adaln_table(c_rows, l, w_mod, b_mod[l])
        if l % 2 == 0:
            p = dict(w_in=ev_w_in[i], conv_w=lru_conv_w[i], conv_b=lru_conv_b[i],
                     lru_wa=lru_wa[i], lru_ba=lru_ba[i], lru_wx=lru_wx[i], lru_bx=lru_bx[i],
                     lru_lam=lru_lam[i], a_re=ssm_a_re[i], a_im=ssm_a_im[i], log_dt=ssm_log_dt[i],
                     b_re=ssm_b_re[i], b_im=ssm_b_im[i], c_re=ssm_c_re[i], c_im=ssm_c_im[i],
                     d=ssm_d[i], glu_w=ssm_glu_w[i], glu_b=ssm_glu_b[i], w_out=ev_w_out[i])
            x, lru_i, re_i, im_i = even_layer(lay, x, mods, g_mix[l], p, state_lru[:, i],
                                              state_ssm_re[:, i], state_ssm_im[:, i])
            new_lru.append(lru_i)
            new_re.append(re_i)
            new_im.append(im_i)
        else:
            x, k_i, v_i = odd_layer(lay, x, mods, g_mix[l], at_w_qkv[i], at_sink[i], at_w_out[i],
                                    cache_k[:, i], cache_v[:, i])
            new_k.append(k_i)
            new_v.append(v_i)
        x = moe_layer(lay, x, g_ffn[l], mods, router_w[l], router_b[l], l, exp_w_gate, exp_w_up,
                      exp_w_down, sh_w_gate[l], sh_w_up[l], sh_w_down[l])
    y_prompt = final_norm(x, g_final, 0, lay.t_ctx).reshape(x_prompt.shape)
    y_sample = final_norm(x, g_final, lay.t_ctx, lay.t_lat).reshape(x_sample.shape)
    return (y_prompt, y_sample, jnp.stack(new_lru, axis=1), jnp.stack(new_re, axis=1),
            jnp.stack(new_im, axis=1), jnp.stack(new_k, axis=1), jnp.stack(new_v, axis=1))


def kernel(x_prompt, x_sample, state_lru, state_ssm_re, state_ssm_im, cache_k, cache_v, c, c_ctx, g_mix, g_ffn, w_mod, b_mod, ev_w_in, lru_conv_w, lru_conv_b, lru_wa, lru_ba, lru_wx, lru_bx, lru_lam, ssm_a_re, ssm_a_im, ssm_log_dt, ssm_b_re, ssm_b_im, ssm_c_re, ssm_c_im, ssm_d, ssm_glu_w, ssm_glu_b, ev_w_out, at_w_qkv, at_sink, at_w_out, router_w, router_b, exp_w_gate, exp_w_up, exp_w_down, sh_w_gate, sh_w_up, sh_w_down, g_final):
    lay = Layout(n_ctx=x_prompt.shape[0], s_ctx=x_prompt.shape[1],
                 n_lat=x_sample.shape[0], s_lat=x_sample.shape[1])
    return _forward(lay, x_prompt, x_sample, state_lru, state_ssm_re, state_ssm_im, cache_k, cache_v,
                    c, c_ctx, g_mix, g_ffn, w_mod, b_mod,
                    ev_w_in, lru_conv_w, lru_conv_b, lru_wa, lru_ba, lru_wx, lru_bx, lru_lam,
                    ssm_a_re, ssm_a_im, ssm_log_dt, ssm_b_re, ssm_b_im, ssm_c_re, ssm_c_im, ssm_d,
                    ssm_glu_w, ssm_glu_b, ev_w_out, at_w_qkv, at_sink, at_w_out,
                    router_w, router_b, exp_w_gate, exp_w_up, exp_w_down, sh_w_gate, sh_w_up,
                    sh_w_down, g_final)
```

```python
import functools
from typing import NamedTuple

import jax
import jax.numpy as jnp
from jax import lax
from jax.experimental import pallas as pl
from jax.experimental.pallas import tpu as pltpu

F32 = jnp.float32
BF16 = jnp.bfloat16
I32 = jnp.int32
HIGHEST = lax.Precision.HIGHEST

D_MODEL = 1024
EPS = 1e-6
N_MOD = 6
GRID_W = 64
D_LRU = 512
LRU_BLOCKS = 8
LRU_C = 8.0
CONV_W = 4
CONV_LEFT = 2
D_SSM = 512
SSM_GROUP = 16
SSM_GROUPS = 32
SSM_STATE = 64
S5_CHUNK = 16
S5_LANES = S5_CHUNK * SSM_GROUP
S5_SCAN_STEPS = 8
HEAD_DIM = 64
N_HEADS = 16
N_KV = 4
GQA = 4
WINDOW = 128
Q_BLOCK = 128
ROPE_BASE = 10000.0
ATTN_SCALE = HEAD_DIM ** -0.5
NEG_INF = -1e30
N_EXPERTS = 256
TOP_K = 8
N_GROUPS = 8
TOPK_GROUPS = 4
GROUP_SIZE = N_EXPERTS // N_GROUPS
D_EXPERT = 256
ROUTE_SCALE = 2.5
MOE_BLOCK = 128

SUBLANES = 8
LANES = 128
ROW_CHUNKS = D_MODEL // (2 * LANES)
PACKED = jnp.int32
SEQ_TILE = 256
ROW_TILE = 512
ROUTER_TILE = 512
DEST_TILE = 1024
DISPATCH_TILE = 512
COMBINE_TILE = 256
EXPERT_X_BUFS = 8
EXPERT_Y_BUFS = 8
VMEM_LIMIT = 56 * 1024 * 1024


class Layout(NamedTuple):
    n_ctx: int
    s_ctx: int
    n_lat: int
    s_lat: int

    @property
    def t_ctx(self):
        return self.n_ctx * self.s_ctx

    @property
    def t_lat(self):
        return self.n_lat * self.s_lat

    @property
    def t(self):
        return self.t_ctx + self.t_lat

    @property
    def n_seq(self):
        return self.n_ctx + self.n_lat


def _cparams(sem):
    return pltpu.CompilerParams(dimension_semantics=sem, vmem_limit_bytes=VMEM_LIMIT)


def _mod_index(lay, tile_rows):
    n_ctx_tiles = lay.t_ctx // tile_rows
    per_lat = lay.s_lat // tile_rows

    def f(i):
        return jnp.where(i < n_ctx_tiles, 0, 1 + (i - n_ctx_tiles) // per_lat)
    return f


def _adaln_kernel(c_ref, w_ref, b_ref, o_ref):
    c = c_ref[...]
    s = c * jax.nn.sigmoid(c)
    o_ref[...] = jnp.dot(s, w_ref[0], precision=HIGHEST, preferred_element_type=F32) + b_ref[...]


def adaln_table(c_rows, layer, w_mod, b_mod):
    n = c_rows.shape[0]
    tn = 1536
    out = pl.pallas_call(
        _adaln_kernel,
        grid=(N_MOD * D_MODEL // tn,),
        in_specs=[pl.BlockSpec((n, D_MODEL), lambda j: (0, 0)),
                  pl.BlockSpec((1, D_MODEL, tn), lambda j: (layer, 0, j)),
                  pl.BlockSpec((1, tn), lambda j: (0, j))],
        out_specs=pl.BlockSpec((n, tn), lambda j: (0, j)),
        out_shape=jax.ShapeDtypeStruct((n, N_MOD * D_MODEL), F32),
        compiler_params=_cparams(("arbitrary",)),
        name="adaln",
    )(c_rows, w_mod, b_mod.reshape(1, -1))
    return out.reshape(n, N_MOD, D_MODEL)


def _modnorm(x, g, mod_ref, slot):
    ms = jnp.mean(x * x, axis=-1, keepdims=True)
    y = x * lax.rsqrt(ms + EPS) * g
    shift = mod_ref[0, slot:slot + 1, :]
    scale = mod_ref[0, slot + 1:slot + 2, :]
    return y * (1.0 + scale) + shift


GROUPS_PER_VREG = LANES // SSM_GROUP
SSM_COL_BLOCKS = D_SSM // LANES


def _group_major_store(val, tmp_ref, dst_ref):
    rows = dst_ref.shape[1]
    for j in range(SSM_COL_BLOCKS):
        tmp_ref[j] = val[:, j * LANES:(j + 1) * LANES]
    for j in range(SSM_COL_BLOCKS):
        steps = [tmp_ref[j, pl.ds(i, rows, stride=S5_CHUNK), :] for i in range(S5_CHUNK)]
        for q in range(GROUPS_PER_VREG):
            dst_ref[j * GROUPS_PER_VREG + q] = jnp.concatenate(
                [w[:, q * SSM_GROUP:(q + 1) * SSM_GROUP] for w in steps], axis=1)


def _group_major_load(src_ref, tmp_ref):
    rows = src_ref.shape[1]
    for j in range(SSM_COL_BLOCKS):
        blocks = [src_ref[j * GROUPS_PER_VREG + q] for q in range(GROUPS_PER_VREG)]
        for i in range(S5_CHUNK):
            tmp_ref[j, pl.ds(i, rows, stride=S5_CHUNK), :] = jnp.concatenate(
                [b[:, i * SSM_GROUP:(i + 1) * SSM_GROUP] for b in blocks], axis=1)
    return jnp.concatenate([tmp_ref[j] for j in range(SSM_COL_BLOCKS)], axis=1)


def _modnorm_mm_kernel(x_ref, g_ref, mod_ref, w_ref, o_ref, *ug_refs, slot, ug_col):
    h = _modnorm(x_ref[...], g_ref[...], mod_ref, slot)
    out = jnp.dot(h.astype(BF16), w_ref[...], preferred_element_type=F32)
    o_ref[...] = out
    if ug_col is not None:
        ug_ref, tmp_ref = ug_refs
        _group_major_store(out[:, ug_col:ug_col + D_SSM], tmp_ref, ug_ref)


def modnorm_matmul(lay, x, g, mods, slot, w_bf16, ug_col=None):
    t = lay.t
    n = w_bf16.shape[1]
    mi = _mod_index(lay, ROW_TILE)
    out_specs = [pl.BlockSpec((ROW_TILE, n), lambda i: (i, 0))]
    out_shape = [jax.ShapeDtypeStruct((t, n), F32)]
    if ug_col is not None:
        out_specs.append(pl.BlockSpec((SSM_GROUPS, ROW_TILE // S5_CHUNK, S5_LANES), lambda i: (0, i, 0)))
        out_shape.append(jax.ShapeDtypeStruct((SSM_GROUPS, t // S5_CHUNK, S5_LANES), F32))
    outs = pl.pallas_call(
        functools.partial(_modnorm_mm_kernel, slot=slot, ug_col=ug_col),
        grid=(t // ROW_TILE,),
        in_specs=[pl.BlockSpec((ROW_TILE, D_MODEL), lambda i: (i, 0)),
                  pl.BlockSpec((1, D_MODEL), lambda i: (0, 0)),
                  pl.BlockSpec((1, N_MOD, D_MODEL), lambda i: (mi(i), 0, 0)),
                  pl.BlockSpec((D_MODEL, n), lambda i: (0, 0))],
        out_specs=out_specs,
        out_shape=out_shape,
        scratch_shapes=([pltpu.VMEM((SSM_COL_BLOCKS, ROW_TILE, LANES), F32)]
                        if ug_col is not None else []),
        compiler_params=_cparams(("arbitrary",)),
        name="modnorm_matmul",
    )(x, g.reshape(1, -1), mods, w_bf16)
    return outs if ug_col is not None else outs[0]


def _seq_tile_maps(lay, reverse):
    assert lay.s_ctx == SEQ_TILE and lay.s_lat % SEQ_TILE == 0
    n_tiles = lay.t // SEQ_TILE
    per_lat = lay.s_lat // SEQ_TILE

    def tile(i):
        return (n_tiles - 1 - i) if reverse else i

    def seq(i):
        ti = tile(i)
        return jnp.where(ti < lay.n_ctx, ti, lay.n_ctx + (ti - lay.n_ctx) // per_lat)

    return n_tiles, tile, seq


def _softplus(x):
    return jnp.maximum(x, 0.0) + jnp.log(1.0 + jnp.exp(-jnp.abs(x)))


def _lru_kernel(rec_ref, prev_ref, next_ref, cw_ref, cb_ref, wg_ref, bg_ref, lam_ref, h0_ref,
                *rest, reverse, n_ctx, per_lat, n_tiles):
    if reverse:
        gate_ref, hf_ref, y_ref, st_ref, a_s, b_s, h_s, carry = rest
    else:
        y_ref, st_ref, a_s, b_s, h_s, carry = rest
    i = pl.program_id(0)
    ti = (n_tiles - 1 - i) if reverse else i
    is_first = jnp.logical_or(ti < n_ctx, (ti - n_ctx) % per_lat == 0)
    is_last = jnp.logical_or(ti < n_ctx, (ti - n_ctx) % per_lat == per_lat - 1)
    ts = SEQ_TILE

    rec = rec_ref[...]
    prev = jnp.where(is_first, 0.0, prev_ref[...])
    nxt = jnp.where(is_last, 0.0, next_ref[...])
    ext = jnp.concatenate([prev, rec, nxt], axis=0)
    n_ext = ts + 2 * SUBLANES
    cw = cw_ref[...]
    xc = cb_ref[...] + cw[2:3, :] * rec
    xc = xc + cw[0:1, :] * pltpu.roll(ext, 2, 0)[SUBLANES:SUBLANES + ts]
    xc = xc + cw[1:2, :] * pltpu.roll(ext, 1, 0)[SUBLANES:SUBLANES + ts]
    xc = xc + cw[3:4, :] * pltpu.roll(ext, n_ext - 1, 0)[SUBLANES:SUBLANES + ts]

    gates = jax.nn.sigmoid(jnp.dot(xc.astype(BF16), wg_ref[...], preferred_element_type=F32)
                           + bg_ref[...])
    r = gates[:, :D_LRU]
    ig = gates[:, D_LRU:]
    log_a = (-LRU_C) * r * _softplus(-lam_ref[...])
    a = jnp.exp(log_a)
    b = jnp.sqrt(1.0 - jnp.exp(2.0 * log_a)) * (ig * xc)

    row8 = lax.broadcasted_iota(I32, (ts, D_LRU), 0) % SUBLANES
    for sh in (1, 2, 4):
        if reverse:
            keep = row8 < SUBLANES - sh
            a_sh = pltpu.roll(a, ts - sh, 0)
            b_sh = pltpu.roll(b, ts - sh, 0)
        else:
            keep = row8 >= sh
            a_sh = pltpu.roll(a, sh, 0)
            b_sh = pltpu.roll(b, sh, 0)
        b = b + a * jnp.where(keep, b_sh, 0.0)
        a = a * jnp.where(keep, a_sh, 1.0)
    a_s[...] = a
    b_s[...] = b

    @pl.when(is_last if reverse else is_first)
    def _():
        carry[...] = h0_ref[0]

    n_grp = ts // SUBLANES

    def body(k, c):
        gi = (n_grp - 1 - k) if reverse else k
        sl = pl.ds(pl.multiple_of(gi * SUBLANES, SUBLANES), SUBLANES)
        h = b_s[sl, :] + a_s[sl, :] * c
        h_s[sl, :] = h
        return h[0:1, :] if reverse else h[SUBLANES - 1:SUBLANES, :]

    c_fin = lax.fori_loop(0, n_grp, body, carry[...], unroll=4)
    carry[...] = c_fin
    st_ref[0] = c_fin
    if reverse:
        y_ref[...] = (hf_ref[...] + h_s[...]) * jax.nn.gelu(gate_ref[...])
    else:
        y_ref[...] = h_s[...]


def lru_pass(lay, proj, conv_w, conv_b, wg_bf16, bg, lam, h0, reverse, hf=None):
    n_tiles, tile, seq = _seq_tile_maps(lay, reverse)
    per_lat = lay.s_lat // SEQ_TILE
    blk8 = SEQ_TILE // SUBLANES
    last8 = lay.t // SUBLANES - 1
    c = D_LRU
    in_specs = [
        pl.BlockSpec((SEQ_TILE, c), lambda i: (tile(i), 1)),
        pl.BlockSpec((SUBLANES, c), lambda i: (jnp.maximum(tile(i) * blk8 - 1, 0), 1)),
        pl.BlockSpec((SUBLANES, c), lambda i: (jnp.minimum(tile(i) * blk8 + blk8, last8), 1)),
        pl.BlockSpec((CONV_W, c), lambda i: (0, 0)),
        pl.BlockSpec((1, c), lambda i: (0, 0)),
        pl.BlockSpec((c, 2 * c), lambda i: (0, 0)),
        pl.BlockSpec((1, 2 * c), lambda i: (0, 0)),
        pl.BlockSpec((1, c), lambda i: (0, 0)),
        pl.BlockSpec((1, 1, c), lambda i: (seq(i), 0, 0)),
    ]
    args = [proj, proj, proj, conv_w, conv_b.reshape(1, -1), wg_bf16, bg.reshape(1, -1),
            lam.reshape(1, -1), h0.reshape(lay.n_seq, 1, c)]
    if reverse:
        in_specs += [pl.BlockSpec((SEQ_TILE, c), lambda i: (tile(i), 0)),
                     pl.BlockSpec((SEQ_TILE, c), lambda i: (tile(i), 0))]
        args += [proj, hf]
    y, st = pl.pallas_call(
        functools.partial(_lru_kernel, reverse=reverse, n_ctx=lay.n_ctx, per_lat=per_lat,
                          n_tiles=n_tiles),
        grid=(n_tiles,),
        in_specs=in_specs,
        out_specs=[pl.BlockSpec((SEQ_TILE, c), lambda i: (tile(i), 0)),
                   pl.BlockSpec((1, 1, c), lambda i: (seq(i), 0, 0))],
        out_shape=[jax.ShapeDtypeStruct((lay.t, c), F32),
                   jax.ShapeDtypeStruct((lay.n_seq, 1, c), F32)],
        scratch_shapes=[pltpu.VMEM((SEQ_TILE, c), F32), pltpu.VMEM((SEQ_TILE, c), F32),
                        pltpu.VMEM((SEQ_TILE, c), F32), pltpu.VMEM((1, c), F32)],
        compiler_params=_cparams(("arbitrary",)),
        name="lru_bwd" if reverse else "lru_fwd",
    )(*args)
    return y, st.reshape(lay.n_seq, c)


def _cmul(a, b):
    return a[0] * b[0] - a[1] * b[1], a[0] * b[1] + a[1] * b[0]


def _s5_matrices(a_re, a_im, log_dt, b_re, b_im, c_re, c_im):
    a_re, a_im = a_re.astype(F32), a_im.astype(F32)
    dt = jnp.exp(log_dt.astype(F32))[..., None]
    z = (a_re * dt, a_im * dt)

    def zpow(k):
        k = k.reshape((-1,) + (1,) * z[0].ndim)
        mag = jnp.exp(k * z[0][None])
        return mag * jnp.cos(k * z[1][None]), mag * jnp.sin(k * z[1][None])

    a_bar = zpow(jnp.ones((1,), F32))
    a_bar = (a_bar[0][0], a_bar[1][0])
    den = a_re * a_re + a_im * a_im
    xr, xi = a_bar[0] - 1.0, a_bar[1]
    q = ((xr * a_re + xi * a_im) / den, (xi * a_re - xr * a_im) / den)
    b_bar = _cmul((q[0][..., None], q[1][..., None]), (b_re.astype(F32), b_im.astype(F32)))
    cc = (c_re.astype(F32), c_im.astype(F32))
    el = S5_CHUNK
    pw = zpow(jnp.arange(el + 1, dtype=F32))
    idx = jnp.arange(el)
    m_in, m_toep, m_out = [], [], []
    for d in range(2):
        p_d = (pw[0][:, d], pw[1][:, d])
        b_d = (b_bar[0][d], b_bar[1][d])
        c_d = (cc[0][d], cc[1][d])
        k_in = (el - 1 - idx) if d == 0 else idx
        w_in = _cmul((p_d[0][k_in][..., None], p_d[1][k_in][..., None]),
                     (b_d[0][None], b_d[1][None]))
        w_in = [jnp.transpose(w, (1, 0, 3, 2)).reshape(SSM_GROUPS, S5_LANES, SSM_STATE) for w in w_in]
        m_in.append(jnp.concatenate(w_in, axis=-1))
        cp = _cmul((c_d[0][None], c_d[1][None]),
                   (p_d[0][:, :, None, :], p_d[1][:, :, None, :]))
        kern = (jnp.einsum('kghp,gpc->kgch', cp[0][:el], b_d[0])
                - jnp.einsum('kghp,gpc->kgch', cp[1][:el], b_d[1]))
        zero = jnp.zeros_like(kern[0])
        rows = []
        for i in range(el):
            if d == 0:
                pieces = [zero] * i + [kern[k] for k in range(el - i)]
            else:
                pieces = [kern[i - j] for j in range(i + 1)] + [zero] * (el - 1 - i)
            rows.append(jnp.concatenate(pieces, axis=-1))
        m_toep.append(jnp.stack(rows, axis=1).reshape(SSM_GROUPS, S5_LANES, S5_LANES))
        k_out = (idx + 1) if d == 0 else (el - idx)
        w_out = [jnp.transpose(w[k_out], (1, 3, 0, 2)).reshape(SSM_GROUPS, SSM_STATE, S5_LANES)
                 for w in cp]
        m_out.append(jnp.concatenate([w_out[0], -w_out[1]], axis=1))
    mul = zpow(el * 2.0 ** jnp.arange(S5_SCAN_STEPS, dtype=F32))
    mul = [jnp.transpose(m, (2, 1, 0, 3)) for m in mul]
    coef_a = jnp.concatenate([mul[0], mul[0]], axis=-1)
    coef_b = jnp.concatenate([-mul[1], mul[1]], axis=-1)
    stack = lambda xs: jnp.stack(xs, axis=1)
    return (stack(m_in).astype(BF16), stack(m_toep).astype(BF16), stack(m_out).astype(BF16),
            coef_a, coef_b)


def _s5_scan(v, ca, cb, seg, reverse):
    n = v.shape[0]
    assert seg <= 2 ** S5_SCAN_STEPS
    row = lax.broadcasted_iota(I32, (n, 2 * SSM_STATE), 0) % seg
    k, sh = 0, 1
    while sh < seg:
        if reverse:
            s = jnp.where(row < seg - sh, pltpu.roll(v, n - sh, 0), 0.0)
        else:
            s = jnp.where(row >= sh, pltpu.roll(v, sh, 0), 0.0)
        v = v + ca[k:k + 1, :] * s + cb[k:k + 1, :] * pltpu.roll(s, SSM_STATE, 1)
        k += 1
        sh *= 2
    return v


def _s5_shift(h, seg, reverse):
    n = h.shape[0]
    row = lax.broadcasted_iota(I32, (n, 2 * SSM_STATE), 0) % seg
    if reverse:
        return jnp.where(row < seg - 1, pltpu.roll(h, n - 1, 0), 0.0)
    return jnp.where(row >= 1, pltpu.roll(h, 1, 0), 0.0)


def _s5_kernel(u_ref, min_ref, mtoep_ref, mout_ref, ca_ref, cb_ref, h0_ref, y_ref, hc_ref,
               v_s, hp_s, *, rc, seg_c, n_lat, seg_l):
    u = u_ref[0].astype(BF16)
    u_c, u_l = u[:rc], u[rc:]
    y_c = jnp.zeros((rc, S5_LANES), F32)
    y_l = jnp.zeros((n_lat * seg_l, S5_LANES), F32)
    for d in range(2):
        reverse = d == 1
        ca = ca_ref[0, d]
        cb = cb_ref[0, d]
        m_in = min_ref[0, d]
        m_toep = mtoep_ref[0, d]
        m_out = mout_ref[0, d]
        h_c = _s5_scan(jnp.dot(u_c, m_in, preferred_element_type=F32), ca, cb, seg_c, reverse)
        hc_ref[0, d] = h_c
        hp_c = _s5_shift(h_c, seg_c, reverse)
        y_c = y_c + jnp.dot(u_c, m_toep, preferred_element_type=F32)
        y_c = y_c + jnp.dot(hp_c.astype(BF16), m_out, preferred_element_type=F32)
        v_s[...] = jnp.dot(u_l, m_in, preferred_element_type=F32)
        for s in range(n_lat):
            h0 = h0_ref[0, d, s:s + 1, :]
            r0 = s * seg_l + (seg_l - 1 if reverse else 0)
            v_s[r0:r0 + 1, :] = (v_s[r0:r0 + 1, :] + ca[0:1, :] * h0
                                 + cb[0:1, :] * pltpu.roll(h0, SSM_STATE, 1))
        h_l = _s5_scan(v_s[...], ca, cb, seg_l, reverse)
        hp_s[...] = _s5_shift(h_l, seg_l, reverse)
        for s in range(n_lat):
            r0 = s * seg_l + (seg_l - 1 if reverse else 0)
            hp_s[r0:r0 + 1, :] = h0_ref[0, d, s:s + 1, :]
        y_l = y_l + jnp.dot(u_l, m_toep, preferred_element_type=F32)
        y_l = y_l + jnp.dot(hp_s[...].astype(BF16), m_out, preferred_element_type=F32)
    y_ref[0, :rc, :] = y_c
    y_ref[0, rc:, :] = y_l


def s5_mixer(lay, u_g, mats, h0):
    m_in, m_toep, m_out, coef_a, coef_b = mats
    rows = lay.t // S5_CHUNK
    rc = lay.t_ctx // S5_CHUNK
    rl = rows - rc
    st2 = 2 * SSM_STATE
    g4 = lambda g: (g, 0, 0, 0)
    return pl.pallas_call(
        functools.partial(_s5_kernel, rc=rc, seg_c=lay.s_ctx // S5_CHUNK, n_lat=lay.n_lat,
                          seg_l=lay.s_lat // S5_CHUNK),
        grid=(SSM_GROUPS,),
        in_specs=[pl.BlockSpec((1, rows, S5_LANES), lambda g: (g, 0, 0)),
                  pl.BlockSpec((1, 2, S5_LANES, st2), g4),
                  pl.BlockSpec((1, 2, S5_LANES, S5_LANES), g4),
                  pl.BlockSpec((1, 2, st2, S5_LANES), g4),
                  pl.BlockSpec((1, 2, S5_SCAN_STEPS, st2), g4),
                  pl.BlockSpec((1, 2, S5_SCAN_STEPS, st2), g4),
                  pl.BlockSpec((1, 2, lay.n_lat, st2), g4)],
        out_specs=[pl.BlockSpec((1, rows, S5_LANES), lambda g: (g, 0, 0)),
                   pl.BlockSpec((1, 2, rc, st2), g4)],
        out_shape=[jax.ShapeDtypeStruct((SSM_GROUPS, rows, S5_LANES), F32),
                   jax.ShapeDtypeStruct((SSM_GROUPS, 2, rc, st2), F32)],
        scratch_shapes=[pltpu.VMEM((rl, st2), F32), pltpu.VMEM((rl, st2), F32)],
        compiler_params=_cparams(("arbitrary",)),
        name="s5_mixer",
    )(u_g, m_in, m_toep, m_out, coef_a, coef_b, h0)


def _even_out_kernel(x_ref, ya_ref, yg_ref, u_ref, d_ref, gw_ref, gb_ref, w_ref, mod_ref, o_ref, yt_s):
    ys = _group_major_load(yg_ref, yt_s) + d_ref[...] * u_ref[...]
    g = jax.nn.gelu(ys)
    yb = g * jax.nn.sigmoid(jnp.dot(g.astype(BF16), gw_ref[...], preferred_element_type=F32)
                            + gb_ref[...])
    out = jnp.dot(ya_ref[...].astype(BF16), w_ref[:D_LRU, :], preferred_element_type=F32)
    out = out + jnp.dot(yb.astype(BF16), w_ref[D_LRU:, :], preferred_element_type=F32)
    o_ref[...] = x_ref[...] + mod_ref[0, 2:3, :] * out


def even_out(lay, x, y_a, y_g, proj, ssm_d, glu_w_bf16, glu_b, w_out_bf16, mods):
    mi = _mod_index(lay, ROW_TILE)
    c = D_SSM
    row = lambda i: (i, 0)
    const = lambda i: (0, 0)
    return pl.pallas_call(
        _even_out_kernel,
        grid=(lay.t // ROW_TILE,),
        in_specs=[pl.BlockSpec((ROW_TILE, D_MODEL), row),
                  pl.BlockSpec((ROW_TILE, c), row),
                  pl.BlockSpec((SSM_GROUPS, ROW_TILE // S5_CHUNK, S5_LANES), lambda i: (0, i, 0)),
                  pl.BlockSpec((ROW_TILE, c), lambda i: (i, 2)),
                  pl.BlockSpec((1, c), const),
                  pl.BlockSpec((c, c), const),
                  pl.BlockSpec((1, c), const),
                  pl.BlockSpec((D_MODEL, D_MODEL), const),
                  pl.BlockSpec((1, N_MOD, D_MODEL), lambda i: (mi(i), 0, 0))],
        out_specs=pl.BlockSpec((ROW_TILE, D_MODEL), row),
        out_shape=jax.ShapeDtypeStruct((lay.t, D_MODEL), F32),
        scratch_shapes=[pltpu.VMEM((SSM_COL_BLOCKS, ROW_TILE, LANES), F32)],
        compiler_params=_cparams(("arbitrary",)),
        name="even_out",
    )(x, y_a, y_g, proj, ssm_d.reshape(1, -1), glu_w_bf16, glu_b.reshape(1, -1), w_out_bf16, mods)


def _softmax_pv(parts, sink_col):
    m = sink_col
    for s, _ in parts:
        m = jnp.maximum(m, jnp.max(s, axis=-1, keepdims=True))
    den = jnp.exp(sink_col - m)
    acc = None
    for s, v in parts:
        p = jnp.exp(s - m)
        den = den + jnp.sum(p, axis=-1, keepdims=True)
        pv = jnp.dot(p.astype(BF16), v.astype(BF16), preferred_element_type=F32)
        acc = pv if acc is None else acc + pv
    return acc / den


def _nt_dot(a, b):
    return lax.dot_general(a.astype(BF16), b.astype(BF16), (((1,), (1,)), ((), ())),
                           preferred_element_type=F32)


def _attn_ctx_kernel(q_ref, k_ref, v_ref, sink_ref, o_ref):
    n = q_ref.shape[0]
    for kh in range(N_KV):
        k = k_ref[:, kh * HEAD_DIM:(kh + 1) * HEAD_DIM]
        v = v_ref[:, kh * HEAD_DIM:(kh + 1) * HEAD_DIM]
        for g in range(GQA):
            h = kh * GQA + g
            q = q_ref[:, h * HEAD_DIM:(h + 1) * HEAD_DIM]
            s = _nt_dot(q, k) * ATTN_SCALE
            sink = jnp.broadcast_to(sink_ref[0:1, h:h + 1], (n, 1))
            o_ref[:, h * HEAD_DIM:(h + 1) * HEAD_DIM] = _softmax_pv([(s, v)], sink)


def attn_context(lay, qkv, sink):
    nq = N_HEADS * HEAD_DIM
    nkv = N_KV * HEAD_DIM
    return pl.pallas_call(
        _attn_ctx_kernel,
        grid=(lay.n_ctx,),
        in_specs=[pl.BlockSpec((lay.s_ctx, nq), lambda b: (b, 0)),
                  pl.BlockSpec((lay.s_ctx, nkv), lambda b: (b, nq // nkv)),
                  pl.BlockSpec((lay.s_ctx, nkv), lambda b: (b, nq // nkv + 1)),
                  pl.BlockSpec((1, N_HEADS), lambda b: (0, 0))],
        out_specs=pl.BlockSpec((lay.s_ctx, nq), lambda b: (b, 0)),
        out_shape=jax.ShapeDtypeStruct((lay.t_ctx, nq), F32),
        compiler_params=_cparams(("arbitrary",)),
        name="attn_context",
    )(qkv, qkv, qkv, sink.reshape(1, -1))


def _rope(x, cos, sin):
    lane = lax.broadcasted_iota(I32, (x.shape[0], 2 * HEAD_DIM), 1) % HEAD_DIM
    outs = []
    for j in range(x.shape[1] // (2 * HEAD_DIM)):
        xs = x[:, j * 2 * HEAD_DIM:(j + 1) * 2 * HEAD_DIM]
        sw = jnp.where(lane < HEAD_DIM // 2,
                       pltpu.roll(xs, 2 * HEAD_DIM - HEAD_DIM // 2, 1),
                       pltpu.roll(xs, HEAD_DIM // 2, 1))
        outs.append(xs * cos + sw * sin)
    return outs


def _attn_lat_kernel(q_ref, k0_ref, k1_ref, k2_ref, v0_ref, v1_ref, v2_ref, ck_ref, cv_ref,
                     cq_ref, sq_ref, c0_ref, c1_ref, c2_ref, s0_ref, s1_ref, s2_ref, sink_ref,
                     o_ref, *, n_blk):
    j = pl.program_id(1)
    qb = Q_BLOCK
    q_parts = [qp * ATTN_SCALE for qp in _rope(q_ref[...], cq_ref[...], sq_ref[...])]
    k_parts = [_rope(kr[...], cr[...], sr[...])
               for kr, cr, sr in ((k0_ref, c0_ref, s0_ref), (k1_ref, c1_ref, s1_ref),
                                  (k2_ref, c2_ref, s2_ref))]
    qi = lax.broadcasted_iota(I32, (qb, 3 * qb), 0)
    km = lax.broadcasted_iota(I32, (qb, 3 * qb), 1)
    kpos = j * qb - qb + km
    mask1 = (jnp.abs(km - qb - qi) <= WINDOW) & (kpos >= 0) & (kpos < n_blk * qb)
    mask = jnp.concatenate([mask1] * GQA, axis=0)
    for kh in range(N_KV):
        half = (kh % 2) * HEAD_DIM
        k_loc = jnp.concatenate([kp[kh // 2][:, half:half + HEAD_DIM] for kp in k_parts], axis=0)
        v_loc = jnp.concatenate([vr[:, kh * HEAD_DIM:(kh + 1) * HEAD_DIM]
                                 for vr in (v0_ref, v1_ref, v2_ref)], axis=0)
        qs, sinks = [], []
        for g in range(GQA):
            h = kh * GQA + g
            qs.append(q_parts[h // 2][:, (h % 2) * HEAD_DIM:(h % 2 + 1) * HEAD_DIM])
            sinks.append(jnp.broadcast_to(sink_ref[0:1, h:h + 1], (qb, 1)))
        q = jnp.concatenate(qs, axis=0)
        sink = jnp.concatenate(sinks, axis=0)
        s_loc = jnp.where(mask, _nt_dot(q, k_loc), NEG_INF)
        s_ctx = _nt_dot(q, ck_ref[0, kh])
        o = _softmax_pv([(s_loc, v_loc), (s_ctx, cv_ref[0, kh])], sink)
        for g in range(GQA):
            h = kh * GQA + g
            o_ref[:, h * HEAD_DIM:(h + 1) * HEAD_DIM] = o[g * qb:(g + 1) * qb]


def _rope_tables(s_len):
    rows = s_len // GRID_W
    row = jnp.repeat(jnp.arange(rows), GRID_W).astype(F32)
    col = jnp.tile(jnp.arange(GRID_W), rows).astype(F32)
    nf = HEAD_DIM // 4
    inv = ROPE_BASE ** (-jnp.arange(nf, dtype=F32) / nf)
    ang = jnp.concatenate([row[:, None] * inv, col[:, None] * inv], axis=-1)
    cos, sin = jnp.cos(ang), jnp.sin(ang)
    cos2 = jnp.tile(jnp.concatenate([cos, cos], axis=-1), (1, 2))
    sin2 = jnp.tile(jnp.concatenate([-sin, sin], axis=-1), (1, 2))
    return cos2, sin2


def attn_latent(lay, qkv, cache_k, cache_v, sink):
    nq = N_HEADS * HEAD_DIM
    nkv = N_KV * HEAD_DIM
    n_blk = lay.s_lat // Q_BLOCK
    base = lay.t_ctx // Q_BLOCK
    n_ctx_keys = cache_k.shape[2]
    cos2, sin2 = _rope_tables(lay.s_lat)
    kcol = nq // nkv

    def qrow(b, j):
        return base + b * n_blk + j

    def krow(off):
        return lambda b, j: base + b * n_blk + jnp.clip(j + off, 0, n_blk - 1)

    def trow(off):
        return lambda b, j: (jnp.clip(j + off, 0, n_blk - 1), 0)

    kv_spec = lambda off, col: pl.BlockSpec((Q_BLOCK, nkv), lambda b, j: (krow(off)(b, j), col))
    tab = lambda off: pl.BlockSpec((Q_BLOCK, 2 * HEAD_DIM), trow(off))
    cache_spec = pl.BlockSpec((1, N_KV, n_ctx_keys, HEAD_DIM), lambda b, j: (b, 0, 0, 0))
    return pl.pallas_call(
        functools.partial(_attn_lat_kernel, n_blk=n_blk),
        grid=(lay.n_lat, n_blk),
        in_specs=[pl.BlockSpec((Q_BLOCK, nq), lambda b, j: (qrow(b, j), 0)),
                  kv_spec(-1, kcol), kv_spec(0, kcol), kv_spec(1, kcol),
                  kv_spec(-1, kcol + 1), kv_spec(0, kcol + 1), kv_spec(1, kcol + 1),
                  cache_spec, cache_spec,
                  tab(0), tab(0), tab(-1), tab(0), tab(1), tab(-1), tab(0), tab(1),
                  pl.BlockSpec((1, N_HEADS), lambda b, j: (0, 0))],
        out_specs=pl.BlockSpec((Q_BLOCK, nq), lambda b, j: (b * n_blk + j, 0)),
        out_shape=jax.ShapeDtypeStruct((lay.t_lat, nq), F32),
        compiler_params=_cparams(("arbitrary", "arbitrary")),
        name="attn_latent",
    )(qkv, qkv, qkv, qkv, qkv, qkv, qkv, cache_k, cache_v,
      cos2, sin2, cos2, cos2, cos2, sin2, sin2, sin2, sink.reshape(1, -1))


def _mm_res_kernel(x_ref, ac_ref, al_ref, w_ref, mod_ref, o_ref, *, n_ctx_tiles):
    a = jnp.where(pl.program_id(0) < n_ctx_tiles, ac_ref[...], al_ref[...])
    out = jnp.dot(a.astype(BF16), w_ref[...], preferred_element_type=F32)
    o_ref[...] = x_ref[...] + mod_ref[0, 2:3, :] * out


def matmul_residual(lay, x, a_ctx, a_lat, w_bf16, mods):
    mi = _mod_index(lay, ROW_TILE)
    k = a_ctx.shape[1]
    nct = lay.t_ctx // ROW_TILE
    return pl.pallas_call(
        functools.partial(_mm_res_kernel, n_ctx_tiles=nct),
        grid=(lay.t // ROW_TILE,),
        in_specs=[pl.BlockSpec((ROW_TILE, D_MODEL), lambda i: (i, 0)),
                  pl.BlockSpec((ROW_TILE, k), lambda i: (jnp.minimum(i, nct - 1), 0)),
                  pl.BlockSpec((ROW_TILE, k), lambda i: (jnp.maximum(i - nct, 0), 0)),
                  pl.BlockSpec((k, D_MODEL), lambda i: (0, 0)),
                  pl.BlockSpec((1, N_MOD, D_MODEL), lambda i: (mi(i), 0, 0))],
        out_specs=pl.BlockSpec((ROW_TILE, D_MODEL), lambda i: (i, 0)),
        out_shape=jax.ShapeDtypeStruct((lay.t, D_MODEL), F32),
        compiler_params=_cparams(("arbitrary",)),
        name="matmul_residual",
    )(x, a_ctx, a_lat, w_bf16, mods)


def _rowtile_load(ref, n, base=0):
    parts = []
    for c in range(ROW_CHUNKS):
        words = ref[pl.ds(base + c, n, stride=ROW_CHUNKS), :]
        for half in range(2):
            parts.append(pltpu.unpack_elementwise(words, index=half, packed_dtype=BF16,
                                                  unpacked_dtype=F32))
    return jnp.concatenate(parts, axis=1)


def _rowtile_store(ref, val, n):
    for c in range(ROW_CHUNKS):
        lo = val[:, 2 * c * LANES:(2 * c + 1) * LANES]
        hi = val[:, (2 * c + 1) * LANES:(2 * c + 2) * LANES]
        ref[pl.ds(c, n, stride=ROW_CHUNKS), :] = pltpu.pack_elementwise([lo, hi], packed_dtype=BF16)


def _row_copy(src, src_row, dst, dst_row, sem):
    return pltpu.make_async_copy(
        src.at[pl.ds(pl.multiple_of(src_row * ROW_CHUNKS, ROW_CHUNKS), ROW_CHUNKS), :],
        dst.at[pl.ds(pl.multiple_of(dst_row * ROW_CHUNKS, ROW_CHUNKS), ROW_CHUNKS), :], sem)


def _router_kernel(x_ref, g_ref, mod_ref, rwt_ref, rb_ref, tri_ref, sg_ref, su_ref, sd_ref,
                   hn_ref, sh_ref, eidx_ref, wts_ref, rank_ref, cnt_ref, cnt_s):
    tm = ROUTER_TILE

    @pl.when(pl.program_id(0) == 0)
    def _():
        cnt_s[...] = jnp.zeros_like(cnt_s)

    h = _modnorm(x_ref[...], g_ref[...], mod_ref, 3)
    _rowtile_store(hn_ref, h, tm)
    hb = h.astype(BF16)
    sgate = jnp.dot(hb, sg_ref[...], preferred_element_type=F32)
    sup = jnp.dot(hb, su_ref[...], preferred_element_type=F32)
    sh_ref[...] = jnp.dot((sgate * jax.nn.sigmoid(sgate) * sup).astype(BF16), sd_ref[...],
                          preferred_element_type=F32)
    logits = lax.dot_general(rwt_ref[...], h, (((1,), (1,)), ((), ())),
                             precision=HIGHEST, preferred_element_type=F32)
    scores = jax.nn.sigmoid(logits)
    choice = scores + rb_ref[...]
    gs_rows = []
    for g in range(N_GROUPS):
        cg = choice[g * GROUP_SIZE:(g + 1) * GROUP_SIZE, :]
        m1 = jnp.max(cg, axis=0, keepdims=True)
        eq = cg == m1
        cnt = jnp.sum(eq.astype(F32), axis=0, keepdims=True)
        m2 = jnp.max(jnp.where(eq, -jnp.inf, cg), axis=0, keepdims=True)
        gs_rows.append(m1 + jnp.where(cnt >= 2.0, m1, m2))
    gs = jnp.concatenate(gs_rows, axis=0)
    gi = lax.broadcasted_iota(I32, (N_GROUPS, tm), 0)
    grank = jnp.zeros((N_GROUPS, tm), I32)
    for g in range(N_GROUPS):
        other = gs[g:g + 1, :]
        ahead = (other > gs) | ((other == gs) & (g < gi))
        grank = grank + ahead.astype(I32)
    gsel = grank < TOPK_GROUPS
    emask = jnp.concatenate(
        [jnp.broadcast_to(gsel[g:g + 1, :], (GROUP_SIZE, tm)) for g in range(N_GROUPS)], axis=0)
    masked = jnp.where(emask, choice, -jnp.inf)
    ei = lax.broadcasted_iota(I32, (N_EXPERTS, tm), 0)
    idxs, ws = [], []
    member = jnp.zeros((N_EXPERTS, tm), F32)
    for _ in range(TOP_K):
        m = jnp.max(masked, axis=0, keepdims=True)
        idx = jnp.min(jnp.where(masked == m, ei, N_EXPERTS), axis=0, keepdims=True)
        hit = ei == idx
        ws.append(jnp.sum(jnp.where(hit, scores, 0.0), axis=0, keepdims=True))
        idxs.append(idx)
        member = jnp.where(hit, 1.0, member)
        masked = jnp.where(hit, -jnp.inf, masked)
    w = jnp.concatenate(ws, axis=0)
    wts_ref[...] = w / jnp.sum(w, axis=0, keepdims=True) * ROUTE_SCALE
    eidx_ref[...] = jnp.concatenate(idxs, axis=0)
    before = jnp.dot(member.astype(BF16), tri_ref[...], preferred_element_type=F32) + cnt_s[...]
    ranks = [jnp.sum(jnp.where(ei == idx, before, 0.0), axis=0, keepdims=True) for idx in idxs]
    rank_ref[...] = jnp.concatenate(ranks, axis=0).astype(I32)
    cnt_s[...] = cnt_s[...] + jnp.sum(member, axis=1, keepdims=True)
    cnt_ref[...] = jnp.broadcast_to(cnt_s[...], cnt_ref.shape)


def moe_router(lay, x, g, mods, router_w, router_b, sg_bf16, su_bf16, sd_bf16):
    t = lay.t
    tm = ROUTER_TILE
    mi = _mod_index(lay, tm)
    tri = (jnp.arange(tm)[:, None] < jnp.arange(tm)[None, :]).astype(BF16)
    tok = lambda i: (0, i)
    const = lambda i: (0, 0)
    return pl.pallas_call(
        _router_kernel,
        grid=(t // tm,),
        in_specs=[pl.BlockSpec((tm, D_MODEL), lambda i: (i, 0)),
                  pl.BlockSpec((1, D_MODEL), const),
                  pl.BlockSpec((1, N_MOD, D_MODEL), lambda i: (mi(i), 0, 0)),
                  pl.BlockSpec((N_EXPERTS, D_MODEL), const),
                  pl.BlockSpec((N_EXPERTS, 1), const),
                  pl.BlockSpec((tm, tm), const),
                  pl.BlockSpec((D_MODEL, D_EXPERT), const),
                  pl.BlockSpec((D_MODEL, D_EXPERT), const),
                  pl.BlockSpec((D_EXPERT, D_MODEL), const)],
        out_specs=[pl.BlockSpec((tm * ROW_CHUNKS, LANES), lambda i: (i, 0)),
                   pl.BlockSpec((tm, D_MODEL), lambda i: (i, 0)),
                   pl.BlockSpec((TOP_K, tm), tok),
                   pl.BlockSpec((TOP_K, tm), tok),
                   pl.BlockSpec((TOP_K, tm), tok),
                   pl.BlockSpec((N_EXPERTS, LANES), const)],
        out_shape=[jax.ShapeDtypeStruct((t * ROW_CHUNKS, LANES), PACKED),
                   jax.ShapeDtypeStruct((t, D_MODEL), F32),
                   jax.ShapeDtypeStruct((TOP_K, t), I32),
                   jax.ShapeDtypeStruct((TOP_K, t), F32),
                   jax.ShapeDtypeStruct((TOP_K, t), I32),
                   jax.ShapeDtypeStruct((N_EXPERTS, LANES), F32)],
        scratch_shapes=[pltpu.VMEM((N_EXPERTS, 1), F32)],
        compiler_params=_cparams(("arbitrary",)),
        name="moe_router",
    )(x, g.reshape(1, -1), mods, router_w.T, router_b.reshape(-1, 1), tri, sg_bf16, su_bf16, sd_bf16)


def _dest_kernel(start_ref, eidx_ref, rank_ref, dest_ref):
    e = eidx_ref[...]

    def body(i, acc):
        return jnp.where(e == i, start_ref[i], acc)

    dest_ref[...] = lax.fori_loop(0, N_EXPERTS, body, jnp.zeros_like(e), unroll=8) + rank_ref[...]


def moe_dest(pad_start, eidx, rank):
    t = eidx.shape[1]
    tn = DEST_TILE
    spec = pl.BlockSpec((TOP_K, tn), lambda i, ps: (0, i))
    return pl.pallas_call(
        _dest_kernel,
        grid_spec=pltpu.PrefetchScalarGridSpec(
            num_scalar_prefetch=1, grid=(t // tn,), in_specs=[spec, spec], out_specs=spec),
        out_shape=jax.ShapeDtypeStruct((TOP_K, t), I32),
        compiler_params=_cparams(("arbitrary",)),
        name="moe_dest",
    )(pad_start, eidx, rank)


def _issue_row_copies(idx_at, n, copy_at, unroll=4):
    def body(i, c):
        for p in range(2):
            r = 2 * i + p
            copy_at(r, idx_at(r)).start(priority=p)
        return c
    lax.fori_loop(0, n // 2, body, 0, unroll=unroll)


def _dispatch_kernel(zrow_ref, dest_hbm, hn_ref, xs_hbm, idx_s, zbuf, isem, zsem, ssem, *, n_tiles):
    i = pl.program_id(0)
    slot = i % 2
    td = DISPATCH_TILE

    n_idx = TOP_K * td

    def idx_copy(tile, s):
        return pltpu.make_async_copy(dest_hbm.at[tile], idx_s.at[pl.ds(s * n_idx, n_idx)], isem.at[s])

    def zero_copy(e):
        r0 = pl.multiple_of(zrow_ref[e] * ROW_CHUNKS, ROW_CHUNKS)
        return pltpu.make_async_copy(zbuf, xs_hbm.at[pl.ds(r0, MOE_BLOCK * ROW_CHUNKS), :], zsem)

    @pl.when(i == 0)
    def _():
        zbuf[...] = jnp.zeros_like(zbuf)

        def zstart(e, c):
            @pl.when(zrow_ref[e] >= 0)
            def _():
                zero_copy(e).start()
            return c

        def zwait(e, c):
            @pl.when(zrow_ref[e] >= 0)
            def _():
                zero_copy(e).wait()
            return c

        lax.fori_loop(0, zrow_ref.shape[0], zstart, 0)
        idx_copy(0, 0).start()
        lax.fori_loop(0, zrow_ref.shape[0], zwait, 0)

    idx_copy(i, slot).wait()

    @pl.when(i + 1 < n_tiles)
    def _():
        idx_copy(i + 1, 1 - slot).start()

    for k in range(TOP_K):
        _issue_row_copies(lambda r: idx_s[slot * n_idx + k * td + r], td,
                          lambda r, d: _row_copy(hn_ref, r, xs_hbm, d, ssem))
    for k in range(TOP_K):
        pltpu.make_async_copy(hn_ref, xs_hbm.at[pl.ds(0, td * ROW_CHUNKS), :], ssem).wait()


def _tile_major(dest, tile):
    t = dest.shape[1]
    return dest.reshape(TOP_K, t // tile, tile).transpose(1, 0, 2).reshape(t // tile, TOP_K * tile)


def moe_dispatch(lay, hn, dest, zero_row, n_rows):
    td = DISPATCH_TILE
    n_tiles = lay.t // td
    return pl.pallas_call(
        functools.partial(_dispatch_kernel, n_tiles=n_tiles),
        grid_spec=pltpu.PrefetchScalarGridSpec(
            num_scalar_prefetch=1,
            grid=(n_tiles,),
            in_specs=[pl.BlockSpec(memory_space=pl.ANY),
                      pl.BlockSpec((td * ROW_CHUNKS, LANES), lambda i, z: (i, 0))],
            out_specs=pl.BlockSpec(memory_space=pl.ANY),
            scratch_shapes=[pltpu.SMEM((2 * TOP_K * td,), I32),
                            pltpu.VMEM((MOE_BLOCK * ROW_CHUNKS, LANES), PACKED),
                            pltpu.SemaphoreType.DMA((2,)),
                            pltpu.SemaphoreType.DMA,
                            pltpu.SemaphoreType.DMA]),
        out_shape=jax.ShapeDtypeStruct((n_rows * ROW_CHUNKS, LANES), PACKED),
        compiler_params=_cparams(("arbitrary",)),
        name="moe_dispatch",
    )(zero_row, dest, hn)


def _expert_kernel(blk0_ref, nblk_ref, tail_ref, xs_hbm, wg_ref, wu_ref, wd_ref, y_hbm,
                   xbuf, ybuf, wg_s, wu_s, wd_s, isem, osem):
    e = pl.program_id(0)
    nb = nblk_ref[e]
    g0 = blk0_ref[e]
    total = blk0_ref[N_EXPERTS - 1] + nblk_ref[N_EXPERTS - 1]
    blk_rows = MOE_BLOCK * ROW_CHUNKS
    n_x = xbuf.shape[0]
    n_y = ybuf.shape[0]

    def block_rows(g):
        return pl.ds(pl.multiple_of(g * blk_rows, blk_rows), blk_rows)

    def fetch(g):
        s = g % n_x
        return pltpu.make_async_copy(xs_hbm.at[block_rows(g), :], xbuf.at[s], isem.at[s])

    def writeback(g):
        s = g % n_y
        return pltpu.make_async_copy(ybuf.at[s], y_hbm.at[block_rows(g), :], osem.at[s])

    ahead = n_x // 2

    @pl.when(e == 0)
    def _():
        for p in range(ahead):
            @pl.when(p < total)
            def _():
                fetch(p).start()

    @pl.when(nb > 0)
    def _():
        wg_s[...] = wg_ref[0, 0].astype(BF16)
        wu_s[...] = wu_ref[0, 0].astype(BF16)
        wd_s[...] = wd_ref[0, 0].astype(BF16)

    def run_blocks(g, n):
        for p in range(n):
            @pl.when(g + ahead + p < total)
            def _():
                fetch(g + ahead + p).start()
        for p in range(n):
            fetch(g + p).wait()
        x = jnp.concatenate([_rowtile_load(xbuf.at[(g + p) % n_x], MOE_BLOCK) for p in range(n)],
                            axis=0).astype(BF16)
        gate = jnp.dot(x, wg_s[...], preferred_element_type=F32)
        up = jnp.dot(x, wu_s[...], preferred_element_type=F32)
        act = gate * jax.nn.sigmoid(gate) * up
        y = jnp.dot(act.astype(BF16), wd_s[...], preferred_element_type=F32)
        for p in range(n):
            @pl.when(g + p >= n_y)
            def _():
                writeback(g + p - n_y).wait()
        for p in range(n):
            _rowtile_store(ybuf.at[(g + p) % n_y], y[p * MOE_BLOCK:(p + 1) * MOE_BLOCK], MOE_BLOCK)
            writeback(g + p).start()

    def quad(jj, c):
        run_blocks(g0 + 4 * jj, 4)
        return c

    lax.fori_loop(0, nb // 4, quad, 0)
    rem = nb % 4

    @pl.when(rem >= 2)
    def _():
        run_blocks(g0 + nb - rem, 2)

    @pl.when(rem % 2 == 1)
    def _():
        run_blocks(g0 + nb - 1, 1)

    @pl.when(e == N_EXPERTS - 1)
    def _():
        for p in range(n_y, 0, -1):
            @pl.when(total >= p)
            def _():
                writeback(total - p).wait()

        ybuf[0] = jnp.zeros(ybuf.shape[1:], PACKED)

        def tail_copy(i):
            r0 = pl.multiple_of(tail_ref[i] * ROW_CHUNKS, blk_rows)
            return pltpu.make_async_copy(ybuf.at[0], y_hbm.at[pl.ds(r0, blk_rows), :], osem.at[0])

        def tstart(i, c):
            @pl.when(tail_ref[i] >= 0)
            def _():
                tail_copy(i).start()
            return c

        def twait(i, c):
            @pl.when(tail_ref[i] >= 0)
            def _():
                tail_copy(i).wait()
            return c

        lax.fori_loop(0, tail_ref.shape[0], tstart, 0)
        lax.fori_loop(0, tail_ref.shape[0], twait, 0)


def moe_experts(xs, first_block, n_blocks, tail_row, layer, w_gate, w_up, w_down):
    wspec = lambda shape: pl.BlockSpec((1, 1) + shape, lambda e, a, b, c: (layer, e, 0, 0))
    blk = (MOE_BLOCK * ROW_CHUNKS, LANES)
    return pl.pallas_call(
        _expert_kernel,
        grid_spec=pltpu.PrefetchScalarGridSpec(
            num_scalar_prefetch=3,
            grid=(N_EXPERTS,),
            in_specs=[pl.BlockSpec(memory_space=pl.ANY),
                      wspec((D_MODEL, D_EXPERT)), wspec((D_MODEL, D_EXPERT)),
                      wspec((D_EXPERT, D_MODEL))],
            out_specs=pl.BlockSpec(memory_space=pl.ANY),
            scratch_shapes=[pltpu.VMEM((EXPERT_X_BUFS,) + blk, PACKED),
                            pltpu.VMEM((EXPERT_Y_BUFS,) + blk, PACKED),
                            pltpu.VMEM((D_MODEL, D_EXPERT), BF16),
                            pltpu.VMEM((D_MODEL, D_EXPERT), BF16),
                            pltpu.VMEM((D_EXPERT, D_MODEL), BF16),
                            pltpu.SemaphoreType.DMA((EXPERT_X_BUFS,)),
                            pltpu.SemaphoreType.DMA((EXPERT_Y_BUFS,))]),
        out_shape=jax.ShapeDtypeStruct(xs.shape, PACKED),
        compiler_params=_cparams(("arbitrary",)),
        name="moe_experts",
    )(first_block, n_blocks, tail_row, xs, w_gate, w_up, w_down)


def _combine_kernel(dest_hbm, y_hbm, x_ref, sh_ref, w_ref, mod_ref, *rest, n_tiles, final):
    if final:
        gf_ref, oc_ref, ol_ref, idx_s, ybuf, isem, gsem = rest
    else:
        o_ref, idx_s, ybuf, isem, gsem = rest
    i = pl.program_id(0)
    slot = i % 2
    tm = COMBINE_TILE
    n_idx = TOP_K * tm
    nxt = jnp.minimum(i + 1, n_tiles - 1)
    nxt2 = jnp.minimum(i + 2, n_tiles - 1)

    def idx_copy(tile, s):
        return pltpu.make_async_copy(dest_hbm.at[tile], idx_s.at[pl.ds(s * n_idx, n_idx)], isem.at[s])

    def gather(s, unroll=4):
        _issue_row_copies(lambda r: idx_s[s * n_idx + r], n_idx,
                          lambda r, d: _row_copy(y_hbm, d, ybuf.at[s], r, gsem.at[s]), unroll)

    def gather_wait(s):
        pltpu.make_async_copy(y_hbm.at[pl.ds(0, n_idx * ROW_CHUNKS), :], ybuf.at[s], gsem.at[s]).wait()

    @pl.when(i == 0)
    def _():
        c = idx_copy(0, 0)
        c.start()
        c.wait()
        gather(0)
        idx_copy(nxt, 1).start()

    idx_copy(nxt, 1 - slot).wait()
    gather_wait(slot)
    gather(1 - slot, unroll=True)
    idx_copy(nxt2, slot).start()

    w = w_ref[...]
    routed = jnp.zeros((tm, D_MODEL), F32)
    for k in range(TOP_K):
        routed = routed + w[:, k:k + 1] * _rowtile_load(ybuf.at[slot], tm, base=k * tm * ROW_CHUNKS)
    out = x_ref[...] + mod_ref[0, 5:6, :] * (routed + sh_ref[...])
    if final:
        ms = jnp.mean(out * out, axis=-1, keepdims=True)
        y = out * lax.rsqrt(ms + EPS) * gf_ref[...]
        oc_ref[...] = y
        ol_ref[...] = y
    else:
        o_ref[...] = out

    @pl.when(i == n_tiles - 1)
    def _():
        gather_wait(1 - slot)
        idx_copy(nxt2, slot).wait()


def moe_combine(lay, x, shared, y_rows, dest, wts_t, mods, g_final=None):
    tm = COMBINE_TILE
    n_tiles = lay.t // tm
    nct = lay.t_ctx // tm
    mi = _mod_index(lay, tm)
    row = lambda i: (i, 0)
    final = g_final is not None
    if final:
        extra_in = [pl.BlockSpec((1, D_MODEL), lambda i: (0, 0))]
        extra_args = [g_final.reshape(1, -1)]
        out_specs = [pl.BlockSpec((tm, D_MODEL), lambda i: (jnp.minimum(i, nct), 0)),
                     pl.BlockSpec((tm, D_MODEL), lambda i: (jnp.maximum(i - nct, 0), 0))]
        out_shape = [jax.ShapeDtypeStruct((lay.t_ctx + tm, D_MODEL), F32),
                     jax.ShapeDtypeStruct((lay.t_lat, D_MODEL), F32)]
    else:
        extra_in, extra_args = [], []
        out_specs = pl.BlockSpec((tm, D_MODEL), row)
        out_shape = jax.ShapeDtypeStruct((lay.t, D_MODEL), F32)
    return pl.pallas_call(
        functools.partial(_combine_kernel, n_tiles=n_tiles, final=final),
        grid=(n_tiles,),
        in_specs=[pl.BlockSpec(memory_space=pl.ANY),
                  pl.BlockSpec(memory_space=pl.ANY),
                  pl.BlockSpec((tm, D_MODEL), row),
                  pl.BlockSpec((tm, D_MODEL), row),
                  pl.BlockSpec((tm, TOP_K), row),
                  pl.BlockSpec((1, N_MOD, D_MODEL), lambda i: (mi(i), 0, 0))] + extra_in,
        out_specs=out_specs,
        out_shape=out_shape,
        scratch_shapes=[pltpu.SMEM((2 * TOP_K * tm,), I32),
                        pltpu.VMEM((2, TOP_K * tm * ROW_CHUNKS, LANES), PACKED),
                        pltpu.SemaphoreType.DMA((2,)),
                        pltpu.SemaphoreType.DMA((2,))],
        compiler_params=_cparams(("arbitrary",)),
        name="moe_combine",
    )(dest, y_rows, x, shared, wts_t, mods, *extra_args)


def moe_layer(lay, x, g, mods, router_w, router_b, layer, w_gate, w_up, w_down, s_gate, s_up, s_down,
              g_final=None):
    t = lay.t
    hn, shared, eidx, wts, rank, cnt = moe_router(
        lay, x, g, mods, router_w, router_b, s_gate.astype(BF16), s_up.astype(BF16), s_down.astype(BF16))
    counts = cnt[:, 0].astype(I32)
    n_blocks = (counts + MOE_BLOCK - 1) // MOE_BLOCK
    padded = n_blocks * MOE_BLOCK
    pad_end = jnp.cumsum(padded)
    pad_start = pad_end - padded
    n_rows = -(-(t * TOP_K + N_EXPERTS * (MOE_BLOCK - 1)) // MOE_BLOCK) * MOE_BLOCK
    dest = moe_dest(pad_start, eidx, rank)
    last_row = jnp.where(n_blocks > 0, pad_end - MOE_BLOCK, -1)
    tail_blk = pad_end[-1] // MOE_BLOCK + jnp.arange(n_rows // MOE_BLOCK - t * TOP_K // MOE_BLOCK)
    tail_row = jnp.where(tail_blk < n_rows // MOE_BLOCK, tail_blk * MOE_BLOCK, -1).astype(I32)
    xs = moe_dispatch(lay, hn, _tile_major(dest, DISPATCH_TILE),
                      jnp.concatenate([last_row, tail_row]), n_rows)
    y_rows = moe_experts(xs, pad_start // MOE_BLOCK, n_blocks, tail_row, layer, w_gate, w_up, w_down)
    return moe_combine(lay, x, shared, y_rows, _tile_major(dest, COMBINE_TILE), wts.T, mods, g_final)


def _block_diag(w):
    nb, bw, _ = w.shape
    eye = jnp.eye(nb, dtype=w.dtype)
    return (eye[:, None, :, None] * w[:, :, None, :]).reshape(nb * bw, nb * bw)


def even_layer(lay, x, mods, g_mix, p, state_lru, state_ssm_re, state_ssm_im):
    t = lay.t
    proj, u_g = modnorm_matmul(lay, x, g_mix, mods, 0, p['w_in'].astype(BF16),
                               ug_col=2 * D_LRU)
    zeros_c = jnp.zeros((lay.n_ctx, D_LRU), F32)
    hf_y, st = None, []
    for d in range(2):
        wg = jnp.concatenate([_block_diag(p['lru_wa'][d]), _block_diag(p['lru_wx'][d])], axis=1)
        bg = jnp.concatenate([p['lru_ba'][d], p['lru_bx'][d]])
        h0 = jnp.concatenate([zeros_c, state_lru[:, d].astype(F32)], axis=0)
        hf_y, s = lru_pass(lay, proj, p['conv_w'], p['conv_b'], wg.astype(BF16), bg,
                           p['lru_lam'][d], h0, reverse=(d == 1), hf=hf_y)
        st.append(s[:lay.n_ctx])
    y_a = hf_y
    new_lru = jnp.stack(st, axis=1)

    mats = _s5_matrices(p['a_re'], p['a_im'], p['log_dt'], p['b_re'], p['b_im'], p['c_re'], p['c_im'])
    h0 = jnp.concatenate([state_ssm_re, state_ssm_im], axis=-1).astype(F32)
    h0 = h0.transpose(2, 1, 0, 3)
    y_g, h_ctx = s5_mixer(lay, u_g, mats, h0)
    seg = lay.s_ctx // S5_CHUNK
    h_ctx = h_ctx.reshape(SSM_GROUPS, 2, lay.n_ctx, seg, 2 * SSM_STATE)
    ends = jnp.stack([h_ctx[:, 0, :, seg - 1], h_ctx[:, 1, :, 0]], axis=1)
    ends = ends.transpose(2, 1, 0, 3)
    x = even_out(lay, x, y_a, y_g, proj, p['d'], p['glu_w'].astype(BF16), p['glu_b'],
                 p['w_out'].astype(BF16), mods)
    return x, new_lru, ends[..., :SSM_STATE], ends[..., SSM_STATE:]


def odd_layer(lay, x, mods, g_mix, w_qkv, sink, w_out, cache_k, cache_v):
    qkv = modnorm_matmul(lay, x, g_mix, mods, 0, w_qkv.astype(BF16))
    o_ctx = attn_context(lay, qkv, sink)
    o_lat = attn_latent(lay, qkv, cache_k, cache_v, sink)
    nq = N_HEADS * HEAD_DIM
    kv = qkv[:lay.t_ctx, nq:].reshape(lay.n_ctx, lay.s_ctx, 2, N_KV, HEAD_DIM)
    k_new = kv[:, :, 0].swapaxes(1, 2)
    v_new = kv[:, :, 1].swapaxes(1, 2)
    x = matmul_residual(lay, x, o_ctx, o_lat, w_out.astype(BF16), mods)
    return x, k_new, v_new


def _forward(lay, x_prompt, x_sample, state_lru, state_ssm_re, state_ssm_im, cache_k, cache_v,
             c, c_ctx, g_mix, g_ffn, w_mod, b_mod,
             ev_w_in, lru_conv_w, lru_conv_b, lru_wa, lru_ba, lru_wx, lru_bx, lru_lam,
             ssm_a_re, ssm_a_im, ssm_log_dt, ssm_b_re, ssm_b_im, ssm_c_re, ssm_c_im, ssm_d,
             ssm_glu_w, ssm_glu_b, ev_w_out, at_w_qkv, at_sink, at_w_out,
             router_w, router_b, exp_w_gate, exp_w_up, exp_w_down, sh_w_gate, sh_w_up, sh_w_down,
             g_final):
    depth = g_mix.shape[0]
    x = jnp.concatenate([x_prompt.reshape(lay.t_ctx, D_MODEL), x_sample.reshape(lay.t_lat, D_MODEL)],
                        axis=0)
    n_c = 1 + lay.n_lat
    c_rows = jnp.concatenate([c_ctx[None, :], c, jnp.zeros((16 - n_c, D_MODEL), F32)], axis=0)
    new_lru, new_re, new_im, new_k, new_v = [], [], [], [], []
    for l in range(depth):
        i = l // 2
        mods = adaln_table(c_rows, l, w_mod, b_mod[l])
        if l % 2 == 0:
            p = dict(w_in=ev_w_in[i], conv_w=lru_conv_w[i], conv_b=lru_conv_b[i],
                     lru_wa=lru_wa[i], lru_ba=lru_ba[i], lru_wx=lru_wx[i], lru_bx=lru_bx[i],
                     lru_lam=lru_lam[i], a_re=ssm_a_re[i], a_im=ssm_a_im[i], log_dt=ssm_log_dt[i],
                     b_re=ssm_b_re[i], b_im=ssm_b_im[i], c_re=ssm_c_re[i], c_im=ssm_c_im[i],
                     d=ssm_d[i], glu_w=ssm_glu_w[i], glu_b=ssm_glu_b[i], w_out=ev_w_out[i])
            x, lru_i, re_i, im_i = even_layer(lay, x, mods, g_mix[l], p, state_lru[:, i],
                                              state_ssm_re[:, i], state_ssm_im[:, i])
            new_lru.append(lru_i)
            new_re.append(re_i)
            new_im.append(im_i)
        else:
            x, k_i, v_i = odd_layer(lay, x, mods, g_mix[l], at_w_qkv[i], at_sink[i], at_w_out[i],
                                    cache_k[:, i], cache_v[:, i])
            new_k.append(k_i)
            new_v.append(v_i)
        x = moe_layer(lay, x, g_ffn[l], mods, router_w[l], router_b[l], l, exp_w_gate, exp_w_up,
                      exp_w_down, sh_w_gate[l], sh_w_up[l], sh_w_down[l],
                      g_final=g_final if l == depth - 1 else None)
    y_ctx, y_lat = x
    y_prompt = y_ctx[:lay.t_ctx].reshape(x_prompt.shape)
    y_sample = y_lat.reshape(x_sample.shape)
    return (y_prompt, y_sample, jnp.stack(new_lru, axis=1), jnp.stack(new_re, axis=1),
            jnp.stack(new_im, axis=1), jnp.stack(new_k, axis=1), jnp.stack(new_v, axis=1))


def kernel(x_prompt, x_sample, state_lru, state_ssm_re, state_ssm_im, cache_k, cache_v, c, c_ctx, g_mix, g_ffn, w_mod, b_mod, ev_w_in, lru_conv_w, lru_conv_b, lru_wa, lru_ba, lru_wx, lru_bx, lru_lam, ssm_a_re, ssm_a_im, ssm_log_dt, ssm_b_re, ssm_b_im, ssm_c_re, ssm_c_im, ssm_d, ssm_glu_w, ssm_glu_b, ev_w_out, at_w_qkv, at_sink, at_w_out, router_w, router_b, exp_w_gate, exp_w_up, exp_w_down, sh_w_gate, sh_w_up, sh_w_down, g_final):
    lay = Layout(n_ctx=x_prompt.shape[0], s_ctx=x_prompt.shape[1],
                 n_lat=x_sample.shape[0], s_lat=x_sample.shape[1])
    return _forward(lay, x_prompt, x_sample, state_lru, state_ssm_re, state_ssm_im, cache_k, cache_v,
                    c, c_ctx, g_mix, g_ffn, w_mod, b_mod,
                    ev_w_in, lru_conv_w, lru_conv_b, lru_wa, lru_ba, lru_wx, lru_bx, lru_lam,
                    ssm_a_re, ssm_a_im, ssm_log_dt, ssm_b_re, ssm_b_im, ssm_c_re, ssm_c_im, ssm_d,
                    ssm_glu_w, ssm_glu_b, ev_w_out, at_w_qkv, at_sink, at_w_out,
                    router_w, router_b, exp_w_gate, exp_w_up, exp_w_down, sh_w_gate, sh_w_up,
                    sh_w_down, g_final)
```

```python
import functools
from typing import NamedTuple

import jax
import jax.numpy as jnp
from jax import lax
from jax.experimental import pallas as pl
from jax.experimental.pallas import tpu as pltpu

F32 = jnp.float32
BF16 = jnp.bfloat16
I32 = jnp.int32
HIGHEST = lax.Precision.HIGHEST

D_MODEL = 1024
EPS = 1e-6
N_MOD = 6
GRID_W = 64
D_LRU = 512
LRU_BLOCKS = 8
LRU_C = 8.0
CONV_W = 4
CONV_LEFT = 2
D_SSM = 512
SSM_GROUP = 16
SSM_GROUPS = 32
SSM_STATE = 64
S5_CHUNK = 16
S5_LANES = S5_CHUNK * SSM_GROUP
S5_SCAN_STEPS = 8
HEAD_DIM = 64
N_HEADS = 16
N_KV = 4
GQA = 4
WINDOW = 128
Q_BLOCK = 128
ROPE_BASE = 10000.0
ATTN_SCALE = HEAD_DIM ** -0.5
NEG_INF = -1e30
N_EXPERTS = 256
TOP_K = 8
N_GROUPS = 8
TOPK_GROUPS = 4
GROUP_SIZE = N_EXPERTS // N_GROUPS
D_EXPERT = 256
ROUTE_SCALE = 2.5
MOE_BLOCK = 128

SUBLANES = 8
LANES = 128
ROW_CHUNKS = D_MODEL // (2 * LANES)
PACKED = jnp.int32
SEQ_TILE = 256
ROW_TILE = 512
ROUTER_TILE = 512
DEST_TILE = 1024
DISPATCH_TILE = 512
COMBINE_TILE = 256
EXPERT_X_BUFS = 8
EXPERT_Y_BUFS = 8
VMEM_LIMIT = 56 * 1024 * 1024


class Layout(NamedTuple):
    n_ctx: int
    s_ctx: int
    n_lat: int
    s_lat: int

    @property
    def t_ctx(self):
        return self.n_ctx * self.s_ctx

    @property
    def t_lat(self):
        return self.n_lat * self.s_lat

    @property
    def t(self):
        return self.t_ctx + self.t_lat

    @property
    def n_seq(self):
        return self.n_ctx + self.n_lat


def _cparams(sem):
    return pltpu.CompilerParams(dimension_semantics=sem, vmem_limit_bytes=VMEM_LIMIT)


def _mod_index(lay, tile_rows):
    n_ctx_tiles = lay.t_ctx // tile_rows
    per_lat = lay.s_lat // tile_rows

    def f(i):
        return jnp.where(i < n_ctx_tiles, 0, 1 + (i - n_ctx_tiles) // per_lat)
    return f


def _adaln_kernel(c_ref, w_ref, b_ref, o_ref):
    c = c_ref[...]
    s = c * jax.nn.sigmoid(c)
    o_ref[...] = jnp.dot(s, w_ref[0], precision=HIGHEST, preferred_element_type=F32) + b_ref[...]


def adaln_table(c_rows, layer, w_mod, b_mod):
    n = c_rows.shape[0]
    tn = 1536
    out = pl.pallas_call(
        _adaln_kernel,
        grid=(N_MOD * D_MODEL // tn,),
        in_specs=[pl.BlockSpec((n, D_MODEL), lambda j: (0, 0)),
                  pl.BlockSpec((1, D_MODEL, tn), lambda j: (layer, 0, j)),
                  pl.BlockSpec((1, tn), lambda j: (0, j))],
        out_specs=pl.BlockSpec((n, tn), lambda j: (0, j)),
        out_shape=jax.ShapeDtypeStruct((n, N_MOD * D_MODEL), F32),
        compiler_params=_cparams(("arbitrary",)),
        name="adaln",
    )(c_rows, w_mod, b_mod.reshape(1, -1))
    return out.reshape(n, N_MOD, D_MODEL)


def _modnorm(x, g, mod_ref, slot):
    ms = jnp.mean(x * x, axis=-1, keepdims=True)
    y = x * lax.rsqrt(ms + EPS) * g
    shift = mod_ref[0, slot:slot + 1, :]
    scale = mod_ref[0, slot + 1:slot + 2, :]
    return y * (1.0 + scale) + shift


GROUPS_PER_VREG = LANES // SSM_GROUP
SSM_COL_BLOCKS = D_SSM // LANES


def _group_major_store(val, tmp_ref, dst_ref):
    rows = dst_ref.shape[1]
    for j in range(SSM_COL_BLOCKS):
        tmp_ref[j] = val[:, j * LANES:(j + 1) * LANES]
    for j in range(SSM_COL_BLOCKS):
        steps = [tmp_ref[j, pl.ds(i, rows, stride=S5_CHUNK), :] for i in range(S5_CHUNK)]
        for q in range(GROUPS_PER_VREG):
            dst_ref[j * GROUPS_PER_VREG + q] = jnp.concatenate(
                [w[:, q * SSM_GROUP:(q + 1) * SSM_GROUP] for w in steps], axis=1)


def _group_major_load(src_ref, tmp_ref):
    rows = src_ref.shape[1]
    for j in range(SSM_COL_BLOCKS):
        blocks = [src_ref[j * GROUPS_PER_VREG + q] for q in range(GROUPS_PER_VREG)]
        for i in range(S5_CHUNK):
            tmp_ref[j, pl.ds(i, rows, stride=S5_CHUNK), :] = jnp.concatenate(
                [b[:, i * SSM_GROUP:(i + 1) * SSM_GROUP] for b in blocks], axis=1)
    return jnp.concatenate([tmp_ref[j] for j in range(SSM_COL_BLOCKS)], axis=1)


def _modnorm_mm_kernel(x_ref, g_ref, mod_ref, w_ref, o_ref, *ug_refs, slot, ug_col):
    h = _modnorm(x_ref[...], g_ref[...], mod_ref, slot)
    out = jnp.dot(h.astype(BF16), w_ref[...], preferred_element_type=F32)
    o_ref[...] = out
    if ug_col is not None:
        ug_ref, tmp_ref = ug_refs
        _group_major_store(out[:, ug_col:ug_col + D_SSM], tmp_ref, ug_ref)


def modnorm_matmul(lay, x, g, mods, slot, w_bf16, ug_col=None):
    t = lay.t
    n = w_bf16.shape[1]
    mi = _mod_index(lay, ROW_TILE)
    out_specs = [pl.BlockSpec((ROW_TILE, n), lambda i: (i, 0))]
    out_shape = [jax.ShapeDtypeStruct((t, n), F32)]
    if ug_col is not None:
        out_specs.append(pl.BlockSpec((SSM_GROUPS, ROW_TILE // S5_CHUNK, S5_LANES), lambda i: (0, i, 0)))
        out_shape.append(jax.ShapeDtypeStruct((SSM_GROUPS, t // S5_CHUNK, S5_LANES), F32))
    outs = pl.pallas_call(
        functools.partial(_modnorm_mm_kernel, slot=slot, ug_col=ug_col),
        grid=(t // ROW_TILE,),
        in_specs=[pl.BlockSpec((ROW_TILE, D_MODEL), lambda i: (i, 0)),
                  pl.BlockSpec((1, D_MODEL), lambda i: (0, 0)),
                  pl.BlockSpec((1, N_MOD, D_MODEL), lambda i: (mi(i), 0, 0)),
                  pl.BlockSpec((D_MODEL, n), lambda i: (0, 0))],
        out_specs=out_specs,
        out_shape=out_shape,
        scratch_shapes=([pltpu.VMEM((SSM_COL_BLOCKS, ROW_TILE, LANES), F32)]
                        if ug_col is not None else []),
        compiler_params=_cparams(("arbitrary",)),
        name="modnorm_matmul",
    )(x, g.reshape(1, -1), mods, w_bf16)
    return outs if ug_col is not None else outs[0]


def _seq_tile_maps(lay, reverse):
    assert lay.s_ctx == SEQ_TILE and lay.s_lat % SEQ_TILE == 0
    n_tiles = lay.t // SEQ_TILE
    per_lat = lay.s_lat // SEQ_TILE

    def tile(i):
        return (n_tiles - 1 - i) if reverse else i

    def seq(i):
        ti = tile(i)
        return jnp.where(ti < lay.n_ctx, ti, lay.n_ctx + (ti - lay.n_ctx) // per_lat)

    return n_tiles, tile, seq


def _softplus(x):
    return jnp.maximum(x, 0.0) + jnp.log(1.0 + jnp.exp(-jnp.abs(x)))


def _lru_kernel(rec_ref, prev_ref, next_ref, cw_ref, cb_ref, wg_ref, bg_ref, lam_ref, h0_ref,
                *rest, reverse, n_ctx, per_lat, n_tiles):
    if reverse:
        gate_ref, hf_ref, y_ref, st_ref, a_s, b_s, h_s, carry = rest
    else:
        y_ref, st_ref, a_s, b_s, h_s, carry = rest
    i = pl.program_id(0)
    ti = (n_tiles - 1 - i) if reverse else i
    is_first = jnp.logical_or(ti < n_ctx, (ti - n_ctx) % per_lat == 0)
    is_last = jnp.logical_or(ti < n_ctx, (ti - n_ctx) % per_lat == per_lat - 1)
    ts = SEQ_TILE

    rec = rec_ref[...]
    prev = jnp.where(is_first, 0.0, prev_ref[...])
    nxt = jnp.where(is_last, 0.0, next_ref[...])
    ext = jnp.concatenate([prev, rec, nxt], axis=0)
    n_ext = ts + 2 * SUBLANES
    cw = cw_ref[...]
    xc = cb_ref[...] + cw[2:3, :] * rec
    xc = xc + cw[0:1, :] * pltpu.roll(ext, 2, 0)[SUBLANES:SUBLANES + ts]
    xc = xc + cw[1:2, :] * pltpu.roll(ext, 1, 0)[SUBLANES:SUBLANES + ts]
    xc = xc + cw[3:4, :] * pltpu.roll(ext, n_ext - 1, 0)[SUBLANES:SUBLANES + ts]

    gates = jax.nn.sigmoid(jnp.dot(xc.astype(BF16), wg_ref[...], preferred_element_type=F32)
                           + bg_ref[...])
    r = gates[:, :D_LRU]
    ig = gates[:, D_LRU:]
    log_a = (-LRU_C) * r * _softplus(-lam_ref[...])
    a = jnp.exp(log_a)
    b = jnp.sqrt(1.0 - jnp.exp(2.0 * log_a)) * (ig * xc)

    row8 = lax.broadcasted_iota(I32, (ts, D_LRU), 0) % SUBLANES
    for sh in (1, 2, 4):
        if reverse:
            keep = row8 < SUBLANES - sh
            a_sh = pltpu.roll(a, ts - sh, 0)
            b_sh = pltpu.roll(b, ts - sh, 0)
        else:
            keep = row8 >= sh
            a_sh = pltpu.roll(a, sh, 0)
            b_sh = pltpu.roll(b, sh, 0)
        b = b + a * jnp.where(keep, b_sh, 0.0)
        a = a * jnp.where(keep, a_sh, 1.0)
    a_s[...] = a
    b_s[...] = b

    @pl.when(is_last if reverse else is_first)
    def _():
        carry[...] = h0_ref[0]

    n_grp = ts // SUBLANES

    def body(k, c):
        gi = (n_grp - 1 - k) if reverse else k
        sl = pl.ds(pl.multiple_of(gi * SUBLANES, SUBLANES), SUBLANES)
        h = b_s[sl, :] + a_s[sl, :] * c
        h_s[sl, :] = h
        return h[0:1, :] if reverse else h[SUBLANES - 1:SUBLANES, :]

    c_fin = lax.fori_loop(0, n_grp, body, carry[...], unroll=4)
    carry[...] = c_fin
    st_ref[0] = c_fin
    if reverse:
        y_ref[...] = (hf_ref[...] + h_s[...]) * jax.nn.gelu(gate_ref[...])
    else:
        y_ref[...] = h_s[...]


def lru_pass(lay, proj, conv_w, conv_b, wg_bf16, bg, lam, h0, reverse, hf=None):
    n_tiles, tile, seq = _seq_tile_maps(lay, reverse)
    per_lat = lay.s_lat // SEQ_TILE
    blk8 = SEQ_TILE // SUBLANES
    last8 = lay.t // SUBLANES - 1
    c = D_LRU
    in_specs = [
        pl.BlockSpec((SEQ_TILE, c), lambda i: (tile(i), 1)),
        pl.BlockSpec((SUBLANES, c), lambda i: (jnp.maximum(tile(i) * blk8 - 1, 0), 1)),
        pl.BlockSpec((SUBLANES, c), lambda i: (jnp.minimum(tile(i) * blk8 + blk8, last8), 1)),
        pl.BlockSpec((CONV_W, c), lambda i: (0, 0)),
        pl.BlockSpec((1, c), lambda i: (0, 0)),
        pl.BlockSpec((c, 2 * c), lambda i: (0, 0)),
        pl.BlockSpec((1, 2 * c), lambda i: (0, 0)),
        pl.BlockSpec((1, c), lambda i: (0, 0)),
        pl.BlockSpec((1, 1, c), lambda i: (seq(i), 0, 0)),
    ]
    args = [proj, proj, proj, conv_w, conv_b.reshape(1, -1), wg_bf16, bg.reshape(1, -1),
            lam.reshape(1, -1), h0.reshape(lay.n_seq, 1, c)]
    if reverse:
        in_specs += [pl.BlockSpec((SEQ_TILE, c), lambda i: (tile(i), 0)),
                     pl.BlockSpec((SEQ_TILE, c), lambda i: (tile(i), 0))]
        args += [proj, hf]
    y, st = pl.pallas_call(
        functools.partial(_lru_kernel, reverse=reverse, n_ctx=lay.n_ctx, per_lat=per_lat,
                          n_tiles=n_tiles),
        grid=(n_tiles,),
        in_specs=in_specs,
        out_specs=[pl.BlockSpec((SEQ_TILE, c), lambda i: (tile(i), 0)),
                   pl.BlockSpec((1, 1, c), lambda i: (seq(i), 0, 0))],
        out_shape=[jax.ShapeDtypeStruct((lay.t, c), F32),
                   jax.ShapeDtypeStruct((lay.n_seq, 1, c), F32)],
        scratch_shapes=[pltpu.VMEM((SEQ_TILE, c), F32), pltpu.VMEM((SEQ_TILE, c), F32),
                        pltpu.VMEM((SEQ_TILE, c), F32), pltpu.VMEM((1, c), F32)],
        compiler_params=_cparams(("arbitrary",)),
        name="lru_bwd" if reverse else "lru_fwd",
    )(*args)
    return y, st.reshape(lay.n_seq, c)


def _cmul(a, b):
    return a[0] * b[0] - a[1] * b[1], a[0] * b[1] + a[1] * b[0]


def _s5_matrices(a_re, a_im, log_dt, b_re, b_im, c_re, c_im):
    a_re, a_im = a_re.astype(F32), a_im.astype(F32)
    dt = jnp.exp(log_dt.astype(F32))[..., None]
    z = (a_re * dt, a_im * dt)

    def zpow(k):
        k = k.reshape((-1,) + (1,) * z[0].ndim)
        mag = jnp.exp(k * z[0][None])
        return mag * jnp.cos(k * z[1][None]), mag * jnp.sin(k * z[1][None])

    a_bar = zpow(jnp.ones((1,), F32))
    a_bar = (a_bar[0][0], a_bar[1][0])
    den = a_re * a_re + a_im * a_im
    xr, xi = a_bar[0] - 1.0, a_bar[1]
    q = ((xr * a_re + xi * a_im) / den, (xi * a_re - xr * a_im) / den)
    b_bar = _cmul((q[0][..., None], q[1][..., None]), (b_re.astype(F32), b_im.astype(F32)))
    cc = (c_re.astype(F32), c_im.astype(F32))
    el = S5_CHUNK
    pw = zpow(jnp.arange(el + 1, dtype=F32))
    idx = jnp.arange(el)
    m_in, m_toep, m_out = [], [], []
    for d in range(2):
        p_d = (pw[0][:, d], pw[1][:, d])
        b_d = (b_bar[0][d], b_bar[1][d])
        c_d = (cc[0][d], cc[1][d])
        k_in = (el - 1 - idx) if d == 0 else idx
        w_in = _cmul((p_d[0][k_in][..., None], p_d[1][k_in][..., None]),
                     (b_d[0][None], b_d[1][None]))
        w_in = [jnp.transpose(w, (1, 0, 3, 2)).reshape(SSM_GROUPS, S5_LANES, SSM_STATE) for w in w_in]
        m_in.append(jnp.concatenate(w_in, axis=-1))
        cp = _cmul((c_d[0][None], c_d[1][None]),
                   (p_d[0][:, :, None, :], p_d[1][:, :, None, :]))
        kern = (jnp.einsum('kghp,gpc->kgch', cp[0][:el], b_d[0])
                - jnp.einsum('kghp,gpc->kgch', cp[1][:el], b_d[1]))
        zero = jnp.zeros_like(kern[0])
        rows = []
        for i in range(el):
            if d == 0:
                pieces = [zero] * i + [kern[k] for k in range(el - i)]
            else:
                pieces = [kern[i - j] for j in range(i + 1)] + [zero] * (el - 1 - i)
            rows.append(jnp.concatenate(pieces, axis=-1))
        m_toep.append(jnp.stack(rows, axis=1).reshape(SSM_GROUPS, S5_LANES, S5_LANES))
        k_out = (idx + 1) if d == 0 else (el - idx)
        w_out = [jnp.transpose(w[k_out], (1, 3, 0, 2)).reshape(SSM_GROUPS, SSM_STATE, S5_LANES)
                 for w in cp]
        m_out.append(jnp.concatenate([w_out[0], -w_out[1]], axis=1))
    mul = zpow(el * 2.0 ** jnp.arange(S5_SCAN_STEPS, dtype=F32))
    mul = [jnp.transpose(m, (2, 1, 0, 3)) for m in mul]
    coef_a = jnp.concatenate([mul[0], mul[0]], axis=-1)
    coef_b = jnp.concatenate([-mul[1], mul[1]], axis=-1)
    stack = lambda xs: jnp.stack(xs, axis=1)
    return (stack(m_in).astype(BF16), stack(m_toep).astype(BF16), stack(m_out).astype(BF16),
            coef_a, coef_b)


def _s5_scan(v, ca, cb, seg, reverse):
    n = v.shape[0]
    assert seg <= 2 ** S5_SCAN_STEPS
    row = lax.broadcasted_iota(I32, (n, 2 * SSM_STATE), 0) % seg
    k, sh = 0, 1
    while sh < seg:
        if reverse:
            s = jnp.where(row < seg - sh, pltpu.roll(v, n - sh, 0), 0.0)
        else:
            s = jnp.where(row >= sh, pltpu.roll(v, sh, 0), 0.0)
        v = v + ca[k:k + 1, :] * s + cb[k:k + 1, :] * pltpu.roll(s, SSM_STATE, 1)
        k += 1
        sh *= 2
    return v


def _s5_shift(h, seg, reverse):
    n = h.shape[0]
    row = lax.broadcasted_iota(I32, (n, 2 * SSM_STATE), 0) % seg
    if reverse:
        return jnp.where(row < seg - 1, pltpu.roll(h, n - 1, 0), 0.0)
    return jnp.where(row >= 1, pltpu.roll(h, 1, 0), 0.0)


def _s5_kernel(u_ref, min_ref, mtoep_ref, mout_ref, ca_ref, cb_ref, h0_ref, y_ref, hc_ref,
               v_s, hp_s, *, rc, seg_c, n_lat, seg_l):
    u = u_ref[0].astype(BF16)
    u_c, u_l = u[:rc], u[rc:]
    y_c = jnp.zeros((rc, S5_LANES), F32)
    y_l = jnp.zeros((n_lat * seg_l, S5_LANES), F32)
    for d in range(2):
        reverse = d == 1
        ca = ca_ref[0, d]
        cb = cb_ref[0, d]
        m_in = min_ref[0, d]
        m_toep = mtoep_ref[0, d]
        m_out = mout_ref[0, d]
        h_c = _s5_scan(jnp.dot(u_c, m_in, preferred_element_type=F32), ca, cb, seg_c, reverse)
        hc_ref[0, d] = h_c
        hp_c = _s5_shift(h_c, seg_c, reverse)
        y_c = y_c + jnp.dot(u_c, m_toep, preferred_element_type=F32)
        y_c = y_c + jnp.dot(hp_c.astype(BF16), m_out, preferred_element_type=F32)
        v_s[...] = jnp.dot(u_l, m_in, preferred_element_type=F32)
        for s in range(n_lat):
            h0 = h0_ref[0, d, s:s + 1, :]
            r0 = s * seg_l + (seg_l - 1 if reverse else 0)
            v_s[r0:r0 + 1, :] = (v_s[r0:r0 + 1, :] + ca[0:1, :] * h0
                                 + cb[0:1, :] * pltpu.roll(h0, SSM_STATE, 1))
        h_l = _s5_scan(v_s[...], ca, cb, seg_l, reverse)
        hp_s[...] = _s5_shift(h_l, seg_l, reverse)
        for s in range(n_lat):
            r0 = s * seg_l + (seg_l - 1 if reverse else 0)
            hp_s[r0:r0 + 1, :] = h0_ref[0, d, s:s + 1, :]
        y_l = y_l + jnp.dot(u_l, m_toep, preferred_element_type=F32)
        y_l = y_l + jnp.dot(hp_s[...].astype(BF16), m_out, preferred_element_type=F32)
    y_ref[0, :rc, :] = y_c
    y_ref[0, rc:, :] = y_l


def s5_mixer(lay, u_g, mats, h0):
    m_in, m_toep, m_out, coef_a, coef_b = mats
    rows = lay.t // S5_CHUNK
    rc = lay.t_ctx // S5_CHUNK
    rl = rows - rc
    st2 = 2 * SSM_STATE
    g4 = lambda g: (g, 0, 0, 0)
    return pl.pallas_call(
        functools.partial(_s5_kernel, rc=rc, seg_c=lay.s_ctx // S5_CHUNK, n_lat=lay.n_lat,
                          seg_l=lay.s_lat // S5_CHUNK),
        grid=(SSM_GROUPS,),
        in_specs=[pl.BlockSpec((1, rows, S5_LANES), lambda g: (g, 0, 0)),
                  pl.BlockSpec((1, 2, S5_LANES, st2), g4),
                  pl.BlockSpec((1, 2, S5_LANES, S5_LANES), g4),
                  pl.BlockSpec((1, 2, st2, S5_LANES), g4),
                  pl.BlockSpec((1, 2, S5_SCAN_STEPS, st2), g4),
                  pl.BlockSpec((1, 2, S5_SCAN_STEPS, st2), g4),
                  pl.BlockSpec((1, 2, lay.n_lat, st2), g4)],
        out_specs=[pl.BlockSpec((1, rows, S5_LANES), lambda g: (g, 0, 0)),
                   pl.BlockSpec((1, 2, rc, st2), g4)],
        out_shape=[jax.ShapeDtypeStruct((SSM_GROUPS, rows, S5_LANES), F32),
                   jax.ShapeDtypeStruct((SSM_GROUPS, 2, rc, st2), F32)],
        scratch_shapes=[pltpu.VMEM((rl, st2), F32), pltpu.VMEM((rl, st2), F32)],
        compiler_params=_cparams(("arbitrary",)),
        name="s5_mixer",
    )(u_g, m_in, m_toep, m_out, coef_a, coef_b, h0)


def _even_out_kernel(x_ref, ya_ref, yg_ref, u_ref, d_ref, gw_ref, gb_ref, w_ref, mod_ref, o_ref, yt_s):
    ys = _group_major_load(yg_ref, yt_s) + d_ref[...] * u_ref[...]
    g = jax.nn.gelu(ys)
    yb = g * jax.nn.sigmoid(jnp.dot(g.astype(BF16), gw_ref[...], preferred_element_type=F32)
                            + gb_ref[...])
    out = jnp.dot(ya_ref[...].astype(BF16), w_ref[:D_LRU, :], preferred_element_type=F32)
    out = out + jnp.dot(yb.astype(BF16), w_ref[D_LRU:, :], preferred_element_type=F32)
    o_ref[...] = x_ref[...] + mod_ref[0, 2:3, :] * out


def even_out(lay, x, y_a, y_g, proj, ssm_d, glu_w_bf16, glu_b, w_out_bf16, mods):
    mi = _mod_index(lay, ROW_TILE)
    c = D_SSM
    row = lambda i: (i, 0)
    const = lambda i: (0, 0)
    return pl.pallas_call(
        _even_out_kernel,
        grid=(lay.t // ROW_TILE,),
        in_specs=[pl.BlockSpec((ROW_TILE, D_MODEL), row),
                  pl.BlockSpec((ROW_TILE, c), row),
                  pl.BlockSpec((SSM_GROUPS, ROW_TILE // S5_CHUNK, S5_LANES), lambda i: (0, i, 0)),
                  pl.BlockSpec((ROW_TILE, c), lambda i: (i, 2)),
                  pl.BlockSpec((1, c), const),
                  pl.BlockSpec((c, c), const),
                  pl.BlockSpec((1, c), const),
                  pl.BlockSpec((D_MODEL, D_MODEL), const),
                  pl.BlockSpec((1, N_MOD, D_MODEL), lambda i: (mi(i), 0, 0))],
        out_specs=pl.BlockSpec((ROW_TILE, D_MODEL), row),
        out_shape=jax.ShapeDtypeStruct((lay.t, D_MODEL), F32),
        scratch_shapes=[pltpu.VMEM((SSM_COL_BLOCKS, ROW_TILE, LANES), F32)],
        compiler_params=_cparams(("arbitrary",)),
        name="even_out",
    )(x, y_a, y_g, proj, ssm_d.reshape(1, -1), glu_w_bf16, glu_b.reshape(1, -1), w_out_bf16, mods)


def _softmax_pv(parts, sink_col):
    m = sink_col
    for s, _ in parts:
        m = jnp.maximum(m, jnp.max(s, axis=-1, keepdims=True))
    den = jnp.exp(sink_col - m)
    acc = None
    for s, v in parts:
        p = jnp.exp(s - m)
        den = den + jnp.sum(p, axis=-1, keepdims=True)
        pv = jnp.dot(p.astype(BF16), v.astype(BF16), preferred_element_type=F32)
        acc = pv if acc is None else acc + pv
    return acc / den


def _nt_dot(a, b):
    return lax.dot_general(a.astype(BF16), b.astype(BF16), (((1,), (1,)), ((), ())),
                           preferred_element_type=F32)


def _attn_ctx_kernel(q_ref, k_ref, v_ref, sink_ref, o_ref):
    n = q_ref.shape[0]
    for kh in range(N_KV):
        k = k_ref[:, kh * HEAD_DIM:(kh + 1) * HEAD_DIM]
        v = v_ref[:, kh * HEAD_DIM:(kh + 1) * HEAD_DIM]
        for g in range(GQA):
            h = kh * GQA + g
            q = q_ref[:, h * HEAD_DIM:(h + 1) * HEAD_DIM]
            s = _nt_dot(q, k) * ATTN_SCALE
            sink = jnp.broadcast_to(sink_ref[0:1, h:h + 1], (n, 1))
            o_ref[:, h * HEAD_DIM:(h + 1) * HEAD_DIM] = _softmax_pv([(s, v)], sink)


def attn_context(lay, qkv, sink):
    nq = N_HEADS * HEAD_DIM
    nkv = N_KV * HEAD_DIM
    return pl.pallas_call(
        _attn_ctx_kernel,
        grid=(lay.n_ctx,),
        in_specs=[pl.BlockSpec((lay.s_ctx, nq), lambda b: (b, 0)),
                  pl.BlockSpec((lay.s_ctx, nkv), lambda b: (b, nq // nkv)),
                  pl.BlockSpec((lay.s_ctx, nkv), lambda b: (b, nq // nkv + 1)),
                  pl.BlockSpec((1, N_HEADS), lambda b: (0, 0))],
        out_specs=pl.BlockSpec((lay.s_ctx, nq), lambda b: (b, 0)),
        out_shape=jax.ShapeDtypeStruct((lay.t_ctx, nq), F32),
        compiler_params=_cparams(("arbitrary",)),
        name="attn_context",
    )(qkv, qkv, qkv, sink.reshape(1, -1))


def _rope(x, cos, sin):
    lane = lax.broadcasted_iota(I32, (x.shape[0], 2 * HEAD_DIM), 1) % HEAD_DIM
    outs = []
    for j in range(x.shape[1] // (2 * HEAD_DIM)):
        xs = x[:, j * 2 * HEAD_DIM:(j + 1) * 2 * HEAD_DIM]
        sw = jnp.where(lane < HEAD_DIM // 2,
                       pltpu.roll(xs, 2 * HEAD_DIM - HEAD_DIM // 2, 1),
                       pltpu.roll(xs, HEAD_DIM // 2, 1))
        outs.append(xs * cos + sw * sin)
    return outs


def _attn_lat_kernel(q_ref, k0_ref, k1_ref, k2_ref, v0_ref, v1_ref, v2_ref, ck_ref, cv_ref,
                     cq_ref, sq_ref, c0_ref, c1_ref, c2_ref, s0_ref, s1_ref, s2_ref, sink_ref,
                     o_ref, *, n_blk):
    j = pl.program_id(1)
    qb = Q_BLOCK
    q_parts = [qp * ATTN_SCALE for qp in _rope(q_ref[...], cq_ref[...], sq_ref[...])]
    k_parts = [_rope(kr[...], cr[...], sr[...])
               for kr, cr, sr in ((k0_ref, c0_ref, s0_ref), (k1_ref, c1_ref, s1_ref),
                                  (k2_ref, c2_ref, s2_ref))]
    qi = lax.broadcasted_iota(I32, (qb, 3 * qb), 0)
    km = lax.broadcasted_iota(I32, (qb, 3 * qb), 1)
    kpos = j * qb - qb + km
    mask1 = (jnp.abs(km - qb - qi) <= WINDOW) & (kpos >= 0) & (kpos < n_blk * qb)
    mask = jnp.concatenate([mask1] * GQA, axis=0)
    for kh in range(N_KV):
        half = (kh % 2) * HEAD_DIM
        k_loc = jnp.concatenate([kp[kh // 2][:, half:half + HEAD_DIM] for kp in k_parts], axis=0)
        v_loc = jnp.concatenate([vr[:, kh * HEAD_DIM:(kh + 1) * HEAD_DIM]
                                 for vr in (v0_ref, v1_ref, v2_ref)], axis=0)
        qs, sinks = [], []
        for g in range(GQA):
            h = kh * GQA + g
            qs.append(q_parts[h // 2][:, (h % 2) * HEAD_DIM:(h % 2 + 1) * HEAD_DIM])
            sinks.append(jnp.broadcast_to(sink_ref[0:1, h:h + 1], (qb, 1)))
        q = jnp.concatenate(qs, axis=0)
        sink = jnp.concatenate(sinks, axis=0)
        s_loc = jnp.where(mask, _nt_dot(q, k_loc), NEG_INF)
        s_ctx = _nt_dot(q, ck_ref[0, kh])
        o = _softmax_pv([(s_loc, v_loc), (s_ctx, cv_ref[0, kh])], sink)
        for g in range(GQA):
            h = kh * GQA + g
            o_ref[:, h * HEAD_DIM:(h + 1) * HEAD_DIM] = o[g * qb:(g + 1) * qb]


def _rope_tables(s_len):
    rows = s_len // GRID_W
    row = jnp.repeat(jnp.arange(rows), GRID_W).astype(F32)
    col = jnp.tile(jnp.arange(GRID_W), rows).astype(F32)
    nf = HEAD_DIM // 4
    inv = ROPE_BASE ** (-jnp.arange(nf, dtype=F32) / nf)
    ang = jnp.concatenate([row[:, None] * inv, col[:, None] * inv], axis=-1)
    cos, sin = jnp.cos(ang), jnp.sin(ang)
    cos2 = jnp.tile(jnp.concatenate([cos, cos], axis=-1), (1, 2))
    sin2 = jnp.tile(jnp.concatenate([-sin, sin], axis=-1), (1, 2))
    return cos2, sin2


def attn_latent(lay, qkv, cache_k, cache_v, sink):
    nq = N_HEADS * HEAD_DIM
    nkv = N_KV * HEAD_DIM
    n_blk = lay.s_lat // Q_BLOCK
    base = lay.t_ctx // Q_BLOCK
    n_ctx_keys = cache_k.shape[2]
    cos2, sin2 = _rope_tables(lay.s_lat)
    kcol = nq // nkv

    def qrow(b, j):
        return base + b * n_blk + j

    def krow(off):
        return lambda b, j: base + b * n_blk + jnp.clip(j + off, 0, n_blk - 1)

    def trow(off):
        return lambda b, j: (jnp.clip(j + off, 0, n_blk - 1), 0)

    kv_spec = lambda off, col: pl.BlockSpec((Q_BLOCK, nkv), lambda b, j: (krow(off)(b, j), col))
    tab = lambda off: pl.BlockSpec((Q_BLOCK, 2 * HEAD_DIM), trow(off))
    cache_spec = pl.BlockSpec((1, N_KV, n_ctx_keys, HEAD_DIM), lambda b, j: (b, 0, 0, 0))
    return pl.pallas_call(
        functools.partial(_attn_lat_kernel, n_blk=n_blk),
        grid=(lay.n_lat, n_blk),
        in_specs=[pl.BlockSpec((Q_BLOCK, nq), lambda b, j: (qrow(b, j), 0)),
                  kv_spec(-1, kcol), kv_spec(0, kcol), kv_spec(1, kcol),
                  kv_spec(-1, kcol + 1), kv_spec(0, kcol + 1), kv_spec(1, kcol + 1),
                  cache_spec, cache_spec,
                  tab(0), tab(0), tab(-1), tab(0), tab(1), tab(-1), tab(0), tab(1),
                  pl.BlockSpec((1, N_HEADS), lambda b, j: (0, 0))],
        out_specs=pl.BlockSpec((Q_BLOCK, nq), lambda b, j: (b * n_blk + j, 0)),
        out_shape=jax.ShapeDtypeStruct((lay.t_lat, nq), F32),
        compiler_params=_cparams(("arbitrary", "arbitrary")),
        name="attn_latent",
    )(qkv, qkv, qkv, qkv, qkv, qkv, qkv, cache_k, cache_v,
      cos2, sin2, cos2, cos2, cos2, sin2, sin2, sin2, sink.reshape(1, -1))


def _mm_res_kernel(x_ref, ac_ref, al_ref, w_ref, mod_ref, o_ref, *, n_ctx_tiles):
    a = jnp.where(pl.program_id(0) < n_ctx_tiles, ac_ref[...], al_ref[...])
    out = jnp.dot(a.astype(BF16), w_ref[...], preferred_element_type=F32)
    o_ref[...] = x_ref[...] + mod_ref[0, 2:3, :] * out


def matmul_residual(lay, x, a_ctx, a_lat, w_bf16, mods):
    mi = _mod_index(lay, ROW_TILE)
    k = a_ctx.shape[1]
    nct = lay.t_ctx // ROW_TILE
    return pl.pallas_call(
        functools.partial(_mm_res_kernel, n_ctx_tiles=nct),
        grid=(lay.t // ROW_TILE,),
        in_specs=[pl.BlockSpec((ROW_TILE, D_MODEL), lambda i: (i, 0)),
                  pl.BlockSpec((ROW_TILE, k), lambda i: (jnp.minimum(i, nct - 1), 0)),
                  pl.BlockSpec((ROW_TILE, k), lambda i: (jnp.maximum(i - nct, 0), 0)),
                  pl.BlockSpec((k, D_MODEL), lambda i: (0, 0)),
                  pl.BlockSpec((1, N_MOD, D_MODEL), lambda i: (mi(i), 0, 0))],
        out_specs=pl.BlockSpec((ROW_TILE, D_MODEL), lambda i: (i, 0)),
        out_shape=jax.ShapeDtypeStruct((lay.t, D_MODEL), F32),
        compiler_params=_cparams(("arbitrary",)),
        name="matmul_residual",
    )(x, a_ctx, a_lat, w_bf16, mods)


def _rowtile_load(ref, n, base=0):
    parts = []
    for c in range(ROW_CHUNKS):
        words = ref[pl.ds(base + c, n, stride=ROW_CHUNKS), :]
        for half in range(2):
            parts.append(pltpu.unpack_elementwise(words, index=half, packed_dtype=BF16,
                                                  unpacked_dtype=F32))
    return jnp.concatenate(parts, axis=1)


def _rowtile_store(ref, val, n):
    for c in range(ROW_CHUNKS):
        lo = val[:, 2 * c * LANES:(2 * c + 1) * LANES]
        hi = val[:, (2 * c + 1) * LANES:(2 * c + 2) * LANES]
        ref[pl.ds(c, n, stride=ROW_CHUNKS), :] = pltpu.pack_elementwise([lo, hi], packed_dtype=BF16)


def _row_copy(src, src_row, dst, dst_row, sem):
    return pltpu.make_async_copy(
        src.at[pl.ds(pl.multiple_of(src_row * ROW_CHUNKS, ROW_CHUNKS), ROW_CHUNKS), :],
        dst.at[pl.ds(pl.multiple_of(dst_row * ROW_CHUNKS, ROW_CHUNKS), ROW_CHUNKS), :], sem)


def _router_kernel(x_ref, g_ref, mod_ref, rwt_ref, rb_ref, tri_ref, sg_ref, su_ref, sd_ref,
                   hn_ref, sh_ref, eidx_ref, wts_ref, rank_ref, cnt_ref, cnt_s):
    tm = ROUTER_TILE

    @pl.when(pl.program_id(0) == 0)
    def _():
        cnt_s[...] = jnp.zeros_like(cnt_s)

    h = _modnorm(x_ref[...], g_ref[...], mod_ref, 3)
    _rowtile_store(hn_ref, h, tm)
    hb = h.astype(BF16)
    sgate = jnp.dot(hb, sg_ref[...], preferred_element_type=F32)
    sup = jnp.dot(hb, su_ref[...], preferred_element_type=F32)
    sh_ref[...] = jnp.dot((sgate * jax.nn.sigmoid(sgate) * sup).astype(BF16), sd_ref[...],
                          preferred_element_type=F32)
    logits = lax.dot_general(rwt_ref[...], h, (((1,), (1,)), ((), ())),
                             precision=HIGHEST, preferred_element_type=F32)
    scores = jax.nn.sigmoid(logits)
    choice = scores + rb_ref[...]
    gs_rows = []
    for g in range(N_GROUPS):
        cg = choice[g * GROUP_SIZE:(g + 1) * GROUP_SIZE, :]
        m1 = jnp.max(cg, axis=0, keepdims=True)
        eq = cg == m1
        cnt = jnp.sum(eq.astype(F32), axis=0, keepdims=True)
        m2 = jnp.max(jnp.where(eq, -jnp.inf, cg), axis=0, keepdims=True)
        gs_rows.append(m1 + jnp.where(cnt >= 2.0, m1, m2))
    gs = jnp.concatenate(gs_rows, axis=0)
    gi = lax.broadcasted_iota(I32, (N_GROUPS, tm), 0)
    grank = jnp.zeros((N_GROUPS, tm), I32)
    for g in range(N_GROUPS):
        other = gs[g:g + 1, :]
        ahead = (other > gs) | ((other == gs) & (g < gi))
        grank = grank + ahead.astype(I32)
    gsel = grank < TOPK_GROUPS
    emask = jnp.concatenate(
        [jnp.broadcast_to(gsel[g:g + 1, :], (GROUP_SIZE, tm)) for g in range(N_GROUPS)], axis=0)
    masked = jnp.where(emask, choice, -jnp.inf)
    ei = lax.broadcasted_iota(I32, (N_EXPERTS, tm), 0)
    idxs, ws = [], []
    member = jnp.zeros((N_EXPERTS, tm), F32)
    for _ in range(TOP_K):
        m = jnp.max(masked, axis=0, keepdims=True)
        idx = jnp.min(jnp.where(masked == m, ei, N_EXPERTS), axis=0, keepdims=True)
        hit = ei == idx
        ws.append(jnp.sum(jnp.where(hit, scores, 0.0), axis=0, keepdims=True))
        idxs.append(idx)
        member = jnp.where(hit, 1.0, member)
        masked = jnp.where(hit, -jnp.inf, masked)
    w = jnp.concatenate(ws, axis=0)
    wts_ref[...] = w / jnp.sum(w, axis=0, keepdims=True) * ROUTE_SCALE
    eidx_ref[...] = jnp.concatenate(idxs, axis=0)
    before = jnp.dot(member.astype(BF16), tri_ref[...], preferred_element_type=F32) + cnt_s[...]
    ranks = [jnp.sum(jnp.where(ei == idx, before, 0.0), axis=0, keepdims=True) for idx in idxs]
    rank_ref[...] = jnp.concatenate(ranks, axis=0).astype(I32)
    cnt_s[...] = cnt_s[...] + jnp.sum(member, axis=1, keepdims=True)
    cnt_ref[...] = jnp.broadcast_to(cnt_s[...], cnt_ref.shape)


def moe_router(lay, x, g, mods, router_w, router_b, sg_bf16, su_bf16, sd_bf16):
    t = lay.t
    tm = ROUTER_TILE
    mi = _mod_index(lay, tm)
    tri = (jnp.arange(tm)[:, None] < jnp.arange(tm)[None, :]).astype(BF16)
    tok = lambda i: (0, i)
    const = lambda i: (0, 0)
    return pl.pallas_call(
        _router_kernel,
        grid=(t // tm,),
        in_specs=[pl.BlockSpec((tm, D_MODEL), lambda i: (i, 0)),
                  pl.BlockSpec((1, D_MODEL), const),
                  pl.BlockSpec((1, N_MOD, D_MODEL), lambda i: (mi(i), 0, 0)),
                  pl.BlockSpec((N_EXPERTS, D_MODEL), const),
                  pl.BlockSpec((N_EXPERTS, 1), const),
                  pl.BlockSpec((tm, tm), const),
                  pl.BlockSpec((D_MODEL, D_EXPERT), const),
                  pl.BlockSpec((D_MODEL, D_EXPERT), const),
                  pl.BlockSpec((D_EXPERT, D_MODEL), const)],
        out_specs=[pl.BlockSpec((tm * ROW_CHUNKS, LANES), lambda i: (i, 0)),
                   pl.BlockSpec((tm, D_MODEL), lambda i: (i, 0)),
                   pl.BlockSpec((TOP_K, tm), tok),
                   pl.BlockSpec((TOP_K, tm), tok),
                   pl.BlockSpec((TOP_K, tm), tok),
                   pl.BlockSpec((N_EXPERTS, LANES), const)],
        out_shape=[jax.ShapeDtypeStruct((t * ROW_CHUNKS, LANES), PACKED),
                   jax.ShapeDtypeStruct((t, D_MODEL), F32),
                   jax.ShapeDtypeStruct((TOP_K, t), I32),
                   jax.ShapeDtypeStruct((TOP_K, t), F32),
                   jax.ShapeDtypeStruct((TOP_K, t), I32),
                   jax.ShapeDtypeStruct((N_EXPERTS, LANES), F32)],
        scratch_shapes=[pltpu.VMEM((N_EXPERTS, 1), F32)],
        compiler_params=_cparams(("arbitrary",)),
        name="moe_router",
    )(x, g.reshape(1, -1), mods, router_w.T, router_b.reshape(-1, 1), tri, sg_bf16, su_bf16, sd_bf16)


def _dest_kernel(start_ref, eidx_ref, rank_ref, dest_ref):
    e = eidx_ref[...]

    def body(i, acc):
        return jnp.where(e == i, start_ref[i], acc)

    dest_ref[...] = lax.fori_loop(0, N_EXPERTS, body, jnp.zeros_like(e), unroll=8) + rank_ref[...]


def moe_dest(pad_start, eidx, rank):
    t = eidx.shape[1]
    tn = DEST_TILE
    spec = pl.BlockSpec((TOP_K, tn), lambda i, ps: (0, i))
    return pl.pallas_call(
        _dest_kernel,
        grid_spec=pltpu.PrefetchScalarGridSpec(
            num_scalar_prefetch=1, grid=(t // tn,), in_specs=[spec, spec], out_specs=spec),
        out_shape=jax.ShapeDtypeStruct((TOP_K, t), I32),
        compiler_params=_cparams(("arbitrary",)),
        name="moe_dest",
    )(pad_start, eidx, rank)


def _issue_row_copies(idx_at, n, copy_at, unroll=4):
    def body(i, c):
        for p in range(2):
            r = 2 * i + p
            copy_at(r, idx_at(r)).start(priority=p)
        return c
    lax.fori_loop(0, n // 2, body, 0, unroll=unroll)


def _dispatch_kernel(zrow_ref, dest_hbm, hn_ref, xs_hbm, idx_s, zbuf, isem, zsem, ssem, *, n_tiles):
    i = pl.program_id(0)
    slot = i % 2
    td = DISPATCH_TILE

    n_idx = TOP_K * td

    def idx_copy(tile, s):
        return pltpu.make_async_copy(dest_hbm.at[tile], idx_s.at[pl.ds(s * n_idx, n_idx)], isem.at[s])

    def zero_copy(e):
        r0 = pl.multiple_of(zrow_ref[e] * ROW_CHUNKS, ROW_CHUNKS)
        return pltpu.make_async_copy(zbuf, xs_hbm.at[pl.ds(r0, MOE_BLOCK * ROW_CHUNKS), :], zsem)

    @pl.when(i == 0)
    def _():
        zbuf[...] = jnp.zeros_like(zbuf)

        def zstart(e, c):
            @pl.when(zrow_ref[e] >= 0)
            def _():
                zero_copy(e).start()
            return c

        def zwait(e, c):
            @pl.when(zrow_ref[e] >= 0)
            def _():
                zero_copy(e).wait()
            return c

        lax.fori_loop(0, zrow_ref.shape[0], zstart, 0)
        idx_copy(0, 0).start()
        lax.fori_loop(0, zrow_ref.shape[0], zwait, 0)

    idx_copy(i, slot).wait()

    @pl.when(i + 1 < n_tiles)
    def _():
        idx_copy(i + 1, 1 - slot).start()

    for k in range(TOP_K):
        _issue_row_copies(lambda r: idx_s[slot * n_idx + k * td + r], td,
                          lambda r, d: _row_copy(hn_ref, r, xs_hbm, d, ssem))
    for k in range(TOP_K):
        pltpu.make_async_copy(hn_ref, xs_hbm.at[pl.ds(0, td * ROW_CHUNKS), :], ssem).wait()


def _tile_major(dest, tile):
    t = dest.shape[1]
    return dest.reshape(TOP_K, t // tile, tile).transpose(1, 0, 2).reshape(t // tile, TOP_K * tile)


def moe_dispatch(lay, hn, dest, zero_row, n_rows):
    td = DISPATCH_TILE
    n_tiles = lay.t // td
    return pl.pallas_call(
        functools.partial(_dispatch_kernel, n_tiles=n_tiles),
        grid_spec=pltpu.PrefetchScalarGridSpec(
            num_scalar_prefetch=1,
            grid=(n_tiles,),
            in_specs=[pl.BlockSpec(memory_space=pl.ANY),
                      pl.BlockSpec((td * ROW_CHUNKS, LANES), lambda i, z: (i, 0))],
            out_specs=pl.BlockSpec(memory_space=pl.ANY),
            scratch_shapes=[pltpu.SMEM((2 * TOP_K * td,), I32),
                            pltpu.VMEM((MOE_BLOCK * ROW_CHUNKS, LANES), PACKED),
                            pltpu.SemaphoreType.DMA((2,)),
                            pltpu.SemaphoreType.DMA,
                            pltpu.SemaphoreType.DMA]),
        out_shape=jax.ShapeDtypeStruct((n_rows * ROW_CHUNKS, LANES), PACKED),
        compiler_params=_cparams(("arbitrary",)),
        name="moe_dispatch",
    )(zero_row, dest, hn)


def _expert_kernel(blk0_ref, nblk_ref, tail_ref, xs_hbm, wg_ref, wu_ref, wd_ref, y_hbm,
                   xbuf, ybuf, wg_s, wu_s, wd_s, isem, osem):
    e = pl.program_id(0)
    nb = nblk_ref[e]
    g0 = blk0_ref[e]
    total = blk0_ref[N_EXPERTS - 1] + nblk_ref[N_EXPERTS - 1]
    blk_rows = MOE_BLOCK * ROW_CHUNKS
    n_x = xbuf.shape[0]
    n_y = ybuf.shape[0]

    def block_rows(g):
        return pl.ds(pl.multiple_of(g * blk_rows, blk_rows), blk_rows)

    def fetch(g):
        s = g % n_x
        return pltpu.make_async_copy(xs_hbm.at[block_rows(g), :], xbuf.at[s], isem.at[s])

    def writeback(g):
        s = g % n_y
        return pltpu.make_async_copy(ybuf.at[s], y_hbm.at[block_rows(g), :], osem.at[s])

    ahead = n_x // 2

    @pl.when(e == 0)
    def _():
        for p in range(ahead):
            @pl.when(p < total)
            def _():
                fetch(p).start()

    @pl.when(nb > 0)
    def _():
        wg_s[...] = wg_ref[0, 0].astype(BF16)
        wu_s[...] = wu_ref[0, 0].astype(BF16)
        wd_s[...] = wd_ref[0, 0].astype(BF16)

    def run_blocks(g, n):
        for p in range(n):
            @pl.when(g + ahead + p < total)
            def _():
                fetch(g + ahead + p).start()
        for p in range(n):
            fetch(g + p).wait()
        x = jnp.concatenate([_rowtile_load(xbuf.at[(g + p) % n_x], MOE_BLOCK) for p in range(n)],
                            axis=0).astype(BF16)
        gate = jnp.dot(x, wg_s[...], preferred_element_type=F32)
        up = jnp.dot(x, wu_s[...], preferred_element_type=F32)
        act = gate * jax.nn.sigmoid(gate) * up
        y = jnp.dot(act.astype(BF16), wd_s[...], preferred_element_type=F32)
        for p in range(n):
            @pl.when(g + p >= n_y)
            def _():
                writeback(g + p - n_y).wait()
        for p in range(n):
            _rowtile_store(ybuf.at[(g + p) % n_y], y[p * MOE_BLOCK:(p + 1) * MOE_BLOCK], MOE_BLOCK)
            writeback(g + p).start()

    def quad(jj, c):
        run_blocks(g0 + 4 * jj, 4)
        return c

    lax.fori_loop(0, nb // 4, quad, 0)
    rem = nb % 4

    @pl.when(rem >= 2)
    def _():
        run_blocks(g0 + nb - rem, 2)

    @pl.when(rem % 2 == 1)
    def _():
        run_blocks(g0 + nb - 1, 1)

    @pl.when(e == N_EXPERTS - 1)
    def _():
        for p in range(n_y, 0, -1):
            @pl.when(total >= p)
            def _():
                writeback(total - p).wait()

        ybuf[0] = jnp.zeros(ybuf.shape[1:], PACKED)

        def tail_copy(i):
            r0 = pl.multiple_of(tail_ref[i] * ROW_CHUNKS, blk_rows)
            return pltpu.make_async_copy(ybuf.at[0], y_hbm.at[pl.ds(r0, blk_rows), :], osem.at[0])

        def tstart(i, c):
            @pl.when(tail_ref[i] >= 0)
            def _():
                tail_copy(i).start()
            return c

        def twait(i, c):
            @pl.when(tail_ref[i] >= 0)
            def _():
                tail_copy(i).wait()
            return c

        lax.fori_loop(0, tail_ref.shape[0], tstart, 0)
        lax.fori_loop(0, tail_ref.shape[0], twait, 0)


def moe_experts(xs, first_block, n_blocks, tail_row, layer, w_gate, w_up, w_down):
    wspec = lambda shape: pl.BlockSpec((1, 1) + shape, lambda e, a, b, c: (layer, e, 0, 0))
    blk = (MOE_BLOCK * ROW_CHUNKS, LANES)
    return pl.pallas_call(
        _expert_kernel,
        grid_spec=pltpu.PrefetchScalarGridSpec(
            num_scalar_prefetch=3,
            grid=(N_EXPERTS,),
            in_specs=[pl.BlockSpec(memory_space=pl.ANY),
                      wspec((D_MODEL, D_EXPERT)), wspec((D_MODEL, D_EXPERT)),
                      wspec((D_EXPERT, D_MODEL))],
            out_specs=pl.BlockSpec(memory_space=pl.ANY),
            scratch_shapes=[pltpu.VMEM((EXPERT_X_BUFS,) + blk, PACKED),
                            pltpu.VMEM((EXPERT_Y_BUFS,) + blk, PACKED),
                            pltpu.VMEM((D_MODEL, D_EXPERT), BF16),
                            pltpu.VMEM((D_MODEL, D_EXPERT), BF16),
                            pltpu.VMEM((D_EXPERT, D_MODEL), BF16),
                            pltpu.SemaphoreType.DMA((EXPERT_X_BUFS,)),
                            pltpu.SemaphoreType.DMA((EXPERT_Y_BUFS,))]),
        out_shape=jax.ShapeDtypeStruct(xs.shape, PACKED),
        compiler_params=_cparams(("arbitrary",)),
        name="moe_experts",
    )(first_block, n_blocks, tail_row, xs, w_gate, w_up, w_down)


def _combine_kernel(dest_hbm, y_hbm, x_ref, sh_ref, w_ref, mod_ref, *rest, n_tiles, final):
    if final:
        gf_ref, oc_ref, ol_ref, idx_s, ybuf, isem, gsem = rest
    else:
        o_ref, idx_s, ybuf, isem, gsem = rest
    i = pl.program_id(0)
    tm = COMBINE_TILE
    n_idx = TOP_K * tm
    last = n_tiles - 1

    def idx_copy(tile, s):
        return pltpu.make_async_copy(dest_hbm.at[tile], idx_s.at[pl.ds(s * n_idx, n_idx)], isem.at[s])

    def gather(s, unroll=4):
        _issue_row_copies(lambda r: idx_s[s * n_idx + r], n_idx,
                          lambda r, d: _row_copy(y_hbm, d, ybuf.at[s], r, gsem.at[s]), unroll)

    def gather_wait(s):
        pltpu.make_async_copy(y_hbm.at[pl.ds(0, n_idx * ROW_CHUNKS), :], ybuf.at[s], gsem.at[s]).wait()

    @pl.when(i == 0)
    def _():
        c = idx_copy(0, 0)
        c.start()
        c.wait()
        gather(0)
        idx_copy(jnp.minimum(1, last), 1).start()

    for half in range(2):
        tile = 2 * i + half
        s = half
        rows = slice(half * tm, (half + 1) * tm)
        idx_copy(jnp.minimum(tile + 1, last), 1 - s).wait()
        gather_wait(s)
        gather(1 - s, unroll=True)
        idx_copy(jnp.minimum(tile + 2, last), s).start()

        w = w_ref[rows, :]
        routed = jnp.zeros((tm, D_MODEL), F32)
        for k in range(TOP_K):
            routed = routed + w[:, k:k + 1] * _rowtile_load(ybuf.at[s], tm, base=k * tm * ROW_CHUNKS)
        out = x_ref[rows, :] + mod_ref[0, 5:6, :] * (routed + sh_ref[rows, :])
        if final:
            ms = jnp.mean(out * out, axis=-1, keepdims=True)
            y = out * lax.rsqrt(ms + EPS) * gf_ref[...]
            oc_ref[rows, :] = y
            ol_ref[rows, :] = y
        else:
            o_ref[rows, :] = out

    @pl.when(i == n_tiles // 2 - 1)
    def _():
        gather_wait(0)
        idx_copy(last, 1).wait()


def moe_combine(lay, x, shared, y_rows, dest, wts_t, mods, g_final=None):
    tm = COMBINE_TILE
    n_tiles = lay.t // tm
    assert n_tiles % 2 == 0
    step = 2 * tm
    nct = lay.t_ctx // step
    mi = _mod_index(lay, step)
    row = lambda i: (i, 0)
    final = g_final is not None
    if final:
        extra_in = [pl.BlockSpec((1, D_MODEL), lambda i: (0, 0))]
        extra_args = [g_final.reshape(1, -1)]
        out_specs = [pl.BlockSpec((step, D_MODEL), lambda i: (jnp.minimum(i, nct), 0)),
                     pl.BlockSpec((step, D_MODEL), lambda i: (jnp.maximum(i - nct, 0), 0))]
        out_shape = [jax.ShapeDtypeStruct((lay.t_ctx + step, D_MODEL), F32),
                     jax.ShapeDtypeStruct((lay.t_lat, D_MODEL), F32)]
    else:
        extra_in, extra_args = [], []
        out_specs = pl.BlockSpec((step, D_MODEL), row)
        out_shape = jax.ShapeDtypeStruct((lay.t, D_MODEL), F32)
    return pl.pallas_call(
        functools.partial(_combine_kernel, n_tiles=n_tiles, final=final),
        grid=(n_tiles // 2,),
        in_specs=[pl.BlockSpec(memory_space=pl.ANY),
                  pl.BlockSpec(memory_space=pl.ANY),
                  pl.BlockSpec((step, D_MODEL), row),
                  pl.BlockSpec((step, D_MODEL), row),
                  pl.BlockSpec((step, TOP_K), row),
                  pl.BlockSpec((1, N_MOD, D_MODEL), lambda i: (mi(i), 0, 0))] + extra_in,
        out_specs=out_specs,
        out_shape=out_shape,
        scratch_shapes=[pltpu.SMEM((2 * TOP_K * tm,), I32),
                        pltpu.VMEM((2, TOP_K * tm * ROW_CHUNKS, LANES), PACKED),
                        pltpu.SemaphoreType.DMA((2,)),
                        pltpu.SemaphoreType.DMA((2,))],
        compiler_params=_cparams(("arbitrary",)),
        name="moe_combine",
    )(dest, y_rows, x, shared, wts_t, mods, *extra_args)


def moe_layer(lay, x, g, mods, router_w, router_b, layer, w_gate, w_up, w_down, s_gate, s_up, s_down,
              g_final=None):
    t = lay.t
    hn, shared, eidx, wts, rank, cnt = moe_router(
        lay, x, g, mods, router_w, router_b, s_gate.astype(BF16), s_up.astype(BF16), s_down.astype(BF16))
    counts = cnt[:, 0].astype(I32)
    n_blocks = (counts + MOE_BLOCK - 1) // MOE_BLOCK
    padded = n_blocks * MOE_BLOCK
    pad_end = jnp.cumsum(padded)
    pad_start = pad_end - padded
    n_rows = -(-(t * TOP_K + N_EXPERTS * (MOE_BLOCK - 1)) // MOE_BLOCK) * MOE_BLOCK
    dest = moe_dest(pad_start, eidx, rank)
    last_row = jnp.where(n_blocks > 0, pad_end - MOE_BLOCK, -1)
    tail_blk = pad_end[-1] // MOE_BLOCK + jnp.arange(n_rows // MOE_BLOCK - t * TOP_K // MOE_BLOCK)
    tail_row = jnp.where(tail_blk < n_rows // MOE_BLOCK, tail_blk * MOE_BLOCK, -1).astype(I32)
    xs = moe_dispatch(lay, hn, _tile_major(dest, DISPATCH_TILE),
                      jnp.concatenate([last_row, tail_row]), n_rows)
    y_rows = moe_experts(xs, pad_start // MOE_BLOCK, n_blocks, tail_row, layer, w_gate, w_up, w_down)
    return moe_combine(lay, x, shared, y_rows, _tile_major(dest, COMBINE_TILE), wts.T, mods, g_final)


def _block_diag(w):
    nb, bw, _ = w.shape
    eye = jnp.eye(nb, dtype=w.dtype)
    return (eye[:, None, :, None] * w[:, :, None, :]).reshape(nb * bw, nb * bw)


def even_layer(lay, x, mods, g_mix, p, state_lru, state_ssm_re, state_ssm_im):
    t = lay.t
    proj, u_g = modnorm_matmul(lay, x, g_mix, mods, 0, p['w_in'].astype(BF16),
                               ug_col=2 * D_LRU)
    zeros_c = jnp.zeros((lay.n_ctx, D_LRU), F32)
    hf_y, st = None, []
    for d in range(2):
        wg = jnp.concatenate([_block_diag(p['lru_wa'][d]), _block_diag(p['lru_wx'][d])], axis=1)
        bg = jnp.concatenate([p['lru_ba'][d], p['lru_bx'][d]])
        h0 = jnp.concatenate([zeros_c, state_lru[:, d].astype(F32)], axis=0)
        hf_y, s = lru_pass(lay, proj, p['conv_w'], p['conv_b'], wg.astype(BF16), bg,
                           p['lru_lam'][d], h0, reverse=(d == 1), hf=hf_y)
        st.append(s[:lay.n_ctx])
    y_a = hf_y
    new_lru = jnp.stack(st, axis=1)

    mats = _s5_matrices(p['a_re'], p['a_im'], p['log_dt'], p['b_re'], p['b_im'], p['c_re'], p['c_im'])
    h0 = jnp.concatenate([state_ssm_re, state_ssm_im], axis=-1).astype(F32)
    h0 = h0.transpose(2, 1, 0, 3)
    y_g, h_ctx = s5_mixer(lay, u_g, mats, h0)
    seg = lay.s_ctx // S5_CHUNK
    h_ctx = h_ctx.reshape(SSM_GROUPS, 2, lay.n_ctx, seg, 2 * SSM_STATE)
    ends = jnp.stack([h_ctx[:, 0, :, seg - 1], h_ctx[:, 1, :, 0]], axis=1)
    ends = ends.transpose(2, 1, 0, 3)
    x = even_out(lay, x, y_a, y_g, proj, p['d'], p['glu_w'].astype(BF16), p['glu_b'],
                 p['w_out'].astype(BF16), mods)
    return x, new_lru, ends[..., :SSM_STATE], ends[..., SSM_STATE:]


def odd_layer(lay, x, mods, g_mix, w_qkv, sink, w_out, cache_k, cache_v):
    qkv = modnorm_matmul(lay, x, g_mix, mods, 0, w_qkv.astype(BF16))
    o_ctx = attn_context(lay, qkv, sink)
    o_lat = attn_latent(lay, qkv, cache_k, cache_v, sink)
    nq = N_HEADS * HEAD_DIM
    kv = qkv[:lay.t_ctx, nq:].reshape(lay.n_ctx, lay.s_ctx, 2, N_KV, HEAD_DIM)
    k_new = kv[:, :, 0].swapaxes(1, 2)
    v_new = kv[:, :, 1].swapaxes(1, 2)
    x = matmul_residual(lay, x, o_ctx, o_lat, w_out.astype(BF16), mods)
    return x, k_new, v_new


def _forward(lay, x_prompt, x_sample, state_lru, state_ssm_re, state_ssm_im, cache_k, cache_v,
             c, c_ctx, g_mix, g_ffn, w_mod, b_mod,
             ev_w_in, lru_conv_w, lru_conv_b, lru_wa, lru_ba, lru_wx, lru_bx, lru_lam,
             ssm_a_re, ssm_a_im, ssm_log_dt, ssm_b_re, ssm_b_im, ssm_c_re, ssm_c_im, ssm_d,
             ssm_glu_w, ssm_glu_b, ev_w_out, at_w_qkv, at_sink, at_w_out,
             router_w, router_b, exp_w_gate, exp_w_up, exp_w_down, sh_w_gate, sh_w_up, sh_w_down,
             g_final):
    depth = g_mix.shape[0]
    x = jnp.concatenate([x_prompt.reshape(lay.t_ctx, D_MODEL), x_sample.reshape(lay.t_lat, D_MODEL)],
                        axis=0)
    n_c = 1 + lay.n_lat
    c_rows = jnp.concatenate([c_ctx[None, :], c, jnp.zeros((16 - n_c, D_MODEL), F32)], axis=0)
    new_lru, new_re, new_im, new_k, new_v = [], [], [], [], []
    for l in range(depth):
        i = l // 2
        mods = adaln_table(c_rows, l, w_mod, b_mod[l])
        if l % 2 == 0:
            p = dict(w_in=ev_w_in[i], conv_w=lru_conv_w[i], conv_b=lru_conv_b[i],
                     lru_wa=lru_wa[i], lru_ba=lru_ba[i], lru_wx=lru_wx[i], lru_bx=lru_bx[i],
                     lru_lam=lru_lam[i], a_re=ssm_a_re[i], a_im=ssm_a_im[i], log_dt=ssm_log_dt[i],
                     b_re=ssm_b_re[i], b_im=ssm_b_im[i], c_re=ssm_c_re[i], c_im=ssm_c_im[i],
                     d=ssm_d[i], glu_w=ssm_glu_w[i], glu_b=ssm_glu_b[i], w_out=ev_w_out[i])
            x, lru_i, re_i, im_i = even_layer(lay, x, mods, g_mix[l], p, state_lru[:, i],
                                              state_ssm_re[:, i], state_ssm_im[:, i])
            new_lru.append(lru_i)
            new_re.append(re_i)
            new_im.append(im_i)
        else:
            x, k_i, v_i = odd_layer(lay, x, mods, g_mix[l], at_w_qkv[i], at_sink[i], at_w_out[i],
                                    cache_k[:, i], cache_v[:, i])
            new_k.append(k_i)
            new_v.append(v_i)
        x = moe_layer(lay, x, g_ffn[l], mods, router_w[l], router_b[l], l, exp_w_gate, exp_w_up,
                      exp_w_down, sh_w_gate[l], sh_w_up[l], sh_w_down[l],
                      g_final=g_final if l == depth - 1 else None)
    y_ctx, y_lat = x
    y_prompt = y_ctx[:lay.t_ctx].reshape(x_prompt.shape)
    y_sample = y_lat.reshape(x_sample.shape)
    return (y_prompt, y_sample, jnp.stack(new_lru, axis=1), jnp.stack(new_re, axis=1),
            jnp.stack(new_im, axis=1), jnp.stack(new_k, axis=1), jnp.stack(new_v, axis=1))


def kernel(x_prompt, x_sample, state_lru, state_ssm_re, state_ssm_im, cache_k, cache_v, c, c_ctx, g_mix, g_ffn, w_mod, b_mod, ev_w_in, lru_conv_w, lru_conv_b, lru_wa, lru_ba, lru_wx, lru_bx, lru_lam, ssm_a_re, ssm_a_im, ssm_log_dt, ssm_b_re, ssm_b_im, ssm_c_re, ssm_c_im, ssm_d, ssm_glu_w, ssm_glu_b, ev_w_out, at_w_qkv, at_sink, at_w_out, router_w, router_b, exp_w_gate, exp_w_up, exp_w_down, sh_w_gate, sh_w_up, sh_w_down, g_final):
    lay = Layout(n_ctx=x_prompt.shape[0], s_ctx=x_prompt.shape[1],
                 n_lat=x_sample.shape[0], s_lat=x_sample.shape[1])
    return _forward(lay, x_prompt, x_sample, state_lru, state_ssm_re, state_ssm_im, cache_k, cache_v,
                    c, c_ctx, g_mix, g_ffn, w_mod, b_mod,
                    ev_w_in, lru_conv_w, lru_conv_b, lru_wa, lru_ba, lru_wx, lru_bx, lru_lam,
                    ssm_a_re, ssm_a_im, ssm_log_dt, ssm_b_re, ssm_b_im, ssm_c_re, ssm_c_im, ssm_d,
                    ssm_glu_w, ssm_glu_b, ev_w_out, at_w_qkv, at_sink, at_w_out,
                    router_w, router_b, exp_w_gate, exp_w_up, exp_w_down, sh_w_gate, sh_w_up,
                    sh_w_down, g_final)
```

```python
import functools
from typing import NamedTuple

import jax
import jax.numpy as jnp
from jax import lax
from jax.experimental import pallas as pl
from jax.experimental.pallas import tpu as pltpu

F32 = jnp.float32
BF16 = jnp.bfloat16
I32 = jnp.int32
HIGHEST = lax.Precision.HIGHEST

D_MODEL = 1024
EPS = 1e-6
N_MOD = 6
GRID_W = 64
D_LRU = 512
LRU_BLOCKS = 8
LRU_C = 8.0
CONV_W = 4
CONV_LEFT = 2
D_SSM = 512
SSM_GROUP = 16
SSM_GROUPS = 32
SSM_STATE = 64
S5_CHUNK = 16
S5_LANES = S5_CHUNK * SSM_GROUP
S5_SCAN_STEPS = 8
HEAD_DIM = 64
N_HEADS = 16
N_KV = 4
GQA = 4
WINDOW = 128
Q_BLOCK = 128
ROPE_BASE = 10000.0
ATTN_SCALE = HEAD_DIM ** -0.5
NEG_INF = -1e30
N_EXPERTS = 256
TOP_K = 8
N_GROUPS = 8
TOPK_GROUPS = 4
GROUP_SIZE = N_EXPERTS // N_GROUPS
D_EXPERT = 256
ROUTE_SCALE = 2.5
MOE_BLOCK = 128

SUBLANES = 8
LANES = 128
ROW_CHUNKS = D_MODEL // (2 * LANES)
PACKED = jnp.int32
SEQ_TILE = 256
ROW_TILE = 512
ROUTER_TILE = 512
DEST_TILE = 1024
DISPATCH_TILE = 512
COMBINE_TILE = 256
EXPERT_X_BUFS = 8
EXPERT_Y_BUFS = 8
VMEM_LIMIT = 56 * 1024 * 1024


class Layout(NamedTuple):
    n_ctx: int
    s_ctx: int
    n_lat: int
    s_lat: int

    @property
    def t_ctx(self):
        return self.n_ctx * self.s_ctx

    @property
    def t_lat(self):
        return self.n_lat * self.s_lat

    @property
    def t(self):
        return self.t_ctx + self.t_lat

    @property
    def n_seq(self):
        return self.n_ctx + self.n_lat


def _cparams(sem):
    return pltpu.CompilerParams(dimension_semantics=sem, vmem_limit_bytes=VMEM_LIMIT)


def _mod_index(lay, tile_rows):
    n_ctx_tiles = lay.t_ctx // tile_rows
    per_lat = lay.s_lat // tile_rows

    def f(i):
        return jnp.where(i < n_ctx_tiles, 0, 1 + (i - n_ctx_tiles) // per_lat)
    return f


def _adaln_kernel(c_ref, w_ref, b_ref, o_ref):
    c = c_ref[...]
    s = c * jax.nn.sigmoid(c)
    o_ref[...] = jnp.dot(s, w_ref[0], precision=HIGHEST, preferred_element_type=F32) + b_ref[...]


def adaln_table(c_rows, layer, w_mod, b_mod):
    n = c_rows.shape[0]
    tn = 1536
    out = pl.pallas_call(
        _adaln_kernel,
        grid=(N_MOD * D_MODEL // tn,),
        in_specs=[pl.BlockSpec((n, D_MODEL), lambda j: (0, 0)),
                  pl.BlockSpec((1, D_MODEL, tn), lambda j: (layer, 0, j)),
                  pl.BlockSpec((1, tn), lambda j: (0, j))],
        out_specs=pl.BlockSpec((n, tn), lambda j: (0, j)),
        out_shape=jax.ShapeDtypeStruct((n, N_MOD * D_MODEL), F32),
        compiler_params=_cparams(("arbitrary",)),
        name="adaln",
    )(c_rows, w_mod, b_mod.reshape(1, -1))
    return out.reshape(n, N_MOD, D_MODEL)


def _modnorm(x, g, mod_ref, slot):
    ms = jnp.mean(x * x, axis=-1, keepdims=True)
    y = x * lax.rsqrt(ms + EPS) * g
    shift = mod_ref[0, slot:slot + 1, :]
    scale = mod_ref[0, slot + 1:slot + 2, :]
    return y * (1.0 + scale) + shift


GROUPS_PER_VREG = LANES // SSM_GROUP
SSM_COL_BLOCKS = D_SSM // LANES


def _group_major_store(val, tmp_ref, dst_ref):
    rows = dst_ref.shape[1]
    for j in range(SSM_COL_BLOCKS):
        tmp_ref[j] = val[:, j * LANES:(j + 1) * LANES]
    for j in range(SSM_COL_BLOCKS):
        steps = [tmp_ref[j, pl.ds(i, rows, stride=S5_CHUNK), :] for i in range(S5_CHUNK)]
        for q in range(GROUPS_PER_VREG):
            dst_ref[j * GROUPS_PER_VREG + q] = jnp.concatenate(
                [w[:, q * SSM_GROUP:(q + 1) * SSM_GROUP] for w in steps], axis=1)


def _group_major_load(src_ref, tmp_ref):
    rows = src_ref.shape[1]
    for j in range(SSM_COL_BLOCKS):
        blocks = [src_ref[j * GROUPS_PER_VREG + q] for q in range(GROUPS_PER_VREG)]
        for i in range(S5_CHUNK):
            tmp_ref[j, pl.ds(i, rows, stride=S5_CHUNK), :] = jnp.concatenate(
                [b[:, i * SSM_GROUP:(i + 1) * SSM_GROUP] for b in blocks], axis=1)
    return jnp.concatenate([tmp_ref[j] for j in range(SSM_COL_BLOCKS)], axis=1)


def _modnorm_mm_kernel(x_ref, g_ref, mod_ref, w_ref, o_ref, *ug_refs, slot, ug_col):
    h = _modnorm(x_ref[...], g_ref[...], mod_ref, slot)
    out = jnp.dot(h.astype(BF16), w_ref[...], preferred_element_type=F32)
    o_ref[...] = out
    if ug_col is not None:
        ug_ref, tmp_ref = ug_refs
        _group_major_store(out[:, ug_col:ug_col + D_SSM], tmp_ref, ug_ref)


def modnorm_matmul(lay, x, g, mods, slot, w_bf16, ug_col=None):
    t = lay.t
    n = w_bf16.shape[1]
    mi = _mod_index(lay, ROW_TILE)
    out_specs = [pl.BlockSpec((ROW_TILE, n), lambda i: (i, 0))]
    out_shape = [jax.ShapeDtypeStruct((t, n), F32)]
    if ug_col is not None:
        out_specs.append(pl.BlockSpec((SSM_GROUPS, ROW_TILE // S5_CHUNK, S5_LANES), lambda i: (0, i, 0)))
        out_shape.append(jax.ShapeDtypeStruct((SSM_GROUPS, t // S5_CHUNK, S5_LANES), F32))
    outs = pl.pallas_call(
        functools.partial(_modnorm_mm_kernel, slot=slot, ug_col=ug_col),
        grid=(t // ROW_TILE,),
        in_specs=[pl.BlockSpec((ROW_TILE, D_MODEL), lambda i: (i, 0)),
                  pl.BlockSpec((1, D_MODEL), lambda i: (0, 0)),
                  pl.BlockSpec((1, N_MOD, D_MODEL), lambda i: (mi(i), 0, 0)),
                  pl.BlockSpec((D_MODEL, n), lambda i: (0, 0))],
        out_specs=out_specs,
        out_shape=out_shape,
        scratch_shapes=([pltpu.VMEM((SSM_COL_BLOCKS, ROW_TILE, LANES), F32)]
                        if ug_col is not None else []),
        compiler_params=_cparams(("arbitrary",)),
        name="modnorm_matmul",
    )(x, g.reshape(1, -1), mods, w_bf16)
    return outs if ug_col is not None else outs[0]


def _seq_tile_maps(lay, reverse):
    assert lay.s_ctx == SEQ_TILE and lay.s_lat % SEQ_TILE == 0
    n_tiles = lay.t // SEQ_TILE
    per_lat = lay.s_lat // SEQ_TILE

    def tile(i):
        return (n_tiles - 1 - i) if reverse else i

    def seq(i):
        ti = tile(i)
        return jnp.where(ti < lay.n_ctx, ti, lay.n_ctx + (ti - lay.n_ctx) // per_lat)

    return n_tiles, tile, seq


def _softplus(x):
    return jnp.maximum(x, 0.0) + jnp.log(1.0 + jnp.exp(-jnp.abs(x)))


def _lru_kernel(rec_ref, prev_ref, next_ref, cw_ref, cb_ref, wg_ref, bg_ref, lam_ref, h0_ref,
                *rest, reverse, n_ctx, per_lat, n_tiles):
    if reverse:
        gate_ref, hf_ref, y_ref, st_ref, a_s, b_s, h_s, carry = rest
    else:
        y_ref, st_ref, a_s, b_s, h_s, carry = rest
    i = pl.program_id(0)
    ti = (n_tiles - 1 - i) if reverse else i
    is_first = jnp.logical_or(ti < n_ctx, (ti - n_ctx) % per_lat == 0)
    is_last = jnp.logical_or(ti < n_ctx, (ti - n_ctx) % per_lat == per_lat - 1)
    ts = SEQ_TILE

    rec = rec_ref[...]
    prev = jnp.where(is_first, 0.0, prev_ref[...])
    nxt = jnp.where(is_last, 0.0, next_ref[...])
    ext = jnp.concatenate([prev, rec, nxt], axis=0)
    n_ext = ts + 2 * SUBLANES
    cw = cw_ref[...]
    xc = cb_ref[...] + cw[2:3, :] * rec
    xc = xc + cw[0:1, :] * pltpu.roll(ext, 2, 0)[SUBLANES:SUBLANES + ts]
    xc = xc + cw[1:2, :] * pltpu.roll(ext, 1, 0)[SUBLANES:SUBLANES + ts]
    xc = xc + cw[3:4, :] * pltpu.roll(ext, n_ext - 1, 0)[SUBLANES:SUBLANES + ts]

    gates = jax.nn.sigmoid(jnp.dot(xc.astype(BF16), wg_ref[...], preferred_element_type=F32)
                           + bg_ref[...])
    r = gates[:, :D_LRU]
    ig = gates[:, D_LRU:]
    log_a = (-LRU_C) * r * _softplus(-lam_ref[...])
    a = jnp.exp(log_a)
    b = jnp.sqrt(1.0 - jnp.exp(2.0 * log_a)) * (ig * xc)

    row8 = lax.broadcasted_iota(I32, (ts, D_LRU), 0) % SUBLANES
    for sh in (1, 2, 4):
        if reverse:
            keep = row8 < SUBLANES - sh
            a_sh = pltpu.roll(a, ts - sh, 0)
            b_sh = pltpu.roll(b, ts - sh, 0)
        else:
            keep = row8 >= sh
            a_sh = pltpu.roll(a, sh, 0)
            b_sh = pltpu.roll(b, sh, 0)
        b = b + a * jnp.where(keep, b_sh, 0.0)
        a = a * jnp.where(keep, a_sh, 1.0)
    a_s[...] = a
    b_s[...] = b

    @pl.when(is_last if reverse else is_first)
    def _():
        carry[...] = h0_ref[0]

    n_grp = ts // SUBLANES

    def body(k, c):
        gi = (n_grp - 1 - k) if reverse else k
        sl = pl.ds(pl.multiple_of(gi * SUBLANES, SUBLANES), SUBLANES)
        h = b_s[sl, :] + a_s[sl, :] * c
        h_s[sl, :] = h
        return h[0:1, :] if reverse else h[SUBLANES - 1:SUBLANES, :]

    c_fin = lax.fori_loop(0, n_grp, body, carry[...], unroll=4)
    carry[...] = c_fin
    st_ref[0] = c_fin
    if reverse:
        y_ref[...] = (hf_ref[...] + h_s[...]) * jax.nn.gelu(gate_ref[...])
    else:
        y_ref[...] = h_s[...]


def lru_pass(lay, proj, conv_w, conv_b, wg_bf16, bg, lam, h0, reverse, hf=None):
    n_tiles, tile, seq = _seq_tile_maps(lay, reverse)
    per_lat = lay.s_lat // SEQ_TILE
    blk8 = SEQ_TILE // SUBLANES
    last8 = lay.t // SUBLANES - 1
    c = D_LRU
    in_specs = [
        pl.BlockSpec((SEQ_TILE, c), lambda i: (tile(i), 1)),
        pl.BlockSpec((SUBLANES, c), lambda i: (jnp.maximum(tile(i) * blk8 - 1, 0), 1)),
        pl.BlockSpec((SUBLANES, c), lambda i: (jnp.minimum(tile(i) * blk8 + blk8, last8), 1)),
        pl.BlockSpec((CONV_W, c), lambda i: (0, 0)),
        pl.BlockSpec((1, c), lambda i: (0, 0)),
        pl.BlockSpec((c, 2 * c), lambda i: (0, 0)),
        pl.BlockSpec((1, 2 * c), lambda i: (0, 0)),
        pl.BlockSpec((1, c), lambda i: (0, 0)),
        pl.BlockSpec((1, 1, c), lambda i: (seq(i), 0, 0)),
    ]
    args = [proj, proj, proj, conv_w, conv_b.reshape(1, -1), wg_bf16, bg.reshape(1, -1),
            lam.reshape(1, -1), h0.reshape(lay.n_seq, 1, c)]
    if reverse:
        in_specs += [pl.BlockSpec((SEQ_TILE, c), lambda i: (tile(i), 0)),
                     pl.BlockSpec((SEQ_TILE, c), lambda i: (tile(i), 0))]
        args += [proj, hf]
    y, st = pl.pallas_call(
        functools.partial(_lru_kernel, reverse=reverse, n_ctx=lay.n_ctx, per_lat=per_lat,
                          n_tiles=n_tiles),
        grid=(n_tiles,),
        in_specs=in_specs,
        out_specs=[pl.BlockSpec((SEQ_TILE, c), lambda i: (tile(i), 0)),
                   pl.BlockSpec((1, 1, c), lambda i: (seq(i), 0, 0))],
        out_shape=[jax.ShapeDtypeStruct((lay.t, c), F32),
                   jax.ShapeDtypeStruct((lay.n_seq, 1, c), F32)],
        scratch_shapes=[pltpu.VMEM((SEQ_TILE, c), F32), pltpu.VMEM((SEQ_TILE, c), F32),
                        pltpu.VMEM((SEQ_TILE, c), F32), pltpu.VMEM((1, c), F32)],
        compiler_params=_cparams(("arbitrary",)),
        name="lru_bwd" if reverse else "lru_fwd",
    )(*args)
    return y, st.reshape(lay.n_seq, c)


def _cmul(a, b):
    return a[0] * b[0] - a[1] * b[1], a[0] * b[1] + a[1] * b[0]


def _s5_matrices(a_re, a_im, log_dt, b_re, b_im, c_re, c_im):
    a_re, a_im = a_re.astype(F32), a_im.astype(F32)
    dt = jnp.exp(log_dt.astype(F32))[..., None]
    z = (a_re * dt, a_im * dt)

    def zpow(k):
        k = k.reshape((-1,) + (1,) * z[0].ndim)
        mag = jnp.exp(k * z[0][None])
        return mag * jnp.cos(k * z[1][None]), mag * jnp.sin(k * z[1][None])

    a_bar = zpow(jnp.ones((1,), F32))
    a_bar = (a_bar[0][0], a_bar[1][0])
    den = a_re * a_re + a_im * a_im
    xr, xi = a_bar[0] - 1.0, a_bar[1]
    q = ((xr * a_re + xi * a_im) / den, (xi * a_re - xr * a_im) / den)
    b_bar = _cmul((q[0][..., None], q[1][..., None]), (b_re.astype(F32), b_im.astype(F32)))
    cc = (c_re.astype(F32), c_im.astype(F32))
    el = S5_CHUNK
    pw = zpow(jnp.arange(el + 1, dtype=F32))
    idx = jnp.arange(el)
    m_in, m_toep, m_out = [], [], []
    for d in range(2):
        p_d = (pw[0][:, d], pw[1][:, d])
        b_d = (b_bar[0][d], b_bar[1][d])
        c_d = (cc[0][d], cc[1][d])
        k_in = (el - 1 - idx) if d == 0 else idx
        w_in = _cmul((p_d[0][k_in][..., None], p_d[1][k_in][..., None]),
                     (b_d[0][None], b_d[1][None]))
        w_in = [jnp.transpose(w, (1, 0, 3, 2)).reshape(SSM_GROUPS, S5_LANES, SSM_STATE) for w in w_in]
        m_in.append(jnp.concatenate(w_in, axis=-1))
        cp = _cmul((c_d[0][None], c_d[1][None]),
                   (p_d[0][:, :, None, :], p_d[1][:, :, None, :]))
        kern = (jnp.einsum('kghp,gpc->kgch', cp[0][:el], b_d[0])
                - jnp.einsum('kghp,gpc->kgch', cp[1][:el], b_d[1]))
        zero = jnp.zeros_like(kern[0])
        rows = []
        for i in range(el):
            if d == 0:
                pieces = [zero] * i + [kern[k] for k in range(el - i)]
            else:
                pieces = [kern[i - j] for j in range(i + 1)] + [zero] * (el - 1 - i)
            rows.append(jnp.concatenate(pieces, axis=-1))
        m_toep.append(jnp.stack(rows, axis=1).reshape(SSM_GROUPS, S5_LANES, S5_LANES))
        k_out = (idx + 1) if d == 0 else (el - idx)
        w_out = [jnp.transpose(w[k_out], (1, 3, 0, 2)).reshape(SSM_GROUPS, SSM_STATE, S5_LANES)
                 for w in cp]
        m_out.append(jnp.concatenate([w_out[0], -w_out[1]], axis=1))
    mul = zpow(el * 2.0 ** jnp.arange(S5_SCAN_STEPS, dtype=F32))
    mul = [jnp.transpose(m, (2, 1, 0, 3)) for m in mul]
    coef_a = jnp.concatenate([mul[0], mul[0]], axis=-1)
    coef_b = jnp.concatenate([-mul[1], mul[1]], axis=-1)
    stack = lambda xs: jnp.stack(xs, axis=1)
    return (stack(m_in).astype(BF16), stack(m_toep).astype(BF16), stack(m_out).astype(BF16),
            coef_a, coef_b)


def _s5_scan(v, ca, cb, seg, reverse):
    n = v.shape[0]
    assert seg <= 2 ** S5_SCAN_STEPS
    row = lax.broadcasted_iota(I32, (n, 2 * SSM_STATE), 0) % seg
    k, sh = 0, 1
    while sh < seg:
        if reverse:
            s = jnp.where(row < seg - sh, pltpu.roll(v, n - sh, 0), 0.0)
        else:
            s = jnp.where(row >= sh, pltpu.roll(v, sh, 0), 0.0)
        v = v + ca[k:k + 1, :] * s + cb[k:k + 1, :] * pltpu.roll(s, SSM_STATE, 1)
        k += 1
        sh *= 2
    return v


def _s5_shift(h, seg, reverse):
    n = h.shape[0]
    row = lax.broadcasted_iota(I32, (n, 2 * SSM_STATE), 0) % seg
    if reverse:
        return jnp.where(row < seg - 1, pltpu.roll(h, n - 1, 0), 0.0)
    return jnp.where(row >= 1, pltpu.roll(h, 1, 0), 0.0)


def _s5_kernel(u_ref, min_ref, mtoep_ref, mout_ref, ca_ref, cb_ref, h0_ref, y_ref, hc_ref,
               v_s, hp_s, *, rc, seg_c, n_lat, seg_l):
    u = u_ref[0].astype(BF16)
    u_c, u_l = u[:rc], u[rc:]
    y_c = jnp.zeros((rc, S5_LANES), F32)
    y_l = jnp.zeros((n_lat * seg_l, S5_LANES), F32)
    for d in range(2):
        reverse = d == 1
        ca = ca_ref[0, d]
        cb = cb_ref[0, d]
        m_in = min_ref[0, d]
        m_toep = mtoep_ref[0, d]
        m_out = mout_ref[0, d]
        h_c = _s5_scan(jnp.dot(u_c, m_in, preferred_element_type=F32), ca, cb, seg_c, reverse)
        hc_ref[0, d] = h_c
        hp_c = _s5_shift(h_c, seg_c, reverse)
        y_c = y_c + jnp.dot(u_c, m_toep, preferred_element_type=F32)
        y_c = y_c + jnp.dot(hp_c.astype(BF16), m_out, preferred_element_type=F32)
        v_s[...] = jnp.dot(u_l, m_in, preferred_element_type=F32)
        for s in range(n_lat):
            h0 = h0_ref[0, d, s:s + 1, :]
            r0 = s * seg_l + (seg_l - 1 if reverse else 0)
            v_s[r0:r0 + 1, :] = (v_s[r0:r0 + 1, :] + ca[0:1, :] * h0
                                 + cb[0:1, :] * pltpu.roll(h0, SSM_STATE, 1))
        h_l = _s5_scan(v_s[...], ca, cb, seg_l, reverse)
        hp_s[...] = _s5_shift(h_l, seg_l, reverse)
        for s in range(n_lat):
            r0 = s * seg_l + (seg_l - 1 if reverse else 0)
            hp_s[r0:r0 + 1, :] = h0_ref[0, d, s:s + 1, :]
        y_l = y_l + jnp.dot(u_l, m_toep, preferred_element_type=F32)
        y_l = y_l + jnp.dot(hp_s[...].astype(BF16), m_out, preferred_element_type=F32)
    y_ref[0, :rc, :] = y_c
    y_ref[0, rc:, :] = y_l


def s5_mixer(lay, u_g, mats, h0):
    m_in, m_toep, m_out, coef_a, coef_b = mats
    rows = lay.t // S5_CHUNK
    rc = lay.t_ctx // S5_CHUNK
    rl = rows - rc
    st2 = 2 * SSM_STATE
    g4 = lambda g: (g, 0, 0, 0)
    return pl.pallas_call(
        functools.partial(_s5_kernel, rc=rc, seg_c=lay.s_ctx // S5_CHUNK, n_lat=lay.n_lat,
                          seg_l=lay.s_lat // S5_CHUNK),
        grid=(SSM_GROUPS,),
        in_specs=[pl.BlockSpec((1, rows, S5_LANES), lambda g: (g, 0, 0)),
                  pl.BlockSpec((1, 2, S5_LANES, st2), g4),
                  pl.BlockSpec((1, 2, S5_LANES, S5_LANES), g4),
                  pl.BlockSpec((1, 2, st2, S5_LANES), g4),
                  pl.BlockSpec((1, 2, S5_SCAN_STEPS, st2), g4),
                  pl.BlockSpec((1, 2, S5_SCAN_STEPS, st2), g4),
                  pl.BlockSpec((1, 2, lay.n_lat, st2), g4)],
        out_specs=[pl.BlockSpec((1, rows, S5_LANES), lambda g: (g, 0, 0)),
                   pl.BlockSpec((1, 2, rc, st2), g4)],
        out_shape=[jax.ShapeDtypeStruct((SSM_GROUPS, rows, S5_LANES), F32),
                   jax.ShapeDtypeStruct((SSM_GROUPS, 2, rc, st2), F32)],
        scratch_shapes=[pltpu.VMEM((rl, st2), F32), pltpu.VMEM((rl, st2), F32)],
        compiler_params=_cparams(("arbitrary",)),
        name="s5_mixer",
    )(u_g, m_in, m_toep, m_out, coef_a, coef_b, h0)


def _even_out_kernel(x_ref, ya_ref, yg_ref, u_ref, d_ref, gw_ref, gb_ref, w_ref, mod_ref, o_ref, yt_s):
    ys = _group_major_load(yg_ref, yt_s) + d_ref[...] * u_ref[...]
    g = jax.nn.gelu(ys)
    yb = g * jax.nn.sigmoid(jnp.dot(g.astype(BF16), gw_ref[...], preferred_element_type=F32)
                            + gb_ref[...])
    out = jnp.dot(ya_ref[...].astype(BF16), w_ref[:D_LRU, :], preferred_element_type=F32)
    out = out + jnp.dot(yb.astype(BF16), w_ref[D_LRU:, :], preferred_element_type=F32)
    o_ref[...] = x_ref[...] + mod_ref[0, 2:3, :] * out


def even_out(lay, x, y_a, y_g, proj, ssm_d, glu_w_bf16, glu_b, w_out_bf16, mods):
    mi = _mod_index(lay, ROW_TILE)
    c = D_SSM
    row = lambda i: (i, 0)
    const = lambda i: (0, 0)
    return pl.pallas_call(
        _even_out_kernel,
        grid=(lay.t // ROW_TILE,),
        in_specs=[pl.BlockSpec((ROW_TILE, D_MODEL), row),
                  pl.BlockSpec((ROW_TILE, c), row),
                  pl.BlockSpec((SSM_GROUPS, ROW_TILE // S5_CHUNK, S5_LANES), lambda i: (0, i, 0)),
                  pl.BlockSpec((ROW_TILE, c), lambda i: (i, 2)),
                  pl.BlockSpec((1, c), const),
                  pl.BlockSpec((c, c), const),
                  pl.BlockSpec((1, c), const),
                  pl.BlockSpec((D_MODEL, D_MODEL), const),
                  pl.BlockSpec((1, N_MOD, D_MODEL), lambda i: (mi(i), 0, 0))],
        out_specs=pl.BlockSpec((ROW_TILE, D_MODEL), row),
        out_shape=jax.ShapeDtypeStruct((lay.t, D_MODEL), F32),
        scratch_shapes=[pltpu.VMEM((SSM_COL_BLOCKS, ROW_TILE, LANES), F32)],
        compiler_params=_cparams(("arbitrary",)),
        name="even_out",
    )(x, y_a, y_g, proj, ssm_d.reshape(1, -1), glu_w_bf16, glu_b.reshape(1, -1), w_out_bf16, mods)


def _softmax_pv(parts, sink_col):
    m = sink_col
    for s, _ in parts:
        m = jnp.maximum(m, jnp.max(s, axis=-1, keepdims=True))
    den = jnp.exp(sink_col - m)
    acc = None
    for s, v in parts:
        p = jnp.exp(s - m)
        den = den + jnp.sum(p, axis=-1, keepdims=True)
        pv = jnp.dot(p.astype(BF16), v.astype(BF16), preferred_element_type=F32)
        acc = pv if acc is None else acc + pv
    return acc / den


def _nt_dot(a, b):
    return lax.dot_general(a.astype(BF16), b.astype(BF16), (((1,), (1,)), ((), ())),
                           preferred_element_type=F32)


def _attn_ctx_kernel(q_ref, k_ref, v_ref, sink_ref, o_ref):
    n = q_ref.shape[0]
    for kh in range(N_KV):
        k = k_ref[:, kh * HEAD_DIM:(kh + 1) * HEAD_DIM]
        v = v_ref[:, kh * HEAD_DIM:(kh + 1) * HEAD_DIM]
        for g in range(GQA):
            h = kh * GQA + g
            q = q_ref[:, h * HEAD_DIM:(h + 1) * HEAD_DIM]
            s = _nt_dot(q, k) * ATTN_SCALE
            sink = jnp.broadcast_to(sink_ref[0:1, h:h + 1], (n, 1))
            o_ref[:, h * HEAD_DIM:(h + 1) * HEAD_DIM] = _softmax_pv([(s, v)], sink)


def attn_context(lay, qkv, sink):
    nq = N_HEADS * HEAD_DIM
    nkv = N_KV * HEAD_DIM
    return pl.pallas_call(
        _attn_ctx_kernel,
        grid=(lay.n_ctx,),
        in_specs=[pl.BlockSpec((lay.s_ctx, nq), lambda b: (b, 0)),
                  pl.BlockSpec((lay.s_ctx, nkv), lambda b: (b, nq // nkv)),
                  pl.BlockSpec((lay.s_ctx, nkv), lambda b: (b, nq // nkv + 1)),
                  pl.BlockSpec((1, N_HEADS), lambda b: (0, 0))],
        out_specs=pl.BlockSpec((lay.s_ctx, nq), lambda b: (b, 0)),
        out_shape=jax.ShapeDtypeStruct((lay.t_ctx, nq), F32),
        compiler_params=_cparams(("arbitrary",)),
        name="attn_context",
    )(qkv, qkv, qkv, sink.reshape(1, -1))


def _rope(x, cos, sin):
    lane = lax.broadcasted_iota(I32, (x.shape[0], 2 * HEAD_DIM), 1) % HEAD_DIM
    outs = []
    for j in range(x.shape[1] // (2 * HEAD_DIM)):
        xs = x[:, j * 2 * HEAD_DIM:(j + 1) * 2 * HEAD_DIM]
        sw = jnp.where(lane < HEAD_DIM // 2,
                       pltpu.roll(xs, 2 * HEAD_DIM - HEAD_DIM // 2, 1),
                       pltpu.roll(xs, HEAD_DIM // 2, 1))
        outs.append(xs * cos + sw * sin)
    return outs


def _attn_lat_kernel(q_ref, k0_ref, k1_ref, k2_ref, v0_ref, v1_ref, v2_ref, ck_ref, cv_ref,
                     cq_ref, sq_ref, c0_ref, c1_ref, c2_ref, s0_ref, s1_ref, s2_ref, sink_ref,
                     o_ref, *, n_blk):
    j = pl.program_id(1)
    qb = Q_BLOCK
    q_parts = [qp * ATTN_SCALE for qp in _rope(q_ref[...], cq_ref[...], sq_ref[...])]
    k_parts = [_rope(kr[...], cr[...], sr[...])
               for kr, cr, sr in ((k0_ref, c0_ref, s0_ref), (k1_ref, c1_ref, s1_ref),
                                  (k2_ref, c2_ref, s2_ref))]
    qi = lax.broadcasted_iota(I32, (qb, 3 * qb), 0)
    km = lax.broadcasted_iota(I32, (qb, 3 * qb), 1)
    kpos = j * qb - qb + km
    mask1 = (jnp.abs(km - qb - qi) <= WINDOW) & (kpos >= 0) & (kpos < n_blk * qb)
    mask = jnp.concatenate([mask1] * GQA, axis=0)
    for kh in range(N_KV):
        half = (kh % 2) * HEAD_DIM
        k_loc = jnp.concatenate([kp[kh // 2][:, half:half + HEAD_DIM] for kp in k_parts], axis=0)
        v_loc = jnp.concatenate([vr[:, kh * HEAD_DIM:(kh + 1) * HEAD_DIM]
                                 for vr in (v0_ref, v1_ref, v2_ref)], axis=0)
        qs, sinks = [], []
        for g in range(GQA):
            h = kh * GQA + g
            qs.append(q_parts[h // 2][:, (h % 2) * HEAD_DIM:(h % 2 + 1) * HEAD_DIM])
            sinks.append(jnp.broadcast_to(sink_ref[0:1, h:h + 1], (qb, 1)))
        q = jnp.concatenate(qs, axis=0)
        sink = jnp.concatenate(sinks, axis=0)
        s_loc = jnp.where(mask, _nt_dot(q, k_loc), NEG_INF)
        s_ctx = _nt_dot(q, ck_ref[0, kh])
        o = _softmax_pv([(s_loc, v_loc), (s_ctx, cv_ref[0, kh])], sink)
        for g in range(GQA):
            h = kh * GQA + g
            o_ref[:, h * HEAD_DIM:(h + 1) * HEAD_DIM] = o[g * qb:(g + 1) * qb]


def _rope_tables(s_len):
    rows = s_len // GRID_W
    row = jnp.repeat(jnp.arange(rows), GRID_W).astype(F32)
    col = jnp.tile(jnp.arange(GRID_W), rows).astype(F32)
    nf = HEAD_DIM // 4
    inv = ROPE_BASE ** (-jnp.arange(nf, dtype=F32) / nf)
    ang = jnp.concatenate([row[:, None] * inv, col[:, None] * inv], axis=-1)
    cos, sin = jnp.cos(ang), jnp.sin(ang)
    cos2 = jnp.tile(jnp.concatenate([cos, cos], axis=-1), (1, 2))
    sin2 = jnp.tile(jnp.concatenate([-sin, sin], axis=-1), (1, 2))
    return cos2, sin2


def attn_latent(lay, qkv, cache_k, cache_v, sink):
    nq = N_HEADS * HEAD_DIM
    nkv = N_KV * HEAD_DIM
    n_blk = lay.s_lat // Q_BLOCK
    base = lay.t_ctx // Q_BLOCK
    n_ctx_keys = cache_k.shape[2]
    cos2, sin2 = _rope_tables(lay.s_lat)
    kcol = nq // nkv

    def qrow(b, j):
        return base + b * n_blk + j

    def krow(off):
        return lambda b, j: base + b * n_blk + jnp.clip(j + off, 0, n_blk - 1)

    def trow(off):
        return lambda b, j: (jnp.clip(j + off, 0, n_blk - 1), 0)

    kv_spec = lambda off, col: pl.BlockSpec((Q_BLOCK, nkv), lambda b, j: (krow(off)(b, j), col))
    tab = lambda off: pl.BlockSpec((Q_BLOCK, 2 * HEAD_DIM), trow(off))
    cache_spec = pl.BlockSpec((1, N_KV, n_ctx_keys, HEAD_DIM), lambda b, j: (b, 0, 0, 0))
    return pl.pallas_call(
        functools.partial(_attn_lat_kernel, n_blk=n_blk),
        grid=(lay.n_lat, n_blk),
        in_specs=[pl.BlockSpec((Q_BLOCK, nq), lambda b, j: (qrow(b, j), 0)),
                  kv_spec(-1, kcol), kv_spec(0, kcol), kv_spec(1, kcol),
                  kv_spec(-1, kcol + 1), kv_spec(0, kcol + 1), kv_spec(1, kcol + 1),
                  cache_spec, cache_spec,
                  tab(0), tab(0), tab(-1), tab(0), tab(1), tab(-1), tab(0), tab(1),
                  pl.BlockSpec((1, N_HEADS), lambda b, j: (0, 0))],
        out_specs=pl.BlockSpec((Q_BLOCK, nq), lambda b, j: (b * n_blk + j, 0)),
        out_shape=jax.ShapeDtypeStruct((lay.t_lat, nq), F32),
        compiler_params=_cparams(("arbitrary", "arbitrary")),
        name="attn_latent",
    )(qkv, qkv, qkv, qkv, qkv, qkv, qkv, cache_k, cache_v,
      cos2, sin2, cos2, cos2, cos2, sin2, sin2, sin2, sink.reshape(1, -1))


def _mm_res_kernel(x_ref, ac_ref, al_ref, w_ref, mod_ref, o_ref, *, n_ctx_tiles):
    a = jnp.where(pl.program_id(0) < n_ctx_tiles, ac_ref[...], al_ref[...])
    out = jnp.dot(a.astype(BF16), w_ref[...], preferred_element_type=F32)
    o_ref[...] = x_ref[...] + mod_ref[0, 2:3, :] * out


def matmul_residual(lay, x, a_ctx, a_lat, w_bf16, mods):
    mi = _mod_index(lay, ROW_TILE)
    k = a_ctx.shape[1]
    nct = lay.t_ctx // ROW_TILE
    return pl.pallas_call(
        functools.partial(_mm_res_kernel, n_ctx_tiles=nct),
        grid=(lay.t // ROW_TILE,),
        in_specs=[pl.BlockSpec((ROW_TILE, D_MODEL), lambda i: (i, 0)),
                  pl.BlockSpec((ROW_TILE, k), lambda i: (jnp.minimum(i, nct - 1), 0)),
                  pl.BlockSpec((ROW_TILE, k), lambda i: (jnp.maximum(i - nct, 0), 0)),
                  pl.BlockSpec((k, D_MODEL), lambda i: (0, 0)),
                  pl.BlockSpec((1, N_MOD, D_MODEL), lambda i: (mi(i), 0, 0))],
        out_specs=pl.BlockSpec((ROW_TILE, D_MODEL), lambda i: (i, 0)),
        out_shape=jax.ShapeDtypeStruct((lay.t, D_MODEL), F32),
        compiler_params=_cparams(("arbitrary",)),
        name="matmul_residual",
    )(x, a_ctx, a_lat, w_bf16, mods)


def _rowtile_load(ref, n, base=0):
    parts = []
    for c in range(ROW_CHUNKS):
        words = ref[pl.ds(base + c, n, stride=ROW_CHUNKS), :]
        for half in range(2):
            parts.append(pltpu.unpack_elementwise(words, index=half, packed_dtype=BF16,
                                                  unpacked_dtype=F32))
    return jnp.concatenate(parts, axis=1)


def _rowtile_store(ref, val, n):
    for c in range(ROW_CHUNKS):
        lo = val[:, 2 * c * LANES:(2 * c + 1) * LANES]
        hi = val[:, (2 * c + 1) * LANES:(2 * c + 2) * LANES]
        ref[pl.ds(c, n, stride=ROW_CHUNKS), :] = pltpu.pack_elementwise([lo, hi], packed_dtype=BF16)


def _row_copy(src, src_row, dst, dst_row, sem):
    return pltpu.make_async_copy(
        src.at[pl.ds(pl.multiple_of(src_row * ROW_CHUNKS, ROW_CHUNKS), ROW_CHUNKS), :],
        dst.at[pl.ds(pl.multiple_of(dst_row * ROW_CHUNKS, ROW_CHUNKS), ROW_CHUNKS), :], sem)


def _router_kernel(x_ref, g_ref, mod_ref, rwt_ref, rb_ref, tri_ref, sg_ref, su_ref, sd_ref,
                   hn_ref, sh_ref, eidx_ref, wts_ref, rank_ref, cnt_ref, cnt_s):
    tm = ROUTER_TILE

    @pl.when(pl.program_id(0) == 0)
    def _():
        cnt_s[...] = jnp.zeros_like(cnt_s)

    h = _modnorm(x_ref[...], g_ref[...], mod_ref, 3)
    _rowtile_store(hn_ref, h, tm)
    hb = h.astype(BF16)
    sgate = jnp.dot(hb, sg_ref[...], preferred_element_type=F32)
    sup = jnp.dot(hb, su_ref[...], preferred_element_type=F32)
    sh_ref[...] = jnp.dot((sgate * jax.nn.sigmoid(sgate) * sup).astype(BF16), sd_ref[...],
                          preferred_element_type=F32)
    logits = lax.dot_general(rwt_ref[...], h, (((1,), (1,)), ((), ())),
                             precision=HIGHEST, preferred_element_type=F32)
    scores = jax.nn.sigmoid(logits)
    choice = scores + rb_ref[...]
    gs_rows = []
    for g in range(N_GROUPS):
        cg = choice[g * GROUP_SIZE:(g + 1) * GROUP_SIZE, :]
        m1 = jnp.max(cg, axis=0, keepdims=True)
        eq = cg == m1
        cnt = jnp.sum(eq.astype(F32), axis=0, keepdims=True)
        m2 = jnp.max(jnp.where(eq, -jnp.inf, cg), axis=0, keepdims=True)
        gs_rows.append(m1 + jnp.where(cnt >= 2.0, m1, m2))
    gs = jnp.concatenate(gs_rows, axis=0)
    gi = lax.broadcasted_iota(I32, (N_GROUPS, tm), 0)
    grank = jnp.zeros((N_GROUPS, tm), I32)
    for g in range(N_GROUPS):
        other = gs[g:g + 1, :]
        ahead = (other > gs) | ((other == gs) & (g < gi))
        grank = grank + ahead.astype(I32)
    gsel = grank < TOPK_GROUPS
    emask = jnp.concatenate(
        [jnp.broadcast_to(gsel[g:g + 1, :], (GROUP_SIZE, tm)) for g in range(N_GROUPS)], axis=0)
    masked = jnp.where(emask, choice, -jnp.inf)
    ei = lax.broadcasted_iota(I32, (N_EXPERTS, tm), 0)
    idxs, ws = [], []
    member = jnp.zeros((N_EXPERTS, tm), F32)
    for _ in range(TOP_K):
        m = jnp.max(masked, axis=0, keepdims=True)
        idx = jnp.min(jnp.where(masked == m, ei, N_EXPERTS), axis=0, keepdims=True)
        hit = ei == idx
        ws.append(jnp.sum(jnp.where(hit, scores, 0.0), axis=0, keepdims=True))
        idxs.append(idx)
        member = jnp.where(hit, 1.0, member)
        masked = jnp.where(hit, -jnp.inf, masked)
    w = jnp.concatenate(ws, axis=0)
    wts_ref[...] = w / jnp.sum(w, axis=0, keepdims=True) * ROUTE_SCALE
    eidx_ref[...] = jnp.concatenate(idxs, axis=0)
    before = jnp.dot(member.astype(BF16), tri_ref[...], preferred_element_type=F32) + cnt_s[...]
    ranks = [jnp.sum(jnp.where(ei == idx, before, 0.0), axis=0, keepdims=True) for idx in idxs]
    rank_ref[...] = jnp.concatenate(ranks, axis=0).astype(I32)
    cnt_s[...] = cnt_s[...] + jnp.sum(member, axis=1, keepdims=True)
    cnt_ref[...] = jnp.broadcast_to(cnt_s[...], cnt_ref.shape)


def moe_router(lay, x, g, mods, router_w, router_b, sg_bf16, su_bf16, sd_bf16):
    t = lay.t
    tm = ROUTER_TILE
    mi = _mod_index(lay, tm)
    tri = (jnp.arange(tm)[:, None] < jnp.arange(tm)[None, :]).astype(BF16)
    tok = lambda i: (0, i)
    const = lambda i: (0, 0)
    return pl.pallas_call(
        _router_kernel,
        grid=(t // tm,),
        in_specs=[pl.BlockSpec((tm, D_MODEL), lambda i: (i, 0)),
                  pl.BlockSpec((1, D_MODEL), const),
                  pl.BlockSpec((1, N_MOD, D_MODEL), lambda i: (mi(i), 0, 0)),
                  pl.BlockSpec((N_EXPERTS, D_MODEL), const),
                  pl.BlockSpec((N_EXPERTS, 1), const),
                  pl.BlockSpec((tm, tm), const),
                  pl.BlockSpec((D_MODEL, D_EXPERT), const),
                  pl.BlockSpec((D_MODEL, D_EXPERT), const),
                  pl.BlockSpec((D_EXPERT, D_MODEL), const)],
        out_specs=[pl.BlockSpec((tm * ROW_CHUNKS, LANES), lambda i: (i, 0)),
                   pl.BlockSpec((tm, D_MODEL), lambda i: (i, 0)),
                   pl.BlockSpec((TOP_K, tm), tok),
                   pl.BlockSpec((TOP_K, tm), tok),
                   pl.BlockSpec((TOP_K, tm), tok),
                   pl.BlockSpec((N_EXPERTS, LANES), const)],
        out_shape=[jax.ShapeDtypeStruct((t * ROW_CHUNKS, LANES), PACKED),
                   jax.ShapeDtypeStruct((t, D_MODEL), F32),
                   jax.ShapeDtypeStruct((TOP_K, t), I32),
                   jax.ShapeDtypeStruct((TOP_K, t), F32),
                   jax.ShapeDtypeStruct((TOP_K, t), I32),
                   jax.ShapeDtypeStruct((N_EXPERTS, LANES), F32)],
        scratch_shapes=[pltpu.VMEM((N_EXPERTS, 1), F32)],
        compiler_params=_cparams(("arbitrary",)),
        name="moe_router",
    )(x, g.reshape(1, -1), mods, router_w.T, router_b.reshape(-1, 1), tri, sg_bf16, su_bf16, sd_bf16)


def _dest_kernel(start_ref, eidx_ref, rank_ref, dest_ref):
    e = eidx_ref[...]

    def body(i, acc):
        return jnp.where(e == i, start_ref[i], acc)

    dest_ref[...] = lax.fori_loop(0, N_EXPERTS, body, jnp.zeros_like(e), unroll=8) + rank_ref[...]


def moe_dest(pad_start, eidx, rank):
    t = eidx.shape[1]
    tn = DEST_TILE
    spec = pl.BlockSpec((TOP_K, tn), lambda i, ps: (0, i))
    return pl.pallas_call(
        _dest_kernel,
        grid_spec=pltpu.PrefetchScalarGridSpec(
            num_scalar_prefetch=1, grid=(t // tn,), in_specs=[spec, spec], out_specs=spec),
        out_shape=jax.ShapeDtypeStruct((TOP_K, t), I32),
        compiler_params=_cparams(("arbitrary",)),
        name="moe_dest",
    )(pad_start, eidx, rank)


def _issue_row_copies(idx_at, n, copy_at, unroll=4):
    def body(i, c):
        for p in range(2):
            r = 2 * i + p
            copy_at(r, idx_at(r)).start(priority=p)
        return c
    lax.fori_loop(0, n // 2, body, 0, unroll=unroll)


def _dispatch_kernel(zrow_ref, dest_hbm, hn_ref, xs_hbm, idx_s, zbuf, isem, zsem, ssem, *, n_tiles):
    i = pl.program_id(0)
    slot = i % 2
    td = DISPATCH_TILE

    n_idx = TOP_K * td

    def idx_copy(tile, s):
        return pltpu.make_async_copy(dest_hbm.at[tile], idx_s.at[pl.ds(s * n_idx, n_idx)], isem.at[s])

    def zero_copy(e):
        r0 = pl.multiple_of(zrow_ref[e] * ROW_CHUNKS, ROW_CHUNKS)
        return pltpu.make_async_copy(zbuf, xs_hbm.at[pl.ds(r0, MOE_BLOCK * ROW_CHUNKS), :], zsem)

    @pl.when(i == 0)
    def _():
        zbuf[...] = jnp.zeros_like(zbuf)

        def zstart(e, c):
            @pl.when(zrow_ref[e] >= 0)
            def _():
                zero_copy(e).start()
            return c

        def zwait(e, c):
            @pl.when(zrow_ref[e] >= 0)
            def _():
                zero_copy(e).wait()
            return c

        lax.fori_loop(0, zrow_ref.shape[0], zstart, 0)
        idx_copy(0, 0).start()
        lax.fori_loop(0, zrow_ref.shape[0], zwait, 0)

    idx_copy(i, slot).wait()

    @pl.when(i + 1 < n_tiles)
    def _():
        idx_copy(i + 1, 1 - slot).start()

    for k in range(TOP_K):
        _issue_row_copies(lambda r: idx_s[slot * n_idx + k * td + r], td,
                          lambda r, d: _row_copy(hn_ref, r, xs_hbm, d, ssem))
    for k in range(TOP_K):
        pltpu.make_async_copy(hn_ref, xs_hbm.at[pl.ds(0, td * ROW_CHUNKS), :], ssem).wait()


def _tile_major(dest, tile):
    t = dest.shape[1]
    return dest.reshape(TOP_K, t // tile, tile).transpose(1, 0, 2).reshape(t // tile, TOP_K * tile)


def moe_dispatch(lay, hn, dest, zero_row, n_rows):
    td = DISPATCH_TILE
    n_tiles = lay.t // td
    return pl.pallas_call(
        functools.partial(_dispatch_kernel, n_tiles=n_tiles),
        grid_spec=pltpu.PrefetchScalarGridSpec(
            num_scalar_prefetch=1,
            grid=(n_tiles,),
            in_specs=[pl.BlockSpec(memory_space=pl.ANY),
                      pl.BlockSpec((td * ROW_CHUNKS, LANES), lambda i, z: (i, 0))],
            out_specs=pl.BlockSpec(memory_space=pl.ANY),
            scratch_shapes=[pltpu.SMEM((2 * TOP_K * td,), I32),
                            pltpu.VMEM((MOE_BLOCK * ROW_CHUNKS, LANES), PACKED),
                            pltpu.SemaphoreType.DMA((2,)),
                            pltpu.SemaphoreType.DMA,
                            pltpu.SemaphoreType.DMA]),
        out_shape=jax.ShapeDtypeStruct((n_rows * ROW_CHUNKS, LANES), PACKED),
        compiler_params=_cparams(("arbitrary",)),
        name="moe_dispatch",
    )(zero_row, dest, hn)


def _expert_kernel(blk0_ref, nblk_ref, tail_ref, xs_hbm, wg_ref, wu_ref, wd_ref, y_hbm,
                   xbuf, ybuf, wg_s, wu_s, wd_s, isem, osem):
    e = pl.program_id(0)
    nb = nblk_ref[e]
    g0 = blk0_ref[e]
    total = blk0_ref[N_EXPERTS - 1] + nblk_ref[N_EXPERTS - 1]
    blk_rows = MOE_BLOCK * ROW_CHUNKS
    n_x = xbuf.shape[0]
    n_y = ybuf.shape[0]

    def block_rows(g):
        return pl.ds(pl.multiple_of(g * blk_rows, blk_rows), blk_rows)

    def fetch(g):
        s = g % n_x
        return pltpu.make_async_copy(xs_hbm.at[block_rows(g), :], xbuf.at[s], isem.at[s])

    def writeback(g):
        s = g % n_y
        return pltpu.make_async_copy(ybuf.at[s], y_hbm.at[block_rows(g), :], osem.at[s])

    ahead = n_x // 2

    @pl.when(e == 0)
    def _():
        for p in range(ahead):
            @pl.when(p < total)
            def _():
                fetch(p).start()

    @pl.when(nb > 0)
    def _():
        wg_s[...] = wg_ref[0, 0].astype(BF16)
        wu_s[...] = wu_ref[0, 0].astype(BF16)
        wd_s[...] = wd_ref[0, 0].astype(BF16)

    def run_blocks(g, n):
        for p in range(n):
            @pl.when(g + ahead + p < total)
            def _():
                fetch(g + ahead + p).start()
        for p in range(n):
            fetch(g + p).wait()
        x = jnp.concatenate([_rowtile_load(xbuf.at[(g + p) % n_x], MOE_BLOCK) for p in range(n)],
                            axis=0).astype(BF16)
        gate = jnp.dot(x, wg_s[...], preferred_element_type=F32)
        up = jnp.dot(x, wu_s[...], preferred_element_type=F32)
        act = gate * jax.nn.sigmoid(gate) * up
        y = jnp.dot(act.astype(BF16), wd_s[...], preferred_element_type=F32)
        for p in range(n):
            @pl.when(g + p >= n_y)
            def _():
                writeback(g + p - n_y).wait()
        for p in range(n):
            _rowtile_store(ybuf.at[(g + p) % n_y], y[p * MOE_BLOCK:(p + 1) * MOE_BLOCK], MOE_BLOCK)
            writeback(g + p).start()

    def quad(jj, c):
        run_blocks(g0 + 4 * jj, 4)
        return c

    lax.fori_loop(0, nb // 4, quad, 0)
    rem = nb % 4

    @pl.when(rem >= 2)
    def _():
        run_blocks(g0 + nb - rem, 2)

    @pl.when(rem % 2 == 1)
    def _():
        run_blocks(g0 + nb - 1, 1)

    @pl.when(e == N_EXPERTS - 1)
    def _():
        for p in range(n_y, 0, -1):
            @pl.when(total >= p)
            def _():
                writeback(total - p).wait()

        ybuf[0] = jnp.zeros(ybuf.shape[1:], PACKED)

        def tail_copy(i):
            r0 = pl.multiple_of(tail_ref[i] * ROW_CHUNKS, blk_rows)
            return pltpu.make_async_copy(ybuf.at[0], y_hbm.at[pl.ds(r0, blk_rows), :], osem.at[0])

        def tstart(i, c):
            @pl.when(tail_ref[i] >= 0)
            def _():
                tail_copy(i).start()
            return c

        def twait(i, c):
            @pl.when(tail_ref[i] >= 0)
            def _():
                tail_copy(i).wait()
            return c

        lax.fori_loop(0, tail_ref.shape[0], tstart, 0)
        lax.fori_loop(0, tail_ref.shape[0], twait, 0)


def moe_experts(xs, first_block, n_blocks, tail_row, layer, w_gate, w_up, w_down):
    wspec = lambda shape: pl.BlockSpec((1, 1) + shape, lambda e, a, b, c: (layer, e, 0, 0))
    blk = (MOE_BLOCK * ROW_CHUNKS, LANES)
    return pl.pallas_call(
        _expert_kernel,
        grid_spec=pltpu.PrefetchScalarGridSpec(
            num_scalar_prefetch=3,
            grid=(N_EXPERTS,),
            in_specs=[pl.BlockSpec(memory_space=pl.ANY),
                      wspec((D_MODEL, D_EXPERT)), wspec((D_MODEL, D_EXPERT)),
                      wspec((D_EXPERT, D_MODEL))],
            out_specs=pl.BlockSpec(memory_space=pl.ANY),
            scratch_shapes=[pltpu.VMEM((EXPERT_X_BUFS,) + blk, PACKED),
                            pltpu.VMEM((EXPERT_Y_BUFS,) + blk, PACKED),
                            pltpu.VMEM((D_MODEL, D_EXPERT), BF16),
                            pltpu.VMEM((D_MODEL, D_EXPERT), BF16),
                            pltpu.VMEM((D_EXPERT, D_MODEL), BF16),
                            pltpu.SemaphoreType.DMA((EXPERT_X_BUFS,)),
                            pltpu.SemaphoreType.DMA((EXPERT_Y_BUFS,))]),
        out_shape=jax.ShapeDtypeStruct(xs.shape, PACKED),
        compiler_params=_cparams(("arbitrary",)),
        name="moe_experts",
    )(first_block, n_blocks, tail_row, xs, w_gate, w_up, w_down)


def _combine_kernel(dest_hbm, y_hbm, x_ref, sh_ref, w_ref, mod_ref, *rest, n_tiles, final, proj):
    if final:
        gf_ref, oc_ref, ol_ref, idx_s, ybuf, isem, gsem = rest
    elif proj:
        gn_ref, modn_ref, wn_ref, o_ref, p_ref, idx_s, ybuf, isem, gsem = rest
    else:
        o_ref, idx_s, ybuf, isem, gsem = rest
    i = pl.program_id(0)
    tm = COMBINE_TILE
    n_idx = TOP_K * tm
    last = n_tiles - 1

    def idx_copy(tile, s):
        return pltpu.make_async_copy(dest_hbm.at[tile], idx_s.at[pl.ds(s * n_idx, n_idx)], isem.at[s])

    def gather(s, unroll=4):
        _issue_row_copies(lambda r: idx_s[s * n_idx + r], n_idx,
                          lambda r, d: _row_copy(y_hbm, d, ybuf.at[s], r, gsem.at[s]), unroll)

    def gather_wait(s):
        pltpu.make_async_copy(y_hbm.at[pl.ds(0, n_idx * ROW_CHUNKS), :], ybuf.at[s], gsem.at[s]).wait()

    @pl.when(i == 0)
    def _():
        c = idx_copy(0, 0)
        c.start()
        c.wait()
        gather(0)
        idx_copy(jnp.minimum(1, last), 1).start()

    for half in range(2):
        tile = 2 * i + half
        s = half
        rows = slice(half * tm, (half + 1) * tm)
        idx_copy(jnp.minimum(tile + 1, last), 1 - s).wait()
        gather_wait(s)
        gather(1 - s, unroll=True)
        idx_copy(jnp.minimum(tile + 2, last), s).start()

        w = w_ref[rows, :]
        routed = jnp.zeros((tm, D_MODEL), F32)
        for k in range(TOP_K):
            routed = routed + w[:, k:k + 1] * _rowtile_load(ybuf.at[s], tm, base=k * tm * ROW_CHUNKS)
        out = x_ref[rows, :] + mod_ref[0, 5:6, :] * (routed + sh_ref[rows, :])
        if final:
            ms = jnp.mean(out * out, axis=-1, keepdims=True)
            y = out * lax.rsqrt(ms + EPS) * gf_ref[...]
            oc_ref[rows, :] = y
            ol_ref[rows, :] = y
        else:
            o_ref[rows, :] = out
            if proj:
                hn = _modnorm(out, gn_ref[...], modn_ref, 0)
                p_ref[rows, :] = jnp.dot(hn.astype(BF16), wn_ref[...], preferred_element_type=F32)

    @pl.when(i == n_tiles // 2 - 1)
    def _():
        gather_wait(0)
        idx_copy(last, 1).wait()


def moe_combine(lay, x, shared, y_rows, dest, wts_t, mods, g_final=None, next_proj=None):
    tm = COMBINE_TILE
    n_tiles = lay.t // tm
    assert n_tiles % 2 == 0
    step = 2 * tm
    nct = lay.t_ctx // step
    mi = _mod_index(lay, step)
    row = lambda i: (i, 0)
    final = g_final is not None
    if final:
        extra_in = [pl.BlockSpec((1, D_MODEL), lambda i: (0, 0))]
        extra_args = [g_final.reshape(1, -1)]
        out_specs = [pl.BlockSpec((step, D_MODEL), lambda i: (jnp.minimum(i, nct), 0)),
                     pl.BlockSpec((step, D_MODEL), lambda i: (jnp.maximum(i - nct, 0), 0))]
        out_shape = [jax.ShapeDtypeStruct((lay.t_ctx + step, D_MODEL), F32),
                     jax.ShapeDtypeStruct((lay.t_lat, D_MODEL), F32)]
    elif next_proj is not None:
        g_next, mods_next, w_next = next_proj
        n_out = w_next.shape[1]
        extra_in = [pl.BlockSpec((1, D_MODEL), lambda i: (0, 0)),
                    pl.BlockSpec((1, N_MOD, D_MODEL), lambda i: (mi(i), 0, 0)),
                    pl.BlockSpec((D_MODEL, n_out), lambda i: (0, 0))]
        extra_args = [g_next.reshape(1, -1), mods_next, w_next]
        out_specs = [pl.BlockSpec((step, D_MODEL), row), pl.BlockSpec((step, n_out), row)]
        out_shape = [jax.ShapeDtypeStruct((lay.t, D_MODEL), F32),
                     jax.ShapeDtypeStruct((lay.t, n_out), F32)]
    else:
        extra_in, extra_args = [], []
        out_specs = pl.BlockSpec((step, D_MODEL), row)
        out_shape = jax.ShapeDtypeStruct((lay.t, D_MODEL), F32)
    return pl.pallas_call(
        functools.partial(_combine_kernel, n_tiles=n_tiles, final=final,
                          proj=next_proj is not None and not final),
        grid=(n_tiles // 2,),
        in_specs=[pl.BlockSpec(memory_space=pl.ANY),
                  pl.BlockSpec(memory_space=pl.ANY),
                  pl.BlockSpec((step, D_MODEL), row),
                  pl.BlockSpec((step, D_MODEL), row),
                  pl.BlockSpec((step, TOP_K), row),
                  pl.BlockSpec((1, N_MOD, D_MODEL), lambda i: (mi(i), 0, 0))] + extra_in,
        out_specs=out_specs,
        out_shape=out_shape,
        scratch_shapes=[pltpu.SMEM((2 * TOP_K * tm,), I32),
                        pltpu.VMEM((2, TOP_K * tm * ROW_CHUNKS, LANES), PACKED),
                        pltpu.SemaphoreType.DMA((2,)),
                        pltpu.SemaphoreType.DMA((2,))],
        compiler_params=_cparams(("arbitrary",)),
        name="moe_combine",
    )(dest, y_rows, x, shared, wts_t, mods, *extra_args)


def moe_layer(lay, x, g, mods, router_w, router_b, layer, w_gate, w_up, w_down, s_gate, s_up, s_down,
              g_final=None, next_proj=None):
    t = lay.t
    hn, shared, eidx, wts, rank, cnt = moe_router(
        lay, x, g, mods, router_w, router_b, s_gate.astype(BF16), s_up.astype(BF16), s_down.astype(BF16))
    counts = cnt[:, 0].astype(I32)
    n_blocks = (counts + MOE_BLOCK - 1) // MOE_BLOCK
    padded = n_blocks * MOE_BLOCK
    pad_end = jnp.cumsum(padded)
    pad_start = pad_end - padded
    n_rows = -(-(t * TOP_K + N_EXPERTS * (MOE_BLOCK - 1)) // MOE_BLOCK) * MOE_BLOCK
    dest = moe_dest(pad_start, eidx, rank)
    last_row = jnp.where(n_blocks > 0, pad_end - MOE_BLOCK, -1)
    tail_blk = pad_end[-1] // MOE_BLOCK + jnp.arange(n_rows // MOE_BLOCK - t * TOP_K // MOE_BLOCK)
    tail_row = jnp.where(tail_blk < n_rows // MOE_BLOCK, tail_blk * MOE_BLOCK, -1).astype(I32)
    xs = moe_dispatch(lay, hn, _tile_major(dest, DISPATCH_TILE),
                      jnp.concatenate([last_row, tail_row]), n_rows)
    y_rows = moe_experts(xs, pad_start // MOE_BLOCK, n_blocks, tail_row, layer, w_gate, w_up, w_down)
    return moe_combine(lay, x, shared, y_rows, _tile_major(dest, COMBINE_TILE), wts.T, mods,
                       g_final, next_proj)


def _block_diag(w):
    nb, bw, _ = w.shape
    eye = jnp.eye(nb, dtype=w.dtype)
    return (eye[:, None, :, None] * w[:, :, None, :]).reshape(nb * bw, nb * bw)


def even_layer(lay, x, mods, g_mix, p, state_lru, state_ssm_re, state_ssm_im):
    t = lay.t
    proj, u_g = modnorm_matmul(lay, x, g_mix, mods, 0, p['w_in'].astype(BF16),
                               ug_col=2 * D_LRU)
    zeros_c = jnp.zeros((lay.n_ctx, D_LRU), F32)
    hf_y, st = None, []
    for d in range(2):
        wg = jnp.concatenate([_block_diag(p['lru_wa'][d]), _block_diag(p['lru_wx'][d])], axis=1)
        bg = jnp.concatenate([p['lru_ba'][d], p['lru_bx'][d]])
        h0 = jnp.concatenate([zeros_c, state_lru[:, d].astype(F32)], axis=0)
        hf_y, s = lru_pass(lay, proj, p['conv_w'], p['conv_b'], wg.astype(BF16), bg,
                           p['lru_lam'][d], h0, reverse=(d == 1), hf=hf_y)
        st.append(s[:lay.n_ctx])
    y_a = hf_y
    new_lru = jnp.stack(st, axis=1)

    mats = _s5_matrices(p['a_re'], p['a_im'], p['log_dt'], p['b_re'], p['b_im'], p['c_re'], p['c_im'])
    h0 = jnp.concatenate([state_ssm_re, state_ssm_im], axis=-1).astype(F32)
    h0 = h0.transpose(2, 1, 0, 3)
    y_g, h_ctx = s5_mixer(lay, u_g, mats, h0)
    seg = lay.s_ctx // S5_CHUNK
    h_ctx = h_ctx.reshape(SSM_GROUPS, 2, lay.n_ctx, seg, 2 * SSM_STATE)
    ends = jnp.stack([h_ctx[:, 0, :, seg - 1], h_ctx[:, 1, :, 0]], axis=1)
    ends = ends.transpose(2, 1, 0, 3)
    x = even_out(lay, x, y_a, y_g, proj, p['d'], p['glu_w'].astype(BF16), p['glu_b'],
                 p['w_out'].astype(BF16), mods)
    return x, new_lru, ends[..., :SSM_STATE], ends[..., SSM_STATE:]


def odd_layer(lay, x, mods, g_mix, w_qkv, sink, w_out, cache_k, cache_v, qkv=None):
    if qkv is None:
        qkv = modnorm_matmul(lay, x, g_mix, mods, 0, w_qkv.astype(BF16))
    o_ctx = attn_context(lay, qkv, sink)
    o_lat = attn_latent(lay, qkv, cache_k, cache_v, sink)
    nq = N_HEADS * HEAD_DIM
    kv = qkv[:lay.t_ctx, nq:].reshape(lay.n_ctx, lay.s_ctx, 2, N_KV, HEAD_DIM)
    k_new = kv[:, :, 0].swapaxes(1, 2)
    v_new = kv[:, :, 1].swapaxes(1, 2)
    x = matmul_residual(lay, x, o_ctx, o_lat, w_out.astype(BF16), mods)
    return x, k_new, v_new


def _forward(lay, x_prompt, x_sample, state_lru, state_ssm_re, state_ssm_im, cache_k, cache_v,
             c, c_ctx, g_mix, g_ffn, w_mod, b_mod,
             ev_w_in, lru_conv_w, lru_conv_b, lru_wa, lru_ba, lru_wx, lru_bx, lru_lam,
             ssm_a_re, ssm_a_im, ssm_log_dt, ssm_b_re, ssm_b_im, ssm_c_re, ssm_c_im, ssm_d,
             ssm_glu_w, ssm_glu_b, ev_w_out, at_w_qkv, at_sink, at_w_out,
             router_w, router_b, exp_w_gate, exp_w_up, exp_w_down, sh_w_gate, sh_w_up, sh_w_down,
             g_final):
    depth = g_mix.shape[0]
    x = jnp.concatenate([x_prompt.reshape(lay.t_ctx, D_MODEL), x_sample.reshape(lay.t_lat, D_MODEL)],
                        axis=0)
    n_c = 1 + lay.n_lat
    c_rows = jnp.concatenate([c_ctx[None, :], c, jnp.zeros((16 - n_c, D_MODEL), F32)], axis=0)
    new_lru, new_re, new_im, new_k, new_v = [], [], [], [], []
    all_mods = [adaln_table(c_rows, l, w_mod, b_mod[l]) for l in range(depth)]
    qkv = None
    for l in range(depth):
        i = l // 2
        mods = all_mods[l]
        if l % 2 == 0:
            p = dict(w_in=ev_w_in[i], conv_w=lru_conv_w[i], conv_b=lru_conv_b[i],
                     lru_wa=lru_wa[i], lru_ba=lru_ba[i], lru_wx=lru_wx[i], lru_bx=lru_bx[i],
                     lru_lam=lru_lam[i], a_re=ssm_a_re[i], a_im=ssm_a_im[i], log_dt=ssm_log_dt[i],
                     b_re=ssm_b_re[i], b_im=ssm_b_im[i], c_re=ssm_c_re[i], c_im=ssm_c_im[i],
                     d=ssm_d[i], glu_w=ssm_glu_w[i], glu_b=ssm_glu_b[i], w_out=ev_w_out[i])
            x, lru_i, re_i, im_i = even_layer(lay, x, mods, g_mix[l], p, state_lru[:, i],
                                              state_ssm_re[:, i], state_ssm_im[:, i])
            new_lru.append(lru_i)
            new_re.append(re_i)
            new_im.append(im_i)
        else:
            x, k_i, v_i = odd_layer(lay, x, mods, g_mix[l], at_w_qkv[i], at_sink[i], at_w_out[i],
                                    cache_k[:, i], cache_v[:, i], qkv)
            new_k.append(k_i)
            new_v.append(v_i)
        next_proj = None
        if l + 1 < depth and (l + 1) % 2 == 1:
            next_proj = (g_mix[l + 1], all_mods[l + 1], at_w_qkv[(l + 1) // 2].astype(BF16))
        x = moe_layer(lay, x, g_ffn[l], mods, router_w[l], router_b[l], l, exp_w_gate, exp_w_up,
                      exp_w_down, sh_w_gate[l], sh_w_up[l], sh_w_down[l],
                      g_final=g_final if l == depth - 1 else None, next_proj=next_proj)
        qkv = None
        if next_proj is not None:
            x, qkv = x
    y_ctx, y_lat = x
    y_prompt = y_ctx[:lay.t_ctx].reshape(x_prompt.shape)
    y_sample = y_lat.reshape(x_sample.shape)
    return (y_prompt, y_sample, jnp.stack(new_lru, axis=1), jnp.stack(new_re, axis=1),
            jnp.stack(new_im, axis=1), jnp.stack(new_k, axis=1), jnp.stack(new_v, axis=1))


def kernel(x_prompt, x_sample, state_lru, state_ssm_re, state_ssm_im, cache_k, cache_v, c, c_ctx, g_mix, g_ffn, w_mod, b_mod, ev_w_in, lru_conv_w, lru_conv_b, lru_wa, lru_ba, lru_wx, lru_bx, lru_lam, ssm_a_re, ssm_a_im, ssm_log_dt, ssm_b_re, ssm_b_im, ssm_c_re, ssm_c_im, ssm_d, ssm_glu_w, ssm_glu_b, ev_w_out, at_w_qkv, at_sink, at_w_out, router_w, router_b, exp_w_gate, exp_w_up, exp_w_down, sh_w_gate, sh_w_up, sh_w_down, g_final):
    lay = Layout(n_ctx=x_prompt.shape[0], s_ctx=x_prompt.shape[1],
                 n_lat=x_sample.shape[0], s_lat=x_sample.shape[1])
    return _forward(lay, x_prompt, x_sample, state_lru, state_ssm_re, state_ssm_im, cache_k, cache_v,
                    c, c_ctx, g_mix, g_ffn, w_mod, b_mod,
                    ev_w_in, lru_conv_w, lru_conv_b, lru_wa, lru_ba, lru_wx, lru_bx, lru_lam,
                    ssm_a_re, ssm_a_im, ssm_log_dt, ssm_b_re, ssm_b_im, ssm_c_re, ssm_c_im, ssm_d,
                    ssm_glu_w, ssm_glu_b, ev_w_out, at_w_qkv, at_sink, at_w_out,
                    router_w, router_b, exp_w_gate, exp_w_up, exp_w_down, sh_w_gate, sh_w_up,
                    sh_w_down, g_final)
```

```python
import functools
from typing import NamedTuple

import jax
import jax.numpy as jnp
from jax import lax
from jax.experimental import pallas as pl
from jax.experimental.pallas import tpu as pltpu

F32 = jnp.float32
BF16 = jnp.bfloat16
I32 = jnp.int32
HIGHEST = lax.Precision.HIGHEST

D_MODEL = 1024
EPS = 1e-6
N_MOD = 6
GRID_W = 64
D_LRU = 512
LRU_BLOCKS = 8
LRU_C = 8.0
CONV_W = 4
CONV_LEFT = 2
D_SSM = 512
SSM_GROUP = 16
SSM_GROUPS = 32
SSM_STATE = 64
S5_CHUNK = 16
S5_LANES = S5_CHUNK * SSM_GROUP
S5_SCAN_STEPS = 8
HEAD_DIM = 64
N_HEADS = 16
N_KV = 4
GQA = 4
WINDOW = 128
Q_BLOCK = 128
ROPE_BASE = 10000.0
ATTN_SCALE = HEAD_DIM ** -0.5
NEG_INF = -1e30
N_EXPERTS = 256
TOP_K = 8
N_GROUPS = 8
TOPK_GROUPS = 4
GROUP_SIZE = N_EXPERTS // N_GROUPS
D_EXPERT = 256
ROUTE_SCALE = 2.5
MOE_BLOCK = 128

SUBLANES = 8
LANES = 128
ROW_CHUNKS = D_MODEL // (2 * LANES)
PACKED = jnp.int32
SEQ_TILE = 256
ROW_TILE = 512
ROUTER_TILE = 512
DEST_TILE = 1024
DISPATCH_TILE = 512
COMBINE_TILE = 256
EXPERT_X_BUFS = 8
EXPERT_Y_BUFS = 8
VMEM_LIMIT = 56 * 1024 * 1024


class Layout(NamedTuple):
    n_ctx: int
    s_ctx: int
    n_lat: int
    s_lat: int

    @property
    def t_ctx(self):
        return self.n_ctx * self.s_ctx

    @property
    def t_lat(self):
        return self.n_lat * self.s_lat

    @property
    def t(self):
        return self.t_ctx + self.t_lat

    @property
    def n_seq(self):
        return self.n_ctx + self.n_lat


def _cparams(sem):
    return pltpu.CompilerParams(dimension_semantics=sem, vmem_limit_bytes=VMEM_LIMIT)


def _mod_index(lay, tile_rows):
    n_ctx_tiles = lay.t_ctx // tile_rows
    per_lat = lay.s_lat // tile_rows

    def f(i):
        return jnp.where(i < n_ctx_tiles, 0, 1 + (i - n_ctx_tiles) // per_lat)
    return f


def _adaln_kernel(c_ref, w_ref, b_ref, o_ref):
    c = c_ref[...]
    s = c * jax.nn.sigmoid(c)
    o_ref[...] = jnp.dot(s, w_ref[0], precision=HIGHEST, preferred_element_type=F32) + b_ref[...]


def adaln_table(c_rows, layer, w_mod, b_mod):
    n = c_rows.shape[0]
    tn = 1536
    out = pl.pallas_call(
        _adaln_kernel,
        grid=(N_MOD * D_MODEL // tn,),
        in_specs=[pl.BlockSpec((n, D_MODEL), lambda j: (0, 0)),
                  pl.BlockSpec((1, D_MODEL, tn), lambda j: (layer, 0, j)),
                  pl.BlockSpec((1, tn), lambda j: (0, j))],
        out_specs=pl.BlockSpec((n, tn), lambda j: (0, j)),
        out_shape=jax.ShapeDtypeStruct((n, N_MOD * D_MODEL), F32),
        compiler_params=_cparams(("arbitrary",)),
        name="adaln",
    )(c_rows, w_mod, b_mod.reshape(1, -1))
    return out.reshape(n, N_MOD, D_MODEL)


def _modnorm(x, g, mod_ref, slot):
    ms = jnp.mean(x * x, axis=-1, keepdims=True)
    y = x * lax.rsqrt(ms + EPS) * g
    shift = mod_ref[0, slot:slot + 1, :]
    scale = mod_ref[0, slot + 1:slot + 2, :]
    return y * (1.0 + scale) + shift


GROUPS_PER_VREG = LANES // SSM_GROUP
SSM_COL_BLOCKS = D_SSM // LANES


def _group_major_store(val, tmp_ref, dst_ref):
    rows = dst_ref.shape[1]
    for j in range(SSM_COL_BLOCKS):
        tmp_ref[j] = val[:, j * LANES:(j + 1) * LANES]
    for j in range(SSM_COL_BLOCKS):
        steps = [tmp_ref[j, pl.ds(i, rows, stride=S5_CHUNK), :] for i in range(S5_CHUNK)]
        for q in range(GROUPS_PER_VREG):
            dst_ref[j * GROUPS_PER_VREG + q] = jnp.concatenate(
                [w[:, q * SSM_GROUP:(q + 1) * SSM_GROUP] for w in steps], axis=1)


def _group_major_load(src_ref, tmp_ref):
    rows = src_ref.shape[1]
    for j in range(SSM_COL_BLOCKS):
        blocks = [src_ref[j * GROUPS_PER_VREG + q] for q in range(GROUPS_PER_VREG)]
        for i in range(S5_CHUNK):
            tmp_ref[j, pl.ds(i, rows, stride=S5_CHUNK), :] = jnp.concatenate(
                [b[:, i * SSM_GROUP:(i + 1) * SSM_GROUP] for b in blocks], axis=1)
    return jnp.concatenate([tmp_ref[j] for j in range(SSM_COL_BLOCKS)], axis=1)


def _split_row_specs(lay, tile_rows, width):
    nct = lay.t_ctx // tile_rows
    return [pl.BlockSpec((tile_rows, width), lambda i: (jnp.minimum(i, nct - 1), 0)),
            pl.BlockSpec((tile_rows, width), lambda i: (jnp.maximum(i - nct, 0), 0))], nct


def _pick_rows(ctx_ref, lat_ref, n_ctx_tiles):
    return jnp.where(pl.program_id(0) < n_ctx_tiles, ctx_ref[...], lat_ref[...])


def _modnorm_mm_kernel(*refs, slot, ug_col, n_ctx_tiles):
    if n_ctx_tiles is None:
        x = refs[0][...]
        refs = refs[1:]
    else:
        x = _pick_rows(refs[0], refs[1], n_ctx_tiles)
        refs = refs[2:]
    g_ref, mod_ref, w_ref, o_ref, *ug_refs = refs
    h = _modnorm(x, g_ref[...], mod_ref, slot)
    out = jnp.dot(h.astype(BF16), w_ref[...], preferred_element_type=F32)
    o_ref[...] = out
    if ug_col is not None:
        ug_ref, tmp_ref = ug_refs
        _group_major_store(out[:, ug_col:ug_col + D_SSM], tmp_ref, ug_ref)


def modnorm_matmul(lay, x, g, mods, slot, w_bf16, ug_col=None):
    t = lay.t
    n = w_bf16.shape[1]
    mi = _mod_index(lay, ROW_TILE)
    out_specs = [pl.BlockSpec((ROW_TILE, n), lambda i: (i, 0))]
    out_shape = [jax.ShapeDtypeStruct((t, n), F32)]
    if ug_col is not None:
        out_specs.append(pl.BlockSpec((SSM_GROUPS, ROW_TILE // S5_CHUNK, S5_LANES), lambda i: (0, i, 0)))
        out_shape.append(jax.ShapeDtypeStruct((SSM_GROUPS, t // S5_CHUNK, S5_LANES), F32))
    if isinstance(x, tuple):
        x_specs, nct = _split_row_specs(lay, ROW_TILE, D_MODEL)
    else:
        x_specs, nct, x = [pl.BlockSpec((ROW_TILE, D_MODEL), lambda i: (i, 0))], None, (x,)
    outs = pl.pallas_call(
        functools.partial(_modnorm_mm_kernel, slot=slot, ug_col=ug_col, n_ctx_tiles=nct),
        grid=(t // ROW_TILE,),
        in_specs=x_specs + [
                  pl.BlockSpec((1, D_MODEL), lambda i: (0, 0)),
                  pl.BlockSpec((1, N_MOD, D_MODEL), lambda i: (mi(i), 0, 0)),
                  pl.BlockSpec((D_MODEL, n), lambda i: (0, 0))],
        out_specs=out_specs,
        out_shape=out_shape,
        scratch_shapes=([pltpu.VMEM((SSM_COL_BLOCKS, ROW_TILE, LANES), F32)]
                        if ug_col is not None else []),
        compiler_params=_cparams(("arbitrary",)),
        name="modnorm_matmul",
    )(*x, g.reshape(1, -1), mods, w_bf16)
    return outs if ug_col is not None else outs[0]


def _seq_tile_maps(lay, reverse):
    assert lay.s_ctx == SEQ_TILE and lay.s_lat % SEQ_TILE == 0
    n_tiles = lay.t // SEQ_TILE
    per_lat = lay.s_lat // SEQ_TILE

    def tile(i):
        return (n_tiles - 1 - i) if reverse else i

    def seq(i):
        ti = tile(i)
        return jnp.where(ti < lay.n_ctx, ti, lay.n_ctx + (ti - lay.n_ctx) // per_lat)

    return n_tiles, tile, seq


def _softplus(x):
    return jnp.maximum(x, 0.0) + jnp.log(1.0 + jnp.exp(-jnp.abs(x)))


def _lru_kernel(rec_ref, prev_ref, next_ref, cw_ref, cb_ref, wg_ref, bg_ref, lam_ref, h0_ref,
                *rest, reverse, n_ctx, per_lat, n_tiles):
    if reverse:
        gate_ref, hf_ref, y_ref, st_ref, a_s, b_s, h_s, carry = rest
    else:
        y_ref, st_ref, a_s, b_s, h_s, carry = rest
    i = pl.program_id(0)
    ti = (n_tiles - 1 - i) if reverse else i
    is_first = jnp.logical_or(ti < n_ctx, (ti - n_ctx) % per_lat == 0)
    is_last = jnp.logical_or(ti < n_ctx, (ti - n_ctx) % per_lat == per_lat - 1)
    ts = SEQ_TILE

    rec = rec_ref[...]
    prev = jnp.where(is_first, 0.0, prev_ref[...])
    nxt = jnp.where(is_last, 0.0, next_ref[...])
    ext = jnp.concatenate([prev, rec, nxt], axis=0)
    n_ext = ts + 2 * SUBLANES
    cw = cw_ref[...]
    xc = cb_ref[...] + cw[2:3, :] * rec
    xc = xc + cw[0:1, :] * pltpu.roll(ext, 2, 0)[SUBLANES:SUBLANES + ts]
    xc = xc + cw[1:2, :] * pltpu.roll(ext, 1, 0)[SUBLANES:SUBLANES + ts]
    xc = xc + cw[3:4, :] * pltpu.roll(ext, n_ext - 1, 0)[SUBLANES:SUBLANES + ts]

    gates = jax.nn.sigmoid(jnp.dot(xc.astype(BF16), wg_ref[...], preferred_element_type=F32)
                           + bg_ref[...])
    r = gates[:, :D_LRU]
    ig = gates[:, D_LRU:]
    log_a = (-LRU_C) * r * _softplus(-lam_ref[...])
    a = jnp.exp(log_a)
    b = jnp.sqrt(1.0 - jnp.exp(2.0 * log_a)) * (ig * xc)

    row8 = lax.broadcasted_iota(I32, (ts, D_LRU), 0) % SUBLANES
    for sh in (1, 2, 4):
        if reverse:
            keep = row8 < SUBLANES - sh
            a_sh = pltpu.roll(a, ts - sh, 0)
            b_sh = pltpu.roll(b, ts - sh, 0)
        else:
            keep = row8 >= sh
            a_sh = pltpu.roll(a, sh, 0)
            b_sh = pltpu.roll(b, sh, 0)
        b = b + a * jnp.where(keep, b_sh, 0.0)
        a = a * jnp.where(keep, a_sh, 1.0)
    a_s[...] = a
    b_s[...] = b

    @pl.when(is_last if reverse else is_first)
    def _():
        carry[...] = h0_ref[0]

    n_grp = ts // SUBLANES

    def body(k, c):
        gi = (n_grp - 1 - k) if reverse else k
        sl = pl.ds(pl.multiple_of(gi * SUBLANES, SUBLANES), SUBLANES)
        h = b_s[sl, :] + a_s[sl, :] * c
        h_s[sl, :] = h
        return h[0:1, :] if reverse else h[SUBLANES - 1:SUBLANES, :]

    c_fin = lax.fori_loop(0, n_grp, body, carry[...], unroll=4)
    carry[...] = c_fin
    st_ref[0] = c_fin
    if reverse:
        y_ref[...] = (hf_ref[...] + h_s[...]) * jax.nn.gelu(gate_ref[...])
    else:
        y_ref[...] = h_s[...]


def lru_pass(lay, proj, conv_w, conv_b, wg_bf16, bg, lam, h0, reverse, hf=None):
    n_tiles, tile, seq = _seq_tile_maps(lay, reverse)
    per_lat = lay.s_lat // SEQ_TILE
    blk8 = SEQ_TILE // SUBLANES
    last8 = lay.t // SUBLANES - 1
    c = D_LRU
    in_specs = [
        pl.BlockSpec((SEQ_TILE, c), lambda i: (tile(i), 1)),
        pl.BlockSpec((SUBLANES, c), lambda i: (jnp.maximum(tile(i) * blk8 - 1, 0), 1)),
        pl.BlockSpec((SUBLANES, c), lambda i: (jnp.minimum(tile(i) * blk8 + blk8, last8), 1)),
        pl.BlockSpec((CONV_W, c), lambda i: (0, 0)),
        pl.BlockSpec((1, c), lambda i: (0, 0)),
        pl.BlockSpec((c, 2 * c), lambda i: (0, 0)),
        pl.BlockSpec((1, 2 * c), lambda i: (0, 0)),
        pl.BlockSpec((1, c), lambda i: (0, 0)),
        pl.BlockSpec((1, 1, c), lambda i: (seq(i), 0, 0)),
    ]
    args = [proj, proj, proj, conv_w, conv_b.reshape(1, -1), wg_bf16, bg.reshape(1, -1),
            lam.reshape(1, -1), h0.reshape(lay.n_seq, 1, c)]
    if reverse:
        in_specs += [pl.BlockSpec((SEQ_TILE, c), lambda i: (tile(i), 0)),
                     pl.BlockSpec((SEQ_TILE, c), lambda i: (tile(i), 0))]
        args += [proj, hf]
    y, st = pl.pallas_call(
        functools.partial(_lru_kernel, reverse=reverse, n_ctx=lay.n_ctx, per_lat=per_lat,
                          n_tiles=n_tiles),
        grid=(n_tiles,),
        in_specs=in_specs,
        out_specs=[pl.BlockSpec((SEQ_TILE, c), lambda i: (tile(i), 0)),
                   pl.BlockSpec((1, 1, c), lambda i: (seq(i), 0, 0))],
        out_shape=[jax.ShapeDtypeStruct((lay.t, c), F32),
                   jax.ShapeDtypeStruct((lay.n_seq, 1, c), F32)],
        scratch_shapes=[pltpu.VMEM((SEQ_TILE, c), F32), pltpu.VMEM((SEQ_TILE, c), F32),
                        pltpu.VMEM((SEQ_TILE, c), F32), pltpu.VMEM((1, c), F32)],
        compiler_params=_cparams(("arbitrary",)),
        name="lru_bwd" if reverse else "lru_fwd",
    )(*args)
    return y, st.reshape(lay.n_seq, c)


def _cmul(a, b):
    return a[0] * b[0] - a[1] * b[1], a[0] * b[1] + a[1] * b[0]


def _s5_matrices(a_re, a_im, log_dt, b_re, b_im, c_re, c_im):
    a_re, a_im = a_re.astype(F32), a_im.astype(F32)
    dt = jnp.exp(log_dt.astype(F32))[..., None]
    z = (a_re * dt, a_im * dt)

    def zpow(k):
        k = k.reshape((-1,) + (1,) * z[0].ndim)
        mag = jnp.exp(k * z[0][None])
        return mag * jnp.cos(k * z[1][None]), mag * jnp.sin(k * z[1][None])

    a_bar = zpow(jnp.ones((1,), F32))
    a_bar = (a_bar[0][0], a_bar[1][0])
    den = a_re * a_re + a_im * a_im
    xr, xi = a_bar[0] - 1.0, a_bar[1]
    q = ((xr * a_re + xi * a_im) / den, (xi * a_re - xr * a_im) / den)
    b_bar = _cmul((q[0][..., None], q[1][..., None]), (b_re.astype(F32), b_im.astype(F32)))
    cc = (c_re.astype(F32), c_im.astype(F32))
    el = S5_CHUNK
    pw = zpow(jnp.arange(el + 1, dtype=F32))
    idx = jnp.arange(el)
    m_in, m_toep, m_out = [], [], []
    for d in range(2):
        p_d = (pw[0][:, d], pw[1][:, d])
        b_d = (b_bar[0][d], b_bar[1][d])
        c_d = (cc[0][d], cc[1][d])
        k_in = (el - 1 - idx) if d == 0 else idx
        w_in = _cmul((p_d[0][k_in][..., None], p_d[1][k_in][..., None]),
                     (b_d[0][None], b_d[1][None]))
        w_in = [jnp.transpose(w, (1, 0, 3, 2)).reshape(SSM_GROUPS, S5_LANES, SSM_STATE) for w in w_in]
        m_in.append(jnp.concatenate(w_in, axis=-1))
        cp = _cmul((c_d[0][None], c_d[1][None]),
                   (p_d[0][:, :, None, :], p_d[1][:, :, None, :]))
        kern = (jnp.einsum('kghp,gpc->kgch', cp[0][:el], b_d[0])
                - jnp.einsum('kghp,gpc->kgch', cp[1][:el], b_d[1]))
        zero = jnp.zeros_like(kern[0])
        rows = []
        for i in range(el):
            if d == 0:
                pieces = [zero] * i + [kern[k] for k in range(el - i)]
            else:
                pieces = [kern[i - j] for j in range(i + 1)] + [zero] * (el - 1 - i)
            rows.append(jnp.concatenate(pieces, axis=-1))
        m_toep.append(jnp.stack(rows, axis=1).reshape(SSM_GROUPS, S5_LANES, S5_LANES))
        k_out = (idx + 1) if d == 0 else (el - idx)
        w_out = [jnp.transpose(w[k_out], (1, 3, 0, 2)).reshape(SSM_GROUPS, SSM_STATE, S5_LANES)
                 for w in cp]
        m_out.append(jnp.concatenate([w_out[0], -w_out[1]], axis=1))
    mul = zpow(el * 2.0 ** jnp.arange(S5_SCAN_STEPS, dtype=F32))
    mul = [jnp.transpose(m, (2, 1, 0, 3)) for m in mul]
    coef_a = jnp.concatenate([mul[0], mul[0]], axis=-1)
    coef_b = jnp.concatenate([-mul[1], mul[1]], axis=-1)
    stack = lambda xs: jnp.stack(xs, axis=1)
    return (stack(m_in).astype(BF16), stack(m_toep).astype(BF16), stack(m_out).astype(BF16),
            coef_a, coef_b)


def _s5_scan(v, ca, cb, seg, reverse):
    n = v.shape[0]
    assert seg <= 2 ** S5_SCAN_STEPS
    row = lax.broadcasted_iota(I32, (n, 2 * SSM_STATE), 0) % seg
    k, sh = 0, 1
    while sh < seg:
        if reverse:
            s = jnp.where(row < seg - sh, pltpu.roll(v, n - sh, 0), 0.0)
        else:
            s = jnp.where(row >= sh, pltpu.roll(v, sh, 0), 0.0)
        v = v + ca[k:k + 1, :] * s + cb[k:k + 1, :] * pltpu.roll(s, SSM_STATE, 1)
        k += 1
        sh *= 2
    return v


def _s5_shift(h, seg, reverse):
    n = h.shape[0]
    row = lax.broadcasted_iota(I32, (n, 2 * SSM_STATE), 0) % seg
    if reverse:
        return jnp.where(row < seg - 1, pltpu.roll(h, n - 1, 0), 0.0)
    return jnp.where(row >= 1, pltpu.roll(h, 1, 0), 0.0)


def _s5_kernel(u_ref, min_ref, mtoep_ref, mout_ref, ca_ref, cb_ref, h0_ref, y_ref, hc_ref,
               v_s, hp_s, *, rc, seg_c, n_lat, seg_l):
    u = u_ref[0].astype(BF16)
    u_c, u_l = u[:rc], u[rc:]
    y_c = jnp.zeros((rc, S5_LANES), F32)
    y_l = jnp.zeros((n_lat * seg_l, S5_LANES), F32)
    for d in range(2):
        reverse = d == 1
        ca = ca_ref[0, d]
        cb = cb_ref[0, d]
        m_in = min_ref[0, d]
        m_toep = mtoep_ref[0, d]
        m_out = mout_ref[0, d]
        h_c = _s5_scan(jnp.dot(u_c, m_in, preferred_element_type=F32), ca, cb, seg_c, reverse)
        hc_ref[0, d] = h_c
        hp_c = _s5_shift(h_c, seg_c, reverse)
        y_c = y_c + jnp.dot(u_c, m_toep, preferred_element_type=F32)
        y_c = y_c + jnp.dot(hp_c.astype(BF16), m_out, preferred_element_type=F32)
        v_s[...] = jnp.dot(u_l, m_in, preferred_element_type=F32)
        for s in range(n_lat):
            h0 = h0_ref[0, d, s:s + 1, :]
            r0 = s * seg_l + (seg_l - 1 if reverse else 0)
            v_s[r0:r0 + 1, :] = (v_s[r0:r0 + 1, :] + ca[0:1, :] * h0
                                 + cb[0:1, :] * pltpu.roll(h0, SSM_STATE, 1))
        h_l = _s5_scan(v_s[...], ca, cb, seg_l, reverse)
        hp_s[...] = _s5_shift(h_l, seg_l, reverse)
        for s in range(n_lat):
            r0 = s * seg_l + (seg_l - 1 if reverse else 0)
            hp_s[r0:r0 + 1, :] = h0_ref[0, d, s:s + 1, :]
        y_l = y_l + jnp.dot(u_l, m_toep, preferred_element_type=F32)
        y_l = y_l + jnp.dot(hp_s[...].astype(BF16), m_out, preferred_element_type=F32)
    y_ref[0, :rc, :] = y_c
    y_ref[0, rc:, :] = y_l


def s5_mixer(lay, u_g, mats, h0):
    m_in, m_toep, m_out, coef_a, coef_b = mats
    rows = lay.t // S5_CHUNK
    rc = lay.t_ctx // S5_CHUNK
    rl = rows - rc
    st2 = 2 * SSM_STATE
    g4 = lambda g: (g, 0, 0, 0)
    return pl.pallas_call(
        functools.partial(_s5_kernel, rc=rc, seg_c=lay.s_ctx // S5_CHUNK, n_lat=lay.n_lat,
                          seg_l=lay.s_lat // S5_CHUNK),
        grid=(SSM_GROUPS,),
        in_specs=[pl.BlockSpec((1, rows, S5_LANES), lambda g: (g, 0, 0)),
                  pl.BlockSpec((1, 2, S5_LANES, st2), g4),
                  pl.BlockSpec((1, 2, S5_LANES, S5_LANES), g4),
                  pl.BlockSpec((1, 2, st2, S5_LANES), g4),
                  pl.BlockSpec((1, 2, S5_SCAN_STEPS, st2), g4),
                  pl.BlockSpec((1, 2, S5_SCAN_STEPS, st2), g4),
                  pl.BlockSpec((1, 2, lay.n_lat, st2), g4)],
        out_specs=[pl.BlockSpec((1, rows, S5_LANES), lambda g: (g, 0, 0)),
                   pl.BlockSpec((1, 2, rc, st2), g4)],
        out_shape=[jax.ShapeDtypeStruct((SSM_GROUPS, rows, S5_LANES), F32),
                   jax.ShapeDtypeStruct((SSM_GROUPS, 2, rc, st2), F32)],
        scratch_shapes=[pltpu.VMEM((rl, st2), F32), pltpu.VMEM((rl, st2), F32)],
        compiler_params=_cparams(("arbitrary",)),
        name="s5_mixer",
    )(u_g, m_in, m_toep, m_out, coef_a, coef_b, h0)


def _even_out_kernel(*refs, n_ctx_tiles):
    if n_ctx_tiles is None:
        x = refs[0][...]
        refs = refs[1:]
    else:
        x = _pick_rows(refs[0], refs[1], n_ctx_tiles)
        refs = refs[2:]
    ya_ref, yg_ref, u_ref, d_ref, gw_ref, gb_ref, w_ref, mod_ref, o_ref, yt_s = refs
    ys = _group_major_load(yg_ref, yt_s) + d_ref[...] * u_ref[...]
    g = jax.nn.gelu(ys)
    yb = g * jax.nn.sigmoid(jnp.dot(g.astype(BF16), gw_ref[...], preferred_element_type=F32)
                            + gb_ref[...])
    out = jnp.dot(ya_ref[...].astype(BF16), w_ref[:D_LRU, :], preferred_element_type=F32)
    out = out + jnp.dot(yb.astype(BF16), w_ref[D_LRU:, :], preferred_element_type=F32)
    o_ref[...] = x + mod_ref[0, 2:3, :] * out


def even_out(lay, x, y_a, y_g, proj, ssm_d, glu_w_bf16, glu_b, w_out_bf16, mods):
    mi = _mod_index(lay, ROW_TILE)
    c = D_SSM
    row = lambda i: (i, 0)
    const = lambda i: (0, 0)
    if isinstance(x, tuple):
        x_specs, nct = _split_row_specs(lay, ROW_TILE, D_MODEL)
    else:
        x_specs, nct, x = [pl.BlockSpec((ROW_TILE, D_MODEL), row)], None, (x,)
    return pl.pallas_call(
        functools.partial(_even_out_kernel, n_ctx_tiles=nct),
        grid=(lay.t // ROW_TILE,),
        in_specs=x_specs + [
                  pl.BlockSpec((ROW_TILE, c), row),
                  pl.BlockSpec((SSM_GROUPS, ROW_TILE // S5_CHUNK, S5_LANES), lambda i: (0, i, 0)),
                  pl.BlockSpec((ROW_TILE, c), lambda i: (i, 2)),
                  pl.BlockSpec((1, c), const),
                  pl.BlockSpec((c, c), const),
                  pl.BlockSpec((1, c), const),
                  pl.BlockSpec((D_MODEL, D_MODEL), const),
                  pl.BlockSpec((1, N_MOD, D_MODEL), lambda i: (mi(i), 0, 0))],
        out_specs=pl.BlockSpec((ROW_TILE, D_MODEL), row),
        out_shape=jax.ShapeDtypeStruct((lay.t, D_MODEL), F32),
        scratch_shapes=[pltpu.VMEM((SSM_COL_BLOCKS, ROW_TILE, LANES), F32)],
        compiler_params=_cparams(("arbitrary",)),
        name="even_out",
    )(*x, y_a, y_g, proj, ssm_d.reshape(1, -1), glu_w_bf16, glu_b.reshape(1, -1), w_out_bf16, mods)


def _softmax_pv(parts, sink_col):
    m = sink_col
    for s, _ in parts:
        m = jnp.maximum(m, jnp.max(s, axis=-1, keepdims=True))
    den = jnp.exp(sink_col - m)
    acc = None
    for s, v in parts:
        p = jnp.exp(s - m)
        den = den + jnp.sum(p, axis=-1, keepdims=True)
        pv = jnp.dot(p.astype(BF16), v.astype(BF16), preferred_element_type=F32)
        acc = pv if acc is None else acc + pv
    return acc / den


def _nt_dot(a, b):
    return lax.dot_general(a.astype(BF16), b.astype(BF16), (((1,), (1,)), ((), ())),
                           preferred_element_type=F32)


def _attn_ctx_kernel(q_ref, k_ref, v_ref, sink_ref, o_ref):
    n = q_ref.shape[0]
    for kh in range(N_KV):
        k = k_ref[:, kh * HEAD_DIM:(kh + 1) * HEAD_DIM]
        v = v_ref[:, kh * HEAD_DIM:(kh + 1) * HEAD_DIM]
        heads = [kh * GQA + g for g in range(GQA)]
        q = jnp.concatenate([q_ref[:, h * HEAD_DIM:(h + 1) * HEAD_DIM] for h in heads], axis=0)
        sink = jnp.concatenate([jnp.broadcast_to(sink_ref[0:1, h:h + 1], (n, 1)) for h in heads],
                               axis=0)
        o = _softmax_pv([(_nt_dot(q * ATTN_SCALE, k), v)], sink)
        for g, h in enumerate(heads):
            o_ref[:, h * HEAD_DIM:(h + 1) * HEAD_DIM] = o[g * n:(g + 1) * n]


def attn_context(lay, qkv, sink):
    nq = N_HEADS * HEAD_DIM
    nkv = N_KV * HEAD_DIM
    return pl.pallas_call(
        _attn_ctx_kernel,
        grid=(lay.n_ctx,),
        in_specs=[pl.BlockSpec((lay.s_ctx, nq), lambda b: (b, 0)),
                  pl.BlockSpec((lay.s_ctx, nkv), lambda b: (b, nq // nkv)),
                  pl.BlockSpec((lay.s_ctx, nkv), lambda b: (b, nq // nkv + 1)),
                  pl.BlockSpec((1, N_HEADS), lambda b: (0, 0))],
        out_specs=pl.BlockSpec((lay.s_ctx, nq), lambda b: (b, 0)),
        out_shape=jax.ShapeDtypeStruct((lay.t_ctx, nq), F32),
        compiler_params=_cparams(("arbitrary",)),
        name="attn_context",
    )(qkv, qkv, qkv, sink.reshape(1, -1))


def _rope(x, cos, sin):
    lane = lax.broadcasted_iota(I32, (x.shape[0], 2 * HEAD_DIM), 1) % HEAD_DIM
    outs = []
    for j in range(x.shape[1] // (2 * HEAD_DIM)):
        xs = x[:, j * 2 * HEAD_DIM:(j + 1) * 2 * HEAD_DIM]
        sw = jnp.where(lane < HEAD_DIM // 2,
                       pltpu.roll(xs, 2 * HEAD_DIM - HEAD_DIM // 2, 1),
                       pltpu.roll(xs, HEAD_DIM // 2, 1))
        outs.append(xs * cos + sw * sin)
    return outs


def _attn_lat_kernel(q_ref, k0_ref, k1_ref, k2_ref, v0_ref, v1_ref, v2_ref, ck_ref, cv_ref,
                     cq_ref, sq_ref, c0_ref, c1_ref, c2_ref, s0_ref, s1_ref, s2_ref, sink_ref,
                     o_ref, *, n_blk):
    j = pl.program_id(1)
    qb = Q_BLOCK
    q_parts = [qp * ATTN_SCALE for qp in _rope(q_ref[...], cq_ref[...], sq_ref[...])]
    k_parts = [_rope(kr[...], cr[...], sr[...])
               for kr, cr, sr in ((k0_ref, c0_ref, s0_ref), (k1_ref, c1_ref, s1_ref),
                                  (k2_ref, c2_ref, s2_ref))]
    qi = lax.broadcasted_iota(I32, (qb, 3 * qb), 0)
    km = lax.broadcasted_iota(I32, (qb, 3 * qb), 1)
    kpos = j * qb - qb + km
    mask1 = (jnp.abs(km - qb - qi) <= WINDOW) & (kpos >= 0) & (kpos < n_blk * qb)
    mask = jnp.concatenate([mask1] * GQA, axis=0)
    for kh in range(N_KV):
        half = (kh % 2) * HEAD_DIM
        k_loc = jnp.concatenate([kp[kh // 2][:, half:half + HEAD_DIM] for kp in k_parts], axis=0)
        v_loc = jnp.concatenate([vr[:, kh * HEAD_DIM:(kh + 1) * HEAD_DIM]
                                 for vr in (v0_ref, v1_ref, v2_ref)], axis=0)
        qs, sinks = [], []
        for g in range(GQA):
            h = kh * GQA + g
            qs.append(q_parts[h // 2][:, (h % 2) * HEAD_DIM:(h % 2 + 1) * HEAD_DIM])
            sinks.append(jnp.broadcast_to(sink_ref[0:1, h:h + 1], (qb, 1)))
        q = jnp.concatenate(qs, axis=0)
        sink = jnp.concatenate(sinks, axis=0)
        s_loc = jnp.where(mask, _nt_dot(q, k_loc), NEG_INF)
        s_ctx = _nt_dot(q, ck_ref[0, kh])
        o = _softmax_pv([(s_loc, v_loc), (s_ctx, cv_ref[0, kh])], sink)
        for g in range(GQA):
            h = kh * GQA + g
            o_ref[:, h * HEAD_DIM:(h + 1) * HEAD_DIM] = o[g * qb:(g + 1) * qb]


def _rope_tables(s_len):
    rows = s_len // GRID_W
    row = jnp.repeat(jnp.arange(rows), GRID_W).astype(F32)
    col = jnp.tile(jnp.arange(GRID_W), rows).astype(F32)
    nf = HEAD_DIM // 4
    inv = ROPE_BASE ** (-jnp.arange(nf, dtype=F32) / nf)
    ang = jnp.concatenate([row[:, None] * inv, col[:, None] * inv], axis=-1)
    cos, sin = jnp.cos(ang), jnp.sin(ang)
    cos2 = jnp.tile(jnp.concatenate([cos, cos], axis=-1), (1, 2))
    sin2 = jnp.tile(jnp.concatenate([-sin, sin], axis=-1), (1, 2))
    return cos2, sin2


def attn_latent(lay, qkv, cache_k, cache_v, sink):
    nq = N_HEADS * HEAD_DIM
    nkv = N_KV * HEAD_DIM
    n_blk = lay.s_lat // Q_BLOCK
    base = lay.t_ctx // Q_BLOCK
    n_ctx_keys = cache_k.shape[2]
    cos2, sin2 = _rope_tables(lay.s_lat)
    kcol = nq // nkv

    def qrow(b, j):
        return base + b * n_blk + j

    def krow(off):
        return lambda b, j: base + b * n_blk + jnp.clip(j + off, 0, n_blk - 1)

    def trow(off):
        return lambda b, j: (jnp.clip(j + off, 0, n_blk - 1), 0)

    kv_spec = lambda off, col: pl.BlockSpec((Q_BLOCK, nkv), lambda b, j: (krow(off)(b, j), col))
    tab = lambda off: pl.BlockSpec((Q_BLOCK, 2 * HEAD_DIM), trow(off))
    cache_spec = pl.BlockSpec((1, N_KV, n_ctx_keys, HEAD_DIM), lambda b, j: (b, 0, 0, 0))
    return pl.pallas_call(
        functools.partial(_attn_lat_kernel, n_blk=n_blk),
        grid=(lay.n_lat, n_blk),
        in_specs=[pl.BlockSpec((Q_BLOCK, nq), lambda b, j: (qrow(b, j), 0)),
                  kv_spec(-1, kcol), kv_spec(0, kcol), kv_spec(1, kcol),
                  kv_spec(-1, kcol + 1), kv_spec(0, kcol + 1), kv_spec(1, kcol + 1),
                  cache_spec, cache_spec,
                  tab(0), tab(0), tab(-1), tab(0), tab(1), tab(-1), tab(0), tab(1),
                  pl.BlockSpec((1, N_HEADS), lambda b, j: (0, 0))],
        out_specs=pl.BlockSpec((Q_BLOCK, nq), lambda b, j: (b * n_blk + j, 0)),
        out_shape=jax.ShapeDtypeStruct((lay.t_lat, nq), F32),
        compiler_params=_cparams(("arbitrary", "arbitrary")),
        name="attn_latent",
    )(qkv, qkv, qkv, qkv, qkv, qkv, qkv, cache_k, cache_v,
      cos2, sin2, cos2, cos2, cos2, sin2, sin2, sin2, sink.reshape(1, -1))


def _mm_res_kernel(x_ref, ac_ref, al_ref, w_ref, mod_ref, o_ref, *, n_ctx_tiles):
    a = jnp.where(pl.program_id(0) < n_ctx_tiles, ac_ref[...], al_ref[...])
    out = jnp.dot(a.astype(BF16), w_ref[...], preferred_element_type=F32)
    o_ref[...] = x_ref[...] + mod_ref[0, 2:3, :] * out


def matmul_residual(lay, x, a_ctx, a_lat, w_bf16, mods):
    mi = _mod_index(lay, ROW_TILE)
    k = a_ctx.shape[1]
    nct = lay.t_ctx // ROW_TILE
    return pl.pallas_call(
        functools.partial(_mm_res_kernel, n_ctx_tiles=nct),
        grid=(lay.t // ROW_TILE,),
        in_specs=[pl.BlockSpec((ROW_TILE, D_MODEL), lambda i: (i, 0)),
                  pl.BlockSpec((ROW_TILE, k), lambda i: (jnp.minimum(i, nct - 1), 0)),
                  pl.BlockSpec((ROW_TILE, k), lambda i: (jnp.maximum(i - nct, 0), 0)),
                  pl.BlockSpec((k, D_MODEL), lambda i: (0, 0)),
                  pl.BlockSpec((1, N_MOD, D_MODEL), lambda i: (mi(i), 0, 0))],
        out_specs=pl.BlockSpec((ROW_TILE, D_MODEL), lambda i: (i, 0)),
        out_shape=jax.ShapeDtypeStruct((lay.t, D_MODEL), F32),
        compiler_params=_cparams(("arbitrary",)),
        name="matmul_residual",
    )(x, a_ctx, a_lat, w_bf16, mods)


def _rowtile_load(ref, n, base=0):
    parts = []
    for c in range(ROW_CHUNKS):
        words = ref[pl.ds(base + c, n, stride=ROW_CHUNKS), :]
        for half in range(2):
            parts.append(pltpu.unpack_elementwise(words, index=half, packed_dtype=BF16,
                                                  unpacked_dtype=F32))
    return jnp.concatenate(parts, axis=1)


def _rowtile_store(ref, val, n):
    for c in range(ROW_CHUNKS):
        lo = val[:, 2 * c * LANES:(2 * c + 1) * LANES]
        hi = val[:, (2 * c + 1) * LANES:(2 * c + 2) * LANES]
        ref[pl.ds(c, n, stride=ROW_CHUNKS), :] = pltpu.pack_elementwise([lo, hi], packed_dtype=BF16)


def _row_copy(src, src_row, dst, dst_row, sem):
    return pltpu.make_async_copy(
        src.at[pl.ds(pl.multiple_of(src_row * ROW_CHUNKS, ROW_CHUNKS), ROW_CHUNKS), :],
        dst.at[pl.ds(pl.multiple_of(dst_row * ROW_CHUNKS, ROW_CHUNKS), ROW_CHUNKS), :], sem)


def _router_kernel(x_ref, g_ref, mod_ref, rwt_ref, rb_ref, tri_ref, sg_ref, su_ref, sd_ref,
                   hn_ref, sh_ref, eidx_ref, wts_ref, rank_ref, cnt_ref, cnt_s):
    tm = ROUTER_TILE

    @pl.when(pl.program_id(0) == 0)
    def _():
        cnt_s[...] = jnp.zeros_like(cnt_s)

    h = _modnorm(x_ref[...], g_ref[...], mod_ref, 3)
    _rowtile_store(hn_ref, h, tm)
    hb = h.astype(BF16)
    sgate = jnp.dot(hb, sg_ref[...], preferred_element_type=F32)
    sup = jnp.dot(hb, su_ref[...], preferred_element_type=F32)
    sh_ref[...] = jnp.dot((sgate * jax.nn.sigmoid(sgate) * sup).astype(BF16), sd_ref[...],
                          preferred_element_type=F32)
    logits = lax.dot_general(rwt_ref[...], h, (((1,), (1,)), ((), ())),
                             precision=HIGHEST, preferred_element_type=F32)
    scores = jax.nn.sigmoid(logits)
    choice = scores + rb_ref[...]
    gs_rows = []
    for g in range(N_GROUPS):
        cg = choice[g * GROUP_SIZE:(g + 1) * GROUP_SIZE, :]
        m1 = jnp.max(cg, axis=0, keepdims=True)
        eq = cg == m1
        cnt = jnp.sum(eq.astype(F32), axis=0, keepdims=True)
        m2 = jnp.max(jnp.where(eq, -jnp.inf, cg), axis=0, keepdims=True)
        gs_rows.append(m1 + jnp.where(cnt >= 2.0, m1, m2))
    gs = jnp.concatenate(gs_rows, axis=0)
    gi = lax.broadcasted_iota(I32, (N_GROUPS, tm), 0)
    grank = jnp.zeros((N_GROUPS, tm), I32)
    for g in range(N_GROUPS):
        other = gs[g:g + 1, :]
        ahead = (other > gs) | ((other == gs) & (g < gi))
        grank = grank + ahead.astype(I32)
    gsel = grank < TOPK_GROUPS
    emask = jnp.concatenate(
        [jnp.broadcast_to(gsel[g:g + 1, :], (GROUP_SIZE, tm)) for g in range(N_GROUPS)], axis=0)
    masked = jnp.where(emask, choice, -jnp.inf)
    ei = lax.broadcasted_iota(I32, (N_EXPERTS, tm), 0)
    idxs, ws = [], []
    member = jnp.zeros((N_EXPERTS, tm), F32)
    for _ in range(TOP_K):
        m = jnp.max(masked, axis=0, keepdims=True)
        idx = jnp.min(jnp.where(masked == m, ei, N_EXPERTS), axis=0, keepdims=True)
        hit = ei == idx
        ws.append(jnp.sum(jnp.where(hit, scores, 0.0), axis=0, keepdims=True))
        idxs.append(idx)
        member = jnp.where(hit, 1.0, member)
        masked = jnp.where(hit, -jnp.inf, masked)
    w = jnp.concatenate(ws, axis=0)
    wts_ref[...] = w / jnp.sum(w, axis=0, keepdims=True) * ROUTE_SCALE
    eidx_ref[...] = jnp.concatenate(idxs, axis=0)
    before = jnp.dot(member.astype(BF16), tri_ref[...], preferred_element_type=F32) + cnt_s[...]
    ranks = [jnp.sum(jnp.where(ei == idx, before, 0.0), axis=0, keepdims=True) for idx in idxs]
    rank_ref[...] = jnp.concatenate(ranks, axis=0).astype(I32)
    cnt_s[...] = cnt_s[...] + jnp.sum(member, axis=1, keepdims=True)
    cnt_ref[...] = jnp.broadcast_to(cnt_s[...], cnt_ref.shape)


def moe_router(lay, x, g, mods, router_w, router_b, sg_bf16, su_bf16, sd_bf16):
    t = lay.t
    tm = ROUTER_TILE
    mi = _mod_index(lay, tm)
    tri = (jnp.arange(tm)[:, None] < jnp.arange(tm)[None, :]).astype(BF16)
    tok = lambda i: (0, i)
    const = lambda i: (0, 0)
    return pl.pallas_call(
        _router_kernel,
        grid=(t // tm,),
        in_specs=[pl.BlockSpec((tm, D_MODEL), lambda i: (i, 0)),
                  pl.BlockSpec((1, D_MODEL), const),
                  pl.BlockSpec((1, N_MOD, D_MODEL), lambda i: (mi(i), 0, 0)),
                  pl.BlockSpec((N_EXPERTS, D_MODEL), const),
                  pl.BlockSpec((N_EXPERTS, 1), const),
                  pl.BlockSpec((tm, tm), const),
                  pl.BlockSpec((D_MODEL, D_EXPERT), const),
                  pl.BlockSpec((D_MODEL, D_EXPERT), const),
                  pl.BlockSpec((D_EXPERT, D_MODEL), const)],
        out_specs=[pl.BlockSpec((tm * ROW_CHUNKS, LANES), lambda i: (i, 0)),
                   pl.BlockSpec((tm, D_MODEL), lambda i: (i, 0)),
                   pl.BlockSpec((TOP_K, tm), tok),
                   pl.BlockSpec((TOP_K, tm), tok),
                   pl.BlockSpec((TOP_K, tm), tok),
                   pl.BlockSpec((N_EXPERTS, LANES), const)],
        out_shape=[jax.ShapeDtypeStruct((t * ROW_CHUNKS, LANES), PACKED),
                   jax.ShapeDtypeStruct((t, D_MODEL), F32),
                   jax.ShapeDtypeStruct((TOP_K, t), I32),
                   jax.ShapeDtypeStruct((TOP_K, t), F32),
                   jax.ShapeDtypeStruct((TOP_K, t), I32),
                   jax.ShapeDtypeStruct((N_EXPERTS, LANES), F32)],
        scratch_shapes=[pltpu.VMEM((N_EXPERTS, 1), F32)],
        compiler_params=_cparams(("arbitrary",)),
        name="moe_router",
    )(x, g.reshape(1, -1), mods, router_w.T, router_b.reshape(-1, 1), tri, sg_bf16, su_bf16, sd_bf16)


def _dest_kernel(start_ref, eidx_ref, rank_ref, dest_ref):
    e = eidx_ref[...]

    def body(i, acc):
        return jnp.where(e == i, start_ref[i], acc)

    dest_ref[...] = lax.fori_loop(0, N_EXPERTS, body, jnp.zeros_like(e), unroll=8) + rank_ref[...]


def moe_dest(pad_start, eidx, rank):
    t = eidx.shape[1]
    tn = DEST_TILE
    spec = pl.BlockSpec((TOP_K, tn), lambda i, ps: (0, i))
    return pl.pallas_call(
        _dest_kernel,
        grid_spec=pltpu.PrefetchScalarGridSpec(
            num_scalar_prefetch=1, grid=(t // tn,), in_specs=[spec, spec], out_specs=spec),
        out_shape=jax.ShapeDtypeStruct((TOP_K, t), I32),
        compiler_params=_cparams(("arbitrary",)),
        name="moe_dest",
    )(pad_start, eidx, rank)


def _issue_row_copies(idx_at, n, copy_at, unroll=4):
    def body(i, c):
        for p in range(2):
            r = 2 * i + p
            copy_at(r, idx_at(r)).start(priority=p)
        return c
    lax.fori_loop(0, n // 2, body, 0, unroll=unroll)


def _dispatch_kernel(zrow_ref, dest_hbm, hn_ref, xs_hbm, idx_s, zbuf, isem, zsem, ssem, *, n_tiles):
    i = pl.program_id(0)
    slot = i % 2
    td = DISPATCH_TILE

    n_idx = TOP_K * td

    def idx_copy(tile, s):
        return pltpu.make_async_copy(dest_hbm.at[tile], idx_s.at[pl.ds(s * n_idx, n_idx)], isem.at[s])

    def zero_copy(e):
        r0 = pl.multiple_of(zrow_ref[e] * ROW_CHUNKS, ROW_CHUNKS)
        return pltpu.make_async_copy(zbuf, xs_hbm.at[pl.ds(r0, MOE_BLOCK * ROW_CHUNKS), :], zsem)

    @pl.when(i == 0)
    def _():
        zbuf[...] = jnp.zeros_like(zbuf)

        def zstart(e, c):
            @pl.when(zrow_ref[e] >= 0)
            def _():
                zero_copy(e).start()
            return c

        def zwait(e, c):
            @pl.when(zrow_ref[e] >= 0)
            def _():
                zero_copy(e).wait()
            return c

        lax.fori_loop(0, zrow_ref.shape[0], zstart, 0)
        idx_copy(0, 0).start()
        lax.fori_loop(0, zrow_ref.shape[0], zwait, 0)

    idx_copy(i, slot).wait()

    @pl.when(i + 1 < n_tiles)
    def _():
        idx_copy(i + 1, 1 - slot).start()

    for k in range(TOP_K):
        _issue_row_copies(lambda r: idx_s[slot * n_idx + k * td + r], td,
                          lambda r, d: _row_copy(hn_ref, r, xs_hbm, d, ssem))
    for k in range(TOP_K):
        pltpu.make_async_copy(hn_ref, xs_hbm.at[pl.ds(0, td * ROW_CHUNKS), :], ssem).wait()


def _tile_major(dest, tile):
    t = dest.shape[1]
    return dest.reshape(TOP_K, t // tile, tile).transpose(1, 0, 2).reshape(t // tile, TOP_K * tile)


def moe_dispatch(lay, hn, dest, zero_row, n_rows):
    td = DISPATCH_TILE
    n_tiles = lay.t // td
    return pl.pallas_call(
        functools.partial(_dispatch_kernel, n_tiles=n_tiles),
        grid_spec=pltpu.PrefetchScalarGridSpec(
            num_scalar_prefetch=1,
            grid=(n_tiles,),
            in_specs=[pl.BlockSpec(memory_space=pl.ANY),
                      pl.BlockSpec((td * ROW_CHUNKS, LANES), lambda i, z: (i, 0))],
            out_specs=pl.BlockSpec(memory_space=pl.ANY),
            scratch_shapes=[pltpu.SMEM((2 * TOP_K * td,), I32),
                            pltpu.VMEM((MOE_BLOCK * ROW_CHUNKS, LANES), PACKED),
                            pltpu.SemaphoreType.DMA((2,)),
                            pltpu.SemaphoreType.DMA,
                            pltpu.SemaphoreType.DMA]),
        out_shape=jax.ShapeDtypeStruct((n_rows * ROW_CHUNKS, LANES), PACKED),
        compiler_params=_cparams(("arbitrary",)),
        name="moe_dispatch",
    )(zero_row, dest, hn)


def _expert_kernel(blk0_ref, nblk_ref, tail_ref, xs_hbm, wg_ref, wu_ref, wd_ref, y_hbm,
                   xbuf, ybuf, wg_s, wu_s, wd_s, isem, osem):
    e = pl.program_id(0)
    nb = nblk_ref[e]
    g0 = blk0_ref[e]
    total = blk0_ref[N_EXPERTS - 1] + nblk_ref[N_EXPERTS - 1]
    blk_rows = MOE_BLOCK * ROW_CHUNKS
    n_x = xbuf.shape[0]
    n_y = ybuf.shape[0]

    def block_rows(g):
        return pl.ds(pl.multiple_of(g * blk_rows, blk_rows), blk_rows)

    def fetch(g):
        s = g % n_x
        return pltpu.make_async_copy(xs_hbm.at[block_rows(g), :], xbuf.at[s], isem.at[s])

    def writeback(g):
        s = g % n_y
        return pltpu.make_async_copy(ybuf.at[s], y_hbm.at[block_rows(g), :], osem.at[s])

    ahead = n_x // 2

    @pl.when(e == 0)
    def _():
        for p in range(ahead):
            @pl.when(p < total)
            def _():
                fetch(p).start()

    @pl.when(nb > 0)
    def _():
        wg_s[...] = wg_ref[0, 0].astype(BF16)
        wu_s[...] = wu_ref[0, 0].astype(BF16)
        wd_s[...] = wd_ref[0, 0].astype(BF16)

    def run_blocks(g, n):
        for p in range(n):
            @pl.when(g + ahead + p < total)
            def _():
                fetch(g + ahead + p).start()
        for p in range(n):
            fetch(g + p).wait()
        x = jnp.concatenate([_rowtile_load(xbuf.at[(g + p) % n_x], MOE_BLOCK) for p in range(n)],
                            axis=0).astype(BF16)
        gate = jnp.dot(x, wg_s[...], preferred_element_type=F32)
        up = jnp.dot(x, wu_s[...], preferred_element_type=F32)
        act = gate * jax.nn.sigmoid(gate) * up
        y = jnp.dot(act.astype(BF16), wd_s[...], preferred_element_type=F32)
        for p in range(n):
            @pl.when(g + p >= n_y)
            def _():
                writeback(g + p - n_y).wait()
        for p in range(n):
            _rowtile_store(ybuf.at[(g + p) % n_y], y[p * MOE_BLOCK:(p + 1) * MOE_BLOCK], MOE_BLOCK)
            writeback(g + p).start()

    def quad(jj, c):
        run_blocks(g0 + 4 * jj, 4)
        return c

    lax.fori_loop(0, nb // 4, quad, 0)
    rem = nb % 4

    @pl.when(rem >= 2)
    def _():
        run_blocks(g0 + nb - rem, 2)

    @pl.when(rem % 2 == 1)
    def _():
        run_blocks(g0 + nb - 1, 1)

    @pl.when(e == N_EXPERTS - 1)
    def _():
        for p in range(n_y, 0, -1):
            @pl.when(total >= p)
            def _():
                writeback(total - p).wait()

        ybuf[0] = jnp.zeros(ybuf.shape[1:], PACKED)

        def tail_copy(i):
            r0 = pl.multiple_of(tail_ref[i] * ROW_CHUNKS, blk_rows)
            return pltpu.make_async_copy(ybuf.at[0], y_hbm.at[pl.ds(r0, blk_rows), :], osem.at[0])

        def tstart(i, c):
            @pl.when(tail_ref[i] >= 0)
            def _():
                tail_copy(i).start()
            return c

        def twait(i, c):
            @pl.when(tail_ref[i] >= 0)
            def _():
                tail_copy(i).wait()
            return c

        lax.fori_loop(0, tail_ref.shape[0], tstart, 0)
        lax.fori_loop(0, tail_ref.shape[0], twait, 0)


def moe_experts(xs, first_block, n_blocks, tail_row, layer, w_gate, w_up, w_down):
    wspec = lambda shape: pl.BlockSpec((1, 1) + shape, lambda e, a, b, c: (layer, e, 0, 0))
    blk = (MOE_BLOCK * ROW_CHUNKS, LANES)
    return pl.pallas_call(
        _expert_kernel,
        grid_spec=pltpu.PrefetchScalarGridSpec(
            num_scalar_prefetch=3,
            grid=(N_EXPERTS,),
            in_specs=[pl.BlockSpec(memory_space=pl.ANY),
                      wspec((D_MODEL, D_EXPERT)), wspec((D_MODEL, D_EXPERT)),
                      wspec((D_EXPERT, D_MODEL))],
            out_specs=pl.BlockSpec(memory_space=pl.ANY),
            scratch_shapes=[pltpu.VMEM((EXPERT_X_BUFS,) + blk, PACKED),
                            pltpu.VMEM((EXPERT_Y_BUFS,) + blk, PACKED),
                            pltpu.VMEM((D_MODEL, D_EXPERT), BF16),
                            pltpu.VMEM((D_MODEL, D_EXPERT), BF16),
                            pltpu.VMEM((D_EXPERT, D_MODEL), BF16),
                            pltpu.SemaphoreType.DMA((EXPERT_X_BUFS,)),
                            pltpu.SemaphoreType.DMA((EXPERT_Y_BUFS,))]),
        out_shape=jax.ShapeDtypeStruct(xs.shape, PACKED),
        compiler_params=_cparams(("arbitrary",)),
        name="moe_experts",
    )(first_block, n_blocks, tail_row, xs, w_gate, w_up, w_down)


def _combine_kernel(dest_hbm, y_hbm, x_ref, sh_ref, w_ref, mod_ref, *rest, n_tiles, final, proj):
    if final:
        gf_ref, oc_ref, ol_ref, idx_s, ybuf, isem, gsem = rest
    elif proj:
        gn_ref, modn_ref, wn_ref, o_ref, p_ref, idx_s, ybuf, isem, gsem = rest
    else:
        o_ref, idx_s, ybuf, isem, gsem = rest
    i = pl.program_id(0)
    tm = COMBINE_TILE
    n_idx = TOP_K * tm
    last = n_tiles - 1

    def idx_copy(tile, s):
        return pltpu.make_async_copy(dest_hbm.at[tile], idx_s.at[pl.ds(s * n_idx, n_idx)], isem.at[s])

    def gather(s, unroll=4):
        _issue_row_copies(lambda r: idx_s[s * n_idx + r], n_idx,
                          lambda r, d: _row_copy(y_hbm, d, ybuf.at[s], r, gsem.at[s]), unroll)

    def gather_wait(s):
        pltpu.make_async_copy(y_hbm.at[pl.ds(0, n_idx * ROW_CHUNKS), :], ybuf.at[s], gsem.at[s]).wait()

    @pl.when(i == 0)
    def _():
        c = idx_copy(0, 0)
        c.start()
        c.wait()
        gather(0)
        idx_copy(jnp.minimum(1, last), 1).start()

    for half in range(2):
        tile = 2 * i + half
        s = half
        rows = slice(half * tm, (half + 1) * tm)
        idx_copy(jnp.minimum(tile + 1, last), 1 - s).wait()
        gather_wait(s)
        gather(1 - s, unroll=True)
        idx_copy(jnp.minimum(tile + 2, last), s).start()

        w = w_ref[rows, :]
        routed = jnp.zeros((tm, D_MODEL), F32)
        for k in range(TOP_K):
            routed = routed + w[:, k:k + 1] * _rowtile_load(ybuf.at[s], tm, base=k * tm * ROW_CHUNKS)
        out = x_ref[rows, :] + mod_ref[0, 5:6, :] * (routed + sh_ref[rows, :])
        if final:
            ms = jnp.mean(out * out, axis=-1, keepdims=True)
            y = out * lax.rsqrt(ms + EPS) * gf_ref[...]
            oc_ref[rows, :] = y
            ol_ref[rows, :] = y
        else:
            o_ref[rows, :] = out
            if proj:
                hn = _modnorm(out, gn_ref[...], modn_ref, 0)
                p_ref[rows, :] = jnp.dot(hn.astype(BF16), wn_ref[...], preferred_element_type=F32)

    @pl.when(i == n_tiles // 2 - 1)
    def _():
        gather_wait(0)
        idx_copy(last, 1).wait()


def moe_combine(lay, x, shared, y_rows, dest, wts_t, mods, g_final=None, next_proj=None):
    tm = COMBINE_TILE
    n_tiles = lay.t // tm
    assert n_tiles % 2 == 0
    step = 2 * tm
    nct = lay.t_ctx // step
    mi = _mod_index(lay, step)
    row = lambda i: (i, 0)
    final = g_final is not None
    if final:
        extra_in = [pl.BlockSpec((1, D_MODEL), lambda i: (0, 0))]
        extra_args = [g_final.reshape(1, -1)]
        out_specs = [pl.BlockSpec((step, D_MODEL), lambda i: (jnp.minimum(i, nct), 0)),
                     pl.BlockSpec((step, D_MODEL), lambda i: (jnp.maximum(i - nct, 0), 0))]
        out_shape = [jax.ShapeDtypeStruct((lay.t_ctx + step, D_MODEL), F32),
                     jax.ShapeDtypeStruct((lay.t_lat, D_MODEL), F32)]
    elif next_proj is not None:
        g_next, mods_next, w_next = next_proj
        n_out = w_next.shape[1]
        extra_in = [pl.BlockSpec((1, D_MODEL), lambda i: (0, 0)),
                    pl.BlockSpec((1, N_MOD, D_MODEL), lambda i: (mi(i), 0, 0)),
                    pl.BlockSpec((D_MODEL, n_out), lambda i: (0, 0))]
        extra_args = [g_next.reshape(1, -1), mods_next, w_next]
        out_specs = [pl.BlockSpec((step, D_MODEL), row), pl.BlockSpec((step, n_out), row)]
        out_shape = [jax.ShapeDtypeStruct((lay.t, D_MODEL), F32),
                     jax.ShapeDtypeStruct((lay.t, n_out), F32)]
    else:
        extra_in, extra_args = [], []
        out_specs = pl.BlockSpec((step, D_MODEL), row)
        out_shape = jax.ShapeDtypeStruct((lay.t, D_MODEL), F32)
    return pl.pallas_call(
        functools.partial(_combine_kernel, n_tiles=n_tiles, final=final,
                          proj=next_proj is not None and not final),
        grid=(n_tiles // 2,),
        in_specs=[pl.BlockSpec(memory_space=pl.ANY),
                  pl.BlockSpec(memory_space=pl.ANY),
                  pl.BlockSpec((step, D_MODEL), row),
                  pl.BlockSpec((step, D_MODEL), row),
                  pl.BlockSpec((step, TOP_K), row),
                  pl.BlockSpec((1, N_MOD, D_MODEL), lambda i: (mi(i), 0, 0))] + extra_in,
        out_specs=out_specs,
        out_shape=out_shape,
        scratch_shapes=[pltpu.SMEM((2 * TOP_K * tm,), I32),
                        pltpu.VMEM((2, TOP_K * tm * ROW_CHUNKS, LANES), PACKED),
                        pltpu.SemaphoreType.DMA((2,)),
                        pltpu.SemaphoreType.DMA((2,))],
        compiler_params=_cparams(("arbitrary",)),
        name="moe_combine",
    )(dest, y_rows, x, shared, wts_t, mods, *extra_args)


def moe_layer(lay, x, g, mods, router_w, router_b, layer, w_gate, w_up, w_down, s_gate, s_up, s_down,
              g_final=None, next_proj=None):
    t = lay.t
    hn, shared, eidx, wts, rank, cnt = moe_router(
        lay, x, g, mods, router_w, router_b, s_gate.astype(BF16), s_up.astype(BF16), s_down.astype(BF16))
    counts = cnt[:, 0].astype(I32)
    n_blocks = (counts + MOE_BLOCK - 1) // MOE_BLOCK
    padded = n_blocks * MOE_BLOCK
    pad_end = jnp.cumsum(padded)
    pad_start = pad_end - padded
    n_rows = -(-(t * TOP_K + N_EXPERTS * (MOE_BLOCK - 1)) // MOE_BLOCK) * MOE_BLOCK
    dest = moe_dest(pad_start, eidx, rank)
    last_row = jnp.where(n_blocks > 0, pad_end - MOE_BLOCK, -1)
    tail_blk = pad_end[-1] // MOE_BLOCK + jnp.arange(n_rows // MOE_BLOCK - t * TOP_K // MOE_BLOCK)
    tail_row = jnp.where(tail_blk < n_rows // MOE_BLOCK, tail_blk * MOE_BLOCK, -1).astype(I32)
    xs = moe_dispatch(lay, hn, _tile_major(dest, DISPATCH_TILE),
                      jnp.concatenate([last_row, tail_row]), n_rows)
    y_rows = moe_experts(xs, pad_start // MOE_BLOCK, n_blocks, tail_row, layer, w_gate, w_up, w_down)
    return moe_combine(lay, x, shared, y_rows, _tile_major(dest, COMBINE_TILE), wts.T, mods,
                       g_final, next_proj)


def _block_diag(w):
    nb, bw, _ = w.shape
    eye = jnp.eye(nb, dtype=w.dtype)
    return (eye[:, None, :, None] * w[:, :, None, :]).reshape(nb * bw, nb * bw)


def even_layer(lay, x, mods, g_mix, p, state_lru, state_ssm_re, state_ssm_im):
    t = lay.t
    proj, u_g = modnorm_matmul(lay, x, g_mix, mods, 0, p['w_in'].astype(BF16),
                               ug_col=2 * D_LRU)
    zeros_c = jnp.zeros((lay.n_ctx, D_LRU), F32)
    hf_y, st = None, []
    for d in range(2):
        wg = jnp.concatenate([_block_diag(p['lru_wa'][d]), _block_diag(p['lru_wx'][d])], axis=1)
        bg = jnp.concatenate([p['lru_ba'][d], p['lru_bx'][d]])
        h0 = jnp.concatenate([zeros_c, state_lru[:, d].astype(F32)], axis=0)
        hf_y, s = lru_pass(lay, proj, p['conv_w'], p['conv_b'], wg.astype(BF16), bg,
                           p['lru_lam'][d], h0, reverse=(d == 1), hf=hf_y)
        st.append(s[:lay.n_ctx])
    y_a = hf_y
    new_lru = jnp.stack(st, axis=1)

    mats = _s5_matrices(p['a_re'], p['a_im'], p['log_dt'], p['b_re'], p['b_im'], p['c_re'], p['c_im'])
    h0 = jnp.concatenate([state_ssm_re, state_ssm_im], axis=-1).astype(F32)
    h0 = h0.transpose(2, 1, 0, 3)
    y_g, h_ctx = s5_mixer(lay, u_g, mats, h0)
    seg = lay.s_ctx // S5_CHUNK
    h_ctx = h_ctx.reshape(SSM_GROUPS, 2, lay.n_ctx, seg, 2 * SSM_STATE)
    ends = jnp.stack([h_ctx[:, 0, :, seg - 1], h_ctx[:, 1, :, 0]], axis=1)
    ends = ends.transpose(2, 1, 0, 3)
    x = even_out(lay, x, y_a, y_g, proj, p['d'], p['glu_w'].astype(BF16), p['glu_b'],
                 p['w_out'].astype(BF16), mods)
    return x, new_lru, ends[..., :SSM_STATE], ends[..., SSM_STATE:]


def odd_layer(lay, x, mods, g_mix, w_qkv, sink, w_out, cache_k, cache_v, qkv=None):
    if qkv is None:
        qkv = modnorm_matmul(lay, x, g_mix, mods, 0, w_qkv.astype(BF16))
    o_ctx = attn_context(lay, qkv, sink)
    o_lat = attn_latent(lay, qkv, cache_k, cache_v, sink)
    nq = N_HEADS * HEAD_DIM
    kv = qkv[:lay.t_ctx, nq:].reshape(lay.n_ctx, lay.s_ctx, 2, N_KV, HEAD_DIM)
    k_new = kv[:, :, 0].swapaxes(1, 2)
    v_new = kv[:, :, 1].swapaxes(1, 2)
    x = matmul_residual(lay, x, o_ctx, o_lat, w_out.astype(BF16), mods)
    return x, k_new, v_new


def _forward(lay, x_prompt, x_sample, state_lru, state_ssm_re, state_ssm_im, cache_k, cache_v,
             c, c_ctx, g_mix, g_ffn, w_mod, b_mod,
             ev_w_in, lru_conv_w, lru_conv_b, lru_wa, lru_ba, lru_wx, lru_bx, lru_lam,
             ssm_a_re, ssm_a_im, ssm_log_dt, ssm_b_re, ssm_b_im, ssm_c_re, ssm_c_im, ssm_d,
             ssm_glu_w, ssm_glu_b, ev_w_out, at_w_qkv, at_sink, at_w_out,
             router_w, router_b, exp_w_gate, exp_w_up, exp_w_down, sh_w_gate, sh_w_up, sh_w_down,
             g_final):
    depth = g_mix.shape[0]
    x = (x_prompt.reshape(lay.t_ctx, D_MODEL), x_sample.reshape(lay.t_lat, D_MODEL))
    n_c = 1 + lay.n_lat
    c_rows = jnp.concatenate([c_ctx[None, :], c, jnp.zeros((16 - n_c, D_MODEL), F32)], axis=0)
    new_lru, new_re, new_im, new_k, new_v = [], [], [], [], []
    all_mods = [adaln_table(c_rows, l, w_mod, b_mod[l]) for l in range(depth)]
    qkv = None
    for l in range(depth):
        i = l // 2
        mods = all_mods[l]
        if l % 2 == 0:
            p = dict(w_in=ev_w_in[i], conv_w=lru_conv_w[i], conv_b=lru_conv_b[i],
                     lru_wa=lru_wa[i], lru_ba=lru_ba[i], lru_wx=lru_wx[i], lru_bx=lru_bx[i],
                     lru_lam=lru_lam[i], a_re=ssm_a_re[i], a_im=ssm_a_im[i], log_dt=ssm_log_dt[i],
                     b_re=ssm_b_re[i], b_im=ssm_b_im[i], c_re=ssm_c_re[i], c_im=ssm_c_im[i],
                     d=ssm_d[i], glu_w=ssm_glu_w[i], glu_b=ssm_glu_b[i], w_out=ev_w_out[i])
            x, lru_i, re_i, im_i = even_layer(lay, x, mods, g_mix[l], p, state_lru[:, i],
                                              state_ssm_re[:, i], state_ssm_im[:, i])
            new_lru.append(lru_i)
            new_re.append(re_i)
            new_im.append(im_i)
        else:
            if isinstance(x, tuple):
                x = jnp.concatenate(x, axis=0)
            x, k_i, v_i = odd_layer(lay, x, mods, g_mix[l], at_w_qkv[i], at_sink[i], at_w_out[i],
                                    cache_k[:, i], cache_v[:, i], qkv)
            new_k.append(k_i)
            new_v.append(v_i)
        next_proj = None
        if l + 1 < depth and (l + 1) % 2 == 1:
            next_proj = (g_mix[l + 1], all_mods[l + 1], at_w_qkv[(l + 1) // 2].astype(BF16))
        x = moe_layer(lay, x, g_ffn[l], mods, router_w[l], router_b[l], l, exp_w_gate, exp_w_up,
                      exp_w_down, sh_w_gate[l], sh_w_up[l], sh_w_down[l],
                      g_final=g_final if l == depth - 1 else None, next_proj=next_proj)
        qkv = None
        if next_proj is not None:
            x, qkv = x
    y_ctx, y_lat = x
    y_prompt = y_ctx[:lay.t_ctx].reshape(x_prompt.shape)
    y_sample = y_lat.reshape(x_sample.shape)
    return (y_prompt, y_sample, jnp.stack(new_lru, axis=1), jnp.stack(new_re, axis=1),
            jnp.stack(new_im, axis=1), jnp.stack(new_k, axis=1), jnp.stack(new_v, axis=1))


def kernel(x_prompt, x_sample, state_lru, state_ssm_re, state_ssm_im, cache_k, cache_v, c, c_ctx, g_mix, g_ffn, w_mod, b_mod, ev_w_in, lru_conv_w, lru_conv_b, lru_wa, lru_ba, lru_wx, lru_bx, lru_lam, ssm_a_re, ssm_a_im, ssm_log_dt, ssm_b_re, ssm_b_im, ssm_c_re, ssm_c_im, ssm_d, ssm_glu_w, ssm_glu_b, ev_w_out, at_w_qkv, at_sink, at_w_out, router_w, router_b, exp_w_gate, exp_w_up, exp_w_down, sh_w_gate, sh_w_up, sh_w_down, g_final):
    lay = Layout(n_ctx=x_prompt.shape[0], s_ctx=x_prompt.shape[1],
                 n_lat=x_sample.shape[0], s_lat=x_sample.shape[1])
    return _forward(lay, x_prompt, x_sample, state_lru, state_ssm_re, state_ssm_im, cache_k, cache_v,
                    c, c_ctx, g_mix, g_ffn, w_mod, b_mod,
                    ev_w_in, lru_conv_w, lru_conv_b, lru_wa, lru_ba, lru_wx, lru_bx, lru_lam,
                    ssm_a_re, ssm_a_im, ssm_log_dt, ssm_b_re, ssm_b_im, ssm_c_re, ssm_c_im, ssm_d,
                    ssm_glu_w, ssm_glu_b, ev_w_out, at_w_qkv, at_sink, at_w_out,
                    router_w, router_b, exp_w_gate, exp_w_up, exp_w_down, sh_w_gate, sh_w_up,
                    sh_w_down, g_final)
```

```python
import functools
from typing import NamedTuple

import jax
import jax.numpy as jnp
from jax import lax
from jax.experimental import pallas as pl
from jax.experimental.pallas import tpu as pltpu

F32 = jnp.float32
BF16 = jnp.bfloat16
I32 = jnp.int32
HIGHEST = lax.Precision.HIGHEST

D_MODEL = 1024
EPS = 1e-6
N_MOD = 6
GRID_W = 64
D_LRU = 512
LRU_BLOCKS = 8
LRU_C = 8.0
CONV_W = 4
CONV_LEFT = 2
D_SSM = 512
SSM_GROUP = 16
SSM_GROUPS = 32
SSM_STATE = 64
S5_CHUNK = 16
S5_LANES = S5_CHUNK * SSM_GROUP
S5_SCAN_STEPS = 8
HEAD_DIM = 64
N_HEADS = 16
N_KV = 4
GQA = 4
WINDOW = 128
Q_BLOCK = 128
ROPE_BASE = 10000.0
ATTN_SCALE = HEAD_DIM ** -0.5
NEG_INF = -1e30
N_EXPERTS = 256
TOP_K = 8
N_GROUPS = 8
TOPK_GROUPS = 4
GROUP_SIZE = N_EXPERTS // N_GROUPS
D_EXPERT = 256
ROUTE_SCALE = 2.5
MOE_BLOCK = 128

SUBLANES = 8
LANES = 128
ROW_CHUNKS = D_MODEL // (2 * LANES)
PACKED = jnp.int32
SEQ_TILE = 256
ROW_TILE = 512
ROUTER_TILE = 512
DEST_TILE = 1024
DISPATCH_TILE = 1024
COMBINE_TILE = 256
EXPERT_X_BUFS = 16
EXPERT_Y_BUFS = 8
VMEM_LIMIT = 56 * 1024 * 1024


class Layout(NamedTuple):
    n_ctx: int
    s_ctx: int
    n_lat: int
    s_lat: int

    @property
    def t_ctx(self):
        return self.n_ctx * self.s_ctx

    @property
    def t_lat(self):
        return self.n_lat * self.s_lat

    @property
    def t(self):
        return self.t_ctx + self.t_lat

    @property
    def n_seq(self):
        return self.n_ctx + self.n_lat


def _cparams(sem):
    return pltpu.CompilerParams(dimension_semantics=sem, vmem_limit_bytes=VMEM_LIMIT)


def _mod_index(lay, tile_rows):
    n_ctx_tiles = lay.t_ctx // tile_rows
    per_lat = lay.s_lat // tile_rows

    def f(i):
        return jnp.where(i < n_ctx_tiles, 0, 1 + (i - n_ctx_tiles) // per_lat)
    return f


def _adaln_kernel(c_ref, w_ref, b_ref, o_ref):
    c = c_ref[...]
    s = c * jax.nn.sigmoid(c)
    o_ref[...] = jnp.dot(s, w_ref[0], precision=HIGHEST, preferred_element_type=F32) + b_ref[...]


def adaln_table(c_rows, layer, w_mod, b_mod):
    n = c_rows.shape[0]
    tn = 1536
    out = pl.pallas_call(
        _adaln_kernel,
        grid=(N_MOD * D_MODEL // tn,),
        in_specs=[pl.BlockSpec((n, D_MODEL), lambda j: (0, 0)),
                  pl.BlockSpec((1, D_MODEL, tn), lambda j: (layer, 0, j)),
                  pl.BlockSpec((1, tn), lambda j: (0, j))],
        out_specs=pl.BlockSpec((n, tn), lambda j: (0, j)),
        out_shape=jax.ShapeDtypeStruct((n, N_MOD * D_MODEL), F32),
        compiler_params=_cparams(("arbitrary",)),
        name="adaln",
    )(c_rows, w_mod, b_mod.reshape(1, -1))
    return out.reshape(n, N_MOD, D_MODEL)


def _modnorm(x, g, mod_ref, slot):
    ms = jnp.mean(x * x, axis=-1, keepdims=True)
    y = x * lax.rsqrt(ms + EPS) * g
    shift = mod_ref[0, slot:slot + 1, :]
    scale = mod_ref[0, slot + 1:slot + 2, :]
    return y * (1.0 + scale) + shift


GROUPS_PER_VREG = LANES // SSM_GROUP
SSM_COL_BLOCKS = D_SSM // LANES


def _group_major_store(val, tmp_ref, dst_ref):
    rows = dst_ref.shape[1]
    for j in range(SSM_COL_BLOCKS):
        tmp_ref[j] = val[:, j * LANES:(j + 1) * LANES]
    for j in range(SSM_COL_BLOCKS):
        steps = [tmp_ref[j, pl.ds(i, rows, stride=S5_CHUNK), :] for i in range(S5_CHUNK)]
        for q in range(GROUPS_PER_VREG):
            dst_ref[j * GROUPS_PER_VREG + q] = jnp.concatenate(
                [w[:, q * SSM_GROUP:(q + 1) * SSM_GROUP] for w in steps], axis=1)


def _group_major_load(src_ref, tmp_ref):
    rows = src_ref.shape[1]
    for j in range(SSM_COL_BLOCKS):
        blocks = [src_ref[j * GROUPS_PER_VREG + q] for q in range(GROUPS_PER_VREG)]
        for i in range(S5_CHUNK):
            tmp_ref[j, pl.ds(i, rows, stride=S5_CHUNK), :] = jnp.concatenate(
                [b[:, i * SSM_GROUP:(i + 1) * SSM_GROUP] for b in blocks], axis=1)
    return jnp.concatenate([tmp_ref[j] for j in range(SSM_COL_BLOCKS)], axis=1)


def _split_row_specs(lay, tile_rows, width):
    nct = lay.t_ctx // tile_rows
    return [pl.BlockSpec((tile_rows, width), lambda i: (jnp.minimum(i, nct - 1), 0)),
            pl.BlockSpec((tile_rows, width), lambda i: (jnp.maximum(i - nct, 0), 0))], nct


def _pick_rows(ctx_ref, lat_ref, n_ctx_tiles):
    return jnp.where(pl.program_id(0) < n_ctx_tiles, ctx_ref[...], lat_ref[...])


def _modnorm_mm_kernel(*refs, slot, ug_col, n_ctx_tiles):
    if n_ctx_tiles is None:
        x = refs[0][...]
        refs = refs[1:]
    else:
        x = _pick_rows(refs[0], refs[1], n_ctx_tiles)
        refs = refs[2:]
    g_ref, mod_ref, w_ref, o_ref, *ug_refs = refs
    h = _modnorm(x, g_ref[...], mod_ref, slot)
    out = jnp.dot(h.astype(BF16), w_ref[...], preferred_element_type=F32)
    o_ref[...] = out
    if ug_col is not None:
        ug_ref, tmp_ref = ug_refs
        _group_major_store(out[:, ug_col:ug_col + D_SSM], tmp_ref, ug_ref)


def modnorm_matmul(lay, x, g, mods, slot, w_bf16, ug_col=None):
    t = lay.t
    n = w_bf16.shape[1]
    mi = _mod_index(lay, ROW_TILE)
    out_specs = [pl.BlockSpec((ROW_TILE, n), lambda i: (i, 0))]
    out_shape = [jax.ShapeDtypeStruct((t, n), F32)]
    if ug_col is not None:
        out_specs.append(pl.BlockSpec((SSM_GROUPS, ROW_TILE // S5_CHUNK, S5_LANES), lambda i: (0, i, 0)))
        out_shape.append(jax.ShapeDtypeStruct((SSM_GROUPS, t // S5_CHUNK, S5_LANES), F32))
    if isinstance(x, tuple):
        x_specs, nct = _split_row_specs(lay, ROW_TILE, D_MODEL)
    else:
        x_specs, nct, x = [pl.BlockSpec((ROW_TILE, D_MODEL), lambda i: (i, 0))], None, (x,)
    outs = pl.pallas_call(
        functools.partial(_modnorm_mm_kernel, slot=slot, ug_col=ug_col, n_ctx_tiles=nct),
        grid=(t // ROW_TILE,),
        in_specs=x_specs + [
                  pl.BlockSpec((1, D_MODEL), lambda i: (0, 0)),
                  pl.BlockSpec((1, N_MOD, D_MODEL), lambda i: (mi(i), 0, 0)),
                  pl.BlockSpec((D_MODEL, n), lambda i: (0, 0))],
        out_specs=out_specs,
        out_shape=out_shape,
        scratch_shapes=([pltpu.VMEM((SSM_COL_BLOCKS, ROW_TILE, LANES), F32)]
                        if ug_col is not None else []),
        compiler_params=_cparams(("arbitrary",)),
        name="modnorm_matmul",
    )(*x, g.reshape(1, -1), mods, w_bf16)
    return outs if ug_col is not None else outs[0]


def _seq_tile_maps(lay, reverse):
    assert lay.s_ctx == SEQ_TILE and lay.s_lat % SEQ_TILE == 0
    n_tiles = lay.t // SEQ_TILE
    per_lat = lay.s_lat // SEQ_TILE

    def tile(i):
        return (n_tiles - 1 - i) if reverse else i

    def seq(i):
        ti = tile(i)
        return jnp.where(ti < lay.n_ctx, ti, lay.n_ctx + (ti - lay.n_ctx) // per_lat)

    return n_tiles, tile, seq


def _softplus(x):
    return jnp.maximum(x, 0.0) + jnp.log(1.0 + jnp.exp(-jnp.abs(x)))


def _lru_kernel(rec_ref, prev_ref, next_ref, cw_ref, cb_ref, wg_ref, bg_ref, lam_ref, h0_ref,
                *rest, reverse, n_ctx, per_lat, n_tiles):
    if reverse:
        gate_ref, hf_ref, y_ref, st_ref, a_s, b_s, h_s, carry = rest
    else:
        y_ref, st_ref, a_s, b_s, h_s, carry = rest
    i = pl.program_id(0)
    ti = (n_tiles - 1 - i) if reverse else i
    is_first = jnp.logical_or(ti < n_ctx, (ti - n_ctx) % per_lat == 0)
    is_last = jnp.logical_or(ti < n_ctx, (ti - n_ctx) % per_lat == per_lat - 1)
    ts = SEQ_TILE

    rec = rec_ref[...]
    prev = jnp.where(is_first, 0.0, prev_ref[...])
    nxt = jnp.where(is_last, 0.0, next_ref[...])
    ext = jnp.concatenate([prev, rec, nxt], axis=0)
    n_ext = ts + 2 * SUBLANES
    cw = cw_ref[...]
    xc = cb_ref[...] + cw[2:3, :] * rec
    xc = xc + cw[0:1, :] * pltpu.roll(ext, 2, 0)[SUBLANES:SUBLANES + ts]
    xc = xc + cw[1:2, :] * pltpu.roll(ext, 1, 0)[SUBLANES:SUBLANES + ts]
    xc = xc + cw[3:4, :] * pltpu.roll(ext, n_ext - 1, 0)[SUBLANES:SUBLANES + ts]

    gates = jax.nn.sigmoid(jnp.dot(xc.astype(BF16), wg_ref[...], preferred_element_type=F32)
                           + bg_ref[...])
    r = gates[:, :D_LRU]
    ig = gates[:, D_LRU:]
    log_a = (-LRU_C) * r * _softplus(-lam_ref[...])
    a = jnp.exp(log_a)
    b = jnp.sqrt(1.0 - jnp.exp(2.0 * log_a)) * (ig * xc)

    row8 = lax.broadcasted_iota(I32, (ts, D_LRU), 0) % SUBLANES
    for sh in (1, 2, 4):
        if reverse:
            keep = row8 < SUBLANES - sh
            a_sh = pltpu.roll(a, ts - sh, 0)
            b_sh = pltpu.roll(b, ts - sh, 0)
        else:
            keep = row8 >= sh
            a_sh = pltpu.roll(a, sh, 0)
            b_sh = pltpu.roll(b, sh, 0)
        b = b + a * jnp.where(keep, b_sh, 0.0)
        a = a * jnp.where(keep, a_sh, 1.0)
    a_s[...] = a
    b_s[...] = b

    @pl.when(is_last if reverse else is_first)
    def _():
        carry[...] = h0_ref[0]

    n_grp = ts // SUBLANES

    def body(k, c):
        gi = (n_grp - 1 - k) if reverse else k
        sl = pl.ds(pl.multiple_of(gi * SUBLANES, SUBLANES), SUBLANES)
        h = b_s[sl, :] + a_s[sl, :] * c
        h_s[sl, :] = h
        return h[0:1, :] if reverse else h[SUBLANES - 1:SUBLANES, :]

    c_fin = lax.fori_loop(0, n_grp, body, carry[...], unroll=4)
    carry[...] = c_fin
    st_ref[0] = c_fin
    if reverse:
        y_ref[...] = (hf_ref[...] + h_s[...]) * jax.nn.gelu(gate_ref[...])
    else:
        y_ref[...] = h_s[...]


def lru_pass(lay, proj, conv_w, conv_b, wg_bf16, bg, lam, h0, reverse, hf=None):
    n_tiles, tile, seq = _seq_tile_maps(lay, reverse)
    per_lat = lay.s_lat // SEQ_TILE
    blk8 = SEQ_TILE // SUBLANES
    last8 = lay.t // SUBLANES - 1
    c = D_LRU
    in_specs = [
        pl.BlockSpec((SEQ_TILE, c), lambda i: (tile(i), 1)),
        pl.BlockSpec((SUBLANES, c), lambda i: (jnp.maximum(tile(i) * blk8 - 1, 0), 1)),
        pl.BlockSpec((SUBLANES, c), lambda i: (jnp.minimum(tile(i) * blk8 + blk8, last8), 1)),
        pl.BlockSpec((CONV_W, c), lambda i: (0, 0)),
        pl.BlockSpec((1, c), lambda i: (0, 0)),
        pl.BlockSpec((c, 2 * c), lambda i: (0, 0)),
        pl.BlockSpec((1, 2 * c), lambda i: (0, 0)),
        pl.BlockSpec((1, c), lambda i: (0, 0)),
        pl.BlockSpec((1, 1, c), lambda i: (seq(i), 0, 0)),
    ]
    args = [proj, proj, proj, conv_w, conv_b.reshape(1, -1), wg_bf16, bg.reshape(1, -1),
            lam.reshape(1, -1), h0.reshape(lay.n_seq, 1, c)]
    if reverse:
        in_specs += [pl.BlockSpec((SEQ_TILE, c), lambda i: (tile(i), 0)),
                     pl.BlockSpec((SEQ_TILE, c), lambda i: (tile(i), 0))]
        args += [proj, hf]
    y, st = pl.pallas_call(
        functools.partial(_lru_kernel, reverse=reverse, n_ctx=lay.n_ctx, per_lat=per_lat,
                          n_tiles=n_tiles),
        grid=(n_tiles,),
        in_specs=in_specs,
        out_specs=[pl.BlockSpec((SEQ_TILE, c), lambda i: (tile(i), 0)),
                   pl.BlockSpec((1, 1, c), lambda i: (seq(i), 0, 0))],
        out_shape=[jax.ShapeDtypeStruct((lay.t, c), F32),
                   jax.ShapeDtypeStruct((lay.n_seq, 1, c), F32)],
        scratch_shapes=[pltpu.VMEM((SEQ_TILE, c), F32), pltpu.VMEM((SEQ_TILE, c), F32),
                        pltpu.VMEM((SEQ_TILE, c), F32), pltpu.VMEM((1, c), F32)],
        compiler_params=_cparams(("arbitrary",)),
        name="lru_bwd" if reverse else "lru_fwd",
    )(*args)
    return y, st.reshape(lay.n_seq, c)


def _cmul(a, b):
    return a[0] * b[0] - a[1] * b[1], a[0] * b[1] + a[1] * b[0]


def _s5_matrices(a_re, a_im, log_dt, b_re, b_im, c_re, c_im):
    a_re, a_im = a_re.astype(F32), a_im.astype(F32)
    dt = jnp.exp(log_dt.astype(F32))[..., None]
    z = (a_re * dt, a_im * dt)

    def zpow(k):
        k = k.reshape((-1,) + (1,) * z[0].ndim)
        mag = jnp.exp(k * z[0][None])
        return mag * jnp.cos(k * z[1][None]), mag * jnp.sin(k * z[1][None])

    a_bar = zpow(jnp.ones((1,), F32))
    a_bar = (a_bar[0][0], a_bar[1][0])
    den = a_re * a_re + a_im * a_im
    xr, xi = a_bar[0] - 1.0, a_bar[1]
    q = ((xr * a_re + xi * a_im) / den, (xi * a_re - xr * a_im) / den)
    b_bar = _cmul((q[0][..., None], q[1][..., None]), (b_re.astype(F32), b_im.astype(F32)))
    cc = (c_re.astype(F32), c_im.astype(F32))
    el = S5_CHUNK
    pw = zpow(jnp.arange(el + 1, dtype=F32))
    idx = jnp.arange(el)
    m_in, m_toep, m_out = [], [], []
    for d in range(2):
        p_d = (pw[0][:, d], pw[1][:, d])
        b_d = (b_bar[0][d], b_bar[1][d])
        c_d = (cc[0][d], cc[1][d])
        k_in = (el - 1 - idx) if d == 0 else idx
        w_in = _cmul((p_d[0][k_in][..., None], p_d[1][k_in][..., None]),
                     (b_d[0][None], b_d[1][None]))
        w_in = [jnp.transpose(w, (1, 0, 3, 2)).reshape(SSM_GROUPS, S5_LANES, SSM_STATE) for w in w_in]
        m_in.append(jnp.concatenate(w_in, axis=-1))
        cp = _cmul((c_d[0][None], c_d[1][None]),
                   (p_d[0][:, :, None, :], p_d[1][:, :, None, :]))
        kern = (jnp.einsum('kghp,gpc->kgch', cp[0][:el], b_d[0])
                - jnp.einsum('kghp,gpc->kgch', cp[1][:el], b_d[1]))
        zero = jnp.zeros_like(kern[0])
        rows = []
        for i in range(el):
            if d == 0:
                pieces = [zero] * i + [kern[k] for k in range(el - i)]
            else:
                pieces = [kern[i - j] for j in range(i + 1)] + [zero] * (el - 1 - i)
            rows.append(jnp.concatenate(pieces, axis=-1))
        m_toep.append(jnp.stack(rows, axis=1).reshape(SSM_GROUPS, S5_LANES, S5_LANES))
        k_out = (idx + 1) if d == 0 else (el - idx)
        w_out = [jnp.transpose(w[k_out], (1, 3, 0, 2)).reshape(SSM_GROUPS, SSM_STATE, S5_LANES)
                 for w in cp]
        m_out.append(jnp.concatenate([w_out[0], -w_out[1]], axis=1))
    mul = zpow(el * 2.0 ** jnp.arange(S5_SCAN_STEPS, dtype=F32))
    mul = [jnp.transpose(m, (2, 1, 0, 3)) for m in mul]
    coef_a = jnp.concatenate([mul[0], mul[0]], axis=-1)
    coef_b = jnp.concatenate([-mul[1], mul[1]], axis=-1)
    stack = lambda xs: jnp.stack(xs, axis=1)
    return (stack(m_in).astype(BF16), stack(m_toep).astype(BF16), stack(m_out).astype(BF16),
            coef_a, coef_b)


def _s5_scan(v, ca, cb, seg, reverse):
    n = v.shape[0]
    assert seg <= 2 ** S5_SCAN_STEPS
    row = lax.broadcasted_iota(I32, (n, 2 * SSM_STATE), 0) % seg
    k, sh = 0, 1
    while sh < seg:
        if reverse:
            s = jnp.where(row < seg - sh, pltpu.roll(v, n - sh, 0), 0.0)
        else:
            s = jnp.where(row >= sh, pltpu.roll(v, sh, 0), 0.0)
        v = v + ca[k:k + 1, :] * s + cb[k:k + 1, :] * pltpu.roll(s, SSM_STATE, 1)
        k += 1
        sh *= 2
    return v


def _s5_shift(h, seg, reverse):
    n = h.shape[0]
    row = lax.broadcasted_iota(I32, (n, 2 * SSM_STATE), 0) % seg
    if reverse:
        return jnp.where(row < seg - 1, pltpu.roll(h, n - 1, 0), 0.0)
    return jnp.where(row >= 1, pltpu.roll(h, 1, 0), 0.0)


def _s5_kernel(u_ref, min_ref, mtoep_ref, mout_ref, ca_ref, cb_ref, h0_ref, y_ref, hc_ref,
               v_s, hp_s, *, rc, seg_c, n_lat, seg_l):
    u = u_ref[0].astype(BF16)
    u_c, u_l = u[:rc], u[rc:]
    y_c = jnp.zeros((rc, S5_LANES), F32)
    y_l = jnp.zeros((n_lat * seg_l, S5_LANES), F32)
    for d in range(2):
        reverse = d == 1
        ca = ca_ref[0, d]
        cb = cb_ref[0, d]
        m_in = min_ref[0, d]
        m_toep = mtoep_ref[0, d]
        m_out = mout_ref[0, d]
        h_c = _s5_scan(jnp.dot(u_c, m_in, preferred_element_type=F32), ca, cb, seg_c, reverse)
        hc_ref[0, d] = h_c
        hp_c = _s5_shift(h_c, seg_c, reverse)
        y_c = y_c + jnp.dot(u_c, m_toep, preferred_element_type=F32)
        y_c = y_c + jnp.dot(hp_c.astype(BF16), m_out, preferred_element_type=F32)
        v_s[...] = jnp.dot(u_l, m_in, preferred_element_type=F32)
        for s in range(n_lat):
            h0 = h0_ref[0, d, s:s + 1, :]
            r0 = s * seg_l + (seg_l - 1 if reverse else 0)
            v_s[r0:r0 + 1, :] = (v_s[r0:r0 + 1, :] + ca[0:1, :] * h0
                                 + cb[0:1, :] * pltpu.roll(h0, SSM_STATE, 1))
        h_l = _s5_scan(v_s[...], ca, cb, seg_l, reverse)
        hp_s[...] = _s5_shift(h_l, seg_l, reverse)
        for s in range(n_lat):
            r0 = s * seg_l + (seg_l - 1 if reverse else 0)
            hp_s[r0:r0 + 1, :] = h0_ref[0, d, s:s + 1, :]
        y_l = y_l + jnp.dot(u_l, m_toep, preferred_element_type=F32)
        y_l = y_l + jnp.dot(hp_s[...].astype(BF16), m_out, preferred_element_type=F32)
    y_ref[0, :rc, :] = y_c
    y_ref[0, rc:, :] = y_l


def s5_mixer(lay, u_g, mats, h0):
    m_in, m_toep, m_out, coef_a, coef_b = mats
    rows = lay.t // S5_CHUNK
    rc = lay.t_ctx // S5_CHUNK
    rl = rows - rc
    st2 = 2 * SSM_STATE
    g4 = lambda g: (g, 0, 0, 0)
    return pl.pallas_call(
        functools.partial(_s5_kernel, rc=rc, seg_c=lay.s_ctx // S5_CHUNK, n_lat=lay.n_lat,
                          seg_l=lay.s_lat // S5_CHUNK),
        grid=(SSM_GROUPS,),
        in_specs=[pl.BlockSpec((1, rows, S5_LANES), lambda g: (g, 0, 0)),
                  pl.BlockSpec((1, 2, S5_LANES, st2), g4),
                  pl.BlockSpec((1, 2, S5_LANES, S5_LANES), g4),
                  pl.BlockSpec((1, 2, st2, S5_LANES), g4),
                  pl.BlockSpec((1, 2, S5_SCAN_STEPS, st2), g4),
                  pl.BlockSpec((1, 2, S5_SCAN_STEPS, st2), g4),
                  pl.BlockSpec((1, 2, lay.n_lat, st2), g4)],
        out_specs=[pl.BlockSpec((1, rows, S5_LANES), lambda g: (g, 0, 0)),
                   pl.BlockSpec((1, 2, rc, st2), g4)],
        out_shape=[jax.ShapeDtypeStruct((SSM_GROUPS, rows, S5_LANES), F32),
                   jax.ShapeDtypeStruct((SSM_GROUPS, 2, rc, st2), F32)],
        scratch_shapes=[pltpu.VMEM((rl, st2), F32), pltpu.VMEM((rl, st2), F32)],
        compiler_params=_cparams(("arbitrary",)),
        name="s5_mixer",
    )(u_g, m_in, m_toep, m_out, coef_a, coef_b, h0)


def _even_out_kernel(*refs, n_ctx_tiles):
    if n_ctx_tiles is None:
        x = refs[0][...]
        refs = refs[1:]
    else:
        x = _pick_rows(refs[0], refs[1], n_ctx_tiles)
        refs = refs[2:]
    ya_ref, yg_ref, u_ref, d_ref, gw_ref, gb_ref, w_ref, mod_ref, o_ref, yt_s = refs
    ys = _group_major_load(yg_ref, yt_s) + d_ref[...] * u_ref[...]
    g = jax.nn.gelu(ys)
    yb = g * jax.nn.sigmoid(jnp.dot(g.astype(BF16), gw_ref[...], preferred_element_type=F32)
                            + gb_ref[...])
    out = jnp.dot(ya_ref[...].astype(BF16), w_ref[:D_LRU, :], preferred_element_type=F32)
    out = out + jnp.dot(yb.astype(BF16), w_ref[D_LRU:, :], preferred_element_type=F32)
    o_ref[...] = x + mod_ref[0, 2:3, :] * out


def even_out(lay, x, y_a, y_g, proj, ssm_d, glu_w_bf16, glu_b, w_out_bf16, mods):
    mi = _mod_index(lay, ROW_TILE)
    c = D_SSM
    row = lambda i: (i, 0)
    const = lambda i: (0, 0)
    if isinstance(x, tuple):
        x_specs, nct = _split_row_specs(lay, ROW_TILE, D_MODEL)
    else:
        x_specs, nct, x = [pl.BlockSpec((ROW_TILE, D_MODEL), row)], None, (x,)
    return pl.pallas_call(
        functools.partial(_even_out_kernel, n_ctx_tiles=nct),
        grid=(lay.t // ROW_TILE,),
        in_specs=x_specs + [
                  pl.BlockSpec((ROW_TILE, c), row),
                  pl.BlockSpec((SSM_GROUPS, ROW_TILE // S5_CHUNK, S5_LANES), lambda i: (0, i, 0)),
                  pl.BlockSpec((ROW_TILE, c), lambda i: (i, 2)),
                  pl.BlockSpec((1, c), const),
                  pl.BlockSpec((c, c), const),
                  pl.BlockSpec((1, c), const),
                  pl.BlockSpec((D_MODEL, D_MODEL), const),
                  pl.BlockSpec((1, N_MOD, D_MODEL), lambda i: (mi(i), 0, 0))],
        out_specs=pl.BlockSpec((ROW_TILE, D_MODEL), row),
        out_shape=jax.ShapeDtypeStruct((lay.t, D_MODEL), F32),
        scratch_shapes=[pltpu.VMEM((SSM_COL_BLOCKS, ROW_TILE, LANES), F32)],
        compiler_params=_cparams(("arbitrary",)),
        name="even_out",
    )(*x, y_a, y_g, proj, ssm_d.reshape(1, -1), glu_w_bf16, glu_b.reshape(1, -1), w_out_bf16, mods)


def _softmax_pv(parts, sink_col):
    m = sink_col
    for s, _ in parts:
        m = jnp.maximum(m, jnp.max(s, axis=-1, keepdims=True))
    den = jnp.exp(sink_col - m)
    acc = None
    for s, v in parts:
        p = jnp.exp(s - m)
        den = den + jnp.sum(p, axis=-1, keepdims=True)
        pv = jnp.dot(p.astype(BF16), v.astype(BF16), preferred_element_type=F32)
        acc = pv if acc is None else acc + pv
    return acc / den


def _nt_dot(a, b):
    return lax.dot_general(a.astype(BF16), b.astype(BF16), (((1,), (1,)), ((), ())),
                           preferred_element_type=F32)


def _attn_ctx_kernel(q_ref, k_ref, v_ref, sink_ref, o_ref):
    n = q_ref.shape[0]
    for kh in range(N_KV):
        k = k_ref[:, kh * HEAD_DIM:(kh + 1) * HEAD_DIM]
        v = v_ref[:, kh * HEAD_DIM:(kh + 1) * HEAD_DIM]
        heads = [kh * GQA + g for g in range(GQA)]
        q = jnp.concatenate([q_ref[:, h * HEAD_DIM:(h + 1) * HEAD_DIM] for h in heads], axis=0)
        sink = jnp.concatenate([jnp.broadcast_to(sink_ref[0:1, h:h + 1], (n, 1)) for h in heads],
                               axis=0)
        o = _softmax_pv([(_nt_dot(q * ATTN_SCALE, k), v)], sink)
        for g, h in enumerate(heads):
            o_ref[:, h * HEAD_DIM:(h + 1) * HEAD_DIM] = o[g * n:(g + 1) * n]


def attn_context(lay, qkv, sink):
    nq = N_HEADS * HEAD_DIM
    nkv = N_KV * HEAD_DIM
    return pl.pallas_call(
        _attn_ctx_kernel,
        grid=(lay.n_ctx,),
        in_specs=[pl.BlockSpec((lay.s_ctx, nq), lambda b: (b, 0)),
                  pl.BlockSpec((lay.s_ctx, nkv), lambda b: (b, nq // nkv)),
                  pl.BlockSpec((lay.s_ctx, nkv), lambda b: (b, nq // nkv + 1)),
                  pl.BlockSpec((1, N_HEADS), lambda b: (0, 0))],
        out_specs=pl.BlockSpec((lay.s_ctx, nq), lambda b: (b, 0)),
        out_shape=jax.ShapeDtypeStruct((lay.t_ctx, nq), F32),
        compiler_params=_cparams(("arbitrary",)),
        name="attn_context",
    )(qkv, qkv, qkv, sink.reshape(1, -1))


def _rope(x, cos, sin):
    lane = lax.broadcasted_iota(I32, (x.shape[0], 2 * HEAD_DIM), 1) % HEAD_DIM
    outs = []
    for j in range(x.shape[1] // (2 * HEAD_DIM)):
        xs = x[:, j * 2 * HEAD_DIM:(j + 1) * 2 * HEAD_DIM]
        sw = jnp.where(lane < HEAD_DIM // 2,
                       pltpu.roll(xs, 2 * HEAD_DIM - HEAD_DIM // 2, 1),
                       pltpu.roll(xs, HEAD_DIM // 2, 1))
        outs.append(xs * cos + sw * sin)
    return outs


def _attn_lat_kernel(q_ref, k0_ref, k1_ref, k2_ref, v0_ref, v1_ref, v2_ref, ck_ref, cv_ref,
                     cq_ref, sq_ref, c0_ref, c1_ref, c2_ref, s0_ref, s1_ref, s2_ref, sink_ref,
                     o_ref, *, n_blk):
    j = pl.program_id(1)
    qb = Q_BLOCK
    q_parts = [qp * ATTN_SCALE for qp in _rope(q_ref[...], cq_ref[...], sq_ref[...])]
    k_parts = [_rope(kr[...], cr[...], sr[...])
               for kr, cr, sr in ((k0_ref, c0_ref, s0_ref), (k1_ref, c1_ref, s1_ref),
                                  (k2_ref, c2_ref, s2_ref))]
    qi = lax.broadcasted_iota(I32, (qb, 3 * qb), 0)
    km = lax.broadcasted_iota(I32, (qb, 3 * qb), 1)
    kpos = j * qb - qb + km
    mask1 = (jnp.abs(km - qb - qi) <= WINDOW) & (kpos >= 0) & (kpos < n_blk * qb)
    mask = jnp.concatenate([mask1] * GQA, axis=0)
    for kh in range(N_KV):
        half = (kh % 2) * HEAD_DIM
        k_loc = jnp.concatenate([kp[kh // 2][:, half:half + HEAD_DIM] for kp in k_parts], axis=0)
        v_loc = jnp.concatenate([vr[:, kh * HEAD_DIM:(kh + 1) * HEAD_DIM]
                                 for vr in (v0_ref, v1_ref, v2_ref)], axis=0)
        qs, sinks = [], []
        for g in range(GQA):
            h = kh * GQA + g
            qs.append(q_parts[h // 2][:, (h % 2) * HEAD_DIM:(h % 2 + 1) * HEAD_DIM])
            sinks.append(jnp.broadcast_to(sink_ref[0:1, h:h + 1], (qb, 1)))
        q = jnp.concatenate(qs, axis=0)
        sink = jnp.concatenate(sinks, axis=0)
        s_loc = jnp.where(mask, _nt_dot(q, k_loc), NEG_INF)
        s_ctx = _nt_dot(q, ck_ref[0, kh])
        o = _softmax_pv([(s_loc, v_loc), (s_ctx, cv_ref[0, kh])], sink)
        for g in range(GQA):
            h = kh * GQA + g
            o_ref[:, h * HEAD_DIM:(h + 1) * HEAD_DIM] = o[g * qb:(g + 1) * qb]


def _rope_tables(s_len):
    rows = s_len // GRID_W
    row = jnp.repeat(jnp.arange(rows), GRID_W).astype(F32)
    col = jnp.tile(jnp.arange(GRID_W), rows).astype(F32)
    nf = HEAD_DIM // 4
    inv = ROPE_BASE ** (-jnp.arange(nf, dtype=F32) / nf)
    ang = jnp.concatenate([row[:, None] * inv, col[:, None] * inv], axis=-1)
    cos, sin = jnp.cos(ang), jnp.sin(ang)
    cos2 = jnp.tile(jnp.concatenate([cos, cos], axis=-1), (1, 2))
    sin2 = jnp.tile(jnp.concatenate([-sin, sin], axis=-1), (1, 2))
    return cos2, sin2


def attn_latent(lay, qkv, cache_k, cache_v, sink):
    nq = N_HEADS * HEAD_DIM
    nkv = N_KV * HEAD_DIM
    n_blk = lay.s_lat // Q_BLOCK
    base = lay.t_ctx // Q_BLOCK
    n_ctx_keys = cache_k.shape[2]
    cos2, sin2 = _rope_tables(lay.s_lat)
    kcol = nq // nkv

    def qrow(b, j):
        return base + b * n_blk + j

    def krow(off):
        return lambda b, j: base + b * n_blk + jnp.clip(j + off, 0, n_blk - 1)

    def trow(off):
        return lambda b, j: (jnp.clip(j + off, 0, n_blk - 1), 0)

    kv_spec = lambda off, col: pl.BlockSpec((Q_BLOCK, nkv), lambda b, j: (krow(off)(b, j), col))
    tab = lambda off: pl.BlockSpec((Q_BLOCK, 2 * HEAD_DIM), trow(off))
    cache_spec = pl.BlockSpec((1, N_KV, n_ctx_keys, HEAD_DIM), lambda b, j: (b, 0, 0, 0))
    return pl.pallas_call(
        functools.partial(_attn_lat_kernel, n_blk=n_blk),
        grid=(lay.n_lat, n_blk),
        in_specs=[pl.BlockSpec((Q_BLOCK, nq), lambda b, j: (qrow(b, j), 0)),
                  kv_spec(-1, kcol), kv_spec(0, kcol), kv_spec(1, kcol),
                  kv_spec(-1, kcol + 1), kv_spec(0, kcol + 1), kv_spec(1, kcol + 1),
                  cache_spec, cache_spec,
                  tab(0), tab(0), tab(-1), tab(0), tab(1), tab(-1), tab(0), tab(1),
                  pl.BlockSpec((1, N_HEADS), lambda b, j: (0, 0))],
        out_specs=pl.BlockSpec((Q_BLOCK, nq), lambda b, j: (b * n_blk + j, 0)),
        out_shape=jax.ShapeDtypeStruct((lay.t_lat, nq), F32),
        compiler_params=_cparams(("arbitrary", "arbitrary")),
        name="attn_latent",
    )(qkv, qkv, qkv, qkv, qkv, qkv, qkv, cache_k, cache_v,
      cos2, sin2, cos2, cos2, cos2, sin2, sin2, sin2, sink.reshape(1, -1))


def _mm_res_kernel(x_ref, ac_ref, al_ref, w_ref, mod_ref, o_ref, *, n_ctx_tiles):
    a = jnp.where(pl.program_id(0) < n_ctx_tiles, ac_ref[...], al_ref[...])
    out = jnp.dot(a.astype(BF16), w_ref[...], preferred_element_type=F32)
    o_ref[...] = x_ref[...] + mod_ref[0, 2:3, :] * out


def matmul_residual(lay, x, a_ctx, a_lat, w_bf16, mods):
    mi = _mod_index(lay, ROW_TILE)
    k = a_ctx.shape[1]
    nct = lay.t_ctx // ROW_TILE
    return pl.pallas_call(
        functools.partial(_mm_res_kernel, n_ctx_tiles=nct),
        grid=(lay.t // ROW_TILE,),
        in_specs=[pl.BlockSpec((ROW_TILE, D_MODEL), lambda i: (i, 0)),
                  pl.BlockSpec((ROW_TILE, k), lambda i: (jnp.minimum(i, nct - 1), 0)),
                  pl.BlockSpec((ROW_TILE, k), lambda i: (jnp.maximum(i - nct, 0), 0)),
                  pl.BlockSpec((k, D_MODEL), lambda i: (0, 0)),
                  pl.BlockSpec((1, N_MOD, D_MODEL), lambda i: (mi(i), 0, 0))],
        out_specs=pl.BlockSpec((ROW_TILE, D_MODEL), lambda i: (i, 0)),
        out_shape=jax.ShapeDtypeStruct((lay.t, D_MODEL), F32),
        compiler_params=_cparams(("arbitrary",)),
        name="matmul_residual",
    )(x, a_ctx, a_lat, w_bf16, mods)


def _rowtile_load(ref, n, base=0):
    parts = []
    for c in range(ROW_CHUNKS):
        words = ref[pl.ds(base + c, n, stride=ROW_CHUNKS), :]
        for half in range(2):
            parts.append(pltpu.unpack_elementwise(words, index=half, packed_dtype=BF16,
                                                  unpacked_dtype=F32))
    return jnp.concatenate(parts, axis=1)


def _rowtile_store(ref, val, n):
    for c in range(ROW_CHUNKS):
        lo = val[:, 2 * c * LANES:(2 * c + 1) * LANES]
        hi = val[:, (2 * c + 1) * LANES:(2 * c + 2) * LANES]
        ref[pl.ds(c, n, stride=ROW_CHUNKS), :] = pltpu.pack_elementwise([lo, hi], packed_dtype=BF16)


def _row_copy(src, src_row, dst, dst_row, sem):
    return pltpu.make_async_copy(
        src.at[pl.ds(pl.multiple_of(src_row * ROW_CHUNKS, ROW_CHUNKS), ROW_CHUNKS), :],
        dst.at[pl.ds(pl.multiple_of(dst_row * ROW_CHUNKS, ROW_CHUNKS), ROW_CHUNKS), :], sem)


def _router_kernel(x_ref, g_ref, mod_ref, rwt_ref, rb_ref, tri_ref, sg_ref, su_ref, sd_ref,
                   hn_ref, sh_ref, eidx_ref, wts_ref, rank_ref, cnt_ref, cnt_s):
    tm = ROUTER_TILE

    @pl.when(pl.program_id(0) == 0)
    def _():
        cnt_s[...] = jnp.zeros_like(cnt_s)

    h = _modnorm(x_ref[...], g_ref[...], mod_ref, 3)
    _rowtile_store(hn_ref, h, tm)
    hb = h.astype(BF16)
    sgate = jnp.dot(hb, sg_ref[...], preferred_element_type=F32)
    sup = jnp.dot(hb, su_ref[...], preferred_element_type=F32)
    sh_ref[...] = jnp.dot((sgate * jax.nn.sigmoid(sgate) * sup).astype(BF16), sd_ref[...],
                          preferred_element_type=F32)
    logits = lax.dot_general(rwt_ref[...], h, (((1,), (1,)), ((), ())),
                             precision=HIGHEST, preferred_element_type=F32)
    scores = jax.nn.sigmoid(logits)
    choice = scores + rb_ref[...]
    gs_rows = []
    for g in range(N_GROUPS):
        cg = choice[g * GROUP_SIZE:(g + 1) * GROUP_SIZE, :]
        m1 = jnp.max(cg, axis=0, keepdims=True)
        eq = cg == m1
        cnt = jnp.sum(eq.astype(F32), axis=0, keepdims=True)
        m2 = jnp.max(jnp.where(eq, -jnp.inf, cg), axis=0, keepdims=True)
        gs_rows.append(m1 + jnp.where(cnt >= 2.0, m1, m2))
    gs = jnp.concatenate(gs_rows, axis=0)
    gi = lax.broadcasted_iota(I32, (N_GROUPS, tm), 0)
    grank = jnp.zeros((N_GROUPS, tm), I32)
    for g in range(N_GROUPS):
        other = gs[g:g + 1, :]
        ahead = (other > gs) | ((other == gs) & (g < gi))
        grank = grank + ahead.astype(I32)
    gsel = grank < TOPK_GROUPS
    emask = jnp.concatenate(
        [jnp.broadcast_to(gsel[g:g + 1, :], (GROUP_SIZE, tm)) for g in range(N_GROUPS)], axis=0)
    masked = jnp.where(emask, choice, -jnp.inf)
    ei = lax.broadcasted_iota(I32, (N_EXPERTS, tm), 0)
    idxs, ws = [], []
    member = jnp.zeros((N_EXPERTS, tm), F32)
    for _ in range(TOP_K):
        m = jnp.max(masked, axis=0, keepdims=True)
        idx = jnp.min(jnp.where(masked == m, ei, N_EXPERTS), axis=0, keepdims=True)
        hit = ei == idx
        ws.append(jnp.sum(jnp.where(hit, scores, 0.0), axis=0, keepdims=True))
        idxs.append(idx)
        member = jnp.where(hit, 1.0, member)
        masked = jnp.where(hit, -jnp.inf, masked)
    w = jnp.concatenate(ws, axis=0)
    wts_ref[...] = w / jnp.sum(w, axis=0, keepdims=True) * ROUTE_SCALE
    eidx_ref[...] = jnp.concatenate(idxs, axis=0)
    before = jnp.dot(member.astype(BF16), tri_ref[...], preferred_element_type=F32) + cnt_s[...]
    ranks = [jnp.sum(jnp.where(ei == idx, before, 0.0), axis=0, keepdims=True) for idx in idxs]
    rank_ref[...] = jnp.concatenate(ranks, axis=0).astype(I32)
    cnt_s[...] = cnt_s[...] + jnp.sum(member, axis=1, keepdims=True)
    cnt_ref[...] = jnp.broadcast_to(cnt_s[...], cnt_ref.shape)


def moe_router(lay, x, g, mods, router_w, router_b, sg_bf16, su_bf16, sd_bf16):
    t = lay.t
    tm = ROUTER_TILE
    mi = _mod_index(lay, tm)
    tri = (jnp.arange(tm)[:, None] < jnp.arange(tm)[None, :]).astype(BF16)
    tok = lambda i: (0, i)
    const = lambda i: (0, 0)
    return pl.pallas_call(
        _router_kernel,
        grid=(t // tm,),
        in_specs=[pl.BlockSpec((tm, D_MODEL), lambda i: (i, 0)),
                  pl.BlockSpec((1, D_MODEL), const),
                  pl.BlockSpec((1, N_MOD, D_MODEL), lambda i: (mi(i), 0, 0)),
                  pl.BlockSpec((N_EXPERTS, D_MODEL), const),
                  pl.BlockSpec((N_EXPERTS, 1), const),
                  pl.BlockSpec((tm, tm), const),
                  pl.BlockSpec((D_MODEL, D_EXPERT), const),
                  pl.BlockSpec((D_MODEL, D_EXPERT), const),
                  pl.BlockSpec((D_EXPERT, D_MODEL), const)],
        out_specs=[pl.BlockSpec((tm * ROW_CHUNKS, LANES), lambda i: (i, 0)),
                   pl.BlockSpec((tm, D_MODEL), lambda i: (i, 0)),
                   pl.BlockSpec((TOP_K, tm), tok),
                   pl.BlockSpec((TOP_K, tm), tok),
                   pl.BlockSpec((TOP_K, tm), tok),
                   pl.BlockSpec((N_EXPERTS, LANES), const)],
        out_shape=[jax.ShapeDtypeStruct((t * ROW_CHUNKS, LANES), PACKED),
                   jax.ShapeDtypeStruct((t, D_MODEL), F32),
                   jax.ShapeDtypeStruct((TOP_K, t), I32),
                   jax.ShapeDtypeStruct((TOP_K, t), F32),
                   jax.ShapeDtypeStruct((TOP_K, t), I32),
                   jax.ShapeDtypeStruct((N_EXPERTS, LANES), F32)],
        scratch_shapes=[pltpu.VMEM((N_EXPERTS, 1), F32)],
        compiler_params=_cparams(("arbitrary",)),
        name="moe_router",
    )(x, g.reshape(1, -1), mods, router_w.T, router_b.reshape(-1, 1), tri, sg_bf16, su_bf16, sd_bf16)


def _dest_kernel(start_ref, eidx_ref, rank_ref, dest_ref):
    e = eidx_ref[...]

    def body(i, acc):
        return jnp.where(e == i, start_ref[i], acc)

    dest_ref[...] = lax.fori_loop(0, N_EXPERTS, body, jnp.zeros_like(e), unroll=8) + rank_ref[...]


def moe_dest(pad_start, eidx, rank):
    t = eidx.shape[1]
    tn = DEST_TILE
    spec = pl.BlockSpec((TOP_K, tn), lambda i, ps: (0, i))
    return pl.pallas_call(
        _dest_kernel,
        grid_spec=pltpu.PrefetchScalarGridSpec(
            num_scalar_prefetch=1, grid=(t // tn,), in_specs=[spec, spec], out_specs=spec),
        out_shape=jax.ShapeDtypeStruct((TOP_K, t), I32),
        compiler_params=_cparams(("arbitrary",)),
        name="moe_dest",
    )(pad_start, eidx, rank)


def _issue_row_copies(idx_at, n, copy_at, unroll=4):
    def body(i, c):
        for p in range(2):
            r = 2 * i + p
            copy_at(r, idx_at(r)).start(priority=p)
        return c
    lax.fori_loop(0, n // 2, body, 0, unroll=unroll)


def _dispatch_kernel(zrow_ref, dest_hbm, hn_ref, xs_hbm, idx_s, zbuf, isem, zsem, ssem, *, n_tiles):
    i = pl.program_id(0)
    slot = i % 2
    td = DISPATCH_TILE

    n_idx = TOP_K * td

    def idx_copy(tile, s):
        return pltpu.make_async_copy(dest_hbm.at[tile], idx_s.at[pl.ds(s * n_idx, n_idx)], isem.at[s])

    def zero_copy(e):
        r0 = pl.multiple_of(zrow_ref[e] * ROW_CHUNKS, ROW_CHUNKS)
        return pltpu.make_async_copy(zbuf, xs_hbm.at[pl.ds(r0, MOE_BLOCK * ROW_CHUNKS), :], zsem)

    @pl.when(i == 0)
    def _():
        zbuf[...] = jnp.zeros_like(zbuf)

        def zstart(e, c):
            @pl.when(zrow_ref[e] >= 0)
            def _():
                zero_copy(e).start()
            return c

        def zwait(e, c):
            @pl.when(zrow_ref[e] >= 0)
            def _():
                zero_copy(e).wait()
            return c

        lax.fori_loop(0, zrow_ref.shape[0], zstart, 0)
        idx_copy(0, 0).start()
        lax.fori_loop(0, zrow_ref.shape[0], zwait, 0)

    idx_copy(i, slot).wait()

    @pl.when(i + 1 < n_tiles)
    def _():
        idx_copy(i + 1, 1 - slot).start()

    for k in range(TOP_K):
        _issue_row_copies(lambda r: idx_s[slot * n_idx + k * td + r], td,
                          lambda r, d: _row_copy(hn_ref, r, xs_hbm, d, ssem))
    for k in range(TOP_K):
        pltpu.make_async_copy(hn_ref, xs_hbm.at[pl.ds(0, td * ROW_CHUNKS), :], ssem).wait()


def _tile_major(dest, tile):
    t = dest.shape[1]
    return dest.reshape(TOP_K, t // tile, tile).transpose(1, 0, 2).reshape(t // tile, TOP_K * tile)


def moe_dispatch(lay, hn, dest, zero_row, n_rows):
    td = DISPATCH_TILE
    n_tiles = lay.t // td
    return pl.pallas_call(
        functools.partial(_dispatch_kernel, n_tiles=n_tiles),
        grid_spec=pltpu.PrefetchScalarGridSpec(
            num_scalar_prefetch=1,
            grid=(n_tiles,),
            in_specs=[pl.BlockSpec(memory_space=pl.ANY),
                      pl.BlockSpec((td * ROW_CHUNKS, LANES), lambda i, z: (i, 0))],
            out_specs=pl.BlockSpec(memory_space=pl.ANY),
            scratch_shapes=[pltpu.SMEM((2 * TOP_K * td,), I32),
                            pltpu.VMEM((MOE_BLOCK * ROW_CHUNKS, LANES), PACKED),
                            pltpu.SemaphoreType.DMA((2,)),
                            pltpu.SemaphoreType.DMA,
                            pltpu.SemaphoreType.DMA]),
        out_shape=jax.ShapeDtypeStruct((n_rows * ROW_CHUNKS, LANES), PACKED),
        compiler_params=_cparams(("arbitrary",)),
        name="moe_dispatch",
    )(zero_row, dest, hn)


def _expert_kernel(blk0_ref, nblk_ref, tail_ref, xs_hbm, wg_ref, wu_ref, wd_ref, y_hbm,
                   xbuf, ybuf, wg_s, wu_s, wd_s, isem, osem):
    e = pl.program_id(0)
    nb = nblk_ref[e]
    g0 = blk0_ref[e]
    total = blk0_ref[N_EXPERTS - 1] + nblk_ref[N_EXPERTS - 1]
    blk_rows = MOE_BLOCK * ROW_CHUNKS
    n_x = xbuf.shape[0]
    n_y = ybuf.shape[0]

    def block_rows(g):
        return pl.ds(pl.multiple_of(g * blk_rows, blk_rows), blk_rows)

    def fetch(g):
        s = g % n_x
        return pltpu.make_async_copy(xs_hbm.at[block_rows(g), :], xbuf.at[s], isem.at[s])

    def writeback(g):
        s = g % n_y
        return pltpu.make_async_copy(ybuf.at[s], y_hbm.at[block_rows(g), :], osem.at[s])

    ahead = n_x // 2

    @pl.when(e == 0)
    def _():
        for p in range(ahead):
            @pl.when(p < total)
            def _():
                fetch(p).start()

    @pl.when(nb > 0)
    def _():
        wg_s[...] = wg_ref[0, 0].astype(BF16)
        wu_s[...] = wu_ref[0, 0].astype(BF16)
        wd_s[...] = wd_ref[0, 0].astype(BF16)

    def run_blocks(g, n):
        for p in range(n):
            @pl.when(g + ahead + p < total)
            def _():
                fetch(g + ahead + p).start()
        for p in range(n):
            fetch(g + p).wait()
        x = jnp.concatenate([_rowtile_load(xbuf.at[(g + p) % n_x], MOE_BLOCK) for p in range(n)],
                            axis=0).astype(BF16)
        gate = jnp.dot(x, wg_s[...], preferred_element_type=F32)
        up = jnp.dot(x, wu_s[...], preferred_element_type=F32)
        act = gate * jax.nn.sigmoid(gate) * up
        y = jnp.dot(act.astype(BF16), wd_s[...], preferred_element_type=F32)
        for p in range(n):
            @pl.when(g + p >= n_y)
            def _():
                writeback(g + p - n_y).wait()
        for p in range(n):
            _rowtile_store(ybuf.at[(g + p) % n_y], y[p * MOE_BLOCK:(p + 1) * MOE_BLOCK], MOE_BLOCK)
            writeback(g + p).start()

    def quad(jj, c):
        run_blocks(g0 + 4 * jj, 4)
        return c

    lax.fori_loop(0, nb // 4, quad, 0)
    rem = nb % 4

    @pl.when(rem >= 2)
    def _():
        run_blocks(g0 + nb - rem, 2)

    @pl.when(rem % 2 == 1)
    def _():
        run_blocks(g0 + nb - 1, 1)

    @pl.when(e == N_EXPERTS - 1)
    def _():
        for p in range(n_y, 0, -1):
            @pl.when(total >= p)
            def _():
                writeback(total - p).wait()

        ybuf[0] = jnp.zeros(ybuf.shape[1:], PACKED)

        def tail_copy(i):
            r0 = pl.multiple_of(tail_ref[i] * ROW_CHUNKS, blk_rows)
            return pltpu.make_async_copy(ybuf.at[0], y_hbm.at[pl.ds(r0, blk_rows), :], osem.at[0])

        def tstart(i, c):
            @pl.when(tail_ref[i] >= 0)
            def _():
                tail_copy(i).start()
            return c

        def twait(i, c):
            @pl.when(tail_ref[i] >= 0)
            def _():
                tail_copy(i).wait()
            return c

        lax.fori_loop(0, tail_ref.shape[0], tstart, 0)
        lax.fori_loop(0, tail_ref.shape[0], twait, 0)


def moe_experts(xs, first_block, n_blocks, tail_row, layer, w_gate, w_up, w_down):
    wspec = lambda shape: pl.BlockSpec((1, 1) + shape, lambda e, a, b, c: (layer, e, 0, 0))
    blk = (MOE_BLOCK * ROW_CHUNKS, LANES)
    return pl.pallas_call(
        _expert_kernel,
        grid_spec=pltpu.PrefetchScalarGridSpec(
            num_scalar_prefetch=3,
            grid=(N_EXPERTS,),
            in_specs=[pl.BlockSpec(memory_space=pl.ANY),
                      wspec((D_MODEL, D_EXPERT)), wspec((D_MODEL, D_EXPERT)),
                      wspec((D_EXPERT, D_MODEL))],
            out_specs=pl.BlockSpec(memory_space=pl.ANY),
            scratch_shapes=[pltpu.VMEM((EXPERT_X_BUFS,) + blk, PACKED),
                            pltpu.VMEM((EXPERT_Y_BUFS,) + blk, PACKED),
                            pltpu.VMEM((D_MODEL, D_EXPERT), BF16),
                            pltpu.VMEM((D_MODEL, D_EXPERT), BF16),
                            pltpu.VMEM((D_EXPERT, D_MODEL), BF16),
                            pltpu.SemaphoreType.DMA((EXPERT_X_BUFS,)),
                            pltpu.SemaphoreType.DMA((EXPERT_Y_BUFS,))]),
        out_shape=jax.ShapeDtypeStruct(xs.shape, PACKED),
        compiler_params=_cparams(("arbitrary",)),
        name="moe_experts",
    )(first_block, n_blocks, tail_row, xs, w_gate, w_up, w_down)


def _combine_kernel(dest_hbm, y_hbm, x_ref, sh_ref, w_ref, mod_ref, *rest, n_tiles, final, proj):
    if final:
        gf_ref, oc_ref, ol_ref, idx_s, ybuf, isem, gsem = rest
    elif proj:
        gn_ref, modn_ref, wn_ref, o_ref, p_ref, idx_s, ybuf, isem, gsem = rest
    else:
        o_ref, idx_s, ybuf, isem, gsem = rest
    i = pl.program_id(0)
    tm = COMBINE_TILE
    n_idx = TOP_K * tm
    last = n_tiles - 1

    def idx_copy(tile, s):
        return pltpu.make_async_copy(dest_hbm.at[tile], idx_s.at[pl.ds(s * n_idx, n_idx)], isem.at[s])

    def gather(s, unroll=4):
        _issue_row_copies(lambda r: idx_s[s * n_idx + r], n_idx,
                          lambda r, d: _row_copy(y_hbm, d, ybuf.at[s], r, gsem.at[s]), unroll)

    def gather_wait(s):
        pltpu.make_async_copy(y_hbm.at[pl.ds(0, n_idx * ROW_CHUNKS), :], ybuf.at[s], gsem.at[s]).wait()

    @pl.when(i == 0)
    def _():
        c = idx_copy(0, 0)
        c.start()
        c.wait()
        gather(0)
        idx_copy(jnp.minimum(1, last), 1).start()

    for half in range(2):
        tile = 2 * i + half
        s = half
        rows = slice(half * tm, (half + 1) * tm)
        idx_copy(jnp.minimum(tile + 1, last), 1 - s).wait()
        gather_wait(s)
        gather(1 - s, unroll=True)
        idx_copy(jnp.minimum(tile + 2, last), s).start()

        w = w_ref[rows, :]
        routed = jnp.zeros((tm, D_MODEL), F32)
        for k in range(TOP_K):
            routed = routed + w[:, k:k + 1] * _rowtile_load(ybuf.at[s], tm, base=k * tm * ROW_CHUNKS)
        out = x_ref[rows, :] + mod_ref[0, 5:6, :] * (routed + sh_ref[rows, :])
        if final:
            ms = jnp.mean(out * out, axis=-1, keepdims=True)
            y = out * lax.rsqrt(ms + EPS) * gf_ref[...]
            oc_ref[rows, :] = y
            ol_ref[rows, :] = y
        else:
            o_ref[rows, :] = out
            if proj:
                hn = _modnorm(out, gn_ref[...], modn_ref, 0)
                p_ref[rows, :] = jnp.dot(hn.astype(BF16), wn_ref[...], preferred_element_type=F32)

    @pl.when(i == n_tiles // 2 - 1)
    def _():
        gather_wait(0)
        idx_copy(last, 1).wait()


def moe_combine(lay, x, shared, y_rows, dest, wts_t, mods, g_final=None, next_proj=None):
    tm = COMBINE_TILE
    n_tiles = lay.t // tm
    assert n_tiles % 2 == 0
    step = 2 * tm
    nct = lay.t_ctx // step
    mi = _mod_index(lay, step)
    row = lambda i: (i, 0)
    final = g_final is not None
    if final:
        extra_in = [pl.BlockSpec((1, D_MODEL), lambda i: (0, 0))]
        extra_args = [g_final.reshape(1, -1)]
        out_specs = [pl.BlockSpec((step, D_MODEL), lambda i: (jnp.minimum(i, nct), 0)),
                     pl.BlockSpec((step, D_MODEL), lambda i: (jnp.maximum(i - nct, 0), 0))]
        out_shape = [jax.ShapeDtypeStruct((lay.t_ctx + step, D_MODEL), F32),
                     jax.ShapeDtypeStruct((lay.t_lat, D_MODEL), F32)]
    elif next_proj is not None:
        g_next, mods_next, w_next = next_proj
        n_out = w_next.shape[1]
        extra_in = [pl.BlockSpec((1, D_MODEL), lambda i: (0, 0)),
                    pl.BlockSpec((1, N_MOD, D_MODEL), lambda i: (mi(i), 0, 0)),
                    pl.BlockSpec((D_MODEL, n_out), lambda i: (0, 0))]
        extra_args = [g_next.reshape(1, -1), mods_next, w_next]
        out_specs = [pl.BlockSpec((step, D_MODEL), row), pl.BlockSpec((step, n_out), row)]
        out_shape = [jax.ShapeDtypeStruct((lay.t, D_MODEL), F32),
                     jax.ShapeDtypeStruct((lay.t, n_out), F32)]
    else:
        extra_in, extra_args = [], []
        out_specs = pl.BlockSpec((step, D_MODEL), row)
        out_shape = jax.ShapeDtypeStruct((lay.t, D_MODEL), F32)
    return pl.pallas_call(
        functools.partial(_combine_kernel, n_tiles=n_tiles, final=final,
                          proj=next_proj is not None and not final),
        grid=(n_tiles // 2,),
        in_specs=[pl.BlockSpec(memory_space=pl.ANY),
                  pl.BlockSpec(memory_space=pl.ANY),
                  pl.BlockSpec((step, D_MODEL), row),
                  pl.BlockSpec((step, D_MODEL), row),
                  pl.BlockSpec((step, TOP_K), row),
                  pl.BlockSpec((1, N_MOD, D_MODEL), lambda i: (mi(i), 0, 0))] + extra_in,
        out_specs=out_specs,
        out_shape=out_shape,
        scratch_shapes=[pltpu.SMEM((2 * TOP_K * tm,), I32),
                        pltpu.VMEM((2, TOP_K * tm * ROW_CHUNKS, LANES), PACKED),
                        pltpu.SemaphoreType.DMA((2,)),
                        pltpu.SemaphoreType.DMA((2,))],
        compiler_params=_cparams(("arbitrary",)),
        name="moe_combine",
    )(dest, y_rows, x, shared, wts_t, mods, *extra_args)


def moe_layer(lay, x, g, mods, router_w, router_b, layer, w_gate, w_up, w_down, s_gate, s_up, s_down,
              g_final=None, next_proj=None):
    t = lay.t
    hn, shared, eidx, wts, rank, cnt = moe_router(
        lay, x, g, mods, router_w, router_b, s_gate.astype(BF16), s_up.astype(BF16), s_down.astype(BF16))
    counts = cnt[:, 0].astype(I32)
    n_blocks = (counts + MOE_BLOCK - 1) // MOE_BLOCK
    padded = n_blocks * MOE_BLOCK
    pad_end = jnp.cumsum(padded)
    pad_start = pad_end - padded
    n_rows = -(-(t * TOP_K + N_EXPERTS * (MOE_BLOCK - 1)) // MOE_BLOCK) * MOE_BLOCK
    dest = moe_dest(pad_start, eidx, rank)
    last_row = jnp.where(n_blocks > 0, pad_end - MOE_BLOCK, -1)
    tail_blk = pad_end[-1] // MOE_BLOCK + jnp.arange(n_rows // MOE_BLOCK - t * TOP_K // MOE_BLOCK)
    tail_row = jnp.where(tail_blk < n_rows // MOE_BLOCK, tail_blk * MOE_BLOCK, -1).astype(I32)
    xs = moe_dispatch(lay, hn, _tile_major(dest, DISPATCH_TILE),
                      jnp.concatenate([last_row, tail_row]), n_rows)
    y_rows = moe_experts(xs, pad_start // MOE_BLOCK, n_blocks, tail_row, layer, w_gate, w_up, w_down)
    return moe_combine(lay, x, shared, y_rows, _tile_major(dest, COMBINE_TILE), wts.T, mods,
                       g_final, next_proj)


def _block_diag(w):
    nb, bw, _ = w.shape
    eye = jnp.eye(nb, dtype=w.dtype)
    return (eye[:, None, :, None] * w[:, :, None, :]).reshape(nb * bw, nb * bw)


def even_layer(lay, x, mods, g_mix, p, state_lru, state_ssm_re, state_ssm_im):
    t = lay.t
    proj, u_g = modnorm_matmul(lay, x, g_mix, mods, 0, p['w_in'].astype(BF16),
                               ug_col=2 * D_LRU)
    zeros_c = jnp.zeros((lay.n_ctx, D_LRU), F32)
    hf_y, st = None, []
    for d in range(2):
        wg = jnp.concatenate([_block_diag(p['lru_wa'][d]), _block_diag(p['lru_wx'][d])], axis=1)
        bg = jnp.concatenate([p['lru_ba'][d], p['lru_bx'][d]])
        h0 = jnp.concatenate([zeros_c, state_lru[:, d].astype(F32)], axis=0)
        hf_y, s = lru_pass(lay, proj, p['conv_w'], p['conv_b'], wg.astype(BF16), bg,
                           p['lru_lam'][d], h0, reverse=(d == 1), hf=hf_y)
        st.append(s[:lay.n_ctx])
    y_a = hf_y
    new_lru = jnp.stack(st, axis=1)

    mats = _s5_matrices(p['a_re'], p['a_im'], p['log_dt'], p['b_re'], p['b_im'], p['c_re'], p['c_im'])
    h0 = jnp.concatenate([state_ssm_re, state_ssm_im], axis=-1).astype(F32)
    h0 = h0.transpose(2, 1, 0, 3)
    y_g, h_ctx = s5_mixer(lay, u_g, mats, h0)
    seg = lay.s_ctx // S5_CHUNK
    h_ctx = h_ctx.reshape(SSM_GROUPS, 2, lay.n_ctx, seg, 2 * SSM_STATE)
    ends = jnp.stack([h_ctx[:, 0, :, seg - 1], h_ctx[:, 1, :, 0]], axis=1)
    ends = ends.transpose(2, 1, 0, 3)
    x = even_out(lay, x, y_a, y_g, proj, p['d'], p['glu_w'].astype(BF16), p['glu_b'],
                 p['w_out'].astype(BF16), mods)
    return x, new_lru, ends[..., :SSM_STATE], ends[..., SSM_STATE:]


def odd_layer(lay, x, mods, g_mix, w_qkv, sink, w_out, cache_k, cache_v, qkv=None):
    if qkv is None:
        qkv = modnorm_matmul(lay, x, g_mix, mods, 0, w_qkv.astype(BF16))
    o_ctx = attn_context(lay, qkv, sink)
    o_lat = attn_latent(lay, qkv, cache_k, cache_v, sink)
    nq = N_HEADS * HEAD_DIM
    kv = qkv[:lay.t_ctx, nq:].reshape(lay.n_ctx, lay.s_ctx, 2, N_KV, HEAD_DIM)
    k_new = kv[:, :, 0].swapaxes(1, 2)
    v_new = kv[:, :, 1].swapaxes(1, 2)
    x = matmul_residual(lay, x, o_ctx, o_lat, w_out.astype(BF16), mods)
    return x, k_new, v_new


def _forward(lay, x_prompt, x_sample, state_lru, state_ssm_re, state_ssm_im, cache_k, cache_v,
             c, c_ctx, g_mix, g_ffn, w_mod, b_mod,
             ev_w_in, lru_conv_w, lru_conv_b, lru_wa, lru_ba, lru_wx, lru_bx, lru_lam,
             ssm_a_re, ssm_a_im, ssm_log_dt, ssm_b_re, ssm_b_im, ssm_c_re, ssm_c_im, ssm_d,
             ssm_glu_w, ssm_glu_b, ev_w_out, at_w_qkv, at_sink, at_w_out,
             router_w, router_b, exp_w_gate, exp_w_up, exp_w_down, sh_w_gate, sh_w_up, sh_w_down,
             g_final):
    depth = g_mix.shape[0]
    x = (x_prompt.reshape(lay.t_ctx, D_MODEL), x_sample.reshape(lay.t_lat, D_MODEL))
    n_c = 1 + lay.n_lat
    c_rows = jnp.concatenate([c_ctx[None, :], c, jnp.zeros((16 - n_c, D_MODEL), F32)], axis=0)
    new_lru, new_re, new_im, new_k, new_v = [], [], [], [], []
    all_mods = [adaln_table(c_rows, l, w_mod, b_mod[l]) for l in range(depth)]
    qkv = None
    for l in range(depth):
        i = l // 2
        mods = all_mods[l]
        if l % 2 == 0:
            p = dict(w_in=ev_w_in[i], conv_w=lru_conv_w[i], conv_b=lru_conv_b[i],
                     lru_wa=lru_wa[i], lru_ba=lru_ba[i], lru_wx=lru_wx[i], lru_bx=lru_bx[i],
                     lru_lam=lru_lam[i], a_re=ssm_a_re[i], a_im=ssm_a_im[i], log_dt=ssm_log_dt[i],
                     b_re=ssm_b_re[i], b_im=ssm_b_im[i], c_re=ssm_c_re[i], c_im=ssm_c_im[i],
                     d=ssm_d[i], glu_w=ssm_glu_w[i], glu_b=ssm_glu_b[i], w_out=ev_w_out[i])
            x, lru_i, re_i, im_i = even_layer(lay, x, mods, g_mix[l], p, state_lru[:, i],
                                              state_ssm_re[:, i], state_ssm_im[:, i])
            new_lru.append(lru_i)
            new_re.append(re_i)
            new_im.append(im_i)
        else:
            if isinstance(x, tuple):
                x = jnp.concatenate(x, axis=0)
            x, k_i, v_i = odd_layer(lay, x, mods, g_mix[l], at_w_qkv[i], at_sink[i], at_w_out[i],
                                    cache_k[:, i], cache_v[:, i], qkv)
            new_k.append(k_i)
            new_v.append(v_i)
        next_proj = None
        if l + 1 < depth and (l + 1) % 2 == 1:
            next_proj = (g_mix[l + 1], all_mods[l + 1], at_w_qkv[(l + 1) // 2].astype(BF16))
        x = moe_layer(lay, x, g_ffn[l], mods, router_w[l], router_b[l], l, exp_w_gate, exp_w_up,
                      exp_w_down, sh_w_gate[l], sh_w_up[l], sh_w_down[l],
                      g_final=g_final if l == depth - 1 else None, next_proj=next_proj)
        qkv = None
        if next_proj is not None:
            x, qkv = x
    y_ctx, y_lat = x
    y_prompt = y_ctx[:lay.t_ctx].reshape(x_prompt.shape)
    y_sample = y_lat.reshape(x_sample.shape)
    return (y_prompt, y_sample, jnp.stack(new_lru, axis=1), jnp.stack(new_re, axis=1),
            jnp.stack(new_im, axis=1), jnp.stack(new_k, axis=1), jnp.stack(new_v, axis=1))


def kernel(x_prompt, x_sample, state_lru, state_ssm_re, state_ssm_im, cache_k, cache_v, c, c_ctx, g_mix, g_ffn, w_mod, b_mod, ev_w_in, lru_conv_w, lru_conv_b, lru_wa, lru_ba, lru_wx, lru_bx, lru_lam, ssm_a_re, ssm_a_im, ssm_log_dt, ssm_b_re, ssm_b_im, ssm_c_re, ssm_c_im, ssm_d, ssm_glu_w, ssm_glu_b, ev_w_out, at_w_qkv, at_sink, at_w_out, router_w, router_b, exp_w_gate, exp_w_up, exp_w_down, sh_w_gate, sh_w_up, sh_w_down, g_final):
    lay = Layout(n_ctx=x_prompt.shape[0], s_ctx=x_prompt.shape[1],
                 n_lat=x_sample.shape[0], s_lat=x_sample.shape[1])
    return _forward(lay, x_prompt, x_sample, state_lru, state_ssm_re, state_ssm_im, cache_k, cache_v,
                    c, c_ctx, g_mix, g_ffn, w_mod, b_mod,
                    ev_w_in, lru_conv_w, lru_conv_b, lru_wa, lru_ba, lru_wx, lru_bx, lru_lam,
                    ssm_a_re, ssm_a_im, ssm_log_dt, ssm_b_re, ssm_b_im, ssm_c_re, ssm_c_im, ssm_d,
                    ssm_glu_w, ssm_glu_b, ev_w_out, at_w_qkv, at_sink, at_w_out,
                    router_w, router_b, exp_w_gate, exp_w_up, exp_w_down, sh_w_gate, sh_w_up,
                    sh_w_down, g_final)
```

```python
import functools
from typing import NamedTuple

import jax
import jax.numpy as jnp
from jax import lax
from jax.experimental import pallas as pl
from jax.experimental.pallas import tpu as pltpu

F32 = jnp.float32
BF16 = jnp.bfloat16
I32 = jnp.int32
HIGHEST = lax.Precision.HIGHEST

D_MODEL = 1024
EPS = 1e-6
N_MOD = 6
GRID_W = 64
D_LRU = 512
LRU_BLOCKS = 8
LRU_C = 8.0
CONV_W = 4
CONV_LEFT = 2
D_SSM = 512
SSM_GROUP = 16
SSM_GROUPS = 32
SSM_STATE = 64
S5_CHUNK = 16
S5_LANES = S5_CHUNK * SSM_GROUP
S5_SCAN_STEPS = 8
HEAD_DIM = 64
N_HEADS = 16
N_KV = 4
GQA = 4
WINDOW = 128
Q_BLOCK = 128
ROPE_BASE = 10000.0
ATTN_SCALE = HEAD_DIM ** -0.5
NEG_INF = -1e30
N_EXPERTS = 256
TOP_K = 8
N_GROUPS = 8
TOPK_GROUPS = 4
GROUP_SIZE = N_EXPERTS // N_GROUPS
D_EXPERT = 256
ROUTE_SCALE = 2.5
MOE_BLOCK = 128

SUBLANES = 8
LANES = 128
ROW_CHUNKS = D_MODEL // (2 * LANES)
PACKED = jnp.int32
SEQ_TILE = 256
ROW_TILE = 512
ROUTER_TILE = 512
DEST_TILE = 1024
DISPATCH_TILE = 1024
COMBINE_TILE = 128
COMBINE_TILES_PER_STEP = 4
COMBINE_LOOKAHEAD = 2
EXPERT_X_BUFS = 16
EXPERT_Y_BUFS = 8
VMEM_LIMIT = 56 * 1024 * 1024


class Layout(NamedTuple):
    n_ctx: int
    s_ctx: int
    n_lat: int
    s_lat: int

    @property
    def t_ctx(self):
        return self.n_ctx * self.s_ctx

    @property
    def t_lat(self):
        return self.n_lat * self.s_lat

    @property
    def t(self):
        return self.t_ctx + self.t_lat

    @property
    def n_seq(self):
        return self.n_ctx + self.n_lat


def _cparams(sem):
    return pltpu.CompilerParams(dimension_semantics=sem, vmem_limit_bytes=VMEM_LIMIT)


def _mod_index(lay, tile_rows):
    n_ctx_tiles = lay.t_ctx // tile_rows
    per_lat = lay.s_lat // tile_rows

    def f(i):
        return jnp.where(i < n_ctx_tiles, 0, 1 + (i - n_ctx_tiles) // per_lat)
    return f


def _adaln_kernel(c_ref, w_ref, b_ref, o_ref):
    c = c_ref[...]
    s = c * jax.nn.sigmoid(c)
    o_ref[...] = jnp.dot(s, w_ref[0], precision=HIGHEST, preferred_element_type=F32) + b_ref[...]


def adaln_table(c_rows, layer, w_mod, b_mod):
    n = c_rows.shape[0]
    tn = 1536
    out = pl.pallas_call(
        _adaln_kernel,
        grid=(N_MOD * D_MODEL // tn,),
        in_specs=[pl.BlockSpec((n, D_MODEL), lambda j: (0, 0)),
                  pl.BlockSpec((1, D_MODEL, tn), lambda j: (layer, 0, j)),
                  pl.BlockSpec((1, tn), lambda j: (0, j))],
        out_specs=pl.BlockSpec((n, tn), lambda j: (0, j)),
        out_shape=jax.ShapeDtypeStruct((n, N_MOD * D_MODEL), F32),
        compiler_params=_cparams(("arbitrary",)),
        name="adaln",
    )(c_rows, w_mod, b_mod.reshape(1, -1))
    return out.reshape(n, N_MOD, D_MODEL)


def _modnorm(x, g, mod_ref, slot):
    ms = jnp.mean(x * x, axis=-1, keepdims=True)
    y = x * lax.rsqrt(ms + EPS) * g
    shift = mod_ref[0, slot:slot + 1, :]
    scale = mod_ref[0, slot + 1:slot + 2, :]
    return y * (1.0 + scale) + shift


GROUPS_PER_VREG = LANES // SSM_GROUP
SSM_COL_BLOCKS = D_SSM // LANES


def _group_major_store(val, tmp_ref, dst_ref):
    rows = dst_ref.shape[1]
    for j in range(SSM_COL_BLOCKS):
        tmp_ref[j] = val[:, j * LANES:(j + 1) * LANES]
    for j in range(SSM_COL_BLOCKS):
        steps = [tmp_ref[j, pl.ds(i, rows, stride=S5_CHUNK), :] for i in range(S5_CHUNK)]
        for q in range(GROUPS_PER_VREG):
            dst_ref[j * GROUPS_PER_VREG + q] = jnp.concatenate(
                [w[:, q * SSM_GROUP:(q + 1) * SSM_GROUP] for w in steps], axis=1)


def _group_major_load(src_ref, tmp_ref):
    rows = src_ref.shape[1]
    for j in range(SSM_COL_BLOCKS):
        blocks = [src_ref[j * GROUPS_PER_VREG + q] for q in range(GROUPS_PER_VREG)]
        for i in range(S5_CHUNK):
            tmp_ref[j, pl.ds(i, rows, stride=S5_CHUNK), :] = jnp.concatenate(
                [b[:, i * SSM_GROUP:(i + 1) * SSM_GROUP] for b in blocks], axis=1)
    return jnp.concatenate([tmp_ref[j] for j in range(SSM_COL_BLOCKS)], axis=1)


def _split_row_specs(lay, tile_rows, width):
    nct = lay.t_ctx // tile_rows
    return [pl.BlockSpec((tile_rows, width), lambda i: (jnp.minimum(i, nct - 1), 0)),
            pl.BlockSpec((tile_rows, width), lambda i: (jnp.maximum(i - nct, 0), 0))], nct


def _pick_rows(ctx_ref, lat_ref, n_ctx_tiles):
    return jnp.where(pl.program_id(0) < n_ctx_tiles, ctx_ref[...], lat_ref[...])


def _modnorm_mm_kernel(*refs, slot, ug_col, n_ctx_tiles):
    if n_ctx_tiles is None:
        x = refs[0][...]
        refs = refs[1:]
    else:
        x = _pick_rows(refs[0], refs[1], n_ctx_tiles)
        refs = refs[2:]
    g_ref, mod_ref, w_ref, o_ref, *ug_refs = refs
    h = _modnorm(x, g_ref[...], mod_ref, slot)
    out = jnp.dot(h.astype(BF16), w_ref[...], preferred_element_type=F32)
    o_ref[...] = out
    if ug_col is not None:
        ug_ref, tmp_ref = ug_refs
        _group_major_store(out[:, ug_col:ug_col + D_SSM], tmp_ref, ug_ref)


def modnorm_matmul(lay, x, g, mods, slot, w_bf16, ug_col=None):
    t = lay.t
    n = w_bf16.shape[1]
    mi = _mod_index(lay, ROW_TILE)
    out_specs = [pl.BlockSpec((ROW_TILE, n), lambda i: (i, 0))]
    out_shape = [jax.ShapeDtypeStruct((t, n), F32)]
    if ug_col is not None:
        out_specs.append(pl.BlockSpec((SSM_GROUPS, ROW_TILE // S5_CHUNK, S5_LANES), lambda i: (0, i, 0)))
        out_shape.append(jax.ShapeDtypeStruct((SSM_GROUPS, t // S5_CHUNK, S5_LANES), F32))
    if isinstance(x, tuple):
        x_specs, nct = _split_row_specs(lay, ROW_TILE, D_MODEL)
    else:
        x_specs, nct, x = [pl.BlockSpec((ROW_TILE, D_MODEL), lambda i: (i, 0))], None, (x,)
    outs = pl.pallas_call(
        functools.partial(_modnorm_mm_kernel, slot=slot, ug_col=ug_col, n_ctx_tiles=nct),
        grid=(t // ROW_TILE,),
        in_specs=x_specs + [
                  pl.BlockSpec((1, D_MODEL), lambda i: (0, 0)),
                  pl.BlockSpec((1, N_MOD, D_MODEL), lambda i: (mi(i), 0, 0)),
                  pl.BlockSpec((D_MODEL, n), lambda i: (0, 0))],
        out_specs=out_specs,
        out_shape=out_shape,
        scratch_shapes=([pltpu.VMEM((SSM_COL_BLOCKS, ROW_TILE, LANES), F32)]
                        if ug_col is not None else []),
        compiler_params=_cparams(("arbitrary",)),
        name="modnorm_matmul",
    )(*x, g.reshape(1, -1), mods, w_bf16)
    return outs if ug_col is not None else outs[0]


def _seq_tile_maps(lay, reverse):
    assert lay.s_ctx == SEQ_TILE and lay.s_lat % SEQ_TILE == 0
    n_tiles = lay.t // SEQ_TILE
    per_lat = lay.s_lat // SEQ_TILE

    def tile(i):
        return (n_tiles - 1 - i) if reverse else i

    def seq(i):
        ti = tile(i)
        return jnp.where(ti < lay.n_ctx, ti, lay.n_ctx + (ti - lay.n_ctx) // per_lat)

    return n_tiles, tile, seq


def _softplus(x):
    return jnp.maximum(x, 0.0) + jnp.log(1.0 + jnp.exp(-jnp.abs(x)))


def _lru_kernel(rec_ref, prev_ref, next_ref, cw_ref, cb_ref, wg_ref, bg_ref, lam_ref, h0_ref,
                *rest, reverse, n_ctx, per_lat, n_tiles):
    if reverse:
        gate_ref, hf_ref, y_ref, st_ref, a_s, b_s, h_s, carry = rest
    else:
        y_ref, st_ref, a_s, b_s, h_s, carry = rest
    i = pl.program_id(0)
    ti = (n_tiles - 1 - i) if reverse else i
    is_first = jnp.logical_or(ti < n_ctx, (ti - n_ctx) % per_lat == 0)
    is_last = jnp.logical_or(ti < n_ctx, (ti - n_ctx) % per_lat == per_lat - 1)
    ts = SEQ_TILE

    rec = rec_ref[...]
    prev = jnp.where(is_first, 0.0, prev_ref[...])
    nxt = jnp.where(is_last, 0.0, next_ref[...])
    ext = jnp.concatenate([prev, rec, nxt], axis=0)
    n_ext = ts + 2 * SUBLANES
    cw = cw_ref[...]
    xc = cb_ref[...] + cw[2:3, :] * rec
    xc = xc + cw[0:1, :] * pltpu.roll(ext, 2, 0)[SUBLANES:SUBLANES + ts]
    xc = xc + cw[1:2, :] * pltpu.roll(ext, 1, 0)[SUBLANES:SUBLANES + ts]
    xc = xc + cw[3:4, :] * pltpu.roll(ext, n_ext - 1, 0)[SUBLANES:SUBLANES + ts]

    gates = jax.nn.sigmoid(jnp.dot(xc.astype(BF16), wg_ref[...], preferred_element_type=F32)
                           + bg_ref[...])
    r = gates[:, :D_LRU]
    ig = gates[:, D_LRU:]
    log_a = (-LRU_C) * r * _softplus(-lam_ref[...])
    a = jnp.exp(log_a)
    b = jnp.sqrt(1.0 - jnp.exp(2.0 * log_a)) * (ig * xc)

    row8 = lax.broadcasted_iota(I32, (ts, D_LRU), 0) % SUBLANES
    for sh in (1, 2, 4):
        if reverse:
            keep = row8 < SUBLANES - sh
            a_sh = pltpu.roll(a, ts - sh, 0)
            b_sh = pltpu.roll(b, ts - sh, 0)
        else:
            keep = row8 >= sh
            a_sh = pltpu.roll(a, sh, 0)
            b_sh = pltpu.roll(b, sh, 0)
        b = b + a * jnp.where(keep, b_sh, 0.0)
        a = a * jnp.where(keep, a_sh, 1.0)
    a_s[...] = a
    b_s[...] = b

    @pl.when(is_last if reverse else is_first)
    def _():
        carry[...] = h0_ref[0]

    n_grp = ts // SUBLANES

    def body(k, c):
        gi = (n_grp - 1 - k) if reverse else k
        sl = pl.ds(pl.multiple_of(gi * SUBLANES, SUBLANES), SUBLANES)
        h = b_s[sl, :] + a_s[sl, :] * c
        h_s[sl, :] = h
        return h[0:1, :] if reverse else h[SUBLANES - 1:SUBLANES, :]

    c_fin = lax.fori_loop(0, n_grp, body, carry[...], unroll=4)
    carry[...] = c_fin
    st_ref[0] = c_fin
    if reverse:
        y_ref[...] = (hf_ref[...] + h_s[...]) * jax.nn.gelu(gate_ref[...])
    else:
        y_ref[...] = h_s[...]


def lru_pass(lay, proj, conv_w, conv_b, wg_bf16, bg, lam, h0, reverse, hf=None):
    n_tiles, tile, seq = _seq_tile_maps(lay, reverse)
    per_lat = lay.s_lat // SEQ_TILE
    blk8 = SEQ_TILE // SUBLANES
    last8 = lay.t // SUBLANES - 1
    c = D_LRU
    in_specs = [
        pl.BlockSpec((SEQ_TILE, c), lambda i: (tile(i), 1)),
        pl.BlockSpec((SUBLANES, c), lambda i: (jnp.maximum(tile(i) * blk8 - 1, 0), 1)),
        pl.BlockSpec((SUBLANES, c), lambda i: (jnp.minimum(tile(i) * blk8 + blk8, last8), 1)),
        pl.BlockSpec((CONV_W, c), lambda i: (0, 0)),
        pl.BlockSpec((1, c), lambda i: (0, 0)),
        pl.BlockSpec((c, 2 * c), lambda i: (0, 0)),
        pl.BlockSpec((1, 2 * c), lambda i: (0, 0)),
        pl.BlockSpec((1, c), lambda i: (0, 0)),
        pl.BlockSpec((1, 1, c), lambda i: (seq(i), 0, 0)),
    ]
    args = [proj, proj, proj, conv_w, conv_b.reshape(1, -1), wg_bf16, bg.reshape(1, -1),
            lam.reshape(1, -1), h0.reshape(lay.n_seq, 1, c)]
    if reverse:
        in_specs += [pl.BlockSpec((SEQ_TILE, c), lambda i: (tile(i), 0)),
                     pl.BlockSpec((SEQ_TILE, c), lambda i: (tile(i), 0))]
        args += [proj, hf]
    y, st = pl.pallas_call(
        functools.partial(_lru_kernel, reverse=reverse, n_ctx=lay.n_ctx, per_lat=per_lat,
                          n_tiles=n_tiles),
        grid=(n_tiles,),
        in_specs=in_specs,
        out_specs=[pl.BlockSpec((SEQ_TILE, c), lambda i: (tile(i), 0)),
                   pl.BlockSpec((1, 1, c), lambda i: (seq(i), 0, 0))],
        out_shape=[jax.ShapeDtypeStruct((lay.t, c), F32),
                   jax.ShapeDtypeStruct((lay.n_seq, 1, c), F32)],
        scratch_shapes=[pltpu.VMEM((SEQ_TILE, c), F32), pltpu.VMEM((SEQ_TILE, c), F32),
                        pltpu.VMEM((SEQ_TILE, c), F32), pltpu.VMEM((1, c), F32)],
        compiler_params=_cparams(("arbitrary",)),
        name="lru_bwd" if reverse else "lru_fwd",
    )(*args)
    return y, st.reshape(lay.n_seq, c)


def _cmul(a, b):
    return a[0] * b[0] - a[1] * b[1], a[0] * b[1] + a[1] * b[0]


def _s5_matrices(a_re, a_im, log_dt, b_re, b_im, c_re, c_im):
    a_re, a_im = a_re.astype(F32), a_im.astype(F32)
    dt = jnp.exp(log_dt.astype(F32))[..., None]
    z = (a_re * dt, a_im * dt)

    def zpow(k):
        k = k.reshape((-1,) + (1,) * z[0].ndim)
        mag = jnp.exp(k * z[0][None])
        return mag * jnp.cos(k * z[1][None]), mag * jnp.sin(k * z[1][None])

    a_bar = zpow(jnp.ones((1,), F32))
    a_bar = (a_bar[0][0], a_bar[1][0])
    den = a_re * a_re + a_im * a_im
    xr, xi = a_bar[0] - 1.0, a_bar[1]
    q = ((xr * a_re + xi * a_im) / den, (xi * a_re - xr * a_im) / den)
    b_bar = _cmul((q[0][..., None], q[1][..., None]), (b_re.astype(F32), b_im.astype(F32)))
    cc = (c_re.astype(F32), c_im.astype(F32))
    el = S5_CHUNK
    pw = zpow(jnp.arange(el + 1, dtype=F32))
    idx = jnp.arange(el)
    m_in, m_toep, m_out = [], [], []
    for d in range(2):
        p_d = (pw[0][:, d], pw[1][:, d])
        b_d = (b_bar[0][d], b_bar[1][d])
        c_d = (cc[0][d], cc[1][d])
        k_in = (el - 1 - idx) if d == 0 else idx
        w_in = _cmul((p_d[0][k_in][..., None], p_d[1][k_in][..., None]),
                     (b_d[0][None], b_d[1][None]))
        w_in = [jnp.transpose(w, (1, 0, 3, 2)).reshape(SSM_GROUPS, S5_LANES, SSM_STATE) for w in w_in]
        m_in.append(jnp.concatenate(w_in, axis=-1))
        cp = _cmul((c_d[0][None], c_d[1][None]),
                   (p_d[0][:, :, None, :], p_d[1][:, :, None, :]))
        kern = (jnp.einsum('kghp,gpc->kgch', cp[0][:el], b_d[0])
                - jnp.einsum('kghp,gpc->kgch', cp[1][:el], b_d[1]))
        zero = jnp.zeros_like(kern[0])
        rows = []
        for i in range(el):
            if d == 0:
                pieces = [zero] * i + [kern[k] for k in range(el - i)]
            else:
                pieces = [kern[i - j] for j in range(i + 1)] + [zero] * (el - 1 - i)
            rows.append(jnp.concatenate(pieces, axis=-1))
        m_toep.append(jnp.stack(rows, axis=1).reshape(SSM_GROUPS, S5_LANES, S5_LANES))
        k_out = (idx + 1) if d == 0 else (el - idx)
        w_out = [jnp.transpose(w[k_out], (1, 3, 0, 2)).reshape(SSM_GROUPS, SSM_STATE, S5_LANES)
                 for w in cp]
        m_out.append(jnp.concatenate([w_out[0], -w_out[1]], axis=1))
    mul = zpow(el * 2.0 ** jnp.arange(S5_SCAN_STEPS, dtype=F32))
    mul = [jnp.transpose(m, (2, 1, 0, 3)) for m in mul]
    coef_a = jnp.concatenate([mul[0], mul[0]], axis=-1)
    coef_b = jnp.concatenate([-mul[1], mul[1]], axis=-1)
    stack = lambda xs: jnp.stack(xs, axis=1)
    return (stack(m_in).astype(BF16), stack(m_toep).astype(BF16), stack(m_out).astype(BF16),
            coef_a, coef_b)


def _s5_scan(v, ca, cb, seg, reverse):
    n = v.shape[0]
    assert seg <= 2 ** S5_SCAN_STEPS
    row = lax.broadcasted_iota(I32, (n, 2 * SSM_STATE), 0) % seg
    k, sh = 0, 1
    while sh < seg:
        if reverse:
            s = jnp.where(row < seg - sh, pltpu.roll(v, n - sh, 0), 0.0)
        else:
            s = jnp.where(row >= sh, pltpu.roll(v, sh, 0), 0.0)
        v = v + ca[k:k + 1, :] * s + cb[k:k + 1, :] * pltpu.roll(s, SSM_STATE, 1)
        k += 1
        sh *= 2
    return v


def _s5_shift(h, seg, reverse):
    n = h.shape[0]
    row = lax.broadcasted_iota(I32, (n, 2 * SSM_STATE), 0) % seg
    if reverse:
        return jnp.where(row < seg - 1, pltpu.roll(h, n - 1, 0), 0.0)
    return jnp.where(row >= 1, pltpu.roll(h, 1, 0), 0.0)


def _s5_kernel(u_ref, min_ref, mtoep_ref, mout_ref, ca_ref, cb_ref, h0_ref, y_ref, hc_ref,
               v_s, hp_s, *, rc, seg_c, n_lat, seg_l):
    u = u_ref[0].astype(BF16)
    u_c, u_l = u[:rc], u[rc:]
    y_c = jnp.zeros((rc, S5_LANES), F32)
    y_l = jnp.zeros((n_lat * seg_l, S5_LANES), F32)
    for d in range(2):
        reverse = d == 1
        ca = ca_ref[0, d]
        cb = cb_ref[0, d]
        m_in = min_ref[0, d]
        m_toep = mtoep_ref[0, d]
        m_out = mout_ref[0, d]
        h_c = _s5_scan(jnp.dot(u_c, m_in, preferred_element_type=F32), ca, cb, seg_c, reverse)
        hc_ref[0, d] = h_c
        hp_c = _s5_shift(h_c, seg_c, reverse)
        y_c = y_c + jnp.dot(u_c, m_toep, preferred_element_type=F32)
        y_c = y_c + jnp.dot(hp_c.astype(BF16), m_out, preferred_element_type=F32)
        v_s[...] = jnp.dot(u_l, m_in, preferred_element_type=F32)
        for s in range(n_lat):
            h0 = h0_ref[0, d, s:s + 1, :]
            r0 = s * seg_l + (seg_l - 1 if reverse else 0)
            v_s[r0:r0 + 1, :] = (v_s[r0:r0 + 1, :] + ca[0:1, :] * h0
                                 + cb[0:1, :] * pltpu.roll(h0, SSM_STATE, 1))
        h_l = _s5_scan(v_s[...], ca, cb, seg_l, reverse)
        hp_s[...] = _s5_shift(h_l, seg_l, reverse)
        for s in range(n_lat):
            r0 = s * seg_l + (seg_l - 1 if reverse else 0)
            hp_s[r0:r0 + 1, :] = h0_ref[0, d, s:s + 1, :]
        y_l = y_l + jnp.dot(u_l, m_toep, preferred_element_type=F32)
        y_l = y_l + jnp.dot(hp_s[...].astype(BF16), m_out, preferred_element_type=F32)
    y_ref[0, :rc, :] = y_c
    y_ref[0, rc:, :] = y_l


def s5_mixer(lay, u_g, mats, h0):
    m_in, m_toep, m_out, coef_a, coef_b = mats
    rows = lay.t // S5_CHUNK
    rc = lay.t_ctx // S5_CHUNK
    rl = rows - rc
    st2 = 2 * SSM_STATE
    g4 = lambda g: (g, 0, 0, 0)
    return pl.pallas_call(
        functools.partial(_s5_kernel, rc=rc, seg_c=lay.s_ctx // S5_CHUNK, n_lat=lay.n_lat,
                          seg_l=lay.s_lat // S5_CHUNK),
        grid=(SSM_GROUPS,),
        in_specs=[pl.BlockSpec((1, rows, S5_LANES), lambda g: (g, 0, 0)),
                  pl.BlockSpec((1, 2, S5_LANES, st2), g4),
                  pl.BlockSpec((1, 2, S5_LANES, S5_LANES), g4),
                  pl.BlockSpec((1, 2, st2, S5_LANES), g4),
                  pl.BlockSpec((1, 2, S5_SCAN_STEPS, st2), g4),
                  pl.BlockSpec((1, 2, S5_SCAN_STEPS, st2), g4),
                  pl.BlockSpec((1, 2, lay.n_lat, st2), g4)],
        out_specs=[pl.BlockSpec((1, rows, S5_LANES), lambda g: (g, 0, 0)),
                   pl.BlockSpec((1, 2, rc, st2), g4)],
        out_shape=[jax.ShapeDtypeStruct((SSM_GROUPS, rows, S5_LANES), F32),
                   jax.ShapeDtypeStruct((SSM_GROUPS, 2, rc, st2), F32)],
        scratch_shapes=[pltpu.VMEM((rl, st2), F32), pltpu.VMEM((rl, st2), F32)],
        compiler_params=_cparams(("arbitrary",)),
        name="s5_mixer",
    )(u_g, m_in, m_toep, m_out, coef_a, coef_b, h0)


def _even_out_kernel(*refs, n_ctx_tiles):
    if n_ctx_tiles is None:
        x = refs[0][...]
        refs = refs[1:]
    else:
        x = _pick_rows(refs[0], refs[1], n_ctx_tiles)
        refs = refs[2:]
    ya_ref, yg_ref, u_ref, d_ref, gw_ref, gb_ref, w_ref, mod_ref, o_ref, yt_s = refs
    ys = _group_major_load(yg_ref, yt_s) + d_ref[...] * u_ref[...]
    g = jax.nn.gelu(ys)
    yb = g * jax.nn.sigmoid(jnp.dot(g.astype(BF16), gw_ref[...], preferred_element_type=F32)
                            + gb_ref[...])
    out = jnp.dot(ya_ref[...].astype(BF16), w_ref[:D_LRU, :], preferred_element_type=F32)
    out = out + jnp.dot(yb.astype(BF16), w_ref[D_LRU:, :], preferred_element_type=F32)
    o_ref[...] = x + mod_ref[0, 2:3, :] * out


def even_out(lay, x, y_a, y_g, proj, ssm_d, glu_w_bf16, glu_b, w_out_bf16, mods):
    mi = _mod_index(lay, ROW_TILE)
    c = D_SSM
    row = lambda i: (i, 0)
    const = lambda i: (0, 0)
    if isinstance(x, tuple):
        x_specs, nct = _split_row_specs(lay, ROW_TILE, D_MODEL)
    else:
        x_specs, nct, x = [pl.BlockSpec((ROW_TILE, D_MODEL), row)], None, (x,)
    return pl.pallas_call(
        functools.partial(_even_out_kernel, n_ctx_tiles=nct),
        grid=(lay.t // ROW_TILE,),
        in_specs=x_specs + [
                  pl.BlockSpec((ROW_TILE, c), row),
                  pl.BlockSpec((SSM_GROUPS, ROW_TILE // S5_CHUNK, S5_LANES), lambda i: (0, i, 0)),
                  pl.BlockSpec((ROW_TILE, c), lambda i: (i, 2)),
                  pl.BlockSpec((1, c), const),
                  pl.BlockSpec((c, c), const),
                  pl.BlockSpec((1, c), const),
                  pl.BlockSpec((D_MODEL, D_MODEL), const),
                  pl.BlockSpec((1, N_MOD, D_MODEL), lambda i: (mi(i), 0, 0))],
        out_specs=pl.BlockSpec((ROW_TILE, D_MODEL), row),
        out_shape=jax.ShapeDtypeStruct((lay.t, D_MODEL), F32),
        scratch_shapes=[pltpu.VMEM((SSM_COL_BLOCKS, ROW_TILE, LANES), F32)],
        compiler_params=_cparams(("arbitrary",)),
        name="even_out",
    )(*x, y_a, y_g, proj, ssm_d.reshape(1, -1), glu_w_bf16, glu_b.reshape(1, -1), w_out_bf16, mods)


def _softmax_pv(parts, sink_col):
    m = sink_col
    for s, _ in parts:
        m = jnp.maximum(m, jnp.max(s, axis=-1, keepdims=True))
    den = jnp.exp(sink_col - m)
    acc = None
    for s, v in parts:
        p = jnp.exp(s - m)
        den = den + jnp.sum(p, axis=-1, keepdims=True)
        pv = jnp.dot(p.astype(BF16), v.astype(BF16), preferred_element_type=F32)
        acc = pv if acc is None else acc + pv
    return acc / den


def _nt_dot(a, b):
    return lax.dot_general(a.astype(BF16), b.astype(BF16), (((1,), (1,)), ((), ())),
                           preferred_element_type=F32)


def _attn_ctx_kernel(q_ref, k_ref, v_ref, sink_ref, o_ref):
    n = q_ref.shape[0]
    for kh in range(N_KV):
        k = k_ref[:, kh * HEAD_DIM:(kh + 1) * HEAD_DIM]
        v = v_ref[:, kh * HEAD_DIM:(kh + 1) * HEAD_DIM]
        heads = [kh * GQA + g for g in range(GQA)]
        q = jnp.concatenate([q_ref[:, h * HEAD_DIM:(h + 1) * HEAD_DIM] for h in heads], axis=0)
        sink = jnp.concatenate([jnp.broadcast_to(sink_ref[0:1, h:h + 1], (n, 1)) for h in heads],
                               axis=0)
        o = _softmax_pv([(_nt_dot(q * ATTN_SCALE, k), v)], sink)
        for g, h in enumerate(heads):
            o_ref[:, h * HEAD_DIM:(h + 1) * HEAD_DIM] = o[g * n:(g + 1) * n]


def attn_context(lay, qkv, sink):
    nq = N_HEADS * HEAD_DIM
    nkv = N_KV * HEAD_DIM
    return pl.pallas_call(
        _attn_ctx_kernel,
        grid=(lay.n_ctx,),
        in_specs=[pl.BlockSpec((lay.s_ctx, nq), lambda b: (b, 0)),
                  pl.BlockSpec((lay.s_ctx, nkv), lambda b: (b, nq // nkv)),
                  pl.BlockSpec((lay.s_ctx, nkv), lambda b: (b, nq // nkv + 1)),
                  pl.BlockSpec((1, N_HEADS), lambda b: (0, 0))],
        out_specs=pl.BlockSpec((lay.s_ctx, nq), lambda b: (b, 0)),
        out_shape=jax.ShapeDtypeStruct((lay.t_ctx, nq), F32),
        compiler_params=_cparams(("arbitrary",)),
        name="attn_context",
    )(qkv, qkv, qkv, sink.reshape(1, -1))


def _rope(x, cos, sin):
    lane = lax.broadcasted_iota(I32, (x.shape[0], 2 * HEAD_DIM), 1) % HEAD_DIM
    outs = []
    for j in range(x.shape[1] // (2 * HEAD_DIM)):
        xs = x[:, j * 2 * HEAD_DIM:(j + 1) * 2 * HEAD_DIM]
        sw = jnp.where(lane < HEAD_DIM // 2,
                       pltpu.roll(xs, 2 * HEAD_DIM - HEAD_DIM // 2, 1),
                       pltpu.roll(xs, HEAD_DIM // 2, 1))
        outs.append(xs * cos + sw * sin)
    return outs


def _attn_lat_kernel(q_ref, k0_ref, k1_ref, k2_ref, v0_ref, v1_ref, v2_ref, ck_ref, cv_ref,
                     cq_ref, sq_ref, c0_ref, c1_ref, c2_ref, s0_ref, s1_ref, s2_ref, sink_ref,
                     o_ref, *, n_blk):
    j = pl.program_id(1)
    qb = Q_BLOCK
    q_parts = [qp * ATTN_SCALE for qp in _rope(q_ref[...], cq_ref[...], sq_ref[...])]
    k_parts = [_rope(kr[...], cr[...], sr[...])
               for kr, cr, sr in ((k0_ref, c0_ref, s0_ref), (k1_ref, c1_ref, s1_ref),
                                  (k2_ref, c2_ref, s2_ref))]
    qi = lax.broadcasted_iota(I32, (qb, 3 * qb), 0)
    km = lax.broadcasted_iota(I32, (qb, 3 * qb), 1)
    kpos = j * qb - qb + km
    mask1 = (jnp.abs(km - qb - qi) <= WINDOW) & (kpos >= 0) & (kpos < n_blk * qb)
    mask = jnp.concatenate([mask1] * GQA, axis=0)
    for kh in range(N_KV):
        half = (kh % 2) * HEAD_DIM
        k_loc = jnp.concatenate([kp[kh // 2][:, half:half + HEAD_DIM] for kp in k_parts], axis=0)
        v_loc = jnp.concatenate([vr[:, kh * HEAD_DIM:(kh + 1) * HEAD_DIM]
                                 for vr in (v0_ref, v1_ref, v2_ref)], axis=0)
        qs, sinks = [], []
        for g in range(GQA):
            h = kh * GQA + g
            qs.append(q_parts[h // 2][:, (h % 2) * HEAD_DIM:(h % 2 + 1) * HEAD_DIM])
            sinks.append(jnp.broadcast_to(sink_ref[0:1, h:h + 1], (qb, 1)))
        q = jnp.concatenate(qs, axis=0)
        sink = jnp.concatenate(sinks, axis=0)
        s_loc = jnp.where(mask, _nt_dot(q, k_loc), NEG_INF)
        s_ctx = _nt_dot(q, ck_ref[0, kh])
        o = _softmax_pv([(s_loc, v_loc), (s_ctx, cv_ref[0, kh])], sink)
        for g in range(GQA):
            h = kh * GQA + g
            o_ref[:, h * HEAD_DIM:(h + 1) * HEAD_DIM] = o[g * qb:(g + 1) * qb]


def _rope_tables(s_len):
    rows = s_len // GRID_W
    row = jnp.repeat(jnp.arange(rows), GRID_W).astype(F32)
    col = jnp.tile(jnp.arange(GRID_W), rows).astype(F32)
    nf = HEAD_DIM // 4
    inv = ROPE_BASE ** (-jnp.arange(nf, dtype=F32) / nf)
    ang = jnp.concatenate([row[:, None] * inv, col[:, None] * inv], axis=-1)
    cos, sin = jnp.cos(ang), jnp.sin(ang)
    cos2 = jnp.tile(jnp.concatenate([cos, cos], axis=-1), (1, 2))
    sin2 = jnp.tile(jnp.concatenate([-sin, sin], axis=-1), (1, 2))
    return cos2, sin2


def attn_latent(lay, qkv, cache_k, cache_v, sink):
    nq = N_HEADS * HEAD_DIM
    nkv = N_KV * HEAD_DIM
    n_blk = lay.s_lat // Q_BLOCK
    base = lay.t_ctx // Q_BLOCK
    n_ctx_keys = cache_k.shape[2]
    cos2, sin2 = _rope_tables(lay.s_lat)
    kcol = nq // nkv

    def qrow(b, j):
        return base + b * n_blk + j

    def krow(off):
        return lambda b, j: base + b * n_blk + jnp.clip(j + off, 0, n_blk - 1)

    def trow(off):
        return lambda b, j: (jnp.clip(j + off, 0, n_blk - 1), 0)

    kv_spec = lambda off, col: pl.BlockSpec((Q_BLOCK, nkv), lambda b, j: (krow(off)(b, j), col))
    tab = lambda off: pl.BlockSpec((Q_BLOCK, 2 * HEAD_DIM), trow(off))
    cache_spec = pl.BlockSpec((1, N_KV, n_ctx_keys, HEAD_DIM), lambda b, j: (b, 0, 0, 0))
    return pl.pallas_call(
        functools.partial(_attn_lat_kernel, n_blk=n_blk),
        grid=(lay.n_lat, n_blk),
        in_specs=[pl.BlockSpec((Q_BLOCK, nq), lambda b, j: (qrow(b, j), 0)),
                  kv_spec(-1, kcol), kv_spec(0, kcol), kv_spec(1, kcol),
                  kv_spec(-1, kcol + 1), kv_spec(0, kcol + 1), kv_spec(1, kcol + 1),
                  cache_spec, cache_spec,
                  tab(0), tab(0), tab(-1), tab(0), tab(1), tab(-1), tab(0), tab(1),
                  pl.BlockSpec((1, N_HEADS), lambda b, j: (0, 0))],
        out_specs=pl.BlockSpec((Q_BLOCK, nq), lambda b, j: (b * n_blk + j, 0)),
        out_shape=jax.ShapeDtypeStruct((lay.t_lat, nq), F32),
        compiler_params=_cparams(("arbitrary", "arbitrary")),
        name="attn_latent",
    )(qkv, qkv, qkv, qkv, qkv, qkv, qkv, cache_k, cache_v,
      cos2, sin2, cos2, cos2, cos2, sin2, sin2, sin2, sink.reshape(1, -1))


def _mm_res_kernel(x_ref, ac_ref, al_ref, w_ref, mod_ref, o_ref, *, n_ctx_tiles):
    a = jnp.where(pl.program_id(0) < n_ctx_tiles, ac_ref[...], al_ref[...])
    out = jnp.dot(a.astype(BF16), w_ref[...], preferred_element_type=F32)
    o_ref[...] = x_ref[...] + mod_ref[0, 2:3, :] * out


def matmul_residual(lay, x, a_ctx, a_lat, w_bf16, mods):
    mi = _mod_index(lay, ROW_TILE)
    k = a_ctx.shape[1]
    nct = lay.t_ctx // ROW_TILE
    return pl.pallas_call(
        functools.partial(_mm_res_kernel, n_ctx_tiles=nct),
        grid=(lay.t // ROW_TILE,),
        in_specs=[pl.BlockSpec((ROW_TILE, D_MODEL), lambda i: (i, 0)),
                  pl.BlockSpec((ROW_TILE, k), lambda i: (jnp.minimum(i, nct - 1), 0)),
                  pl.BlockSpec((ROW_TILE, k), lambda i: (jnp.maximum(i - nct, 0), 0)),
                  pl.BlockSpec((k, D_MODEL), lambda i: (0, 0)),
                  pl.BlockSpec((1, N_MOD, D_MODEL), lambda i: (mi(i), 0, 0))],
        out_specs=pl.BlockSpec((ROW_TILE, D_MODEL), lambda i: (i, 0)),
        out_shape=jax.ShapeDtypeStruct((lay.t, D_MODEL), F32),
        compiler_params=_cparams(("arbitrary",)),
        name="matmul_residual",
    )(x, a_ctx, a_lat, w_bf16, mods)


def _rowtile_load(ref, n, base=0):
    parts = []
    for c in range(ROW_CHUNKS):
        words = ref[pl.ds(base + c, n, stride=ROW_CHUNKS), :]
        for half in range(2):
            parts.append(pltpu.unpack_elementwise(words, index=half, packed_dtype=BF16,
                                                  unpacked_dtype=F32))
    return jnp.concatenate(parts, axis=1)


def _rowtile_store(ref, val, n):
    for c in range(ROW_CHUNKS):
        lo = val[:, 2 * c * LANES:(2 * c + 1) * LANES]
        hi = val[:, (2 * c + 1) * LANES:(2 * c + 2) * LANES]
        ref[pl.ds(c, n, stride=ROW_CHUNKS), :] = pltpu.pack_elementwise([lo, hi], packed_dtype=BF16)


def _row_copy(src, src_row, dst, dst_row, sem):
    return pltpu.make_async_copy(
        src.at[pl.ds(pl.multiple_of(src_row * ROW_CHUNKS, ROW_CHUNKS), ROW_CHUNKS), :],
        dst.at[pl.ds(pl.multiple_of(dst_row * ROW_CHUNKS, ROW_CHUNKS), ROW_CHUNKS), :], sem)


def _router_kernel(x_ref, g_ref, mod_ref, rwt_ref, rb_ref, tri_ref, sg_ref, su_ref, sd_ref,
                   hn_ref, sh_ref, eidx_ref, wts_ref, rank_ref, cnt_ref, cnt_s):
    tm = ROUTER_TILE

    @pl.when(pl.program_id(0) == 0)
    def _():
        cnt_s[...] = jnp.zeros_like(cnt_s)

    h = _modnorm(x_ref[...], g_ref[...], mod_ref, 3)
    _rowtile_store(hn_ref, h, tm)
    hb = h.astype(BF16)
    sgate = jnp.dot(hb, sg_ref[...], preferred_element_type=F32)
    sup = jnp.dot(hb, su_ref[...], preferred_element_type=F32)
    sh_ref[...] = jnp.dot((sgate * jax.nn.sigmoid(sgate) * sup).astype(BF16), sd_ref[...],
                          preferred_element_type=F32)
    logits = lax.dot_general(rwt_ref[...], h, (((1,), (1,)), ((), ())),
                             precision=HIGHEST, preferred_element_type=F32)
    scores = jax.nn.sigmoid(logits)
    choice = scores + rb_ref[...]
    gs_rows = []
    for g in range(N_GROUPS):
        cg = choice[g * GROUP_SIZE:(g + 1) * GROUP_SIZE, :]
        m1 = jnp.max(cg, axis=0, keepdims=True)
        eq = cg == m1
        cnt = jnp.sum(eq.astype(F32), axis=0, keepdims=True)
        m2 = jnp.max(jnp.where(eq, -jnp.inf, cg), axis=0, keepdims=True)
        gs_rows.append(m1 + jnp.where(cnt >= 2.0, m1, m2))
    gs = jnp.concatenate(gs_rows, axis=0)
    gi = lax.broadcasted_iota(I32, (N_GROUPS, tm), 0)
    grank = jnp.zeros((N_GROUPS, tm), I32)
    for g in range(N_GROUPS):
        other = gs[g:g + 1, :]
        ahead = (other > gs) | ((other == gs) & (g < gi))
        grank = grank + ahead.astype(I32)
    gsel = grank < TOPK_GROUPS
    emask = jnp.concatenate(
        [jnp.broadcast_to(gsel[g:g + 1, :], (GROUP_SIZE, tm)) for g in range(N_GROUPS)], axis=0)
    masked = jnp.where(emask, choice, -jnp.inf)
    ei = lax.broadcasted_iota(I32, (N_EXPERTS, tm), 0)
    idxs, ws = [], []
    member = jnp.zeros((N_EXPERTS, tm), F32)
    for _ in range(TOP_K):
        m = jnp.max(masked, axis=0, keepdims=True)
        idx = jnp.min(jnp.where(masked == m, ei, N_EXPERTS), axis=0, keepdims=True)
        hit = ei == idx
        ws.append(jnp.sum(jnp.where(hit, scores, 0.0), axis=0, keepdims=True))
        idxs.append(idx)
        member = jnp.where(hit, 1.0, member)
        masked = jnp.where(hit, -jnp.inf, masked)
    w = jnp.concatenate(ws, axis=0)
    wts_ref[...] = w / jnp.sum(w, axis=0, keepdims=True) * ROUTE_SCALE
    eidx_ref[...] = jnp.concatenate(idxs, axis=0)
    before = jnp.dot(member.astype(BF16), tri_ref[...], preferred_element_type=F32) + cnt_s[...]
    ranks = [jnp.sum(jnp.where(ei == idx, before, 0.0), axis=0, keepdims=True) for idx in idxs]
    rank_ref[...] = jnp.concatenate(ranks, axis=0).astype(I32)
    cnt_s[...] = cnt_s[...] + jnp.sum(member, axis=1, keepdims=True)
    cnt_ref[...] = jnp.broadcast_to(cnt_s[...], cnt_ref.shape)


def moe_router(lay, x, g, mods, router_w, router_b, sg_bf16, su_bf16, sd_bf16):
    t = lay.t
    tm = ROUTER_TILE
    mi = _mod_index(lay, tm)
    tri = (jnp.arange(tm)[:, None] < jnp.arange(tm)[None, :]).astype(BF16)
    tok = lambda i: (0, i)
    const = lambda i: (0, 0)
    return pl.pallas_call(
        _router_kernel,
        grid=(t // tm,),
        in_specs=[pl.BlockSpec((tm, D_MODEL), lambda i: (i, 0)),
                  pl.BlockSpec((1, D_MODEL), const),
                  pl.BlockSpec((1, N_MOD, D_MODEL), lambda i: (mi(i), 0, 0)),
                  pl.BlockSpec((N_EXPERTS, D_MODEL), const),
                  pl.BlockSpec((N_EXPERTS, 1), const),
                  pl.BlockSpec((tm, tm), const),
                  pl.BlockSpec((D_MODEL, D_EXPERT), const),
                  pl.BlockSpec((D_MODEL, D_EXPERT), const),
                  pl.BlockSpec((D_EXPERT, D_MODEL), const)],
        out_specs=[pl.BlockSpec((tm * ROW_CHUNKS, LANES), lambda i: (i, 0)),
                   pl.BlockSpec((tm, D_MODEL), lambda i: (i, 0)),
                   pl.BlockSpec((TOP_K, tm), tok),
                   pl.BlockSpec((TOP_K, tm), tok),
                   pl.BlockSpec((TOP_K, tm), tok),
                   pl.BlockSpec((N_EXPERTS, LANES), const)],
        out_shape=[jax.ShapeDtypeStruct((t * ROW_CHUNKS, LANES), PACKED),
                   jax.ShapeDtypeStruct((t, D_MODEL), F32),
                   jax.ShapeDtypeStruct((TOP_K, t), I32),
                   jax.ShapeDtypeStruct((TOP_K, t), F32),
                   jax.ShapeDtypeStruct((TOP_K, t), I32),
                   jax.ShapeDtypeStruct((N_EXPERTS, LANES), F32)],
        scratch_shapes=[pltpu.VMEM((N_EXPERTS, 1), F32)],
        compiler_params=_cparams(("arbitrary",)),
        name="moe_router",
    )(x, g.reshape(1, -1), mods, router_w.T, router_b.reshape(-1, 1), tri, sg_bf16, su_bf16, sd_bf16)


def _dest_kernel(start_ref, eidx_ref, rank_ref, dest_ref):
    e = eidx_ref[...]

    def body(i, acc):
        return jnp.where(e == i, start_ref[i], acc)

    dest_ref[...] = lax.fori_loop(0, N_EXPERTS, body, jnp.zeros_like(e), unroll=8) + rank_ref[...]


def moe_dest(pad_start, eidx, rank):
    t = eidx.shape[1]
    tn = DEST_TILE
    spec = pl.BlockSpec((TOP_K, tn), lambda i, ps: (0, i))
    return pl.pallas_call(
        _dest_kernel,
        grid_spec=pltpu.PrefetchScalarGridSpec(
            num_scalar_prefetch=1, grid=(t // tn,), in_specs=[spec, spec], out_specs=spec),
        out_shape=jax.ShapeDtypeStruct((TOP_K, t), I32),
        compiler_params=_cparams(("arbitrary",)),
        name="moe_dest",
    )(pad_start, eidx, rank)


def _issue_row_copies(idx_at, n, copy_at, unroll=4):
    def body(i, c):
        for p in range(2):
            r = 2 * i + p
            copy_at(r, idx_at(r)).start(priority=p)
        return c
    lax.fori_loop(0, n // 2, body, 0, unroll=unroll)


def _dispatch_kernel(zrow_ref, dest_hbm, hn_ref, xs_hbm, idx_s, zbuf, isem, zsem, ssem, *, n_tiles):
    i = pl.program_id(0)
    slot = i % 2
    td = DISPATCH_TILE

    n_idx = TOP_K * td

    def idx_copy(tile, s):
        return pltpu.make_async_copy(dest_hbm.at[tile], idx_s.at[pl.ds(s * n_idx, n_idx)], isem.at[s])

    def zero_copy(e):
        r0 = pl.multiple_of(zrow_ref[e] * ROW_CHUNKS, ROW_CHUNKS)
        return pltpu.make_async_copy(zbuf, xs_hbm.at[pl.ds(r0, MOE_BLOCK * ROW_CHUNKS), :], zsem)

    @pl.when(i == 0)
    def _():
        zbuf[...] = jnp.zeros_like(zbuf)

        def zstart(e, c):
            @pl.when(zrow_ref[e] >= 0)
            def _():
                zero_copy(e).start()
            return c

        def zwait(e, c):
            @pl.when(zrow_ref[e] >= 0)
            def _():
                zero_copy(e).wait()
            return c

        lax.fori_loop(0, zrow_ref.shape[0], zstart, 0)
        idx_copy(0, 0).start()
        lax.fori_loop(0, zrow_ref.shape[0], zwait, 0)

    idx_copy(i, slot).wait()

    @pl.when(i + 1 < n_tiles)
    def _():
        idx_copy(i + 1, 1 - slot).start()

    for k in range(TOP_K):
        _issue_row_copies(lambda r: idx_s[slot * n_idx + k * td + r], td,
                          lambda r, d: _row_copy(hn_ref, r, xs_hbm, d, ssem))
    for k in range(TOP_K):
        pltpu.make_async_copy(hn_ref, xs_hbm.at[pl.ds(0, td * ROW_CHUNKS), :], ssem).wait()


def _tile_major(dest, tile):
    t = dest.shape[1]
    return dest.reshape(TOP_K, t // tile, tile).transpose(1, 0, 2).reshape(t // tile, TOP_K * tile)


def moe_dispatch(lay, hn, dest, zero_row, n_rows):
    td = DISPATCH_TILE
    n_tiles = lay.t // td
    return pl.pallas_call(
        functools.partial(_dispatch_kernel, n_tiles=n_tiles),
        grid_spec=pltpu.PrefetchScalarGridSpec(
            num_scalar_prefetch=1,
            grid=(n_tiles,),
            in_specs=[pl.BlockSpec(memory_space=pl.ANY),
                      pl.BlockSpec((td * ROW_CHUNKS, LANES), lambda i, z: (i, 0))],
            out_specs=pl.BlockSpec(memory_space=pl.ANY),
            scratch_shapes=[pltpu.SMEM((2 * TOP_K * td,), I32),
                            pltpu.VMEM((MOE_BLOCK * ROW_CHUNKS, LANES), PACKED),
                            pltpu.SemaphoreType.DMA((2,)),
                            pltpu.SemaphoreType.DMA,
                            pltpu.SemaphoreType.DMA]),
        out_shape=jax.ShapeDtypeStruct((n_rows * ROW_CHUNKS, LANES), PACKED),
        compiler_params=_cparams(("arbitrary",)),
        name="moe_dispatch",
    )(zero_row, dest, hn)


def _expert_kernel(blk0_ref, nblk_ref, tail_ref, xs_hbm, wg_ref, wu_ref, wd_ref, y_hbm,
                   xbuf, ybuf, wg_s, wu_s, wd_s, isem, osem):
    e = pl.program_id(0)
    nb = nblk_ref[e]
    g0 = blk0_ref[e]
    total = blk0_ref[N_EXPERTS - 1] + nblk_ref[N_EXPERTS - 1]
    blk_rows = MOE_BLOCK * ROW_CHUNKS
    n_x = xbuf.shape[0]
    n_y = ybuf.shape[0]

    def block_rows(g):
        return pl.ds(pl.multiple_of(g * blk_rows, blk_rows), blk_rows)

    def fetch(g):
        s = g % n_x
        return pltpu.make_async_copy(xs_hbm.at[block_rows(g), :], xbuf.at[s], isem.at[s])

    def writeback(g):
        s = g % n_y
        return pltpu.make_async_copy(ybuf.at[s], y_hbm.at[block_rows(g), :], osem.at[s])

    ahead = n_x // 2

    @pl.when(e == 0)
    def _():
        for p in range(ahead):
            @pl.when(p < total)
            def _():
                fetch(p).start()

    @pl.when(nb > 0)
    def _():
        wg_s[...] = wg_ref[0, 0].astype(BF16)
        wu_s[...] = wu_ref[0, 0].astype(BF16)
        wd_s[...] = wd_ref[0, 0].astype(BF16)

    def run_blocks(g, n):
        for p in range(n):
            @pl.when(g + ahead + p < total)
            def _():
                fetch(g + ahead + p).start()
        for p in range(n):
            fetch(g + p).wait()
        x = jnp.concatenate([_rowtile_load(xbuf.at[(g + p) % n_x], MOE_BLOCK) for p in range(n)],
                            axis=0).astype(BF16)
        gate = jnp.dot(x, wg_s[...], preferred_element_type=F32)
        up = jnp.dot(x, wu_s[...], preferred_element_type=F32)
        act = gate * jax.nn.sigmoid(gate) * up
        y = jnp.dot(act.astype(BF16), wd_s[...], preferred_element_type=F32)
        for p in range(n):
            @pl.when(g + p >= n_y)
            def _():
                writeback(g + p - n_y).wait()
        for p in range(n):
            _rowtile_store(ybuf.at[(g + p) % n_y], y[p * MOE_BLOCK:(p + 1) * MOE_BLOCK], MOE_BLOCK)
            writeback(g + p).start()

    def quad(jj, c):
        run_blocks(g0 + 4 * jj, 4)
        return c

    lax.fori_loop(0, nb // 4, quad, 0)
    rem = nb % 4

    @pl.when(rem >= 2)
    def _():
        run_blocks(g0 + nb - rem, 2)

    @pl.when(rem % 2 == 1)
    def _():
        run_blocks(g0 + nb - 1, 1)

    @pl.when(e == N_EXPERTS - 1)
    def _():
        for p in range(n_y, 0, -1):
            @pl.when(total >= p)
            def _():
                writeback(total - p).wait()

        ybuf[0] = jnp.zeros(ybuf.shape[1:], PACKED)

        def tail_copy(i):
            r0 = pl.multiple_of(tail_ref[i] * ROW_CHUNKS, blk_rows)
            return pltpu.make_async_copy(ybuf.at[0], y_hbm.at[pl.ds(r0, blk_rows), :], osem.at[0])

        def tstart(i, c):
            @pl.when(tail_ref[i] >= 0)
            def _():
                tail_copy(i).start()
            return c

        def twait(i, c):
            @pl.when(tail_ref[i] >= 0)
            def _():
                tail_copy(i).wait()
            return c

        lax.fori_loop(0, tail_ref.shape[0], tstart, 0)
        lax.fori_loop(0, tail_ref.shape[0], twait, 0)


def moe_experts(xs, first_block, n_blocks, tail_row, layer, w_gate, w_up, w_down):
    wspec = lambda shape: pl.BlockSpec((1, 1) + shape, lambda e, a, b, c: (layer, e, 0, 0))
    blk = (MOE_BLOCK * ROW_CHUNKS, LANES)
    return pl.pallas_call(
        _expert_kernel,
        grid_spec=pltpu.PrefetchScalarGridSpec(
            num_scalar_prefetch=3,
            grid=(N_EXPERTS,),
            in_specs=[pl.BlockSpec(memory_space=pl.ANY),
                      wspec((D_MODEL, D_EXPERT)), wspec((D_MODEL, D_EXPERT)),
                      wspec((D_EXPERT, D_MODEL))],
            out_specs=pl.BlockSpec(memory_space=pl.ANY),
            scratch_shapes=[pltpu.VMEM((EXPERT_X_BUFS,) + blk, PACKED),
                            pltpu.VMEM((EXPERT_Y_BUFS,) + blk, PACKED),
                            pltpu.VMEM((D_MODEL, D_EXPERT), BF16),
                            pltpu.VMEM((D_MODEL, D_EXPERT), BF16),
                            pltpu.VMEM((D_EXPERT, D_MODEL), BF16),
                            pltpu.SemaphoreType.DMA((EXPERT_X_BUFS,)),
                            pltpu.SemaphoreType.DMA((EXPERT_Y_BUFS,))]),
        out_shape=jax.ShapeDtypeStruct(xs.shape, PACKED),
        compiler_params=_cparams(("arbitrary",)),
        name="moe_experts",
    )(first_block, n_blocks, tail_row, xs, w_gate, w_up, w_down)


def _combine_kernel(dest_hbm, y_hbm, x_ref, sh_ref, w_ref, mod_ref, *rest, n_tiles, final, proj):
    if final:
        gf_ref, oc_ref, ol_ref, idx_s, ybuf, isem, gsem = rest
    elif proj:
        gn_ref, modn_ref, wn_ref, o_ref, p_ref, idx_s, ybuf, isem, gsem = rest
    else:
        o_ref, idx_s, ybuf, isem, gsem = rest
    i = pl.program_id(0)
    tm = COMBINE_TILE
    n_idx = TOP_K * tm
    last = n_tiles - 1

    def idx_copy(tile, s):
        return pltpu.make_async_copy(dest_hbm.at[tile], idx_s.at[pl.ds(s * n_idx, n_idx)], isem.at[s])

    def gather(s, unroll=4):
        _issue_row_copies(lambda r: idx_s[s * n_idx + r], n_idx,
                          lambda r, d: _row_copy(y_hbm, d, ybuf.at[s], r, gsem.at[s]), unroll)

    def gather_wait(s):
        pltpu.make_async_copy(y_hbm.at[pl.ds(0, n_idx * ROW_CHUNKS), :], ybuf.at[s], gsem.at[s]).wait()

    nslot = COMBINE_TILES_PER_STEP
    ahead = COMBINE_LOOKAHEAD
    assert n_tiles % nslot == 0 and ahead + 1 < nslot

    @pl.when(i == 0)
    def _():
        for t0 in range(ahead):
            c = idx_copy(min(t0, last), t0)
            c.start()
            c.wait()
            gather(t0)
        idx_copy(min(ahead, last), ahead).start()

    for half in range(nslot):
        tile = nslot * i + half
        s = half
        s_far = (half + ahead) % nslot
        rows = slice(half * tm, (half + 1) * tm)
        idx_copy(jnp.minimum(tile + ahead, last), s_far).wait()
        gather_wait(s)
        gather(s_far, unroll=True)
        idx_copy(jnp.minimum(tile + ahead + 1, last), (half + ahead + 1) % nslot).start()

        w = w_ref[rows, :]
        routed = jnp.zeros((tm, D_MODEL), F32)
        for k in range(TOP_K):
            routed = routed + w[:, k:k + 1] * _rowtile_load(ybuf.at[s], tm, base=k * tm * ROW_CHUNKS)
        out = x_ref[rows, :] + mod_ref[0, 5:6, :] * (routed + sh_ref[rows, :])
        if final:
            ms = jnp.mean(out * out, axis=-1, keepdims=True)
            y = out * lax.rsqrt(ms + EPS) * gf_ref[...]
            oc_ref[rows, :] = y
            ol_ref[rows, :] = y
        else:
            o_ref[rows, :] = out
            if proj:
                hn = _modnorm(out, gn_ref[...], modn_ref, 0)
                p_ref[rows, :] = jnp.dot(hn.astype(BF16), wn_ref[...], preferred_element_type=F32)

    @pl.when(i == n_tiles // nslot - 1)
    def _():
        for p in range(ahead):
            gather_wait(p)
        idx_copy(last, ahead).wait()


def moe_combine(lay, x, shared, y_rows, dest, wts_t, mods, g_final=None, next_proj=None):
    tm = COMBINE_TILE
    n_tiles = lay.t // tm
    nslot = COMBINE_TILES_PER_STEP
    step = nslot * tm
    nct = lay.t_ctx // step
    mi = _mod_index(lay, step)
    row = lambda i: (i, 0)
    final = g_final is not None
    if final:
        extra_in = [pl.BlockSpec((1, D_MODEL), lambda i: (0, 0))]
        extra_args = [g_final.reshape(1, -1)]
        out_specs = [pl.BlockSpec((step, D_MODEL), lambda i: (jnp.minimum(i, nct), 0)),
                     pl.BlockSpec((step, D_MODEL), lambda i: (jnp.maximum(i - nct, 0), 0))]
        out_shape = [jax.ShapeDtypeStruct((lay.t_ctx + step, D_MODEL), F32),
                     jax.ShapeDtypeStruct((lay.t_lat, D_MODEL), F32)]
    elif next_proj is not None:
        g_next, mods_next, w_next = next_proj
        n_out = w_next.shape[1]
        extra_in = [pl.BlockSpec((1, D_MODEL), lambda i: (0, 0)),
                    pl.BlockSpec((1, N_MOD, D_MODEL), lambda i: (mi(i), 0, 0)),
                    pl.BlockSpec((D_MODEL, n_out), lambda i: (0, 0))]
        extra_args = [g_next.reshape(1, -1), mods_next, w_next]
        out_specs = [pl.BlockSpec((step, D_MODEL), row), pl.BlockSpec((step, n_out), row)]
        out_shape = [jax.ShapeDtypeStruct((lay.t, D_MODEL), F32),
                     jax.ShapeDtypeStruct((lay.t, n_out), F32)]
    else:
        extra_in, extra_args = [], []
        out_specs = pl.BlockSpec((step, D_MODEL), row)
        out_shape = jax.ShapeDtypeStruct((lay.t, D_MODEL), F32)
    return pl.pallas_call(
        functools.partial(_combine_kernel, n_tiles=n_tiles, final=final,
                          proj=next_proj is not None and not final),
        grid=(n_tiles // nslot,),
        in_specs=[pl.BlockSpec(memory_space=pl.ANY),
                  pl.BlockSpec(memory_space=pl.ANY),
                  pl.BlockSpec((step, D_MODEL), row),
                  pl.BlockSpec((step, D_MODEL), row),
                  pl.BlockSpec((step, TOP_K), row),
                  pl.BlockSpec((1, N_MOD, D_MODEL), lambda i: (mi(i), 0, 0))] + extra_in,
        out_specs=out_specs,
        out_shape=out_shape,
        scratch_shapes=[pltpu.SMEM((nslot * TOP_K * tm,), I32),
                        pltpu.VMEM((nslot, TOP_K * tm * ROW_CHUNKS, LANES), PACKED),
                        pltpu.SemaphoreType.DMA((nslot,)),
                        pltpu.SemaphoreType.DMA((nslot,))],
        compiler_params=_cparams(("arbitrary",)),
        name="moe_combine",
    )(dest, y_rows, x, shared, wts_t, mods, *extra_args)


def moe_layer(lay, x, g, mods, router_w, router_b, layer, w_gate, w_up, w_down, s_gate, s_up, s_down,
              g_final=None, next_proj=None):
    t = lay.t
    hn, shared, eidx, wts, rank, cnt = moe_router(
        lay, x, g, mods, router_w, router_b, s_gate.astype(BF16), s_up.astype(BF16), s_down.astype(BF16))
    counts = cnt[:, 0].astype(I32)
    n_blocks = (counts + MOE_BLOCK - 1) // MOE_BLOCK
    padded = n_blocks * MOE_BLOCK
    pad_end = jnp.cumsum(padded)
    pad_start = pad_end - padded
    n_rows = -(-(t * TOP_K + N_EXPERTS * (MOE_BLOCK - 1)) // MOE_BLOCK) * MOE_BLOCK
    dest = moe_dest(pad_start, eidx, rank)
    last_row = jnp.where(n_blocks > 0, pad_end - MOE_BLOCK, -1)
    tail_blk = pad_end[-1] // MOE_BLOCK + jnp.arange(n_rows // MOE_BLOCK - t * TOP_K // MOE_BLOCK)
    tail_row = jnp.where(tail_blk < n_rows // MOE_BLOCK, tail_blk * MOE_BLOCK, -1).astype(I32)
    xs = moe_dispatch(lay, hn, _tile_major(dest, DISPATCH_TILE),
                      jnp.concatenate([last_row, tail_row]), n_rows)
    y_rows = moe_experts(xs, pad_start // MOE_BLOCK, n_blocks, tail_row, layer, w_gate, w_up, w_down)
    return moe_combine(lay, x, shared, y_rows, _tile_major(dest, COMBINE_TILE), wts.T, mods,
                       g_final, next_proj)


def _block_diag(w):
    nb, bw, _ = w.shape
    eye = jnp.eye(nb, dtype=w.dtype)
    return (eye[:, None, :, None] * w[:, :, None, :]).reshape(nb * bw, nb * bw)


def even_layer(lay, x, mods, g_mix, p, state_lru, state_ssm_re, state_ssm_im):
    t = lay.t
    proj, u_g = modnorm_matmul(lay, x, g_mix, mods, 0, p['w_in'].astype(BF16),
                               ug_col=2 * D_LRU)
    zeros_c = jnp.zeros((lay.n_ctx, D_LRU), F32)
    hf_y, st = None, []
    for d in range(2):
        wg = jnp.concatenate([_block_diag(p['lru_wa'][d]), _block_diag(p['lru_wx'][d])], axis=1)
        bg = jnp.concatenate([p['lru_ba'][d], p['lru_bx'][d]])
        h0 = jnp.concatenate([zeros_c, state_lru[:, d].astype(F32)], axis=0)
        hf_y, s = lru_pass(lay, proj, p['conv_w'], p['conv_b'], wg.astype(BF16), bg,
                           p['lru_lam'][d], h0, reverse=(d == 1), hf=hf_y)
        st.append(s[:lay.n_ctx])
    y_a = hf_y
    new_lru = jnp.stack(st, axis=1)

    mats = _s5_matrices(p['a_re'], p['a_im'], p['log_dt'], p['b_re'], p['b_im'], p['c_re'], p['c_im'])
    h0 = jnp.concatenate([state_ssm_re, state_ssm_im], axis=-1).astype(F32)
    h0 = h0.transpose(2, 1, 0, 3)
    y_g, h_ctx = s5_mixer(lay, u_g, mats, h0)
    seg = lay.s_ctx // S5_CHUNK
    h_ctx = h_ctx.reshape(SSM_GROUPS, 2, lay.n_ctx, seg, 2 * SSM_STATE)
    ends = jnp.stack([h_ctx[:, 0, :, seg - 1], h_ctx[:, 1, :, 0]], axis=1)
    ends = ends.transpose(2, 1, 0, 3)
    x = even_out(lay, x, y_a, y_g, proj, p['d'], p['glu_w'].astype(BF16), p['glu_b'],
                 p['w_out'].astype(BF16), mods)
    return x, new_lru, ends[..., :SSM_STATE], ends[..., SSM_STATE:]


def odd_layer(lay, x, mods, g_mix, w_qkv, sink, w_out, cache_k, cache_v, qkv=None):
    if qkv is None:
        qkv = modnorm_matmul(lay, x, g_mix, mods, 0, w_qkv.astype(BF16))
    o_ctx = attn_context(lay, qkv, sink)
    o_lat = attn_latent(lay, qkv, cache_k, cache_v, sink)
    nq = N_HEADS * HEAD_DIM
    kv = qkv[:lay.t_ctx, nq:].reshape(lay.n_ctx, lay.s_ctx, 2, N_KV, HEAD_DIM)
    k_new = kv[:, :, 0].swapaxes(1, 2)
    v_new = kv[:, :, 1].swapaxes(1, 2)
    x = matmul_residual(lay, x, o_ctx, o_lat, w_out.astype(BF16), mods)
    return x, k_new, v_new


def _forward(lay, x_prompt, x_sample, state_lru, state_ssm_re, state_ssm_im, cache_k, cache_v,
             c, c_ctx, g_mix, g_ffn, w_mod, b_mod,
             ev_w_in, lru_conv_w, lru_conv_b, lru_wa, lru_ba, lru_wx, lru_bx, lru_lam,
             ssm_a_re, ssm_a_im, ssm_log_dt, ssm_b_re, ssm_b_im, ssm_c_re, ssm_c_im, ssm_d,
             ssm_glu_w, ssm_glu_b, ev_w_out, at_w_qkv, at_sink, at_w_out,
             router_w, router_b, exp_w_gate, exp_w_up, exp_w_down, sh_w_gate, sh_w_up, sh_w_down,
             g_final):
    depth = g_mix.shape[0]
    x = (x_prompt.reshape(lay.t_ctx, D_MODEL), x_sample.reshape(lay.t_lat, D_MODEL))
    n_c = 1 + lay.n_lat
    c_rows = jnp.concatenate([c_ctx[None, :], c, jnp.zeros((16 - n_c, D_MODEL), F32)], axis=0)
    new_lru, new_re, new_im, new_k, new_v = [], [], [], [], []
    all_mods = [adaln_table(c_rows, l, w_mod, b_mod[l]) for l in range(depth)]
    qkv = None
    for l in range(depth):
        i = l // 2
        mods = all_mods[l]
        if l % 2 == 0:
            p = dict(w_in=ev_w_in[i], conv_w=lru_conv_w[i], conv_b=lru_conv_b[i],
                     lru_wa=lru_wa[i], lru_ba=lru_ba[i], lru_wx=lru_wx[i], lru_bx=lru_bx[i],
                     lru_lam=lru_lam[i], a_re=ssm_a_re[i], a_im=ssm_a_im[i], log_dt=ssm_log_dt[i],
                     b_re=ssm_b_re[i], b_im=ssm_b_im[i], c_re=ssm_c_re[i], c_im=ssm_c_im[i],
                     d=ssm_d[i], glu_w=ssm_glu_w[i], glu_b=ssm_glu_b[i], w_out=ev_w_out[i])
            x, lru_i, re_i, im_i = even_layer(lay, x, mods, g_mix[l], p, state_lru[:, i],
                                              state_ssm_re[:, i], state_ssm_im[:, i])
            new_lru.append(lru_i)
            new_re.append(re_i)
            new_im.append(im_i)
        else:
            if isinstance(x, tuple):
                x = jnp.concatenate(x, axis=0)
            x, k_i, v_i = odd_layer(lay, x, mods, g_mix[l], at_w_qkv[i], at_sink[i], at_w_out[i],
                                    cache_k[:, i], cache_v[:, i], qkv)
            new_k.append(k_i)
            new_v.append(v_i)
        next_proj = None
        if l + 1 < depth and (l + 1) % 2 == 1:
            next_proj = (g_mix[l + 1], all_mods[l + 1], at_w_qkv[(l + 1) // 2].astype(BF16))
        x = moe_layer(lay, x, g_ffn[l], mods, router_w[l], router_b[l], l, exp_w_gate, exp_w_up,
                      exp_w_down, sh_w_gate[l], sh_w_up[l], sh_w_down[l],
                      g_final=g_final if l == depth - 1 else None, next_proj=next_proj)
        qkv = None
        if next_proj is not None:
            x, qkv = x
    y_ctx, y_lat = x
    y_prompt = y_ctx[:lay.t_ctx].reshape(x_prompt.shape)
    y_sample = y_lat.reshape(x_sample.shape)
    return (y_prompt, y_sample, jnp.stack(new_lru, axis=1), jnp.stack(new_re, axis=1),
            jnp.stack(new_im, axis=1), jnp.stack(new_k, axis=1), jnp.stack(new_v, axis=1))


def kernel(x_prompt, x_sample, state_lru, state_ssm_re, state_ssm_im, cache_k, cache_v, c, c_ctx, g_mix, g_ffn, w_mod, b_mod, ev_w_in, lru_conv_w, lru_conv_b, lru_wa, lru_ba, lru_wx, lru_bx, lru_lam, ssm_a_re, ssm_a_im, ssm_log_dt, ssm_b_re, ssm_b_im, ssm_c_re, ssm_c_im, ssm_d, ssm_glu_w, ssm_glu_b, ev_w_out, at_w_qkv, at_sink, at_w_out, router_w, router_b, exp_w_gate, exp_w_up, exp_w_down, sh_w_gate, sh_w_up, sh_w_down, g_final):
    lay = Layout(n_ctx=x_prompt.shape[0], s_ctx=x_prompt.shape[1],
                 n_lat=x_sample.shape[0], s_lat=x_sample.shape[1])
    return _forward(lay, x_prompt, x_sample, state_lru, state_ssm_re, state_ssm_im, cache_k, cache_v,
                    c, c_ctx, g_mix, g_ffn, w_mod, b_mod,
                    ev_w_in, lru_conv_w, lru_conv_b, lru_wa, lru_ba, lru_wx, lru_bx, lru_lam,
                    ssm_a_re, ssm_a_im, ssm_log_dt, ssm_b_re, ssm_b_im, ssm_c_re, ssm_c_im, ssm_d,
                    ssm_glu_w, ssm_glu_b, ev_w_out, at_w_qkv, at_sink, at_w_out,
                    router_w, router_b, exp_w_gate, exp_w_up, exp_w_down, sh_w_gate, sh_w_up,
                    sh_w_down, g_final)
```

```python
import functools
from typing import NamedTuple

import jax
import jax.numpy as jnp
from jax import lax
from jax.experimental import pallas as pl
from jax.experimental.pallas import tpu as pltpu

F32 = jnp.float32
BF16 = jnp.bfloat16
I32 = jnp.int32
HIGHEST = lax.Precision.HIGHEST

D_MODEL = 1024
EPS = 1e-6
N_MOD = 6
GRID_W = 64
D_LRU = 512
LRU_BLOCKS = 8
LRU_C = 8.0
CONV_W = 4
CONV_LEFT = 2
D_SSM = 512
SSM_GROUP = 16
SSM_GROUPS = 32
SSM_STATE = 64
S5_CHUNK = 16
S5_LANES = S5_CHUNK * SSM_GROUP
S5_SCAN_STEPS = 8
HEAD_DIM = 64
N_HEADS = 16
N_KV = 4
GQA = 4
WINDOW = 128
Q_BLOCK = 128
ROPE_BASE = 10000.0
ATTN_SCALE = HEAD_DIM ** -0.5
NEG_INF = -1e30
N_EXPERTS = 256
TOP_K = 8
N_GROUPS = 8
TOPK_GROUPS = 4
GROUP_SIZE = N_EXPERTS // N_GROUPS
D_EXPERT = 256
ROUTE_SCALE = 2.5
MOE_BLOCK = 128

SUBLANES = 8
LANES = 128
ROW_CHUNKS = D_MODEL // (2 * LANES)
PACKED = jnp.int32
SEQ_TILE = 256
ROW_TILE = 512
ROUTER_TILE = 512
DEST_TILE = 1024
DISPATCH_TILE = 2048
COMBINE_TILE = 128
COMBINE_TILES_PER_STEP = 4
COMBINE_LOOKAHEAD = 2
EXPERT_X_BUFS = 24
EXPERT_Y_BUFS = 8
VMEM_LIMIT = 56 * 1024 * 1024


class Layout(NamedTuple):
    n_ctx: int
    s_ctx: int
    n_lat: int
    s_lat: int

    @property
    def t_ctx(self):
        return self.n_ctx * self.s_ctx

    @property
    def t_lat(self):
        return self.n_lat * self.s_lat

    @property
    def t(self):
        return self.t_ctx + self.t_lat

    @property
    def n_seq(self):
        return self.n_ctx + self.n_lat


def _cparams(sem):
    return pltpu.CompilerParams(dimension_semantics=sem, vmem_limit_bytes=VMEM_LIMIT)


def _mod_index(lay, tile_rows):
    n_ctx_tiles = lay.t_ctx // tile_rows
    per_lat = lay.s_lat // tile_rows

    def f(i):
        return jnp.where(i < n_ctx_tiles, 0, 1 + (i - n_ctx_tiles) // per_lat)
    return f


def _adaln_kernel(c_ref, w_ref, b_ref, o_ref):
    c = c_ref[...]
    s = c * jax.nn.sigmoid(c)
    o_ref[...] = jnp.dot(s, w_ref[0], precision=HIGHEST, preferred_element_type=F32) + b_ref[...]


def adaln_table(c_rows, layer, w_mod, b_mod):
    n = c_rows.shape[0]
    tn = 1536
    out = pl.pallas_call(
        _adaln_kernel,
        grid=(N_MOD * D_MODEL // tn,),
        in_specs=[pl.BlockSpec((n, D_MODEL), lambda j: (0, 0)),
                  pl.BlockSpec((1, D_MODEL, tn), lambda j: (layer, 0, j)),
                  pl.BlockSpec((1, tn), lambda j: (0, j))],
        out_specs=pl.BlockSpec((n, tn), lambda j: (0, j)),
        out_shape=jax.ShapeDtypeStruct((n, N_MOD * D_MODEL), F32),
        compiler_params=_cparams(("arbitrary",)),
        name="adaln",
    )(c_rows, w_mod, b_mod.reshape(1, -1))
    return out.reshape(n, N_MOD, D_MODEL)


def _modnorm(x, g, mod_ref, slot):
    ms = jnp.mean(x * x, axis=-1, keepdims=True)
    y = x * lax.rsqrt(ms + EPS) * g
    shift = mod_ref[0, slot:slot + 1, :]
    scale = mod_ref[0, slot + 1:slot + 2, :]
    return y * (1.0 + scale) + shift


GROUPS_PER_VREG = LANES // SSM_GROUP
SSM_COL_BLOCKS = D_SSM // LANES


def _group_major_store(val, tmp_ref, dst_ref):
    rows = dst_ref.shape[1]
    for j in range(SSM_COL_BLOCKS):
        tmp_ref[j] = val[:, j * LANES:(j + 1) * LANES]
    for j in range(SSM_COL_BLOCKS):
        steps = [tmp_ref[j, pl.ds(i, rows, stride=S5_CHUNK), :] for i in range(S5_CHUNK)]
        for q in range(GROUPS_PER_VREG):
            dst_ref[j * GROUPS_PER_VREG + q] = jnp.concatenate(
                [w[:, q * SSM_GROUP:(q + 1) * SSM_GROUP] for w in steps], axis=1)


def _group_major_load(src_ref, tmp_ref):
    rows = src_ref.shape[1]
    for j in range(SSM_COL_BLOCKS):
        blocks = [src_ref[j * GROUPS_PER_VREG + q] for q in range(GROUPS_PER_VREG)]
        for i in range(S5_CHUNK):
            tmp_ref[j, pl.ds(i, rows, stride=S5_CHUNK), :] = jnp.concatenate(
                [b[:, i * SSM_GROUP:(i + 1) * SSM_GROUP] for b in blocks], axis=1)
    return jnp.concatenate([tmp_ref[j] for j in range(SSM_COL_BLOCKS)], axis=1)


def _split_row_specs(lay, tile_rows, width):
    nct = lay.t_ctx // tile_rows
    return [pl.BlockSpec((tile_rows, width), lambda i: (jnp.minimum(i, nct - 1), 0)),
            pl.BlockSpec((tile_rows, width), lambda i: (jnp.maximum(i - nct, 0), 0))], nct


def _pick_rows(ctx_ref, lat_ref, n_ctx_tiles):
    return jnp.where(pl.program_id(0) < n_ctx_tiles, ctx_ref[...], lat_ref[...])


def _modnorm_mm_kernel(*refs, slot, ug_col, n_ctx_tiles):
    if n_ctx_tiles is None:
        x = refs[0][...]
        refs = refs[1:]
    else:
        x = _pick_rows(refs[0], refs[1], n_ctx_tiles)
        refs = refs[2:]
    g_ref, mod_ref, w_ref, o_ref, *ug_refs = refs
    h = _modnorm(x, g_ref[...], mod_ref, slot)
    out = jnp.dot(h.astype(BF16), w_ref[...], preferred_element_type=F32)
    o_ref[...] = out
    if ug_col is not None:
        ug_ref, tmp_ref = ug_refs
        _group_major_store(out[:, ug_col:ug_col + D_SSM], tmp_ref, ug_ref)


def modnorm_matmul(lay, x, g, mods, slot, w_bf16, ug_col=None):
    t = lay.t
    n = w_bf16.shape[1]
    mi = _mod_index(lay, ROW_TILE)
    out_specs = [pl.BlockSpec((ROW_TILE, n), lambda i: (i, 0))]
    out_shape = [jax.ShapeDtypeStruct((t, n), F32)]
    if ug_col is not None:
        out_specs.append(pl.BlockSpec((SSM_GROUPS, ROW_TILE // S5_CHUNK, S5_LANES), lambda i: (0, i, 0)))
        out_shape.append(jax.ShapeDtypeStruct((SSM_GROUPS, t // S5_CHUNK, S5_LANES), F32))
    if isinstance(x, tuple):
        x_specs, nct = _split_row_specs(lay, ROW_TILE, D_MODEL)
    else:
        x_specs, nct, x = [pl.BlockSpec((ROW_TILE, D_MODEL), lambda i: (i, 0))], None, (x,)
    outs = pl.pallas_call(
        functools.partial(_modnorm_mm_kernel, slot=slot, ug_col=ug_col, n_ctx_tiles=nct),
        grid=(t // ROW_TILE,),
        in_specs=x_specs + [
                  pl.BlockSpec((1, D_MODEL), lambda i: (0, 0)),
                  pl.BlockSpec((1, N_MOD, D_MODEL), lambda i: (mi(i), 0, 0)),
                  pl.BlockSpec((D_MODEL, n), lambda i: (0, 0))],
        out_specs=out_specs,
        out_shape=out_shape,
        scratch_shapes=([pltpu.VMEM((SSM_COL_BLOCKS, ROW_TILE, LANES), F32)]
                        if ug_col is not None else []),
        compiler_params=_cparams(("arbitrary",)),
        name="modnorm_matmul",
    )(*x, g.reshape(1, -1), mods, w_bf16)
    return outs if ug_col is not None else outs[0]


def _seq_tile_maps(lay, reverse):
    assert lay.s_ctx == SEQ_TILE and lay.s_lat % SEQ_TILE == 0
    n_tiles = lay.t // SEQ_TILE
    per_lat = lay.s_lat // SEQ_TILE

    def tile(i):
        return (n_tiles - 1 - i) if reverse else i

    def seq(i):
        ti = tile(i)
        return jnp.where(ti < lay.n_ctx, ti, lay.n_ctx + (ti - lay.n_ctx) // per_lat)

    return n_tiles, tile, seq


def _softplus(x):
    return jnp.maximum(x, 0.0) + jnp.log(1.0 + jnp.exp(-jnp.abs(x)))


def _lru_kernel(rec_ref, prev_ref, next_ref, cw_ref, cb_ref, wg_ref, bg_ref, lam_ref, h0_ref,
                *rest, reverse, n_ctx, per_lat, n_tiles):
    if reverse:
        gate_ref, hf_ref, y_ref, st_ref, a_s, b_s, h_s, carry = rest
    else:
        y_ref, st_ref, a_s, b_s, h_s, carry = rest
    i = pl.program_id(0)
    ti = (n_tiles - 1 - i) if reverse else i
    is_first = jnp.logical_or(ti < n_ctx, (ti - n_ctx) % per_lat == 0)
    is_last = jnp.logical_or(ti < n_ctx, (ti - n_ctx) % per_lat == per_lat - 1)
    ts = SEQ_TILE

    rec = rec_ref[...]
    prev = jnp.where(is_first, 0.0, prev_ref[...])
    nxt = jnp.where(is_last, 0.0, next_ref[...])
    ext = jnp.concatenate([prev, rec, nxt], axis=0)
    n_ext = ts + 2 * SUBLANES
    cw = cw_ref[...]
    xc = cb_ref[...] + cw[2:3, :] * rec
    xc = xc + cw[0:1, :] * pltpu.roll(ext, 2, 0)[SUBLANES:SUBLANES + ts]
    xc = xc + cw[1:2, :] * pltpu.roll(ext, 1, 0)[SUBLANES:SUBLANES + ts]
    xc = xc + cw[3:4, :] * pltpu.roll(ext, n_ext - 1, 0)[SUBLANES:SUBLANES + ts]

    gates = jax.nn.sigmoid(jnp.dot(xc.astype(BF16), wg_ref[...], preferred_element_type=F32)
                           + bg_ref[...])
    r = gates[:, :D_LRU]
    ig = gates[:, D_LRU:]
    log_a = (-LRU_C) * r * _softplus(-lam_ref[...])
    a = jnp.exp(log_a)
    b = jnp.sqrt(1.0 - jnp.exp(2.0 * log_a)) * (ig * xc)

    row8 = lax.broadcasted_iota(I32, (ts, D_LRU), 0) % SUBLANES
    for sh in (1, 2, 4):
        if reverse:
            keep = row8 < SUBLANES - sh
            a_sh = pltpu.roll(a, ts - sh, 0)
            b_sh = pltpu.roll(b, ts - sh, 0)
        else:
            keep = row8 >= sh
            a_sh = pltpu.roll(a, sh, 0)
            b_sh = pltpu.roll(b, sh, 0)
        b = b + a * jnp.where(keep, b_sh, 0.0)
        a = a * jnp.where(keep, a_sh, 1.0)
    a_s[...] = a
    b_s[...] = b

    @pl.when(is_last if reverse else is_first)
    def _():
        carry[...] = h0_ref[0]

    n_grp = ts // SUBLANES

    def body(k, c):
        gi = (n_grp - 1 - k) if reverse else k
        sl = pl.ds(pl.multiple_of(gi * SUBLANES, SUBLANES), SUBLANES)
        h = b_s[sl, :] + a_s[sl, :] * c
        h_s[sl, :] = h
        return h[0:1, :] if reverse else h[SUBLANES - 1:SUBLANES, :]

    c_fin = lax.fori_loop(0, n_grp, body, carry[...], unroll=4)
    carry[...] = c_fin
    st_ref[0] = c_fin
    if reverse:
        y_ref[...] = (hf_ref[...] + h_s[...]) * jax.nn.gelu(gate_ref[...])
    else:
        y_ref[...] = h_s[...]


def lru_pass(lay, proj, conv_w, conv_b, wg_bf16, bg, lam, h0, reverse, hf=None):
    n_tiles, tile, seq = _seq_tile_maps(lay, reverse)
    per_lat = lay.s_lat // SEQ_TILE
    blk8 = SEQ_TILE // SUBLANES
    last8 = lay.t // SUBLANES - 1
    c = D_LRU
    in_specs = [
        pl.BlockSpec((SEQ_TILE, c), lambda i: (tile(i), 1)),
        pl.BlockSpec((SUBLANES, c), lambda i: (jnp.maximum(tile(i) * blk8 - 1, 0), 1)),
        pl.BlockSpec((SUBLANES, c), lambda i: (jnp.minimum(tile(i) * blk8 + blk8, last8), 1)),
        pl.BlockSpec((CONV_W, c), lambda i: (0, 0)),
        pl.BlockSpec((1, c), lambda i: (0, 0)),
        pl.BlockSpec((c, 2 * c), lambda i: (0, 0)),
        pl.BlockSpec((1, 2 * c), lambda i: (0, 0)),
        pl.BlockSpec((1, c), lambda i: (0, 0)),
        pl.BlockSpec((1, 1, c), lambda i: (seq(i), 0, 0)),
    ]
    args = [proj, proj, proj, conv_w, conv_b.reshape(1, -1), wg_bf16, bg.reshape(1, -1),
            lam.reshape(1, -1), h0.reshape(lay.n_seq, 1, c)]
    if reverse:
        in_specs += [pl.BlockSpec((SEQ_TILE, c), lambda i: (tile(i), 0)),
                     pl.BlockSpec((SEQ_TILE, c), lambda i: (tile(i), 0))]
        args += [proj, hf]
    y, st = pl.pallas_call(
        functools.partial(_lru_kernel, reverse=reverse, n_ctx=lay.n_ctx, per_lat=per_lat,
                          n_tiles=n_tiles),
        grid=(n_tiles,),
        in_specs=in_specs,
        out_specs=[pl.BlockSpec((SEQ_TILE, c), lambda i: (tile(i), 0)),
                   pl.BlockSpec((1, 1, c), lambda i: (seq(i), 0, 0))],
        out_shape=[jax.ShapeDtypeStruct((lay.t, c), F32),
                   jax.ShapeDtypeStruct((lay.n_seq, 1, c), F32)],
        scratch_shapes=[pltpu.VMEM((SEQ_TILE, c), F32), pltpu.VMEM((SEQ_TILE, c), F32),
                        pltpu.VMEM((SEQ_TILE, c), F32), pltpu.VMEM((1, c), F32)],
        compiler_params=_cparams(("arbitrary",)),
        name="lru_bwd" if reverse else "lru_fwd",
    )(*args)
    return y, st.reshape(lay.n_seq, c)


def _cmul(a, b):
    return a[0] * b[0] - a[1] * b[1], a[0] * b[1] + a[1] * b[0]


def _s5_matrices(a_re, a_im, log_dt, b_re, b_im, c_re, c_im):
    a_re, a_im = a_re.astype(F32), a_im.astype(F32)
    dt = jnp.exp(log_dt.astype(F32))[..., None]
    z = (a_re * dt, a_im * dt)

    def zpow(k):
        k = k.reshape((-1,) + (1,) * z[0].ndim)
        mag = jnp.exp(k * z[0][None])
        return mag * jnp.cos(k * z[1][None]), mag * jnp.sin(k * z[1][None])

    a_bar = zpow(jnp.ones((1,), F32))
    a_bar = (a_bar[0][0], a_bar[1][0])
    den = a_re * a_re + a_im * a_im
    xr, xi = a_bar[0] - 1.0, a_bar[1]
    q = ((xr * a_re + xi * a_im) / den, (xi * a_re - xr * a_im) / den)
    b_bar = _cmul((q[0][..., None], q[1][..., None]), (b_re.astype(F32), b_im.astype(F32)))
    cc = (c_re.astype(F32), c_im.astype(F32))
    el = S5_CHUNK
    pw = zpow(jnp.arange(el + 1, dtype=F32))
    idx = jnp.arange(el)
    m_in, m_toep, m_out = [], [], []
    for d in range(2):
        p_d = (pw[0][:, d], pw[1][:, d])
        b_d = (b_bar[0][d], b_bar[1][d])
        c_d = (cc[0][d], cc[1][d])
        k_in = (el - 1 - idx) if d == 0 else idx
        w_in = _cmul((p_d[0][k_in][..., None], p_d[1][k_in][..., None]),
                     (b_d[0][None], b_d[1][None]))
        w_in = [jnp.transpose(w, (1, 0, 3, 2)).reshape(SSM_GROUPS, S5_LANES, SSM_STATE) for w in w_in]
        m_in.append(jnp.concatenate(w_in, axis=-1))
        cp = _cmul((c_d[0][None], c_d[1][None]),
                   (p_d[0][:, :, None, :], p_d[1][:, :, None, :]))
        kern = (jnp.einsum('kghp,gpc->kgch', cp[0][:el], b_d[0])
                - jnp.einsum('kghp,gpc->kgch', cp[1][:el], b_d[1]))
        zero = jnp.zeros_like(kern[0])
        rows = []
        for i in range(el):
            if d == 0:
                pieces = [zero] * i + [kern[k] for k in range(el - i)]
            else:
                pieces = [kern[i - j] for j in range(i + 1)] + [zero] * (el - 1 - i)
            rows.append(jnp.concatenate(pieces, axis=-1))
        m_toep.append(jnp.stack(rows, axis=1).reshape(SSM_GROUPS, S5_LANES, S5_LANES))
        k_out = (idx + 1) if d == 0 else (el - idx)
        w_out = [jnp.transpose(w[k_out], (1, 3, 0, 2)).reshape(SSM_GROUPS, SSM_STATE, S5_LANES)
                 for w in cp]
        m_out.append(jnp.concatenate([w_out[0], -w_out[1]], axis=1))
    mul = zpow(el * 2.0 ** jnp.arange(S5_SCAN_STEPS, dtype=F32))
    mul = [jnp.transpose(m, (2, 1, 0, 3)) for m in mul]
    coef_a = jnp.concatenate([mul[0], mul[0]], axis=-1)
    coef_b = jnp.concatenate([-mul[1], mul[1]], axis=-1)
    stack = lambda xs: jnp.stack(xs, axis=1)
    return (stack(m_in).astype(BF16), stack(m_toep).astype(BF16), stack(m_out).astype(BF16),
            coef_a, coef_b)


def _s5_scan(v, ca, cb, seg, reverse):
    n = v.shape[0]
    assert seg <= 2 ** S5_SCAN_STEPS
    row = lax.broadcasted_iota(I32, (n, 2 * SSM_STATE), 0) % seg
    k, sh = 0, 1
    while sh < seg:
        if reverse:
            s = jnp.where(row < seg - sh, pltpu.roll(v, n - sh, 0), 0.0)
        else:
            s = jnp.where(row >= sh, pltpu.roll(v, sh, 0), 0.0)
        v = v + ca[k:k + 1, :] * s + cb[k:k + 1, :] * pltpu.roll(s, SSM_STATE, 1)
        k += 1
        sh *= 2
    return v


def _s5_shift(h, seg, reverse):
    n = h.shape[0]
    row = lax.broadcasted_iota(I32, (n, 2 * SSM_STATE), 0) % seg
    if reverse:
        return jnp.where(row < seg - 1, pltpu.roll(h, n - 1, 0), 0.0)
    return jnp.where(row >= 1, pltpu.roll(h, 1, 0), 0.0)


def _s5_kernel(u_ref, min_ref, mtoep_ref, mout_ref, ca_ref, cb_ref, h0_ref, y_ref, hc_ref,
               v_s, hp_s, *, rc, seg_c, n_lat, seg_l):
    u = u_ref[0].astype(BF16)
    u_c, u_l = u[:rc], u[rc:]
    y_c = jnp.zeros((rc, S5_LANES), F32)
    y_l = jnp.zeros((n_lat * seg_l, S5_LANES), F32)
    for d in range(2):
        reverse = d == 1
        ca = ca_ref[0, d]
        cb = cb_ref[0, d]
        m_in = min_ref[0, d]
        m_toep = mtoep_ref[0, d]
        m_out = mout_ref[0, d]
        h_c = _s5_scan(jnp.dot(u_c, m_in, preferred_element_type=F32), ca, cb, seg_c, reverse)
        hc_ref[0, d] = h_c
        hp_c = _s5_shift(h_c, seg_c, reverse)
        y_c = y_c + jnp.dot(u_c, m_toep, preferred_element_type=F32)
        y_c = y_c + jnp.dot(hp_c.astype(BF16), m_out, preferred_element_type=F32)
        v_s[...] = jnp.dot(u_l, m_in, preferred_element_type=F32)
        for s in range(n_lat):
            h0 = h0_ref[0, d, s:s + 1, :]
            r0 = s * seg_l + (seg_l - 1 if reverse else 0)
            v_s[r0:r0 + 1, :] = (v_s[r0:r0 + 1, :] + ca[0:1, :] * h0
                                 + cb[0:1, :] * pltpu.roll(h0, SSM_STATE, 1))
        h_l = _s5_scan(v_s[...], ca, cb, seg_l, reverse)
        hp_s[...] = _s5_shift(h_l, seg_l, reverse)
        for s in range(n_lat):
            r0 = s * seg_l + (seg_l - 1 if reverse else 0)
            hp_s[r0:r0 + 1, :] = h0_ref[0, d, s:s + 1, :]
        y_l = y_l + jnp.dot(u_l, m_toep, preferred_element_type=F32)
        y_l = y_l + jnp.dot(hp_s[...].astype(BF16), m_out, preferred_element_type=F32)
    y_ref[0, :rc, :] = y_c
    y_ref[0, rc:, :] = y_l


def s5_mixer(lay, u_g, mats, h0):
    m_in, m_toep, m_out, coef_a, coef_b = mats
    rows = lay.t // S5_CHUNK
    rc = lay.t_ctx // S5_CHUNK
    rl = rows - rc
    st2 = 2 * SSM_STATE
    g4 = lambda g: (g, 0, 0, 0)
    return pl.pallas_call(
        functools.partial(_s5_kernel, rc=rc, seg_c=lay.s_ctx // S5_CHUNK, n_lat=lay.n_lat,
                          seg_l=lay.s_lat // S5_CHUNK),
        grid=(SSM_GROUPS,),
        in_specs=[pl.BlockSpec((1, rows, S5_LANES), lambda g: (g, 0, 0)),
                  pl.BlockSpec((1, 2, S5_LANES, st2), g4),
                  pl.BlockSpec((1, 2, S5_LANES, S5_LANES), g4),
                  pl.BlockSpec((1, 2, st2, S5_LANES), g4),
                  pl.BlockSpec((1, 2, S5_SCAN_STEPS, st2), g4),
                  pl.BlockSpec((1, 2, S5_SCAN_STEPS, st2), g4),
                  pl.BlockSpec((1, 2, lay.n_lat, st2), g4)],
        out_specs=[pl.BlockSpec((1, rows, S5_LANES), lambda g: (g, 0, 0)),
                   pl.BlockSpec((1, 2, rc, st2), g4)],
        out_shape=[jax.ShapeDtypeStruct((SSM_GROUPS, rows, S5_LANES), F32),
                   jax.ShapeDtypeStruct((SSM_GROUPS, 2, rc, st2), F32)],
        scratch_shapes=[pltpu.VMEM((rl, st2), F32), pltpu.VMEM((rl, st2), F32)],
        compiler_params=_cparams(("arbitrary",)),
        name="s5_mixer",
    )(u_g, m_in, m_toep, m_out, coef_a, coef_b, h0)


def _even_out_kernel(*refs, n_ctx_tiles):
    if n_ctx_tiles is None:
        x = refs[0][...]
        refs = refs[1:]
    else:
        x = _pick_rows(refs[0], refs[1], n_ctx_tiles)
        refs = refs[2:]
    ya_ref, yg_ref, u_ref, d_ref, gw_ref, gb_ref, w_ref, mod_ref, o_ref, yt_s = refs
    ys = _group_major_load(yg_ref, yt_s) + d_ref[...] * u_ref[...]
    g = jax.nn.gelu(ys)
    yb = g * jax.nn.sigmoid(jnp.dot(g.astype(BF16), gw_ref[...], preferred_element_type=F32)
                            + gb_ref[...])
    out = jnp.dot(ya_ref[...].astype(BF16), w_ref[:D_LRU, :], preferred_element_type=F32)
    out = out + jnp.dot(yb.astype(BF16), w_ref[D_LRU:, :], preferred_element_type=F32)
    o_ref[...] = x + mod_ref[0, 2:3, :] * out


def even_out(lay, x, y_a, y_g, proj, ssm_d, glu_w_bf16, glu_b, w_out_bf16, mods):
    mi = _mod_index(lay, ROW_TILE)
    c = D_SSM
    row = lambda i: (i, 0)
    const = lambda i: (0, 0)
    if isinstance(x, tuple):
        x_specs, nct = _split_row_specs(lay, ROW_TILE, D_MODEL)
    else:
        x_specs, nct, x = [pl.BlockSpec((ROW_TILE, D_MODEL), row)], None, (x,)
    return pl.pallas_call(
        functools.partial(_even_out_kernel, n_ctx_tiles=nct),
        grid=(lay.t // ROW_TILE,),
        in_specs=x_specs + [
                  pl.BlockSpec((ROW_TILE, c), row),
                  pl.BlockSpec((SSM_GROUPS, ROW_TILE // S5_CHUNK, S5_LANES), lambda i: (0, i, 0)),
                  pl.BlockSpec((ROW_TILE, c), lambda i: (i, 2)),
                  pl.BlockSpec((1, c), const),
                  pl.BlockSpec((c, c), const),
                  pl.BlockSpec((1, c), const),
                  pl.BlockSpec((D_MODEL, D_MODEL), const),
                  pl.BlockSpec((1, N_MOD, D_MODEL), lambda i: (mi(i), 0, 0))],
        out_specs=pl.BlockSpec((ROW_TILE, D_MODEL), row),
        out_shape=jax.ShapeDtypeStruct((lay.t, D_MODEL), F32),
        scratch_shapes=[pltpu.VMEM((SSM_COL_BLOCKS, ROW_TILE, LANES), F32)],
        compiler_params=_cparams(("arbitrary",)),
        name="even_out",
    )(*x, y_a, y_g, proj, ssm_d.reshape(1, -1), glu_w_bf16, glu_b.reshape(1, -1), w_out_bf16, mods)


def _softmax_pv(parts, sink_col):
    m = sink_col
    for s, _ in parts:
        m = jnp.maximum(m, jnp.max(s, axis=-1, keepdims=True))
    den = jnp.exp(sink_col - m)
    acc = None
    for s, v in parts:
        p = jnp.exp(s - m)
        den = den + jnp.sum(p, axis=-1, keepdims=True)
        pv = jnp.dot(p.astype(BF16), v.astype(BF16), preferred_element_type=F32)
        acc = pv if acc is None else acc + pv
    return acc / den


def _nt_dot(a, b):
    return lax.dot_general(a.astype(BF16), b.astype(BF16), (((1,), (1,)), ((), ())),
                           preferred_element_type=F32)


def _attn_ctx_kernel(q_ref, k_ref, v_ref, sink_ref, o_ref):
    n = q_ref.shape[0]
    for kh in range(N_KV):
        k = k_ref[:, kh * HEAD_DIM:(kh + 1) * HEAD_DIM]
        v = v_ref[:, kh * HEAD_DIM:(kh + 1) * HEAD_DIM]
        heads = [kh * GQA + g for g in range(GQA)]
        q = jnp.concatenate([q_ref[:, h * HEAD_DIM:(h + 1) * HEAD_DIM] for h in heads], axis=0)
        sink = jnp.concatenate([jnp.broadcast_to(sink_ref[0:1, h:h + 1], (n, 1)) for h in heads],
                               axis=0)
        o = _softmax_pv([(_nt_dot(q * ATTN_SCALE, k), v)], sink)
        for g, h in enumerate(heads):
            o_ref[:, h * HEAD_DIM:(h + 1) * HEAD_DIM] = o[g * n:(g + 1) * n]


def attn_context(lay, qkv, sink):
    nq = N_HEADS * HEAD_DIM
    nkv = N_KV * HEAD_DIM
    return pl.pallas_call(
        _attn_ctx_kernel,
        grid=(lay.n_ctx,),
        in_specs=[pl.BlockSpec((lay.s_ctx, nq), lambda b: (b, 0)),
                  pl.BlockSpec((lay.s_ctx, nkv), lambda b: (b, nq // nkv)),
                  pl.BlockSpec((lay.s_ctx, nkv), lambda b: (b, nq // nkv + 1)),
                  pl.BlockSpec((1, N_HEADS), lambda b: (0, 0))],
        out_specs=pl.BlockSpec((lay.s_ctx, nq), lambda b: (b, 0)),
        out_shape=jax.ShapeDtypeStruct((lay.t_ctx, nq), F32),
        compiler_params=_cparams(("arbitrary",)),
        name="attn_context",
    )(qkv, qkv, qkv, sink.reshape(1, -1))


def _rope(x, cos, sin):
    lane = lax.broadcasted_iota(I32, (x.shape[0], 2 * HEAD_DIM), 1) % HEAD_DIM
    outs = []
    for j in range(x.shape[1] // (2 * HEAD_DIM)):
        xs = x[:, j * 2 * HEAD_DIM:(j + 1) * 2 * HEAD_DIM]
        sw = jnp.where(lane < HEAD_DIM // 2,
                       pltpu.roll(xs, 2 * HEAD_DIM - HEAD_DIM // 2, 1),
                       pltpu.roll(xs, HEAD_DIM // 2, 1))
        outs.append(xs * cos + sw * sin)
    return outs


def _attn_lat_kernel(q_ref, k0_ref, k1_ref, k2_ref, v0_ref, v1_ref, v2_ref, ck_ref, cv_ref,
                     cq_ref, sq_ref, c0_ref, c1_ref, c2_ref, s0_ref, s1_ref, s2_ref, sink_ref,
                     o_ref, *, n_blk):
    j = pl.program_id(1)
    qb = Q_BLOCK
    q_parts = [qp * ATTN_SCALE for qp in _rope(q_ref[...], cq_ref[...], sq_ref[...])]
    k_parts = [_rope(kr[...], cr[...], sr[...])
               for kr, cr, sr in ((k0_ref, c0_ref, s0_ref), (k1_ref, c1_ref, s1_ref),
                                  (k2_ref, c2_ref, s2_ref))]
    qi = lax.broadcasted_iota(I32, (qb, 3 * qb), 0)
    km = lax.broadcasted_iota(I32, (qb, 3 * qb), 1)
    kpos = j * qb - qb + km
    mask1 = (jnp.abs(km - qb - qi) <= WINDOW) & (kpos >= 0) & (kpos < n_blk * qb)
    mask = jnp.concatenate([mask1] * GQA, axis=0)
    for kh in range(N_KV):
        half = (kh % 2) * HEAD_DIM
        k_loc = jnp.concatenate([kp[kh // 2][:, half:half + HEAD_DIM] for kp in k_parts], axis=0)
        v_loc = jnp.concatenate([vr[:, kh * HEAD_DIM:(kh + 1) * HEAD_DIM]
                                 for vr in (v0_ref, v1_ref, v2_ref)], axis=0)
        qs, sinks = [], []
        for g in range(GQA):
            h = kh * GQA + g
            qs.append(q_parts[h // 2][:, (h % 2) * HEAD_DIM:(h % 2 + 1) * HEAD_DIM])
            sinks.append(jnp.broadcast_to(sink_ref[0:1, h:h + 1], (qb, 1)))
        q = jnp.concatenate(qs, axis=0)
        sink = jnp.concatenate(sinks, axis=0)
        s_loc = jnp.where(mask, _nt_dot(q, k_loc), NEG_INF)
        s_ctx = _nt_dot(q, ck_ref[0, kh])
        o = _softmax_pv([(s_loc, v_loc), (s_ctx, cv_ref[0, kh])], sink)
        for g in range(GQA):
            h = kh * GQA + g
            o_ref[:, h * HEAD_DIM:(h + 1) * HEAD_DIM] = o[g * qb:(g + 1) * qb]


def _rope_tables(s_len):
    rows = s_len // GRID_W
    row = jnp.repeat(jnp.arange(rows), GRID_W).astype(F32)
    col = jnp.tile(jnp.arange(GRID_W), rows).astype(F32)
    nf = HEAD_DIM // 4
    inv = ROPE_BASE ** (-jnp.arange(nf, dtype=F32) / nf)
    ang = jnp.concatenate([row[:, None] * inv, col[:, None] * inv], axis=-1)
    cos, sin = jnp.cos(ang), jnp.sin(ang)
    cos2 = jnp.tile(jnp.concatenate([cos, cos], axis=-1), (1, 2))
    sin2 = jnp.tile(jnp.concatenate([-sin, sin], axis=-1), (1, 2))
    return cos2, sin2


def attn_latent(lay, qkv, cache_k, cache_v, sink):
    nq = N_HEADS * HEAD_DIM
    nkv = N_KV * HEAD_DIM
    n_blk = lay.s_lat // Q_BLOCK
    base = lay.t_ctx // Q_BLOCK
    n_ctx_keys = cache_k.shape[2]
    cos2, sin2 = _rope_tables(lay.s_lat)
    kcol = nq // nkv

    def qrow(b, j):
        return base + b * n_blk + j

    def krow(off):
        return lambda b, j: base + b * n_blk + jnp.clip(j + off, 0, n_blk - 1)

    def trow(off):
        return lambda b, j: (jnp.clip(j + off, 0, n_blk - 1), 0)

    kv_spec = lambda off, col: pl.BlockSpec((Q_BLOCK, nkv), lambda b, j: (krow(off)(b, j), col))
    tab = lambda off: pl.BlockSpec((Q_BLOCK, 2 * HEAD_DIM), trow(off))
    cache_spec = pl.BlockSpec((1, N_KV, n_ctx_keys, HEAD_DIM), lambda b, j: (b, 0, 0, 0))
    return pl.pallas_call(
        functools.partial(_attn_lat_kernel, n_blk=n_blk),
        grid=(lay.n_lat, n_blk),
        in_specs=[pl.BlockSpec((Q_BLOCK, nq), lambda b, j: (qrow(b, j), 0)),
                  kv_spec(-1, kcol), kv_spec(0, kcol), kv_spec(1, kcol),
                  kv_spec(-1, kcol + 1), kv_spec(0, kcol + 1), kv_spec(1, kcol + 1),
                  cache_spec, cache_spec,
                  tab(0), tab(0), tab(-1), tab(0), tab(1), tab(-1), tab(0), tab(1),
                  pl.BlockSpec((1, N_HEADS), lambda b, j: (0, 0))],
        out_specs=pl.BlockSpec((Q_BLOCK, nq), lambda b, j: (b * n_blk + j, 0)),
        out_shape=jax.ShapeDtypeStruct((lay.t_lat, nq), F32),
        compiler_params=_cparams(("arbitrary", "arbitrary")),
        name="attn_latent",
    )(qkv, qkv, qkv, qkv, qkv, qkv, qkv, cache_k, cache_v,
      cos2, sin2, cos2, cos2, cos2, sin2, sin2, sin2, sink.reshape(1, -1))


def _mm_res_kernel(x_ref, ac_ref, al_ref, w_ref, mod_ref, o_ref, *, n_ctx_tiles):
    a = jnp.where(pl.program_id(0) < n_ctx_tiles, ac_ref[...], al_ref[...])
    out = jnp.dot(a.astype(BF16), w_ref[...], preferred_element_type=F32)
    o_ref[...] = x_ref[...] + mod_ref[0, 2:3, :] * out


def matmul_residual(lay, x, a_ctx, a_lat, w_bf16, mods):
    mi = _mod_index(lay, ROW_TILE)
    k = a_ctx.shape[1]
    nct = lay.t_ctx // ROW_TILE
    return pl.pallas_call(
        functools.partial(_mm_res_kernel, n_ctx_tiles=nct),
        grid=(lay.t // ROW_TILE,),
        in_specs=[pl.BlockSpec((ROW_TILE, D_MODEL), lambda i: (i, 0)),
                  pl.BlockSpec((ROW_TILE, k), lambda i: (jnp.minimum(i, nct - 1), 0)),
                  pl.BlockSpec((ROW_TILE, k), lambda i: (jnp.maximum(i - nct, 0), 0)),
                  pl.BlockSpec((k, D_MODEL), lambda i: (0, 0)),
                  pl.BlockSpec((1, N_MOD, D_MODEL), lambda i: (mi(i), 0, 0))],
        out_specs=pl.BlockSpec((ROW_TILE, D_MODEL), lambda i: (i, 0)),
        out_shape=jax.ShapeDtypeStruct((lay.t, D_MODEL), F32),
        compiler_params=_cparams(("arbitrary",)),
        name="matmul_residual",
    )(x, a_ctx, a_lat, w_bf16, mods)


def _rowtile_load(ref, n, base=0):
    parts = []
    for c in range(ROW_CHUNKS):
        words = ref[pl.ds(base + c, n, stride=ROW_CHUNKS), :]
        for half in range(2):
            parts.append(pltpu.unpack_elementwise(words, index=half, packed_dtype=BF16,
                                                  unpacked_dtype=F32))
    return jnp.concatenate(parts, axis=1)


def _rowtile_store(ref, val, n):
    for c in range(ROW_CHUNKS):
        lo = val[:, 2 * c * LANES:(2 * c + 1) * LANES]
        hi = val[:, (2 * c + 1) * LANES:(2 * c + 2) * LANES]
        ref[pl.ds(c, n, stride=ROW_CHUNKS), :] = pltpu.pack_elementwise([lo, hi], packed_dtype=BF16)


def _row_copy(src, src_row, dst, dst_row, sem):
    return pltpu.make_async_copy(
        src.at[pl.ds(pl.multiple_of(src_row * ROW_CHUNKS, ROW_CHUNKS), ROW_CHUNKS), :],
        dst.at[pl.ds(pl.multiple_of(dst_row * ROW_CHUNKS, ROW_CHUNKS), ROW_CHUNKS), :], sem)


def _router_kernel(x_ref, g_ref, mod_ref, rwt_ref, rb_ref, tri_ref, sg_ref, su_ref, sd_ref,
                   hn_ref, sh_ref, eidx_ref, wts_ref, rank_ref, cnt_ref, cnt_s):
    tm = ROUTER_TILE

    @pl.when(pl.program_id(0) == 0)
    def _():
        cnt_s[...] = jnp.zeros_like(cnt_s)

    h = _modnorm(x_ref[...], g_ref[...], mod_ref, 3)
    _rowtile_store(hn_ref, h, tm)
    hb = h.astype(BF16)
    sgate = jnp.dot(hb, sg_ref[...], preferred_element_type=F32)
    sup = jnp.dot(hb, su_ref[...], preferred_element_type=F32)
    sh_ref[...] = jnp.dot((sgate * jax.nn.sigmoid(sgate) * sup).astype(BF16), sd_ref[...],
                          preferred_element_type=F32)
    logits = lax.dot_general(rwt_ref[...], h, (((1,), (1,)), ((), ())),
                             precision=HIGHEST, preferred_element_type=F32)
    scores = jax.nn.sigmoid(logits)
    choice = scores + rb_ref[...]
    gs_rows = []
    for g in range(N_GROUPS):
        cg = choice[g * GROUP_SIZE:(g + 1) * GROUP_SIZE, :]
        m1 = jnp.max(cg, axis=0, keepdims=True)
        eq = cg == m1
        cnt = jnp.sum(eq.astype(F32), axis=0, keepdims=True)
        m2 = jnp.max(jnp.where(eq, -jnp.inf, cg), axis=0, keepdims=True)
        gs_rows.append(m1 + jnp.where(cnt >= 2.0, m1, m2))
    gs = jnp.concatenate(gs_rows, axis=0)
    gi = lax.broadcasted_iota(I32, (N_GROUPS, tm), 0)
    grank = jnp.zeros((N_GROUPS, tm), I32)
    for g in range(N_GROUPS):
        other = gs[g:g + 1, :]
        ahead = (other > gs) | ((other == gs) & (g < gi))
        grank = grank + ahead.astype(I32)
    gsel = grank < TOPK_GROUPS
    emask = jnp.concatenate(
        [jnp.broadcast_to(gsel[g:g + 1, :], (GROUP_SIZE, tm)) for g in range(N_GROUPS)], axis=0)
    masked = jnp.where(emask, choice, -jnp.inf)
    ei = lax.broadcasted_iota(I32, (N_EXPERTS, tm), 0)
    idxs, ws = [], []
    member = jnp.zeros((N_EXPERTS, tm), F32)
    for _ in range(TOP_K):
        m = jnp.max(masked, axis=0, keepdims=True)
        idx = jnp.min(jnp.where(masked == m, ei, N_EXPERTS), axis=0, keepdims=True)
        hit = ei == idx
        ws.append(jnp.sum(jnp.where(hit, scores, 0.0), axis=0, keepdims=True))
        idxs.append(idx)
        member = jnp.where(hit, 1.0, member)
        masked = jnp.where(hit, -jnp.inf, masked)
    w = jnp.concatenate(ws, axis=0)
    wts_ref[...] = w / jnp.sum(w, axis=0, keepdims=True) * ROUTE_SCALE
    eidx_ref[...] = jnp.concatenate(idxs, axis=0)
    before = jnp.dot(member.astype(BF16), tri_ref[...], preferred_element_type=F32) + cnt_s[...]
    ranks = [jnp.sum(jnp.where(ei == idx, before, 0.0), axis=0, keepdims=True) for idx in idxs]
    rank_ref[...] = jnp.concatenate(ranks, axis=0).astype(I32)
    cnt_s[...] = cnt_s[...] + jnp.sum(member, axis=1, keepdims=True)
    cnt_ref[...] = jnp.broadcast_to(cnt_s[...], cnt_ref.shape)


def moe_router(lay, x, g, mods, router_w, router_b, sg_bf16, su_bf16, sd_bf16):
    t = lay.t
    tm = ROUTER_TILE
    mi = _mod_index(lay, tm)
    tri = (jnp.arange(tm)[:, None] < jnp.arange(tm)[None, :]).astype(BF16)
    tok = lambda i: (0, i)
    const = lambda i: (0, 0)
    return pl.pallas_call(
        _router_kernel,
        grid=(t // tm,),
        in_specs=[pl.BlockSpec((tm, D_MODEL), lambda i: (i, 0)),
                  pl.BlockSpec((1, D_MODEL), const),
                  pl.BlockSpec((1, N_MOD, D_MODEL), lambda i: (mi(i), 0, 0)),
                  pl.BlockSpec((N_EXPERTS, D_MODEL), const),
                  pl.BlockSpec((N_EXPERTS, 1), const),
                  pl.BlockSpec((tm, tm), const),
                  pl.BlockSpec((D_MODEL, D_EXPERT), const),
                  pl.BlockSpec((D_MODEL, D_EXPERT), const),
                  pl.BlockSpec((D_EXPERT, D_MODEL), const)],
        out_specs=[pl.BlockSpec((tm * ROW_CHUNKS, LANES), lambda i: (i, 0)),
                   pl.BlockSpec((tm, D_MODEL), lambda i: (i, 0)),
                   pl.BlockSpec((TOP_K, tm), tok),
                   pl.BlockSpec((TOP_K, tm), tok),
                   pl.BlockSpec((TOP_K, tm), tok),
                   pl.BlockSpec((N_EXPERTS, LANES), const)],
        out_shape=[jax.ShapeDtypeStruct((t * ROW_CHUNKS, LANES), PACKED),
                   jax.ShapeDtypeStruct((t, D_MODEL), F32),
                   jax.ShapeDtypeStruct((TOP_K, t), I32),
                   jax.ShapeDtypeStruct((TOP_K, t), F32),
                   jax.ShapeDtypeStruct((TOP_K, t), I32),
                   jax.ShapeDtypeStruct((N_EXPERTS, LANES), F32)],
        scratch_shapes=[pltpu.VMEM((N_EXPERTS, 1), F32)],
        compiler_params=_cparams(("arbitrary",)),
        name="moe_router",
    )(x, g.reshape(1, -1), mods, router_w.T, router_b.reshape(-1, 1), tri, sg_bf16, su_bf16, sd_bf16)


def _dest_kernel(start_ref, eidx_ref, rank_ref, dest_ref):
    e = eidx_ref[...]

    def body(i, acc):
        return jnp.where(e == i, start_ref[i], acc)

    dest_ref[...] = lax.fori_loop(0, N_EXPERTS, body, jnp.zeros_like(e), unroll=8) + rank_ref[...]


def moe_dest(pad_start, eidx, rank):
    t = eidx.shape[1]
    tn = DEST_TILE
    spec = pl.BlockSpec((TOP_K, tn), lambda i, ps: (0, i))
    return pl.pallas_call(
        _dest_kernel,
        grid_spec=pltpu.PrefetchScalarGridSpec(
            num_scalar_prefetch=1, grid=(t // tn,), in_specs=[spec, spec], out_specs=spec),
        out_shape=jax.ShapeDtypeStruct((TOP_K, t), I32),
        compiler_params=_cparams(("arbitrary",)),
        name="moe_dest",
    )(pad_start, eidx, rank)


def _issue_row_copies(idx_at, n, copy_at, unroll=4):
    def body(i, c):
        for p in range(2):
            r = 2 * i + p
            copy_at(r, idx_at(r)).start(priority=p)
        return c
    lax.fori_loop(0, n // 2, body, 0, unroll=unroll)


def _dispatch_kernel(zrow_ref, dest_hbm, hn_ref, xs_hbm, idx_s, zbuf, isem, zsem, ssem, *, n_tiles):
    i = pl.program_id(0)
    slot = i % 2
    td = DISPATCH_TILE

    n_idx = TOP_K * td

    def idx_copy(tile, s):
        return pltpu.make_async_copy(dest_hbm.at[tile], idx_s.at[pl.ds(s * n_idx, n_idx)], isem.at[s])

    def zero_copy(e):
        r0 = pl.multiple_of(zrow_ref[e] * ROW_CHUNKS, ROW_CHUNKS)
        return pltpu.make_async_copy(zbuf, xs_hbm.at[pl.ds(r0, MOE_BLOCK * ROW_CHUNKS), :], zsem)

    @pl.when(i == 0)
    def _():
        zbuf[...] = jnp.zeros_like(zbuf)

        def zstart(e, c):
            @pl.when(zrow_ref[e] >= 0)
            def _():
                zero_copy(e).start()
            return c

        def zwait(e, c):
            @pl.when(zrow_ref[e] >= 0)
            def _():
                zero_copy(e).wait()
            return c

        lax.fori_loop(0, zrow_ref.shape[0], zstart, 0)
        idx_copy(0, 0).start()
        lax.fori_loop(0, zrow_ref.shape[0], zwait, 0)

    idx_copy(i, slot).wait()

    @pl.when(i + 1 < n_tiles)
    def _():
        idx_copy(i + 1, 1 - slot).start()

    for k in range(TOP_K):
        _issue_row_copies(lambda r: idx_s[slot * n_idx + k * td + r], td,
                          lambda r, d: _row_copy(hn_ref, r, xs_hbm, d, ssem))
    for k in range(TOP_K):
        pltpu.make_async_copy(hn_ref, xs_hbm.at[pl.ds(0, td * ROW_CHUNKS), :], ssem).wait()


def _tile_major(dest, tile):
    t = dest.shape[1]
    return dest.reshape(TOP_K, t // tile, tile).transpose(1, 0, 2).reshape(t // tile, TOP_K * tile)


def moe_dispatch(lay, hn, dest, zero_row, n_rows):
    td = DISPATCH_TILE
    n_tiles = lay.t // td
    return pl.pallas_call(
        functools.partial(_dispatch_kernel, n_tiles=n_tiles),
        grid_spec=pltpu.PrefetchScalarGridSpec(
            num_scalar_prefetch=1,
            grid=(n_tiles,),
            in_specs=[pl.BlockSpec(memory_space=pl.ANY),
                      pl.BlockSpec((td * ROW_CHUNKS, LANES), lambda i, z: (i, 0))],
            out_specs=pl.BlockSpec(memory_space=pl.ANY),
            scratch_shapes=[pltpu.SMEM((2 * TOP_K * td,), I32),
                            pltpu.VMEM((MOE_BLOCK * ROW_CHUNKS, LANES), PACKED),
                            pltpu.SemaphoreType.DMA((2,)),
                            pltpu.SemaphoreType.DMA,
                            pltpu.SemaphoreType.DMA]),
        out_shape=jax.ShapeDtypeStruct((n_rows * ROW_CHUNKS, LANES), PACKED),
        compiler_params=_cparams(("arbitrary",)),
        name="moe_dispatch",
    )(zero_row, dest, hn)


def _expert_kernel(blk0_ref, nblk_ref, tail_ref, xs_hbm, wg_ref, wu_ref, wd_ref, y_hbm,
                   xbuf, ybuf, wg_s, wu_s, wd_s, isem, osem):
    e = pl.program_id(0)
    nb = nblk_ref[e]
    g0 = blk0_ref[e]
    total = blk0_ref[N_EXPERTS - 1] + nblk_ref[N_EXPERTS - 1]
    blk_rows = MOE_BLOCK * ROW_CHUNKS
    n_x = xbuf.shape[0]
    n_y = ybuf.shape[0]

    def block_rows(g):
        return pl.ds(pl.multiple_of(g * blk_rows, blk_rows), blk_rows)

    def fetch(g):
        s = g % n_x
        return pltpu.make_async_copy(xs_hbm.at[block_rows(g), :], xbuf.at[s], isem.at[s])

    def writeback(g):
        s = g % n_y
        return pltpu.make_async_copy(ybuf.at[s], y_hbm.at[block_rows(g), :], osem.at[s])

    ahead = n_x // 2

    @pl.when(e == 0)
    def _():
        for p in range(ahead):
            @pl.when(p < total)
            def _():
                fetch(p).start()

    @pl.when(nb > 0)
    def _():
        wg_s[...] = wg_ref[0, 0].astype(BF16)
        wu_s[...] = wu_ref[0, 0].astype(BF16)
        wd_s[...] = wd_ref[0, 0].astype(BF16)

    def run_blocks(g, n):
        for p in range(n):
            @pl.when(g + ahead + p < total)
            def _():
                fetch(g + ahead + p).start()
        for p in range(n):
            fetch(g + p).wait()
        x = jnp.concatenate([_rowtile_load(xbuf.at[(g + p) % n_x], MOE_BLOCK) for p in range(n)],
                            axis=0).astype(BF16)
        gate = jnp.dot(x, wg_s[...], preferred_element_type=F32)
        up = jnp.dot(x, wu_s[...], preferred_element_type=F32)
        act = gate * jax.nn.sigmoid(gate) * up
        y = jnp.dot(act.astype(BF16), wd_s[...], preferred_element_type=F32)
        for p in range(n):
            @pl.when(g + p >= n_y)
            def _():
                writeback(g + p - n_y).wait()
        for p in range(n):
            _rowtile_store(ybuf.at[(g + p) % n_y], y[p * MOE_BLOCK:(p + 1) * MOE_BLOCK], MOE_BLOCK)
            writeback(g + p).start()

    def quad(jj, c):
        run_blocks(g0 + 4 * jj, 4)
        return c

    lax.fori_loop(0, nb // 4, quad, 0)
    rem = nb % 4

    @pl.when(rem >= 2)
    def _():
        run_blocks(g0 + nb - rem, 2)

    @pl.when(rem % 2 == 1)
    def _():
        run_blocks(g0 + nb - 1, 1)

    @pl.when(e == N_EXPERTS - 1)
    def _():
        for p in range(n_y, 0, -1):
            @pl.when(total >= p)
            def _():
                writeback(total - p).wait()

        ybuf[0] = jnp.zeros(ybuf.shape[1:], PACKED)

        def tail_copy(i):
            r0 = pl.multiple_of(tail_ref[i] * ROW_CHUNKS, blk_rows)
            return pltpu.make_async_copy(ybuf.at[0], y_hbm.at[pl.ds(r0, blk_rows), :], osem.at[0])

        def tstart(i, c):
            @pl.when(tail_ref[i] >= 0)
            def _():
                tail_copy(i).start()
            return c

        def twait(i, c):
            @pl.when(tail_ref[i] >= 0)
            def _():
                tail_copy(i).wait()
            return c

        lax.fori_loop(0, tail_ref.shape[0], tstart, 0)
        lax.fori_loop(0, tail_ref.shape[0], twait, 0)


def moe_experts(xs, first_block, n_blocks, tail_row, layer, w_gate, w_up, w_down):
    wspec = lambda shape: pl.BlockSpec((1, 1) + shape, lambda e, a, b, c: (layer, e, 0, 0))
    blk = (MOE_BLOCK * ROW_CHUNKS, LANES)
    return pl.pallas_call(
        _expert_kernel,
        grid_spec=pltpu.PrefetchScalarGridSpec(
            num_scalar_prefetch=3,
            grid=(N_EXPERTS,),
            in_specs=[pl.BlockSpec(memory_space=pl.ANY),
                      wspec((D_MODEL, D_EXPERT)), wspec((D_MODEL, D_EXPERT)),
                      wspec((D_EXPERT, D_MODEL))],
            out_specs=pl.BlockSpec(memory_space=pl.ANY),
            scratch_shapes=[pltpu.VMEM((EXPERT_X_BUFS,) + blk, PACKED),
                            pltpu.VMEM((EXPERT_Y_BUFS,) + blk, PACKED),
                            pltpu.VMEM((D_MODEL, D_EXPERT), BF16),
                            pltpu.VMEM((D_MODEL, D_EXPERT), BF16),
                            pltpu.VMEM((D_EXPERT, D_MODEL), BF16),
                            pltpu.SemaphoreType.DMA((EXPERT_X_BUFS,)),
                            pltpu.SemaphoreType.DMA((EXPERT_Y_BUFS,))]),
        out_shape=jax.ShapeDtypeStruct(xs.shape, PACKED),
        compiler_params=_cparams(("arbitrary",)),
        name="moe_experts",
    )(first_block, n_blocks, tail_row, xs, w_gate, w_up, w_down)


def _combine_kernel(dest_hbm, y_hbm, x_ref, sh_ref, w_ref, mod_ref, *rest, n_tiles, final, proj):
    if final:
        gf_ref, oc_ref, ol_ref, idx_s, ybuf, isem, gsem = rest
    elif proj:
        gn_ref, modn_ref, wn_ref, o_ref, p_ref, idx_s, ybuf, isem, gsem = rest
    else:
        o_ref, idx_s, ybuf, isem, gsem = rest
    i = pl.program_id(0)
    tm = COMBINE_TILE
    n_idx = TOP_K * tm
    last = n_tiles - 1

    def idx_copy(tile, s):
        return pltpu.make_async_copy(dest_hbm.at[tile], idx_s.at[pl.ds(s * n_idx, n_idx)], isem.at[s])

    def gather(s, unroll=4):
        _issue_row_copies(lambda r: idx_s[s * n_idx + r], n_idx,
                          lambda r, d: _row_copy(y_hbm, d, ybuf.at[s], r, gsem.at[s]), unroll)

    def gather_wait(s):
        pltpu.make_async_copy(y_hbm.at[pl.ds(0, n_idx * ROW_CHUNKS), :], ybuf.at[s], gsem.at[s]).wait()

    nslot = COMBINE_TILES_PER_STEP
    ahead = COMBINE_LOOKAHEAD
    assert n_tiles % nslot == 0 and ahead + 1 < nslot

    @pl.when(i == 0)
    def _():
        for t0 in range(ahead):
            c = idx_copy(min(t0, last), t0)
            c.start()
            c.wait()
            gather(t0)
        idx_copy(min(ahead, last), ahead).start()

    for half in range(nslot):
        tile = nslot * i + half
        s = half
        s_far = (half + ahead) % nslot
        rows = slice(half * tm, (half + 1) * tm)
        idx_copy(jnp.minimum(tile + ahead, last), s_far).wait()
        gather_wait(s)
        gather(s_far, unroll=True)
        idx_copy(jnp.minimum(tile + ahead + 1, last), (half + ahead + 1) % nslot).start()

        w = w_ref[rows, :]
        routed = jnp.zeros((tm, D_MODEL), F32)
        for k in range(TOP_K):
            routed = routed + w[:, k:k + 1] * _rowtile_load(ybuf.at[s], tm, base=k * tm * ROW_CHUNKS)
        out = x_ref[rows, :] + mod_ref[0, 5:6, :] * (routed + sh_ref[rows, :])
        if final:
            ms = jnp.mean(out * out, axis=-1, keepdims=True)
            y = out * lax.rsqrt(ms + EPS) * gf_ref[...]
            oc_ref[rows, :] = y
            ol_ref[rows, :] = y
        else:
            o_ref[rows, :] = out
            if proj:
                hn = _modnorm(out, gn_ref[...], modn_ref, 0)
                p_ref[rows, :] = jnp.dot(hn.astype(BF16), wn_ref[...], preferred_element_type=F32)

    @pl.when(i == n_tiles // nslot - 1)
    def _():
        for p in range(ahead):
            gather_wait(p)
        idx_copy(last, ahead).wait()


def moe_combine(lay, x, shared, y_rows, dest, wts_t, mods, g_final=None, next_proj=None):
    tm = COMBINE_TILE
    n_tiles = lay.t // tm
    nslot = COMBINE_TILES_PER_STEP
    step = nslot * tm
    nct = lay.t_ctx // step
    mi = _mod_index(lay, step)
    row = lambda i: (i, 0)
    final = g_final is not None
    if final:
        extra_in = [pl.BlockSpec((1, D_MODEL), lambda i: (0, 0))]
        extra_args = [g_final.reshape(1, -1)]
        out_specs = [pl.BlockSpec((step, D_MODEL), lambda i: (jnp.minimum(i, nct), 0)),
                     pl.BlockSpec((step, D_MODEL), lambda i: (jnp.maximum(i - nct, 0), 0))]
        out_shape = [jax.ShapeDtypeStruct((lay.t_ctx + step, D_MODEL), F32),
                     jax.ShapeDtypeStruct((lay.t_lat, D_MODEL), F32)]
    elif next_proj is not None:
        g_next, mods_next, w_next = next_proj
        n_out = w_next.shape[1]
        extra_in = [pl.BlockSpec((1, D_MODEL), lambda i: (0, 0)),
                    pl.BlockSpec((1, N_MOD, D_MODEL), lambda i: (mi(i), 0, 0)),
                    pl.BlockSpec((D_MODEL, n_out), lambda i: (0, 0))]
        extra_args = [g_next.reshape(1, -1), mods_next, w_next]
        out_specs = [pl.BlockSpec((step, D_MODEL), row), pl.BlockSpec((step, n_out), row)]
        out_shape = [jax.ShapeDtypeStruct((lay.t, D_MODEL), F32),
                     jax.ShapeDtypeStruct((lay.t, n_out), F32)]
    else:
        extra_in, extra_args = [], []
        out_specs = pl.BlockSpec((step, D_MODEL), row)
        out_shape = jax.ShapeDtypeStruct((lay.t, D_MODEL), F32)
    return pl.pallas_call(
        functools.partial(_combine_kernel, n_tiles=n_tiles, final=final,
                          proj=next_proj is not None and not final),
        grid=(n_tiles // nslot,),
        in_specs=[pl.BlockSpec(memory_space=pl.ANY),
                  pl.BlockSpec(memory_space=pl.ANY),
                  pl.BlockSpec((step, D_MODEL), row),
                  pl.BlockSpec((step, D_MODEL), row),
                  pl.BlockSpec((step, TOP_K), row),
                  pl.BlockSpec((1, N_MOD, D_MODEL), lambda i: (mi(i), 0, 0))] + extra_in,
        out_specs=out_specs,
        out_shape=out_shape,
        scratch_shapes=[pltpu.SMEM((nslot * TOP_K * tm,), I32),
                        pltpu.VMEM((nslot, TOP_K * tm * ROW_CHUNKS, LANES), PACKED),
                        pltpu.SemaphoreType.DMA((nslot,)),
                        pltpu.SemaphoreType.DMA((nslot,))],
        compiler_params=_cparams(("arbitrary",)),
        name="moe_combine",
    )(dest, y_rows, x, shared, wts_t, mods, *extra_args)


def moe_layer(lay, x, g, mods, router_w, router_b, layer, w_gate, w_up, w_down, s_gate, s_up, s_down,
              g_final=None, next_proj=None):
    t = lay.t
    hn, shared, eidx, wts, rank, cnt = moe_router(
        lay, x, g, mods, router_w, router_b, s_gate.astype(BF16), s_up.astype(BF16), s_down.astype(BF16))
    counts = cnt[:, 0].astype(I32)
    n_blocks = (counts + MOE_BLOCK - 1) // MOE_BLOCK
    padded = n_blocks * MOE_BLOCK
    pad_end = jnp.cumsum(padded)
    pad_start = pad_end - padded
    n_rows = -(-(t * TOP_K + N_EXPERTS * (MOE_BLOCK - 1)) // MOE_BLOCK) * MOE_BLOCK
    dest = moe_dest(pad_start, eidx, rank)
    last_row = jnp.where(n_blocks > 0, pad_end - MOE_BLOCK, -1)
    tail_blk = pad_end[-1] // MOE_BLOCK + jnp.arange(n_rows // MOE_BLOCK - t * TOP_K // MOE_BLOCK)
    tail_row = jnp.where(tail_blk < n_rows // MOE_BLOCK, tail_blk * MOE_BLOCK, -1).astype(I32)
    xs = moe_dispatch(lay, hn, _tile_major(dest, DISPATCH_TILE),
                      jnp.concatenate([last_row, tail_row]), n_rows)
    y_rows = moe_experts(xs, pad_start // MOE_BLOCK, n_blocks, tail_row, layer, w_gate, w_up, w_down)
    return moe_combine(lay, x, shared, y_rows, _tile_major(dest, COMBINE_TILE), wts.T, mods,
                       g_final, next_proj)


def _block_diag(w):
    nb, bw, _ = w.shape
    eye = jnp.eye(nb, dtype=w.dtype)
    return (eye[:, None, :, None] * w[:, :, None, :]).reshape(nb * bw, nb * bw)


def even_layer(lay, x, mods, g_mix, p, state_lru, state_ssm_re, state_ssm_im):
    t = lay.t
    proj, u_g = modnorm_matmul(lay, x, g_mix, mods, 0, p['w_in'].astype(BF16),
                               ug_col=2 * D_LRU)
    zeros_c = jnp.zeros((lay.n_ctx, D_LRU), F32)
    hf_y, st = None, []
    for d in range(2):
        wg = jnp.concatenate([_block_diag(p['lru_wa'][d]), _block_diag(p['lru_wx'][d])], axis=1)
        bg = jnp.concatenate([p['lru_ba'][d], p['lru_bx'][d]])
        h0 = jnp.concatenate([zeros_c, state_lru[:, d].astype(F32)], axis=0)
        hf_y, s = lru_pass(lay, proj, p['conv_w'], p['conv_b'], wg.astype(BF16), bg,
                           p['lru_lam'][d], h0, reverse=(d == 1), hf=hf_y)
        st.append(s[:lay.n_ctx])
    y_a = hf_y
    new_lru = jnp.stack(st, axis=1)

    mats = _s5_matrices(p['a_re'], p['a_im'], p['log_dt'], p['b_re'], p['b_im'], p['c_re'], p['c_im'])
    h0 = jnp.concatenate([state_ssm_re, state_ssm_im], axis=-1).astype(F32)
    h0 = h0.transpose(2, 1, 0, 3)
    y_g, h_ctx = s5_mixer(lay, u_g, mats, h0)
    seg = lay.s_ctx // S5_CHUNK
    h_ctx = h_ctx.reshape(SSM_GROUPS, 2, lay.n_ctx, seg, 2 * SSM_STATE)
    ends = jnp.stack([h_ctx[:, 0, :, seg - 1], h_ctx[:, 1, :, 0]], axis=1)
    ends = ends.transpose(2, 1, 0, 3)
    x = even_out(lay, x, y_a, y_g, proj, p['d'], p['glu_w'].astype(BF16), p['glu_b'],
                 p['w_out'].astype(BF16), mods)
    return x, new_lru, ends[..., :SSM_STATE], ends[..., SSM_STATE:]


def odd_layer(lay, x, mods, g_mix, w_qkv, sink, w_out, cache_k, cache_v, qkv=None):
    if qkv is None:
        qkv = modnorm_matmul(lay, x, g_mix, mods, 0, w_qkv.astype(BF16))
    o_ctx = attn_context(lay, qkv, sink)
    o_lat = attn_latent(lay, qkv, cache_k, cache_v, sink)
    nq = N_HEADS * HEAD_DIM
    kv = qkv[:lay.t_ctx, nq:].reshape(lay.n_ctx, lay.s_ctx, 2, N_KV, HEAD_DIM)
    k_new = kv[:, :, 0].swapaxes(1, 2)
    v_new = kv[:, :, 1].swapaxes(1, 2)
    x = matmul_residual(lay, x, o_ctx, o_lat, w_out.astype(BF16), mods)
    return x, k_new, v_new


def _forward(lay, x_prompt, x_sample, state_lru, state_ssm_re, state_ssm_im, cache_k, cache_v,
             c, c_ctx, g_mix, g_ffn, w_mod, b_mod,
             ev_w_in, lru_conv_w, lru_conv_b, lru_wa, lru_ba, lru_wx, lru_bx, lru_lam,
             ssm_a_re, ssm_a_im, ssm_log_dt, ssm_b_re, ssm_b_im, ssm_c_re, ssm_c_im, ssm_d,
             ssm_glu_w, ssm_glu_b, ev_w_out, at_w_qkv, at_sink, at_w_out,
             router_w, router_b, exp_w_gate, exp_w_up, exp_w_down, sh_w_gate, sh_w_up, sh_w_down,
             g_final):
    depth = g_mix.shape[0]
    x = (x_prompt.reshape(lay.t_ctx, D_MODEL), x_sample.reshape(lay.t_lat, D_MODEL))
    n_c = 1 + lay.n_lat
    c_rows = jnp.concatenate([c_ctx[None, :], c, jnp.zeros((16 - n_c, D_MODEL), F32)], axis=0)
    new_lru, new_re, new_im, new_k, new_v = [], [], [], [], []
    all_mods = [adaln_table(c_rows, l, w_mod, b_mod[l]) for l in range(depth)]
    qkv = None
    for l in range(depth):
        i = l // 2
        mods = all_mods[l]
        if l % 2 == 0:
            p = dict(w_in=ev_w_in[i], conv_w=lru_conv_w[i], conv_b=lru_conv_b[i],
                     lru_wa=lru_wa[i], lru_ba=lru_ba[i], lru_wx=lru_wx[i], lru_bx=lru_bx[i],
                     lru_lam=lru_lam[i], a_re=ssm_a_re[i], a_im=ssm_a_im[i], log_dt=ssm_log_dt[i],
                     b_re=ssm_b_re[i], b_im=ssm_b_im[i], c_re=ssm_c_re[i], c_im=ssm_c_im[i],
                     d=ssm_d[i], glu_w=ssm_glu_w[i], glu_b=ssm_glu_b[i], w_out=ev_w_out[i])
            x, lru_i, re_i, im_i = even_layer(lay, x, mods, g_mix[l], p, state_lru[:, i],
                                              state_ssm_re[:, i], state_ssm_im[:, i])
            new_lru.append(lru_i)
            new_re.append(re_i)
            new_im.append(im_i)
        else:
            if isinstance(x, tuple):
                x = jnp.concatenate(x, axis=0)
            x, k_i, v_i = odd_layer(lay, x, mods, g_mix[l], at_w_qkv[i], at_sink[i], at_w_out[i],
                                    cache_k[:, i], cache_v[:, i], qkv)
            new_k.append(k_i)
            new_v.append(v_i)
        next_proj = None
        if l + 1 < depth and (l + 1) % 2 == 1:
            next_proj = (g_mix[l + 1], all_mods[l + 1], at_w_qkv[(l + 1) // 2].astype(BF16))
        x = moe_layer(lay, x, g_ffn[l], mods, router_w[l], router_b[l], l, exp_w_gate, exp_w_up,
                      exp_w_down, sh_w_gate[l], sh_w_up[l], sh_w_down[l],
                      g_final=g_final if l == depth - 1 else None, next_proj=next_proj)
        qkv = None
        if next_proj is not None:
            x, qkv = x
    y_ctx, y_lat = x
    y_prompt = y_ctx[:lay.t_ctx].reshape(x_prompt.shape)
    y_sample = y_lat.reshape(x_sample.shape)
    return (y_prompt, y_sample, jnp.stack(new_lru, axis=1), jnp.stack(new_re, axis=1),
            jnp.stack(new_im, axis=1), jnp.stack(new_k, axis=1), jnp.stack(new_v, axis=1))


def kernel(x_prompt, x_sample, state_lru, state_ssm_re, state_ssm_im, cache_k, cache_v, c, c_ctx, g_mix, g_ffn, w_mod, b_mod, ev_w_in, lru_conv_w, lru_conv_b, lru_wa, lru_ba, lru_wx, lru_bx, lru_lam, ssm_a_re, ssm_a_im, ssm_log_dt, ssm_b_re, ssm_b_im, ssm_c_re, ssm_c_im, ssm_d, ssm_glu_w, ssm_glu_b, ev_w_out, at_w_qkv, at_sink, at_w_out, router_w, router_b, exp_w_gate, exp_w_up, exp_w_down, sh_w_gate, sh_w_up, sh_w_down, g_final):
    lay = Layout(n_ctx=x_prompt.shape[0], s_ctx=x_prompt.shape[1],
                 n_lat=x_sample.shape[0], s_lat=x_sample.shape[1])
    return _forward(lay, x_prompt, x_sample, state_lru, state_ssm_re, state_ssm_im, cache_k, cache_v,
                    c, c_ctx, g_mix, g_ffn, w_mod, b_mod,
                    ev_w_in, lru_conv_w, lru_conv_b, lru_wa, lru_ba, lru_wx, lru_bx, lru_lam,
                    ssm_a_re, ssm_a_im, ssm_log_dt, ssm_b_re, ssm_b_im, ssm_c_re, ssm_c_im, ssm_d,
                    ssm_glu_w, ssm_glu_b, ev_w_out, at_w_qkv, at_sink, at_w_out,
                    router_w, router_b, exp_w_gate, exp_w_up, exp_w_down, sh_w_gate, sh_w_up,
                    sh_w_down, g_final)
```

```python
import functools
from typing import NamedTuple

import jax
import jax.numpy as jnp
from jax import lax
from jax.experimental import pallas as pl
from jax.experimental.pallas import tpu as pltpu

F32 = jnp.float32
BF16 = jnp.bfloat16
I32 = jnp.int32
HIGHEST = lax.Precision.HIGHEST

D_MODEL = 1024
EPS = 1e-6
N_MOD = 6
GRID_W = 64
D_LRU = 512
LRU_BLOCKS = 8
LRU_C = 8.0
CONV_W = 4
CONV_LEFT = 2
D_SSM = 512
SSM_GROUP = 16
SSM_GROUPS = 32
SSM_STATE = 64
S5_CHUNK = 16
S5_LANES = S5_CHUNK * SSM_GROUP
S5_SCAN_STEPS = 8
HEAD_DIM = 64
N_HEADS = 16
N_KV = 4
GQA = 4
WINDOW = 128
Q_BLOCK = 128
ROPE_BASE = 10000.0
ATTN_SCALE = HEAD_DIM ** -0.5
NEG_INF = -1e30
N_EXPERTS = 256
TOP_K = 8
N_GROUPS = 8
TOPK_GROUPS = 4
GROUP_SIZE = N_EXPERTS // N_GROUPS
D_EXPERT = 256
ROUTE_SCALE = 2.5
MOE_BLOCK = 128

SUBLANES = 8
LANES = 128
ROW_CHUNKS = D_MODEL // (2 * LANES)
PACKED = jnp.int32
SEQ_TILE = 256
ROW_TILE = 512
ROUTER_TILE = 512
DEST_TILE = 1024
DISPATCH_TILE = 2048
COMBINE_TILE = 128
COMBINE_TILES_PER_STEP = 4
COMBINE_LOOKAHEAD = 2
EXPERT_X_BUFS = 24
EXPERT_Y_BUFS = 8
VMEM_LIMIT = 56 * 1024 * 1024


class Layout(NamedTuple):
    n_ctx: int
    s_ctx: int
    n_lat: int
    s_lat: int

    @property
    def t_ctx(self):
        return self.n_ctx * self.s_ctx

    @property
    def t_lat(self):
        return self.n_lat * self.s_lat

    @property
    def t(self):
        return self.t_ctx + self.t_lat

    @property
    def n_seq(self):
        return self.n_ctx + self.n_lat


def _cparams(sem):
    return pltpu.CompilerParams(dimension_semantics=sem, vmem_limit_bytes=VMEM_LIMIT)


def _mod_index(lay, tile_rows):
    n_ctx_tiles = lay.t_ctx // tile_rows
    per_lat = lay.s_lat // tile_rows

    def f(i):
        return jnp.where(i < n_ctx_tiles, 0, 1 + (i - n_ctx_tiles) // per_lat)
    return f


def _adaln_kernel(c_ref, w_ref, b_ref, o_ref):
    c = c_ref[...]
    s = c * jax.nn.sigmoid(c)
    o_ref[...] = jnp.dot(s, w_ref[0], precision=HIGHEST, preferred_element_type=F32) + b_ref[...]


def adaln_table(c_rows, layer, w_mod, b_mod):
    n = c_rows.shape[0]
    tn = 1536
    out = pl.pallas_call(
        _adaln_kernel,
        grid=(N_MOD * D_MODEL // tn,),
        in_specs=[pl.BlockSpec((n, D_MODEL), lambda j: (0, 0)),
                  pl.BlockSpec((1, D_MODEL, tn), lambda j: (layer, 0, j)),
                  pl.BlockSpec((1, tn), lambda j: (0, j))],
        out_specs=pl.BlockSpec((n, tn), lambda j: (0, j)),
        out_shape=jax.ShapeDtypeStruct((n, N_MOD * D_MODEL), F32),
        compiler_params=_cparams(("arbitrary",)),
        name="adaln",
    )(c_rows, w_mod, b_mod.reshape(1, -1))
    return out.reshape(n, N_MOD, D_MODEL)


def _modnorm(x, g, mod_ref, slot):
    ms = jnp.mean(x * x, axis=-1, keepdims=True)
    y = x * lax.rsqrt(ms + EPS) * g
    shift = mod_ref[0, slot:slot + 1, :]
    scale = mod_ref[0, slot + 1:slot + 2, :]
    return y * (1.0 + scale) + shift


GROUPS_PER_VREG = LANES // SSM_GROUP
SSM_COL_BLOCKS = D_SSM // LANES


def _group_major_store(val, tmp_ref, dst_ref):
    rows = dst_ref.shape[1]
    for j in range(SSM_COL_BLOCKS):
        tmp_ref[j] = val[:, j * LANES:(j + 1) * LANES]
    for j in range(SSM_COL_BLOCKS):
        steps = [tmp_ref[j, pl.ds(i, rows, stride=S5_CHUNK), :] for i in range(S5_CHUNK)]
        for q in range(GROUPS_PER_VREG):
            dst_ref[j * GROUPS_PER_VREG + q] = jnp.concatenate(
                [w[:, q * SSM_GROUP:(q + 1) * SSM_GROUP] for w in steps], axis=1)


def _group_major_load(src_ref, tmp_ref):
    rows = src_ref.shape[1]
    for j in range(SSM_COL_BLOCKS):
        blocks = [src_ref[j * GROUPS_PER_VREG + q] for q in range(GROUPS_PER_VREG)]
        for i in range(S5_CHUNK):
            tmp_ref[j, pl.ds(i, rows, stride=S5_CHUNK), :] = jnp.concatenate(
                [b[:, i * SSM_GROUP:(i + 1) * SSM_GROUP] for b in blocks], axis=1)
    return jnp.concatenate([tmp_ref[j] for j in range(SSM_COL_BLOCKS)], axis=1)


def _split_row_specs(lay, tile_rows, width):
    nct = lay.t_ctx // tile_rows
    return [pl.BlockSpec((tile_rows, width), lambda i: (jnp.minimum(i, nct - 1), 0)),
            pl.BlockSpec((tile_rows, width), lambda i: (jnp.maximum(i - nct, 0), 0))], nct


def _pick_rows(ctx_ref, lat_ref, n_ctx_tiles):
    return jnp.where(pl.program_id(0) < n_ctx_tiles, ctx_ref[...], lat_ref[...])


def _modnorm_mm_kernel(*refs, slot, ug_col, n_ctx_tiles):
    if n_ctx_tiles is None:
        x = refs[0][...]
        refs = refs[1:]
    else:
        x = _pick_rows(refs[0], refs[1], n_ctx_tiles)
        refs = refs[2:]
    g_ref, mod_ref, w_ref, o_ref, *ug_refs = refs
    h = _modnorm(x, g_ref[...], mod_ref, slot)
    out = jnp.dot(h.astype(BF16), w_ref[...], preferred_element_type=F32)
    o_ref[...] = out
    if ug_col is not None:
        ug_ref, tmp_ref = ug_refs
        _group_major_store(out[:, ug_col:ug_col + D_SSM], tmp_ref, ug_ref)


def modnorm_matmul(lay, x, g, mods, slot, w_bf16, ug_col=None):
    t = lay.t
    n = w_bf16.shape[1]
    mi = _mod_index(lay, ROW_TILE)
    out_specs = [pl.BlockSpec((ROW_TILE, n), lambda i: (i, 0))]
    out_shape = [jax.ShapeDtypeStruct((t, n), F32)]
    if ug_col is not None:
        out_specs.append(pl.BlockSpec((SSM_GROUPS, ROW_TILE // S5_CHUNK, S5_LANES), lambda i: (0, i, 0)))
        out_shape.append(jax.ShapeDtypeStruct((SSM_GROUPS, t // S5_CHUNK, S5_LANES), F32))
    if isinstance(x, tuple):
        x_specs, nct = _split_row_specs(lay, ROW_TILE, D_MODEL)
    else:
        x_specs, nct, x = [pl.BlockSpec((ROW_TILE, D_MODEL), lambda i: (i, 0))], None, (x,)
    outs = pl.pallas_call(
        functools.partial(_modnorm_mm_kernel, slot=slot, ug_col=ug_col, n_ctx_tiles=nct),
        grid=(t // ROW_TILE,),
        in_specs=x_specs + [
                  pl.BlockSpec((1, D_MODEL), lambda i: (0, 0)),
                  pl.BlockSpec((1, N_MOD, D_MODEL), lambda i: (mi(i), 0, 0)),
                  pl.BlockSpec((D_MODEL, n), lambda i: (0, 0))],
        out_specs=out_specs,
        out_shape=out_shape,
        scratch_shapes=([pltpu.VMEM((SSM_COL_BLOCKS, ROW_TILE, LANES), F32)]
                        if ug_col is not None else []),
        compiler_params=_cparams(("arbitrary",)),
        name="modnorm_matmul",
    )(*x, g.reshape(1, -1), mods, w_bf16)
    return outs if ug_col is not None else outs[0]


def _seq_tile_maps(lay, reverse):
    assert lay.s_ctx == SEQ_TILE and lay.s_lat % SEQ_TILE == 0
    n_tiles = lay.t // SEQ_TILE
    per_lat = lay.s_lat // SEQ_TILE

    def tile(i):
        return (n_tiles - 1 - i) if reverse else i

    def seq(i):
        ti = tile(i)
        return jnp.where(ti < lay.n_ctx, ti, lay.n_ctx + (ti - lay.n_ctx) // per_lat)

    return n_tiles, tile, seq


def _softplus(x):
    return jnp.maximum(x, 0.0) + jnp.log(1.0 + jnp.exp(-jnp.abs(x)))


def _lru_kernel(rec_ref, prev_ref, next_ref, cw_ref, cb_ref, wg_ref, bg_ref, lam_ref, h0_ref,
                *rest, reverse, n_ctx, per_lat, n_tiles):
    if reverse:
        gate_ref, hf_ref, y_ref, st_ref, a_s, b_s, h_s, carry = rest
    else:
        y_ref, st_ref, a_s, b_s, h_s, carry = rest
    i = pl.program_id(0)
    ti = (n_tiles - 1 - i) if reverse else i
    is_first = jnp.logical_or(ti < n_ctx, (ti - n_ctx) % per_lat == 0)
    is_last = jnp.logical_or(ti < n_ctx, (ti - n_ctx) % per_lat == per_lat - 1)
    ts = SEQ_TILE

    rec = rec_ref[...]
    prev = jnp.where(is_first, 0.0, prev_ref[...])
    nxt = jnp.where(is_last, 0.0, next_ref[...])
    ext = jnp.concatenate([prev, rec, nxt], axis=0)
    n_ext = ts + 2 * SUBLANES
    cw = cw_ref[...]
    xc = cb_ref[...] + cw[2:3, :] * rec
    xc = xc + cw[0:1, :] * pltpu.roll(ext, 2, 0)[SUBLANES:SUBLANES + ts]
    xc = xc + cw[1:2, :] * pltpu.roll(ext, 1, 0)[SUBLANES:SUBLANES + ts]
    xc = xc + cw[3:4, :] * pltpu.roll(ext, n_ext - 1, 0)[SUBLANES:SUBLANES + ts]

    gates = jax.nn.sigmoid(jnp.dot(xc.astype(BF16), wg_ref[...], preferred_element_type=F32)
                           + bg_ref[...])
    r = gates[:, :D_LRU]
    ig = gates[:, D_LRU:]
    log_a = (-LRU_C) * r * _softplus(-lam_ref[...])
    a = jnp.exp(log_a)
    b = jnp.sqrt(1.0 - jnp.exp(2.0 * log_a)) * (ig * xc)

    row8 = lax.broadcasted_iota(I32, (ts, D_LRU), 0) % SUBLANES
    for sh in (1, 2, 4):
        if reverse:
            keep = row8 < SUBLANES - sh
            a_sh = pltpu.roll(a, ts - sh, 0)
            b_sh = pltpu.roll(b, ts - sh, 0)
        else:
            keep = row8 >= sh
            a_sh = pltpu.roll(a, sh, 0)
            b_sh = pltpu.roll(b, sh, 0)
        b = b + a * jnp.where(keep, b_sh, 0.0)
        a = a * jnp.where(keep, a_sh, 1.0)
    a_s[...] = a
    b_s[...] = b

    @pl.when(is_last if reverse else is_first)
    def _():
        carry[...] = h0_ref[0]

    n_grp = ts // SUBLANES

    def body(k, c):
        gi = (n_grp - 1 - k) if reverse else k
        sl = pl.ds(pl.multiple_of(gi * SUBLANES, SUBLANES), SUBLANES)
        h = b_s[sl, :] + a_s[sl, :] * c
        h_s[sl, :] = h
        return h[0:1, :] if reverse else h[SUBLANES - 1:SUBLANES, :]

    c_fin = lax.fori_loop(0, n_grp, body, carry[...], unroll=4)
    carry[...] = c_fin
    st_ref[0] = c_fin
    if reverse:
        y_ref[...] = (hf_ref[...] + h_s[...]) * jax.nn.gelu(gate_ref[...])
    else:
        y_ref[...] = h_s[...]


def lru_pass(lay, proj, conv_w, conv_b, wg_bf16, bg, lam, h0, reverse, hf=None):
    n_tiles, tile, seq = _seq_tile_maps(lay, reverse)
    per_lat = lay.s_lat // SEQ_TILE
    blk8 = SEQ_TILE // SUBLANES
    last8 = lay.t // SUBLANES - 1
    c = D_LRU
    in_specs = [
        pl.BlockSpec((SEQ_TILE, c), lambda i: (tile(i), 1)),
        pl.BlockSpec((SUBLANES, c), lambda i: (jnp.maximum(tile(i) * blk8 - 1, 0), 1)),
        pl.BlockSpec((SUBLANES, c), lambda i: (jnp.minimum(tile(i) * blk8 + blk8, last8), 1)),
        pl.BlockSpec((CONV_W, c), lambda i: (0, 0)),
        pl.BlockSpec((1, c), lambda i: (0, 0)),
        pl.BlockSpec((c, 2 * c), lambda i: (0, 0)),
        pl.BlockSpec((1, 2 * c), lambda i: (0, 0)),
        pl.BlockSpec((1, c), lambda i: (0, 0)),
        pl.BlockSpec((1, 1, c), lambda i: (seq(i), 0, 0)),
    ]
    args = [proj, proj, proj, conv_w, conv_b.reshape(1, -1), wg_bf16, bg.reshape(1, -1),
            lam.reshape(1, -1), h0.reshape(lay.n_seq, 1, c)]
    if reverse:
        in_specs += [pl.BlockSpec((SEQ_TILE, c), lambda i: (tile(i), 0)),
                     pl.BlockSpec((SEQ_TILE, c), lambda i: (tile(i), 0))]
        args += [proj, hf]
    y, st = pl.pallas_call(
        functools.partial(_lru_kernel, reverse=reverse, n_ctx=lay.n_ctx, per_lat=per_lat,
                          n_tiles=n_tiles),
        grid=(n_tiles,),
        in_specs=in_specs,
        out_specs=[pl.BlockSpec((SEQ_TILE, c), lambda i: (tile(i), 0)),
                   pl.BlockSpec((1, 1, c), lambda i: (seq(i), 0, 0))],
        out_shape=[jax.ShapeDtypeStruct((lay.t, c), F32),
                   jax.ShapeDtypeStruct((lay.n_seq, 1, c), F32)],
        scratch_shapes=[pltpu.VMEM((SEQ_TILE, c), F32), pltpu.VMEM((SEQ_TILE, c), F32),
                        pltpu.VMEM((SEQ_TILE, c), F32), pltpu.VMEM((1, c), F32)],
        compiler_params=_cparams(("arbitrary",)),
        name="lru_bwd" if reverse else "lru_fwd",
    )(*args)
    return y, st.reshape(lay.n_seq, c)


def _cmul(a, b):
    return a[0] * b[0] - a[1] * b[1], a[0] * b[1] + a[1] * b[0]


def _s5_matrices(a_re, a_im, log_dt, b_re, b_im, c_re, c_im):
    a_re, a_im = a_re.astype(F32), a_im.astype(F32)
    dt = jnp.exp(log_dt.astype(F32))[..., None]
    z = (a_re * dt, a_im * dt)

    def zpow(k):
        k = k.reshape((-1,) + (1,) * z[0].ndim)
        mag = jnp.exp(k * z[0][None])
        return mag * jnp.cos(k * z[1][None]), mag * jnp.sin(k * z[1][None])

    a_bar = zpow(jnp.ones((1,), F32))
    a_bar = (a_bar[0][0], a_bar[1][0])
    den = a_re * a_re + a_im * a_im
    xr, xi = a_bar[0] - 1.0, a_bar[1]
    q = ((xr * a_re + xi * a_im) / den, (xi * a_re - xr * a_im) / den)
    b_bar = _cmul((q[0][..., None], q[1][..., None]), (b_re.astype(F32), b_im.astype(F32)))
    cc = (c_re.astype(F32), c_im.astype(F32))
    el = S5_CHUNK
    pw = zpow(jnp.arange(el + 1, dtype=F32))
    idx = jnp.arange(el)
    m_in, m_toep, m_out = [], [], []
    for d in range(2):
        p_d = (pw[0][:, d], pw[1][:, d])
        b_d = (b_bar[0][d], b_bar[1][d])
        c_d = (cc[0][d], cc[1][d])
        k_in = (el - 1 - idx) if d == 0 else idx
        w_in = _cmul((p_d[0][k_in][..., None], p_d[1][k_in][..., None]),
                     (b_d[0][None], b_d[1][None]))
        w_in = [jnp.transpose(w, (1, 0, 3, 2)).reshape(SSM_GROUPS, S5_LANES, SSM_STATE) for w in w_in]
        m_in.append(jnp.concatenate(w_in, axis=-1))
        cp = _cmul((c_d[0][None], c_d[1][None]),
                   (p_d[0][:, :, None, :], p_d[1][:, :, None, :]))
        kern = (jnp.einsum('kghp,gpc->kgch', cp[0][:el], b_d[0])
                - jnp.einsum('kghp,gpc->kgch', cp[1][:el], b_d[1]))
        zero = jnp.zeros_like(kern[0])
        rows = []
        for i in range(el):
            if d == 0:
                pieces = [zero] * i + [kern[k] for k in range(el - i)]
            else:
                pieces = [kern[i - j] for j in range(i + 1)] + [zero] * (el - 1 - i)
            rows.append(jnp.concatenate(pieces, axis=-1))
        m_toep.append(jnp.stack(rows, axis=1).reshape(SSM_GROUPS, S5_LANES, S5_LANES))
        k_out = (idx + 1) if d == 0 else (el - idx)
        w_out = [jnp.transpose(w[k_out], (1, 3, 0, 2)).reshape(SSM_GROUPS, SSM_STATE, S5_LANES)
                 for w in cp]
        m_out.append(jnp.concatenate([w_out[0], -w_out[1]], axis=1))
    mul = zpow(el * 2.0 ** jnp.arange(S5_SCAN_STEPS, dtype=F32))
    mul = [jnp.transpose(m, (2, 1, 0, 3)) for m in mul]
    coef_a = jnp.concatenate([mul[0], mul[0]], axis=-1)
    coef_b = jnp.concatenate([-mul[1], mul[1]], axis=-1)
    stack = lambda xs: jnp.stack(xs, axis=1)
    return (stack(m_in).astype(BF16), stack(m_toep).astype(BF16), stack(m_out).astype(BF16),
            coef_a, coef_b)


def _s5_scan(v, ca, cb, seg, reverse):
    n = v.shape[0]
    assert seg <= 2 ** S5_SCAN_STEPS
    row = lax.broadcasted_iota(I32, (n, 2 * SSM_STATE), 0) % seg
    k, sh = 0, 1
    while sh < seg:
        if reverse:
            s = jnp.where(row < seg - sh, pltpu.roll(v, n - sh, 0), 0.0)
        else:
            s = jnp.where(row >= sh, pltpu.roll(v, sh, 0), 0.0)
        v = v + ca[k:k + 1, :] * s + cb[k:k + 1, :] * pltpu.roll(s, SSM_STATE, 1)
        k += 1
        sh *= 2
    return v


def _s5_shift(h, seg, reverse):
    n = h.shape[0]
    row = lax.broadcasted_iota(I32, (n, 2 * SSM_STATE), 0) % seg
    if reverse:
        return jnp.where(row < seg - 1, pltpu.roll(h, n - 1, 0), 0.0)
    return jnp.where(row >= 1, pltpu.roll(h, 1, 0), 0.0)


def _s5_kernel(u_ref, min_ref, mtoep_ref, mout_ref, ca_ref, cb_ref, h0_ref, y_ref, hc_ref,
               v_s, hp_s, *, rc, seg_c, n_lat, seg_l):
    u = u_ref[0].astype(BF16)
    u_c, u_l = u[:rc], u[rc:]
    y_c = jnp.zeros((rc, S5_LANES), F32)
    y_l = jnp.zeros((n_lat * seg_l, S5_LANES), F32)
    for d in range(2):
        reverse = d == 1
        ca = ca_ref[0, d]
        cb = cb_ref[0, d]
        m_in = min_ref[0, d]
        m_toep = mtoep_ref[0, d]
        m_out = mout_ref[0, d]
        h_c = _s5_scan(jnp.dot(u_c, m_in, preferred_element_type=F32), ca, cb, seg_c, reverse)
        hc_ref[0, d] = h_c
        hp_c = _s5_shift(h_c, seg_c, reverse)
        y_c = y_c + jnp.dot(u_c, m_toep, preferred_element_type=F32)
        y_c = y_c + jnp.dot(hp_c.astype(BF16), m_out, preferred_element_type=F32)
        v_s[...] = jnp.dot(u_l, m_in, preferred_element_type=F32)
        for s in range(n_lat):
            h0 = h0_ref[0, d, s:s + 1, :]
            r0 = s * seg_l + (seg_l - 1 if reverse else 0)
            v_s[r0:r0 + 1, :] = (v_s[r0:r0 + 1, :] + ca[0:1, :] * h0
                                 + cb[0:1, :] * pltpu.roll(h0, SSM_STATE, 1))
        h_l = _s5_scan(v_s[...], ca, cb, seg_l, reverse)
        hp_s[...] = _s5_shift(h_l, seg_l, reverse)
        for s in range(n_lat):
            r0 = s * seg_l + (seg_l - 1 if reverse else 0)
            hp_s[r0:r0 + 1, :] = h0_ref[0, d, s:s + 1, :]
        y_l = y_l + jnp.dot(u_l, m_toep, preferred_element_type=F32)
        y_l = y_l + jnp.dot(hp_s[...].astype(BF16), m_out, preferred_element_type=F32)
    y_ref[0, :rc, :] = y_c
    y_ref[0, rc:, :] = y_l


def s5_mixer(lay, u_g, mats, h0):
    m_in, m_toep, m_out, coef_a, coef_b = mats
    rows = lay.t // S5_CHUNK
    rc = lay.t_ctx // S5_CHUNK
    rl = rows - rc
    st2 = 2 * SSM_STATE
    g4 = lambda g: (g, 0, 0, 0)
    return pl.pallas_call(
        functools.partial(_s5_kernel, rc=rc, seg_c=lay.s_ctx // S5_CHUNK, n_lat=lay.n_lat,
                          seg_l=lay.s_lat // S5_CHUNK),
        grid=(SSM_GROUPS,),
        in_specs=[pl.BlockSpec((1, rows, S5_LANES), lambda g: (g, 0, 0)),
                  pl.BlockSpec((1, 2, S5_LANES, st2), g4),
                  pl.BlockSpec((1, 2, S5_LANES, S5_LANES), g4),
                  pl.BlockSpec((1, 2, st2, S5_LANES), g4),
                  pl.BlockSpec((1, 2, S5_SCAN_STEPS, st2), g4),
                  pl.BlockSpec((1, 2, S5_SCAN_STEPS, st2), g4),
                  pl.BlockSpec((1, 2, lay.n_lat, st2), g4)],
        out_specs=[pl.BlockSpec((1, rows, S5_LANES), lambda g: (g, 0, 0)),
                   pl.BlockSpec((1, 2, rc, st2), g4)],
        out_shape=[jax.ShapeDtypeStruct((SSM_GROUPS, rows, S5_LANES), F32),
                   jax.ShapeDtypeStruct((SSM_GROUPS, 2, rc, st2), F32)],
        scratch_shapes=[pltpu.VMEM((rl, st2), F32), pltpu.VMEM((rl, st2), F32)],
        compiler_params=_cparams(("arbitrary",)),
        name="s5_mixer",
    )(u_g, m_in, m_toep, m_out, coef_a, coef_b, h0)


def _even_out_kernel(*refs, n_ctx_tiles):
    if n_ctx_tiles is None:
        x = refs[0][...]
        refs = refs[1:]
    else:
        x = _pick_rows(refs[0], refs[1], n_ctx_tiles)
        refs = refs[2:]
    ya_ref, yg_ref, u_ref, d_ref, gw_ref, gb_ref, w_ref, mod_ref, o_ref, yt_s = refs
    ys = _group_major_load(yg_ref, yt_s) + d_ref[...] * u_ref[...]
    g = jax.nn.gelu(ys)
    yb = g * jax.nn.sigmoid(jnp.dot(g.astype(BF16), gw_ref[...], preferred_element_type=F32)
                            + gb_ref[...])
    out = jnp.dot(ya_ref[...].astype(BF16), w_ref[:D_LRU, :], preferred_element_type=F32)
    out = out + jnp.dot(yb.astype(BF16), w_ref[D_LRU:, :], preferred_element_type=F32)
    o_ref[...] = x + mod_ref[0, 2:3, :] * out


def even_out(lay, x, y_a, y_g, proj, ssm_d, glu_w_bf16, glu_b, w_out_bf16, mods):
    mi = _mod_index(lay, ROW_TILE)
    c = D_SSM
    row = lambda i: (i, 0)
    const = lambda i: (0, 0)
    if isinstance(x, tuple):
        x_specs, nct = _split_row_specs(lay, ROW_TILE, D_MODEL)
    else:
        x_specs, nct, x = [pl.BlockSpec((ROW_TILE, D_MODEL), row)], None, (x,)
    return pl.pallas_call(
        functools.partial(_even_out_kernel, n_ctx_tiles=nct),
        grid=(lay.t // ROW_TILE,),
        in_specs=x_specs + [
                  pl.BlockSpec((ROW_TILE, c), row),
                  pl.BlockSpec((SSM_GROUPS, ROW_TILE // S5_CHUNK, S5_LANES), lambda i: (0, i, 0)),
                  pl.BlockSpec((ROW_TILE, c), lambda i: (i, 2)),
                  pl.BlockSpec((1, c), const),
                  pl.BlockSpec((c, c), const),
                  pl.BlockSpec((1, c), const),
                  pl.BlockSpec((D_MODEL, D_MODEL), const),
                  pl.BlockSpec((1, N_MOD, D_MODEL), lambda i: (mi(i), 0, 0))],
        out_specs=pl.BlockSpec((ROW_TILE, D_MODEL), row),
        out_shape=jax.ShapeDtypeStruct((lay.t, D_MODEL), F32),
        scratch_shapes=[pltpu.VMEM((SSM_COL_BLOCKS, ROW_TILE, LANES), F32)],
        compiler_params=_cparams(("arbitrary",)),
        name="even_out",
    )(*x, y_a, y_g, proj, ssm_d.reshape(1, -1), glu_w_bf16, glu_b.reshape(1, -1), w_out_bf16, mods)


def _softmax_pv(parts, sink_col):
    m = sink_col
    for s, _ in parts:
        m = jnp.maximum(m, jnp.max(s, axis=-1, keepdims=True))
    den = jnp.exp(sink_col - m)
    acc = None
    for s, v in parts:
        p = jnp.exp(s - m)
        den = den + jnp.sum(p, axis=-1, keepdims=True)
        pv = jnp.dot(p.astype(BF16), v.astype(BF16), preferred_element_type=F32)
        acc = pv if acc is None else acc + pv
    return acc / den


def _nt_dot(a, b):
    return lax.dot_general(a.astype(BF16), b.astype(BF16), (((1,), (1,)), ((), ())),
                           preferred_element_type=F32)


def _attn_ctx_kernel(q_ref, k_ref, v_ref, sink_ref, o_ref):
    n = q_ref.shape[0]
    for kh in range(N_KV):
        k = k_ref[:, kh * HEAD_DIM:(kh + 1) * HEAD_DIM]
        v = v_ref[:, kh * HEAD_DIM:(kh + 1) * HEAD_DIM]
        heads = [kh * GQA + g for g in range(GQA)]
        q = jnp.concatenate([q_ref[:, h * HEAD_DIM:(h + 1) * HEAD_DIM] for h in heads], axis=0)
        sink = jnp.concatenate([jnp.broadcast_to(sink_ref[0:1, h:h + 1], (n, 1)) for h in heads],
                               axis=0)
        o = _softmax_pv([(_nt_dot(q * ATTN_SCALE, k), v)], sink)
        for g, h in enumerate(heads):
            o_ref[:, h * HEAD_DIM:(h + 1) * HEAD_DIM] = o[g * n:(g + 1) * n]


def attn_context(lay, qkv, sink):
    nq = N_HEADS * HEAD_DIM
    nkv = N_KV * HEAD_DIM
    return pl.pallas_call(
        _attn_ctx_kernel,
        grid=(lay.n_ctx,),
        in_specs=[pl.BlockSpec((lay.s_ctx, nq), lambda b: (b, 0)),
                  pl.BlockSpec((lay.s_ctx, nkv), lambda b: (b, nq // nkv)),
                  pl.BlockSpec((lay.s_ctx, nkv), lambda b: (b, nq // nkv + 1)),
                  pl.BlockSpec((1, N_HEADS), lambda b: (0, 0))],
        out_specs=pl.BlockSpec((lay.s_ctx, nq), lambda b: (b, 0)),
        out_shape=jax.ShapeDtypeStruct((lay.t_ctx, nq), F32),
        compiler_params=_cparams(("arbitrary",)),
        name="attn_context",
    )(qkv, qkv, qkv, sink.reshape(1, -1))


def _rope(x, cos, sin):
    lane = lax.broadcasted_iota(I32, (x.shape[0], 2 * HEAD_DIM), 1) % HEAD_DIM
    outs = []
    for j in range(x.shape[1] // (2 * HEAD_DIM)):
        xs = x[:, j * 2 * HEAD_DIM:(j + 1) * 2 * HEAD_DIM]
        sw = jnp.where(lane < HEAD_DIM // 2,
                       pltpu.roll(xs, 2 * HEAD_DIM - HEAD_DIM // 2, 1),
                       pltpu.roll(xs, HEAD_DIM // 2, 1))
        outs.append(xs * cos + sw * sin)
    return outs


def _attn_lat_kernel(q_ref, k0_ref, k1_ref, k2_ref, v0_ref, v1_ref, v2_ref, ck_ref, cv_ref,
                     cq_ref, sq_ref, c0_ref, c1_ref, c2_ref, s0_ref, s1_ref, s2_ref, sink_ref,
                     o_ref, *, n_blk):
    j = pl.program_id(1)
    qb = Q_BLOCK
    q_parts = [qp * ATTN_SCALE for qp in _rope(q_ref[...], cq_ref[...], sq_ref[...])]
    k_parts = [_rope(kr[...], cr[...], sr[...])
               for kr, cr, sr in ((k0_ref, c0_ref, s0_ref), (k1_ref, c1_ref, s1_ref),
                                  (k2_ref, c2_ref, s2_ref))]
    qi = lax.broadcasted_iota(I32, (qb, 3 * qb), 0)
    km = lax.broadcasted_iota(I32, (qb, 3 * qb), 1)
    kpos = j * qb - qb + km
    mask1 = (jnp.abs(km - qb - qi) <= WINDOW) & (kpos >= 0) & (kpos < n_blk * qb)
    mask = jnp.concatenate([mask1] * GQA, axis=0)
    for kh in range(N_KV):
        half = (kh % 2) * HEAD_DIM
        k_loc = jnp.concatenate([kp[kh // 2][:, half:half + HEAD_DIM] for kp in k_parts], axis=0)
        v_loc = jnp.concatenate([vr[:, kh * HEAD_DIM:(kh + 1) * HEAD_DIM]
                                 for vr in (v0_ref, v1_ref, v2_ref)], axis=0)
        qs, sinks = [], []
        for g in range(GQA):
            h = kh * GQA + g
            qs.append(q_parts[h // 2][:, (h % 2) * HEAD_DIM:(h % 2 + 1) * HEAD_DIM])
            sinks.append(jnp.broadcast_to(sink_ref[0:1, h:h + 1], (qb, 1)))
        q = jnp.concatenate(qs, axis=0)
        sink = jnp.concatenate(sinks, axis=0)
        s_loc = jnp.where(mask, _nt_dot(q, k_loc), NEG_INF)
        s_ctx = _nt_dot(q, ck_ref[0, kh])
        o = _softmax_pv([(s_loc, v_loc), (s_ctx, cv_ref[0, kh])], sink)
        for g in range(GQA):
            h = kh * GQA + g
            o_ref[:, h * HEAD_DIM:(h + 1) * HEAD_DIM] = o[g * qb:(g + 1) * qb]


def _rope_tables(s_len):
    rows = s_len // GRID_W
    row = jnp.repeat(jnp.arange(rows), GRID_W).astype(F32)
    col = jnp.tile(jnp.arange(GRID_W), rows).astype(F32)
    nf = HEAD_DIM // 4
    inv = ROPE_BASE ** (-jnp.arange(nf, dtype=F32) / nf)
    ang = jnp.concatenate([row[:, None] * inv, col[:, None] * inv], axis=-1)
    cos, sin = jnp.cos(ang), jnp.sin(ang)
    cos2 = jnp.tile(jnp.concatenate([cos, cos], axis=-1), (1, 2))
    sin2 = jnp.tile(jnp.concatenate([-sin, sin], axis=-1), (1, 2))
    return cos2, sin2


def attn_latent(lay, qkv, cache_k, cache_v, sink):
    nq = N_HEADS * HEAD_DIM
    nkv = N_KV * HEAD_DIM
    n_blk = lay.s_lat // Q_BLOCK
    base = lay.t_ctx // Q_BLOCK
    n_ctx_keys = cache_k.shape[2]
    cos2, sin2 = _rope_tables(lay.s_lat)
    kcol = nq // nkv

    def qrow(b, j):
        return base + b * n_blk + j

    def krow(off):
        return lambda b, j: base + b * n_blk + jnp.clip(j + off, 0, n_blk - 1)

    def trow(off):
        return lambda b, j: (jnp.clip(j + off, 0, n_blk - 1), 0)

    kv_spec = lambda off, col: pl.BlockSpec((Q_BLOCK, nkv), lambda b, j: (krow(off)(b, j), col))
    tab = lambda off: pl.BlockSpec((Q_BLOCK, 2 * HEAD_DIM), trow(off))
    cache_spec = pl.BlockSpec((1, N_KV, n_ctx_keys, HEAD_DIM), lambda b, j: (b, 0, 0, 0))
    return pl.pallas_call(
        functools.partial(_attn_lat_kernel, n_blk=n_blk),
        grid=(lay.n_lat, n_blk),
        in_specs=[pl.BlockSpec((Q_BLOCK, nq), lambda b, j: (qrow(b, j), 0)),
                  kv_spec(-1, kcol), kv_spec(0, kcol), kv_spec(1, kcol),
                  kv_spec(-1, kcol + 1), kv_spec(0, kcol + 1), kv_spec(1, kcol + 1),
                  cache_spec, cache_spec,
                  tab(0), tab(0), tab(-1), tab(0), tab(1), tab(-1), tab(0), tab(1),
                  pl.BlockSpec((1, N_HEADS), lambda b, j: (0, 0))],
        out_specs=pl.BlockSpec((Q_BLOCK, nq), lambda b, j: (b * n_blk + j, 0)),
        out_shape=jax.ShapeDtypeStruct((lay.t_lat, nq), F32),
        compiler_params=_cparams(("arbitrary", "arbitrary")),
        name="attn_latent",
    )(qkv, qkv, qkv, qkv, qkv, qkv, qkv, cache_k, cache_v,
      cos2, sin2, cos2, cos2, cos2, sin2, sin2, sin2, sink.reshape(1, -1))


def _rowtile_load(ref, n, base=0):
    parts = []
    for c in range(ROW_CHUNKS):
        words = ref[pl.ds(base + c, n, stride=ROW_CHUNKS), :]
        for half in range(2):
            parts.append(pltpu.unpack_elementwise(words, index=half, packed_dtype=BF16,
                                                  unpacked_dtype=F32))
    return jnp.concatenate(parts, axis=1)


def _rowtile_store(ref, val, n):
    for c in range(ROW_CHUNKS):
        lo = val[:, 2 * c * LANES:(2 * c + 1) * LANES]
        hi = val[:, (2 * c + 1) * LANES:(2 * c + 2) * LANES]
        ref[pl.ds(c, n, stride=ROW_CHUNKS), :] = pltpu.pack_elementwise([lo, hi], packed_dtype=BF16)


def _row_copy(src, src_row, dst, dst_row, sem):
    return pltpu.make_async_copy(
        src.at[pl.ds(pl.multiple_of(src_row * ROW_CHUNKS, ROW_CHUNKS), ROW_CHUNKS), :],
        dst.at[pl.ds(pl.multiple_of(dst_row * ROW_CHUNKS, ROW_CHUNKS), ROW_CHUNKS), :], sem)


def _router_kernel(x_ref, g_ref, mod_ref, rwt_ref, rb_ref, tri_ref, sg_ref, su_ref, sd_ref, *rest,
                   n_ctx_tiles):
    if n_ctx_tiles is None:
        hn_ref, sh_ref, eidx_ref, wts_ref, rank_ref, cnt_ref, cnt_s = rest
        x = x_ref[...]
    else:
        ac_ref, al_ref, wo_ref, hn_ref, sh_ref, eidx_ref, wts_ref, rank_ref, cnt_ref, xn_ref, cnt_s = rest
        a = _pick_rows(ac_ref, al_ref, n_ctx_tiles)
        x = x_ref[...] + mod_ref[0, 2:3, :] * jnp.dot(a.astype(BF16), wo_ref[...],
                                                     preferred_element_type=F32)
        xn_ref[...] = x
    tm = ROUTER_TILE

    @pl.when(pl.program_id(0) == 0)
    def _():
        cnt_s[...] = jnp.zeros_like(cnt_s)

    h = _modnorm(x, g_ref[...], mod_ref, 3)
    _rowtile_store(hn_ref, h, tm)
    hb = h.astype(BF16)
    sgate = jnp.dot(hb, sg_ref[...], preferred_element_type=F32)
    sup = jnp.dot(hb, su_ref[...], preferred_element_type=F32)
    sh_ref[...] = jnp.dot((sgate * jax.nn.sigmoid(sgate) * sup).astype(BF16), sd_ref[...],
                          preferred_element_type=F32)
    logits = lax.dot_general(rwt_ref[...], h, (((1,), (1,)), ((), ())),
                             precision=HIGHEST, preferred_element_type=F32)
    scores = jax.nn.sigmoid(logits)
    choice = scores + rb_ref[...]
    gs_rows = []
    for g in range(N_GROUPS):
        cg = choice[g * GROUP_SIZE:(g + 1) * GROUP_SIZE, :]
        m1 = jnp.max(cg, axis=0, keepdims=True)
        eq = cg == m1
        cnt = jnp.sum(eq.astype(F32), axis=0, keepdims=True)
        m2 = jnp.max(jnp.where(eq, -jnp.inf, cg), axis=0, keepdims=True)
        gs_rows.append(m1 + jnp.where(cnt >= 2.0, m1, m2))
    gs = jnp.concatenate(gs_rows, axis=0)
    gi = lax.broadcasted_iota(I32, (N_GROUPS, tm), 0)
    grank = jnp.zeros((N_GROUPS, tm), I32)
    for g in range(N_GROUPS):
        other = gs[g:g + 1, :]
        ahead = (other > gs) | ((other == gs) & (g < gi))
        grank = grank + ahead.astype(I32)
    gsel = grank < TOPK_GROUPS
    emask = jnp.concatenate(
        [jnp.broadcast_to(gsel[g:g + 1, :], (GROUP_SIZE, tm)) for g in range(N_GROUPS)], axis=0)
    masked = jnp.where(emask, choice, -jnp.inf)
    ei = lax.broadcasted_iota(I32, (N_EXPERTS, tm), 0)
    idxs, ws = [], []
    member = jnp.zeros((N_EXPERTS, tm), F32)
    for _ in range(TOP_K):
        m = jnp.max(masked, axis=0, keepdims=True)
        idx = jnp.min(jnp.where(masked == m, ei, N_EXPERTS), axis=0, keepdims=True)
        hit = ei == idx
        ws.append(jnp.sum(jnp.where(hit, scores, 0.0), axis=0, keepdims=True))
        idxs.append(idx)
        member = jnp.where(hit, 1.0, member)
        masked = jnp.where(hit, -jnp.inf, masked)
    w = jnp.concatenate(ws, axis=0)
    wts_ref[...] = w / jnp.sum(w, axis=0, keepdims=True) * ROUTE_SCALE
    eidx_ref[...] = jnp.concatenate(idxs, axis=0)
    before = jnp.dot(member.astype(BF16), tri_ref[...], preferred_element_type=F32) + cnt_s[...]
    ranks = [jnp.sum(jnp.where(ei == idx, before, 0.0), axis=0, keepdims=True) for idx in idxs]
    rank_ref[...] = jnp.concatenate(ranks, axis=0).astype(I32)
    cnt_s[...] = cnt_s[...] + jnp.sum(member, axis=1, keepdims=True)
    cnt_ref[...] = jnp.broadcast_to(cnt_s[...], cnt_ref.shape)


def moe_router(lay, x, g, mods, router_w, router_b, sg_bf16, su_bf16, sd_bf16, attn_out=None):
    t = lay.t
    tm = ROUTER_TILE
    mi = _mod_index(lay, tm)
    tri = (jnp.arange(tm)[:, None] < jnp.arange(tm)[None, :]).astype(BF16)
    tok = lambda i: (0, i)
    const = lambda i: (0, 0)
    extra_in, extra_args, extra_out, extra_shape, nct = [], [], [], [], None
    if attn_out is not None:
        a_ctx, a_lat, w_out = attn_out
        row_specs, nct = _split_row_specs(lay, tm, a_ctx.shape[1])
        extra_in = row_specs + [pl.BlockSpec(w_out.shape, const)]
        extra_args = [a_ctx, a_lat, w_out]
        extra_out = [pl.BlockSpec((tm, D_MODEL), lambda i: (i, 0))]
        extra_shape = [jax.ShapeDtypeStruct((t, D_MODEL), F32)]
    return pl.pallas_call(
        functools.partial(_router_kernel, n_ctx_tiles=nct),
        grid=(t // tm,),
        in_specs=[pl.BlockSpec((tm, D_MODEL), lambda i: (i, 0)),
                  pl.BlockSpec((1, D_MODEL), const),
                  pl.BlockSpec((1, N_MOD, D_MODEL), lambda i: (mi(i), 0, 0)),
                  pl.BlockSpec((N_EXPERTS, D_MODEL), const),
                  pl.BlockSpec((N_EXPERTS, 1), const),
                  pl.BlockSpec((tm, tm), const),
                  pl.BlockSpec((D_MODEL, D_EXPERT), const),
                  pl.BlockSpec((D_MODEL, D_EXPERT), const),
                  pl.BlockSpec((D_EXPERT, D_MODEL), const)] + extra_in,
        out_specs=[pl.BlockSpec((tm * ROW_CHUNKS, LANES), lambda i: (i, 0)),
                   pl.BlockSpec((tm, D_MODEL), lambda i: (i, 0)),
                   pl.BlockSpec((TOP_K, tm), tok),
                   pl.BlockSpec((TOP_K, tm), tok),
                   pl.BlockSpec((TOP_K, tm), tok),
                   pl.BlockSpec((N_EXPERTS, LANES), const)] + extra_out,
        out_shape=[jax.ShapeDtypeStruct((t * ROW_CHUNKS, LANES), PACKED),
                   jax.ShapeDtypeStruct((t, D_MODEL), F32),
                   jax.ShapeDtypeStruct((TOP_K, t), I32),
                   jax.ShapeDtypeStruct((TOP_K, t), F32),
                   jax.ShapeDtypeStruct((TOP_K, t), I32),
                   jax.ShapeDtypeStruct((N_EXPERTS, LANES), F32)] + extra_shape,
        scratch_shapes=[pltpu.VMEM((N_EXPERTS, 1), F32)],
        compiler_params=_cparams(("arbitrary",)),
        name="moe_router",
    )(x, g.reshape(1, -1), mods, router_w.T, router_b.reshape(-1, 1), tri, sg_bf16, su_bf16, sd_bf16,
      *extra_args)


def _dest_kernel(start_ref, eidx_ref, rank_ref, dest_ref):
    e = eidx_ref[...]

    def body(i, acc):
        return jnp.where(e == i, start_ref[i], acc)

    dest_ref[...] = lax.fori_loop(0, N_EXPERTS, body, jnp.zeros_like(e), unroll=8) + rank_ref[...]


def moe_dest(pad_start, eidx, rank):
    t = eidx.shape[1]
    tn = DEST_TILE
    spec = pl.BlockSpec((TOP_K, tn), lambda i, ps: (0, i))
    return pl.pallas_call(
        _dest_kernel,
        grid_spec=pltpu.PrefetchScalarGridSpec(
            num_scalar_prefetch=1, grid=(t // tn,), in_specs=[spec, spec], out_specs=spec),
        out_shape=jax.ShapeDtypeStruct((TOP_K, t), I32),
        compiler_params=_cparams(("arbitrary",)),
        name="moe_dest",
    )(pad_start, eidx, rank)


def _issue_row_copies(idx_at, n, copy_at, unroll=4):
    def body(i, c):
        for p in range(2):
            r = 2 * i + p
            copy_at(r, idx_at(r)).start(priority=p)
        return c
    lax.fori_loop(0, n // 2, body, 0, unroll=unroll)


def _dispatch_kernel(zrow_ref, dest_hbm, hn_ref, xs_hbm, idx_s, zbuf, isem, zsem, ssem, *, n_tiles):
    i = pl.program_id(0)
    slot = i % 2
    td = DISPATCH_TILE

    n_idx = TOP_K * td

    def idx_copy(tile, s):
        return pltpu.make_async_copy(dest_hbm.at[tile], idx_s.at[pl.ds(s * n_idx, n_idx)], isem.at[s])

    def zero_copy(e):
        r0 = pl.multiple_of(zrow_ref[e] * ROW_CHUNKS, ROW_CHUNKS)
        return pltpu.make_async_copy(zbuf, xs_hbm.at[pl.ds(r0, MOE_BLOCK * ROW_CHUNKS), :], zsem)

    @pl.when(i == 0)
    def _():
        zbuf[...] = jnp.zeros_like(zbuf)

        def zstart(e, c):
            @pl.when(zrow_ref[e] >= 0)
            def _():
                zero_copy(e).start()
            return c

        def zwait(e, c):
            @pl.when(zrow_ref[e] >= 0)
            def _():
                zero_copy(e).wait()
            return c

        lax.fori_loop(0, zrow_ref.shape[0], zstart, 0)
        idx_copy(0, 0).start()
        lax.fori_loop(0, zrow_ref.shape[0], zwait, 0)

    idx_copy(i, slot).wait()

    @pl.when(i + 1 < n_tiles)
    def _():
        idx_copy(i + 1, 1 - slot).start()

    for k in range(TOP_K):
        _issue_row_copies(lambda r: idx_s[slot * n_idx + k * td + r], td,
                          lambda r, d: _row_copy(hn_ref, r, xs_hbm, d, ssem))
    for k in range(TOP_K):
        pltpu.make_async_copy(hn_ref, xs_hbm.at[pl.ds(0, td * ROW_CHUNKS), :], ssem).wait()


def _tile_major(dest, tile):
    t = dest.shape[1]
    return dest.reshape(TOP_K, t // tile, tile).transpose(1, 0, 2).reshape(t // tile, TOP_K * tile)


def moe_dispatch(lay, hn, dest, zero_row, n_rows):
    td = DISPATCH_TILE
    n_tiles = lay.t // td
    return pl.pallas_call(
        functools.partial(_dispatch_kernel, n_tiles=n_tiles),
        grid_spec=pltpu.PrefetchScalarGridSpec(
            num_scalar_prefetch=1,
            grid=(n_tiles,),
            in_specs=[pl.BlockSpec(memory_space=pl.ANY),
                      pl.BlockSpec((td * ROW_CHUNKS, LANES), lambda i, z: (i, 0))],
            out_specs=pl.BlockSpec(memory_space=pl.ANY),
            scratch_shapes=[pltpu.SMEM((2 * TOP_K * td,), I32),
                            pltpu.VMEM((MOE_BLOCK * ROW_CHUNKS, LANES), PACKED),
                            pltpu.SemaphoreType.DMA((2,)),
                            pltpu.SemaphoreType.DMA,
                            pltpu.SemaphoreType.DMA]),
        out_shape=jax.ShapeDtypeStruct((n_rows * ROW_CHUNKS, LANES), PACKED),
        compiler_params=_cparams(("arbitrary",)),
        name="moe_dispatch",
    )(zero_row, dest, hn)


def _expert_kernel(blk0_ref, nblk_ref, tail_ref, xs_hbm, wg_ref, wu_ref, wd_ref, y_hbm,
                   xbuf, ybuf, wg_s, wu_s, wd_s, isem, osem):
    e = pl.program_id(0)
    nb = nblk_ref[e]
    g0 = blk0_ref[e]
    total = blk0_ref[N_EXPERTS - 1] + nblk_ref[N_EXPERTS - 1]
    blk_rows = MOE_BLOCK * ROW_CHUNKS
    n_x = xbuf.shape[0]
    n_y = ybuf.shape[0]

    def block_rows(g):
        return pl.ds(pl.multiple_of(g * blk_rows, blk_rows), blk_rows)

    def fetch(g):
        s = g % n_x
        return pltpu.make_async_copy(xs_hbm.at[block_rows(g), :], xbuf.at[s], isem.at[s])

    def writeback(g):
        s = g % n_y
        return pltpu.make_async_copy(ybuf.at[s], y_hbm.at[block_rows(g), :], osem.at[s])

    ahead = n_x // 2

    @pl.when(e == 0)
    def _():
        for p in range(ahead):
            @pl.when(p < total)
            def _():
                fetch(p).start()

    @pl.when(nb > 0)
    def _():
        wg_s[...] = wg_ref[0, 0].astype(BF16)
        wu_s[...] = wu_ref[0, 0].astype(BF16)
        wd_s[...] = wd_ref[0, 0].astype(BF16)

    def run_blocks(g, n):
        for p in range(n):
            @pl.when(g + ahead + p < total)
            def _():
                fetch(g + ahead + p).start()
        for p in range(n):
            fetch(g + p).wait()
        x = jnp.concatenate([_rowtile_load(xbuf.at[(g + p) % n_x], MOE_BLOCK) for p in range(n)],
                            axis=0).astype(BF16)
        gate = jnp.dot(x, wg_s[...], preferred_element_type=F32)
        up = jnp.dot(x, wu_s[...], preferred_element_type=F32)
        act = gate * jax.nn.sigmoid(gate) * up
        y = jnp.dot(act.astype(BF16), wd_s[...], preferred_element_type=F32)
        for p in range(n):
            @pl.when(g + p >= n_y)
            def _():
                writeback(g + p - n_y).wait()
        for p in range(n):
            _rowtile_store(ybuf.at[(g + p) % n_y], y[p * MOE_BLOCK:(p + 1) * MOE_BLOCK], MOE_BLOCK)
            writeback(g + p).start()

    def quad(jj, c):
        run_blocks(g0 + 4 * jj, 4)
        return c

    lax.fori_loop(0, nb // 4, quad, 0)
    rem = nb % 4

    @pl.when(rem >= 2)
    def _():
        run_blocks(g0 + nb - rem, 2)

    @pl.when(rem % 2 == 1)
    def _():
        run_blocks(g0 + nb - 1, 1)

    @pl.when(e == N_EXPERTS - 1)
    def _():
        for p in range(n_y, 0, -1):
            @pl.when(total >= p)
            def _():
                writeback(total - p).wait()

        ybuf[0] = jnp.zeros(ybuf.shape[1:], PACKED)

        def tail_copy(i):
            r0 = pl.multiple_of(tail_ref[i] * ROW_CHUNKS, blk_rows)
            return pltpu.make_async_copy(ybuf.at[0], y_hbm.at[pl.ds(r0, blk_rows), :], osem.at[0])

        def tstart(i, c):
            @pl.when(tail_ref[i] >= 0)
            def _():
                tail_copy(i).start()
            return c

        def twait(i, c):
            @pl.when(tail_ref[i] >= 0)
            def _():
                tail_copy(i).wait()
            return c

        lax.fori_loop(0, tail_ref.shape[0], tstart, 0)
        lax.fori_loop(0, tail_ref.shape[0], twait, 0)


def moe_experts(xs, first_block, n_blocks, tail_row, layer, w_gate, w_up, w_down):
    wspec = lambda shape: pl.BlockSpec((1, 1) + shape, lambda e, a, b, c: (layer, e, 0, 0))
    blk = (MOE_BLOCK * ROW_CHUNKS, LANES)
    return pl.pallas_call(
        _expert_kernel,
        grid_spec=pltpu.PrefetchScalarGridSpec(
            num_scalar_prefetch=3,
            grid=(N_EXPERTS,),
            in_specs=[pl.BlockSpec(memory_space=pl.ANY),
                      wspec((D_MODEL, D_EXPERT)), wspec((D_MODEL, D_EXPERT)),
                      wspec((D_EXPERT, D_MODEL))],
            out_specs=pl.BlockSpec(memory_space=pl.ANY),
            scratch_shapes=[pltpu.VMEM((EXPERT_X_BUFS,) + blk, PACKED),
                            pltpu.VMEM((EXPERT_Y_BUFS,) + blk, PACKED),
                            pltpu.VMEM((D_MODEL, D_EXPERT), BF16),
                            pltpu.VMEM((D_MODEL, D_EXPERT), BF16),
                            pltpu.VMEM((D_EXPERT, D_MODEL), BF16),
                            pltpu.SemaphoreType.DMA((EXPERT_X_BUFS,)),
                            pltpu.SemaphoreType.DMA((EXPERT_Y_BUFS,))]),
        out_shape=jax.ShapeDtypeStruct(xs.shape, PACKED),
        compiler_params=_cparams(("arbitrary",)),
        name="moe_experts",
    )(first_block, n_blocks, tail_row, xs, w_gate, w_up, w_down)


def _combine_kernel(dest_hbm, y_hbm, x_ref, sh_ref, w_ref, mod_ref, *rest, n_tiles, final, proj):
    if final:
        gf_ref, oc_ref, ol_ref, idx_s, ybuf, isem, gsem = rest
    elif proj:
        gn_ref, modn_ref, wn_ref, o_ref, p_ref, idx_s, ybuf, isem, gsem = rest
    else:
        o_ref, idx_s, ybuf, isem, gsem = rest
    i = pl.program_id(0)
    tm = COMBINE_TILE
    n_idx = TOP_K * tm
    last = n_tiles - 1

    def idx_copy(tile, s):
        return pltpu.make_async_copy(dest_hbm.at[tile], idx_s.at[pl.ds(s * n_idx, n_idx)], isem.at[s])

    def gather(s, unroll=4):
        _issue_row_copies(lambda r: idx_s[s * n_idx + r], n_idx,
                          lambda r, d: _row_copy(y_hbm, d, ybuf.at[s], r, gsem.at[s]), unroll)

    def gather_wait(s):
        pltpu.make_async_copy(y_hbm.at[pl.ds(0, n_idx * ROW_CHUNKS), :], ybuf.at[s], gsem.at[s]).wait()

    nslot = COMBINE_TILES_PER_STEP
    ahead = COMBINE_LOOKAHEAD
    assert n_tiles % nslot == 0 and ahead + 1 < nslot

    @pl.when(i == 0)
    def _():
        for t0 in range(ahead):
            c = idx_copy(min(t0, last), t0)
            c.start()
            c.wait()
            gather(t0)
        idx_copy(min(ahead, last), ahead).start()

    for half in range(nslot):
        tile = nslot * i + half
        s = half
        s_far = (half + ahead) % nslot
        rows = slice(half * tm, (half + 1) * tm)
        idx_copy(jnp.minimum(tile + ahead, last), s_far).wait()
        gather_wait(s)
        gather(s_far, unroll=True)
        idx_copy(jnp.minimum(tile + ahead + 1, last), (half + ahead + 1) % nslot).start()

        w = w_ref[rows, :]
        routed = jnp.zeros((tm, D_MODEL), F32)
        for k in range(TOP_K):
            routed = routed + w[:, k:k + 1] * _rowtile_load(ybuf.at[s], tm, base=k * tm * ROW_CHUNKS)
        out = x_ref[rows, :] + mod_ref[0, 5:6, :] * (routed + sh_ref[rows, :])
        if final:
            ms = jnp.mean(out * out, axis=-1, keepdims=True)
            y = out * lax.rsqrt(ms + EPS) * gf_ref[...]
            oc_ref[rows, :] = y
            ol_ref[rows, :] = y
        else:
            o_ref[rows, :] = out
            if proj:
                hn = _modnorm(out, gn_ref[...], modn_ref, 0)
                p_ref[rows, :] = jnp.dot(hn.astype(BF16), wn_ref[...], preferred_element_type=F32)

    @pl.when(i == n_tiles // nslot - 1)
    def _():
        for p in range(ahead):
            gather_wait(p)
        idx_copy(last, ahead).wait()


def moe_combine(lay, x, shared, y_rows, dest, wts_t, mods, g_final=None, next_proj=None):
    tm = COMBINE_TILE
    n_tiles = lay.t // tm
    nslot = COMBINE_TILES_PER_STEP
    step = nslot * tm
    nct = lay.t_ctx // step
    mi = _mod_index(lay, step)
    row = lambda i: (i, 0)
    final = g_final is not None
    if final:
        extra_in = [pl.BlockSpec((1, D_MODEL), lambda i: (0, 0))]
        extra_args = [g_final.reshape(1, -1)]
        out_specs = [pl.BlockSpec((step, D_MODEL), lambda i: (jnp.minimum(i, nct), 0)),
                     pl.BlockSpec((step, D_MODEL), lambda i: (jnp.maximum(i - nct, 0), 0))]
        out_shape = [jax.ShapeDtypeStruct((lay.t_ctx + step, D_MODEL), F32),
                     jax.ShapeDtypeStruct((lay.t_lat, D_MODEL), F32)]
    elif next_proj is not None:
        g_next, mods_next, w_next = next_proj
        n_out = w_next.shape[1]
        extra_in = [pl.BlockSpec((1, D_MODEL), lambda i: (0, 0)),
                    pl.BlockSpec((1, N_MOD, D_MODEL), lambda i: (mi(i), 0, 0)),
                    pl.BlockSpec((D_MODEL, n_out), lambda i: (0, 0))]
        extra_args = [g_next.reshape(1, -1), mods_next, w_next]
        out_specs = [pl.BlockSpec((step, D_MODEL), row), pl.BlockSpec((step, n_out), row)]
        out_shape = [jax.ShapeDtypeStruct((lay.t, D_MODEL), F32),
                     jax.ShapeDtypeStruct((lay.t, n_out), F32)]
    else:
        extra_in, extra_args = [], []
        out_specs = pl.BlockSpec((step, D_MODEL), row)
        out_shape = jax.ShapeDtypeStruct((lay.t, D_MODEL), F32)
    return pl.pallas_call(
        functools.partial(_combine_kernel, n_tiles=n_tiles, final=final,
                          proj=next_proj is not None and not final),
        grid=(n_tiles // nslot,),
        in_specs=[pl.BlockSpec(memory_space=pl.ANY),
                  pl.BlockSpec(memory_space=pl.ANY),
                  pl.BlockSpec((step, D_MODEL), row),
                  pl.BlockSpec((step, D_MODEL), row),
                  pl.BlockSpec((step, TOP_K), row),
                  pl.BlockSpec((1, N_MOD, D_MODEL), lambda i: (mi(i), 0, 0))] + extra_in,
        out_specs=out_specs,
        out_shape=out_shape,
        scratch_shapes=[pltpu.SMEM((nslot * TOP_K * tm,), I32),
                        pltpu.VMEM((nslot, TOP_K * tm * ROW_CHUNKS, LANES), PACKED),
                        pltpu.SemaphoreType.DMA((nslot,)),
                        pltpu.SemaphoreType.DMA((nslot,))],
        compiler_params=_cparams(("arbitrary",)),
        name="moe_combine",
    )(dest, y_rows, x, shared, wts_t, mods, *extra_args)


def moe_layer(lay, x, g, mods, router_w, router_b, layer, w_gate, w_up, w_down, s_gate, s_up, s_down,
              g_final=None, next_proj=None, attn_out=None):
    t = lay.t
    outs = moe_router(lay, x, g, mods, router_w, router_b, s_gate.astype(BF16), s_up.astype(BF16),
                      s_down.astype(BF16), attn_out)
    hn, shared, eidx, wts, rank, cnt = outs[:6]
    if attn_out is not None:
        x = outs[6]
    counts = cnt[:, 0].astype(I32)
    n_blocks = (counts + MOE_BLOCK - 1) // MOE_BLOCK
    padded = n_blocks * MOE_BLOCK
    pad_end = jnp.cumsum(padded)
    pad_start = pad_end - padded
    n_rows = -(-(t * TOP_K + N_EXPERTS * (MOE_BLOCK - 1)) // MOE_BLOCK) * MOE_BLOCK
    dest = moe_dest(pad_start, eidx, rank)
    last_row = jnp.where(n_blocks > 0, pad_end - MOE_BLOCK, -1)
    tail_blk = pad_end[-1] // MOE_BLOCK + jnp.arange(n_rows // MOE_BLOCK - t * TOP_K // MOE_BLOCK)
    tail_row = jnp.where(tail_blk < n_rows // MOE_BLOCK, tail_blk * MOE_BLOCK, -1).astype(I32)
    xs = moe_dispatch(lay, hn, _tile_major(dest, DISPATCH_TILE),
                      jnp.concatenate([last_row, tail_row]), n_rows)
    y_rows = moe_experts(xs, pad_start // MOE_BLOCK, n_blocks, tail_row, layer, w_gate, w_up, w_down)
    return moe_combine(lay, x, shared, y_rows, _tile_major(dest, COMBINE_TILE), wts.T, mods,
                       g_final, next_proj)


def _block_diag(w):
    nb, bw, _ = w.shape
    eye = jnp.eye(nb, dtype=w.dtype)
    return (eye[:, None, :, None] * w[:, :, None, :]).reshape(nb * bw, nb * bw)


def even_layer(lay, x, mods, g_mix, p, state_lru, state_ssm_re, state_ssm_im):
    t = lay.t
    proj, u_g = modnorm_matmul(lay, x, g_mix, mods, 0, p['w_in'].astype(BF16),
                               ug_col=2 * D_LRU)
    zeros_c = jnp.zeros((lay.n_ctx, D_LRU), F32)
    hf_y, st = None, []
    for d in range(2):
        wg = jnp.concatenate([_block_diag(p['lru_wa'][d]), _block_diag(p['lru_wx'][d])], axis=1)
        bg = jnp.concatenate([p['lru_ba'][d], p['lru_bx'][d]])
        h0 = jnp.concatenate([zeros_c, state_lru[:, d].astype(F32)], axis=0)
        hf_y, s = lru_pass(lay, proj, p['conv_w'], p['conv_b'], wg.astype(BF16), bg,
                           p['lru_lam'][d], h0, reverse=(d == 1), hf=hf_y)
        st.append(s[:lay.n_ctx])
    y_a = hf_y
    new_lru = jnp.stack(st, axis=1)

    mats = _s5_matrices(p['a_re'], p['a_im'], p['log_dt'], p['b_re'], p['b_im'], p['c_re'], p['c_im'])
    h0 = jnp.concatenate([state_ssm_re, state_ssm_im], axis=-1).astype(F32)
    h0 = h0.transpose(2, 1, 0, 3)
    y_g, h_ctx = s5_mixer(lay, u_g, mats, h0)
    seg = lay.s_ctx // S5_CHUNK
    h_ctx = h_ctx.reshape(SSM_GROUPS, 2, lay.n_ctx, seg, 2 * SSM_STATE)
    ends = jnp.stack([h_ctx[:, 0, :, seg - 1], h_ctx[:, 1, :, 0]], axis=1)
    ends = ends.transpose(2, 1, 0, 3)
    x = even_out(lay, x, y_a, y_g, proj, p['d'], p['glu_w'].astype(BF16), p['glu_b'],
                 p['w_out'].astype(BF16), mods)
    return x, new_lru, ends[..., :SSM_STATE], ends[..., SSM_STATE:]


def odd_layer(lay, x, mods, g_mix, w_qkv, sink, w_out, cache_k, cache_v, qkv=None):
    if qkv is None:
        qkv = modnorm_matmul(lay, x, g_mix, mods, 0, w_qkv.astype(BF16))
    o_ctx = attn_context(lay, qkv, sink)
    o_lat = attn_latent(lay, qkv, cache_k, cache_v, sink)
    nq = N_HEADS * HEAD_DIM
    kv = qkv[:lay.t_ctx, nq:].reshape(lay.n_ctx, lay.s_ctx, 2, N_KV, HEAD_DIM)
    k_new = kv[:, :, 0].swapaxes(1, 2)
    v_new = kv[:, :, 1].swapaxes(1, 2)
    return (o_ctx, o_lat, w_out.astype(BF16)), k_new, v_new


def _forward(lay, x_prompt, x_sample, state_lru, state_ssm_re, state_ssm_im, cache_k, cache_v,
             c, c_ctx, g_mix, g_ffn, w_mod, b_mod,
             ev_w_in, lru_conv_w, lru_conv_b, lru_wa, lru_ba, lru_wx, lru_bx, lru_lam,
             ssm_a_re, ssm_a_im, ssm_log_dt, ssm_b_re, ssm_b_im, ssm_c_re, ssm_c_im, ssm_d,
             ssm_glu_w, ssm_glu_b, ev_w_out, at_w_qkv, at_sink, at_w_out,
             router_w, router_b, exp_w_gate, exp_w_up, exp_w_down, sh_w_gate, sh_w_up, sh_w_down,
             g_final):
    depth = g_mix.shape[0]
    x = (x_prompt.reshape(lay.t_ctx, D_MODEL), x_sample.reshape(lay.t_lat, D_MODEL))
    n_c = 1 + lay.n_lat
    c_rows = jnp.concatenate([c_ctx[None, :], c, jnp.zeros((16 - n_c, D_MODEL), F32)], axis=0)
    new_lru, new_re, new_im, new_k, new_v = [], [], [], [], []
    all_mods = [adaln_table(c_rows, l, w_mod, b_mod[l]) for l in range(depth)]
    qkv = None
    for l in range(depth):
        i = l // 2
        mods = all_mods[l]
        if l % 2 == 0:
            p = dict(w_in=ev_w_in[i], conv_w=lru_conv_w[i], conv_b=lru_conv_b[i],
                     lru_wa=lru_wa[i], lru_ba=lru_ba[i], lru_wx=lru_wx[i], lru_bx=lru_bx[i],
                     lru_lam=lru_lam[i], a_re=ssm_a_re[i], a_im=ssm_a_im[i], log_dt=ssm_log_dt[i],
                     b_re=ssm_b_re[i], b_im=ssm_b_im[i], c_re=ssm_c_re[i], c_im=ssm_c_im[i],
                     d=ssm_d[i], glu_w=ssm_glu_w[i], glu_b=ssm_glu_b[i], w_out=ev_w_out[i])
            x, lru_i, re_i, im_i = even_layer(lay, x, mods, g_mix[l], p, state_lru[:, i],
                                              state_ssm_re[:, i], state_ssm_im[:, i])
            new_lru.append(lru_i)
            new_re.append(re_i)
            new_im.append(im_i)
        else:
            if isinstance(x, tuple):
                x = jnp.concatenate(x, axis=0)
            attn_out, k_i, v_i = odd_layer(lay, x, mods, g_mix[l], at_w_qkv[i], at_sink[i], at_w_out[i],
                                           cache_k[:, i], cache_v[:, i], qkv)
            new_k.append(k_i)
            new_v.append(v_i)
        next_proj = None
        if l + 1 < depth and (l + 1) % 2 == 1:
            next_proj = (g_mix[l + 1], all_mods[l + 1], at_w_qkv[(l + 1) // 2].astype(BF16))
        x = moe_layer(lay, x, g_ffn[l], mods, router_w[l], router_b[l], l, exp_w_gate, exp_w_up,
                      exp_w_down, sh_w_gate[l], sh_w_up[l], sh_w_down[l],
                      g_final=g_final if l == depth - 1 else None, next_proj=next_proj,
                      attn_out=attn_out if l % 2 == 1 else None)
        qkv = None
        if next_proj is not None:
            x, qkv = x
    y_ctx, y_lat = x
    y_prompt = y_ctx[:lay.t_ctx].reshape(x_prompt.shape)
    y_sample = y_lat.reshape(x_sample.shape)
    return (y_prompt, y_sample, jnp.stack(new_lru, axis=1), jnp.stack(new_re, axis=1),
            jnp.stack(new_im, axis=1), jnp.stack(new_k, axis=1), jnp.stack(new_v, axis=1))


def kernel(x_prompt, x_sample, state_lru, state_ssm_re, state_ssm_im, cache_k, cache_v, c, c_ctx, g_mix, g_ffn, w_mod, b_mod, ev_w_in, lru_conv_w, lru_conv_b, lru_wa, lru_ba, lru_wx, lru_bx, lru_lam, ssm_a_re, ssm_a_im, ssm_log_dt, ssm_b_re, ssm_b_im, ssm_c_re, ssm_c_im, ssm_d, ssm_glu_w, ssm_glu_b, ev_w_out, at_w_qkv, at_sink, at_w_out, router_w, router_b, exp_w_gate, exp_w_up, exp_w_down, sh_w_gate, sh_w_up, sh_w_down, g_final):
    lay = Layout(n_ctx=x_prompt.shape[0], s_ctx=x_prompt.shape[1],
                 n_lat=x_sample.shape[0], s_lat=x_sample.shape[1])
    return _forward(lay, x_prompt, x_sample, state_lru, state_ssm_re, state_ssm_im, cache_k, cache_v,
                    c, c_ctx, g_mix, g_ffn, w_mod, b_mod,
                    ev_w_in, lru_conv_w, lru_conv_b, lru_wa, lru_ba, lru_wx, lru_bx, lru_lam,
                    ssm_a_re, ssm_a_im, ssm_log_dt, ssm_b_re, ssm_b_im, ssm_c_re, ssm_c_im, ssm_d,
                    ssm_glu_w, ssm_glu_b, ev_w_out, at_w_qkv, at_sink, at_w_out,
                    router_w, router_b, exp_w_gate, exp_w_up, exp_w_down, sh_w_gate, sh_w_up,
                    sh_w_down, g_final)
```
